```python
import math
import jax
import jax.numpy as jnp
from jax import lax
import numpy as np

D_MODEL = 1024
BATCH = 8
SEQ = 2048
DEPTH = 4
DEC_BATCH = 128
DEC_SEQ = 1
PAST_LEN = 16384
PAGE_SIZE = 128

F32 = jnp.float32
EPS = 1e-6
N_MIXERS = 3
LAYER_KIND = tuple(i % N_MIXERS for i in range(DEPTH))
LAYER_SLOT = tuple(sum(1 for j in range(i) if j % N_MIXERS == i % N_MIXERS) for i in range(DEPTH))
N_A = sum(1 for k in LAYER_KIND if k == 0)
N_B = sum(1 for k in LAYER_KIND if k == 1)
N_C = sum(1 for k in LAYER_KIND if k == 2)
A_D_INNER = 2 * D_MODEL
A_HEAD_DIM = 64
A_N_HEADS = A_D_INNER // A_HEAD_DIM
A_N_GROUPS = 8
A_HEADS_PER_GROUP = A_N_HEADS // A_N_GROUPS
A_D_STATE = 128
A_GN = A_N_GROUPS * A_D_STATE
A_CONV = 4
A_CONV_CH = A_D_INNER + 2 * A_GN
A_IN = A_D_INNER + A_CONV_CH + A_N_HEADS
A_CHUNK = 128
A_DT_MIN = 0.001
A_DT_MAX = 0.1
B_D = 2 * D_MODEL
B_N_GROUPS = 8
B_GROUP_DIM = B_D // B_N_GROUPS
B_CHUNK = 128
C_WINDOWS = (2, 4, 8, 16)
C_N_GROUPS = len(C_WINDOWS)
C_GROUP_DIM = D_MODEL // C_N_GROUPS
C_STATE = max(C_WINDOWS) - 1
N_EXPERTS = 16
N_EXPERT_GROUPS = 4
EXPERTS_PER_GROUP = N_EXPERTS // N_EXPERT_GROUPS
TOP_K = 2
D_EXPERT = D_MODEL // 2

kernel_name = 'hybrid_ssd_gmlp_pool_moe_adaln_step'


def _rms(x):
    xf = x.astype(F32)
    return xf * lax.rsqrt(jnp.mean(xf * xf, axis=-1, keepdims=True) + EPS)


def rmsnorm(x, g):
    return (_rms(x) * g).astype(x.dtype)


def layernorm(x, g, b):
    xf = x.astype(F32)
    xc = xf - jnp.mean(xf, axis=-1, keepdims=True)
    var = jnp.mean(xc * xc, axis=-1, keepdims=True)
    return (xc * lax.rsqrt(var + EPS) * g + b).astype(x.dtype)


def causal_dwconv(x, prev, w, b):
    L = x.shape[1]
    xp = jnp.concatenate([prev.astype(x.dtype), x], axis=1)
    y = b + sum(xp[:, k:k + L] * w[k] for k in range(A_CONV))
    return y, xp[:, xp.shape[1] - (A_CONV - 1):]


def _pad_seq(t, pad):
    return jnp.pad(t, [(0, 0), (0, pad)] + [(0, 0)] * (t.ndim - 2))


def ssd_chunked(x, dt, A, Bm, Cm, h0):
    b, L = x.shape[0], x.shape[1]
    q = min(A_CHUNK, L)
    pad = (-L) % q
    if pad:
        x, dt, Bm, Cm = _pad_seq(x, pad), _pad_seq(dt, pad), _pad_seq(Bm, pad), _pad_seq(Cm, pad)
    nc = (L + pad) // q
    G, R, P, N = A_N_GROUPS, A_HEADS_PER_GROUP, A_HEAD_DIM, A_D_STATE
    xc = x.reshape(b, nc, q, G, R, P)
    dtc = dt.reshape(b, nc, q, G, R)
    Bc = Bm.reshape(b, nc, q, G, N)
    Cc = Cm.reshape(b, nc, q, G, N)
    cs = jnp.cumsum(dtc * A.reshape(G, R), axis=2)
    xdt = xc * dtc[..., None]
    seg = cs[:, :, :, None] - cs[:, :, None, :]
    causal = jnp.tril(jnp.ones((q, q), dtype=bool))[None, None, :, :, None, None]
    decay = jnp.exp(jnp.where(causal, seg, -jnp.inf))
    cb = jnp.einsum('bclgn,bcsgn->bclsg', Cc, Bc)
    y_diag = jnp.einsum('bclsg,bclsgr,bcsgrp->bclgrp', cb, decay, xdt)
    decay_to_end = jnp.exp(cs[:, :, -1:] - cs)
    chunk_states = jnp.einsum('bcsgn,bcsgr,bcsgrp->bcgrpn', Bc, decay_to_end, xdt)
    chunk_decay = jnp.exp(cs[:, :, -1])

    def step(h, inp):
        s_c, d_c = inp
        return h * d_c[..., None, None] + s_c, h

    h_last, h_start = lax.scan(step, h0.reshape(b, G, R, P, N),
                               (jnp.moveaxis(chunk_states, 1, 0), jnp.moveaxis(chunk_decay, 1, 0)))
    h_start = jnp.moveaxis(h_start, 0, 1)
    y_off = jnp.einsum('bclgn,bcgrpn,bclgr->bclgrp', Cc, h_start, jnp.exp(cs))
    y = (y_diag + y_off).reshape(b, nc * q, G * R, P)[:, :L]
    return y, h_last.reshape(b, G * R, P, N)


def mamba2_mixer(h, conv_prev, ssm_prev, w_in, conv_w, conv_b, dt_bias, a_log, d_skip, norm_g, w_out):
    b, L = h.shape[0], h.shape[1]
    proj = h @ w_in
    z = proj[..., :A_D_INNER]
    xbc = proj[..., A_D_INNER:A_D_INNER + A_CONV_CH]
    dt_raw = proj[..., A_D_INNER + A_CONV_CH:]
    xbc, conv_new = causal_dwconv(xbc, conv_prev, conv_w, conv_b)
    xbc = jax.nn.silu(xbc)
    xs = xbc[..., :A_D_INNER].reshape(b, L, A_N_HEADS, A_HEAD_DIM)
    Bm = xbc[..., A_D_INNER:A_D_INNER + A_GN].reshape(b, L, A_N_GROUPS, A_D_STATE)
    Cm = xbc[..., A_D_INNER + A_GN:].reshape(b, L, A_N_GROUPS, A_D_STATE)
    dt = jax.nn.softplus(dt_raw.astype(F32) + dt_bias.astype(F32))
    A = -jnp.exp(a_log.astype(F32))
    y, ssm_new = ssd_chunked(xs.astype(F32), dt, A, Bm.astype(F32), Cm.astype(F32), ssm_prev.astype(F32))
    y = y + d_skip.astype(F32)[:, None] * xs.astype(F32)
    gated = y.reshape(b, L, A_D_INNER) * jax.nn.silu(z.astype(F32))
    gn = _rms(gated.reshape(b, L, A_N_GROUPS, A_D_INNER // A_N_GROUPS)).reshape(b, L, A_D_INNER)
    out = (gn * norm_g).astype(h.dtype) @ w_out
    return out, conv_new, ssm_new


def chunk_mlp_mixer(h, w_in, b_in, ln_g, ln_b, w_s, b_s, w_out):
    b, L = h.shape[0], h.shape[1]
    uv = jax.nn.gelu(h @ w_in + b_in)
    u, v = uv[..., :B_D], uv[..., B_D:]
    v = layernorm(v, ln_g, ln_b)
    pad = (-L) % B_CHUNK
    nc = (L + pad) // B_CHUNK
    vc = _pad_seq(v, pad).reshape(b, nc, B_CHUNK, B_N_GROUPS, B_GROUP_DIM)
    mask = jnp.tril(jnp.ones((B_CHUNK, B_CHUNK), dtype=bool))
    ws = jnp.where(mask, w_s, 0.0)
    mixed = jnp.einsum('gij,bcjgd->bcigd', ws, vc) + b_s.T[None, None, :, :, None]
    mixed = mixed.reshape(b, nc * B_CHUNK, B_D)[:, :L]
    return (u * mixed).astype(h.dtype) @ w_out, v


def pool_mixer(h, prev, start, w_g, scale):
    b, L = h.shape[0], h.shape[1]
    hp = jnp.concatenate([prev.astype(h.dtype), h], axis=1)
    csum = jnp.pad(jnp.cumsum(hp.astype(F32), axis=1), ((0, 0), (1, 0), (0, 0)))
    pos = start + jnp.arange(L)
    outs = []
    for gi, w in enumerate(C_WINDOWS):
        lo_c, hi_c = gi * C_GROUP_DIM, (gi + 1) * C_GROUP_DIM
        hi = csum[:, C_STATE + 1:C_STATE + 1 + L, lo_c:hi_c]
        lo = csum[:, C_STATE + 1 - w:C_STATE + 1 - w + L, lo_c:hi_c]
        cnt = jnp.minimum(pos + 1, w).astype(F32)[None, :, None]
        outs.append((hi - lo) / cnt)
    pooled = jnp.concatenate(outs, axis=-1).astype(h.dtype) - h
    pg = pooled.reshape(b, L, C_N_GROUPS, C_GROUP_DIM)
    y = jnp.einsum('blgi,gio->blgo', pg, w_g).reshape(b, L, D_MODEL) * scale
    return y, hp[:, hp.shape[1] - C_STATE:]


def moe_ffn(h, router_w, router_b, w_gate, w_up, w_down):
    b, L, D = h.shape
    t = h.reshape(b * L, D)
    scores = jax.nn.sigmoid((t @ router_w).astype(F32))
    biased = (scores + router_b).reshape(-1, N_EXPERT_GROUPS, EXPERTS_PER_GROUP)
    group_score = jnp.sum(lax.top_k(biased, TOP_K)[0], axis=-1)
    g_sel = jnp.argmax(group_score, axis=-1)
    in_group = jnp.take_along_axis(biased, g_sel[:, None, None], axis=1)[:, 0]
    _, local_idx = lax.top_k(in_group, TOP_K)
    expert_idx = g_sel[:, None] * EXPERTS_PER_GROUP + local_idx
    sel = jnp.take_along_axis(scores, expert_idx, axis=1)
    weights = sel / jnp.sum(sel, axis=-1, keepdims=True)
    gates = jnp.sum(jax.nn.one_hot(expert_idx, N_EXPERTS, dtype=F32) * weights[..., None], axis=1)
    hg = jnp.einsum('td,edf->tef', t, w_gate)
    hu = jnp.einsum('td,edf->tef', t, w_up)
    act = jax.nn.silu(hg) * hu * gates[..., None].astype(t.dtype)
    return jnp.einsum('tef,efd->td', act, w_down).reshape(b, L, D)


def trunk(x, c, start, conv_prev, ssm_prev, pool_prev, w_mod, b_mod, norm_g, final_g,
          a_w_in, a_conv_w, a_conv_b, a_dt_bias, a_log, a_d, a_norm_g, a_w_out,
          b_w_in, b_b_in, b_ln_g, b_ln_b, b_w_s, b_b_s, b_w_out, c_w_g, c_scale,
          router_w, router_b, e_w_gate, e_w_up, e_w_down):
    conv_new, ssm_new, v_new, pool_new = [], [], [], []
    for i in range(DEPTH):
        mod = jax.nn.silu(c) @ w_mod[i] + b_mod[i]
        sh1, sc1, g1, sh2, sc2, g2 = jnp.split(mod, 6, axis=-1)
        hn = rmsnorm(x, norm_g[i, 0]) * (1 + sc1[:, None]) + sh1[:, None]
        kind, s = LAYER_KIND[i], LAYER_SLOT[i]
        if kind == 0:
            y, cv, ss = mamba2_mixer(hn, conv_prev[s], ssm_prev[s], a_w_in[s], a_conv_w[s], a_conv_b[s],
                                     a_dt_bias[s], a_log[s], a_d[s], a_norm_g[s], a_w_out[s])
            conv_new.append(cv)
            ssm_new.append(ss)
        elif kind == 1:
            y, v = chunk_mlp_mixer(hn, b_w_in[s], b_b_in[s], b_ln_g[s], b_ln_b[s], b_w_s[s], b_b_s[s], b_w_out[s])
            v_new.append(v)
        else:
            y, pr = pool_mixer(hn, pool_prev[s], start, c_w_g[s], c_scale[s])
            pool_new.append(pr)
        x = x + (g1[:, None] * y).astype(x.dtype)
        hn = rmsnorm(x, norm_g[i, 1]) * (1 + sc2[:, None]) + sh2[:, None]
        x = x + (g2[:, None] * moe_ffn(hn, router_w, router_b, e_w_gate[i], e_w_up[i], e_w_down[i])).astype(x.dtype)
    return rmsnorm(x, final_g), jnp.stack(conv_new), jnp.stack(ssm_new), jnp.stack(v_new), jnp.stack(pool_new)


def setup_inputs(seed: int = 0) -> dict:
    key = jax.random.key(seed)
    ks = iter(jax.random.split(key, 40))

    def nrm(shape, scale):
        return jax.random.normal(next(ks), shape, F32) * scale

    dt0 = jnp.exp(jax.random.uniform(next(ks), (N_A, A_N_HEADS), F32)
                  * (math.log(A_DT_MAX) - math.log(A_DT_MIN)) + math.log(A_DT_MIN))
    a_dt_bias = dt0 + jnp.log(-jnp.expm1(-dt0))
    a_log = jnp.log(jax.random.uniform(next(ks), (N_A, A_N_HEADS), F32, 1.0, 16.0))
    return {
        'x_prompt': nrm((BATCH, SEQ, D_MODEL), 1.0),
        'x_sample': nrm((DEC_BATCH, DEC_SEQ, D_MODEL), 1.0),
        'c_prompt': nrm((BATCH, D_MODEL), 1.0),
        'c_sample': nrm((DEC_BATCH, D_MODEL), 1.0),
        'state_a_conv': nrm((N_A, DEC_BATCH, A_CONV - 1, A_CONV_CH), 1.0),
        'state_a_ssm': nrm((N_A, DEC_BATCH, A_N_HEADS, A_HEAD_DIM, A_D_STATE), 0.1),
        'state_c_pool': nrm((N_C, DEC_BATCH, C_STATE, D_MODEL), 1.0),
        'w_mod': nrm((DEPTH, D_MODEL, 6 * D_MODEL), 0.3 * D_MODEL ** -0.5),
        'b_mod': nrm((DEPTH, 6 * D_MODEL), 0.02),
        'norm_g': 1.0 + nrm((DEPTH, 2, D_MODEL), 0.02),
        'final_g': 1.0 + nrm((D_MODEL,), 0.02),
        'a_w_in': nrm((N_A, D_MODEL, A_IN), D_MODEL ** -0.5),
        'a_conv_w': nrm((N_A, A_CONV, A_CONV_CH), A_CONV ** -0.5),
        'a_conv_b': nrm((N_A, A_CONV_CH), 0.01),
        'a_dt_bias': a_dt_bias,
        'a_log': a_log,
        'a_d': 1.0 + nrm((N_A, A_N_HEADS), 0.1),
        'a_norm_g': 1.0 + nrm((N_A, A_D_INNER), 0.02),
        'a_w_out': nrm((N_A, A_D_INNER, D_MODEL), A_D_INNER ** -0.5),
        'b_w_in': nrm((N_B, D_MODEL, 2 * B_D), D_MODEL ** -0.5),
        'b_b_in': nrm((N_B, 2 * B_D), 0.01),
        'b_ln_g': 1.0 + nrm((N_B, B_D), 0.02),
        'b_ln_b': nrm((N_B, B_D), 0.01),
        'b_w_s': nrm((N_B, B_N_GROUPS, B_CHUNK, B_CHUNK), B_CHUNK ** -0.5),
        'b_b_s': 1.0 + nrm((N_B, B_N_GROUPS, B_CHUNK), 0.1),
        'b_w_out': nrm((N_B, B_D, D_MODEL), B_D ** -0.5),
        'c_w_g': nrm((N_C, C_N_GROUPS, C_GROUP_DIM, C_GROUP_DIM), C_GROUP_DIM ** -0.5),
        'c_scale': 1.0 + nrm((N_C, D_MODEL), 0.1),
        'router_w': nrm((D_MODEL, N_EXPERTS), D_MODEL ** -0.5),
        'router_b': nrm((N_EXPERTS,), 0.01),
        'e_w_gate': nrm((DEPTH, N_EXPERTS, D_MODEL, D_EXPERT), D_MODEL ** -0.5),
        'e_w_up': nrm((DEPTH, N_EXPERTS, D_MODEL, D_EXPERT), D_MODEL ** -0.5),
        'e_w_down': nrm((DEPTH, N_EXPERTS, D_EXPERT, D_MODEL), D_EXPERT ** -0.5),
    }


def reference(x_prompt, x_sample, c_prompt, c_sample, state_a_conv, state_a_ssm, state_c_pool,
              w_mod, b_mod, norm_g, final_g,
              a_w_in, a_conv_w, a_conv_b, a_dt_bias, a_log, a_d, a_norm_g, a_w_out,
              b_w_in, b_b_in, b_ln_g, b_ln_b, b_w_s, b_b_s, b_w_out, c_w_g, c_scale,
              router_w, router_b, e_w_gate, e_w_up, e_w_down):
    weights = (w_mod, b_mod, norm_g, final_g,
               a_w_in, a_conv_w, a_conv_b, a_dt_bias, a_log, a_d, a_norm_g, a_w_out,
               b_w_in, b_b_in, b_ln_g, b_ln_b, b_w_s, b_b_s, b_w_out, c_w_g, c_scale,
               router_w, router_b, e_w_gate, e_w_up, e_w_down)
    bp = x_prompt.shape[0]
    conv0 = jnp.zeros((N_A, bp, A_CONV - 1, A_CONV_CH), x_prompt.dtype)
    ssm0 = jnp.zeros((N_A, bp, A_N_HEADS, A_HEAD_DIM, A_D_STATE), F32)
    pool0 = jnp.zeros((N_C, bp, C_STATE, D_MODEL), x_prompt.dtype)
    y_prompt, conv_p, ssm_p, _, pool_p = trunk(x_prompt, c_prompt, 0, conv0, ssm0, pool0, *weights)
    y_sample, conv_s, ssm_s, v_s, pool_s = trunk(x_sample, c_sample, PAST_LEN, state_a_conv, state_a_ssm,
                                                 state_c_pool, *weights)
    return (y_prompt, y_sample, conv_p, ssm_p, pool_p, conv_s, ssm_s, pool_s, v_s)
```

```python
import functools
import math

import numpy as np
import jax
import jax.numpy as jnp
from jax import lax
from jax.experimental import pallas as pl
from jax.experimental.pallas import tpu as pltpu

F32 = jnp.float32
BF16 = jnp.bfloat16
I32 = jnp.int32
EPS = 1e-6

LANE = 128
D_MODEL = 1024
DEPTH = 4
PAST_LEN = 16384
LAYER_KIND = (0, 1, 2, 0)
LAYER_SLOT = (0, 0, 0, 1)
A_D_INNER = 2 * D_MODEL
A_HEAD_DIM = 64
A_N_HEADS = A_D_INNER // A_HEAD_DIM
A_N_GROUPS = 8
A_HPG = A_N_HEADS // A_N_GROUPS
A_D_STATE = 128
A_GN = A_N_GROUPS * A_D_STATE
A_CONV = 4
A_CONV_CH = A_D_INNER + 2 * A_GN
A_ZX = A_D_INNER + A_CONV_CH
A_CHUNK = 128
A_GW = A_HPG * A_HEAD_DIM
B_D = 2 * D_MODEL
B_N_GROUPS = 8
B_GROUP_DIM = B_D // B_N_GROUPS
B_CHUNK = 128
C_WINDOWS = (2, 4, 8, 16)
C_GROUP_DIM = D_MODEL // len(C_WINDOWS)
C_STATE = max(C_WINDOWS) - 1
N_EXPERTS = 16
N_EXPERT_GROUPS = 4
EXPERTS_PER_GROUP = 4
D_EXPERT = D_MODEL // 2
PAIRS = ((0, 1), (0, 2), (0, 3), (1, 2), (1, 3), (2, 3))
N_BUCKETS = N_EXPERT_GROUPS * len(PAIRS)
BUCKET_ROWS = 32
ROW_PLANES = D_MODEL // LANE
EXPERT_TILE = 256
VMEM_LIMIT = 56 * 1024 * 1024


def _cparams(sem, vmem=VMEM_LIMIT):
    return pltpu.CompilerParams(dimension_semantics=sem, vmem_limit_bytes=vmem)


def _sigmoid(x):
    return 1.0 / (1.0 + jnp.exp(-x))


def _silu(x):
    return x * _sigmoid(x)


def _softplus(x):
    return jnp.maximum(x, 0.0) + jnp.log1p(jnp.exp(-jnp.abs(x)))


def _split2(a):
    hi = a.astype(BF16)
    lo = (a - hi.astype(F32)).astype(BF16)
    return hi, lo


def _split3(a):
    hi = a.astype(BF16)
    r = a - hi.astype(F32)
    mid = r.astype(BF16)
    lo = (r - mid.astype(F32)).astype(BF16)
    return hi, mid, lo


def _dot(a, b):
    return jnp.dot(a, b, preferred_element_type=F32)


def _dot_nt(a, b):
    return lax.dot_general(a, b, (((1,), (1,)), ((), ())), preferred_element_type=F32)


def _dot_tn(a, b):
    return lax.dot_general(a, b, (((0,), (0,)), ((), ())), preferred_element_type=F32)


def _dot3(a, b):
    a_hi, a_lo = _split2(a)
    b_hi, b_lo = _split2(b)
    return _dot(a_hi, b_hi) + (_dot(a_lo, b_hi) + _dot(a_hi, b_lo))


def _prenorm(x, g, sc, sh):
    ms = jnp.mean(x * x, axis=-1, keepdims=True)
    return (x * lax.rsqrt(ms + EPS) * g) * (1.0 + sc) + sh


def _mod_body(c_ref, w_ref, b_ref, o_ref):
    o_ref[...] = _dot3(_silu(c_ref[...]), w_ref[...]) + b_ref[...]


def _mod_call(c_all, w_mod, b_mod):
    nb, d = c_all.shape
    depth, _, n = w_mod.shape
    tn = 1536
    return pl.pallas_call(
        _mod_body,
        grid=(depth, n // tn),
        in_specs=[pl.BlockSpec((nb, d), lambda i, j: (0, 0)),
                  pl.BlockSpec((None, d, tn), lambda i, j: (i, 0, j)),
                  pl.BlockSpec((None, 1, tn), lambda i, j: (i, 0, j))],
        out_specs=pl.BlockSpec((None, nb, tn), lambda i, j: (i, 0, j)),
        out_shape=jax.ShapeDtypeStruct((depth, nb, n), F32),
        compiler_params=_cparams(("arbitrary", "arbitrary")),
        name="mod",
    )(c_all, w_mod, b_mod.reshape(depth, 1, n))


class Mod:
    def __init__(self, arr, layer, per_row, rows_per_seq=None):
        self.arr, self.layer, self.per_row, self.rows_per_seq = arr, layer, per_row, rows_per_seq

    def spec(self, which, tm, ngrid):
        layer = self.layer
        if self.per_row:
            if ngrid == 1:
                return pl.BlockSpec((None, tm, D_MODEL), lambda i: (layer, i, which))
            return pl.BlockSpec((None, tm, D_MODEL), lambda i, j: (layer, i, which))
        tiles = self.rows_per_seq // tm
        if ngrid == 1:
            return pl.BlockSpec((None, None, None, 1, D_MODEL), lambda i: (layer, i // tiles, which, 0, 0))
        return pl.BlockSpec((None, None, None, 1, D_MODEL), lambda i, j: (layer, i // tiles, which, 0, 0))


def _norm_mm_body(x_ref, g_ref, sc_ref, sh_ref, w_ref, o_ref, hn_ref):
    @pl.when(pl.program_id(1) == 0)
    def _():
        hn_ref[...] = _prenorm(x_ref[...], g_ref[...], sc_ref[...], sh_ref[...]).astype(BF16)

    o_ref[...] = _dot(hn_ref[...], w_ref[...].astype(BF16)).astype(o_ref.dtype)


def _norm_mm_call(x, g, mod, which_sc, which_sh, w, n_cols, tm, tn, out_dtype, name):
    t, d = x.shape
    return pl.pallas_call(
        _norm_mm_body,
        grid=(t // tm, n_cols // tn),
        in_specs=[pl.BlockSpec((tm, d), lambda i, j: (i, 0)),
                  pl.BlockSpec((1, d), lambda i, j: (0, 0)),
                  mod.spec(which_sc, tm, 2), mod.spec(which_sh, tm, 2),
                  pl.BlockSpec((d, tn), lambda i, j: (0, j))],
        out_specs=pl.BlockSpec((tm, tn), lambda i, j: (i, j)),
        out_shape=jax.ShapeDtypeStruct((t, n_cols), out_dtype),
        scratch_shapes=[pltpu.VMEM((tm, d), BF16)],
        compiler_params=_cparams(("arbitrary", "arbitrary")),
        name=name,
    )(x, g.reshape(1, d), mod.arr, mod.arr, w)


def _norm_mm3_body(x_ref, g_ref, sc_ref, sh_ref, w_ref, o_ref):
    hn = _prenorm(x_ref[...], g_ref[...], sc_ref[...], sh_ref[...])
    o_ref[...] = _dot3(hn, w_ref[...])


def _norm_mm3_call(x, g, mod, which_sc, which_sh, w, tm, name):
    t, d = x.shape
    n = w.shape[1]
    return pl.pallas_call(
        _norm_mm3_body,
        grid=(t // tm,),
        in_specs=[pl.BlockSpec((tm, d), lambda i: (i, 0)),
                  pl.BlockSpec((1, d), lambda i: (0, 0)),
                  mod.spec(which_sc, tm, 1), mod.spec(which_sh, tm, 1),
                  pl.BlockSpec((d, n), lambda i: (0, 0))],
        out_specs=pl.BlockSpec((tm, n), lambda i: (i, 0)),
        out_shape=jax.ShapeDtypeStruct((t, n), F32),
        compiler_params=_cparams(("arbitrary",)),
        name=name,
    )(x, g.reshape(1, d), mod.arr, mod.arr, w)


def _out_res_body(y_ref, w_ref, x_ref, gate_ref, o_ref, wbf_ref):
    @pl.when(pl.program_id(0) == 0)
    def _():
        wbf_ref[...] = w_ref[...].astype(BF16)

    o_ref[...] = x_ref[...] + gate_ref[...] * _dot(y_ref[...], wbf_ref[...])


def _out_res_call(y, w, x, mod, which_gate, tm, name):
    t, k = y.shape
    d = x.shape[1]
    return pl.pallas_call(
        _out_res_body,
        grid=(t // tm,),
        in_specs=[pl.BlockSpec((tm, k), lambda i: (i, 0)),
                  pl.BlockSpec((k, d), lambda i: (0, 0)),
                  pl.BlockSpec((tm, d), lambda i: (i, 0)),
                  mod.spec(which_gate, tm, 1)],
        out_specs=pl.BlockSpec((tm, d), lambda i: (i, 0)),
        out_shape=jax.ShapeDtypeStruct((t, d), F32),
        scratch_shapes=[pltpu.VMEM((k, d), BF16)],
        compiler_params=_cparams(("arbitrary",)),
        name=name,
    )(y, w, x, mod.arr)


def _head_expand():
    h = np.arange(LANE)[:, None]
    c = np.arange(A_D_INNER)[None, :]
    return jnp.asarray((c // A_HEAD_DIM == h).astype(np.float32), dtype=BF16)


def _gate_norm(y, z, ng):
    gated = y * _silu(z)
    ms = jnp.mean(gated * gated, axis=-1, keepdims=True)
    return gated * lax.rsqrt(ms + EPS) * ng


def _ssd_body(z_ref, xs_ref, bc_ref, dt_ref, cprev_ref, sprev_ref, cw_ref, cb_ref, dtb_ref, alog_ref,
              dx_ref, ng_ref, exp_ref, yn_ref, cnew_ref, snew_ref, xpad_ref, act_ref):
    c = pl.program_id(1)
    q = A_CHUNK
    top = 8
    tail0 = top - (A_CONV - 1)

    @pl.when(c == 0)
    def _():
        xpad_ref[tail0:top, :] = cprev_ref[...]
        snew_ref[...] = sprev_ref[...]

    xpad_ref[top:top + q, 0:A_D_INNER] = xs_ref[...].astype(F32)
    xpad_ref[top:top + q, A_D_INNER:A_CONV_CH] = bc_ref[...].astype(F32)
    cw = 512
    for j in range(A_CONV_CH // cw):
        sl = slice(j * cw, (j + 1) * cw)
        conv = cb_ref[:, sl] + sum(xpad_ref[tail0 + k:tail0 + k + q, sl] * cw_ref[k:k + 1, sl]
                                   for k in range(A_CONV))
        act_ref[:, sl] = _silu(conv)
    tail = xpad_ref[top + q - (A_CONV - 1):top + q, :]
    xpad_ref[tail0:top, :] = tail

    @pl.when(c == pl.num_programs(1) - 1)
    def _():
        cnew_ref[...] = tail

    dt = _softplus(dt_ref[...] + dtb_ref[...])
    a = dt * (-jnp.exp(alog_ref[...]))
    row = lax.broadcasted_iota(I32, (q, q), 0)
    col = lax.broadcasted_iota(I32, (q, q), 1)
    causal = row >= col
    tril = jnp.where(causal, 1.0, 0.0).astype(BF16)
    a_hi, a_mid, a_lo = _split3(a)
    cs = _dot(tril, a_hi) + (_dot(tril, a_mid) + _dot(tril, a_lo))
    cs_t = cs.T
    dt_t = dt.T
    cs_last = cs[q - 1:q, :]
    ecs = jnp.exp(cs)
    wend = jnp.exp(cs_last - cs) * dt
    st_hi, st_lo = _split2(jnp.concatenate([ecs, wend], axis=0))
    st_x = _dot(st_hi, exp_ref[...]) + _dot(st_lo, exp_ref[...])
    lane_head = lax.broadcasted_iota(I32, (q, A_GW), 1) // A_HEAD_DIM

    for g in range(A_N_GROUPS):
        gsl = slice(g * A_GW, (g + 1) * A_GW)
        b_g = act_ref[:, A_D_INNER + g * A_D_STATE:A_D_INNER + (g + 1) * A_D_STATE].astype(BF16)
        c_g = act_ref[:, A_D_INNER + A_GN + g * A_D_STATE:A_D_INNER + A_GN + (g + 1) * A_D_STATE].astype(BF16)
        x_g = act_ref[:, gsl]
        x_bf = x_g.astype(BF16)
        h_g = snew_ref[gsl, :]
        cb = _dot_nt(c_g, b_g)
        y = jnp.zeros((q, A_GW), F32)
        for r in range(A_HPG):
            h = g * A_HPG + r
            seg = cs[:, h:h + 1] - cs_t[h:h + 1, :]
            decay = jnp.exp(jnp.where(causal, seg, -1e30))
            wm = (cb * decay * dt_t[h:h + 1, :]).astype(BF16)
            y = y + _dot(wm, jnp.where(lane_head == r, x_bf, jnp.zeros_like(x_bf)))
        y = y + st_x[0:q, gsl] * _dot_nt(c_g, h_g.astype(BF16)) + dx_ref[:, gsl] * x_g
        yn_ref[:, gsl] = _gate_norm(y, z_ref[:, gsl].astype(F32), ng_ref[:, gsl]).astype(BF16)
        s_new = _dot_tn((x_g * st_x[q:2 * q, gsl]).astype(BF16), b_g)
        for r in range(A_HPG):
            h = g * A_HPG + r
            rsl = slice(g * A_GW + r * A_HEAD_DIM, g * A_GW + (r + 1) * A_HEAD_DIM)
            keep = jnp.exp(cs[q - 1:q, h:h + 1])
            snew_ref[rsl, :] = snew_ref[rsl, :] * keep + s_new[r * A_HEAD_DIM:(r + 1) * A_HEAD_DIM, :]


def _ssd_call(proj, dt_raw, conv_prev, ssm_prev, conv_w, conv_b, dt_bias, a_log, d_x, norm_g, nb, name):
    t = proj.shape[0]
    q = A_CHUNK
    nc = t // nb // q
    hp = A_N_HEADS * A_HEAD_DIM
    row = lambda b, c: (b * nc + c, 0)
    full = lambda shape: pl.BlockSpec(shape, lambda b, c: (0,) * len(shape))
    return pl.pallas_call(
        _ssd_body,
        grid=(nb, nc),
        in_specs=[pl.BlockSpec((q, A_D_INNER), lambda b, c: (b * nc + c, 0)),
                  pl.BlockSpec((q, A_D_INNER), lambda b, c: (b * nc + c, 1)),
                  pl.BlockSpec((q, 2 * A_GN), lambda b, c: (b * nc + c, 2)),
                  pl.BlockSpec((q, LANE), row),
                  pl.BlockSpec((None, A_CONV - 1, A_CONV_CH), lambda b, c: (b, 0, 0)),
                  pl.BlockSpec((None, hp, A_D_STATE), lambda b, c: (b, 0, 0)),
                  full((A_CONV, A_CONV_CH)), full((1, A_CONV_CH)), full((1, LANE)), full((1, LANE)),
                  full((1, A_D_INNER)), full((1, A_D_INNER)), full((LANE, A_D_INNER))],
        out_specs=[pl.BlockSpec((q, A_D_INNER), row),
                   pl.BlockSpec((None, A_CONV - 1, A_CONV_CH), lambda b, c: (b, 0, 0)),
                   pl.BlockSpec((None, hp, A_D_STATE), lambda b, c: (b, 0, 0))],
        out_shape=[jax.ShapeDtypeStruct((t, A_D_INNER), BF16),
                   jax.ShapeDtypeStruct((nb, A_CONV - 1, A_CONV_CH), F32),
                   jax.ShapeDtypeStruct((nb, hp, A_D_STATE), F32)],
        scratch_shapes=[pltpu.VMEM((8 + q, A_CONV_CH), F32), pltpu.VMEM((q, A_CONV_CH), F32)],
        compiler_params=_cparams(("arbitrary", "arbitrary")),
        name=name,
    )(proj, proj, proj, dt_raw, conv_prev, ssm_prev, conv_w, conv_b, dt_bias, a_log, d_x, norm_g, _head_expand())


def _ssd_step_body(z_ref, xs_ref, bc_ref, dt_ref, cprev_ref, sprev_ref, cw_ref, cb_ref, dtb_ref, alog_ref,
                   dx_ref, ng_ref, exp_ref, yn_ref, cnew_ref, snew_ref, da_ref, y_ref):
    bt = z_ref.shape[0]
    cur = jnp.concatenate([xs_ref[...], bc_ref[...]], axis=1).astype(F32)
    conv = cb_ref[...] + cur * cw_ref[A_CONV - 1:A_CONV, :]
    for k in range(A_CONV - 1):
        conv = conv + cprev_ref[:, k, :] * cw_ref[k:k + 1, :]
    for k in range(A_CONV - 2):
        cnew_ref[:, k, :] = cprev_ref[:, k + 1, :]
    cnew_ref[:, A_CONV - 2, :] = cur
    act = _silu(conv)
    xs = act[:, 0:A_D_INNER]
    bm = act[:, A_D_INNER:A_D_INNER + A_GN]
    cm = act[:, A_D_INNER + A_GN:A_CONV_CH]
    dt = _softplus(dt_ref[...] + dtb_ref[...])
    da_ref[...] = jnp.exp(dt * (-jnp.exp(alog_ref[...])))
    dt_hi, dt_lo = _split2(dt)
    dt_x = _dot(dt_hi, exp_ref[...]) + _dot(dt_lo, exp_ref[...])
    xdt = xs * dt_x
    rows = lax.broadcasted_iota(I32, (bt, 1), 0)
    y_ref[...] = jnp.zeros_like(y_ref)

    def per_seq(j, carry):
        mine = rows == j
        xdt_j = jnp.where(mine, xdt, 0.0)
        da_j = da_ref[pl.ds(j, 1), :]
        for g in range(A_N_GROUPS):
            gsl = slice(g * A_GW, (g + 1) * A_GW)
            nsl = slice(g * A_D_STATE, (g + 1) * A_D_STATE)
            xh, xl = _split2(xdt_j[:, gsl])
            bh, bl = _split2(bm[:, nsl])
            outer = _dot_tn(xh, bh) + (_dot_tn(xl, bh) + _dot_tn(xh, bl))
            for r in range(A_HPG):
                h = g * A_HPG + r
                rsl = slice(r * A_HEAD_DIM, (r + 1) * A_HEAD_DIM)
                hsl = slice(g * A_GW + r * A_HEAD_DIM, g * A_GW + (r + 1) * A_HEAD_DIM)
                snew_ref[j, hsl, :] = sprev_ref[j, hsl, :] * da_j[:, h:h + 1] + outer[rsl, :]
            h_new = snew_ref[j, gsl, :]
            hh, hl = _split2(h_new)
            ch, cl = _split2(cm[:, nsl])
            yg = _dot_nt(ch, hh) + (_dot_nt(cl, hh) + _dot_nt(ch, hl))
            y_ref[:, gsl] = y_ref[:, gsl] + jnp.where(mine, yg, 0.0)
        return carry

    lax.fori_loop(0, bt, per_seq, 0)
    y = y_ref[...] + dx_ref[...] * xs
    z = z_ref[...].astype(F32)
    for g in range(A_N_GROUPS):
        gsl = slice(g * A_GW, (g + 1) * A_GW)
        yn_ref[:, gsl] = _gate_norm(y[:, gsl], z[:, gsl], ng_ref[:, gsl]).astype(BF16)


def _ssd_step_call(proj, dt_raw, conv_prev, ssm_prev, conv_w, conv_b, dt_bias, a_log, d_x, norm_g, name):
    nb = proj.shape[0]
    bt = 8
    hp = A_N_HEADS * A_HEAD_DIM
    full = lambda shape: pl.BlockSpec(shape, lambda i: (0,) * len(shape))
    return pl.pallas_call(
        _ssd_step_body,
        grid=(nb // bt,),
        in_specs=[pl.BlockSpec((bt, A_D_INNER), lambda i: (i, 0)),
                  pl.BlockSpec((bt, A_D_INNER), lambda i: (i, 1)),
                  pl.BlockSpec((bt, 2 * A_GN), lambda i: (i, 2)),
                  pl.BlockSpec((bt, LANE), lambda i: (i, 0)),
                  pl.BlockSpec((bt, A_CONV - 1, A_CONV_CH), lambda i: (i, 0, 0)),
                  pl.BlockSpec((bt, hp, A_D_STATE), lambda i: (i, 0, 0)),
                  full((A_CONV, A_CONV_CH)), full((1, A_CONV_CH)), full((1, LANE)), full((1, LANE)),
                  full((1, A_D_INNER)), full((1, A_D_INNER)), full((LANE, A_D_INNER))],
        out_specs=[pl.BlockSpec((bt, A_D_INNER), lambda i: (i, 0)),
                   pl.BlockSpec((bt, A_CONV - 1, A_CONV_CH), lambda i: (i, 0, 0)),
                   pl.BlockSpec((bt, hp, A_D_STATE), lambda i: (i, 0, 0))],
        out_shape=[jax.ShapeDtypeStruct((nb, A_D_INNER), BF16),
                   jax.ShapeDtypeStruct((nb, A_CONV - 1, A_CONV_CH), F32),
                   jax.ShapeDtypeStruct((nb, hp, A_D_STATE), F32)],
        scratch_shapes=[pltpu.VMEM((bt, LANE), F32), pltpu.VMEM((bt, A_D_INNER), F32)],
        compiler_params=_cparams(("arbitrary",)),
        name=name,
    )(proj, proj, proj, dt_raw, conv_prev, ssm_prev, conv_w, conv_b, dt_bias, a_log, d_x, norm_g, _head_expand())


def _gmlp_in_body(x_ref, g_ref, sc_ref, sh_ref, w_ref, b_ref, lg_ref, lb_ref, o_ref, hn_ref):
    j = pl.program_id(1)

    @pl.when(j == 0)
    def _():
        hn_ref[...] = _prenorm(x_ref[...], g_ref[...], sc_ref[...], sh_ref[...]).astype(BF16)

    uv = jax.nn.gelu(_dot(hn_ref[...], w_ref[...].astype(BF16)) + b_ref[...], approximate=True)

    @pl.when(j == 0)
    def _():
        o_ref[...] = uv.astype(o_ref.dtype)

    @pl.when(j == 1)
    def _():
        vc = uv - jnp.mean(uv, axis=-1, keepdims=True)
        var = jnp.mean(vc * vc, axis=-1, keepdims=True)
        o_ref[...] = (vc * lax.rsqrt(var + EPS) * lg_ref[...] + lb_ref[...]).astype(o_ref.dtype)


def _gmlp_in_call(x, g, mod, w, b, ln_g, ln_b, tm, out_dtype, name):
    t, d = x.shape
    return pl.pallas_call(
        _gmlp_in_body,
        grid=(t // tm, 2),
        in_specs=[pl.BlockSpec((tm, d), lambda i, j: (i, 0)),
                  pl.BlockSpec((1, d), lambda i, j: (0, 0)),
                  mod.spec(1, tm, 2), mod.spec(0, tm, 2),
                  pl.BlockSpec((d, B_D), lambda i, j: (0, j)),
                  pl.BlockSpec((1, B_D), lambda i, j: (0, j)),
                  pl.BlockSpec((1, B_D), lambda i, j: (0, 0)),
                  pl.BlockSpec((1, B_D), lambda i, j: (0, 0))],
        out_specs=pl.BlockSpec((tm, B_D), lambda i, j: (i, j)),
        out_shape=jax.ShapeDtypeStruct((t, 2 * B_D), out_dtype),
        scratch_shapes=[pltpu.VMEM((tm, d), BF16)],
        compiler_params=_cparams(("arbitrary", "arbitrary")),
        name=name,
    )(x, g.reshape(1, d), mod.arr, mod.arr, w, b.reshape(1, -1), ln_g.reshape(1, -1), ln_b.reshape(1, -1))


def _gmlp_out_body(u_ref, v_ref, ws_ref, bs_ref, w_ref, x_ref, gate_ref, o_ref, wbf_ref, wsbf_ref, m_ref):
    q = B_CHUNK

    @pl.when(pl.program_id(0) == 0)
    def _():
        wbf_ref[...] = w_ref[...].astype(BF16)
        causal = lax.broadcasted_iota(I32, (q, q), 0) >= lax.broadcasted_iota(I32, (q, q), 1)
        for g in range(B_N_GROUPS):
            wsbf_ref[g] = jnp.where(causal, ws_ref[g], 0.0).astype(BF16)

    for ci in range(u_ref.shape[0] // q):
        rsl = slice(ci * q, (ci + 1) * q)
        for g in range(B_N_GROUPS):
            gsl = slice(g * B_GROUP_DIM, (g + 1) * B_GROUP_DIM)
            mixed = _dot(wsbf_ref[g], v_ref[rsl, gsl].astype(BF16)) + bs_ref[:, g:g + 1]
            m_ref[rsl, gsl] = (u_ref[rsl, gsl].astype(F32) * mixed).astype(BF16)
    o_ref[...] = x_ref[...] + gate_ref[...] * _dot(m_ref[...], wbf_ref[...])


def _gmlp_out_call(uv, w_s, b_s, w_out, x, mod, tm, name):
    t, d = x.shape
    q = B_CHUNK
    return pl.pallas_call(
        _gmlp_out_body,
        grid=(t // tm,),
        in_specs=[pl.BlockSpec((tm, B_D), lambda i: (i, 0)),
                  pl.BlockSpec((tm, B_D), lambda i: (i, 1)),
                  pl.BlockSpec((B_N_GROUPS, q, q), lambda i: (0, 0, 0)),
                  pl.BlockSpec((q, B_N_GROUPS), lambda i: (0, 0)),
                  pl.BlockSpec((B_D, d), lambda i: (0, 0)),
                  pl.BlockSpec((tm, d), lambda i: (i, 0)),
                  mod.spec(2, tm, 1)],
        out_specs=pl.BlockSpec((tm, d), lambda i: (i, 0)),
        out_shape=jax.ShapeDtypeStruct((t, d), F32),
        scratch_shapes=[pltpu.VMEM((B_D, d), BF16), pltpu.VMEM((B_N_GROUPS, q, q), BF16),
                        pltpu.VMEM((tm, B_D), BF16)],
        compiler_params=_cparams(("arbitrary",)),
        name=name,
    )(uv, uv, w_s, b_s.T, w_out, x, mod.arr)


def _gmlp_out_step_body(u_ref, v_ref, wd_ref, bd_ref, w_ref, x_ref, gate_ref, o_ref):
    mixed = v_ref[...] * wd_ref[...] + bd_ref[...]
    m = (u_ref[...] * mixed).astype(BF16)
    o_ref[...] = x_ref[...] + gate_ref[...] * _dot(m, w_ref[...].astype(BF16))


def _gmlp_out_step_call(uv, w_s, b_s, w_out, x, mod, name):
    t, d = x.shape
    wd = jnp.repeat(w_s[:, 0, 0], B_GROUP_DIM).reshape(1, B_D)
    bd = jnp.repeat(b_s[:, 0], B_GROUP_DIM).reshape(1, B_D)
    return pl.pallas_call(
        _gmlp_out_step_body,
        grid=(1,),
        in_specs=[pl.BlockSpec((t, B_D), lambda i: (0, 0)),
                  pl.BlockSpec((t, B_D), lambda i: (0, 1)),
                  pl.BlockSpec((1, B_D), lambda i: (0, 0)),
                  pl.BlockSpec((1, B_D), lambda i: (0, 0)),
                  pl.BlockSpec((B_D, d), lambda i: (0, 0)),
                  pl.BlockSpec((t, d), lambda i: (0, 0)),
                  mod.spec(2, t, 1)],
        out_specs=pl.BlockSpec((t, d), lambda i: (0, 0)),
        out_shape=jax.ShapeDtypeStruct((t, d), F32),
        compiler_params=_cparams(("arbitrary",)),
        name=name,
    )(uv, uv, wd, bd, w_out, x, mod.arr)


def _pool_matmul(pooled, wg_ref):
    outs = []
    for gi in range(len(C_WINDOWS)):
        gsl = slice(gi * C_GROUP_DIM, (gi + 1) * C_GROUP_DIM)
        outs.append(_dot(pooled[:, gsl].astype(BF16), wg_ref[gi].astype(BF16)))
    return jnp.concatenate(outs, axis=-1)


def _pool_body(x_ref, g_ref, sc_ref, sh_ref, gate_ref, prev_ref, wg_ref, scale_ref, o_ref, pnew_ref, hp_ref,
               *, tiles_per_seq, start):
    i = pl.program_id(0)
    tm = x_ref.shape[0]
    top = 16
    ti = i % tiles_per_seq

    @pl.when(ti == 0)
    def _():
        hp_ref[top - C_STATE:top, :] = prev_ref[...]

    hn = _prenorm(x_ref[...], g_ref[...], sc_ref[...], sh_ref[...])
    hp_ref[top:top + tm, :] = hn
    pos = start + ti * tm + lax.broadcasted_iota(I32, (tm, 1), 0)
    outs = []
    for gi, w in enumerate(C_WINDOWS):
        gsl = slice(gi * C_GROUP_DIM, (gi + 1) * C_GROUP_DIM)
        acc = hn[:, gsl]
        for k in range(1, w):
            acc = acc + hp_ref[top - k:top - k + tm, gsl]
        cnt = jnp.minimum(pos + 1, w).astype(F32)
        outs.append(acc / cnt - hn[:, gsl])
    y = _pool_matmul(jnp.concatenate(outs, axis=-1), wg_ref) * scale_ref[...]
    o_ref[...] = x_ref[...] + gate_ref[...] * y
    hist = hp_ref[top + tm - C_STATE:top + tm, :]
    hp_ref[top - C_STATE:top, :] = hist

    @pl.when(ti == tiles_per_seq - 1)
    def _():
        pnew_ref[...] = hist


def _pool_call(x, g, mod, prev, w_g, scale, nb, tm, start, name):
    t, d = x.shape
    tiles = t // nb // tm
    ng = len(C_WINDOWS)
    return pl.pallas_call(
        functools.partial(_pool_body, tiles_per_seq=tiles, start=start),
        grid=(t // tm,),
        in_specs=[pl.BlockSpec((tm, d), lambda i: (i, 0)),
                  pl.BlockSpec((1, d), lambda i: (0, 0)),
                  mod.spec(1, tm, 1), mod.spec(0, tm, 1), mod.spec(2, tm, 1),
                  pl.BlockSpec((None, C_STATE, d), lambda i: (i // tiles, 0, 0)),
                  pl.BlockSpec((ng, C_GROUP_DIM, C_GROUP_DIM), lambda i: (0, 0, 0)),
                  pl.BlockSpec((1, d), lambda i: (0, 0))],
        out_specs=[pl.BlockSpec((tm, d), lambda i: (i, 0)),
                   pl.BlockSpec((None, C_STATE, d), lambda i: (i // tiles, 0, 0))],
        out_shape=[jax.ShapeDtypeStruct((t, d), F32), jax.ShapeDtypeStruct((nb, C_STATE, d), F32)],
        scratch_shapes=[pltpu.VMEM((16 + tm, d), F32)],
        compiler_params=_cparams(("arbitrary",)),
        name=name,
    )(x, g.reshape(1, d), mod.arr, mod.arr, mod.arr, prev, w_g, scale.reshape(1, d))


def _pool_step_body(x_ref, g_ref, sc_ref, sh_ref, gate_ref, prev_ref, wg_ref, scale_ref, o_ref, pnew_ref, *, start):
    hn = _prenorm(x_ref[...], g_ref[...], sc_ref[...], sh_ref[...])
    outs = []
    for gi, w in enumerate(C_WINDOWS):
        gsl = slice(gi * C_GROUP_DIM, (gi + 1) * C_GROUP_DIM)
        acc = hn[:, gsl]
        for k in range(1, w):
            acc = acc + prev_ref[:, C_STATE - k, gsl]
        outs.append(acc / float(min(start + 1, w)) - hn[:, gsl])
    y = _pool_matmul(jnp.concatenate(outs, axis=-1), wg_ref) * scale_ref[...]
    o_ref[...] = x_ref[...] + gate_ref[...] * y
    for k in range(C_STATE - 1):
        pnew_ref[:, k, :] = prev_ref[:, k + 1, :]
    pnew_ref[:, C_STATE - 1, :] = hn


def _pool_step_call(x, g, mod, prev, w_g, scale, start, name):
    t, d = x.shape
    bt = 32
    ng = len(C_WINDOWS)
    return pl.pallas_call(
        functools.partial(_pool_step_body, start=start),
        grid=(t // bt,),
        in_specs=[pl.BlockSpec((bt, d), lambda i: (i, 0)),
                  pl.BlockSpec((1, d), lambda i: (0, 0)),
                  mod.spec(1, bt, 1), mod.spec(0, bt, 1), mod.spec(2, bt, 1),
                  pl.BlockSpec((bt, C_STATE, d), lambda i: (i, 0, 0)),
                  pl.BlockSpec((ng, C_GROUP_DIM, C_GROUP_DIM), lambda i: (0, 0, 0)),
                  pl.BlockSpec((1, d), lambda i: (0, 0))],
        out_specs=[pl.BlockSpec((bt, d), lambda i: (i, 0)),
                   pl.BlockSpec((bt, C_STATE, d), lambda i: (i, 0, 0))],
        out_shape=[jax.ShapeDtypeStruct((t, d), F32), jax.ShapeDtypeStruct((t, C_STATE, d), F32)],
        compiler_params=_cparams(("arbitrary",)),
        name=name,
    )(x, g.reshape(1, d), mod.arr, mod.arr, mod.arr, prev, w_g, scale.reshape(1, d))


def _route_rows(s, b):
    npg = EXPERTS_PER_GROUP
    gscore = []
    for q in range(N_EXPERT_GROUPS):
        v = b[q * npg:(q + 1) * npg]
        best = None
        for i in range(npg):
            for j in range(i + 1, npg):
                best = v[i] + v[j] if best is None else jnp.maximum(best, v[i] + v[j])
        gscore.append(best)
    gsel = jnp.zeros_like(gscore[0], dtype=I32)
    gbest = gscore[0]
    for q in range(1, N_EXPERT_GROUPS):
        better = gscore[q] > gbest
        gsel = jnp.where(better, q, gsel)
        gbest = jnp.where(better, gscore[q], gbest)
    vb, vs = [], []
    for k in range(npg):
        bk, sk = b[k], s[k]
        for q in range(1, N_EXPERT_GROUPS):
            bk = jnp.where(gsel == q, b[q * npg + k], bk)
            sk = jnp.where(gsel == q, s[q * npg + k], sk)
        vb.append(bk)
        vs.append(sk)
    i1 = jnp.zeros_like(gsel)
    m1 = vb[0]
    for k in range(1, npg):
        better = vb[k] > m1
        i1 = jnp.where(better, k, i1)
        m1 = jnp.where(better, vb[k], m1)
    i2 = jnp.full_like(gsel, -1)
    m2 = jnp.zeros_like(m1)
    for k in range(npg):
        better = (i1 != k) & ((i2 < 0) | (vb[k] > m2))
        i2 = jnp.where(better, k, i2)
        m2 = jnp.where(better, vb[k], m2)
    s1 = vs[0]
    s2 = vs[0]
    for k in range(1, npg):
        s1 = jnp.where(i1 == k, vs[k], s1)
        s2 = jnp.where(i2 == k, vs[k], s2)
    w1 = s1 / (s1 + s2)
    w2 = s2 / (s1 + s2)
    lo = jnp.minimum(i1, i2)
    hi = jnp.maximum(i1, i2)
    pair = jnp.zeros_like(gsel)
    for p, (a, c) in enumerate(PAIRS):
        pair = jnp.where((lo == a) & (hi == c), p, pair)
    first_is_lo = i1 < i2
    return (gsel * len(PAIRS) + pair, jnp.where(first_is_lo, w1, w2), jnp.where(first_is_lo, w2, w1))


def _route_body(x_ref, g_ref, sc_ref, sh_ref, rw_ref, rb_ref, cnt_in_ref, *rest, aliased):
    if aliased:
        rest = rest[3:]
    rows_ref, bucket_ref, rank_ref, cnt_ref = rest
    tm = x_ref.shape[0]

    @pl.when(pl.program_id(0) == 0)
    def _():
        cnt_ref[...] = cnt_in_ref[...]

    hn = _prenorm(x_ref[...], g_ref[...], sc_ref[...], sh_ref[...])
    logits_t = _dot3(hn, rw_ref[...]).T
    scores = _sigmoid(logits_t[0:N_EXPERTS, :])
    biased = scores + rb_ref[0:N_EXPERTS, :]
    bucket, w_lo, w_hi = _route_rows([scores[e:e + 1, :] for e in range(N_EXPERTS)],
                                     [biased[e:e + 1, :] for e in range(N_EXPERTS)])
    bucket_ref[...] = bucket
    onehot = (lax.broadcasted_iota(I32, (BUCKET_ROWS, tm), 0) == bucket).astype(F32)
    before = (lax.broadcasted_iota(I32, (tm, tm), 0) < lax.broadcasted_iota(I32, (tm, tm), 1)).astype(BF16)
    earlier = _dot(onehot.astype(BF16), before) + cnt_ref[:, 0:1]
    rank_ref[...] = jnp.sum(onehot * earlier, axis=0, keepdims=True).astype(I32)
    cnt_ref[...] = cnt_ref[...] + jnp.sum(onehot, axis=1, keepdims=True)

    lo_bits = lax.bitcast_convert_type(w_lo, I32)
    hi_bits = lax.bitcast_convert_type(w_hi, I32)
    parts = [lax.shift_right_logical(lo_bits, 16), lo_bits & 0xFFFF,
             lax.shift_right_logical(hi_bits, 16), hi_bits & 0xFFFF]
    sub = lax.broadcasted_iota(I32, (LANE, tm), 0)
    gate_t = jnp.zeros((LANE, tm), F32)
    for k, part in enumerate(parts):
        gate_t = jnp.where(sub == k, part.astype(F32), gate_t)
    gate_words = gate_t.T.astype(I32)
    words = lax.bitcast_convert_type(hn.astype(BF16).astype(F32), I32)
    for j in range(ROW_PLANES):
        plane = words[:, j * LANE:(j + 1) * LANE]
        rows_ref[:, j, :] = (plane | gate_words) if j == 0 else plane


def _route_call(x, g, mod, router_w, router_b, counts, tm, total_rows, row_offset, prior, name):
    t, d = x.shape
    blk0 = row_offset // tm
    rw = jnp.pad(router_w, ((0, 0), (0, LANE - N_EXPERTS)))
    rb = jnp.pad(router_b.reshape(-1, 1), ((0, LANE - N_EXPERTS), (0, 0)))
    in_specs = [pl.BlockSpec((tm, d), lambda i: (i, 0)),
                pl.BlockSpec((1, d), lambda i: (0, 0)),
                mod.spec(4, tm, 1), mod.spec(3, tm, 1),
                pl.BlockSpec((d, LANE), lambda i: (0, 0)),
                pl.BlockSpec((LANE, 1), lambda i: (0, 0)),
                pl.BlockSpec((BUCKET_ROWS, LANE), lambda i: (0, 0))]
    args = [x, g.reshape(1, d), mod.arr, mod.arr, rw, rb, counts]
    aliases = {}
    if prior is not None:
        in_specs += [pl.BlockSpec(memory_space=pl.ANY)] * 3
        aliases = {len(args): 0, len(args) + 1: 1, len(args) + 2: 2}
        args += list(prior)
    return pl.pallas_call(
        functools.partial(_route_body, aliased=prior is not None),
        grid=(t // tm,),
        in_specs=in_specs,
        out_specs=[pl.BlockSpec((tm, ROW_PLANES, LANE), lambda i: (blk0 + i, 0, 0)),
                   pl.BlockSpec((1, tm), lambda i: (0, blk0 + i)),
                   pl.BlockSpec((1, tm), lambda i: (0, blk0 + i)),
                   pl.BlockSpec((BUCKET_ROWS, LANE), lambda i: (0, 0))],
        out_shape=[jax.ShapeDtypeStruct((total_rows, ROW_PLANES, LANE), I32),
                   jax.ShapeDtypeStruct((1, total_rows), I32),
                   jax.ShapeDtypeStruct((1, total_rows), I32),
                   jax.ShapeDtypeStruct((BUCKET_ROWS, LANE), F32)],
        input_output_aliases=aliases,
        compiler_params=_cparams(("arbitrary",)),
        name=name,
    )(*args)


def _permute_body(dest_ref, rows_ref, zeros_ref, out_ref, sem, *, chunk):
    del zeros_ref
    base = pl.program_id(0) * chunk

    def issue(i, carry):
        pltpu.make_async_copy(rows_ref.at[base + i], out_ref.at[dest_ref[base + i]], sem).start()
        return carry

    lax.fori_loop(0, chunk, issue, 0)

    def drain(i, carry):
        pltpu.make_async_copy(rows_ref.at[base + i], out_ref.at[dest_ref[base + i]], sem).wait()
        return carry

    lax.fori_loop(0, chunk, drain, 0)


def _permute_chunk(total, limit=8192):
    return max(c for c in range(1, limit + 1) if total % c == 0)


def _permute_call(dest, rows, sorted_rows, chunk, name):
    t = rows.shape[0]
    zeros = jnp.zeros((sorted_rows,) + rows.shape[1:], rows.dtype)
    return pl.pallas_call(
        functools.partial(_permute_body, chunk=chunk),
        grid_spec=pltpu.PrefetchScalarGridSpec(
            num_scalar_prefetch=1,
            grid=(t // chunk,),
            in_specs=[pl.BlockSpec(memory_space=pl.ANY), pl.BlockSpec(memory_space=pl.ANY)],
            out_specs=pl.BlockSpec(memory_space=pl.ANY),
            scratch_shapes=[pltpu.SemaphoreType.DMA(())]),
        out_shape=jax.ShapeDtypeStruct(zeros.shape, zeros.dtype),
        input_output_aliases={2: 0},
        compiler_params=_cparams(("arbitrary",)),
        name=name,
    )(dest, rows, zeros)


def _experts_body(elo_ref, ehi_ref, fresh_ref, valid_ref, rows_ref, wg0_ref, wg1_ref, wu0_ref, wu1_ref,
                  wd0_ref, wd1_ref, y_ref, wg_ref, wu_ref, wd_ref, x_ref):
    i = pl.program_id(0)
    f = D_EXPERT

    @pl.when(fresh_ref[i] == 1)
    def _():
        wg_ref[:, 0:f] = wg0_ref[...].astype(BF16)
        wg_ref[:, f:2 * f] = wg1_ref[...].astype(BF16)
        wu_ref[:, 0:f] = wu0_ref[...].astype(BF16)
        wu_ref[:, f:2 * f] = wu1_ref[...].astype(BF16)
        wd_ref[0:f, :] = wd0_ref[...].astype(BF16)
        wd_ref[f:2 * f, :] = wd1_ref[...].astype(BF16)

    @pl.when(valid_ref[i] == 1)
    def _():
        for j in range(ROW_PLANES):
            plane = rows_ref[:, j, :] & jnp.int32(-65536)
            x_ref[:, j * LANE:(j + 1) * LANE] = lax.bitcast_convert_type(plane, F32).astype(BF16)
        gw = rows_ref[:, 0, :] & jnp.int32(0xFFFF)
        w_lo = lax.bitcast_convert_type((gw[:, 0:1] << 16) | gw[:, 1:2], F32)
        w_hi = lax.bitcast_convert_type((gw[:, 2:3] << 16) | gw[:, 3:4], F32)
        x = x_ref[...]
        act = _silu(_dot(x, wg_ref[...])) * _dot(x, wu_ref[...])
        lane = lax.broadcasted_iota(I32, act.shape, 1)
        act = act * jnp.where(lane < f, w_lo, w_hi)
        y = _dot(act.astype(BF16), wd_ref[...])
        for j in range(ROW_PLANES):
            y_ref[:, j, :] = y[:, j * LANE:(j + 1) * LANE]

    @pl.when(valid_ref[i] == 0)
    def _():
        y_ref[...] = jnp.zeros_like(y_ref)


def _experts_call(e_lo, e_hi, fresh, valid, sorted_rows, w_gate, w_up, w_down, name):
    n_tiles = e_lo.shape[0]
    tile = EXPERT_TILE
    d, f = D_MODEL, D_EXPERT
    lo = lambda i, elo, ehi, fr, va: (elo[i], 0, 0)
    hi = lambda i, elo, ehi, fr, va: (ehi[i], 0, 0)
    return pl.pallas_call(
        _experts_body,
        grid_spec=pltpu.PrefetchScalarGridSpec(
            num_scalar_prefetch=4,
            grid=(n_tiles,),
            in_specs=[pl.BlockSpec((tile, ROW_PLANES, LANE), lambda i, *_: (i, 0, 0)),
                      pl.BlockSpec((None, d, f), lo), pl.BlockSpec((None, d, f), hi),
                      pl.BlockSpec((None, d, f), lo), pl.BlockSpec((None, d, f), hi),
                      pl.BlockSpec((None, f, d), lo), pl.BlockSpec((None, f, d), hi)],
            out_specs=pl.BlockSpec((tile, ROW_PLANES, LANE), lambda i, *_: (i, 0, 0)),
            scratch_shapes=[pltpu.VMEM((d, 2 * f), BF16), pltpu.VMEM((d, 2 * f), BF16),
                            pltpu.VMEM((2 * f, d), BF16), pltpu.VMEM((tile, d), BF16)]),
        out_shape=jax.ShapeDtypeStruct((n_tiles * tile, ROW_PLANES, LANE), F32),
        compiler_params=_cparams(("arbitrary",)),
        name=name,
    )(e_lo, e_hi, fresh, valid, sorted_rows, w_gate, w_gate, w_up, w_up, w_down, w_down)


def _unpermute_body(dest_ref, ys_ref, x_ref, gate_ref, fg_ref, o_ref, *rest, tm, row_offset, final):
    if final:
        on_ref, buf_ref, sem = rest
    else:
        buf_ref, sem = rest
    i = pl.program_id(0)
    n = pl.num_programs(0)

    def copies(step, slot, fn):
        base = row_offset + step * tm

        def body(r, carry):
            fn(pltpu.make_async_copy(ys_ref.at[dest_ref[base + r]], buf_ref.at[slot, r], sem.at[slot]))
            return carry

        lax.fori_loop(0, tm, body, 0)

    @pl.when(i == 0)
    def _():
        copies(0, 0, lambda cp: cp.start())

    @pl.when(i + 1 < n)
    def _():
        copies(i + 1, (i + 1) % 2, lambda cp: cp.start())

    slot = i % 2
    copies(i, slot, lambda cp: cp.wait())
    y = jnp.concatenate([buf_ref[slot, :, j, :] for j in range(ROW_PLANES)], axis=-1)
    xn = x_ref[...] + gate_ref[...] * y
    o_ref[...] = xn
    if final:
        ms = jnp.mean(xn * xn, axis=-1, keepdims=True)
        on_ref[...] = xn * lax.rsqrt(ms + EPS) * fg_ref[...]


def _unpermute_call(dest, y_sorted, x, mod, final_g, tm, row_offset, final, name):
    t, d = x.shape
    n_out = 2 if final else 1
    gate_spec = mod.spec(5, tm, 1)
    gate_map = gate_spec.index_map
    out_spec = pl.BlockSpec((tm, d), lambda i, dest: (i, 0))
    return pl.pallas_call(
        functools.partial(_unpermute_body, tm=tm, row_offset=row_offset, final=final),
        grid_spec=pltpu.PrefetchScalarGridSpec(
            num_scalar_prefetch=1,
            grid=(t // tm,),
            in_specs=[pl.BlockSpec(memory_space=pl.ANY),
                      pl.BlockSpec((tm, d), lambda i, dest: (i, 0)),
                      pl.BlockSpec(gate_spec.block_shape, lambda i, dest: gate_map(i)),
                      pl.BlockSpec((1, d), lambda i, dest: (0, 0))],
            out_specs=[out_spec] * n_out,
            scratch_shapes=[pltpu.VMEM((2, tm, ROW_PLANES, LANE), F32), pltpu.SemaphoreType.DMA((2,))]),
        out_shape=[jax.ShapeDtypeStruct((t, d), F32)] * n_out,
        compiler_params=_cparams(("arbitrary",)),
        name=name,
    )(dest, y_sorted, x, mod.arr, final_g.reshape(1, d))


def _tile_tables(counts, n_tiles):
    tile = EXPERT_TILE
    tiles_per_bucket = (counts + tile - 1) // tile
    ends = jnp.cumsum(tiles_per_bucket)
    starts = ends - tiles_per_bucket
    used = ends[-1]
    ti = jnp.arange(n_tiles, dtype=I32)
    tile_bucket = jnp.minimum(jnp.searchsorted(ends, ti, side="right"), N_BUCKETS - 1).astype(I32)
    last_bucket = jnp.minimum(jnp.searchsorted(ends, used - 1, side="right"), N_BUCKETS - 1).astype(I32)
    valid = (ti < used).astype(I32)
    tile_bucket = jnp.where(valid == 1, tile_bucket, last_bucket)
    pair_lo = jnp.asarray([p[0] for p in PAIRS], I32)
    pair_hi = jnp.asarray([p[1] for p in PAIRS], I32)
    grp = tile_bucket // len(PAIRS)
    e_lo = grp * EXPERTS_PER_GROUP + pair_lo[tile_bucket % len(PAIRS)]
    e_hi = grp * EXPERTS_PER_GROUP + pair_hi[tile_bucket % len(PAIRS)]
    fresh = jnp.concatenate([jnp.ones((1,), I32), (tile_bucket[1:] != tile_bucket[:-1]).astype(I32)])
    return e_lo, e_hi, fresh, valid, (starts * tile).astype(I32)


def _moe_layer(xp, xs, norm_g2, mod_p, mod_s, router_w, router_b, w_gate, w_up, w_down, final_g, final):
    tp, ts = xp.shape[0], xs.shape[0]
    total = tp + ts
    n_tiles = -(-total // EXPERT_TILE) + N_BUCKETS
    zero_counts = jnp.zeros((BUCKET_ROWS, LANE), F32)
    rows, bucket, rank, counts = _route_call(xp, norm_g2, mod_p, router_w, router_b, zero_counts, 512, total, 0,
                                             None, "route_prompt")
    rows, bucket, rank, counts = _route_call(xs, norm_g2, mod_s, router_w, router_b, counts, ts, total, tp,
                                             (rows, bucket, rank), "route_sample")
    e_lo, e_hi, fresh, valid, bucket_start = _tile_tables(counts[:N_BUCKETS, 0].astype(I32), n_tiles)
    dest = bucket_start[bucket[0]] + rank[0]
    sorted_rows = _permute_call(dest, rows, n_tiles * EXPERT_TILE, _permute_chunk(total), "permute")
    y_sorted = _experts_call(e_lo, e_hi, fresh, valid, sorted_rows, w_gate, w_up, w_down, "experts")
    outp = _unpermute_call(dest, y_sorted, xp, mod_p, final_g, 256, 0, final, "unpermute_prompt")
    outs = _unpermute_call(dest, y_sorted, xs, mod_s, final_g, ts, tp, final, "unpermute_sample")
    return outp, outs


def _mamba_layer(x, g, mod, conv_prev, ssm_prev, w_in, conv_w, conv_b, dt_bias, a_log, d_skip, norm_g, w_out,
                 nb, step, tm, tag):
    w_dt = jnp.pad(w_in[:, A_ZX:], ((0, 0), (0, LANE - A_N_HEADS)))
    pad_h = lambda v: jnp.pad(v.reshape(1, -1), ((0, 0), (0, LANE - A_N_HEADS)))
    proj = _norm_mm_call(x, g, mod, 1, 0, w_in, A_ZX, tm, min(512, A_ZX), BF16, "a_in_" + tag)
    dt_raw = _norm_mm3_call(x, g, mod, 1, 0, w_dt, tm, "a_dt_" + tag)
    d_x = jnp.repeat(d_skip, A_HEAD_DIM).reshape(1, A_D_INNER)
    ssm2d = ssm_prev.reshape(ssm_prev.shape[0], A_N_HEADS * A_HEAD_DIM, A_D_STATE)
    args = (proj, dt_raw, conv_prev, ssm2d, conv_w, conv_b.reshape(1, -1), pad_h(dt_bias), pad_h(a_log), d_x,
            norm_g.reshape(1, -1))
    if step:
        yn, conv_new, ssm_new = _ssd_step_call(*args, "ssd_step_" + tag)
    else:
        yn, conv_new, ssm_new = _ssd_call(*args, nb, "ssd_" + tag)
    x = _out_res_call(yn, w_out, x, mod, 2, tm, "a_out_" + tag)
    return x, conv_new, ssm_new.reshape(ssm_prev.shape)


def kernel(x_prompt, x_sample, c_prompt, c_sample, state_a_conv, state_a_ssm, state_c_pool, w_mod, b_mod, norm_g, final_g, a_w_in, a_conv_w, a_conv_b, a_dt_bias, a_log, a_d, a_norm_g, a_w_out, b_w_in, b_b_in, b_ln_g, b_ln_b, b_w_s, b_b_s, b_w_out, c_w_g, c_scale, router_w, router_b, e_w_gate, e_w_up, e_w_down):
    bp, seq, d = x_prompt.shape
    bs = x_sample.shape[0]
    n_a, n_c = state_a_conv.shape[0], state_c_pool.shape[0]
    mod_all = _mod_call(jnp.concatenate([c_prompt, c_sample], axis=0), w_mod, b_mod)
    mod_p_arr = mod_all[:, :bp].reshape(DEPTH, bp, 6, 1, d)
    mod_s_arr = mod_all[:, bp:]
    xp = x_prompt.reshape(bp * seq, d)
    xs = x_sample.reshape(bs, d)
    conv_p, ssm_p, pool_p, conv_s, ssm_s, pool_s, v_s = [], [], [], [], [], [], []
    yp = ys = None
    for i in range(DEPTH):
        kind, s = LAYER_KIND[i], LAYER_SLOT[i]
        mod_p = Mod(mod_p_arr, i, False, seq)
        mod_s = Mod(mod_s_arr, i, True)
        g1 = norm_g[i, 0]
        if kind == 0:
            weights = (a_w_in[s], a_conv_w[s], a_conv_b[s], a_dt_bias[s], a_log[s], a_d[s], a_norm_g[s], a_w_out[s])
            conv0 = jnp.zeros((bp, A_CONV - 1, A_CONV_CH), F32)
            ssm0 = jnp.zeros((bp, A_N_HEADS, A_HEAD_DIM, A_D_STATE), F32)
            xp, cv, ss = _mamba_layer(xp, g1, mod_p, conv0, ssm0, *weights, bp, False, 1024, "p%d" % i)
            conv_p.append(cv)
            ssm_p.append(ss)
            xs, cv, ss = _mamba_layer(xs, g1, mod_s, state_a_conv[s], state_a_ssm[s], *weights, bs, True, bs,
                                      "s%d" % i)
            conv_s.append(cv)
            ssm_s.append(ss)
        elif kind == 1:
            uv = _gmlp_in_call(xp, g1, mod_p, b_w_in[s], b_b_in[s], b_ln_g[s], b_ln_b[s], 512, BF16, "b_in_p%d" % i)
            xp = _gmlp_out_call(uv, b_w_s[s], b_b_s[s], b_w_out[s], xp, mod_p, 512, "b_out_p%d" % i)
            uv = _gmlp_in_call(xs, g1, mod_s, b_w_in[s], b_b_in[s], b_ln_g[s], b_ln_b[s], bs, F32, "b_in_s%d" % i)
            xs = _gmlp_out_step_call(uv, b_w_s[s], b_b_s[s], b_w_out[s], xs, mod_s, "b_out_s%d" % i)
            v_s.append(uv[:, B_D:].reshape(bs, 1, B_D))
        else:
            pool0 = jnp.zeros((bp, C_STATE, d), F32)
            xp, pr = _pool_call(xp, g1, mod_p, pool0, c_w_g[s], c_scale[s], bp, 512, 0, "pool_p%d" % i)
            pool_p.append(pr)
            xs, pr = _pool_step_call(xs, g1, mod_s, state_c_pool[s], c_w_g[s], c_scale[s], PAST_LEN, "pool_s%d" % i)
            pool_s.append(pr)
        final = i == DEPTH - 1
        outp, outs = _moe_layer(xp, xs, norm_g[i, 1], mod_p, mod_s, router_w, router_b, e_w_gate[i], e_w_up[i],
                                e_w_down[i], final_g, final)
        xp, xs = outp[0], outs[0]
        if final:
            yp, ys = outp[1], outs[1]
    return (yp.reshape(bp, seq, d), ys.reshape(bs, 1, d), jnp.stack(conv_p), jnp.stack(ssm_p), jnp.stack(pool_p),
            jnp.stack(conv_s), jnp.stack(ssm_s), jnp.stack(pool_s), jnp.stack(v_s))
```

```python
import functools
import math

import numpy as np
import jax
import jax.numpy as jnp
from jax import lax
from jax.experimental import pallas as pl
from jax.experimental.pallas import tpu as pltpu

F32 = jnp.float32
BF16 = jnp.bfloat16
I32 = jnp.int32
EPS = 1e-6

LANE = 128
D_MODEL = 1024
DEPTH = 4
PAST_LEN = 16384
LAYER_KIND = (0, 1, 2, 0)
LAYER_SLOT = (0, 0, 0, 1)
A_D_INNER = 2 * D_MODEL
A_HEAD_DIM = 64
A_N_HEADS = A_D_INNER // A_HEAD_DIM
A_N_GROUPS = 8
A_HPG = A_N_HEADS // A_N_GROUPS
A_D_STATE = 128
A_GN = A_N_GROUPS * A_D_STATE
A_CONV = 4
A_CONV_CH = A_D_INNER + 2 * A_GN
A_ZX = A_D_INNER + A_CONV_CH
A_CHUNK = 128
A_GW = A_HPG * A_HEAD_DIM
B_D = 2 * D_MODEL
B_N_GROUPS = 8
B_GROUP_DIM = B_D // B_N_GROUPS
B_CHUNK = 128
C_WINDOWS = (2, 4, 8, 16)
C_GROUP_DIM = D_MODEL // len(C_WINDOWS)
C_STATE = max(C_WINDOWS) - 1
N_EXPERTS = 16
N_EXPERT_GROUPS = 4
EXPERTS_PER_GROUP = 4
D_EXPERT = D_MODEL // 2
PAIRS = ((0, 1), (0, 2), (0, 3), (1, 2), (1, 3), (2, 3))
N_BUCKETS = N_EXPERT_GROUPS * len(PAIRS)
BUCKET_ROWS = 32
ROW_PLANES = D_MODEL // LANE
EXPERT_TILE = 256
VMEM_LIMIT = 56 * 1024 * 1024


def _cparams(sem, vmem=VMEM_LIMIT):
    return pltpu.CompilerParams(dimension_semantics=sem, vmem_limit_bytes=vmem)


def _sigmoid(x):
    return 1.0 / (1.0 + jnp.exp(-x))


def _silu(x):
    return x * _sigmoid(x)


def _softplus(x):
    return jnp.maximum(x, 0.0) + jnp.log1p(jnp.exp(-jnp.abs(x)))


def _split2(a):
    hi = a.astype(BF16)
    lo = (a - hi.astype(F32)).astype(BF16)
    return hi, lo


def _split3(a):
    hi = a.astype(BF16)
    r = a - hi.astype(F32)
    mid = r.astype(BF16)
    lo = (r - mid.astype(F32)).astype(BF16)
    return hi, mid, lo


def _dot(a, b):
    return jnp.dot(a, b, preferred_element_type=F32)


def _dot_nt(a, b):
    return lax.dot_general(a, b, (((1,), (1,)), ((), ())), preferred_element_type=F32)


def _dot_tn(a, b):
    return lax.dot_general(a, b, (((0,), (0,)), ((), ())), preferred_element_type=F32)


def _dot3(a, b):
    a_hi, a_lo = _split2(a)
    b_hi, b_lo = _split2(b)
    return _dot(a_hi, b_hi) + (_dot(a_lo, b_hi) + _dot(a_hi, b_lo))


def _prenorm(x, g, sc, sh):
    ms = jnp.mean(x * x, axis=-1, keepdims=True)
    return (x * lax.rsqrt(ms + EPS) * g) * (1.0 + sc) + sh


def _mod_body(c_ref, w_ref, b_ref, o_ref):
    o_ref[...] = _dot3(_silu(c_ref[...]), w_ref[...]) + b_ref[...]


def _mod_call(c_all, w_mod, b_mod):
    nb, d = c_all.shape
    depth, _, n = w_mod.shape
    tn = 1536
    return pl.pallas_call(
        _mod_body,
        grid=(depth, n // tn),
        in_specs=[pl.BlockSpec((nb, d), lambda i, j: (0, 0)),
                  pl.BlockSpec((None, d, tn), lambda i, j: (i, 0, j)),
                  pl.BlockSpec((None, 1, tn), lambda i, j: (i, 0, j))],
        out_specs=pl.BlockSpec((None, nb, tn), lambda i, j: (i, 0, j)),
        out_shape=jax.ShapeDtypeStruct((depth, nb, n), F32),
        compiler_params=_cparams(("arbitrary", "arbitrary")),
        name="mod",
    )(c_all, w_mod, b_mod.reshape(depth, 1, n))


class Mod:
    def __init__(self, arr, layer, per_row, rows_per_seq=None):
        self.arr, self.layer, self.per_row, self.rows_per_seq = arr, layer, per_row, rows_per_seq

    def spec(self, which, tm, ngrid):
        layer = self.layer
        if self.per_row:
            if ngrid == 1:
                return pl.BlockSpec((None, tm, D_MODEL), lambda i: (layer, i, which))
            return pl.BlockSpec((None, tm, D_MODEL), lambda i, j: (layer, i, which))
        tiles = self.rows_per_seq // tm
        if ngrid == 1:
            return pl.BlockSpec((None, None, None, 1, D_MODEL), lambda i: (layer, i // tiles, which, 0, 0))
        return pl.BlockSpec((None, None, None, 1, D_MODEL), lambda i, j: (layer, i // tiles, which, 0, 0))


def _norm_mm_body(x_ref, g_ref, sc_ref, sh_ref, w_ref, o_ref, hn_ref):
    @pl.when(pl.program_id(1) == 0)
    def _():
        hn_ref[...] = _prenorm(x_ref[...], g_ref[...], sc_ref[...], sh_ref[...]).astype(BF16)

    o_ref[...] = _dot(hn_ref[...], w_ref[...].astype(BF16)).astype(o_ref.dtype)


def _norm_mm_call(x, g, mod, which_sc, which_sh, w_all, slot, n_cols, tm, tn, out_dtype, name):
    t, d = x.shape
    return pl.pallas_call(
        _norm_mm_body,
        grid=(t // tm, n_cols // tn),
        in_specs=[pl.BlockSpec((tm, d), lambda i, j: (i, 0)),
                  pl.BlockSpec((1, d), lambda i, j: (0, 0)),
                  mod.spec(which_sc, tm, 2), mod.spec(which_sh, tm, 2),
                  pl.BlockSpec((None, d, tn), lambda i, j: (slot, 0, j))],
        out_specs=pl.BlockSpec((tm, tn), lambda i, j: (i, j)),
        out_shape=jax.ShapeDtypeStruct((t, n_cols), out_dtype),
        scratch_shapes=[pltpu.VMEM((tm, d), BF16)],
        compiler_params=_cparams(("arbitrary", "arbitrary")),
        name=name,
    )(x, g.reshape(1, d), mod.arr, mod.arr, w_all)


def _norm_mm3_body(x_ref, g_ref, sc_ref, sh_ref, w_ref, o_ref):
    hn = _prenorm(x_ref[...], g_ref[...], sc_ref[...], sh_ref[...])
    o_ref[...] = _dot3(hn, w_ref[...])


def _norm_mm3_call(x, g, mod, which_sc, which_sh, w, tm, name):
    t, d = x.shape
    n = w.shape[1]
    return pl.pallas_call(
        _norm_mm3_body,
        grid=(t // tm,),
        in_specs=[pl.BlockSpec((tm, d), lambda i: (i, 0)),
                  pl.BlockSpec((1, d), lambda i: (0, 0)),
                  mod.spec(which_sc, tm, 1), mod.spec(which_sh, tm, 1),
                  pl.BlockSpec((d, n), lambda i: (0, 0))],
        out_specs=pl.BlockSpec((tm, n), lambda i: (i, 0)),
        out_shape=jax.ShapeDtypeStruct((t, n), F32),
        compiler_params=_cparams(("arbitrary",)),
        name=name,
    )(x, g.reshape(1, d), mod.arr, mod.arr, w)


def _out_res_body(y_ref, w_ref, x_ref, gate_ref, o_ref, wbf_ref):
    @pl.when(pl.program_id(0) == 0)
    def _():
        wbf_ref[...] = w_ref[...].astype(BF16)

    o_ref[...] = x_ref[...] + gate_ref[...] * _dot(y_ref[...], wbf_ref[...])


def _out_res_call(y, w, x, mod, which_gate, tm, name):
    t, k = y.shape
    d = x.shape[1]
    return pl.pallas_call(
        _out_res_body,
        grid=(t // tm,),
        in_specs=[pl.BlockSpec((tm, k), lambda i: (i, 0)),
                  pl.BlockSpec((k, d), lambda i: (0, 0)),
                  pl.BlockSpec((tm, d), lambda i: (i, 0)),
                  mod.spec(which_gate, tm, 1)],
        out_specs=pl.BlockSpec((tm, d), lambda i: (i, 0)),
        out_shape=jax.ShapeDtypeStruct((t, d), F32),
        scratch_shapes=[pltpu.VMEM((k, d), BF16)],
        compiler_params=_cparams(("arbitrary",)),
        name=name,
    )(y, w, x, mod.arr)


def _head_expand():
    h = np.arange(LANE)[:, None]
    c = np.arange(A_D_INNER)[None, :]
    return jnp.asarray((c // A_HEAD_DIM == h).astype(np.float32), dtype=BF16)


def _gate_norm(y, z, ng):
    gated = y * _silu(z)
    ms = jnp.mean(gated * gated, axis=-1, keepdims=True)
    return gated * lax.rsqrt(ms + EPS) * ng


def _ssd_body(z_ref, xs_ref, bc_ref, dt_ref, cprev_ref, sprev_ref, cw_ref, cb_ref, dtb_ref, alog_ref,
              dx_ref, ng_ref, exp_ref, yn_ref, cnew_ref, snew_ref, tail_ref, act_ref):
    c = pl.program_id(1)
    q = A_CHUNK
    nt = A_CONV - 1

    @pl.when(c == 0)
    def _():
        tail_ref[0:nt, :] = cprev_ref[...]
        snew_ref[...] = sprev_ref[...]

    row = lax.broadcasted_iota(I32, (q, q), 0)
    col = lax.broadcasted_iota(I32, (q, q), 1)
    shifts = [jnp.where(row - col == nt - k, 1.0, 0.0).astype(BF16) for k in range(nt)]
    sub = lax.broadcasted_iota(I32, (8, 1), 0)
    cw = 512
    for j in range(A_CONV_CH // cw):
        sl = slice(j * cw, (j + 1) * cw)
        src = xs_ref if (j + 1) * cw <= A_D_INNER else bc_ref
        off = j * cw if src is xs_ref else j * cw - A_D_INNER
        x_bf = src[:, off:off + cw]
        conv = cb_ref[:, sl] + x_bf.astype(F32) * cw_ref[nt:nt + 1, sl]
        for k in range(nt):
            conv = conv + _dot(shifts[k], x_bf) * cw_ref[k:k + 1, sl]
        act_ref[:, sl] = _silu(conv)
        corr = jnp.zeros((8, cw), F32)
        for l in range(nt):
            c_l = sum(tail_ref[l + k:l + k + 1, sl] * cw_ref[k:k + 1, sl] for k in range(nt - l))
            corr = jnp.where(sub == l, c_l, corr)
        act_ref[0:8, sl] = _silu(conv[0:8, :] + corr)
        tail_ref[0:nt, sl] = x_bf[q - 8:q, :].astype(F32)[8 - nt:8, :]

    @pl.when(c == pl.num_programs(1) - 1)
    def _():
        cnew_ref[...] = tail_ref[0:nt, :]

    dt = _softplus(dt_ref[...] + dtb_ref[...])
    a = dt * (-jnp.exp(alog_ref[...]))
    row = lax.broadcasted_iota(I32, (q, q), 0)
    col = lax.broadcasted_iota(I32, (q, q), 1)
    causal = row >= col
    tril = jnp.where(causal, 1.0, 0.0).astype(BF16)
    a_hi, a_mid, a_lo = _split3(a)
    cs = _dot(tril, a_hi) + (_dot(tril, a_mid) + _dot(tril, a_lo))
    cs_t = cs.T
    dt_t = dt.T
    cs_last = cs[q - 1:q, :]
    ecs = jnp.exp(cs)
    wend = jnp.exp(cs_last - cs) * dt
    st_hi, st_lo = _split2(jnp.concatenate([ecs, wend], axis=0))
    st_x = _dot(st_hi, exp_ref[...]) + _dot(st_lo, exp_ref[...])
    lane_head = lax.broadcasted_iota(I32, (q, A_GW), 1) // A_HEAD_DIM

    for g in range(A_N_GROUPS):
        gsl = slice(g * A_GW, (g + 1) * A_GW)
        b_g = act_ref[:, A_D_INNER + g * A_D_STATE:A_D_INNER + (g + 1) * A_D_STATE].astype(BF16)
        c_g = act_ref[:, A_D_INNER + A_GN + g * A_D_STATE:A_D_INNER + A_GN + (g + 1) * A_D_STATE].astype(BF16)
        x_g = act_ref[:, gsl]
        x_bf = x_g.astype(BF16)
        h_g = snew_ref[gsl, :]
        cb = _dot_nt(c_g, b_g)
        y = jnp.zeros((q, A_GW), F32)
        for r in range(A_HPG):
            h = g * A_HPG + r
            seg = cs[:, h:h + 1] - cs_t[h:h + 1, :]
            decay = jnp.exp(jnp.where(causal, seg, -1e30))
            wm = (cb * decay * dt_t[h:h + 1, :]).astype(BF16)
            y = y + _dot(wm, jnp.where(lane_head == r, x_bf, jnp.zeros_like(x_bf)))
        y = y + st_x[0:q, gsl] * _dot_nt(c_g, h_g.astype(BF16)) + dx_ref[:, gsl] * x_g
        yn_ref[:, gsl] = _gate_norm(y, z_ref[:, gsl].astype(F32), ng_ref[:, gsl]).astype(BF16)
        s_new = _dot_tn((x_g * st_x[q:2 * q, gsl]).astype(BF16), b_g)
        for r in range(A_HPG):
            h = g * A_HPG + r
            rsl = slice(g * A_GW + r * A_HEAD_DIM, g * A_GW + (r + 1) * A_HEAD_DIM)
            keep = jnp.exp(cs[q - 1:q, h:h + 1])
            snew_ref[rsl, :] = snew_ref[rsl, :] * keep + s_new[r * A_HEAD_DIM:(r + 1) * A_HEAD_DIM, :]


def _ssd_call(proj, dt_raw, conv_prev, ssm_prev, conv_w, conv_b, dt_bias, a_log, d_x, norm_g, nb, name):
    t = proj.shape[0]
    q = A_CHUNK
    nc = t // nb // q
    hp = A_N_HEADS * A_HEAD_DIM
    row = lambda b, c: (b * nc + c, 0)
    full = lambda shape: pl.BlockSpec(shape, lambda b, c: (0,) * len(shape))
    return pl.pallas_call(
        _ssd_body,
        grid=(nb, nc),
        in_specs=[pl.BlockSpec((q, A_D_INNER), lambda b, c: (b * nc + c, 0)),
                  pl.BlockSpec((q, A_D_INNER), lambda b, c: (b * nc + c, 1)),
                  pl.BlockSpec((q, 2 * A_GN), lambda b, c: (b * nc + c, 2)),
                  pl.BlockSpec((q, LANE), row),
                  pl.BlockSpec((None, A_CONV - 1, A_CONV_CH), lambda b, c: (b, 0, 0)),
                  pl.BlockSpec((None, hp, A_D_STATE), lambda b, c: (b, 0, 0)),
                  full((A_CONV, A_CONV_CH)), full((1, A_CONV_CH)), full((1, LANE)), full((1, LANE)),
                  full((1, A_D_INNER)), full((1, A_D_INNER)), full((LANE, A_D_INNER))],
        out_specs=[pl.BlockSpec((q, A_D_INNER), row),
                   pl.BlockSpec((None, A_CONV - 1, A_CONV_CH), lambda b, c: (b, 0, 0)),
                   pl.BlockSpec((None, hp, A_D_STATE), lambda b, c: (b, 0, 0))],
        out_shape=[jax.ShapeDtypeStruct((t, A_D_INNER), BF16),
                   jax.ShapeDtypeStruct((nb, A_CONV - 1, A_CONV_CH), F32),
                   jax.ShapeDtypeStruct((nb, hp, A_D_STATE), F32)],
        scratch_shapes=[pltpu.VMEM((8, A_CONV_CH), F32), pltpu.VMEM((q, A_CONV_CH), F32)],
        compiler_params=_cparams(("arbitrary", "arbitrary")),
        name=name,
    )(proj, proj, proj, dt_raw, conv_prev, ssm_prev, conv_w, conv_b, dt_bias, a_log, d_x, norm_g, _head_expand())


def _ssd_step_body(z_ref, xs_ref, bc_ref, dt_ref, cprev_ref, sprev_ref, cw_ref, cb_ref, dtb_ref, alog_ref,
                   dx_ref, ng_ref, exp_ref, yn_ref, cnew_ref, snew_ref, da_ref, y_ref):
    bt = z_ref.shape[0]
    cur = jnp.concatenate([xs_ref[...], bc_ref[...]], axis=1).astype(F32)
    conv = cb_ref[...] + cur * cw_ref[A_CONV - 1:A_CONV, :]
    for k in range(A_CONV - 1):
        conv = conv + cprev_ref[:, k, :] * cw_ref[k:k + 1, :]
    for k in range(A_CONV - 2):
        cnew_ref[:, k, :] = cprev_ref[:, k + 1, :]
    cnew_ref[:, A_CONV - 2, :] = cur
    act = _silu(conv)
    xs = act[:, 0:A_D_INNER]
    bm = act[:, A_D_INNER:A_D_INNER + A_GN]
    cm = act[:, A_D_INNER + A_GN:A_CONV_CH]
    dt = _softplus(dt_ref[...] + dtb_ref[...])
    da_ref[...] = jnp.exp(dt * (-jnp.exp(alog_ref[...])))
    dt_hi, dt_lo = _split2(dt)
    dt_x = _dot(dt_hi, exp_ref[...]) + _dot(dt_lo, exp_ref[...])
    xdt = xs * dt_x
    rows = lax.broadcasted_iota(I32, (bt, 1), 0)
    y_ref[...] = jnp.zeros_like(y_ref)

    def per_seq(j, carry):
        mine = rows == j
        xdt_j = jnp.where(mine, xdt, 0.0)
        da_j = da_ref[pl.ds(j, 1), :]
        for g in range(A_N_GROUPS):
            gsl = slice(g * A_GW, (g + 1) * A_GW)
            nsl = slice(g * A_D_STATE, (g + 1) * A_D_STATE)
            xh, xl = _split2(xdt_j[:, gsl])
            bh, bl = _split2(bm[:, nsl])
            outer = _dot_tn(xh, bh) + (_dot_tn(xl, bh) + _dot_tn(xh, bl))
            for r in range(A_HPG):
                h = g * A_HPG + r
                rsl = slice(r * A_HEAD_DIM, (r + 1) * A_HEAD_DIM)
                hsl = slice(g * A_GW + r * A_HEAD_DIM, g * A_GW + (r + 1) * A_HEAD_DIM)
                snew_ref[j, hsl, :] = sprev_ref[j, hsl, :] * da_j[:, h:h + 1] + outer[rsl, :]
            h_new = snew_ref[j, gsl, :]
            hh, hl = _split2(h_new)
            ch, cl = _split2(cm[:, nsl])
            yg = _dot_nt(ch, hh) + (_dot_nt(cl, hh) + _dot_nt(ch, hl))
            y_ref[:, gsl] = y_ref[:, gsl] + jnp.where(mine, yg, 0.0)
        return carry

    lax.fori_loop(0, bt, per_seq, 0)
    y = y_ref[...] + dx_ref[...] * xs
    z = z_ref[...].astype(F32)
    for g in range(A_N_GROUPS):
        gsl = slice(g * A_GW, (g + 1) * A_GW)
        yn_ref[:, gsl] = _gate_norm(y[:, gsl], z[:, gsl], ng_ref[:, gsl]).astype(BF16)


_SSD_STEP_INPUTS = 13


def _ssd_step_aliased_body(*refs):
    _ssd_step_body(*refs[:_SSD_STEP_INPUTS], *refs[_SSD_STEP_INPUTS + 1:])


def _ssd_step_call(proj, dt_raw, conv_prev, ssm_all, slot, ssm_out, conv_w, conv_b, dt_bias, a_log, d_x, norm_g,
                   name):
    nb = proj.shape[0]
    bt = 8
    hp = A_N_HEADS * A_HEAD_DIM
    full = lambda shape: pl.BlockSpec(shape, lambda i: (0,) * len(shape))
    state_spec = pl.BlockSpec((None, bt, hp, A_D_STATE), lambda i: (slot, i, 0, 0))
    in_specs = [pl.BlockSpec((bt, A_D_INNER), lambda i: (i, 0)),
                pl.BlockSpec((bt, A_D_INNER), lambda i: (i, 1)),
                pl.BlockSpec((bt, 2 * A_GN), lambda i: (i, 2)),
                pl.BlockSpec((bt, LANE), lambda i: (i, 0)),
                pl.BlockSpec((bt, A_CONV - 1, A_CONV_CH), lambda i: (i, 0, 0)),
                state_spec,
                full((A_CONV, A_CONV_CH)), full((1, A_CONV_CH)), full((1, LANE)), full((1, LANE)),
                full((1, A_D_INNER)), full((1, A_D_INNER)), full((LANE, A_D_INNER))]
    args = [proj, proj, proj, dt_raw, conv_prev, ssm_all, conv_w, conv_b, dt_bias, a_log, d_x, norm_g, _head_expand()]
    assert len(args) == _SSD_STEP_INPUTS
    aliases = {}
    if ssm_out is not None:
        in_specs.append(pl.BlockSpec(memory_space=pl.ANY))
        args.append(ssm_out)
        aliases = {_SSD_STEP_INPUTS: 2}
    return pl.pallas_call(
        _ssd_step_body if ssm_out is None else _ssd_step_aliased_body,
        grid=(nb // bt,),
        in_specs=in_specs,
        out_specs=[pl.BlockSpec((bt, A_D_INNER), lambda i: (i, 0)),
                   pl.BlockSpec((bt, A_CONV - 1, A_CONV_CH), lambda i: (i, 0, 0)),
                   state_spec],
        out_shape=[jax.ShapeDtypeStruct((nb, A_D_INNER), BF16),
                   jax.ShapeDtypeStruct((nb, A_CONV - 1, A_CONV_CH), F32),
                   jax.ShapeDtypeStruct(ssm_all.shape, F32)],
        scratch_shapes=[pltpu.VMEM((bt, LANE), F32), pltpu.VMEM((bt, A_D_INNER), F32)],
        input_output_aliases=aliases,
        compiler_params=_cparams(("arbitrary",)),
        name=name,
    )(*args)


def _gmlp_in_body(x_ref, g_ref, sc_ref, sh_ref, w_ref, b_ref, lg_ref, lb_ref, o_ref, hn_ref):
    j = pl.program_id(1)

    @pl.when(j == 0)
    def _():
        hn_ref[...] = _prenorm(x_ref[...], g_ref[...], sc_ref[...], sh_ref[...]).astype(BF16)

    uv = jax.nn.gelu(_dot(hn_ref[...], w_ref[...].astype(BF16)) + b_ref[...], approximate=True)

    @pl.when(j == 0)
    def _():
        o_ref[...] = uv.astype(o_ref.dtype)

    @pl.when(j == 1)
    def _():
        vc = uv - jnp.mean(uv, axis=-1, keepdims=True)
        var = jnp.mean(vc * vc, axis=-1, keepdims=True)
        o_ref[...] = (vc * lax.rsqrt(var + EPS) * lg_ref[...] + lb_ref[...]).astype(o_ref.dtype)


def _gmlp_in_call(x, g, mod, w, b, ln_g, ln_b, tm, out_dtype, name):
    t, d = x.shape
    return pl.pallas_call(
        _gmlp_in_body,
        grid=(t // tm, 2),
        in_specs=[pl.BlockSpec((tm, d), lambda i, j: (i, 0)),
                  pl.BlockSpec((1, d), lambda i, j: (0, 0)),
                  mod.spec(1, tm, 2), mod.spec(0, tm, 2),
                  pl.BlockSpec((d, B_D), lambda i, j: (0, j)),
                  pl.BlockSpec((1, B_D), lambda i, j: (0, j)),
                  pl.BlockSpec((1, B_D), lambda i, j: (0, 0)),
                  pl.BlockSpec((1, B_D), lambda i, j: (0, 0))],
        out_specs=pl.BlockSpec((tm, B_D), lambda i, j: (i, j)),
        out_shape=jax.ShapeDtypeStruct((t, 2 * B_D), out_dtype),
        scratch_shapes=[pltpu.VMEM((tm, d), BF16)],
        compiler_params=_cparams(("arbitrary", "arbitrary")),
        name=name,
    )(x, g.reshape(1, d), mod.arr, mod.arr, w, b.reshape(1, -1), ln_g.reshape(1, -1), ln_b.reshape(1, -1))


def _gmlp_out_body(u_ref, v_ref, ws_ref, bs_ref, w_ref, x_ref, gate_ref, o_ref, wbf_ref, wsbf_ref, m_ref):
    q = B_CHUNK

    @pl.when(pl.program_id(0) == 0)
    def _():
        wbf_ref[...] = w_ref[...].astype(BF16)
        causal = lax.broadcasted_iota(I32, (q, q), 0) >= lax.broadcasted_iota(I32, (q, q), 1)
        for g in range(B_N_GROUPS):
            wsbf_ref[g] = jnp.where(causal, ws_ref[g], 0.0).astype(BF16)

    for ci in range(u_ref.shape[0] // q):
        rsl = slice(ci * q, (ci + 1) * q)
        for g in range(B_N_GROUPS):
            gsl = slice(g * B_GROUP_DIM, (g + 1) * B_GROUP_DIM)
            mixed = _dot(wsbf_ref[g], v_ref[rsl, gsl].astype(BF16)) + bs_ref[:, g:g + 1]
            m_ref[rsl, gsl] = (u_ref[rsl, gsl].astype(F32) * mixed).astype(BF16)
    o_ref[...] = x_ref[...] + gate_ref[...] * _dot(m_ref[...], wbf_ref[...])


def _gmlp_out_call(uv, w_s, b_s, w_out, x, mod, tm, name):
    t, d = x.shape
    q = B_CHUNK
    return pl.pallas_call(
        _gmlp_out_body,
        grid=(t // tm,),
        in_specs=[pl.BlockSpec((tm, B_D), lambda i: (i, 0)),
                  pl.BlockSpec((tm, B_D), lambda i: (i, 1)),
                  pl.BlockSpec((B_N_GROUPS, q, q), lambda i: (0, 0, 0)),
                  pl.BlockSpec((q, B_N_GROUPS), lambda i: (0, 0)),
                  pl.BlockSpec((B_D, d), lambda i: (0, 0)),
                  pl.BlockSpec((tm, d), lambda i: (i, 0)),
                  mod.spec(2, tm, 1)],
        out_specs=pl.BlockSpec((tm, d), lambda i: (i, 0)),
        out_shape=jax.ShapeDtypeStruct((t, d), F32),
        scratch_shapes=[pltpu.VMEM((B_D, d), BF16), pltpu.VMEM((B_N_GROUPS, q, q), BF16),
                        pltpu.VMEM((tm, B_D), BF16)],
        compiler_params=_cparams(("arbitrary",)),
        name=name,
    )(uv, uv, w_s, b_s.T, w_out, x, mod.arr)


def _gmlp_out_step_body(u_ref, v_ref, wd_ref, bd_ref, w_ref, x_ref, gate_ref, o_ref):
    mixed = v_ref[...] * wd_ref[...] + bd_ref[...]
    m = (u_ref[...] * mixed).astype(BF16)
    o_ref[...] = x_ref[...] + gate_ref[...] * _dot(m, w_ref[...].astype(BF16))


def _gmlp_out_step_call(uv, w_s, b_s, w_out, x, mod, name):
    t, d = x.shape
    wd = jnp.repeat(w_s[:, 0, 0], B_GROUP_DIM).reshape(1, B_D)
    bd = jnp.repeat(b_s[:, 0], B_GROUP_DIM).reshape(1, B_D)
    return pl.pallas_call(
        _gmlp_out_step_body,
        grid=(1,),
        in_specs=[pl.BlockSpec((t, B_D), lambda i: (0, 0)),
                  pl.BlockSpec((t, B_D), lambda i: (0, 1)),
                  pl.BlockSpec((1, B_D), lambda i: (0, 0)),
                  pl.BlockSpec((1, B_D), lambda i: (0, 0)),
                  pl.BlockSpec((B_D, d), lambda i: (0, 0)),
                  pl.BlockSpec((t, d), lambda i: (0, 0)),
                  mod.spec(2, t, 1)],
        out_specs=pl.BlockSpec((t, d), lambda i: (0, 0)),
        out_shape=jax.ShapeDtypeStruct((t, d), F32),
        compiler_params=_cparams(("arbitrary",)),
        name=name,
    )(uv, uv, wd, bd, w_out, x, mod.arr)


def _pool_matmul(pooled, wg_ref):
    outs = []
    for gi in range(len(C_WINDOWS)):
        gsl = slice(gi * C_GROUP_DIM, (gi + 1) * C_GROUP_DIM)
        outs.append(_dot(pooled[:, gsl].astype(BF16), wg_ref[gi].astype(BF16)))
    return jnp.concatenate(outs, axis=-1)


def _pool_body(x_ref, g_ref, sc_ref, sh_ref, gate_ref, prev_ref, wg_ref, scale_ref, o_ref, pnew_ref, hp_ref,
               *, tiles_per_seq, start):
    i = pl.program_id(0)
    tm = x_ref.shape[0]
    top = 16
    ti = i % tiles_per_seq

    @pl.when(ti == 0)
    def _():
        hp_ref[top - C_STATE:top, :] = prev_ref[...]

    hn = _prenorm(x_ref[...], g_ref[...], sc_ref[...], sh_ref[...])
    hp_ref[top:top + tm, :] = hn
    pos = start + ti * tm + lax.broadcasted_iota(I32, (tm, 1), 0)
    outs = []
    for gi, w in enumerate(C_WINDOWS):
        gsl = slice(gi * C_GROUP_DIM, (gi + 1) * C_GROUP_DIM)
        acc = hn[:, gsl]
        for k in range(1, w):
            acc = acc + hp_ref[top - k:top - k + tm, gsl]
        cnt = jnp.minimum(pos + 1, w).astype(F32)
        outs.append(acc / cnt - hn[:, gsl])
    y = _pool_matmul(jnp.concatenate(outs, axis=-1), wg_ref) * scale_ref[...]
    o_ref[...] = x_ref[...] + gate_ref[...] * y
    hist = hp_ref[top + tm - C_STATE:top + tm, :]
    hp_ref[top - C_STATE:top, :] = hist

    @pl.when(ti == tiles_per_seq - 1)
    def _():
        pnew_ref[...] = hist


def _pool_call(x, g, mod, prev, w_g, scale, nb, tm, start, name):
    t, d = x.shape
    tiles = t // nb // tm
    ng = len(C_WINDOWS)
    return pl.pallas_call(
        functools.partial(_pool_body, tiles_per_seq=tiles, start=start),
        grid=(t // tm,),
        in_specs=[pl.BlockSpec((tm, d), lambda i: (i, 0)),
                  pl.BlockSpec((1, d), lambda i: (0, 0)),
                  mod.spec(1, tm, 1), mod.spec(0, tm, 1), mod.spec(2, tm, 1),
                  pl.BlockSpec((None, C_STATE, d), lambda i: (i // tiles, 0, 0)),
                  pl.BlockSpec((ng, C_GROUP_DIM, C_GROUP_DIM), lambda i: (0, 0, 0)),
                  pl.BlockSpec((1, d), lambda i: (0, 0))],
        out_specs=[pl.BlockSpec((tm, d), lambda i: (i, 0)),
                   pl.BlockSpec((None, C_STATE, d), lambda i: (i // tiles, 0, 0))],
        out_shape=[jax.ShapeDtypeStruct((t, d), F32), jax.ShapeDtypeStruct((nb, C_STATE, d), F32)],
        scratch_shapes=[pltpu.VMEM((16 + tm, d), F32)],
        compiler_params=_cparams(("arbitrary",)),
        name=name,
    )(x, g.reshape(1, d), mod.arr, mod.arr, mod.arr, prev, w_g, scale.reshape(1, d))


def _pool_step_body(x_ref, g_ref, sc_ref, sh_ref, gate_ref, prev_ref, wg_ref, scale_ref, o_ref, pnew_ref, *, start):
    hn = _prenorm(x_ref[...], g_ref[...], sc_ref[...], sh_ref[...])
    outs = []
    for gi, w in enumerate(C_WINDOWS):
        gsl = slice(gi * C_GROUP_DIM, (gi + 1) * C_GROUP_DIM)
        acc = hn[:, gsl]
        for k in range(1, w):
            acc = acc + prev_ref[:, C_STATE - k, gsl]
        outs.append(acc / float(min(start + 1, w)) - hn[:, gsl])
    y = _pool_matmul(jnp.concatenate(outs, axis=-1), wg_ref) * scale_ref[...]
    o_ref[...] = x_ref[...] + gate_ref[...] * y
    for k in range(C_STATE - 1):
        pnew_ref[:, k, :] = prev_ref[:, k + 1, :]
    pnew_ref[:, C_STATE - 1, :] = hn


def _pool_step_call(x, g, mod, prev, w_g, scale, start, name):
    t, d = x.shape
    bt = 32
    ng = len(C_WINDOWS)
    return pl.pallas_call(
        functools.partial(_pool_step_body, start=start),
        grid=(t // bt,),
        in_specs=[pl.BlockSpec((bt, d), lambda i: (i, 0)),
                  pl.BlockSpec((1, d), lambda i: (0, 0)),
                  mod.spec(1, bt, 1), mod.spec(0, bt, 1), mod.spec(2, bt, 1),
                  pl.BlockSpec((bt, C_STATE, d), lambda i: (i, 0, 0)),
                  pl.BlockSpec((ng, C_GROUP_DIM, C_GROUP_DIM), lambda i: (0, 0, 0)),
                  pl.BlockSpec((1, d), lambda i: (0, 0))],
        out_specs=[pl.BlockSpec((bt, d), lambda i: (i, 0)),
                   pl.BlockSpec((bt, C_STATE, d), lambda i: (i, 0, 0))],
        out_shape=[jax.ShapeDtypeStruct((t, d), F32), jax.ShapeDtypeStruct((t, C_STATE, d), F32)],
        compiler_params=_cparams(("arbitrary",)),
        name=name,
    )(x, g.reshape(1, d), mod.arr, mod.arr, mod.arr, prev, w_g, scale.reshape(1, d))


def _route_rows(s, b):
    npg = EXPERTS_PER_GROUP
    gscore = []
    for q in range(N_EXPERT_GROUPS):
        v = b[q * npg:(q + 1) * npg]
        best = None
        for i in range(npg):
            for j in range(i + 1, npg):
                best = v[i] + v[j] if best is None else jnp.maximum(best, v[i] + v[j])
        gscore.append(best)
    gsel = jnp.zeros_like(gscore[0], dtype=I32)
    gbest = gscore[0]
    for q in range(1, N_EXPERT_GROUPS):
        better = gscore[q] > gbest
        gsel = jnp.where(better, q, gsel)
        gbest = jnp.where(better, gscore[q], gbest)
    vb, vs = [], []
    for k in range(npg):
        bk, sk = b[k], s[k]
        for q in range(1, N_EXPERT_GROUPS):
            bk = jnp.where(gsel == q, b[q * npg + k], bk)
            sk = jnp.where(gsel == q, s[q * npg + k], sk)
        vb.append(bk)
        vs.append(sk)
    i1 = jnp.zeros_like(gsel)
    m1 = vb[0]
    for k in range(1, npg):
        better = vb[k] > m1
        i1 = jnp.where(better, k, i1)
        m1 = jnp.where(better, vb[k], m1)
    i2 = jnp.full_like(gsel, -1)
    m2 = jnp.zeros_like(m1)
    for k in range(npg):
        better = (i1 != k) & ((i2 < 0) | (vb[k] > m2))
        i2 = jnp.where(better, k, i2)
        m2 = jnp.where(better, vb[k], m2)
    s1 = vs[0]
    s2 = vs[0]
    for k in range(1, npg):
        s1 = jnp.where(i1 == k, vs[k], s1)
        s2 = jnp.where(i2 == k, vs[k], s2)
    w1 = s1 / (s1 + s2)
    w2 = s2 / (s1 + s2)
    lo = jnp.minimum(i1, i2)
    hi = jnp.maximum(i1, i2)
    pair = jnp.zeros_like(gsel)
    for p, (a, c) in enumerate(PAIRS):
        pair = jnp.where((lo == a) & (hi == c), p, pair)
    first_is_lo = i1 < i2
    return (gsel * len(PAIRS) + pair, jnp.where(first_is_lo, w1, w2), jnp.where(first_is_lo, w2, w1))


def _route_body(x_ref, g_ref, sc_ref, sh_ref, rw_ref, rb_ref, cnt_in_ref, *rest, aliased):
    if aliased:
        rest = rest[3:]
    rows_ref, bucket_ref, rank_ref, cnt_ref = rest
    tm = x_ref.shape[0]

    @pl.when(pl.program_id(0) == 0)
    def _():
        cnt_ref[...] = cnt_in_ref[...]

    hn = _prenorm(x_ref[...], g_ref[...], sc_ref[...], sh_ref[...])
    logits_t = _dot3(hn, rw_ref[...]).T
    scores = _sigmoid(logits_t[0:N_EXPERTS, :])
    biased = scores + rb_ref[0:N_EXPERTS, :]
    bucket, w_lo, w_hi = _route_rows([scores[e:e + 1, :] for e in range(N_EXPERTS)],
                                     [biased[e:e + 1, :] for e in range(N_EXPERTS)])
    bucket_ref[...] = bucket
    onehot = (lax.broadcasted_iota(I32, (BUCKET_ROWS, tm), 0) == bucket).astype(F32)
    before = (lax.broadcasted_iota(I32, (tm, tm), 0) < lax.broadcasted_iota(I32, (tm, tm), 1)).astype(BF16)
    earlier = _dot(onehot.astype(BF16), before) + cnt_ref[:, 0:1]
    rank_ref[...] = jnp.sum(onehot * earlier, axis=0, keepdims=True).astype(I32)
    cnt_ref[...] = cnt_ref[...] + jnp.sum(onehot, axis=1, keepdims=True)

    lo_bits = lax.bitcast_convert_type(w_lo, I32)
    hi_bits = lax.bitcast_convert_type(w_hi, I32)
    parts = [lax.shift_right_logical(lo_bits, 16), lo_bits & 0xFFFF,
             lax.shift_right_logical(hi_bits, 16), hi_bits & 0xFFFF]
    sub = lax.broadcasted_iota(I32, (LANE, tm), 0)
    gate_t = jnp.zeros((LANE, tm), F32)
    for k, part in enumerate(parts):
        gate_t = jnp.where(sub == k, part.astype(F32), gate_t)
    gate_words = gate_t.T.astype(I32)
    words = lax.bitcast_convert_type(hn.astype(BF16).astype(F32), I32)
    for j in range(ROW_PLANES):
        plane = words[:, j * LANE:(j + 1) * LANE]
        rows_ref[pl.ds(j, tm, stride=ROW_PLANES), :] = (plane | gate_words) if j == 0 else plane


def _route_call(x, g, mod, router_w, router_b, counts, tm, total_rows, row_offset, prior, name):
    t, d = x.shape
    blk0 = row_offset // tm
    rw = jnp.pad(router_w, ((0, 0), (0, LANE - N_EXPERTS)))
    rb = jnp.pad(router_b.reshape(-1, 1), ((0, LANE - N_EXPERTS), (0, 0)))
    in_specs = [pl.BlockSpec((tm, d), lambda i: (i, 0)),
                pl.BlockSpec((1, d), lambda i: (0, 0)),
                mod.spec(4, tm, 1), mod.spec(3, tm, 1),
                pl.BlockSpec((d, LANE), lambda i: (0, 0)),
                pl.BlockSpec((LANE, 1), lambda i: (0, 0)),
                pl.BlockSpec((BUCKET_ROWS, LANE), lambda i: (0, 0))]
    args = [x, g.reshape(1, d), mod.arr, mod.arr, rw, rb, counts]
    aliases = {}
    if prior is not None:
        in_specs += [pl.BlockSpec(memory_space=pl.ANY)] * 3
        aliases = {len(args): 0, len(args) + 1: 1, len(args) + 2: 2}
        args += list(prior)
    return pl.pallas_call(
        functools.partial(_route_body, aliased=prior is not None),
        grid=(t // tm,),
        in_specs=in_specs,
        out_specs=[pl.BlockSpec((tm * ROW_PLANES, LANE), lambda i: (blk0 + i, 0)),
                   pl.BlockSpec((1, tm), lambda i: (0, blk0 + i)),
                   pl.BlockSpec((1, tm), lambda i: (0, blk0 + i)),
                   pl.BlockSpec((BUCKET_ROWS, LANE), lambda i: (0, 0))],
        out_shape=[jax.ShapeDtypeStruct((total_rows * ROW_PLANES, LANE), I32),
                   jax.ShapeDtypeStruct((1, total_rows), I32),
                   jax.ShapeDtypeStruct((1, total_rows), I32),
                   jax.ShapeDtypeStruct((BUCKET_ROWS, LANE), F32)],
        input_output_aliases=aliases,
        compiler_params=_cparams(("arbitrary",)),
        name=name,
    )(*args)


def _invert_body(dest_ref, gather_ref, scatter_ref, *, total):
    def init(s, carry):
        gather_ref[s] = 0
        scatter_ref[s] = total + s % EXPERT_TILE
        return carry

    lax.fori_loop(0, gather_ref.shape[0], init, 0)

    def put(t, carry):
        d = dest_ref[t]
        gather_ref[d] = t
        scatter_ref[d] = t
        return carry

    lax.fori_loop(0, total, put, 0)


def _invert_call(dest, n_slots, name):
    total = dest.shape[0]
    smem = pl.BlockSpec(memory_space=pltpu.SMEM)
    return pl.pallas_call(
        functools.partial(_invert_body, total=total),
        in_specs=[smem],
        out_specs=[smem, smem],
        out_shape=[jax.ShapeDtypeStruct((n_slots,), I32)] * 2,
        name=name,
    )(dest)


def _experts_body(elo_ref, ehi_ref, fresh_ref, valid_ref, gather_ref, scatter_ref, rows_ref, wg0_ref, wg1_ref,
                  wu0_ref, wu1_ref, wd0_ref, wd1_ref, y_ref, wg_ref, wu_ref, wd_ref, xb_ref, yb_ref, gsem, ssem,
                  *, token_words):
    i = pl.program_id(0)
    n = pl.num_programs(0)
    f = D_EXPERT
    tile = EXPERT_TILE
    words = tile * ROW_PLANES

    def start_gather(step, slot):
        def body(r, carry):
            tok = gather_ref[step * tile + r]
            pltpu.make_async_copy(rows_ref.at[pl.ds(pl.multiple_of(tok * ROW_PLANES, ROW_PLANES), ROW_PLANES)],
                                  xb_ref.at[slot, pl.ds(r * ROW_PLANES, ROW_PLANES)], gsem.at[slot]).start()
            return carry

        lax.fori_loop(0, tile, body, 0, unroll=8)

    def start_scatter(step, slot):
        def body(r, carry):
            tok = scatter_ref[step * tile + r]
            pltpu.make_async_copy(yb_ref.at[slot, pl.ds(r * ROW_PLANES, ROW_PLANES)],
                                  y_ref.at[pl.ds(pl.multiple_of(tok * ROW_PLANES, ROW_PLANES), ROW_PLANES)],
                                  ssem.at[slot]).start()
            return carry

        lax.fori_loop(0, tile, body, 0, unroll=8)

    def wait_gather(slot):
        pltpu.make_async_copy(rows_ref.at[pl.ds(0, words)], xb_ref.at[slot], gsem.at[slot]).wait()

    def wait_scatter(slot):
        pltpu.make_async_copy(yb_ref.at[slot], y_ref.at[pl.ds(0, words)], ssem.at[slot]).wait()

    @pl.when(i == 0)
    def _():
        yb_ref[1] = jnp.zeros((words, LANE), F32)
        spare = pltpu.make_async_copy(yb_ref.at[1], y_ref.at[pl.ds(token_words, words)], ssem.at[1])
        spare.start()
        spare.wait()

    @pl.when(fresh_ref[i] == 1)
    def _():
        wg_ref[:, 0:f] = wg0_ref[...].astype(BF16)
        wg_ref[:, f:2 * f] = wg1_ref[...].astype(BF16)
        wu_ref[:, 0:f] = wu0_ref[...].astype(BF16)
        wu_ref[:, f:2 * f] = wu1_ref[...].astype(BF16)
        wd_ref[0:f, :] = wd0_ref[...].astype(BF16)
        wd_ref[f:2 * f, :] = wd1_ref[...].astype(BF16)

    nxt = jnp.minimum(i + 1, n - 1)
    more = jnp.logical_and(i + 1 < n, valid_ref[nxt] == 1)

    @pl.when(valid_ref[i] == 1)
    def _():
        slot = i % 2

        @pl.when(i == 0)
        def _():
            start_gather(0, 0)

        @pl.when(more)
        def _():
            start_gather(i + 1, 1 - slot)

        wait_gather(slot)
        planes = [xb_ref[slot, pl.ds(j, tile, stride=ROW_PLANES), :] for j in range(ROW_PLANES)]
        x = jnp.concatenate([lax.bitcast_convert_type(p & jnp.int32(-65536), F32).astype(BF16) for p in planes],
                            axis=-1)
        gw = planes[0] & jnp.int32(0xFFFF)
        w_lo = lax.bitcast_convert_type((gw[:, 0:1] << 16) | gw[:, 1:2], F32)
        w_hi = lax.bitcast_convert_type((gw[:, 2:3] << 16) | gw[:, 3:4], F32)
        act = _silu(_dot(x, wg_ref[...])) * _dot(x, wu_ref[...])
        lane = lax.broadcasted_iota(I32, act.shape, 1)
        act = act * jnp.where(lane < f, w_lo, w_hi)
        y = _dot(act.astype(BF16), wd_ref[...])

        @pl.when(i >= 2)
        def _():
            wait_scatter(slot)

        for j in range(ROW_PLANES):
            yb_ref[slot, pl.ds(j, tile, stride=ROW_PLANES), :] = y[:, j * LANE:(j + 1) * LANE]
        start_scatter(i, slot)

        @pl.when(jnp.logical_not(more))
        def _():
            @pl.when(i >= 1)
            def _():
                wait_scatter(1 - slot)

            wait_scatter(slot)


def _experts_call(e_lo, e_hi, fresh, valid, gather_idx, scatter_idx, rows, w_gate, w_up, w_down, layer, name):
    n_tiles = e_lo.shape[0]
    tile = EXPERT_TILE
    d, f = D_MODEL, D_EXPERT
    lo = lambda i, elo, ehi, *_: (layer, elo[i], 0, 0)
    hi = lambda i, elo, ehi, *_: (layer, ehi[i], 0, 0)
    return pl.pallas_call(
        functools.partial(_experts_body, token_words=rows.shape[0]),
        grid_spec=pltpu.PrefetchScalarGridSpec(
            num_scalar_prefetch=6,
            grid=(n_tiles,),
            in_specs=[pl.BlockSpec(memory_space=pl.ANY),
                      pl.BlockSpec((None, None, d, f), lo), pl.BlockSpec((None, None, d, f), hi),
                      pl.BlockSpec((None, None, d, f), lo), pl.BlockSpec((None, None, d, f), hi),
                      pl.BlockSpec((None, None, f, d), lo), pl.BlockSpec((None, None, f, d), hi)],
            out_specs=pl.BlockSpec(memory_space=pl.ANY),
            scratch_shapes=[pltpu.VMEM((d, 2 * f), BF16), pltpu.VMEM((d, 2 * f), BF16),
                            pltpu.VMEM((2 * f, d), BF16),
                            pltpu.VMEM((2, tile * ROW_PLANES, LANE), I32),
                            pltpu.VMEM((2, tile * ROW_PLANES, LANE), F32),
                            pltpu.SemaphoreType.DMA((2,)), pltpu.SemaphoreType.DMA((2,))]),
        out_shape=jax.ShapeDtypeStruct((rows.shape[0] + tile * ROW_PLANES, LANE), F32),
        compiler_params=_cparams(("arbitrary",)),
        name=name,
    )(e_lo, e_hi, fresh, valid, gather_idx, scatter_idx, rows, w_gate, w_gate, w_up, w_up, w_down, w_down)


def _moe_res_body(y_ref, x_ref, gate_ref, fg_ref, o_ref, *rest, final):
    tm = x_ref.shape[0]
    y = jnp.concatenate([y_ref[pl.ds(j, tm, stride=ROW_PLANES), :] for j in range(ROW_PLANES)], axis=-1)
    xn = x_ref[...] + gate_ref[...] * y
    o_ref[...] = xn
    if final:
        ms = jnp.mean(xn * xn, axis=-1, keepdims=True)
        rest[0][...] = xn * lax.rsqrt(ms + EPS) * fg_ref[...]


def _moe_res_call(y_tok, x, mod, final_g, tm, row_offset, final, name):
    t, d = x.shape
    blk0 = row_offset // tm
    n_out = 2 if final else 1
    return pl.pallas_call(
        functools.partial(_moe_res_body, final=final),
        grid=(t // tm,),
        in_specs=[pl.BlockSpec((tm * ROW_PLANES, LANE), lambda i: (blk0 + i, 0)),
                  pl.BlockSpec((tm, d), lambda i: (i, 0)),
                  mod.spec(5, tm, 1),
                  pl.BlockSpec((1, d), lambda i: (0, 0))],
        out_specs=[pl.BlockSpec((tm, d), lambda i: (i, 0))] * n_out,
        out_shape=[jax.ShapeDtypeStruct((t, d), F32)] * n_out,
        compiler_params=_cparams(("arbitrary",)),
        name=name,
    )(y_tok, x, mod.arr, final_g.reshape(1, d))


def _tile_tables(counts, n_tiles):
    tile = EXPERT_TILE
    tiles_per_bucket = (counts + tile - 1) // tile
    ends = jnp.cumsum(tiles_per_bucket)
    starts = ends - tiles_per_bucket
    used = ends[-1]
    ti = jnp.arange(n_tiles, dtype=I32)
    valid = (ti < used).astype(I32)
    tile_bucket = jnp.sum((jnp.minimum(ti, used - 1)[:, None] >= ends[None, :]).astype(I32), axis=1)
    pair_lo = jnp.asarray([p[0] for p in PAIRS], I32)
    pair_hi = jnp.asarray([p[1] for p in PAIRS], I32)
    grp = tile_bucket // len(PAIRS)
    e_lo = grp * EXPERTS_PER_GROUP + pair_lo[tile_bucket % len(PAIRS)]
    e_hi = grp * EXPERTS_PER_GROUP + pair_hi[tile_bucket % len(PAIRS)]
    fresh = jnp.concatenate([jnp.ones((1,), I32), (tile_bucket[1:] != tile_bucket[:-1]).astype(I32)])
    return e_lo, e_hi, fresh, valid, (starts * tile).astype(I32)


def _moe_layer(xp, xs, norm_g2, mod_p, mod_s, router_w, router_b, w_gate, w_up, w_down, layer, final_g, final):
    tp, ts = xp.shape[0], xs.shape[0]
    total = tp + ts
    n_tiles = -(-total // EXPERT_TILE) + N_BUCKETS
    zero_counts = jnp.zeros((BUCKET_ROWS, LANE), F32)
    rows, bucket, rank, counts = _route_call(xp, norm_g2, mod_p, router_w, router_b, zero_counts, 512, total, 0,
                                             None, "route_prompt")
    rows, bucket, rank, counts = _route_call(xs, norm_g2, mod_s, router_w, router_b, counts, ts, total, tp,
                                             (rows, bucket, rank), "route_sample")
    e_lo, e_hi, fresh, valid, bucket_start = _tile_tables(counts[:N_BUCKETS, 0].astype(I32), n_tiles)
    dest = bucket_start[bucket[0]] + rank[0]
    gather_idx, scatter_idx = _invert_call(dest, n_tiles * EXPERT_TILE, "invert")
    y_tok = _experts_call(e_lo, e_hi, fresh, valid, gather_idx, scatter_idx, rows, w_gate, w_up, w_down, layer,
                          "experts")
    outp = _moe_res_call(y_tok, xp, mod_p, final_g, 512, 0, final, "moe_res_prompt")
    outs = _moe_res_call(y_tok, xs, mod_s, final_g, ts, tp, final, "moe_res_sample")
    return outp, outs


def _mamba_layer(x, g, mod, conv_prev, ssm, w_in_all, slot, conv_w, conv_b, dt_bias, a_log, d_skip, norm_g, w_out,
                 tm, tag):
    w_dt = jnp.pad(w_in_all[slot, :, A_ZX:], ((0, 0), (0, LANE - A_N_HEADS)))
    pad_h = lambda v: jnp.pad(v.reshape(1, -1), ((0, 0), (0, LANE - A_N_HEADS)))
    proj = _norm_mm_call(x, g, mod, 1, 0, w_in_all, slot, A_ZX, tm, 512, BF16, "a_in_" + tag)
    dt_raw = _norm_mm3_call(x, g, mod, 1, 0, w_dt, tm, "a_dt_" + tag)
    d_x = jnp.repeat(d_skip, A_HEAD_DIM).reshape(1, A_D_INNER)
    weights = (conv_w, conv_b.reshape(1, -1), pad_h(dt_bias), pad_h(a_log), d_x, norm_g.reshape(1, -1))
    if ssm[0] == "step":
        yn, conv_new, ssm_new = _ssd_step_call(proj, dt_raw, conv_prev, ssm[1], ssm[2], ssm[3], *weights,
                                               "ssd_step_" + tag)
    else:
        yn, conv_new, ssm_new = _ssd_call(proj, dt_raw, conv_prev, ssm[2], *weights, ssm[1], "ssd_" + tag)
    x = _out_res_call(yn, w_out, x, mod, 2, tm, "a_out_" + tag)
    return x, conv_new, ssm_new


def kernel(x_prompt, x_sample, c_prompt, c_sample, state_a_conv, state_a_ssm, state_c_pool, w_mod, b_mod, norm_g, final_g, a_w_in, a_conv_w, a_conv_b, a_dt_bias, a_log, a_d, a_norm_g, a_w_out, b_w_in, b_b_in, b_ln_g, b_ln_b, b_w_s, b_b_s, b_w_out, c_w_g, c_scale, router_w, router_b, e_w_gate, e_w_up, e_w_down):
    bp, seq, d = x_prompt.shape
    bs = x_sample.shape[0]
    n_a, n_c = state_a_conv.shape[0], state_c_pool.shape[0]
    mod_all = _mod_call(jnp.concatenate([c_prompt, c_sample], axis=0), w_mod, b_mod)
    mod_p_arr = mod_all[:, :bp].reshape(DEPTH, bp, 6, 1, d)
    mod_s_arr = mod_all[:, bp:]
    xp = x_prompt.reshape(bp * seq, d)
    xs = x_sample.reshape(bs, d)
    conv_p, ssm_p, pool_p, conv_s, pool_s, v_s = [], [], [], [], [], []
    hp = A_N_HEADS * A_HEAD_DIM
    ssm_s_in = state_a_ssm.reshape(n_a, bs, hp, A_D_STATE)
    ssm_s_out = None
    yp = ys = None
    for i in range(DEPTH):
        kind, s = LAYER_KIND[i], LAYER_SLOT[i]
        mod_p = Mod(mod_p_arr, i, False, seq)
        mod_s = Mod(mod_s_arr, i, True)
        g1 = norm_g[i, 0]
        if kind == 0:
            weights = (a_w_in, s, a_conv_w[s], a_conv_b[s], a_dt_bias[s], a_log[s], a_d[s], a_norm_g[s], a_w_out[s])
            conv0 = jnp.zeros((bp, A_CONV - 1, A_CONV_CH), F32)
            ssm0 = jnp.zeros((bp, hp, A_D_STATE), F32)
            xp, cv, ss = _mamba_layer(xp, g1, mod_p, conv0, ("prompt", bp, ssm0), *weights, 1024, "p%d" % i)
            conv_p.append(cv)
            ssm_p.append(ss.reshape(bp, A_N_HEADS, A_HEAD_DIM, A_D_STATE))
            xs, cv, ssm_s_out = _mamba_layer(xs, g1, mod_s, state_a_conv[s], ("step", ssm_s_in, s, ssm_s_out),
                                             *weights, bs, "s%d" % i)
            conv_s.append(cv)
        elif kind == 1:
            uv = _gmlp_in_call(xp, g1, mod_p, b_w_in[s], b_b_in[s], b_ln_g[s], b_ln_b[s], 512, BF16, "b_in_p%d" % i)
            xp = _gmlp_out_call(uv, b_w_s[s], b_b_s[s], b_w_out[s], xp, mod_p, 512, "b_out_p%d" % i)
            uv = _gmlp_in_call(xs, g1, mod_s, b_w_in[s], b_b_in[s], b_ln_g[s], b_ln_b[s], bs, F32, "b_in_s%d" % i)
            xs = _gmlp_out_step_call(uv, b_w_s[s], b_b_s[s], b_w_out[s], xs, mod_s, "b_out_s%d" % i)
            v_s.append(uv[:, B_D:].reshape(bs, 1, B_D))
        else:
            pool0 = jnp.zeros((bp, C_STATE, d), F32)
            xp, pr = _pool_call(xp, g1, mod_p, pool0, c_w_g[s], c_scale[s], bp, 512, 0, "pool_p%d" % i)
            pool_p.append(pr)
            xs, pr = _pool_step_call(xs, g1, mod_s, state_c_pool[s], c_w_g[s], c_scale[s], PAST_LEN, "pool_s%d" % i)
            pool_s.append(pr)
        final = i == DEPTH - 1
        outp, outs = _moe_layer(xp, xs, norm_g[i, 1], mod_p, mod_s, router_w, router_b, e_w_gate, e_w_up, e_w_down, i,
                                final_g, final)
        xp, xs = outp[0], outs[0]
        if final:
            yp, ys = outp[1], outs[1]
    return (yp.reshape(bp, seq, d), ys.reshape(bs, 1, d), jnp.stack(conv_p), jnp.stack(ssm_p), jnp.stack(pool_p),
            jnp.stack(conv_s), ssm_s_out.reshape(state_a_ssm.shape), jnp.stack(pool_s), jnp.stack(v_s))
```

```python
import functools
import math

import numpy as np
import jax
import jax.numpy as jnp
from jax import lax
from jax.experimental import pallas as pl
from jax.experimental.pallas import tpu as pltpu

F32 = jnp.float32
BF16 = jnp.bfloat16
I32 = jnp.int32
EPS = 1e-6

LANE = 128
D_MODEL = 1024
DEPTH = 4
PAST_LEN = 16384
LAYER_KIND = (0, 1, 2, 0)
LAYER_SLOT = (0, 0, 0, 1)
A_D_INNER = 2 * D_MODEL
A_HEAD_DIM = 64
A_N_HEADS = A_D_INNER // A_HEAD_DIM
A_N_GROUPS = 8
A_HPG = A_N_HEADS // A_N_GROUPS
A_D_STATE = 128
A_GN = A_N_GROUPS * A_D_STATE
A_CONV = 4
A_CONV_CH = A_D_INNER + 2 * A_GN
A_ZX = A_D_INNER + A_CONV_CH
A_CHUNK = 128
A_GW = A_HPG * A_HEAD_DIM
B_D = 2 * D_MODEL
B_N_GROUPS = 8
B_GROUP_DIM = B_D // B_N_GROUPS
B_CHUNK = 128
C_WINDOWS = (2, 4, 8, 16)
C_GROUP_DIM = D_MODEL // len(C_WINDOWS)
C_STATE = max(C_WINDOWS) - 1
N_EXPERTS = 16
N_EXPERT_GROUPS = 4
EXPERTS_PER_GROUP = 4
D_EXPERT = D_MODEL // 2
PAIRS = ((0, 1), (0, 2), (0, 3), (1, 3), (1, 2), (2, 3))
N_BUCKETS = N_EXPERT_GROUPS * len(PAIRS)
BUCKET_ROWS = 32
ROW_PLANES = D_MODEL // LANE
OUT_PLANES = 2 * ROW_PLANES
EXPERT_TILE = 256
VMEM_LIMIT = 56 * 1024 * 1024


def _cparams(sem, vmem=VMEM_LIMIT):
    return pltpu.CompilerParams(dimension_semantics=sem, vmem_limit_bytes=vmem)


def _sigmoid(x):
    return 1.0 / (1.0 + jnp.exp(-x))


def _silu(x):
    return x * _sigmoid(x)


def _softplus(x):
    return jnp.maximum(x, 0.0) + jnp.log1p(jnp.exp(-jnp.abs(x)))


def _split2(a):
    hi = a.astype(BF16)
    lo = (a - hi.astype(F32)).astype(BF16)
    return hi, lo


def _split3(a):
    hi = a.astype(BF16)
    r = a - hi.astype(F32)
    mid = r.astype(BF16)
    lo = (r - mid.astype(F32)).astype(BF16)
    return hi, mid, lo


def _dot(a, b):
    return jnp.dot(a, b, preferred_element_type=F32)


def _dot_nt(a, b):
    return lax.dot_general(a, b, (((1,), (1,)), ((), ())), preferred_element_type=F32)


def _dot_tn(a, b):
    return lax.dot_general(a, b, (((0,), (0,)), ((), ())), preferred_element_type=F32)


def _dot3(a, b):
    a_hi, a_lo = _split2(a)
    b_hi, b_lo = _split2(b)
    return _dot(a_hi, b_hi) + (_dot(a_lo, b_hi) + _dot(a_hi, b_lo))


def _prenorm(x, g, sc, sh):
    ms = jnp.mean(x * x, axis=-1, keepdims=True)
    return (x * lax.rsqrt(ms + EPS) * g) * (1.0 + sc) + sh


def _mod_body(c_ref, w_ref, b_ref, o_ref):
    o_ref[...] = _dot3(_silu(c_ref[...]), w_ref[...]) + b_ref[...]


def _mod_call(c_all, w_mod, b_mod):
    nb, d = c_all.shape
    depth, _, n = w_mod.shape
    tn = 1536
    return pl.pallas_call(
        _mod_body,
        grid=(depth, n // tn),
        in_specs=[pl.BlockSpec((nb, d), lambda i, j: (0, 0)),
                  pl.BlockSpec((None, d, tn), lambda i, j: (i, 0, j)),
                  pl.BlockSpec((None, 1, tn), lambda i, j: (i, 0, j))],
        out_specs=pl.BlockSpec((None, nb, tn), lambda i, j: (i, 0, j)),
        out_shape=jax.ShapeDtypeStruct((depth, nb, n), F32),
        compiler_params=_cparams(("arbitrary", "arbitrary")),
        name="mod",
    )(c_all, w_mod, b_mod.reshape(depth, 1, n))


class Mod:
    def __init__(self, arr, layer, per_row, rows_per_seq=None):
        self.arr, self.layer, self.per_row, self.rows_per_seq = arr, layer, per_row, rows_per_seq

    def spec(self, which, tm, ngrid):
        layer = self.layer
        if self.per_row:
            if ngrid == 1:
                return pl.BlockSpec((None, tm, D_MODEL), lambda i: (layer, i, which))
            return pl.BlockSpec((None, tm, D_MODEL), lambda i, j: (layer, i, which))
        tiles = self.rows_per_seq // tm
        if ngrid == 1:
            return pl.BlockSpec((None, None, None, 1, D_MODEL), lambda i: (layer, i // tiles, which, 0, 0))
        return pl.BlockSpec((None, None, None, 1, D_MODEL), lambda i, j: (layer, i // tiles, which, 0, 0))


def _norm_mm_body(x_ref, g_ref, sc_ref, sh_ref, w_ref, o_ref, hn_ref):
    @pl.when(pl.program_id(1) == 0)
    def _():
        hn_ref[...] = _prenorm(x_ref[...], g_ref[...], sc_ref[...], sh_ref[...]).astype(BF16)

    o_ref[...] = _dot(hn_ref[...], w_ref[...]).astype(o_ref.dtype)


def _norm_mm_call(x, g, mod, which_sc, which_sh, w, tm, tn, out_dtype, name):
    t, d = x.shape
    n_cols = w.shape[1]
    return pl.pallas_call(
        _norm_mm_body,
        grid=(t // tm, n_cols // tn),
        in_specs=[pl.BlockSpec((tm, d), lambda i, j: (i, 0)),
                  pl.BlockSpec((1, d), lambda i, j: (0, 0)),
                  mod.spec(which_sc, tm, 2), mod.spec(which_sh, tm, 2),
                  pl.BlockSpec((d, tn), lambda i, j: (0, j))],
        out_specs=pl.BlockSpec((tm, tn), lambda i, j: (i, j)),
        out_shape=jax.ShapeDtypeStruct((t, n_cols), out_dtype),
        scratch_shapes=[pltpu.VMEM((tm, d), BF16)],
        compiler_params=_cparams(("arbitrary", "arbitrary")),
        name=name,
    )(x, g.reshape(1, d), mod.arr, mod.arr, w)


def _norm_mm3_body(x_ref, g_ref, sc_ref, sh_ref, w_ref, o_ref):
    hn = _prenorm(x_ref[...], g_ref[...], sc_ref[...], sh_ref[...])
    o_ref[...] = _dot3(hn, w_ref[...])


def _norm_mm3_call(x, g, mod, which_sc, which_sh, w, tm, name):
    t, d = x.shape
    n = w.shape[1]
    return pl.pallas_call(
        _norm_mm3_body,
        grid=(t // tm,),
        in_specs=[pl.BlockSpec((tm, d), lambda i: (i, 0)),
                  pl.BlockSpec((1, d), lambda i: (0, 0)),
                  mod.spec(which_sc, tm, 1), mod.spec(which_sh, tm, 1),
                  pl.BlockSpec((d, n), lambda i: (0, 0))],
        out_specs=pl.BlockSpec((tm, n), lambda i: (i, 0)),
        out_shape=jax.ShapeDtypeStruct((t, n), F32),
        compiler_params=_cparams(("arbitrary",)),
        name=name,
    )(x, g.reshape(1, d), mod.arr, mod.arr, w)


def _out_res_body(y_ref, w_ref, x_ref, gate_ref, o_ref, wbf_ref):
    @pl.when(pl.program_id(0) == 0)
    def _():
        wbf_ref[...] = w_ref[...].astype(BF16)

    o_ref[...] = x_ref[...] + gate_ref[...] * _dot(y_ref[...], wbf_ref[...])


def _out_res_call(y, w, x, mod, which_gate, tm, name):
    t, k = y.shape
    d = x.shape[1]
    return pl.pallas_call(
        _out_res_body,
        grid=(t // tm,),
        in_specs=[pl.BlockSpec((tm, k), lambda i: (i, 0)),
                  pl.BlockSpec((k, d), lambda i: (0, 0)),
                  pl.BlockSpec((tm, d), lambda i: (i, 0)),
                  mod.spec(which_gate, tm, 1)],
        out_specs=pl.BlockSpec((tm, d), lambda i: (i, 0)),
        out_shape=jax.ShapeDtypeStruct((t, d), F32),
        scratch_shapes=[pltpu.VMEM((k, d), BF16)],
        compiler_params=_cparams(("arbitrary",)),
        name=name,
    )(y, w, x, mod.arr)


def _head_expand():
    h = np.arange(LANE)[:, None]
    c = np.arange(A_D_INNER)[None, :]
    return jnp.asarray((c // A_HEAD_DIM == h).astype(np.float32), dtype=BF16)


def _gate_norm(y, z, ng):
    gated = y * _silu(z)
    ms = jnp.mean(gated * gated, axis=-1, keepdims=True)
    return gated * lax.rsqrt(ms + EPS) * ng


def _ssd_body(z_ref, xs_ref, bc_ref, dt_ref, cprev_ref, sprev_ref, cw_ref, cb_ref, dtb_ref, alog_ref,
              dx_ref, ng_ref, exp_ref, yn_ref, cnew_ref, snew_ref, tail_ref, act_ref):
    c = pl.program_id(1)
    q = A_CHUNK
    nt = A_CONV - 1

    @pl.when(c == 0)
    def _():
        tail_ref[0:nt, :] = cprev_ref[...]
        snew_ref[...] = sprev_ref[...]

    row = lax.broadcasted_iota(I32, (q, q), 0)
    col = lax.broadcasted_iota(I32, (q, q), 1)
    shifts = [jnp.where(row - col == nt - k, 1.0, 0.0).astype(BF16) for k in range(nt)]
    sub = lax.broadcasted_iota(I32, (8, 1), 0)
    cw = 512
    for j in range(A_CONV_CH // cw):
        sl = slice(j * cw, (j + 1) * cw)
        src = xs_ref if (j + 1) * cw <= A_D_INNER else bc_ref
        off = j * cw if src is xs_ref else j * cw - A_D_INNER
        x_bf = src[:, off:off + cw]
        conv = cb_ref[:, sl] + x_bf.astype(F32) * cw_ref[nt:nt + 1, sl]
        for k in range(nt):
            conv = conv + _dot(shifts[k], x_bf) * cw_ref[k:k + 1, sl]
        act_ref[:, sl] = _silu(conv)
        corr = jnp.zeros((8, cw), F32)
        for l in range(nt):
            c_l = sum(tail_ref[l + k:l + k + 1, sl] * cw_ref[k:k + 1, sl] for k in range(nt - l))
            corr = jnp.where(sub == l, c_l, corr)
        act_ref[0:8, sl] = _silu(conv[0:8, :] + corr)
        tail_ref[0:nt, sl] = x_bf[q - 8:q, :].astype(F32)[8 - nt:8, :]

    @pl.when(c == pl.num_programs(1) - 1)
    def _():
        cnew_ref[...] = tail_ref[0:nt, :]

    dt = _softplus(dt_ref[...] + dtb_ref[...])
    a = dt * (-jnp.exp(alog_ref[...]))
    row = lax.broadcasted_iota(I32, (q, q), 0)
    col = lax.broadcasted_iota(I32, (q, q), 1)
    causal = row >= col
    tril = jnp.where(causal, 1.0, 0.0).astype(BF16)
    a_hi, a_mid, a_lo = _split3(a)
    cs = _dot(tril, a_hi) + (_dot(tril, a_mid) + _dot(tril, a_lo))
    cs_t = cs.T
    dt_t = dt.T
    cs_last = cs[q - 1:q, :]
    ecs = jnp.exp(cs)
    wend = jnp.exp(cs_last - cs) * dt
    st_hi, st_lo = _split2(jnp.concatenate([ecs, wend], axis=0))
    st_x = _dot(st_hi, exp_ref[...]) + _dot(st_lo, exp_ref[...])
    lane_head = lax.broadcasted_iota(I32, (q, A_GW), 1) // A_HEAD_DIM

    for g in range(A_N_GROUPS):
        gsl = slice(g * A_GW, (g + 1) * A_GW)
        b_g = act_ref[:, A_D_INNER + g * A_D_STATE:A_D_INNER + (g + 1) * A_D_STATE].astype(BF16)
        c_g = act_ref[:, A_D_INNER + A_GN + g * A_D_STATE:A_D_INNER + A_GN + (g + 1) * A_D_STATE].astype(BF16)
        x_g = act_ref[:, gsl]
        x_bf = x_g.astype(BF16)
        h_g = snew_ref[gsl, :]
        cb = _dot_nt(c_g, b_g)
        y = jnp.zeros((q, A_GW), F32)
        for r in range(A_HPG):
            h = g * A_HPG + r
            seg = cs[:, h:h + 1] - cs_t[h:h + 1, :]
            decay = jnp.exp(jnp.where(causal, seg, -1e30))
            wm = (cb * decay * dt_t[h:h + 1, :]).astype(BF16)
            y = y + _dot(wm, jnp.where(lane_head == r, x_bf, jnp.zeros_like(x_bf)))
        y = y + st_x[0:q, gsl] * _dot_nt(c_g, h_g.astype(BF16)) + dx_ref[:, gsl] * x_g
        yn_ref[:, gsl] = _gate_norm(y, z_ref[:, gsl].astype(F32), ng_ref[:, gsl]).astype(BF16)
        s_new = _dot_tn((x_g * st_x[q:2 * q, gsl]).astype(BF16), b_g)
        for r in range(A_HPG):
            h = g * A_HPG + r
            rsl = slice(g * A_GW + r * A_HEAD_DIM, g * A_GW + (r + 1) * A_HEAD_DIM)
            keep = jnp.exp(cs[q - 1:q, h:h + 1])
            snew_ref[rsl, :] = snew_ref[rsl, :] * keep + s_new[r * A_HEAD_DIM:(r + 1) * A_HEAD_DIM, :]


def _ssd_call(proj, dt_raw, conv_prev, ssm_prev, conv_w, conv_b, dt_bias, a_log, d_x, norm_g, nb, name):
    t = proj.shape[0]
    q = A_CHUNK
    nc = t // nb // q
    hp = A_N_HEADS * A_HEAD_DIM
    row = lambda b, c: (b * nc + c, 0)
    full = lambda shape: pl.BlockSpec(shape, lambda b, c: (0,) * len(shape))
    return pl.pallas_call(
        _ssd_body,
        grid=(nb, nc),
        in_specs=[pl.BlockSpec((q, A_D_INNER), lambda b, c: (b * nc + c, 0)),
                  pl.BlockSpec((q, A_D_INNER), lambda b, c: (b * nc + c, 1)),
                  pl.BlockSpec((q, 2 * A_GN), lambda b, c: (b * nc + c, 2)),
                  pl.BlockSpec((q, LANE), row),
                  pl.BlockSpec((None, A_CONV - 1, A_CONV_CH), lambda b, c: (b, 0, 0)),
                  pl.BlockSpec((None, hp, A_D_STATE), lambda b, c: (b, 0, 0)),
                  full((A_CONV, A_CONV_CH)), full((1, A_CONV_CH)), full((1, LANE)), full((1, LANE)),
                  full((1, A_D_INNER)), full((1, A_D_INNER)), full((LANE, A_D_INNER))],
        out_specs=[pl.BlockSpec((q, A_D_INNER), row),
                   pl.BlockSpec((None, A_CONV - 1, A_CONV_CH), lambda b, c: (b, 0, 0)),
                   pl.BlockSpec((None, hp, A_D_STATE), lambda b, c: (b, 0, 0))],
        out_shape=[jax.ShapeDtypeStruct((t, A_D_INNER), BF16),
                   jax.ShapeDtypeStruct((nb, A_CONV - 1, A_CONV_CH), F32),
                   jax.ShapeDtypeStruct((nb, hp, A_D_STATE), F32)],
        scratch_shapes=[pltpu.VMEM((8, A_CONV_CH), F32), pltpu.VMEM((q, A_CONV_CH), F32)],
        compiler_params=_cparams(("arbitrary", "arbitrary")),
        name=name,
    )(proj, proj, proj, dt_raw, conv_prev, ssm_prev, conv_w, conv_b, dt_bias, a_log, d_x, norm_g, _head_expand())


def _ssd_step_body(z_ref, xs_ref, bc_ref, dt_ref, cprev_ref, sprev_ref, cw_ref, cb_ref, dtb_ref, alog_ref,
                   dx_ref, ng_ref, exp_ref, yn_ref, cnew_ref, snew_ref, da_ref, y_ref):
    bt = z_ref.shape[0]
    cur = jnp.concatenate([xs_ref[...], bc_ref[...]], axis=1).astype(F32)
    conv = cb_ref[...] + cur * cw_ref[A_CONV - 1:A_CONV, :]
    for k in range(A_CONV - 1):
        conv = conv + cprev_ref[:, k, :] * cw_ref[k:k + 1, :]
    for k in range(A_CONV - 2):
        cnew_ref[:, k, :] = cprev_ref[:, k + 1, :]
    cnew_ref[:, A_CONV - 2, :] = cur
    act = _silu(conv)
    xs = act[:, 0:A_D_INNER]
    bm = act[:, A_D_INNER:A_D_INNER + A_GN]
    cm = act[:, A_D_INNER + A_GN:A_CONV_CH]
    dt = _softplus(dt_ref[...] + dtb_ref[...])
    da_ref[...] = jnp.exp(dt * (-jnp.exp(alog_ref[...])))
    dt_hi, dt_lo = _split2(dt)
    dt_x = _dot(dt_hi, exp_ref[...]) + _dot(dt_lo, exp_ref[...])
    xdt = xs * dt_x
    rows = lax.broadcasted_iota(I32, (bt, 1), 0)
    y_ref[...] = jnp.zeros_like(y_ref)

    def per_seq(j, carry):
        mine = rows == j
        xdt_j = jnp.where(mine, xdt, 0.0)
        da_j = da_ref[pl.ds(j, 1), :]
        for g in range(A_N_GROUPS):
            gsl = slice(g * A_GW, (g + 1) * A_GW)
            nsl = slice(g * A_D_STATE, (g + 1) * A_D_STATE)
            xh, xl = _split2(xdt_j[:, gsl])
            bh, bl = _split2(bm[:, nsl])
            outer = _dot_tn(xh, bh) + (_dot_tn(xl, bh) + _dot_tn(xh, bl))
            for r in range(A_HPG):
                h = g * A_HPG + r
                rsl = slice(r * A_HEAD_DIM, (r + 1) * A_HEAD_DIM)
                hsl = slice(g * A_GW + r * A_HEAD_DIM, g * A_GW + (r + 1) * A_HEAD_DIM)
                snew_ref[j, hsl, :] = sprev_ref[j, hsl, :] * da_j[:, h:h + 1] + outer[rsl, :]
            h_new = snew_ref[j, gsl, :]
            hh, hl = _split2(h_new)
            ch, cl = _split2(cm[:, nsl])
            yg = _dot_nt(ch, hh) + (_dot_nt(cl, hh) + _dot_nt(ch, hl))
            y_ref[:, gsl] = y_ref[:, gsl] + jnp.where(mine, yg, 0.0)
        return carry

    lax.fori_loop(0, bt, per_seq, 0)
    y = y_ref[...] + dx_ref[...] * xs
    z = z_ref[...].astype(F32)
    for g in range(A_N_GROUPS):
        gsl = slice(g * A_GW, (g + 1) * A_GW)
        yn_ref[:, gsl] = _gate_norm(y[:, gsl], z[:, gsl], ng_ref[:, gsl]).astype(BF16)


_SSD_STEP_INPUTS = 13


def _ssd_step_aliased_body(*refs):
    _ssd_step_body(*refs[:_SSD_STEP_INPUTS], *refs[_SSD_STEP_INPUTS + 1:])


def _ssd_step_call(proj, dt_raw, conv_prev, ssm_all, slot, ssm_out, conv_w, conv_b, dt_bias, a_log, d_x, norm_g,
                   name):
    nb = proj.shape[0]
    bt = 8
    hp = A_N_HEADS * A_HEAD_DIM
    full = lambda shape: pl.BlockSpec(shape, lambda i: (0,) * len(shape))
    state_spec = pl.BlockSpec((None, bt, hp, A_D_STATE), lambda i: (slot, i, 0, 0))
    in_specs = [pl.BlockSpec((bt, A_D_INNER), lambda i: (i, 0)),
                pl.BlockSpec((bt, A_D_INNER), lambda i: (i, 1)),
                pl.BlockSpec((bt, 2 * A_GN), lambda i: (i, 2)),
                pl.BlockSpec((bt, LANE), lambda i: (i, 0)),
                pl.BlockSpec((bt, A_CONV - 1, A_CONV_CH), lambda i: (i, 0, 0)),
                state_spec,
                full((A_CONV, A_CONV_CH)), full((1, A_CONV_CH)), full((1, LANE)), full((1, LANE)),
                full((1, A_D_INNER)), full((1, A_D_INNER)), full((LANE, A_D_INNER))]
    args = [proj, proj, proj, dt_raw, conv_prev, ssm_all, conv_w, conv_b, dt_bias, a_log, d_x, norm_g, _head_expand()]
    assert len(args) == _SSD_STEP_INPUTS
    aliases = {}
    if ssm_out is not None:
        in_specs.append(pl.BlockSpec(memory_space=pl.ANY))
        args.append(ssm_out)
        aliases = {_SSD_STEP_INPUTS: 2}
    return pl.pallas_call(
        _ssd_step_body if ssm_out is None else _ssd_step_aliased_body,
        grid=(nb // bt,),
        in_specs=in_specs,
        out_specs=[pl.BlockSpec((bt, A_D_INNER), lambda i: (i, 0)),
                   pl.BlockSpec((bt, A_CONV - 1, A_CONV_CH), lambda i: (i, 0, 0)),
                   state_spec],
        out_shape=[jax.ShapeDtypeStruct((nb, A_D_INNER), BF16),
                   jax.ShapeDtypeStruct((nb, A_CONV - 1, A_CONV_CH), F32),
                   jax.ShapeDtypeStruct(ssm_all.shape, F32)],
        scratch_shapes=[pltpu.VMEM((bt, LANE), F32), pltpu.VMEM((bt, A_D_INNER), F32)],
        input_output_aliases=aliases,
        compiler_params=_cparams(("arbitrary",)),
        name=name,
    )(*args)


def _gmlp_in_body(x_ref, g_ref, sc_ref, sh_ref, w_ref, b_ref, lg_ref, lb_ref, o_ref, hn_ref):
    j = pl.program_id(1)

    @pl.when(j == 0)
    def _():
        hn_ref[...] = _prenorm(x_ref[...], g_ref[...], sc_ref[...], sh_ref[...]).astype(BF16)

    uv = jax.nn.gelu(_dot(hn_ref[...], w_ref[...]) + b_ref[...], approximate=True)

    @pl.when(j == 0)
    def _():
        o_ref[...] = uv.astype(o_ref.dtype)

    @pl.when(j == 1)
    def _():
        vc = uv - jnp.mean(uv, axis=-1, keepdims=True)
        var = jnp.mean(vc * vc, axis=-1, keepdims=True)
        o_ref[...] = (vc * lax.rsqrt(var + EPS) * lg_ref[...] + lb_ref[...]).astype(o_ref.dtype)


def _gmlp_in_call(x, g, mod, w, b, ln_g, ln_b, tm, out_dtype, name):
    t, d = x.shape
    return pl.pallas_call(
        _gmlp_in_body,
        grid=(t // tm, 2),
        in_specs=[pl.BlockSpec((tm, d), lambda i, j: (i, 0)),
                  pl.BlockSpec((1, d), lambda i, j: (0, 0)),
                  mod.spec(1, tm, 2), mod.spec(0, tm, 2),
                  pl.BlockSpec((d, B_D), lambda i, j: (0, j)),
                  pl.BlockSpec((1, B_D), lambda i, j: (0, j)),
                  pl.BlockSpec((1, B_D), lambda i, j: (0, 0)),
                  pl.BlockSpec((1, B_D), lambda i, j: (0, 0))],
        out_specs=pl.BlockSpec((tm, B_D), lambda i, j: (i, j)),
        out_shape=jax.ShapeDtypeStruct((t, 2 * B_D), out_dtype),
        scratch_shapes=[pltpu.VMEM((tm, d), BF16)],
        compiler_params=_cparams(("arbitrary", "arbitrary")),
        name=name,
    )(x, g.reshape(1, d), mod.arr, mod.arr, w, b.reshape(1, -1), ln_g.reshape(1, -1), ln_b.reshape(1, -1))


def _gmlp_out_body(u_ref, v_ref, ws_ref, bs_ref, w_ref, x_ref, gate_ref, o_ref, wbf_ref, wsbf_ref, m_ref):
    q = B_CHUNK

    @pl.when(pl.program_id(0) == 0)
    def _():
        wbf_ref[...] = w_ref[...].astype(BF16)
        causal = lax.broadcasted_iota(I32, (q, q), 0) >= lax.broadcasted_iota(I32, (q, q), 1)
        for g in range(B_N_GROUPS):
            wsbf_ref[g] = jnp.where(causal, ws_ref[g], 0.0).astype(BF16)

    for ci in range(u_ref.shape[0] // q):
        rsl = slice(ci * q, (ci + 1) * q)
        for g in range(B_N_GROUPS):
            gsl = slice(g * B_GROUP_DIM, (g + 1) * B_GROUP_DIM)
            mixed = _dot(wsbf_ref[g], v_ref[rsl, gsl].astype(BF16)) + bs_ref[:, g:g + 1]
            m_ref[rsl, gsl] = (u_ref[rsl, gsl].astype(F32) * mixed).astype(BF16)
    o_ref[...] = x_ref[...] + gate_ref[...] * _dot(m_ref[...], wbf_ref[...])


def _gmlp_out_call(uv, w_s, b_s, w_out, x, mod, tm, name):
    t, d = x.shape
    q = B_CHUNK
    return pl.pallas_call(
        _gmlp_out_body,
        grid=(t // tm,),
        in_specs=[pl.BlockSpec((tm, B_D), lambda i: (i, 0)),
                  pl.BlockSpec((tm, B_D), lambda i: (i, 1)),
                  pl.BlockSpec((B_N_GROUPS, q, q), lambda i: (0, 0, 0)),
                  pl.BlockSpec((q, B_N_GROUPS), lambda i: (0, 0)),
                  pl.BlockSpec((B_D, d), lambda i: (0, 0)),
                  pl.BlockSpec((tm, d), lambda i: (i, 0)),
                  mod.spec(2, tm, 1)],
        out_specs=pl.BlockSpec((tm, d), lambda i: (i, 0)),
        out_shape=jax.ShapeDtypeStruct((t, d), F32),
        scratch_shapes=[pltpu.VMEM((B_D, d), BF16), pltpu.VMEM((B_N_GROUPS, q, q), BF16),
                        pltpu.VMEM((tm, B_D), BF16)],
        compiler_params=_cparams(("arbitrary",)),
        name=name,
    )(uv, uv, w_s, b_s.T, w_out, x, mod.arr)


def _gmlp_out_step_body(u_ref, v_ref, wd_ref, bd_ref, w_ref, x_ref, gate_ref, o_ref):
    mixed = v_ref[...] * wd_ref[...] + bd_ref[...]
    m = (u_ref[...] * mixed).astype(BF16)
    o_ref[...] = x_ref[...] + gate_ref[...] * _dot(m, w_ref[...].astype(BF16))


def _gmlp_out_step_call(uv, w_s, b_s, w_out, x, mod, name):
    t, d = x.shape
    wd = jnp.repeat(w_s[:, 0, 0], B_GROUP_DIM).reshape(1, B_D)
    bd = jnp.repeat(b_s[:, 0], B_GROUP_DIM).reshape(1, B_D)
    return pl.pallas_call(
        _gmlp_out_step_body,
        grid=(1,),
        in_specs=[pl.BlockSpec((t, B_D), lambda i: (0, 0)),
                  pl.BlockSpec((t, B_D), lambda i: (0, 1)),
                  pl.BlockSpec((1, B_D), lambda i: (0, 0)),
                  pl.BlockSpec((1, B_D), lambda i: (0, 0)),
                  pl.BlockSpec((B_D, d), lambda i: (0, 0)),
                  pl.BlockSpec((t, d), lambda i: (0, 0)),
                  mod.spec(2, t, 1)],
        out_specs=pl.BlockSpec((t, d), lambda i: (0, 0)),
        out_shape=jax.ShapeDtypeStruct((t, d), F32),
        compiler_params=_cparams(("arbitrary",)),
        name=name,
    )(uv, uv, wd, bd, w_out, x, mod.arr)


def _pool_matmul(pooled, wg_ref):
    outs = []
    for gi in range(len(C_WINDOWS)):
        gsl = slice(gi * C_GROUP_DIM, (gi + 1) * C_GROUP_DIM)
        outs.append(_dot(pooled[:, gsl].astype(BF16), wg_ref[gi].astype(BF16)))
    return jnp.concatenate(outs, axis=-1)


def _pool_body(x_ref, g_ref, sc_ref, sh_ref, gate_ref, prev_ref, wg_ref, scale_ref, o_ref, pnew_ref, hp_ref,
               *, tiles_per_seq, start):
    i = pl.program_id(0)
    tm = x_ref.shape[0]
    top = 16
    ti = i % tiles_per_seq

    @pl.when(ti == 0)
    def _():
        hp_ref[top - C_STATE:top, :] = prev_ref[...]

    hn = _prenorm(x_ref[...], g_ref[...], sc_ref[...], sh_ref[...])
    hp_ref[top:top + tm, :] = hn
    pos = start + ti * tm + lax.broadcasted_iota(I32, (tm, 1), 0)
    outs = []
    for gi, w in enumerate(C_WINDOWS):
        gsl = slice(gi * C_GROUP_DIM, (gi + 1) * C_GROUP_DIM)
        acc = hn[:, gsl]
        for k in range(1, w):
            acc = acc + hp_ref[top - k:top - k + tm, gsl]
        cnt = jnp.minimum(pos + 1, w).astype(F32)
        outs.append(acc / cnt - hn[:, gsl])
    y = _pool_matmul(jnp.concatenate(outs, axis=-1), wg_ref) * scale_ref[...]
    o_ref[...] = x_ref[...] + gate_ref[...] * y
    hist = hp_ref[top + tm - C_STATE:top + tm, :]
    hp_ref[top - C_STATE:top, :] = hist

    @pl.when(ti == tiles_per_seq - 1)
    def _():
        pnew_ref[...] = hist


def _pool_call(x, g, mod, prev, w_g, scale, nb, tm, start, name):
    t, d = x.shape
    tiles = t // nb // tm
    ng = len(C_WINDOWS)
    return pl.pallas_call(
        functools.partial(_pool_body, tiles_per_seq=tiles, start=start),
        grid=(t // tm,),
        in_specs=[pl.BlockSpec((tm, d), lambda i: (i, 0)),
                  pl.BlockSpec((1, d), lambda i: (0, 0)),
                  mod.spec(1, tm, 1), mod.spec(0, tm, 1), mod.spec(2, tm, 1),
                  pl.BlockSpec((None, C_STATE, d), lambda i: (i // tiles, 0, 0)),
                  pl.BlockSpec((ng, C_GROUP_DIM, C_GROUP_DIM), lambda i: (0, 0, 0)),
                  pl.BlockSpec((1, d), lambda i: (0, 0))],
        out_specs=[pl.BlockSpec((tm, d), lambda i: (i, 0)),
                   pl.BlockSpec((None, C_STATE, d), lambda i: (i // tiles, 0, 0))],
        out_shape=[jax.ShapeDtypeStruct((t, d), F32), jax.ShapeDtypeStruct((nb, C_STATE, d), F32)],
        scratch_shapes=[pltpu.VMEM((16 + tm, d), F32)],
        compiler_params=_cparams(("arbitrary",)),
        name=name,
    )(x, g.reshape(1, d), mod.arr, mod.arr, mod.arr, prev, w_g, scale.reshape(1, d))


def _pool_step_body(x_ref, g_ref, sc_ref, sh_ref, gate_ref, prev_ref, wg_ref, scale_ref, o_ref, pnew_ref, *, start):
    hn = _prenorm(x_ref[...], g_ref[...], sc_ref[...], sh_ref[...])
    outs = []
    for gi, w in enumerate(C_WINDOWS):
        gsl = slice(gi * C_GROUP_DIM, (gi + 1) * C_GROUP_DIM)
        acc = hn[:, gsl]
        for k in range(1, w):
            acc = acc + prev_ref[:, C_STATE - k, gsl]
        outs.append(acc / float(min(start + 1, w)) - hn[:, gsl])
    y = _pool_matmul(jnp.concatenate(outs, axis=-1), wg_ref) * scale_ref[...]
    o_ref[...] = x_ref[...] + gate_ref[...] * y
    for k in range(C_STATE - 1):
        pnew_ref[:, k, :] = prev_ref[:, k + 1, :]
    pnew_ref[:, C_STATE - 1, :] = hn


def _pool_step_call(x, g, mod, prev, w_g, scale, start, name):
    t, d = x.shape
    bt = 32
    ng = len(C_WINDOWS)
    return pl.pallas_call(
        functools.partial(_pool_step_body, start=start),
        grid=(t // bt,),
        in_specs=[pl.BlockSpec((bt, d), lambda i: (i, 0)),
                  pl.BlockSpec((1, d), lambda i: (0, 0)),
                  mod.spec(1, bt, 1), mod.spec(0, bt, 1), mod.spec(2, bt, 1),
                  pl.BlockSpec((bt, C_STATE, d), lambda i: (i, 0, 0)),
                  pl.BlockSpec((ng, C_GROUP_DIM, C_GROUP_DIM), lambda i: (0, 0, 0)),
                  pl.BlockSpec((1, d), lambda i: (0, 0))],
        out_specs=[pl.BlockSpec((bt, d), lambda i: (i, 0)),
                   pl.BlockSpec((bt, C_STATE, d), lambda i: (i, 0, 0))],
        out_shape=[jax.ShapeDtypeStruct((t, d), F32), jax.ShapeDtypeStruct((t, C_STATE, d), F32)],
        compiler_params=_cparams(("arbitrary",)),
        name=name,
    )(x, g.reshape(1, d), mod.arr, mod.arr, mod.arr, prev, w_g, scale.reshape(1, d))


def _route_rows(s, b):
    npg = EXPERTS_PER_GROUP
    gscore = []
    for q in range(N_EXPERT_GROUPS):
        v = b[q * npg:(q + 1) * npg]
        best = None
        for i in range(npg):
            for j in range(i + 1, npg):
                best = v[i] + v[j] if best is None else jnp.maximum(best, v[i] + v[j])
        gscore.append(best)
    gsel = jnp.zeros_like(gscore[0], dtype=I32)
    gbest = gscore[0]
    for q in range(1, N_EXPERT_GROUPS):
        better = gscore[q] > gbest
        gsel = jnp.where(better, q, gsel)
        gbest = jnp.where(better, gscore[q], gbest)
    vb, vs = [], []
    for k in range(npg):
        bk, sk = b[k], s[k]
        for q in range(1, N_EXPERT_GROUPS):
            bk = jnp.where(gsel == q, b[q * npg + k], bk)
            sk = jnp.where(gsel == q, s[q * npg + k], sk)
        vb.append(bk)
        vs.append(sk)
    i1 = jnp.zeros_like(gsel)
    m1 = vb[0]
    for k in range(1, npg):
        better = vb[k] > m1
        i1 = jnp.where(better, k, i1)
        m1 = jnp.where(better, vb[k], m1)
    i2 = jnp.full_like(gsel, -1)
    m2 = jnp.zeros_like(m1)
    for k in range(npg):
        better = (i1 != k) & ((i2 < 0) | (vb[k] > m2))
        i2 = jnp.where(better, k, i2)
        m2 = jnp.where(better, vb[k], m2)
    s1 = vs[0]
    s2 = vs[0]
    for k in range(1, npg):
        s1 = jnp.where(i1 == k, vs[k], s1)
        s2 = jnp.where(i2 == k, vs[k], s2)
    w1 = s1 / (s1 + s2)
    w2 = s2 / (s1 + s2)
    lo = jnp.minimum(i1, i2)
    hi = jnp.maximum(i1, i2)
    pair = jnp.zeros_like(gsel)
    for p, (a, c) in enumerate(PAIRS):
        pair = jnp.where((lo == a) & (hi == c), p, pair)
    first_is_lo = i1 < i2
    return (gsel * len(PAIRS) + pair, jnp.where(first_is_lo, w1, w2), jnp.where(first_is_lo, w2, w1))


def _route_body(x_ref, g_ref, sc_ref, sh_ref, rw_ref, rb_ref, cnt_in_ref, *rest, aliased):
    if aliased:
        rest = rest[4:]
    rows_ref, gates_ref, bucket_ref, rank_ref, cnt_ref = rest
    tm = x_ref.shape[0]

    @pl.when(pl.program_id(0) == 0)
    def _():
        cnt_ref[...] = cnt_in_ref[...]

    hn = _prenorm(x_ref[...], g_ref[...], sc_ref[...], sh_ref[...])
    logits_t = _dot3(hn, rw_ref[...]).T
    scores = _sigmoid(logits_t[0:N_EXPERTS, :])
    biased = scores + rb_ref[0:N_EXPERTS, :]
    bucket, w_lo, w_hi = _route_rows([scores[e:e + 1, :] for e in range(N_EXPERTS)],
                                     [biased[e:e + 1, :] for e in range(N_EXPERTS)])
    bucket_ref[...] = bucket
    onehot = (lax.broadcasted_iota(I32, (BUCKET_ROWS, tm), 0) == bucket).astype(F32)
    before = (lax.broadcasted_iota(I32, (tm, tm), 0) < lax.broadcasted_iota(I32, (tm, tm), 1)).astype(BF16)
    earlier = _dot(onehot.astype(BF16), before) + cnt_ref[:, 0:1]
    rank_ref[...] = jnp.sum(onehot * earlier, axis=0, keepdims=True).astype(I32)
    cnt_ref[...] = cnt_ref[...] + jnp.sum(onehot, axis=1, keepdims=True)

    sub = lax.broadcasted_iota(I32, (LANE, tm), 0)
    gate_t = jnp.where(sub == 0, w_lo, jnp.where(sub == 1, w_hi, 0.0))
    gates_ref[...] = gate_t.T
    for j in range(ROW_PLANES):
        rows_ref[pl.ds(j, tm, stride=ROW_PLANES), :] = hn[:, j * LANE:(j + 1) * LANE]


def _route_call(x, g, mod, router_w, router_b, counts, tm, total_rows, row_offset, prior, name):
    t, d = x.shape
    blk0 = row_offset // tm
    rw = jnp.pad(router_w, ((0, 0), (0, LANE - N_EXPERTS)))
    rb = jnp.pad(router_b.reshape(-1, 1), ((0, LANE - N_EXPERTS), (0, 0)))
    in_specs = [pl.BlockSpec((tm, d), lambda i: (i, 0)),
                pl.BlockSpec((1, d), lambda i: (0, 0)),
                mod.spec(4, tm, 1), mod.spec(3, tm, 1),
                pl.BlockSpec((d, LANE), lambda i: (0, 0)),
                pl.BlockSpec((LANE, 1), lambda i: (0, 0)),
                pl.BlockSpec((BUCKET_ROWS, LANE), lambda i: (0, 0))]
    args = [x, g.reshape(1, d), mod.arr, mod.arr, rw, rb, counts]
    aliases = {}
    if prior is not None:
        in_specs += [pl.BlockSpec(memory_space=pl.ANY)] * len(prior)
        aliases = {len(args) + k: k for k in range(len(prior))}
        args += list(prior)
    return pl.pallas_call(
        functools.partial(_route_body, aliased=prior is not None),
        grid=(t // tm,),
        in_specs=in_specs,
        out_specs=[pl.BlockSpec((tm * ROW_PLANES, LANE), lambda i: (blk0 + i, 0)),
                   pl.BlockSpec((tm, LANE), lambda i: (blk0 + i, 0)),
                   pl.BlockSpec((1, tm), lambda i: (0, blk0 + i)),
                   pl.BlockSpec((1, tm), lambda i: (0, blk0 + i)),
                   pl.BlockSpec((BUCKET_ROWS, LANE), lambda i: (0, 0))],
        out_shape=[jax.ShapeDtypeStruct((total_rows * ROW_PLANES, LANE), F32),
                   jax.ShapeDtypeStruct((total_rows, LANE), F32),
                   jax.ShapeDtypeStruct((1, total_rows), I32),
                   jax.ShapeDtypeStruct((1, total_rows), I32),
                   jax.ShapeDtypeStruct((BUCKET_ROWS, LANE), F32)],
        input_output_aliases=aliases,
        compiler_params=_cparams(("arbitrary",)),
        name=name,
    )(*args)


def _invert_body(dest_ref, gather_init_ref, scatter_init_ref, gather_ref, scatter_ref):
    pltpu.sync_copy(gather_init_ref, gather_ref)
    pltpu.sync_copy(scatter_init_ref, scatter_ref)

    def put(t, carry):
        d = dest_ref[t]
        gather_ref[d] = t
        scatter_ref[d] = t
        return carry

    lax.fori_loop(0, dest_ref.shape[0], put, 0, unroll=8)


def _invert_call(dest, n_slots, name):
    total = dest.shape[0]
    slots = jnp.arange(n_slots, dtype=I32)
    gather_init = jnp.zeros((n_slots,), I32)
    scatter_init = total + slots % EXPERT_TILE
    smem = pl.BlockSpec(memory_space=pltpu.SMEM)
    return pl.pallas_call(
        _invert_body,
        in_specs=[smem, pl.BlockSpec(memory_space=pl.ANY), pl.BlockSpec(memory_space=pl.ANY)],
        out_specs=[smem, smem],
        out_shape=[jax.ShapeDtypeStruct((n_slots,), I32)] * 2,
        name=name,
    )(dest, gather_init, scatter_init)


def _experts_body(elo_ref, ehi_ref, fresh_ref, valid_ref, gather_ref, scatter_ref, rows_ref, wg0_ref, wg1_ref,
                  wu0_ref, wu1_ref, wd0_ref, wd1_ref, y_ref, wg_ref, wu_ref, wd_ref, xb_ref, yb_ref, gsem, ssem,
                  *, token_words):
    i = pl.program_id(0)
    n = pl.num_programs(0)
    f = D_EXPERT
    tile = EXPERT_TILE
    words = tile * ROW_PLANES
    out_words = tile * OUT_PLANES
    group = 8

    def start_gather(step, slot):
        def body(b, carry):
            for k in range(group):
                r = b * group + k
                tok = gather_ref[step * tile + r]
                pltpu.make_async_copy(rows_ref.at[pl.ds(pl.multiple_of(tok * ROW_PLANES, ROW_PLANES), ROW_PLANES)],
                                      xb_ref.at[slot, pl.ds(pl.multiple_of(r * ROW_PLANES, ROW_PLANES), ROW_PLANES)],
                                      gsem.at[slot]).start(priority=k % 2)
            return carry

        lax.fori_loop(0, tile // group, body, 0)

    def start_scatter(step, slot):
        def body(b, carry):
            for k in range(group):
                r = b * group + k
                tok = scatter_ref[step * tile + r]
                pltpu.make_async_copy(yb_ref.at[slot, pl.ds(pl.multiple_of(r * OUT_PLANES, OUT_PLANES), OUT_PLANES)],
                                      y_ref.at[pl.ds(pl.multiple_of(tok * OUT_PLANES, OUT_PLANES), OUT_PLANES)],
                                      ssem.at[slot]).start(priority=k % 2)
            return carry

        lax.fori_loop(0, tile // group, body, 0)

    def wait_gather(slot):
        pltpu.make_async_copy(rows_ref.at[pl.ds(0, words)], xb_ref.at[slot], gsem.at[slot]).wait()

    def wait_scatter(slot):
        pltpu.make_async_copy(yb_ref.at[slot], y_ref.at[pl.ds(0, out_words)], ssem.at[slot]).wait()

    @pl.when(i == 0)
    def _():
        yb_ref[1] = jnp.zeros((out_words, LANE), F32)
        spare = pltpu.make_async_copy(yb_ref.at[1], y_ref.at[pl.ds(token_words, out_words)], ssem.at[1])
        spare.start()
        spare.wait()

    @pl.when(fresh_ref[i] == 1)
    def _():
        wg_ref[:, 0:f] = wg0_ref[...].astype(BF16)
        wg_ref[:, f:2 * f] = wg1_ref[...].astype(BF16)
        wu_ref[:, 0:f] = wu0_ref[...].astype(BF16)
        wu_ref[:, f:2 * f] = wu1_ref[...].astype(BF16)
        wd_ref[0:f, :] = wd0_ref[...].astype(BF16)
        wd_ref[f:2 * f, :] = wd1_ref[...].astype(BF16)

    nxt = jnp.minimum(i + 1, n - 1)
    more = jnp.logical_and(i + 1 < n, valid_ref[nxt] == 1)

    @pl.when(valid_ref[i] == 1)
    def _():
        slot = i % 2

        @pl.when(i == 0)
        def _():
            start_gather(0, 0)

        @pl.when(more)
        def _():
            start_gather(i + 1, 1 - slot)

        wait_gather(slot)
        x = jnp.concatenate([xb_ref[slot, pl.ds(j, tile, stride=ROW_PLANES), :].astype(BF16)
                             for j in range(ROW_PLANES)], axis=-1)
        act = (_silu(_dot(x, wg_ref[...])) * _dot(x, wu_ref[...])).astype(BF16)
        y_lo = _dot(act[:, 0:f], wd_ref[0:f, :])
        y_hi = _dot(act[:, f:2 * f], wd_ref[f:2 * f, :])

        @pl.when(i >= 2)
        def _():
            wait_scatter(slot)

        for j in range(ROW_PLANES):
            yb_ref[slot, pl.ds(j, tile, stride=OUT_PLANES), :] = y_lo[:, j * LANE:(j + 1) * LANE]
            yb_ref[slot, pl.ds(ROW_PLANES + j, tile, stride=OUT_PLANES), :] = y_hi[:, j * LANE:(j + 1) * LANE]
        start_scatter(i, slot)

        @pl.when(jnp.logical_not(more))
        def _():
            @pl.when(i >= 1)
            def _():
                wait_scatter(1 - slot)

            wait_scatter(slot)


def _experts_call(e_lo, e_hi, fresh, valid, gather_idx, scatter_idx, rows, w_gate, w_up, w_down, layer, name):
    n_tiles = e_lo.shape[0]
    tile = EXPERT_TILE
    d, f = D_MODEL, D_EXPERT
    tokens = rows.shape[0] // ROW_PLANES
    lo = lambda i, elo, ehi, *_: (layer, elo[i], 0, 0)
    hi = lambda i, elo, ehi, *_: (layer, ehi[i], 0, 0)
    return pl.pallas_call(
        functools.partial(_experts_body, token_words=tokens * OUT_PLANES),
        grid_spec=pltpu.PrefetchScalarGridSpec(
            num_scalar_prefetch=6,
            grid=(n_tiles,),
            in_specs=[pl.BlockSpec(memory_space=pl.ANY),
                      pl.BlockSpec((None, None, d, f), lo), pl.BlockSpec((None, None, d, f), hi),
                      pl.BlockSpec((None, None, d, f), lo), pl.BlockSpec((None, None, d, f), hi),
                      pl.BlockSpec((None, None, f, d), lo), pl.BlockSpec((None, None, f, d), hi)],
            out_specs=pl.BlockSpec(memory_space=pl.ANY),
            scratch_shapes=[pltpu.VMEM((d, 2 * f), BF16), pltpu.VMEM((d, 2 * f), BF16),
                            pltpu.VMEM((2 * f, d), BF16),
                            pltpu.VMEM((2, tile * ROW_PLANES, LANE), F32),
                            pltpu.VMEM((2, tile * OUT_PLANES, LANE), F32),
                            pltpu.SemaphoreType.DMA((2,)), pltpu.SemaphoreType.DMA((2,))]),
        out_shape=jax.ShapeDtypeStruct(((tokens + tile) * OUT_PLANES, LANE), F32),
        compiler_params=_cparams(("arbitrary",)),
        name=name,
    )(e_lo, e_hi, fresh, valid, gather_idx, scatter_idx, rows, w_gate, w_gate, w_up, w_up, w_down, w_down)


def _moe_res_body(y_ref, w_ref, x_ref, gate_ref, fg_ref, o_ref, *rest, final):
    tm = x_ref.shape[0]
    y_lo = jnp.concatenate([y_ref[pl.ds(j, tm, stride=OUT_PLANES), :] for j in range(ROW_PLANES)], axis=-1)
    y_hi = jnp.concatenate([y_ref[pl.ds(ROW_PLANES + j, tm, stride=OUT_PLANES), :] for j in range(ROW_PLANES)],
                           axis=-1)
    xn = x_ref[...] + gate_ref[...] * (w_ref[:, 0:1] * y_lo + w_ref[:, 1:2] * y_hi)
    o_ref[...] = xn
    if final:
        ms = jnp.mean(xn * xn, axis=-1, keepdims=True)
        rest[0][...] = xn * lax.rsqrt(ms + EPS) * fg_ref[...]


def _moe_res_call(y_tok, weights, x, mod, final_g, tm, row_offset, final, name):
    t, d = x.shape
    blk0 = row_offset // tm
    n_out = 2 if final else 1
    return pl.pallas_call(
        functools.partial(_moe_res_body, final=final),
        grid=(t // tm,),
        in_specs=[pl.BlockSpec((tm * OUT_PLANES, LANE), lambda i: (blk0 + i, 0)),
                  pl.BlockSpec((tm, LANE), lambda i: (blk0 + i, 0)),
                  pl.BlockSpec((tm, d), lambda i: (i, 0)),
                  mod.spec(5, tm, 1),
                  pl.BlockSpec((1, d), lambda i: (0, 0))],
        out_specs=[pl.BlockSpec((tm, d), lambda i: (i, 0))] * n_out,
        out_shape=[jax.ShapeDtypeStruct((t, d), F32)] * n_out,
        compiler_params=_cparams(("arbitrary",)),
        name=name,
    )(y_tok, weights, x, mod.arr, final_g.reshape(1, d))


def _tile_tables(counts, n_tiles):
    tile = EXPERT_TILE
    tiles_per_bucket = (counts + tile - 1) // tile
    ends = jnp.cumsum(tiles_per_bucket)
    starts = ends - tiles_per_bucket
    used = ends[-1]
    ti = jnp.arange(n_tiles, dtype=I32)
    valid = (ti < used).astype(I32)
    tile_bucket = jnp.sum((jnp.minimum(ti, used - 1)[:, None] >= ends[None, :]).astype(I32), axis=1)
    pair_lo = jnp.asarray([p[0] for p in PAIRS], I32)
    pair_hi = jnp.asarray([p[1] for p in PAIRS], I32)
    grp = tile_bucket // len(PAIRS)
    e_lo = grp * EXPERTS_PER_GROUP + pair_lo[tile_bucket % len(PAIRS)]
    e_hi = grp * EXPERTS_PER_GROUP + pair_hi[tile_bucket % len(PAIRS)]
    fresh = jnp.concatenate([jnp.ones((1,), I32), (tile_bucket[1:] != tile_bucket[:-1]).astype(I32)])
    return e_lo, e_hi, fresh, valid, (starts * tile).astype(I32)


def _moe_layer(xp, xs, norm_g2, mod_p, mod_s, router_w, router_b, w_gate, w_up, w_down, layer, final_g, final):
    tp, ts = xp.shape[0], xs.shape[0]
    total = tp + ts
    n_tiles = -(-total // EXPERT_TILE) + N_BUCKETS
    zero_counts = jnp.zeros((BUCKET_ROWS, LANE), F32)
    rows, weights, bucket, rank, counts = _route_call(xp, norm_g2, mod_p, router_w, router_b, zero_counts, 512, total,
                                                      0, None, "route_prompt")
    rows, weights, bucket, rank, counts = _route_call(xs, norm_g2, mod_s, router_w, router_b, counts, ts, total, tp,
                                                      (rows, weights, bucket, rank), "route_sample")
    e_lo, e_hi, fresh, valid, bucket_start = _tile_tables(counts[:N_BUCKETS, 0].astype(I32), n_tiles)
    dest = bucket_start[bucket[0]] + rank[0]
    gather_idx, scatter_idx = _invert_call(dest, n_tiles * EXPERT_TILE, "invert")
    y_tok = _experts_call(e_lo, e_hi, fresh, valid, gather_idx, scatter_idx, rows, w_gate, w_up, w_down, layer,
                          "experts")
    outp = _moe_res_call(y_tok, weights, xp, mod_p, final_g, 512, 0, final, "moe_res_prompt")
    outs = _moe_res_call(y_tok, weights, xs, mod_s, final_g, ts, tp, final, "moe_res_sample")
    return outp, outs


def _mamba_layer(x, g, mod, conv_prev, ssm, w_zx, w_dt, conv_w, conv_b, dt_bias, a_log, d_skip, norm_g, w_out,
                 tm, tag):
    pad_h = lambda v: jnp.pad(v.reshape(1, -1), ((0, 0), (0, LANE - A_N_HEADS)))
    tm_in = min(2 * tm, x.shape[0])
    proj = _norm_mm_call(x, g, mod, 1, 0, w_zx, tm_in, 512, BF16, "a_in_" + tag)
    dt_raw = _norm_mm3_call(x, g, mod, 1, 0, w_dt, tm, "a_dt_" + tag)
    d_x = jnp.repeat(d_skip, A_HEAD_DIM).reshape(1, A_D_INNER)
    weights = (conv_w, conv_b.reshape(1, -1), pad_h(dt_bias), pad_h(a_log), d_x, norm_g.reshape(1, -1))
    if ssm[0] == "step":
        yn, conv_new, ssm_new = _ssd_step_call(proj, dt_raw, conv_prev, ssm[1], ssm[2], ssm[3], *weights,
                                               "ssd_step_" + tag)
    else:
        yn, conv_new, ssm_new = _ssd_call(proj, dt_raw, conv_prev, ssm[2], *weights, ssm[1], "ssd_" + tag)
    x = _out_res_call(yn, w_out, x, mod, 2, tm, "a_out_" + tag)
    return x, conv_new, ssm_new


def kernel(x_prompt, x_sample, c_prompt, c_sample, state_a_conv, state_a_ssm, state_c_pool, w_mod, b_mod, norm_g, final_g, a_w_in, a_conv_w, a_conv_b, a_dt_bias, a_log, a_d, a_norm_g, a_w_out, b_w_in, b_b_in, b_ln_g, b_ln_b, b_w_s, b_b_s, b_w_out, c_w_g, c_scale, router_w, router_b, e_w_gate, e_w_up, e_w_down):
    bp, seq, d = x_prompt.shape
    bs = x_sample.shape[0]
    n_a, n_c = state_a_conv.shape[0], state_c_pool.shape[0]
    mod_all = _mod_call(jnp.concatenate([c_prompt, c_sample], axis=0), w_mod, b_mod)
    mod_p_arr = mod_all[:, :bp].reshape(DEPTH, bp, 6, 1, d)
    mod_s_arr = mod_all[:, bp:]
    xp = x_prompt.reshape(bp * seq, d)
    xs = x_sample.reshape(bs, d)
    conv_p, ssm_p, pool_p, conv_s, pool_s, v_s = [], [], [], [], [], []
    hp = A_N_HEADS * A_HEAD_DIM
    ssm_s_in = state_a_ssm.reshape(n_a, bs, hp, A_D_STATE)
    ssm_s_out = None
    yp = ys = None
    for i in range(DEPTH):
        kind, s = LAYER_KIND[i], LAYER_SLOT[i]
        mod_p = Mod(mod_p_arr, i, False, seq)
        mod_s = Mod(mod_s_arr, i, True)
        g1 = norm_g[i, 0]
        if kind == 0:
            w_zx = a_w_in[s, :, :A_ZX].astype(BF16)
            w_dt = jnp.pad(a_w_in[s, :, A_ZX:], ((0, 0), (0, LANE - A_N_HEADS)))
            weights = (w_zx, w_dt, a_conv_w[s], a_conv_b[s], a_dt_bias[s], a_log[s], a_d[s], a_norm_g[s], a_w_out[s])
            conv0 = jnp.zeros((bp, A_CONV - 1, A_CONV_CH), F32)
            ssm0 = jnp.zeros((bp, hp, A_D_STATE), F32)
            xp, cv, ss = _mamba_layer(xp, g1, mod_p, conv0, ("prompt", bp, ssm0), *weights, 1024, "p%d" % i)
            conv_p.append(cv)
            ssm_p.append(ss.reshape(bp, A_N_HEADS, A_HEAD_DIM, A_D_STATE))
            xs, cv, ssm_s_out = _mamba_layer(xs, g1, mod_s, state_a_conv[s], ("step", ssm_s_in, s, ssm_s_out),
                                             *weights, bs, "s%d" % i)
            conv_s.append(cv)
        elif kind == 1:
            w_uv = b_w_in[s].astype(BF16)
            uv = _gmlp_in_call(xp, g1, mod_p, w_uv, b_b_in[s], b_ln_g[s], b_ln_b[s], 512, BF16, "b_in_p%d" % i)
            xp = _gmlp_out_call(uv, b_w_s[s], b_b_s[s], b_w_out[s], xp, mod_p, 512, "b_out_p%d" % i)
            uv = _gmlp_in_call(xs, g1, mod_s, w_uv, b_b_in[s], b_ln_g[s], b_ln_b[s], bs, F32, "b_in_s%d" % i)
            xs = _gmlp_out_step_call(uv, b_w_s[s], b_b_s[s], b_w_out[s], xs, mod_s, "b_out_s%d" % i)
            v_s.append(uv[:, B_D:].reshape(bs, 1, B_D))
        else:
            pool0 = jnp.zeros((bp, C_STATE, d), F32)
            xp, pr = _pool_call(xp, g1, mod_p, pool0, c_w_g[s], c_scale[s], bp, 512, 0, "pool_p%d" % i)
            pool_p.append(pr)
            xs, pr = _pool_step_call(xs, g1, mod_s, state_c_pool[s], c_w_g[s], c_scale[s], PAST_LEN, "pool_s%d" % i)
            pool_s.append(pr)
        final = i == DEPTH - 1
        outp, outs = _moe_layer(xp, xs, norm_g[i, 1], mod_p, mod_s, router_w, router_b, e_w_gate, e_w_up, e_w_down, i,
                                final_g, final)
        xp, xs = outp[0], outs[0]
        if final:
            yp, ys = outp[1], outs[1]
    return (yp.reshape(bp, seq, d), ys.reshape(bs, 1, d), jnp.stack(conv_p), jnp.stack(ssm_p), jnp.stack(pool_p),
            jnp.stack(conv_s), ssm_s_out.reshape(state_a_ssm.shape), jnp.stack(pool_s), jnp.stack(v_s))
```

```python
import functools
import math

import numpy as np
import jax
import jax.numpy as jnp
from jax import lax
from jax.experimental import pallas as pl
from jax.experimental.pallas import tpu as pltpu

F32 = jnp.float32
BF16 = jnp.bfloat16
I32 = jnp.int32
EPS = 1e-6

LANE = 128
D_MODEL = 1024
DEPTH = 4
PAST_LEN = 16384
LAYER_KIND = (0, 1, 2, 0)
LAYER_SLOT = (0, 0, 0, 1)
A_D_INNER = 2 * D_MODEL
A_HEAD_DIM = 64
A_N_HEADS = A_D_INNER // A_HEAD_DIM
A_N_GROUPS = 8
A_HPG = A_N_HEADS // A_N_GROUPS
A_D_STATE = 128
A_GN = A_N_GROUPS * A_D_STATE
A_CONV = 4
A_CONV_CH = A_D_INNER + 2 * A_GN
A_ZX = A_D_INNER + A_CONV_CH
A_CHUNK = 128
A_GW = A_HPG * A_HEAD_DIM
B_D = 2 * D_MODEL
B_N_GROUPS = 8
B_GROUP_DIM = B_D // B_N_GROUPS
B_CHUNK = 128
C_WINDOWS = (2, 4, 8, 16)
C_GROUP_DIM = D_MODEL // len(C_WINDOWS)
C_STATE = max(C_WINDOWS) - 1
N_EXPERTS = 16
N_EXPERT_GROUPS = 4
EXPERTS_PER_GROUP = 4
D_EXPERT = D_MODEL // 2
PAIRS = ((0, 1), (0, 2), (0, 3), (1, 3), (1, 2), (2, 3))
N_BUCKETS = N_EXPERT_GROUPS * len(PAIRS)
BUCKET_ROWS = 32
ROW_PLANES = D_MODEL // LANE
OUT_PLANES = 2 * ROW_PLANES
EXPERT_TILE = 256
VMEM_LIMIT = 56 * 1024 * 1024


def _cparams(sem, vmem=VMEM_LIMIT):
    return pltpu.CompilerParams(dimension_semantics=sem, vmem_limit_bytes=vmem)


def _sigmoid(x):
    return 1.0 / (1.0 + jnp.exp(-x))


def _silu(x):
    return x * _sigmoid(x)


def _softplus(x):
    return jnp.maximum(x, 0.0) + jnp.log1p(jnp.exp(-jnp.abs(x)))


def _split2(a):
    hi = a.astype(BF16)
    lo = (a - hi.astype(F32)).astype(BF16)
    return hi, lo


def _split3(a):
    hi = a.astype(BF16)
    r = a - hi.astype(F32)
    mid = r.astype(BF16)
    lo = (r - mid.astype(F32)).astype(BF16)
    return hi, mid, lo


def _dot(a, b):
    return jnp.dot(a, b, preferred_element_type=F32)


def _dot_nt(a, b):
    return lax.dot_general(a, b, (((1,), (1,)), ((), ())), preferred_element_type=F32)


def _dot_tn(a, b):
    return lax.dot_general(a, b, (((0,), (0,)), ((), ())), preferred_element_type=F32)


def _dot3(a, b):
    a_hi, a_lo = _split2(a)
    b_hi, b_lo = _split2(b)
    return _dot(a_hi, b_hi) + (_dot(a_lo, b_hi) + _dot(a_hi, b_lo))


def _prenorm(x, g, sc, sh):
    ms = jnp.mean(x * x, axis=-1, keepdims=True)
    return (x * lax.rsqrt(ms + EPS) * g) * (1.0 + sc) + sh


def _mod_body(c_ref, w_ref, b_ref, o_ref):
    o_ref[...] = _dot3(_silu(c_ref[...]), w_ref[...]) + b_ref[...]


def _mod_call(c_all, w_mod, b_mod):
    nb, d = c_all.shape
    depth, _, n = w_mod.shape
    tn = 1536
    return pl.pallas_call(
        _mod_body,
        grid=(depth, n // tn),
        in_specs=[pl.BlockSpec((nb, d), lambda i, j: (0, 0)),
                  pl.BlockSpec((None, d, tn), lambda i, j: (i, 0, j)),
                  pl.BlockSpec((None, 1, tn), lambda i, j: (i, 0, j))],
        out_specs=pl.BlockSpec((None, nb, tn), lambda i, j: (i, 0, j)),
        out_shape=jax.ShapeDtypeStruct((depth, nb, n), F32),
        compiler_params=_cparams(("arbitrary", "arbitrary")),
        name="mod",
    )(c_all, w_mod, b_mod.reshape(depth, 1, n))


class Mod:
    def __init__(self, arr, layer, per_row, rows_per_seq=None):
        self.arr, self.layer, self.per_row, self.rows_per_seq = arr, layer, per_row, rows_per_seq

    def spec(self, which, tm, ngrid):
        layer = self.layer
        if self.per_row:
            if ngrid == 1:
                return pl.BlockSpec((None, tm, D_MODEL), lambda i: (layer, i, which))
            return pl.BlockSpec((None, tm, D_MODEL), lambda i, j: (layer, i, which))
        tiles = self.rows_per_seq // tm
        if ngrid == 1:
            return pl.BlockSpec((None, None, None, 1, D_MODEL), lambda i: (layer, i // tiles, which, 0, 0))
        return pl.BlockSpec((None, None, None, 1, D_MODEL), lambda i, j: (layer, i // tiles, which, 0, 0))


def _norm_mm_body(x_ref, g_ref, sc_ref, sh_ref, w_ref, o_ref, hn_ref):
    @pl.when(pl.program_id(1) == 0)
    def _():
        hn_ref[...] = _prenorm(x_ref[...], g_ref[...], sc_ref[...], sh_ref[...]).astype(BF16)

    o_ref[...] = _dot(hn_ref[...], w_ref[...]).astype(o_ref.dtype)


def _norm_mm_call(x, g, mod, which_sc, which_sh, w, tm, tn, out_dtype, name):
    t, d = x.shape
    n_cols = w.shape[1]
    return pl.pallas_call(
        _norm_mm_body,
        grid=(t // tm, n_cols // tn),
        in_specs=[pl.BlockSpec((tm, d), lambda i, j: (i, 0)),
                  pl.BlockSpec((1, d), lambda i, j: (0, 0)),
                  mod.spec(which_sc, tm, 2), mod.spec(which_sh, tm, 2),
                  pl.BlockSpec((d, tn), lambda i, j: (0, j))],
        out_specs=pl.BlockSpec((tm, tn), lambda i, j: (i, j)),
        out_shape=jax.ShapeDtypeStruct((t, n_cols), out_dtype),
        scratch_shapes=[pltpu.VMEM((tm, d), BF16)],
        compiler_params=_cparams(("arbitrary", "arbitrary")),
        name=name,
    )(x, g.reshape(1, d), mod.arr, mod.arr, w)


def _norm_mm3_body(x_ref, g_ref, sc_ref, sh_ref, w_ref, o_ref):
    hn = _prenorm(x_ref[...], g_ref[...], sc_ref[...], sh_ref[...])
    o_ref[...] = _dot3(hn, w_ref[...])


def _norm_mm3_call(x, g, mod, which_sc, which_sh, w, tm, name):
    t, d = x.shape
    n = w.shape[1]
    return pl.pallas_call(
        _norm_mm3_body,
        grid=(t // tm,),
        in_specs=[pl.BlockSpec((tm, d), lambda i: (i, 0)),
                  pl.BlockSpec((1, d), lambda i: (0, 0)),
                  mod.spec(which_sc, tm, 1), mod.spec(which_sh, tm, 1),
                  pl.BlockSpec((d, n), lambda i: (0, 0))],
        out_specs=pl.BlockSpec((tm, n), lambda i: (i, 0)),
        out_shape=jax.ShapeDtypeStruct((t, n), F32),
        compiler_params=_cparams(("arbitrary",)),
        name=name,
    )(x, g.reshape(1, d), mod.arr, mod.arr, w)


def _out_res_body(y_ref, w_ref, x_ref, gate_ref, o_ref, wbf_ref):
    @pl.when(pl.program_id(0) == 0)
    def _():
        wbf_ref[...] = w_ref[...].astype(BF16)

    o_ref[...] = x_ref[...] + gate_ref[...] * _dot(y_ref[...], wbf_ref[...])


def _out_res_call(y, w, x, mod, which_gate, tm, name):
    t, k = y.shape
    d = x.shape[1]
    return pl.pallas_call(
        _out_res_body,
        grid=(t // tm,),
        in_specs=[pl.BlockSpec((tm, k), lambda i: (i, 0)),
                  pl.BlockSpec((k, d), lambda i: (0, 0)),
                  pl.BlockSpec((tm, d), lambda i: (i, 0)),
                  mod.spec(which_gate, tm, 1)],
        out_specs=pl.BlockSpec((tm, d), lambda i: (i, 0)),
        out_shape=jax.ShapeDtypeStruct((t, d), F32),
        scratch_shapes=[pltpu.VMEM((k, d), BF16)],
        compiler_params=_cparams(("arbitrary",)),
        name=name,
    )(y, w, x, mod.arr)


def _head_expand():
    h = np.arange(LANE)[:, None]
    c = np.arange(A_D_INNER)[None, :]
    return jnp.asarray((c // A_HEAD_DIM == h).astype(np.float32), dtype=BF16)


def _gate_norm(y, z, ng):
    gated = y * _silu(z)
    ms = jnp.mean(gated * gated, axis=-1, keepdims=True)
    return gated * lax.rsqrt(ms + EPS) * ng


def _ssd_body(z_ref, xs_ref, bc_ref, dt_ref, cprev_ref, sprev_ref, cw_ref, cb_ref, dtb_ref, alog_ref,
              dx_ref, ng_ref, exp_ref, yn_ref, cnew_ref, snew_ref, tail_ref, act_ref):
    c = pl.program_id(1)
    q = A_CHUNK
    nt = A_CONV - 1

    @pl.when(c == 0)
    def _():
        tail_ref[0:nt, :] = cprev_ref[...]
        snew_ref[...] = sprev_ref[...]

    row = lax.broadcasted_iota(I32, (q, q), 0)
    col = lax.broadcasted_iota(I32, (q, q), 1)
    shifts = [jnp.where(row - col == nt - k, 1.0, 0.0).astype(BF16) for k in range(nt)]
    sub = lax.broadcasted_iota(I32, (8, 1), 0)
    cw = 512
    for j in range(A_CONV_CH // cw):
        sl = slice(j * cw, (j + 1) * cw)
        src = xs_ref if (j + 1) * cw <= A_D_INNER else bc_ref
        off = j * cw if src is xs_ref else j * cw - A_D_INNER
        x_bf = src[:, off:off + cw]
        conv = cb_ref[:, sl] + x_bf.astype(F32) * cw_ref[nt:nt + 1, sl]
        for k in range(nt):
            conv = conv + _dot(shifts[k], x_bf) * cw_ref[k:k + 1, sl]
        act_ref[:, sl] = _silu(conv)
        corr = jnp.zeros((8, cw), F32)
        for l in range(nt):
            c_l = sum(tail_ref[l + k:l + k + 1, sl] * cw_ref[k:k + 1, sl] for k in range(nt - l))
            corr = jnp.where(sub == l, c_l, corr)
        act_ref[0:8, sl] = _silu(conv[0:8, :] + corr)
        tail_ref[0:nt, sl] = x_bf[q - 8:q, :].astype(F32)[8 - nt:8, :]

    @pl.when(c == pl.num_programs(1) - 1)
    def _():
        cnew_ref[...] = tail_ref[0:nt, :]

    dt = _softplus(dt_ref[...] + dtb_ref[...])
    a = dt * (-jnp.exp(alog_ref[...]))
    row = lax.broadcasted_iota(I32, (q, q), 0)
    col = lax.broadcasted_iota(I32, (q, q), 1)
    causal = row >= col
    tril = jnp.where(causal, 1.0, 0.0).astype(BF16)
    a_hi, a_mid, a_lo = _split3(a)
    cs = _dot(tril, a_hi) + (_dot(tril, a_mid) + _dot(tril, a_lo))
    cs_t = cs.T
    dt_t = dt.T
    cs_last = cs[q - 1:q, :]
    ecs = jnp.exp(cs)
    wend = jnp.exp(cs_last - cs) * dt
    st_hi, st_lo = _split2(jnp.concatenate([ecs, wend], axis=0))
    st_x = _dot(st_hi, exp_ref[...]) + _dot(st_lo, exp_ref[...])
    lane_head = lax.broadcasted_iota(I32, (q, A_GW), 1) // A_HEAD_DIM

    for g in range(A_N_GROUPS):
        gsl = slice(g * A_GW, (g + 1) * A_GW)
        b_g = act_ref[:, A_D_INNER + g * A_D_STATE:A_D_INNER + (g + 1) * A_D_STATE].astype(BF16)
        c_g = act_ref[:, A_D_INNER + A_GN + g * A_D_STATE:A_D_INNER + A_GN + (g + 1) * A_D_STATE].astype(BF16)
        x_g = act_ref[:, gsl]
        x_bf = x_g.astype(BF16)
        h_g = snew_ref[gsl, :]
        cb = _dot_nt(c_g, b_g)
        y = jnp.zeros((q, A_GW), F32)
        for r in range(A_HPG):
            h = g * A_HPG + r
            seg = cs[:, h:h + 1] - cs_t[h:h + 1, :]
            decay = jnp.exp(jnp.where(causal, seg, -1e30))
            wm = (cb * decay * dt_t[h:h + 1, :]).astype(BF16)
            y = y + _dot(wm, jnp.where(lane_head == r, x_bf, jnp.zeros_like(x_bf)))
        y = y + st_x[0:q, gsl] * _dot_nt(c_g, h_g.astype(BF16)) + dx_ref[:, gsl] * x_g
        yn_ref[:, gsl] = _gate_norm(y, z_ref[:, gsl].astype(F32), ng_ref[:, gsl]).astype(BF16)
        s_new = _dot_tn((x_g * st_x[q:2 * q, gsl]).astype(BF16), b_g)
        for r in range(A_HPG):
            h = g * A_HPG + r
            rsl = slice(g * A_GW + r * A_HEAD_DIM, g * A_GW + (r + 1) * A_HEAD_DIM)
            keep = jnp.exp(cs[q - 1:q, h:h + 1])
            snew_ref[rsl, :] = snew_ref[rsl, :] * keep + s_new[r * A_HEAD_DIM:(r + 1) * A_HEAD_DIM, :]


def _ssd_call(proj, dt_raw, conv_prev, ssm_prev, conv_w, conv_b, dt_bias, a_log, d_x, norm_g, nb, name):
    t = proj.shape[0]
    q = A_CHUNK
    nc = t // nb // q
    hp = A_N_HEADS * A_HEAD_DIM
    row = lambda b, c: (b * nc + c, 0)
    full = lambda shape: pl.BlockSpec(shape, lambda b, c: (0,) * len(shape))
    return pl.pallas_call(
        _ssd_body,
        grid=(nb, nc),
        in_specs=[pl.BlockSpec((q, A_D_INNER), lambda b, c: (b * nc + c, 0)),
                  pl.BlockSpec((q, A_D_INNER), lambda b, c: (b * nc + c, 1)),
                  pl.BlockSpec((q, 2 * A_GN), lambda b, c: (b * nc + c, 2)),
                  pl.BlockSpec((q, LANE), row),
                  pl.BlockSpec((None, A_CONV - 1, A_CONV_CH), lambda b, c: (b, 0, 0)),
                  pl.BlockSpec((None, hp, A_D_STATE), lambda b, c: (b, 0, 0)),
                  full((A_CONV, A_CONV_CH)), full((1, A_CONV_CH)), full((1, LANE)), full((1, LANE)),
                  full((1, A_D_INNER)), full((1, A_D_INNER)), full((LANE, A_D_INNER))],
        out_specs=[pl.BlockSpec((q, A_D_INNER), row),
                   pl.BlockSpec((None, A_CONV - 1, A_CONV_CH), lambda b, c: (b, 0, 0)),
                   pl.BlockSpec((None, hp, A_D_STATE), lambda b, c: (b, 0, 0))],
        out_shape=[jax.ShapeDtypeStruct((t, A_D_INNER), BF16),
                   jax.ShapeDtypeStruct((nb, A_CONV - 1, A_CONV_CH), F32),
                   jax.ShapeDtypeStruct((nb, hp, A_D_STATE), F32)],
        scratch_shapes=[pltpu.VMEM((8, A_CONV_CH), F32), pltpu.VMEM((q, A_CONV_CH), F32)],
        compiler_params=_cparams(("arbitrary", "arbitrary")),
        name=name,
    )(proj, proj, proj, dt_raw, conv_prev, ssm_prev, conv_w, conv_b, dt_bias, a_log, d_x, norm_g, _head_expand())


def _ssd_step_body(z_ref, xs_ref, bc_ref, dt_ref, cprev_ref, sprev_ref, cw_ref, cb_ref, dtb_ref, alog_ref,
                   dx_ref, ng_ref, exp_ref, yn_ref, cnew_ref, snew_ref, da_ref, y_ref):
    bt = z_ref.shape[0]
    cur = jnp.concatenate([xs_ref[...], bc_ref[...]], axis=1).astype(F32)
    conv = cb_ref[...] + cur * cw_ref[A_CONV - 1:A_CONV, :]
    for k in range(A_CONV - 1):
        conv = conv + cprev_ref[:, k, :] * cw_ref[k:k + 1, :]
    for k in range(A_CONV - 2):
        cnew_ref[:, k, :] = cprev_ref[:, k + 1, :]
    cnew_ref[:, A_CONV - 2, :] = cur
    act = _silu(conv)
    xs = act[:, 0:A_D_INNER]
    bm = act[:, A_D_INNER:A_D_INNER + A_GN]
    cm = act[:, A_D_INNER + A_GN:A_CONV_CH]
    dt = _softplus(dt_ref[...] + dtb_ref[...])
    da_ref[...] = jnp.exp(dt * (-jnp.exp(alog_ref[...])))
    dt_hi, dt_lo = _split2(dt)
    dt_x = _dot(dt_hi, exp_ref[...]) + _dot(dt_lo, exp_ref[...])
    xdt = xs * dt_x
    rows = lax.broadcasted_iota(I32, (bt, 1), 0)
    y_ref[...] = jnp.zeros_like(y_ref)

    def per_seq(j, carry):
        mine = rows == j
        xdt_j = jnp.where(mine, xdt, 0.0)
        da_j = da_ref[pl.ds(j, 1), :]
        for g in range(A_N_GROUPS):
            gsl = slice(g * A_GW, (g + 1) * A_GW)
            nsl = slice(g * A_D_STATE, (g + 1) * A_D_STATE)
            xh, xl = _split2(xdt_j[:, gsl])
            bh, bl = _split2(bm[:, nsl])
            outer = _dot_tn(xh, bh) + (_dot_tn(xl, bh) + _dot_tn(xh, bl))
            for r in range(A_HPG):
                h = g * A_HPG + r
                rsl = slice(r * A_HEAD_DIM, (r + 1) * A_HEAD_DIM)
                hsl = slice(g * A_GW + r * A_HEAD_DIM, g * A_GW + (r + 1) * A_HEAD_DIM)
                snew_ref[j, hsl, :] = sprev_ref[j, hsl, :] * da_j[:, h:h + 1] + outer[rsl, :]
            h_new = snew_ref[j, gsl, :]
            hh, hl = _split2(h_new)
            ch, cl = _split2(cm[:, nsl])
            yg = _dot_nt(ch, hh) + (_dot_nt(cl, hh) + _dot_nt(ch, hl))
            y_ref[:, gsl] = y_ref[:, gsl] + jnp.where(mine, yg, 0.0)
        return carry

    lax.fori_loop(0, bt, per_seq, 0)
    y = y_ref[...] + dx_ref[...] * xs
    z = z_ref[...].astype(F32)
    for g in range(A_N_GROUPS):
        gsl = slice(g * A_GW, (g + 1) * A_GW)
        yn_ref[:, gsl] = _gate_norm(y[:, gsl], z[:, gsl], ng_ref[:, gsl]).astype(BF16)


_SSD_STEP_INPUTS = 13


def _ssd_step_aliased_body(*refs):
    _ssd_step_body(*refs[:_SSD_STEP_INPUTS], *refs[_SSD_STEP_INPUTS + 1:])


def _ssd_step_call(proj, dt_raw, conv_prev, ssm_all, slot, ssm_out, conv_w, conv_b, dt_bias, a_log, d_x, norm_g,
                   name):
    nb = proj.shape[0]
    bt = 8
    hp = A_N_HEADS * A_HEAD_DIM
    full = lambda shape: pl.BlockSpec(shape, lambda i: (0,) * len(shape))
    state_spec = pl.BlockSpec((None, bt, hp, A_D_STATE), lambda i: (slot, i, 0, 0))
    in_specs = [pl.BlockSpec((bt, A_D_INNER), lambda i: (i, 0)),
                pl.BlockSpec((bt, A_D_INNER), lambda i: (i, 1)),
                pl.BlockSpec((bt, 2 * A_GN), lambda i: (i, 2)),
                pl.BlockSpec((bt, LANE), lambda i: (i, 0)),
                pl.BlockSpec((bt, A_CONV - 1, A_CONV_CH), lambda i: (i, 0, 0)),
                state_spec,
                full((A_CONV, A_CONV_CH)), full((1, A_CONV_CH)), full((1, LANE)), full((1, LANE)),
                full((1, A_D_INNER)), full((1, A_D_INNER)), full((LANE, A_D_INNER))]
    args = [proj, proj, proj, dt_raw, conv_prev, ssm_all, conv_w, conv_b, dt_bias, a_log, d_x, norm_g, _head_expand()]
    assert len(args) == _SSD_STEP_INPUTS
    aliases = {}
    if ssm_out is not None:
        in_specs.append(pl.BlockSpec(memory_space=pl.ANY))
        args.append(ssm_out)
        aliases = {_SSD_STEP_INPUTS: 2}
    return pl.pallas_call(
        _ssd_step_body if ssm_out is None else _ssd_step_aliased_body,
        grid=(nb // bt,),
        in_specs=in_specs,
        out_specs=[pl.BlockSpec((bt, A_D_INNER), lambda i: (i, 0)),
                   pl.BlockSpec((bt, A_CONV - 1, A_CONV_CH), lambda i: (i, 0, 0)),
                   state_spec],
        out_shape=[jax.ShapeDtypeStruct((nb, A_D_INNER), BF16),
                   jax.ShapeDtypeStruct((nb, A_CONV - 1, A_CONV_CH), F32),
                   jax.ShapeDtypeStruct(ssm_all.shape, F32)],
        scratch_shapes=[pltpu.VMEM((bt, LANE), F32), pltpu.VMEM((bt, A_D_INNER), F32)],
        input_output_aliases=aliases,
        compiler_params=_cparams(("arbitrary",)),
        name=name,
    )(*args)


def _gmlp_in_body(x_ref, g_ref, sc_ref, sh_ref, w_ref, b_ref, lg_ref, lb_ref, o_ref, hn_ref):
    j = pl.program_id(1)

    @pl.when(j == 0)
    def _():
        hn_ref[...] = _prenorm(x_ref[...], g_ref[...], sc_ref[...], sh_ref[...]).astype(BF16)

    uv = jax.nn.gelu(_dot(hn_ref[...], w_ref[...]) + b_ref[...], approximate=True)

    @pl.when(j == 0)
    def _():
        o_ref[...] = uv.astype(o_ref.dtype)

    @pl.when(j == 1)
    def _():
        vc = uv - jnp.mean(uv, axis=-1, keepdims=True)
        var = jnp.mean(vc * vc, axis=-1, keepdims=True)
        o_ref[...] = (vc * lax.rsqrt(var + EPS) * lg_ref[...] + lb_ref[...]).astype(o_ref.dtype)


def _gmlp_in_call(x, g, mod, w, b, ln_g, ln_b, tm, out_dtype, name):
    t, d = x.shape
    return pl.pallas_call(
        _gmlp_in_body,
        grid=(t // tm, 2),
        in_specs=[pl.BlockSpec((tm, d), lambda i, j: (i, 0)),
                  pl.BlockSpec((1, d), lambda i, j: (0, 0)),
                  mod.spec(1, tm, 2), mod.spec(0, tm, 2),
                  pl.BlockSpec((d, B_D), lambda i, j: (0, j)),
                  pl.BlockSpec((1, B_D), lambda i, j: (0, j)),
                  pl.BlockSpec((1, B_D), lambda i, j: (0, 0)),
                  pl.BlockSpec((1, B_D), lambda i, j: (0, 0))],
        out_specs=pl.BlockSpec((tm, B_D), lambda i, j: (i, j)),
        out_shape=jax.ShapeDtypeStruct((t, 2 * B_D), out_dtype),
        scratch_shapes=[pltpu.VMEM((tm, d), BF16)],
        compiler_params=_cparams(("arbitrary", "arbitrary")),
        name=name,
    )(x, g.reshape(1, d), mod.arr, mod.arr, w, b.reshape(1, -1), ln_g.reshape(1, -1), ln_b.reshape(1, -1))


def _gmlp_out_body(u_ref, v_ref, ws_ref, bs_ref, w_ref, x_ref, gate_ref, o_ref, wbf_ref, wsbf_ref, m_ref):
    q = B_CHUNK

    @pl.when(pl.program_id(0) == 0)
    def _():
        wbf_ref[...] = w_ref[...].astype(BF16)
        causal = lax.broadcasted_iota(I32, (q, q), 0) >= lax.broadcasted_iota(I32, (q, q), 1)
        for g in range(B_N_GROUPS):
            wsbf_ref[g] = jnp.where(causal, ws_ref[g], 0.0).astype(BF16)

    for ci in range(u_ref.shape[0] // q):
        rsl = slice(ci * q, (ci + 1) * q)
        for g in range(B_N_GROUPS):
            gsl = slice(g * B_GROUP_DIM, (g + 1) * B_GROUP_DIM)
            mixed = _dot(wsbf_ref[g], v_ref[rsl, gsl].astype(BF16)) + bs_ref[:, g:g + 1]
            m_ref[rsl, gsl] = (u_ref[rsl, gsl].astype(F32) * mixed).astype(BF16)
    o_ref[...] = x_ref[...] + gate_ref[...] * _dot(m_ref[...], wbf_ref[...])


def _gmlp_out_call(uv, w_s, b_s, w_out, x, mod, tm, name):
    t, d = x.shape
    q = B_CHUNK
    return pl.pallas_call(
        _gmlp_out_body,
        grid=(t // tm,),
        in_specs=[pl.BlockSpec((tm, B_D), lambda i: (i, 0)),
                  pl.BlockSpec((tm, B_D), lambda i: (i, 1)),
                  pl.BlockSpec((B_N_GROUPS, q, q), lambda i: (0, 0, 0)),
                  pl.BlockSpec((q, B_N_GROUPS), lambda i: (0, 0)),
                  pl.BlockSpec((B_D, d), lambda i: (0, 0)),
                  pl.BlockSpec((tm, d), lambda i: (i, 0)),
                  mod.spec(2, tm, 1)],
        out_specs=pl.BlockSpec((tm, d), lambda i: (i, 0)),
        out_shape=jax.ShapeDtypeStruct((t, d), F32),
        scratch_shapes=[pltpu.VMEM((B_D, d), BF16), pltpu.VMEM((B_N_GROUPS, q, q), BF16),
                        pltpu.VMEM((tm, B_D), BF16)],
        compiler_params=_cparams(("arbitrary",)),
        name=name,
    )(uv, uv, w_s, b_s.T, w_out, x, mod.arr)


def _gmlp_out_step_body(u_ref, v_ref, wd_ref, bd_ref, w_ref, x_ref, gate_ref, o_ref):
    mixed = v_ref[...] * wd_ref[...] + bd_ref[...]
    m = (u_ref[...] * mixed).astype(BF16)
    o_ref[...] = x_ref[...] + gate_ref[...] * _dot(m, w_ref[...].astype(BF16))


def _gmlp_out_step_call(uv, w_s, b_s, w_out, x, mod, name):
    t, d = x.shape
    wd = jnp.repeat(w_s[:, 0, 0], B_GROUP_DIM).reshape(1, B_D)
    bd = jnp.repeat(b_s[:, 0], B_GROUP_DIM).reshape(1, B_D)
    return pl.pallas_call(
        _gmlp_out_step_body,
        grid=(1,),
        in_specs=[pl.BlockSpec((t, B_D), lambda i: (0, 0)),
                  pl.BlockSpec((t, B_D), lambda i: (0, 1)),
                  pl.BlockSpec((1, B_D), lambda i: (0, 0)),
                  pl.BlockSpec((1, B_D), lambda i: (0, 0)),
                  pl.BlockSpec((B_D, d), lambda i: (0, 0)),
                  pl.BlockSpec((t, d), lambda i: (0, 0)),
                  mod.spec(2, t, 1)],
        out_specs=pl.BlockSpec((t, d), lambda i: (0, 0)),
        out_shape=jax.ShapeDtypeStruct((t, d), F32),
        compiler_params=_cparams(("arbitrary",)),
        name=name,
    )(uv, uv, wd, bd, w_out, x, mod.arr)


def _pool_matmul(pooled, wg_ref):
    outs = []
    for gi in range(len(C_WINDOWS)):
        gsl = slice(gi * C_GROUP_DIM, (gi + 1) * C_GROUP_DIM)
        outs.append(_dot(pooled[:, gsl].astype(BF16), wg_ref[gi].astype(BF16)))
    return jnp.concatenate(outs, axis=-1)


def _pool_body(x_ref, g_ref, sc_ref, sh_ref, gate_ref, prev_ref, wg_ref, scale_ref, o_ref, pnew_ref, hp_ref,
               *, tiles_per_seq, start):
    i = pl.program_id(0)
    tm = x_ref.shape[0]
    top = 16
    ti = i % tiles_per_seq

    @pl.when(ti == 0)
    def _():
        hp_ref[top - C_STATE:top, :] = prev_ref[...]

    hn = _prenorm(x_ref[...], g_ref[...], sc_ref[...], sh_ref[...])
    hp_ref[top:top + tm, :] = hn
    pos = start + ti * tm + lax.broadcasted_iota(I32, (tm, 1), 0)
    outs = []
    for gi, w in enumerate(C_WINDOWS):
        gsl = slice(gi * C_GROUP_DIM, (gi + 1) * C_GROUP_DIM)
        acc = hn[:, gsl]
        for k in range(1, w):
            acc = acc + hp_ref[top - k:top - k + tm, gsl]
        cnt = jnp.minimum(pos + 1, w).astype(F32)
        outs.append(acc / cnt - hn[:, gsl])
    y = _pool_matmul(jnp.concatenate(outs, axis=-1), wg_ref) * scale_ref[...]
    o_ref[...] = x_ref[...] + gate_ref[...] * y
    hist = hp_ref[top + tm - C_STATE:top + tm, :]
    hp_ref[top - C_STATE:top, :] = hist

    @pl.when(ti == tiles_per_seq - 1)
    def _():
        pnew_ref[...] = hist


def _pool_call(x, g, mod, prev, w_g, scale, nb, tm, start, name):
    t, d = x.shape
    tiles = t // nb // tm
    ng = len(C_WINDOWS)
    return pl.pallas_call(
        functools.partial(_pool_body, tiles_per_seq=tiles, start=start),
        grid=(t // tm,),
        in_specs=[pl.BlockSpec((tm, d), lambda i: (i, 0)),
                  pl.BlockSpec((1, d), lambda i: (0, 0)),
                  mod.spec(1, tm, 1), mod.spec(0, tm, 1), mod.spec(2, tm, 1),
                  pl.BlockSpec((None, C_STATE, d), lambda i: (i // tiles, 0, 0)),
                  pl.BlockSpec((ng, C_GROUP_DIM, C_GROUP_DIM), lambda i: (0, 0, 0)),
                  pl.BlockSpec((1, d), lambda i: (0, 0))],
        out_specs=[pl.BlockSpec((tm, d), lambda i: (i, 0)),
                   pl.BlockSpec((None, C_STATE, d), lambda i: (i // tiles, 0, 0))],
        out_shape=[jax.ShapeDtypeStruct((t, d), F32), jax.ShapeDtypeStruct((nb, C_STATE, d), F32)],
        scratch_shapes=[pltpu.VMEM((16 + tm, d), F32)],
        compiler_params=_cparams(("arbitrary",)),
        name=name,
    )(x, g.reshape(1, d), mod.arr, mod.arr, mod.arr, prev, w_g, scale.reshape(1, d))


def _pool_step_body(x_ref, g_ref, sc_ref, sh_ref, gate_ref, prev_ref, wg_ref, scale_ref, o_ref, pnew_ref, *, start):
    hn = _prenorm(x_ref[...], g_ref[...], sc_ref[...], sh_ref[...])
    outs = []
    for gi, w in enumerate(C_WINDOWS):
        gsl = slice(gi * C_GROUP_DIM, (gi + 1) * C_GROUP_DIM)
        acc = hn[:, gsl]
        for k in range(1, w):
            acc = acc + prev_ref[:, C_STATE - k, gsl]
        outs.append(acc / float(min(start + 1, w)) - hn[:, gsl])
    y = _pool_matmul(jnp.concatenate(outs, axis=-1), wg_ref) * scale_ref[...]
    o_ref[...] = x_ref[...] + gate_ref[...] * y
    for k in range(C_STATE - 1):
        pnew_ref[:, k, :] = prev_ref[:, k + 1, :]
    pnew_ref[:, C_STATE - 1, :] = hn


def _pool_step_call(x, g, mod, prev, w_g, scale, start, name):
    t, d = x.shape
    bt = 32
    ng = len(C_WINDOWS)
    return pl.pallas_call(
        functools.partial(_pool_step_body, start=start),
        grid=(t // bt,),
        in_specs=[pl.BlockSpec((bt, d), lambda i: (i, 0)),
                  pl.BlockSpec((1, d), lambda i: (0, 0)),
                  mod.spec(1, bt, 1), mod.spec(0, bt, 1), mod.spec(2, bt, 1),
                  pl.BlockSpec((bt, C_STATE, d), lambda i: (i, 0, 0)),
                  pl.BlockSpec((ng, C_GROUP_DIM, C_GROUP_DIM), lambda i: (0, 0, 0)),
                  pl.BlockSpec((1, d), lambda i: (0, 0))],
        out_specs=[pl.BlockSpec((bt, d), lambda i: (i, 0)),
                   pl.BlockSpec((bt, C_STATE, d), lambda i: (i, 0, 0))],
        out_shape=[jax.ShapeDtypeStruct((t, d), F32), jax.ShapeDtypeStruct((t, C_STATE, d), F32)],
        compiler_params=_cparams(("arbitrary",)),
        name=name,
    )(x, g.reshape(1, d), mod.arr, mod.arr, mod.arr, prev, w_g, scale.reshape(1, d))


def _route_rows(s, b):
    npg = EXPERTS_PER_GROUP
    gscore = []
    for q in range(N_EXPERT_GROUPS):
        v = b[q * npg:(q + 1) * npg]
        best = None
        for i in range(npg):
            for j in range(i + 1, npg):
                best = v[i] + v[j] if best is None else jnp.maximum(best, v[i] + v[j])
        gscore.append(best)
    gsel = jnp.zeros_like(gscore[0], dtype=I32)
    gbest = gscore[0]
    for q in range(1, N_EXPERT_GROUPS):
        better = gscore[q] > gbest
        gsel = jnp.where(better, q, gsel)
        gbest = jnp.where(better, gscore[q], gbest)
    vb, vs = [], []
    for k in range(npg):
        bk, sk = b[k], s[k]
        for q in range(1, N_EXPERT_GROUPS):
            bk = jnp.where(gsel == q, b[q * npg + k], bk)
            sk = jnp.where(gsel == q, s[q * npg + k], sk)
        vb.append(bk)
        vs.append(sk)
    i1 = jnp.zeros_like(gsel)
    m1 = vb[0]
    for k in range(1, npg):
        better = vb[k] > m1
        i1 = jnp.where(better, k, i1)
        m1 = jnp.where(better, vb[k], m1)
    i2 = jnp.full_like(gsel, -1)
    m2 = jnp.zeros_like(m1)
    for k in range(npg):
        better = (i1 != k) & ((i2 < 0) | (vb[k] > m2))
        i2 = jnp.where(better, k, i2)
        m2 = jnp.where(better, vb[k], m2)
    s1 = vs[0]
    s2 = vs[0]
    for k in range(1, npg):
        s1 = jnp.where(i1 == k, vs[k], s1)
        s2 = jnp.where(i2 == k, vs[k], s2)
    w1 = s1 / (s1 + s2)
    w2 = s2 / (s1 + s2)
    lo = jnp.minimum(i1, i2)
    hi = jnp.maximum(i1, i2)
    pair = jnp.zeros_like(gsel)
    for p, (a, c) in enumerate(PAIRS):
        pair = jnp.where((lo == a) & (hi == c), p, pair)
    first_is_lo = i1 < i2
    return (gsel * len(PAIRS) + pair, jnp.where(first_is_lo, w1, w2), jnp.where(first_is_lo, w2, w1))


def _route_body(x_ref, g_ref, sc_ref, sh_ref, rw_ref, rb_ref, cnt_in_ref, *rest, aliased):
    if aliased:
        rest = rest[4:]
    rows_ref, gates_ref, bucket_ref, rank_ref, cnt_ref = rest
    tm = x_ref.shape[0]

    @pl.when(pl.program_id(0) == 0)
    def _():
        cnt_ref[...] = cnt_in_ref[...]

    hn = _prenorm(x_ref[...], g_ref[...], sc_ref[...], sh_ref[...])
    logits_t = _dot3(hn, rw_ref[...]).T
    scores = _sigmoid(logits_t[0:N_EXPERTS, :])
    biased = scores + rb_ref[0:N_EXPERTS, :]
    bucket, w_lo, w_hi = _route_rows([scores[e:e + 1, :] for e in range(N_EXPERTS)],
                                     [biased[e:e + 1, :] for e in range(N_EXPERTS)])
    bucket_ref[...] = bucket
    onehot = (lax.broadcasted_iota(I32, (BUCKET_ROWS, tm), 0) == bucket).astype(F32)
    before = (lax.broadcasted_iota(I32, (tm, tm), 0) < lax.broadcasted_iota(I32, (tm, tm), 1)).astype(BF16)
    earlier = _dot(onehot.astype(BF16), before) + cnt_ref[:, 0:1]
    rank_ref[...] = jnp.sum(onehot * earlier, axis=0, keepdims=True).astype(I32)
    cnt_ref[...] = cnt_ref[...] + jnp.sum(onehot, axis=1, keepdims=True)

    sub = lax.broadcasted_iota(I32, (LANE, tm), 0)
    gate_t = jnp.where(sub == 0, w_lo, jnp.where(sub == 1, w_hi, 0.0))
    gates_ref[...] = gate_t.T
    for j in range(ROW_PLANES):
        rows_ref[pl.ds(j, tm, stride=ROW_PLANES), :] = hn[:, j * LANE:(j + 1) * LANE]


def _route_call(x, g, mod, router_w, router_b, counts, tm, total_rows, row_offset, prior, name):
    t, d = x.shape
    blk0 = row_offset // tm
    rw = jnp.pad(router_w, ((0, 0), (0, LANE - N_EXPERTS)))
    rb = jnp.pad(router_b.reshape(-1, 1), ((0, LANE - N_EXPERTS), (0, 0)))
    in_specs = [pl.BlockSpec((tm, d), lambda i: (i, 0)),
                pl.BlockSpec((1, d), lambda i: (0, 0)),
                mod.spec(4, tm, 1), mod.spec(3, tm, 1),
                pl.BlockSpec((d, LANE), lambda i: (0, 0)),
                pl.BlockSpec((LANE, 1), lambda i: (0, 0)),
                pl.BlockSpec((BUCKET_ROWS, LANE), lambda i: (0, 0))]
    args = [x, g.reshape(1, d), mod.arr, mod.arr, rw, rb, counts]
    aliases = {}
    if prior is not None:
        in_specs += [pl.BlockSpec(memory_space=pl.ANY)] * len(prior)
        aliases = {len(args) + k: k for k in range(len(prior))}
        args += list(prior)
    return pl.pallas_call(
        functools.partial(_route_body, aliased=prior is not None),
        grid=(t // tm,),
        in_specs=in_specs,
        out_specs=[pl.BlockSpec((tm * ROW_PLANES, LANE), lambda i: (blk0 + i, 0)),
                   pl.BlockSpec((tm, LANE), lambda i: (blk0 + i, 0)),
                   pl.BlockSpec((1, tm), lambda i: (0, blk0 + i)),
                   pl.BlockSpec((1, tm), lambda i: (0, blk0 + i)),
                   pl.BlockSpec((BUCKET_ROWS, LANE), lambda i: (0, 0))],
        out_shape=[jax.ShapeDtypeStruct((total_rows * ROW_PLANES, LANE), F32),
                   jax.ShapeDtypeStruct((total_rows, LANE), F32),
                   jax.ShapeDtypeStruct((1, total_rows), I32),
                   jax.ShapeDtypeStruct((1, total_rows), I32),
                   jax.ShapeDtypeStruct((BUCKET_ROWS, LANE), F32)],
        input_output_aliases=aliases,
        compiler_params=_cparams(("arbitrary",)),
        name=name,
    )(*args)


def _invert_body(dest_ref, gather_init_ref, scatter_init_ref, gather_ref, scatter_ref):
    pltpu.sync_copy(gather_init_ref, gather_ref)
    pltpu.sync_copy(scatter_init_ref, scatter_ref)

    def put(t, carry):
        d = dest_ref[t]
        gather_ref[d] = t
        scatter_ref[EXPERT_TILE + d] = t
        return carry

    lax.fori_loop(0, dest_ref.shape[0], put, 0, unroll=8)


def _invert_call(dest, n_slots, name):
    total = dest.shape[0]
    gather_init = jnp.zeros((n_slots,), I32)
    scatter_init = total + jnp.arange(n_slots + EXPERT_TILE, dtype=I32) % EXPERT_TILE
    smem = pl.BlockSpec(memory_space=pltpu.SMEM)
    return pl.pallas_call(
        _invert_body,
        in_specs=[smem, pl.BlockSpec(memory_space=pl.ANY), pl.BlockSpec(memory_space=pl.ANY)],
        out_specs=[smem, smem],
        out_shape=[jax.ShapeDtypeStruct(gather_init.shape, I32), jax.ShapeDtypeStruct(scatter_init.shape, I32)],
        name=name,
    )(dest, gather_init, scatter_init)


def _unrolled(lo, hi, body, carry):
    for b in range(lo, hi):
        carry = body(b, carry)
    return carry


def _experts_body(elo_ref, ehi_ref, fresh_ref, valid_ref, gather_ref, scatter_ref, rows_ref, wg0_ref, wg1_ref,
                  wu0_ref, wu1_ref, wd0_ref, wd1_ref, y_ref, wg_ref, wu_ref, wd_ref, xb_ref, yb_ref, gsem, ssem,
                  *, token_words):
    i = pl.program_id(0)
    n = pl.num_programs(0)
    f = D_EXPERT
    tile = EXPERT_TILE
    words = tile * ROW_PLANES
    out_words = tile * OUT_PLANES
    group = 8

    def start_gather(step, slot, loop):
        def body(b, carry):
            for k in range(group):
                r = b * group + k
                tok = gather_ref[step * tile + r]
                pltpu.make_async_copy(rows_ref.at[pl.ds(pl.multiple_of(tok * ROW_PLANES, ROW_PLANES), ROW_PLANES)],
                                      xb_ref.at[slot, pl.ds(pl.multiple_of(r * ROW_PLANES, ROW_PLANES), ROW_PLANES)],
                                      gsem.at[slot]).start(priority=k % 2)
            return carry

        loop(0, tile // group, body, 0)

    def start_scatter(block, slot, loop):
        def body(b, carry):
            for k in range(group):
                r = b * group + k
                tok = scatter_ref[block * tile + r]
                pltpu.make_async_copy(yb_ref.at[slot, pl.ds(pl.multiple_of(r * OUT_PLANES, OUT_PLANES), OUT_PLANES)],
                                      y_ref.at[pl.ds(pl.multiple_of(tok * OUT_PLANES, OUT_PLANES), OUT_PLANES)],
                                      ssem.at[slot]).start(priority=k % 2)
            return carry

        loop(0, tile // group, body, 0)

    def wait_gather(slot):
        pltpu.make_async_copy(rows_ref.at[pl.ds(0, words)], xb_ref.at[slot], gsem.at[slot]).wait()

    def wait_scatter(slot):
        pltpu.make_async_copy(yb_ref.at[slot], y_ref.at[pl.ds(0, out_words)], ssem.at[slot]).wait()

    @pl.when(i == 0)
    def _():
        yb_ref[1] = jnp.zeros((out_words, LANE), F32)
        spare = pltpu.make_async_copy(yb_ref.at[1], y_ref.at[pl.ds(token_words, out_words)], ssem.at[1])
        spare.start()
        spare.wait()

    @pl.when(fresh_ref[i] == 1)
    def _():
        wg_ref[:, 0:f] = wg0_ref[...].astype(BF16)
        wg_ref[:, f:2 * f] = wg1_ref[...].astype(BF16)
        wu_ref[:, 0:f] = wu0_ref[...].astype(BF16)
        wu_ref[:, f:2 * f] = wu1_ref[...].astype(BF16)
        wd_ref[0:f, :] = wd0_ref[...].astype(BF16)
        wd_ref[f:2 * f, :] = wd1_ref[...].astype(BF16)

    nxt = jnp.minimum(i + 1, n - 1)
    more = jnp.logical_and(i + 1 < n, valid_ref[nxt] == 1)

    @pl.when(valid_ref[i] == 1)
    def _():
        slot = i % 2

        @pl.when(i == 0)
        def _():
            start_gather(0, 0, lax.fori_loop)

        wait_gather(slot)
        start_gather(nxt, 1 - slot, _unrolled)
        start_scatter(i, 1 - slot, _unrolled)
        x = jnp.concatenate([xb_ref[slot, pl.ds(j, tile, stride=ROW_PLANES), :].astype(BF16)
                             for j in range(ROW_PLANES)], axis=-1)
        act = (_silu(_dot(x, wg_ref[...])) * _dot(x, wu_ref[...])).astype(BF16)
        y_lo = _dot(act[:, 0:f], wd_ref[0:f, :])
        y_hi = _dot(act[:, f:2 * f], wd_ref[f:2 * f, :])

        @pl.when(i >= 1)
        def _():
            wait_scatter(slot)

        for j in range(ROW_PLANES):
            yb_ref[slot, pl.ds(j, tile, stride=OUT_PLANES), :] = y_lo[:, j * LANE:(j + 1) * LANE]
            yb_ref[slot, pl.ds(ROW_PLANES + j, tile, stride=OUT_PLANES), :] = y_hi[:, j * LANE:(j + 1) * LANE]

        @pl.when(jnp.logical_not(more))
        def _():
            wait_gather(1 - slot)
            wait_scatter(1 - slot)
            start_scatter(i + 1, slot, lax.fori_loop)
            wait_scatter(slot)


def _experts_call(e_lo, e_hi, fresh, valid, gather_idx, scatter_idx, rows, w_gate, w_up, w_down, layer, name):
    n_tiles = e_lo.shape[0]
    tile = EXPERT_TILE
    d, f = D_MODEL, D_EXPERT
    tokens = rows.shape[0] // ROW_PLANES
    lo = lambda i, elo, ehi, *_: (layer, elo[i], 0, 0)
    hi = lambda i, elo, ehi, *_: (layer, ehi[i], 0, 0)
    return pl.pallas_call(
        functools.partial(_experts_body, token_words=tokens * OUT_PLANES),
        grid_spec=pltpu.PrefetchScalarGridSpec(
            num_scalar_prefetch=6,
            grid=(n_tiles,),
            in_specs=[pl.BlockSpec(memory_space=pl.ANY),
                      pl.BlockSpec((None, None, d, f), lo), pl.BlockSpec((None, None, d, f), hi),
                      pl.BlockSpec((None, None, d, f), lo), pl.BlockSpec((None, None, d, f), hi),
                      pl.BlockSpec((None, None, f, d), lo), pl.BlockSpec((None, None, f, d), hi)],
            out_specs=pl.BlockSpec(memory_space=pl.ANY),
            scratch_shapes=[pltpu.VMEM((d, 2 * f), BF16), pltpu.VMEM((d, 2 * f), BF16),
                            pltpu.VMEM((2 * f, d), BF16),
                            pltpu.VMEM((2, tile * ROW_PLANES, LANE), F32),
                            pltpu.VMEM((2, tile * OUT_PLANES, LANE), F32),
                            pltpu.SemaphoreType.DMA((2,)), pltpu.SemaphoreType.DMA((2,))]),
        out_shape=jax.ShapeDtypeStruct(((tokens + tile) * OUT_PLANES, LANE), F32),
        compiler_params=_cparams(("arbitrary",)),
        name=name,
    )(e_lo, e_hi, fresh, valid, gather_idx, scatter_idx, rows, w_gate, w_gate, w_up, w_up, w_down, w_down)


def _moe_res_body(y_ref, w_ref, x_ref, gate_ref, fg_ref, o_ref, *rest, final):
    tm = x_ref.shape[0]
    y_lo = jnp.concatenate([y_ref[pl.ds(j, tm, stride=OUT_PLANES), :] for j in range(ROW_PLANES)], axis=-1)
    y_hi = jnp.concatenate([y_ref[pl.ds(ROW_PLANES + j, tm, stride=OUT_PLANES), :] for j in range(ROW_PLANES)],
                           axis=-1)
    xn = x_ref[...] + gate_ref[...] * (w_ref[:, 0:1] * y_lo + w_ref[:, 1:2] * y_hi)
    o_ref[...] = xn
    if final:
        ms = jnp.mean(xn * xn, axis=-1, keepdims=True)
        rest[0][...] = xn * lax.rsqrt(ms + EPS) * fg_ref[...]


def _moe_res_call(y_tok, weights, x, mod, final_g, tm, row_offset, final, name):
    t, d = x.shape
    blk0 = row_offset // tm
    n_out = 2 if final else 1
    return pl.pallas_call(
        functools.partial(_moe_res_body, final=final),
        grid=(t // tm,),
        in_specs=[pl.BlockSpec((tm * OUT_PLANES, LANE), lambda i: (blk0 + i, 0)),
                  pl.BlockSpec((tm, LANE), lambda i: (blk0 + i, 0)),
                  pl.BlockSpec((tm, d), lambda i: (i, 0)),
                  mod.spec(5, tm, 1),
                  pl.BlockSpec((1, d), lambda i: (0, 0))],
        out_specs=[pl.BlockSpec((tm, d), lambda i: (i, 0))] * n_out,
        out_shape=[jax.ShapeDtypeStruct((t, d), F32)] * n_out,
        compiler_params=_cparams(("arbitrary",)),
        name=name,
    )(y_tok, weights, x, mod.arr, final_g.reshape(1, d))


def _tile_tables(counts, n_tiles):
    tile = EXPERT_TILE
    tiles_per_bucket = (counts + tile - 1) // tile
    ends = jnp.cumsum(tiles_per_bucket)
    starts = ends - tiles_per_bucket
    used = ends[-1]
    ti = jnp.arange(n_tiles, dtype=I32)
    valid = (ti < used).astype(I32)
    tile_bucket = jnp.sum((jnp.minimum(ti, used - 1)[:, None] >= ends[None, :]).astype(I32), axis=1)
    pair_lo = jnp.asarray([p[0] for p in PAIRS], I32)
    pair_hi = jnp.asarray([p[1] for p in PAIRS], I32)
    grp = tile_bucket // len(PAIRS)
    e_lo = grp * EXPERTS_PER_GROUP + pair_lo[tile_bucket % len(PAIRS)]
    e_hi = grp * EXPERTS_PER_GROUP + pair_hi[tile_bucket % len(PAIRS)]
    fresh = jnp.concatenate([jnp.ones((1,), I32), (tile_bucket[1:] != tile_bucket[:-1]).astype(I32)])
    return e_lo, e_hi, fresh, valid, (starts * tile).astype(I32)


def _moe_layer(xp, xs, norm_g2, mod_p, mod_s, router_w, router_b, w_gate, w_up, w_down, layer, final_g, final):
    tp, ts = xp.shape[0], xs.shape[0]
    total = tp + ts
    n_tiles = -(-total // EXPERT_TILE) + N_BUCKETS
    zero_counts = jnp.zeros((BUCKET_ROWS, LANE), F32)
    rows, weights, bucket, rank, counts = _route_call(xp, norm_g2, mod_p, router_w, router_b, zero_counts, 512, total,
                                                      0, None, "route_prompt")
    rows, weights, bucket, rank, counts = _route_call(xs, norm_g2, mod_s, router_w, router_b, counts, ts, total, tp,
                                                      (rows, weights, bucket, rank), "route_sample")
    e_lo, e_hi, fresh, valid, bucket_start = _tile_tables(counts[:N_BUCKETS, 0].astype(I32), n_tiles)
    dest = bucket_start[bucket[0]] + rank[0]
    gather_idx, scatter_idx = _invert_call(dest, n_tiles * EXPERT_TILE, "invert")
    y_tok = _experts_call(e_lo, e_hi, fresh, valid, gather_idx, scatter_idx, rows, w_gate, w_up, w_down, layer,
                          "experts")
    outp = _moe_res_call(y_tok, weights, xp, mod_p, final_g, 512, 0, final, "moe_res_prompt")
    outs = _moe_res_call(y_tok, weights, xs, mod_s, final_g, ts, tp, final, "moe_res_sample")
    return outp, outs


def _mamba_layer(x, g, mod, conv_prev, ssm, w_zx, w_dt, conv_w, conv_b, dt_bias, a_log, d_skip, norm_g, w_out,
                 tm, tag):
    pad_h = lambda v: jnp.pad(v.reshape(1, -1), ((0, 0), (0, LANE - A_N_HEADS)))
    tm_in = min(2 * tm, x.shape[0])
    proj = _norm_mm_call(x, g, mod, 1, 0, w_zx, tm_in, 512, BF16, "a_in_" + tag)
    dt_raw = _norm_mm3_call(x, g, mod, 1, 0, w_dt, tm, "a_dt_" + tag)
    d_x = jnp.repeat(d_skip, A_HEAD_DIM).reshape(1, A_D_INNER)
    weights = (conv_w, conv_b.reshape(1, -1), pad_h(dt_bias), pad_h(a_log), d_x, norm_g.reshape(1, -1))
    if ssm[0] == "step":
        yn, conv_new, ssm_new = _ssd_step_call(proj, dt_raw, conv_prev, ssm[1], ssm[2], ssm[3], *weights,
                                               "ssd_step_" + tag)
    else:
        yn, conv_new, ssm_new = _ssd_call(proj, dt_raw, conv_prev, ssm[2], *weights, ssm[1], "ssd_" + tag)
    x = _out_res_call(yn, w_out, x, mod, 2, tm, "a_out_" + tag)
    return x, conv_new, ssm_new


def kernel(x_prompt, x_sample, c_prompt, c_sample, state_a_conv, state_a_ssm, state_c_pool, w_mod, b_mod, norm_g, final_g, a_w_in, a_conv_w, a_conv_b, a_dt_bias, a_log, a_d, a_norm_g, a_w_out, b_w_in, b_b_in, b_ln_g, b_ln_b, b_w_s, b_b_s, b_w_out, c_w_g, c_scale, router_w, router_b, e_w_gate, e_w_up, e_w_down):
    bp, seq, d = x_prompt.shape
    bs = x_sample.shape[0]
    n_a, n_c = state_a_conv.shape[0], state_c_pool.shape[0]
    mod_all = _mod_call(jnp.concatenate([c_prompt, c_sample], axis=0), w_mod, b_mod)
    mod_p_arr = mod_all[:, :bp].reshape(DEPTH, bp, 6, 1, d)
    mod_s_arr = mod_all[:, bp:]
    xp = x_prompt.reshape(bp * seq, d)
    xs = x_sample.reshape(bs, d)
    conv_p, ssm_p, pool_p, conv_s, pool_s, v_s = [], [], [], [], [], []
    hp = A_N_HEADS * A_HEAD_DIM
    ssm_s_in = state_a_ssm.reshape(n_a, bs, hp, A_D_STATE)
    ssm_s_out = None
    yp = ys = None
    for i in range(DEPTH):
        kind, s = LAYER_KIND[i], LAYER_SLOT[i]
        mod_p = Mod(mod_p_arr, i, False, seq)
        mod_s = Mod(mod_s_arr, i, True)
        g1 = norm_g[i, 0]
        if kind == 0:
            w_zx = a_w_in[s, :, :A_ZX].astype(BF16)
            w_dt = jnp.pad(a_w_in[s, :, A_ZX:], ((0, 0), (0, LANE - A_N_HEADS)))
            weights = (w_zx, w_dt, a_conv_w[s], a_conv_b[s], a_dt_bias[s], a_log[s], a_d[s], a_norm_g[s], a_w_out[s])
            conv0 = jnp.zeros((bp, A_CONV - 1, A_CONV_CH), F32)
            ssm0 = jnp.zeros((bp, hp, A_D_STATE), F32)
            xp, cv, ss = _mamba_layer(xp, g1, mod_p, conv0, ("prompt", bp, ssm0), *weights, 1024, "p%d" % i)
            conv_p.append(cv)
            ssm_p.append(ss.reshape(bp, A_N_HEADS, A_HEAD_DIM, A_D_STATE))
            xs, cv, ssm_s_out = _mamba_layer(xs, g1, mod_s, state_a_conv[s], ("step", ssm_s_in, s, ssm_s_out),
                                             *weights, bs, "s%d" % i)
            conv_s.append(cv)
        elif kind == 1:
            w_uv = b_w_in[s].astype(BF16)
            uv = _gmlp_in_call(xp, g1, mod_p, w_uv, b_b_in[s], b_ln_g[s], b_ln_b[s], 512, BF16, "b_in_p%d" % i)
            xp = _gmlp_out_call(uv, b_w_s[s], b_b_s[s], b_w_out[s], xp, mod_p, 512, "b_out_p%d" % i)
            uv = _gmlp_in_call(xs, g1, mod_s, w_uv, b_b_in[s], b_ln_g[s], b_ln_b[s], bs, F32, "b_in_s%d" % i)
            xs = _gmlp_out_step_call(uv, b_w_s[s], b_b_s[s], b_w_out[s], xs, mod_s, "b_out_s%d" % i)
            v_s.append(uv[:, B_D:].reshape(bs, 1, B_D))
        else:
            pool0 = jnp.zeros((bp, C_STATE, d), F32)
            xp, pr = _pool_call(xp, g1, mod_p, pool0, c_w_g[s], c_scale[s], bp, 512, 0, "pool_p%d" % i)
            pool_p.append(pr)
            xs, pr = _pool_step_call(xs, g1, mod_s, state_c_pool[s], c_w_g[s], c_scale[s], PAST_LEN, "pool_s%d" % i)
            pool_s.append(pr)
        final = i == DEPTH - 1
        outp, outs = _moe_layer(xp, xs, norm_g[i, 1], mod_p, mod_s, router_w, router_b, e_w_gate, e_w_up, e_w_down, i,
                                final_g, final)
        xp, xs = outp[0], outs[0]
        if final:
            yp, ys = outp[1], outs[1]
    return (yp.reshape(bp, seq, d), ys.reshape(bs, 1, d), jnp.stack(conv_p), jnp.stack(ssm_p), jnp.stack(pool_p),
            jnp.stack(conv_s), ssm_s_out.reshape(state_a_ssm.shape), jnp.stack(pool_s), jnp.stack(v_s))
```

```python
import functools
import math

import numpy as np
import jax
import jax.numpy as jnp
from jax import lax
from jax.experimental import pallas as pl
from jax.experimental.pallas import tpu as pltpu

F32 = jnp.float32
BF16 = jnp.bfloat16
I32 = jnp.int32
EPS = 1e-6

LANE = 128
D_MODEL = 1024
DEPTH = 4
PAST_LEN = 16384
LAYER_KIND = (0, 1, 2, 0)
LAYER_SLOT = (0, 0, 0, 1)
A_D_INNER = 2 * D_MODEL
A_HEAD_DIM = 64
A_N_HEADS = A_D_INNER // A_HEAD_DIM
A_N_GROUPS = 8
A_HPG = A_N_HEADS // A_N_GROUPS
A_D_STATE = 128
A_GN = A_N_GROUPS * A_D_STATE
A_CONV = 4
A_CONV_CH = A_D_INNER + 2 * A_GN
A_ZX = A_D_INNER + A_CONV_CH
A_CHUNK = 128
A_GW = A_HPG * A_HEAD_DIM
B_D = 2 * D_MODEL
B_N_GROUPS = 8
B_GROUP_DIM = B_D // B_N_GROUPS
B_CHUNK = 128
C_WINDOWS = (2, 4, 8, 16)
C_GROUP_DIM = D_MODEL // len(C_WINDOWS)
C_STATE = max(C_WINDOWS) - 1
N_EXPERTS = 16
N_EXPERT_GROUPS = 4
EXPERTS_PER_GROUP = 4
D_EXPERT = D_MODEL // 2
PAIRS = ((0, 1), (0, 2), (0, 3), (1, 2), (1, 3), (2, 3))
N_BUCKETS = N_EXPERT_GROUPS * len(PAIRS)
BUCKET_ROWS = 32
ROW_PLANES = D_MODEL // LANE
OUT_PLANES = 2 * ROW_PLANES
EXPERT_TILE = 256
VMEM_LIMIT = 56 * 1024 * 1024


def _cparams(sem, vmem=VMEM_LIMIT):
    return pltpu.CompilerParams(dimension_semantics=sem, vmem_limit_bytes=vmem)


def _sigmoid(x):
    return 1.0 / (1.0 + jnp.exp(-x))


def _silu(x):
    return x * _sigmoid(x)


def _softplus(x):
    return jnp.maximum(x, 0.0) + jnp.log1p(jnp.exp(-jnp.abs(x)))


def _split2(a):
    hi = a.astype(BF16)
    lo = (a - hi.astype(F32)).astype(BF16)
    return hi, lo


def _split3(a):
    hi = a.astype(BF16)
    r = a - hi.astype(F32)
    mid = r.astype(BF16)
    lo = (r - mid.astype(F32)).astype(BF16)
    return hi, mid, lo


def _dot(a, b):
    return jnp.dot(a, b, preferred_element_type=F32)


def _dot_nt(a, b):
    return lax.dot_general(a, b, (((1,), (1,)), ((), ())), preferred_element_type=F32)


def _dot_tn(a, b):
    return lax.dot_general(a, b, (((0,), (0,)), ((), ())), preferred_element_type=F32)


def _dot3(a, b):
    a_hi, a_lo = _split2(a)
    b_hi, b_lo = _split2(b)
    return _dot(a_hi, b_hi) + (_dot(a_lo, b_hi) + _dot(a_hi, b_lo))


def _prenorm(x, g, sc, sh):
    ms = jnp.mean(x * x, axis=-1, keepdims=True)
    return (x * lax.rsqrt(ms + EPS) * g) * (1.0 + sc) + sh


def _mod_body(c_ref, w_ref, b_ref, o_ref):
    o_ref[...] = _dot3(_silu(c_ref[...]), w_ref[...]) + b_ref[...]


def _mod_call(c_all, w_mod, b_mod):
    nb, d = c_all.shape
    depth, _, n = w_mod.shape
    tn = 1536
    return pl.pallas_call(
        _mod_body,
        grid=(depth, n // tn),
        in_specs=[pl.BlockSpec((nb, d), lambda i, j: (0, 0)),
                  pl.BlockSpec((None, d, tn), lambda i, j: (i, 0, j)),
                  pl.BlockSpec((None, 1, tn), lambda i, j: (i, 0, j))],
        out_specs=pl.BlockSpec((None, nb, tn), lambda i, j: (i, 0, j)),
        out_shape=jax.ShapeDtypeStruct((depth, nb, n), F32),
        compiler_params=_cparams(("arbitrary", "arbitrary")),
        name="mod",
    )(c_all, w_mod, b_mod.reshape(depth, 1, n))


class Mod:
    def __init__(self, arr, layer, per_row, rows_per_seq=None):
        self.arr, self.layer, self.per_row, self.rows_per_seq = arr, layer, per_row, rows_per_seq

    def spec(self, which, tm, ngrid):
        layer = self.layer
        if self.per_row:
            if ngrid == 1:
                return pl.BlockSpec((None, tm, D_MODEL), lambda i: (layer, i, which))
            return pl.BlockSpec((None, tm, D_MODEL), lambda i, j: (layer, i, which))
        tiles = self.rows_per_seq // tm
        if ngrid == 1:
            return pl.BlockSpec((None, None, None, 1, D_MODEL), lambda i: (layer, i // tiles, which, 0, 0))
        return pl.BlockSpec((None, None, None, 1, D_MODEL), lambda i, j: (layer, i // tiles, which, 0, 0))


def _norm_mm_body(x_ref, g_ref, sc_ref, sh_ref, w_ref, o_ref, hn_ref):
    @pl.when(pl.program_id(1) == 0)
    def _():
        hn_ref[...] = _prenorm(x_ref[...], g_ref[...], sc_ref[...], sh_ref[...]).astype(BF16)

    o_ref[...] = _dot(hn_ref[...], w_ref[...]).astype(o_ref.dtype)


def _norm_mm_call(x, g, mod, which_sc, which_sh, w, tm, tn, out_dtype, name):
    t, d = x.shape
    n_cols = w.shape[1]
    return pl.pallas_call(
        _norm_mm_body,
        grid=(t // tm, n_cols // tn),
        in_specs=[pl.BlockSpec((tm, d), lambda i, j: (i, 0)),
                  pl.BlockSpec((1, d), lambda i, j: (0, 0)),
                  mod.spec(which_sc, tm, 2), mod.spec(which_sh, tm, 2),
                  pl.BlockSpec((d, tn), lambda i, j: (0, j))],
        out_specs=pl.BlockSpec((tm, tn), lambda i, j: (i, j)),
        out_shape=jax.ShapeDtypeStruct((t, n_cols), out_dtype),
        scratch_shapes=[pltpu.VMEM((tm, d), BF16)],
        compiler_params=_cparams(("arbitrary", "arbitrary")),
        name=name,
    )(x, g.reshape(1, d), mod.arr, mod.arr, w)


def _norm_mm3_body(x_ref, g_ref, sc_ref, sh_ref, w_ref, o_ref):
    hn = _prenorm(x_ref[...], g_ref[...], sc_ref[...], sh_ref[...])
    o_ref[...] = _dot3(hn, w_ref[...])


def _norm_mm3_call(x, g, mod, which_sc, which_sh, w, tm, name):
    t, d = x.shape
    n = w.shape[1]
    return pl.pallas_call(
        _norm_mm3_body,
        grid=(t // tm,),
        in_specs=[pl.BlockSpec((tm, d), lambda i: (i, 0)),
                  pl.BlockSpec((1, d), lambda i: (0, 0)),
                  mod.spec(which_sc, tm, 1), mod.spec(which_sh, tm, 1),
                  pl.BlockSpec((d, n), lambda i: (0, 0))],
        out_specs=pl.BlockSpec((tm, n), lambda i: (i, 0)),
        out_shape=jax.ShapeDtypeStruct((t, n), F32),
        compiler_params=_cparams(("arbitrary",)),
        name=name,
    )(x, g.reshape(1, d), mod.arr, mod.arr, w)


def _out_res_body(y_ref, w_ref, x_ref, gate_ref, o_ref, wbf_ref):
    @pl.when(pl.program_id(0) == 0)
    def _():
        wbf_ref[...] = w_ref[...].astype(BF16)

    o_ref[...] = x_ref[...] + gate_ref[...] * _dot(y_ref[...], wbf_ref[...])


def _out_res_call(y, w, x, mod, which_gate, tm, name):
    t, k = y.shape
    d = x.shape[1]
    return pl.pallas_call(
        _out_res_body,
        grid=(t // tm,),
        in_specs=[pl.BlockSpec((tm, k), lambda i: (i, 0)),
                  pl.BlockSpec((k, d), lambda i: (0, 0)),
                  pl.BlockSpec((tm, d), lambda i: (i, 0)),
                  mod.spec(which_gate, tm, 1)],
        out_specs=pl.BlockSpec((tm, d), lambda i: (i, 0)),
        out_shape=jax.ShapeDtypeStruct((t, d), F32),
        scratch_shapes=[pltpu.VMEM((k, d), BF16)],
        compiler_params=_cparams(("arbitrary",)),
        name=name,
    )(y, w, x, mod.arr)


def _head_expand():
    h = np.arange(LANE)[:, None]
    c = np.arange(A_D_INNER)[None, :]
    return jnp.asarray((c // A_HEAD_DIM == h).astype(np.float32), dtype=BF16)


def _gate_norm(y, z, ng):
    gated = y * _silu(z)
    ms = jnp.mean(gated * gated, axis=-1, keepdims=True)
    return gated * lax.rsqrt(ms + EPS) * ng


def _ssd_body(z_ref, xs_ref, bc_ref, dt_ref, cprev_ref, sprev_ref, cw_ref, cb_ref, dtb_ref, alog_ref,
              dx_ref, ng_ref, exp_ref, yn_ref, cnew_ref, snew_ref, tail_ref, act_ref):
    c = pl.program_id(1)
    q = A_CHUNK
    nt = A_CONV - 1

    @pl.when(c == 0)
    def _():
        tail_ref[0:nt, :] = cprev_ref[...]
        snew_ref[...] = sprev_ref[...]

    row = lax.broadcasted_iota(I32, (q, q), 0)
    col = lax.broadcasted_iota(I32, (q, q), 1)
    shifts = [jnp.where(row - col == nt - k, 1.0, 0.0).astype(BF16) for k in range(nt)]
    sub = lax.broadcasted_iota(I32, (8, 1), 0)
    cw = 512
    for j in range(A_CONV_CH // cw):
        sl = slice(j * cw, (j + 1) * cw)
        src = xs_ref if (j + 1) * cw <= A_D_INNER else bc_ref
        off = j * cw if src is xs_ref else j * cw - A_D_INNER
        x_bf = src[:, off:off + cw]
        conv = cb_ref[:, sl] + x_bf.astype(F32) * cw_ref[nt:nt + 1, sl]
        for k in range(nt):
            conv = conv + _dot(shifts[k], x_bf) * cw_ref[k:k + 1, sl]
        act_ref[:, sl] = _silu(conv)
        corr = jnp.zeros((8, cw), F32)
        for l in range(nt):
            c_l = sum(tail_ref[l + k:l + k + 1, sl] * cw_ref[k:k + 1, sl] for k in range(nt - l))
            corr = jnp.where(sub == l, c_l, corr)
        act_ref[0:8, sl] = _silu(conv[0:8, :] + corr)
        tail_ref[0:nt, sl] = x_bf[q - 8:q, :].astype(F32)[8 - nt:8, :]

    @pl.when(c == pl.num_programs(1) - 1)
    def _():
        cnew_ref[...] = tail_ref[0:nt, :]

    dt = _softplus(dt_ref[...] + dtb_ref[...])
    a = dt * (-jnp.exp(alog_ref[...]))
    row = lax.broadcasted_iota(I32, (q, q), 0)
    col = lax.broadcasted_iota(I32, (q, q), 1)
    causal = row >= col
    tril = jnp.where(causal, 1.0, 0.0).astype(BF16)
    a_hi, a_mid, a_lo = _split3(a)
    cs = _dot(tril, a_hi) + (_dot(tril, a_mid) + _dot(tril, a_lo))
    cs_t = cs.T
    dt_t = dt.T
    cs_last = cs[q - 1:q, :]
    ecs = jnp.exp(cs)
    wend = jnp.exp(cs_last - cs) * dt
    st_hi, st_lo = _split2(jnp.concatenate([ecs, wend], axis=0))
    st_x = _dot(st_hi, exp_ref[...]) + _dot(st_lo, exp_ref[...])
    lane_head = lax.broadcasted_iota(I32, (q, A_GW), 1) // A_HEAD_DIM

    for g in range(A_N_GROUPS):
        gsl = slice(g * A_GW, (g + 1) * A_GW)
        b_g = act_ref[:, A_D_INNER + g * A_D_STATE:A_D_INNER + (g + 1) * A_D_STATE].astype(BF16)
        c_g = act_ref[:, A_D_INNER + A_GN + g * A_D_STATE:A_D_INNER + A_GN + (g + 1) * A_D_STATE].astype(BF16)
        x_g = act_ref[:, gsl]
        x_bf = x_g.astype(BF16)
        h_g = snew_ref[gsl, :]
        cb = _dot_nt(c_g, b_g)
        y = jnp.zeros((q, A_GW), F32)
        for r in range(A_HPG):
            h = g * A_HPG + r
            seg = cs[:, h:h + 1] - cs_t[h:h + 1, :]
            decay = jnp.exp(jnp.where(causal, seg, -1e30))
            wm = (cb * decay * dt_t[h:h + 1, :]).astype(BF16)
            y = y + _dot(wm, jnp.where(lane_head == r, x_bf, jnp.zeros_like(x_bf)))
        y = y + st_x[0:q, gsl] * _dot_nt(c_g, h_g.astype(BF16)) + dx_ref[:, gsl] * x_g
        yn_ref[:, gsl] = _gate_norm(y, z_ref[:, gsl].astype(F32), ng_ref[:, gsl]).astype(BF16)
        s_new = _dot_tn((x_g * st_x[q:2 * q, gsl]).astype(BF16), b_g)
        for r in range(A_HPG):
            h = g * A_HPG + r
            rsl = slice(g * A_GW + r * A_HEAD_DIM, g * A_GW + (r + 1) * A_HEAD_DIM)
            keep = jnp.exp(cs[q - 1:q, h:h + 1])
            snew_ref[rsl, :] = snew_ref[rsl, :] * keep + s_new[r * A_HEAD_DIM:(r + 1) * A_HEAD_DIM, :]


def _ssd_call(proj, dt_raw, conv_prev, ssm_prev, conv_w, conv_b, dt_bias, a_log, d_x, norm_g, nb, name):
    t = proj.shape[0]
    q = A_CHUNK
    nc = t // nb // q
    hp = A_N_HEADS * A_HEAD_DIM
    row = lambda b, c: (b * nc + c, 0)
    full = lambda shape: pl.BlockSpec(shape, lambda b, c: (0,) * len(shape))
    return pl.pallas_call(
        _ssd_body,
        grid=(nb, nc),
        in_specs=[pl.BlockSpec((q, A_D_INNER), lambda b, c: (b * nc + c, 0)),
                  pl.BlockSpec((q, A_D_INNER), lambda b, c: (b * nc + c, 1)),
                  pl.BlockSpec((q, 2 * A_GN), lambda b, c: (b * nc + c, 2)),
                  pl.BlockSpec((q, LANE), row),
                  pl.BlockSpec((None, A_CONV - 1, A_CONV_CH), lambda b, c: (b, 0, 0)),
                  pl.BlockSpec((None, hp, A_D_STATE), lambda b, c: (b, 0, 0)),
                  full((A_CONV, A_CONV_CH)), full((1, A_CONV_CH)), full((1, LANE)), full((1, LANE)),
                  full((1, A_D_INNER)), full((1, A_D_INNER)), full((LANE, A_D_INNER))],
        out_specs=[pl.BlockSpec((q, A_D_INNER), row),
                   pl.BlockSpec((None, A_CONV - 1, A_CONV_CH), lambda b, c: (b, 0, 0)),
                   pl.BlockSpec((None, hp, A_D_STATE), lambda b, c: (b, 0, 0))],
        out_shape=[jax.ShapeDtypeStruct((t, A_D_INNER), BF16),
                   jax.ShapeDtypeStruct((nb, A_CONV - 1, A_CONV_CH), F32),
                   jax.ShapeDtypeStruct((nb, hp, A_D_STATE), F32)],
        scratch_shapes=[pltpu.VMEM((8, A_CONV_CH), F32), pltpu.VMEM((q, A_CONV_CH), F32)],
        compiler_params=_cparams(("arbitrary", "arbitrary")),
        name=name,
    )(proj, proj, proj, dt_raw, conv_prev, ssm_prev, conv_w, conv_b, dt_bias, a_log, d_x, norm_g, _head_expand())


def _ssd_step_body(z_ref, xs_ref, bc_ref, dt_ref, cprev_ref, sprev_ref, cw_ref, cb_ref, dtb_ref, alog_ref,
                   dx_ref, ng_ref, exp_ref, yn_ref, cnew_ref, snew_ref, da_ref, y_ref):
    bt = z_ref.shape[0]
    cur = jnp.concatenate([xs_ref[...], bc_ref[...]], axis=1).astype(F32)
    conv = cb_ref[...] + cur * cw_ref[A_CONV - 1:A_CONV, :]
    for k in range(A_CONV - 1):
        conv = conv + cprev_ref[:, k, :] * cw_ref[k:k + 1, :]
    for k in range(A_CONV - 2):
        cnew_ref[:, k, :] = cprev_ref[:, k + 1, :]
    cnew_ref[:, A_CONV - 2, :] = cur
    act = _silu(conv)
    xs = act[:, 0:A_D_INNER]
    bm_bf = act[:, A_D_INNER:A_D_INNER + A_GN].astype(BF16)
    cm_bf = act[:, A_D_INNER + A_GN:A_CONV_CH].astype(BF16)
    dt = _softplus(dt_ref[...] + dtb_ref[...])
    da_ref[...] = jnp.exp(dt * (-jnp.exp(alog_ref[...])))
    dt_hi, dt_lo = _split2(dt)
    dt_x = _dot(dt_hi, exp_ref[...]) + _dot(dt_lo, exp_ref[...])
    xdt = xs * dt_x
    rows = lax.broadcasted_iota(I32, (bt, 1), 0)
    y_ref[...] = jnp.zeros_like(y_ref)

    def per_seq(j, carry):
        mine = rows == j
        xdt_j = jnp.where(mine, xdt, 0.0)
        da_j = da_ref[pl.ds(j, 1), :]
        for g in range(A_N_GROUPS):
            gsl = slice(g * A_GW, (g + 1) * A_GW)
            nsl = slice(g * A_D_STATE, (g + 1) * A_D_STATE)
            outer = _dot_tn(xdt_j[:, gsl].astype(BF16), bm_bf[:, nsl])
            for r in range(A_HPG):
                h = g * A_HPG + r
                rsl = slice(r * A_HEAD_DIM, (r + 1) * A_HEAD_DIM)
                hsl = slice(g * A_GW + r * A_HEAD_DIM, g * A_GW + (r + 1) * A_HEAD_DIM)
                snew_ref[j, hsl, :] = sprev_ref[j, hsl, :] * da_j[:, h:h + 1] + outer[rsl, :]
            yg = _dot_nt(cm_bf[:, nsl], snew_ref[j, gsl, :].astype(BF16))
            y_ref[:, gsl] = y_ref[:, gsl] + jnp.where(mine, yg, 0.0)
        return carry

    lax.fori_loop(0, bt, per_seq, 0)
    y = y_ref[...] + dx_ref[...] * xs
    z = z_ref[...].astype(F32)
    for g in range(A_N_GROUPS):
        gsl = slice(g * A_GW, (g + 1) * A_GW)
        yn_ref[:, gsl] = _gate_norm(y[:, gsl], z[:, gsl], ng_ref[:, gsl]).astype(BF16)


_SSD_STEP_INPUTS = 13


def _ssd_step_aliased_body(*refs):
    _ssd_step_body(*refs[:_SSD_STEP_INPUTS], *refs[_SSD_STEP_INPUTS + 1:])


def _ssd_step_call(proj, dt_raw, conv_prev, ssm_all, slot, ssm_out, conv_w, conv_b, dt_bias, a_log, d_x, norm_g,
                   name):
    nb = proj.shape[0]
    bt = 8
    hp = A_N_HEADS * A_HEAD_DIM
    full = lambda shape: pl.BlockSpec(shape, lambda i: (0,) * len(shape))
    state_spec = pl.BlockSpec((None, bt, hp, A_D_STATE), lambda i: (slot, i, 0, 0))
    in_specs = [pl.BlockSpec((bt, A_D_INNER), lambda i: (i, 0)),
                pl.BlockSpec((bt, A_D_INNER), lambda i: (i, 1)),
                pl.BlockSpec((bt, 2 * A_GN), lambda i: (i, 2)),
                pl.BlockSpec((bt, LANE), lambda i: (i, 0)),
                pl.BlockSpec((bt, A_CONV - 1, A_CONV_CH), lambda i: (i, 0, 0)),
                state_spec,
                full((A_CONV, A_CONV_CH)), full((1, A_CONV_CH)), full((1, LANE)), full((1, LANE)),
                full((1, A_D_INNER)), full((1, A_D_INNER)), full((LANE, A_D_INNER))]
    args = [proj, proj, proj, dt_raw, conv_prev, ssm_all, conv_w, conv_b, dt_bias, a_log, d_x, norm_g, _head_expand()]
    assert len(args) == _SSD_STEP_INPUTS
    aliases = {}
    if ssm_out is not None:
        in_specs.append(pl.BlockSpec(memory_space=pl.ANY))
        args.append(ssm_out)
        aliases = {_SSD_STEP_INPUTS: 2}
    return pl.pallas_call(
        _ssd_step_body if ssm_out is None else _ssd_step_aliased_body,
        grid=(nb // bt,),
        in_specs=in_specs,
        out_specs=[pl.BlockSpec((bt, A_D_INNER), lambda i: (i, 0)),
                   pl.BlockSpec((bt, A_CONV - 1, A_CONV_CH), lambda i: (i, 0, 0)),
                   state_spec],
        out_shape=[jax.ShapeDtypeStruct((nb, A_D_INNER), BF16),
                   jax.ShapeDtypeStruct((nb, A_CONV - 1, A_CONV_CH), F32),
                   jax.ShapeDtypeStruct(ssm_all.shape, F32)],
        scratch_shapes=[pltpu.VMEM((bt, LANE), F32), pltpu.VMEM((bt, A_D_INNER), F32)],
        input_output_aliases=aliases,
        compiler_params=_cparams(("arbitrary",)),
        name=name,
    )(*args)


def _gmlp_in_body(x_ref, g_ref, sc_ref, sh_ref, w_ref, b_ref, lg_ref, lb_ref, o_ref, hn_ref):
    j = pl.program_id(1)

    @pl.when(j == 0)
    def _():
        hn_ref[...] = _prenorm(x_ref[...], g_ref[...], sc_ref[...], sh_ref[...]).astype(BF16)

    uv = jax.nn.gelu(_dot(hn_ref[...], w_ref[...]) + b_ref[...], approximate=True)

    @pl.when(j == 0)
    def _():
        o_ref[...] = uv.astype(o_ref.dtype)

    @pl.when(j == 1)
    def _():
        vc = uv - jnp.mean(uv, axis=-1, keepdims=True)
        var = jnp.mean(vc * vc, axis=-1, keepdims=True)
        o_ref[...] = (vc * lax.rsqrt(var + EPS) * lg_ref[...] + lb_ref[...]).astype(o_ref.dtype)


def _gmlp_in_call(x, g, mod, w, b, ln_g, ln_b, tm, out_dtype, name):
    t, d = x.shape
    return pl.pallas_call(
        _gmlp_in_body,
        grid=(t // tm, 2),
        in_specs=[pl.BlockSpec((tm, d), lambda i, j: (i, 0)),
                  pl.BlockSpec((1, d), lambda i, j: (0, 0)),
                  mod.spec(1, tm, 2), mod.spec(0, tm, 2),
                  pl.BlockSpec((d, B_D), lambda i, j: (0, j)),
                  pl.BlockSpec((1, B_D), lambda i, j: (0, j)),
                  pl.BlockSpec((1, B_D), lambda i, j: (0, 0)),
                  pl.BlockSpec((1, B_D), lambda i, j: (0, 0))],
        out_specs=pl.BlockSpec((tm, B_D), lambda i, j: (i, j)),
        out_shape=jax.ShapeDtypeStruct((t, 2 * B_D), out_dtype),
        scratch_shapes=[pltpu.VMEM((tm, d), BF16)],
        compiler_params=_cparams(("arbitrary", "arbitrary")),
        name=name,
    )(x, g.reshape(1, d), mod.arr, mod.arr, w, b.reshape(1, -1), ln_g.reshape(1, -1), ln_b.reshape(1, -1))


def _gmlp_out_body(u_ref, v_ref, ws_ref, bs_ref, w_ref, x_ref, gate_ref, o_ref, wbf_ref, wsbf_ref, m_ref):
    q = B_CHUNK

    @pl.when(pl.program_id(0) == 0)
    def _():
        wbf_ref[...] = w_ref[...].astype(BF16)
        causal = lax.broadcasted_iota(I32, (q, q), 0) >= lax.broadcasted_iota(I32, (q, q), 1)
        for g in range(B_N_GROUPS):
            wsbf_ref[g] = jnp.where(causal, ws_ref[g], 0.0).astype(BF16)

    for ci in range(u_ref.shape[0] // q):
        rsl = slice(ci * q, (ci + 1) * q)
        for g in range(B_N_GROUPS):
            gsl = slice(g * B_GROUP_DIM, (g + 1) * B_GROUP_DIM)
            mixed = _dot(wsbf_ref[g], v_ref[rsl, gsl].astype(BF16)) + bs_ref[:, g:g + 1]
            m_ref[rsl, gsl] = (u_ref[rsl, gsl].astype(F32) * mixed).astype(BF16)
    o_ref[...] = x_ref[...] + gate_ref[...] * _dot(m_ref[...], wbf_ref[...])


def _gmlp_out_call(uv, w_s, b_s, w_out, x, mod, tm, name):
    t, d = x.shape
    q = B_CHUNK
    return pl.pallas_call(
        _gmlp_out_body,
        grid=(t // tm,),
        in_specs=[pl.BlockSpec((tm, B_D), lambda i: (i, 0)),
                  pl.BlockSpec((tm, B_D), lambda i: (i, 1)),
                  pl.BlockSpec((B_N_GROUPS, q, q), lambda i: (0, 0, 0)),
                  pl.BlockSpec((q, B_N_GROUPS), lambda i: (0, 0)),
                  pl.BlockSpec((B_D, d), lambda i: (0, 0)),
                  pl.BlockSpec((tm, d), lambda i: (i, 0)),
                  mod.spec(2, tm, 1)],
        out_specs=pl.BlockSpec((tm, d), lambda i: (i, 0)),
        out_shape=jax.ShapeDtypeStruct((t, d), F32),
        scratch_shapes=[pltpu.VMEM((B_D, d), BF16), pltpu.VMEM((B_N_GROUPS, q, q), BF16),
                        pltpu.VMEM((tm, B_D), BF16)],
        compiler_params=_cparams(("arbitrary",)),
        name=name,
    )(uv, uv, w_s, b_s.T, w_out, x, mod.arr)


def _gmlp_out_step_body(u_ref, v_ref, wd_ref, bd_ref, w_ref, x_ref, gate_ref, o_ref):
    mixed = v_ref[...] * wd_ref[...] + bd_ref[...]
    m = (u_ref[...] * mixed).astype(BF16)
    o_ref[...] = x_ref[...] + gate_ref[...] * _dot(m, w_ref[...].astype(BF16))


def _gmlp_out_step_call(uv, w_s, b_s, w_out, x, mod, name):
    t, d = x.shape
    wd = jnp.repeat(w_s[:, 0, 0], B_GROUP_DIM).reshape(1, B_D)
    bd = jnp.repeat(b_s[:, 0], B_GROUP_DIM).reshape(1, B_D)
    return pl.pallas_call(
        _gmlp_out_step_body,
        grid=(1,),
        in_specs=[pl.BlockSpec((t, B_D), lambda i: (0, 0)),
                  pl.BlockSpec((t, B_D), lambda i: (0, 1)),
                  pl.BlockSpec((1, B_D), lambda i: (0, 0)),
                  pl.BlockSpec((1, B_D), lambda i: (0, 0)),
                  pl.BlockSpec((B_D, d), lambda i: (0, 0)),
                  pl.BlockSpec((t, d), lambda i: (0, 0)),
                  mod.spec(2, t, 1)],
        out_specs=pl.BlockSpec((t, d), lambda i: (0, 0)),
        out_shape=jax.ShapeDtypeStruct((t, d), F32),
        compiler_params=_cparams(("arbitrary",)),
        name=name,
    )(uv, uv, wd, bd, w_out, x, mod.arr)


def _pool_matmul(pooled, wg_ref):
    outs = []
    for gi in range(len(C_WINDOWS)):
        gsl = slice(gi * C_GROUP_DIM, (gi + 1) * C_GROUP_DIM)
        outs.append(_dot(pooled[:, gsl].astype(BF16), wg_ref[gi].astype(BF16)))
    return jnp.concatenate(outs, axis=-1)


def _pool_body(x_ref, g_ref, sc_ref, sh_ref, gate_ref, prev_ref, wg_ref, scale_ref, o_ref, pnew_ref, hp_ref,
               *, tiles_per_seq, start):
    i = pl.program_id(0)
    tm = x_ref.shape[0]
    top = 16
    ti = i % tiles_per_seq

    @pl.when(ti == 0)
    def _():
        hp_ref[top - C_STATE:top, :] = prev_ref[...]

    hn = _prenorm(x_ref[...], g_ref[...], sc_ref[...], sh_ref[...])
    hp_ref[top:top + tm, :] = hn
    pos = start + ti * tm + lax.broadcasted_iota(I32, (tm, 1), 0)
    outs = []
    for gi, w in enumerate(C_WINDOWS):
        gsl = slice(gi * C_GROUP_DIM, (gi + 1) * C_GROUP_DIM)
        acc = hn[:, gsl]
        for k in range(1, w):
            acc = acc + hp_ref[top - k:top - k + tm, gsl]
        cnt = jnp.minimum(pos + 1, w).astype(F32)
        outs.append(acc / cnt - hn[:, gsl])
    y = _pool_matmul(jnp.concatenate(outs, axis=-1), wg_ref) * scale_ref[...]
    o_ref[...] = x_ref[...] + gate_ref[...] * y
    hist = hp_ref[top + tm - C_STATE:top + tm, :]
    hp_ref[top - C_STATE:top, :] = hist

    @pl.when(ti == tiles_per_seq - 1)
    def _():
        pnew_ref[...] = hist


def _pool_call(x, g, mod, prev, w_g, scale, nb, tm, start, name):
    t, d = x.shape
    tiles = t // nb // tm
    ng = len(C_WINDOWS)
    return pl.pallas_call(
        functools.partial(_pool_body, tiles_per_seq=tiles, start=start),
        grid=(t // tm,),
        in_specs=[pl.BlockSpec((tm, d), lambda i: (i, 0)),
                  pl.BlockSpec((1, d), lambda i: (0, 0)),
                  mod.spec(1, tm, 1), mod.spec(0, tm, 1), mod.spec(2, tm, 1),
                  pl.BlockSpec((None, C_STATE, d), lambda i: (i // tiles, 0, 0)),
                  pl.BlockSpec((ng, C_GROUP_DIM, C_GROUP_DIM), lambda i: (0, 0, 0)),
                  pl.BlockSpec((1, d), lambda i: (0, 0))],
        out_specs=[pl.BlockSpec((tm, d), lambda i: (i, 0)),
                   pl.BlockSpec((None, C_STATE, d), lambda i: (i // tiles, 0, 0))],
        out_shape=[jax.ShapeDtypeStruct((t, d), F32), jax.ShapeDtypeStruct((nb, C_STATE, d), F32)],
        scratch_shapes=[pltpu.VMEM((16 + tm, d), F32)],
        compiler_params=_cparams(("arbitrary",)),
        name=name,
    )(x, g.reshape(1, d), mod.arr, mod.arr, mod.arr, prev, w_g, scale.reshape(1, d))


def _pool_step_body(x_ref, g_ref, sc_ref, sh_ref, gate_ref, prev_ref, wg_ref, scale_ref, o_ref, pnew_ref, *, start):
    hn = _prenorm(x_ref[...], g_ref[...], sc_ref[...], sh_ref[...])
    outs = []
    for gi, w in enumerate(C_WINDOWS):
        gsl = slice(gi * C_GROUP_DIM, (gi + 1) * C_GROUP_DIM)
        acc = hn[:, gsl]
        for k in range(1, w):
            acc = acc + prev_ref[:, C_STATE - k, gsl]
        outs.append(acc / float(min(start + 1, w)) - hn[:, gsl])
    y = _pool_matmul(jnp.concatenate(outs, axis=-1), wg_ref) * scale_ref[...]
    o_ref[...] = x_ref[...] + gate_ref[...] * y
    for k in range(C_STATE - 1):
        pnew_ref[:, k, :] = prev_ref[:, k + 1, :]
    pnew_ref[:, C_STATE - 1, :] = hn


def _pool_step_call(x, g, mod, prev, w_g, scale, start, name):
    t, d = x.shape
    bt = 32
    ng = len(C_WINDOWS)
    return pl.pallas_call(
        functools.partial(_pool_step_body, start=start),
        grid=(t // bt,),
        in_specs=[pl.BlockSpec((bt, d), lambda i: (i, 0)),
                  pl.BlockSpec((1, d), lambda i: (0, 0)),
                  mod.spec(1, bt, 1), mod.spec(0, bt, 1), mod.spec(2, bt, 1),
                  pl.BlockSpec((bt, C_STATE, d), lambda i: (i, 0, 0)),
                  pl.BlockSpec((ng, C_GROUP_DIM, C_GROUP_DIM), lambda i: (0, 0, 0)),
                  pl.BlockSpec((1, d), lambda i: (0, 0))],
        out_specs=[pl.BlockSpec((bt, d), lambda i: (i, 0)),
                   pl.BlockSpec((bt, C_STATE, d), lambda i: (i, 0, 0))],
        out_shape=[jax.ShapeDtypeStruct((t, d), F32), jax.ShapeDtypeStruct((t, C_STATE, d), F32)],
        compiler_params=_cparams(("arbitrary",)),
        name=name,
    )(x, g.reshape(1, d), mod.arr, mod.arr, mod.arr, prev, w_g, scale.reshape(1, d))


def _route_rows(s, b):
    npg = EXPERTS_PER_GROUP
    gscore = []
    for q in range(N_EXPERT_GROUPS):
        v = b[q * npg:(q + 1) * npg]
        best = None
        for i in range(npg):
            for j in range(i + 1, npg):
                best = v[i] + v[j] if best is None else jnp.maximum(best, v[i] + v[j])
        gscore.append(best)
    gsel = jnp.zeros_like(gscore[0], dtype=I32)
    gbest = gscore[0]
    for q in range(1, N_EXPERT_GROUPS):
        better = gscore[q] > gbest
        gsel = jnp.where(better, q, gsel)
        gbest = jnp.where(better, gscore[q], gbest)
    vb, vs = [], []
    for k in range(npg):
        bk, sk = b[k], s[k]
        for q in range(1, N_EXPERT_GROUPS):
            bk = jnp.where(gsel == q, b[q * npg + k], bk)
            sk = jnp.where(gsel == q, s[q * npg + k], sk)
        vb.append(bk)
        vs.append(sk)
    i1 = jnp.zeros_like(gsel)
    m1 = vb[0]
    for k in range(1, npg):
        better = vb[k] > m1
        i1 = jnp.where(better, k, i1)
        m1 = jnp.where(better, vb[k], m1)
    i2 = jnp.full_like(gsel, -1)
    m2 = jnp.zeros_like(m1)
    for k in range(npg):
        better = (i1 != k) & ((i2 < 0) | (vb[k] > m2))
        i2 = jnp.where(better, k, i2)
        m2 = jnp.where(better, vb[k], m2)
    s1 = vs[0]
    s2 = vs[0]
    for k in range(1, npg):
        s1 = jnp.where(i1 == k, vs[k], s1)
        s2 = jnp.where(i2 == k, vs[k], s2)
    w1 = s1 / (s1 + s2)
    w2 = s2 / (s1 + s2)
    lo = jnp.minimum(i1, i2)
    hi = jnp.maximum(i1, i2)
    pair = jnp.zeros_like(gsel)
    for p, (a, c) in enumerate(PAIRS):
        pair = jnp.where((lo == a) & (hi == c), p, pair)
    first_is_lo = i1 < i2
    return (gsel * len(PAIRS) + pair, jnp.where(first_is_lo, w1, w2), jnp.where(first_is_lo, w2, w1))


def _route_body(x_ref, g_ref, sc_ref, sh_ref, rw_ref, rb_ref, cnt_in_ref, *rest, aliased):
    if aliased:
        rest = rest[4:]
    rows_ref, gates_ref, bucket_ref, rank_ref, cnt_ref = rest
    tm = x_ref.shape[0]

    @pl.when(pl.program_id(0) == 0)
    def _():
        cnt_ref[...] = cnt_in_ref[...]

    hn = _prenorm(x_ref[...], g_ref[...], sc_ref[...], sh_ref[...])
    logits_t = _dot3(hn, rw_ref[...]).T
    scores = _sigmoid(logits_t[0:N_EXPERTS, :])
    biased = scores + rb_ref[0:N_EXPERTS, :]
    bucket, w_lo, w_hi = _route_rows([scores[e:e + 1, :] for e in range(N_EXPERTS)],
                                     [biased[e:e + 1, :] for e in range(N_EXPERTS)])
    bucket_ref[...] = bucket
    onehot = (lax.broadcasted_iota(I32, (BUCKET_ROWS, tm), 0) == bucket).astype(F32)
    before = (lax.broadcasted_iota(I32, (tm, tm), 0) < lax.broadcasted_iota(I32, (tm, tm), 1)).astype(BF16)
    earlier = _dot(onehot.astype(BF16), before) + cnt_ref[:, 0:1]
    rank_ref[...] = jnp.sum(onehot * earlier, axis=0, keepdims=True).astype(I32)
    cnt_ref[...] = cnt_ref[...] + jnp.sum(onehot, axis=1, keepdims=True)

    sub = lax.broadcasted_iota(I32, (LANE, tm), 0)
    gate_t = jnp.where(sub == 0, w_lo, jnp.where(sub == 1, w_hi, 0.0))
    gates_ref[...] = gate_t.T
    for j in range(ROW_PLANES):
        rows_ref[pl.ds(j, tm, stride=ROW_PLANES), :] = hn[:, j * LANE:(j + 1) * LANE]


def _route_call(x, g, mod, router_w, router_b, counts, tm, total_rows, row_offset, prior, name):
    t, d = x.shape
    blk0 = row_offset // tm
    rw = jnp.pad(router_w, ((0, 0), (0, LANE - N_EXPERTS)))
    rb = jnp.pad(router_b.reshape(-1, 1), ((0, LANE - N_EXPERTS), (0, 0)))
    in_specs = [pl.BlockSpec((tm, d), lambda i: (i, 0)),
                pl.BlockSpec((1, d), lambda i: (0, 0)),
                mod.spec(4, tm, 1), mod.spec(3, tm, 1),
                pl.BlockSpec((d, LANE), lambda i: (0, 0)),
                pl.BlockSpec((LANE, 1), lambda i: (0, 0)),
                pl.BlockSpec((BUCKET_ROWS, LANE), lambda i: (0, 0))]
    args = [x, g.reshape(1, d), mod.arr, mod.arr, rw, rb, counts]
    aliases = {}
    if prior is not None:
        in_specs += [pl.BlockSpec(memory_space=pl.ANY)] * len(prior)
        aliases = {len(args) + k: k for k in range(len(prior))}
        args += list(prior)
    return pl.pallas_call(
        functools.partial(_route_body, aliased=prior is not None),
        grid=(t // tm,),
        in_specs=in_specs,
        out_specs=[pl.BlockSpec((tm * ROW_PLANES, LANE), lambda i: (blk0 + i, 0)),
                   pl.BlockSpec((tm, LANE), lambda i: (blk0 + i, 0)),
                   pl.BlockSpec((1, tm), lambda i: (0, blk0 + i)),
                   pl.BlockSpec((1, tm), lambda i: (0, blk0 + i)),
                   pl.BlockSpec((BUCKET_ROWS, LANE), lambda i: (0, 0))],
        out_shape=[jax.ShapeDtypeStruct((total_rows * ROW_PLANES, LANE), F32),
                   jax.ShapeDtypeStruct((total_rows, LANE), F32),
                   jax.ShapeDtypeStruct((1, total_rows), I32),
                   jax.ShapeDtypeStruct((1, total_rows), I32),
                   jax.ShapeDtypeStruct((BUCKET_ROWS, LANE), F32)],
        input_output_aliases=aliases,
        compiler_params=_cparams(("arbitrary",)),
        name=name,
    )(*args)


def _invert_body(dest_ref, gather_init_ref, scatter_init_ref, gather_ref, scatter_ref):
    pltpu.sync_copy(gather_init_ref, gather_ref)
    pltpu.sync_copy(scatter_init_ref, scatter_ref)

    def put(t, carry):
        d = dest_ref[t]
        gather_ref[d] = t
        scatter_ref[EXPERT_TILE + d] = t
        return carry

    lax.fori_loop(0, dest_ref.shape[0], put, 0, unroll=8)


def _invert_call(dest, n_slots, name):
    total = dest.shape[0]
    gather_init = jnp.zeros((n_slots,), I32)
    scatter_init = total + jnp.arange(n_slots + EXPERT_TILE, dtype=I32) % EXPERT_TILE
    smem = pl.BlockSpec(memory_space=pltpu.SMEM)
    return pl.pallas_call(
        _invert_body,
        in_specs=[smem, pl.BlockSpec(memory_space=pl.ANY), pl.BlockSpec(memory_space=pl.ANY)],
        out_specs=[smem, smem],
        out_shape=[jax.ShapeDtypeStruct(gather_init.shape, I32), jax.ShapeDtypeStruct(scatter_init.shape, I32)],
        name=name,
    )(dest, gather_init, scatter_init)


def _unrolled(lo, hi, body, carry):
    for b in range(lo, hi):
        carry = body(b, carry)
    return carry


def _experts_body(grp_ref, lo_ref, hi_ref, fresh_ref, valid_ref, gather_ref, scatter_ref, rows_ref, wg_in_ref,
                  wu_in_ref, wd_in_ref, y_ref, wg_ref, wu_ref, wd_ref, xb_ref, yb_ref, gsem, ssem, *, token_words):
    i = pl.program_id(0)
    n = pl.num_programs(0)
    f = D_EXPERT
    tile = EXPERT_TILE
    words = tile * ROW_PLANES
    out_words = tile * OUT_PLANES
    group = 8

    def start_gather(step, slot, loop):
        def body(b, carry):
            for k in range(group):
                r = b * group + k
                tok = gather_ref[step * tile + r]
                pltpu.make_async_copy(rows_ref.at[pl.ds(pl.multiple_of(tok * ROW_PLANES, ROW_PLANES), ROW_PLANES)],
                                      xb_ref.at[slot, pl.ds(pl.multiple_of(r * ROW_PLANES, ROW_PLANES), ROW_PLANES)],
                                      gsem.at[slot]).start(priority=k % 2)
            return carry

        loop(0, tile // group, body, 0)

    def start_scatter(block, slot, loop):
        def body(b, carry):
            for k in range(group):
                r = b * group + k
                tok = scatter_ref[block * tile + r]
                pltpu.make_async_copy(yb_ref.at[slot, pl.ds(pl.multiple_of(r * OUT_PLANES, OUT_PLANES), OUT_PLANES)],
                                      y_ref.at[pl.ds(pl.multiple_of(tok * OUT_PLANES, OUT_PLANES), OUT_PLANES)],
                                      ssem.at[slot]).start(priority=k % 2)
            return carry

        loop(0, tile // group, body, 0)

    def wait_gather(slot):
        pltpu.make_async_copy(rows_ref.at[pl.ds(0, words)], xb_ref.at[slot], gsem.at[slot]).wait()

    def wait_scatter(slot):
        pltpu.make_async_copy(yb_ref.at[slot], y_ref.at[pl.ds(0, out_words)], ssem.at[slot]).wait()

    @pl.when(i == 0)
    def _():
        yb_ref[1] = jnp.zeros((out_words, LANE), F32)
        spare = pltpu.make_async_copy(yb_ref.at[1], y_ref.at[pl.ds(token_words, out_words)], ssem.at[1])
        spare.start()
        spare.wait()

    del grp_ref

    @pl.when(fresh_ref[i] == 1)
    def _():
        for e in range(EXPERTS_PER_GROUP):
            wg_ref[e] = wg_in_ref[e].astype(BF16)
            wu_ref[e] = wu_in_ref[e].astype(BF16)
            wd_ref[e] = wd_in_ref[e].astype(BF16)

    nxt = jnp.minimum(i + 1, n - 1)
    more = jnp.logical_and(i + 1 < n, valid_ref[nxt] == 1)

    @pl.when(valid_ref[i] == 1)
    def _():
        slot = i % 2

        @pl.when(i == 0)
        def _():
            start_gather(0, 0, lax.fori_loop)

        wait_gather(slot)
        start_gather(nxt, 1 - slot, _unrolled)
        start_scatter(i, 1 - slot, _unrolled)
        x = jnp.concatenate([xb_ref[slot, pl.ds(j, tile, stride=ROW_PLANES), :].astype(BF16)
                             for j in range(ROW_PLANES)], axis=-1)
        def expert(e):
            act = (_silu(_dot(x, wg_ref[e])) * _dot(x, wu_ref[e])).astype(BF16)
            return _dot(act, wd_ref[e])

        y_lo = expert(lo_ref[i])
        y_hi = expert(hi_ref[i])

        @pl.when(i >= 1)
        def _():
            wait_scatter(slot)

        for j in range(ROW_PLANES):
            yb_ref[slot, pl.ds(j, tile, stride=OUT_PLANES), :] = y_lo[:, j * LANE:(j + 1) * LANE]
            yb_ref[slot, pl.ds(ROW_PLANES + j, tile, stride=OUT_PLANES), :] = y_hi[:, j * LANE:(j + 1) * LANE]

        @pl.when(jnp.logical_not(more))
        def _():
            wait_gather(1 - slot)
            wait_scatter(1 - slot)
            start_scatter(i + 1, slot, lax.fori_loop)
            wait_scatter(slot)


def _experts_call(grp, lo, hi, fresh, valid, gather_idx, scatter_idx, rows, w_gate, w_up, w_down, layer, name):
    n_tiles = grp.shape[0]
    tile = EXPERT_TILE
    d, f = D_MODEL, D_EXPERT
    npg = EXPERTS_PER_GROUP
    tokens = rows.shape[0] // ROW_PLANES
    group_block = lambda i, grp, *_: (layer, grp[i], 0, 0)
    once = pl.Buffered(1)
    return pl.pallas_call(
        functools.partial(_experts_body, token_words=tokens * OUT_PLANES),
        grid_spec=pltpu.PrefetchScalarGridSpec(
            num_scalar_prefetch=7,
            grid=(n_tiles,),
            in_specs=[pl.BlockSpec(memory_space=pl.ANY),
                      pl.BlockSpec((None, npg, d, f), group_block, pipeline_mode=once),
                      pl.BlockSpec((None, npg, d, f), group_block, pipeline_mode=once),
                      pl.BlockSpec((None, npg, f, d), group_block, pipeline_mode=once)],
            out_specs=pl.BlockSpec(memory_space=pl.ANY),
            scratch_shapes=[pltpu.VMEM((npg, d, f), BF16), pltpu.VMEM((npg, d, f), BF16),
                            pltpu.VMEM((npg, f, d), BF16),
                            pltpu.VMEM((2, tile * ROW_PLANES, LANE), F32),
                            pltpu.VMEM((2, tile * OUT_PLANES, LANE), F32),
                            pltpu.SemaphoreType.DMA((2,)), pltpu.SemaphoreType.DMA((2,))]),
        out_shape=jax.ShapeDtypeStruct(((tokens + tile) * OUT_PLANES, LANE), F32),
        compiler_params=_cparams(("arbitrary",)),
        name=name,
    )(grp, lo, hi, fresh, valid, gather_idx, scatter_idx, rows, w_gate, w_up, w_down)


def _moe_res_body(y_ref, w_ref, x_ref, gate_ref, fg_ref, o_ref, *rest, final):
    tm = x_ref.shape[0]
    y_lo = jnp.concatenate([y_ref[pl.ds(j, tm, stride=OUT_PLANES), :] for j in range(ROW_PLANES)], axis=-1)
    y_hi = jnp.concatenate([y_ref[pl.ds(ROW_PLANES + j, tm, stride=OUT_PLANES), :] for j in range(ROW_PLANES)],
                           axis=-1)
    xn = x_ref[...] + gate_ref[...] * (w_ref[:, 0:1] * y_lo + w_ref[:, 1:2] * y_hi)
    o_ref[...] = xn
    if final:
        ms = jnp.mean(xn * xn, axis=-1, keepdims=True)
        rest[0][...] = xn * lax.rsqrt(ms + EPS) * fg_ref[...]


def _moe_res_call(y_tok, weights, x, mod, final_g, tm, row_offset, final, name):
    t, d = x.shape
    blk0 = row_offset // tm
    n_out = 2 if final else 1
    return pl.pallas_call(
        functools.partial(_moe_res_body, final=final),
        grid=(t // tm,),
        in_specs=[pl.BlockSpec((tm * OUT_PLANES, LANE), lambda i: (blk0 + i, 0)),
                  pl.BlockSpec((tm, LANE), lambda i: (blk0 + i, 0)),
                  pl.BlockSpec((tm, d), lambda i: (i, 0)),
                  mod.spec(5, tm, 1),
                  pl.BlockSpec((1, d), lambda i: (0, 0))],
        out_specs=[pl.BlockSpec((tm, d), lambda i: (i, 0))] * n_out,
        out_shape=[jax.ShapeDtypeStruct((t, d), F32)] * n_out,
        compiler_params=_cparams(("arbitrary",)),
        name=name,
    )(y_tok, weights, x, mod.arr, final_g.reshape(1, d))


def _tile_tables(counts, n_tiles):
    tile = EXPERT_TILE
    tiles_per_bucket = (counts + tile - 1) // tile
    ends = jnp.cumsum(tiles_per_bucket)
    starts = ends - tiles_per_bucket
    used = ends[-1]
    ti = jnp.arange(n_tiles, dtype=I32)
    valid = (ti < used).astype(I32)
    tile_bucket = jnp.sum((jnp.minimum(ti, used - 1)[:, None] >= ends[None, :]).astype(I32), axis=1)
    pair_lo = jnp.asarray([p[0] for p in PAIRS], I32)
    pair_hi = jnp.asarray([p[1] for p in PAIRS], I32)
    grp = tile_bucket // len(PAIRS)
    lo = pair_lo[tile_bucket % len(PAIRS)]
    hi = pair_hi[tile_bucket % len(PAIRS)]
    fresh = jnp.concatenate([jnp.ones((1,), I32), (grp[1:] != grp[:-1]).astype(I32)])
    return grp, lo, hi, fresh, valid, (starts * tile).astype(I32)


def _moe_layer(xp, xs, norm_g2, mod_p, mod_s, router_w, router_b, w_gate, w_up, w_down, layer, final_g, final):
    tp, ts = xp.shape[0], xs.shape[0]
    total = tp + ts
    n_tiles = -(-total // EXPERT_TILE) + N_BUCKETS
    zero_counts = jnp.zeros((BUCKET_ROWS, LANE), F32)
    rows, weights, bucket, rank, counts = _route_call(xp, norm_g2, mod_p, router_w, router_b, zero_counts, 512, total,
                                                      0, None, "route_prompt")
    rows, weights, bucket, rank, counts = _route_call(xs, norm_g2, mod_s, router_w, router_b, counts, ts, total, tp,
                                                      (rows, weights, bucket, rank), "route_sample")
    grp, lo, hi, fresh, valid, bucket_start = _tile_tables(counts[:N_BUCKETS, 0].astype(I32), n_tiles)
    dest = bucket_start[bucket[0]] + rank[0]
    gather_idx, scatter_idx = _invert_call(dest, n_tiles * EXPERT_TILE, "invert")
    y_tok = _experts_call(grp, lo, hi, fresh, valid, gather_idx, scatter_idx, rows, w_gate, w_up, w_down, layer,
                          "experts")
    outp = _moe_res_call(y_tok, weights, xp, mod_p, final_g, 512, 0, final, "moe_res_prompt")
    outs = _moe_res_call(y_tok, weights, xs, mod_s, final_g, ts, tp, final, "moe_res_sample")
    return outp, outs


def _mamba_layer(x, g, mod, conv_prev, ssm, w_zx, w_dt, conv_w, conv_b, dt_bias, a_log, d_skip, norm_g, w_out,
                 tm, tag):
    pad_h = lambda v: jnp.pad(v.reshape(1, -1), ((0, 0), (0, LANE - A_N_HEADS)))
    tm_in = min(2 * tm, x.shape[0])
    proj = _norm_mm_call(x, g, mod, 1, 0, w_zx, tm_in, 512, BF16, "a_in_" + tag)
    dt_raw = _norm_mm3_call(x, g, mod, 1, 0, w_dt, tm, "a_dt_" + tag)
    d_x = jnp.repeat(d_skip, A_HEAD_DIM).reshape(1, A_D_INNER)
    weights = (conv_w, conv_b.reshape(1, -1), pad_h(dt_bias), pad_h(a_log), d_x, norm_g.reshape(1, -1))
    if ssm[0] == "step":
        yn, conv_new, ssm_new = _ssd_step_call(proj, dt_raw, conv_prev, ssm[1], ssm[2], ssm[3], *weights,
                                               "ssd_step_" + tag)
    else:
        yn, conv_new, ssm_new = _ssd_call(proj, dt_raw, conv_prev, ssm[2], *weights, ssm[1], "ssd_" + tag)
    x = _out_res_call(yn, w_out, x, mod, 2, tm, "a_out_" + tag)
    return x, conv_new, ssm_new


def kernel(x_prompt, x_sample, c_prompt, c_sample, state_a_conv, state_a_ssm, state_c_pool, w_mod, b_mod, norm_g, final_g, a_w_in, a_conv_w, a_conv_b, a_dt_bias, a_log, a_d, a_norm_g, a_w_out, b_w_in, b_b_in, b_ln_g, b_ln_b, b_w_s, b_b_s, b_w_out, c_w_g, c_scale, router_w, router_b, e_w_gate, e_w_up, e_w_down):
    bp, seq, d = x_prompt.shape
    bs = x_sample.shape[0]
    n_a, n_c = state_a_conv.shape[0], state_c_pool.shape[0]
    mod_all = _mod_call(jnp.concatenate([c_prompt, c_sample], axis=0), w_mod, b_mod)
    mod_p_arr = mod_all[:, :bp].reshape(DEPTH, bp, 6, 1, d)
    mod_s_arr = mod_all[:, bp:]
    xp = x_prompt.reshape(bp * seq, d)
    xs = x_sample.reshape(bs, d)
    conv_p, ssm_p, pool_p, conv_s, pool_s, v_s = [], [], [], [], [], []
    hp = A_N_HEADS * A_HEAD_DIM
    ssm_s_in = state_a_ssm.reshape(n_a, bs, hp, A_D_STATE)
    ssm_s_out = None
    yp = ys = None
    for i in range(DEPTH):
        kind, s = LAYER_KIND[i], LAYER_SLOT[i]
        mod_p = Mod(mod_p_arr, i, False, seq)
        mod_s = Mod(mod_s_arr, i, True)
        g1 = norm_g[i, 0]
        if kind == 0:
            w_zx = a_w_in[s, :, :A_ZX].astype(BF16)
            w_dt = jnp.pad(a_w_in[s, :, A_ZX:], ((0, 0), (0, LANE - A_N_HEADS)))
            weights = (w_zx, w_dt, a_conv_w[s], a_conv_b[s], a_dt_bias[s], a_log[s], a_d[s], a_norm_g[s], a_w_out[s])
            conv0 = jnp.zeros((bp, A_CONV - 1, A_CONV_CH), F32)
            ssm0 = jnp.zeros((bp, hp, A_D_STATE), F32)
            xp, cv, ss = _mamba_layer(xp, g1, mod_p, conv0, ("prompt", bp, ssm0), *weights, 1024, "p%d" % i)
            conv_p.append(cv)
            ssm_p.append(ss.reshape(bp, A_N_HEADS, A_HEAD_DIM, A_D_STATE))
            xs, cv, ssm_s_out = _mamba_layer(xs, g1, mod_s, state_a_conv[s], ("step", ssm_s_in, s, ssm_s_out),
                                             *weights, bs, "s%d" % i)
            conv_s.append(cv)
        elif kind == 1:
            w_uv = b_w_in[s].astype(BF16)
            uv = _gmlp_in_call(xp, g1, mod_p, w_uv, b_b_in[s], b_ln_g[s], b_ln_b[s], 512, BF16, "b_in_p%d" % i)
            xp = _gmlp_out_call(uv, b_w_s[s], b_b_s[s], b_w_out[s], xp, mod_p, 512, "b_out_p%d" % i)
            uv = _gmlp_in_call(xs, g1, mod_s, w_uv, b_b_in[s], b_ln_g[s], b_ln_b[s], bs, F32, "b_in_s%d" % i)
            xs = _gmlp_out_step_call(uv, b_w_s[s], b_b_s[s], b_w_out[s], xs, mod_s, "b_out_s%d" % i)
            v_s.append(uv[:, B_D:].reshape(bs, 1, B_D))
        else:
            pool0 = jnp.zeros((bp, C_STATE, d), F32)
            xp, pr = _pool_call(xp, g1, mod_p, pool0, c_w_g[s], c_scale[s], bp, 512, 0, "pool_p%d" % i)
            pool_p.append(pr)
            xs, pr = _pool_step_call(xs, g1, mod_s, state_c_pool[s], c_w_g[s], c_scale[s], PAST_LEN, "pool_s%d" % i)
            pool_s.append(pr)
        final = i == DEPTH - 1
        outp, outs = _moe_layer(xp, xs, norm_g[i, 1], mod_p, mod_s, router_w, router_b, e_w_gate, e_w_up, e_w_down, i,
                                final_g, final)
        xp, xs = outp[0], outs[0]
        if final:
            yp, ys = outp[1], outs[1]
    return (yp.reshape(bp, seq, d), ys.reshape(bs, 1, d), jnp.stack(conv_p), jnp.stack(ssm_p), jnp.stack(pool_p),
            jnp.stack(conv_s), ssm_s_out.reshape(state_a_ssm.shape), jnp.stack(pool_s), jnp.stack(v_s))
```

```python
import functools
import math

import numpy as np
import jax
import jax.numpy as jnp
from jax import lax
from jax.experimental import pallas as pl
from jax.experimental.pallas import tpu as pltpu

F32 = jnp.float32
BF16 = jnp.bfloat16
I32 = jnp.int32
EPS = 1e-6

LANE = 128
D_MODEL = 1024
DEPTH = 4
PAST_LEN = 16384
LAYER_KIND = (0, 1, 2, 0)
LAYER_SLOT = (0, 0, 0, 1)
A_D_INNER = 2 * D_MODEL
A_HEAD_DIM = 64
A_N_HEADS = A_D_INNER // A_HEAD_DIM
A_N_GROUPS = 8
A_HPG = A_N_HEADS // A_N_GROUPS
A_D_STATE = 128
A_GN = A_N_GROUPS * A_D_STATE
A_CONV = 4
A_CONV_CH = A_D_INNER + 2 * A_GN
A_ZX = A_D_INNER + A_CONV_CH
A_CHUNK = 128
A_GW = A_HPG * A_HEAD_DIM
B_D = 2 * D_MODEL
B_N_GROUPS = 8
B_GROUP_DIM = B_D // B_N_GROUPS
B_CHUNK = 128
C_WINDOWS = (2, 4, 8, 16)
C_GROUP_DIM = D_MODEL // len(C_WINDOWS)
C_STATE = max(C_WINDOWS) - 1
N_EXPERTS = 16
N_EXPERT_GROUPS = 4
EXPERTS_PER_GROUP = 4
D_EXPERT = D_MODEL // 2
PAIRS = ((0, 1), (0, 2), (0, 3), (1, 2), (1, 3), (2, 3))
N_BUCKETS = N_EXPERT_GROUPS * len(PAIRS)
BUCKET_ROWS = 32
ROW_PLANES = D_MODEL // LANE
OUT_PLANES = 2 * ROW_PLANES
EXPERT_TILE = 256
VMEM_LIMIT = 56 * 1024 * 1024


def _cparams(sem, vmem=VMEM_LIMIT):
    return pltpu.CompilerParams(dimension_semantics=sem, vmem_limit_bytes=vmem)


def _sigmoid(x):
    return 1.0 / (1.0 + jnp.exp(-x))


def _silu(x):
    return x * _sigmoid(x)


def _gelu_tanh(x):
    c = 2.0 * math.sqrt(2.0 / math.pi)
    return x / (1.0 + jnp.exp(x * (-c - (c * 0.044715) * (x * x))))


def _softplus(x):
    return jnp.maximum(x, 0.0) + jnp.log1p(jnp.exp(-jnp.abs(x)))


def _split2(a):
    hi = a.astype(BF16)
    lo = (a - hi.astype(F32)).astype(BF16)
    return hi, lo


def _split3(a):
    hi = a.astype(BF16)
    r = a - hi.astype(F32)
    mid = r.astype(BF16)
    lo = (r - mid.astype(F32)).astype(BF16)
    return hi, mid, lo


def _dot(a, b):
    return jnp.dot(a, b, preferred_element_type=F32)


def _dot_nt(a, b):
    return lax.dot_general(a, b, (((1,), (1,)), ((), ())), preferred_element_type=F32)


def _dot_tn(a, b):
    return lax.dot_general(a, b, (((0,), (0,)), ((), ())), preferred_element_type=F32)


def _dot3(a, b):
    a_hi, a_lo = _split2(a)
    b_hi, b_lo = _split2(b)
    return _dot(a_hi, b_hi) + (_dot(a_lo, b_hi) + _dot(a_hi, b_lo))


def _prenorm(x, g, sc, sh):
    ms = jnp.mean(x * x, axis=-1, keepdims=True)
    return (x * lax.rsqrt(ms + EPS) * g) * (1.0 + sc) + sh


def _mod_body(c_ref, w_ref, b_ref, o_ref):
    o_ref[...] = _dot3(_silu(c_ref[...]), w_ref[...]) + b_ref[...]


def _mod_call(c_all, w_mod, b_mod):
    nb, d = c_all.shape
    depth, _, n = w_mod.shape
    tn = 1536
    return pl.pallas_call(
        _mod_body,
        grid=(depth, n // tn),
        in_specs=[pl.BlockSpec((nb, d), lambda i, j: (0, 0)),
                  pl.BlockSpec((None, d, tn), lambda i, j: (i, 0, j)),
                  pl.BlockSpec((None, 1, tn), lambda i, j: (i, 0, j))],
        out_specs=pl.BlockSpec((None, nb, tn), lambda i, j: (i, 0, j)),
        out_shape=jax.ShapeDtypeStruct((depth, nb, n), F32),
        compiler_params=_cparams(("arbitrary", "arbitrary")),
        name="mod",
    )(c_all, w_mod, b_mod.reshape(depth, 1, n))


class Mod:
    def __init__(self, arr, layer, per_row, rows_per_seq=None):
        self.arr, self.layer, self.per_row, self.rows_per_seq = arr, layer, per_row, rows_per_seq

    def spec(self, which, tm, ngrid):
        layer = self.layer
        if self.per_row:
            if ngrid == 1:
                return pl.BlockSpec((None, tm, D_MODEL), lambda i: (layer, i, which))
            return pl.BlockSpec((None, tm, D_MODEL), lambda i, j: (layer, i, which))
        tiles = self.rows_per_seq // tm
        if ngrid == 1:
            return pl.BlockSpec((None, None, None, 1, D_MODEL), lambda i: (layer, i // tiles, which, 0, 0))
        return pl.BlockSpec((None, None, None, 1, D_MODEL), lambda i, j: (layer, i // tiles, which, 0, 0))


def _norm_mm_body(x_ref, g_ref, sc_ref, sh_ref, w_ref, o_ref, hn_ref):
    @pl.when(pl.program_id(1) == 0)
    def _():
        hn_ref[...] = _prenorm(x_ref[...], g_ref[...], sc_ref[...], sh_ref[...]).astype(BF16)

    o_ref[...] = _dot(hn_ref[...], w_ref[...]).astype(o_ref.dtype)


def _norm_mm_call(x, g, mod, which_sc, which_sh, w, tm, tn, out_dtype, name):
    t, d = x.shape
    n_cols = w.shape[1]
    return pl.pallas_call(
        _norm_mm_body,
        grid=(t // tm, n_cols // tn),
        in_specs=[pl.BlockSpec((tm, d), lambda i, j: (i, 0)),
                  pl.BlockSpec((1, d), lambda i, j: (0, 0)),
                  mod.spec(which_sc, tm, 2), mod.spec(which_sh, tm, 2),
                  pl.BlockSpec((d, tn), lambda i, j: (0, j))],
        out_specs=pl.BlockSpec((tm, tn), lambda i, j: (i, j)),
        out_shape=jax.ShapeDtypeStruct((t, n_cols), out_dtype),
        scratch_shapes=[pltpu.VMEM((tm, d), BF16)],
        compiler_params=_cparams(("arbitrary", "arbitrary")),
        name=name,
    )(x, g.reshape(1, d), mod.arr, mod.arr, w)


def _norm_mm3_body(x_ref, g_ref, sc_ref, sh_ref, w_ref, o_ref):
    hn = _prenorm(x_ref[...], g_ref[...], sc_ref[...], sh_ref[...])
    o_ref[...] = _dot3(hn, w_ref[...])


def _norm_mm3_call(x, g, mod, which_sc, which_sh, w, tm, name):
    t, d = x.shape
    n = w.shape[1]
    return pl.pallas_call(
        _norm_mm3_body,
        grid=(t // tm,),
        in_specs=[pl.BlockSpec((tm, d), lambda i: (i, 0)),
                  pl.BlockSpec((1, d), lambda i: (0, 0)),
                  mod.spec(which_sc, tm, 1), mod.spec(which_sh, tm, 1),
                  pl.BlockSpec((d, n), lambda i: (0, 0))],
        out_specs=pl.BlockSpec((tm, n), lambda i: (i, 0)),
        out_shape=jax.ShapeDtypeStruct((t, n), F32),
        compiler_params=_cparams(("arbitrary",)),
        name=name,
    )(x, g.reshape(1, d), mod.arr, mod.arr, w)


def _out_res_body(y_ref, w_ref, x_ref, gate_ref, o_ref, wbf_ref):
    @pl.when(pl.program_id(0) == 0)
    def _():
        wbf_ref[...] = w_ref[...].astype(BF16)

    o_ref[...] = x_ref[...] + gate_ref[...] * _dot(y_ref[...], wbf_ref[...])


def _out_res_call(y, w, x, mod, which_gate, tm, name):
    t, k = y.shape
    d = x.shape[1]
    return pl.pallas_call(
        _out_res_body,
        grid=(t // tm,),
        in_specs=[pl.BlockSpec((tm, k), lambda i: (i, 0)),
                  pl.BlockSpec((k, d), lambda i: (0, 0)),
                  pl.BlockSpec((tm, d), lambda i: (i, 0)),
                  mod.spec(which_gate, tm, 1)],
        out_specs=pl.BlockSpec((tm, d), lambda i: (i, 0)),
        out_shape=jax.ShapeDtypeStruct((t, d), F32),
        scratch_shapes=[pltpu.VMEM((k, d), BF16)],
        compiler_params=_cparams(("arbitrary",)),
        name=name,
    )(y, w, x, mod.arr)


def _head_expand():
    h = np.arange(LANE)[:, None]
    c = np.arange(A_D_INNER)[None, :]
    return jnp.asarray((c // A_HEAD_DIM == h).astype(np.float32), dtype=BF16)


def _gate_norm(y, z, ng):
    gated = y * _silu(z)
    ms = jnp.mean(gated * gated, axis=-1, keepdims=True)
    return gated * lax.rsqrt(ms + EPS) * ng


def _ssd_body(z_ref, xs_ref, bc_ref, dt_ref, cprev_ref, sprev_ref, cw_ref, cb_ref, dtb_ref, alog_ref,
              dx_ref, ng_ref, exp_ref, yn_ref, cnew_ref, snew_ref, tail_ref, act_ref):
    c = pl.program_id(1)
    q = A_CHUNK
    nt = A_CONV - 1

    @pl.when(c == 0)
    def _():
        tail_ref[0:nt, :] = cprev_ref[...]
        snew_ref[...] = sprev_ref[...]

    row = lax.broadcasted_iota(I32, (q, q), 0)
    col = lax.broadcasted_iota(I32, (q, q), 1)
    shifts = [jnp.where(row - col == nt - k, 1.0, 0.0).astype(BF16) for k in range(nt)]
    sub = lax.broadcasted_iota(I32, (8, 1), 0)
    cw = 512
    for j in range(A_CONV_CH // cw):
        sl = slice(j * cw, (j + 1) * cw)
        src = xs_ref if (j + 1) * cw <= A_D_INNER else bc_ref
        off = j * cw if src is xs_ref else j * cw - A_D_INNER
        x_bf = src[:, off:off + cw]
        conv = cb_ref[:, sl] + x_bf.astype(F32) * cw_ref[nt:nt + 1, sl]
        for k in range(nt):
            conv = conv + _dot(shifts[k], x_bf) * cw_ref[k:k + 1, sl]
        act_ref[:, sl] = _silu(conv)
        corr = jnp.zeros((8, cw), F32)
        for l in range(nt):
            c_l = sum(tail_ref[l + k:l + k + 1, sl] * cw_ref[k:k + 1, sl] for k in range(nt - l))
            corr = jnp.where(sub == l, c_l, corr)
        act_ref[0:8, sl] = _silu(conv[0:8, :] + corr)
        tail_ref[0:nt, sl] = x_bf[q - 8:q, :].astype(F32)[8 - nt:8, :]

    @pl.when(c == pl.num_programs(1) - 1)
    def _():
        cnew_ref[...] = tail_ref[0:nt, :]

    dt = _softplus(dt_ref[...] + dtb_ref[...])
    a = dt * (-jnp.exp(alog_ref[...]))
    row = lax.broadcasted_iota(I32, (q, q), 0)
    col = lax.broadcasted_iota(I32, (q, q), 1)
    causal = row >= col
    tril = jnp.where(causal, 1.0, 0.0).astype(BF16)
    a_hi, a_mid, a_lo = _split3(a)
    cs = _dot(tril, a_hi) + (_dot(tril, a_mid) + _dot(tril, a_lo))
    cs_t = cs.T
    dt_t = dt.T
    cs_last = cs[q - 1:q, :]
    ecs = jnp.exp(cs)
    wend = jnp.exp(cs_last - cs) * dt
    st_hi, st_lo = _split2(jnp.concatenate([ecs, wend], axis=0))
    st_x = _dot(st_hi, exp_ref[...]) + _dot(st_lo, exp_ref[...])
    lane_head = lax.broadcasted_iota(I32, (q, A_GW), 1) // A_HEAD_DIM

    for g in range(A_N_GROUPS):
        gsl = slice(g * A_GW, (g + 1) * A_GW)
        b_g = act_ref[:, A_D_INNER + g * A_D_STATE:A_D_INNER + (g + 1) * A_D_STATE].astype(BF16)
        c_g = act_ref[:, A_D_INNER + A_GN + g * A_D_STATE:A_D_INNER + A_GN + (g + 1) * A_D_STATE].astype(BF16)
        x_g = act_ref[:, gsl]
        x_bf = x_g.astype(BF16)
        h_g = snew_ref[gsl, :]
        cb = _dot_nt(c_g, b_g)
        y = jnp.zeros((q, A_GW), F32)
        for r in range(A_HPG):
            h = g * A_HPG + r
            seg = cs[:, h:h + 1] - cs_t[h:h + 1, :]
            decay = jnp.exp(jnp.where(causal, seg, -1e30))
            wm = (cb * decay * dt_t[h:h + 1, :]).astype(BF16)
            y = y + _dot(wm, jnp.where(lane_head == r, x_bf, jnp.zeros_like(x_bf)))
        y = y + st_x[0:q, gsl] * _dot_nt(c_g, h_g.astype(BF16)) + dx_ref[:, gsl] * x_g
        yn_ref[:, gsl] = _gate_norm(y, z_ref[:, gsl].astype(F32), ng_ref[:, gsl]).astype(BF16)
        s_new = _dot_tn((x_g * st_x[q:2 * q, gsl]).astype(BF16), b_g)
        for r in range(A_HPG):
            h = g * A_HPG + r
            rsl = slice(g * A_GW + r * A_HEAD_DIM, g * A_GW + (r + 1) * A_HEAD_DIM)
            keep = jnp.exp(cs[q - 1:q, h:h + 1])
            snew_ref[rsl, :] = snew_ref[rsl, :] * keep + s_new[r * A_HEAD_DIM:(r + 1) * A_HEAD_DIM, :]


def _ssd_call(proj, dt_raw, conv_prev, ssm_prev, conv_w, conv_b, dt_bias, a_log, d_x, norm_g, nb, name):
    t = proj.shape[0]
    q = A_CHUNK
    nc = t // nb // q
    hp = A_N_HEADS * A_HEAD_DIM
    row = lambda b, c: (b * nc + c, 0)
    full = lambda shape: pl.BlockSpec(shape, lambda b, c: (0,) * len(shape))
    return pl.pallas_call(
        _ssd_body,
        grid=(nb, nc),
        in_specs=[pl.BlockSpec((q, A_D_INNER), lambda b, c: (b * nc + c, 0)),
                  pl.BlockSpec((q, A_D_INNER), lambda b, c: (b * nc + c, 1)),
                  pl.BlockSpec((q, 2 * A_GN), lambda b, c: (b * nc + c, 2)),
                  pl.BlockSpec((q, LANE), row),
                  pl.BlockSpec((None, A_CONV - 1, A_CONV_CH), lambda b, c: (b, 0, 0)),
                  pl.BlockSpec((None, hp, A_D_STATE), lambda b, c: (b, 0, 0)),
                  full((A_CONV, A_CONV_CH)), full((1, A_CONV_CH)), full((1, LANE)), full((1, LANE)),
                  full((1, A_D_INNER)), full((1, A_D_INNER)), full((LANE, A_D_INNER))],
        out_specs=[pl.BlockSpec((q, A_D_INNER), row),
                   pl.BlockSpec((None, A_CONV - 1, A_CONV_CH), lambda b, c: (b, 0, 0)),
                   pl.BlockSpec((None, hp, A_D_STATE), lambda b, c: (b, 0, 0))],
        out_shape=[jax.ShapeDtypeStruct((t, A_D_INNER), BF16),
                   jax.ShapeDtypeStruct((nb, A_CONV - 1, A_CONV_CH), F32),
                   jax.ShapeDtypeStruct((nb, hp, A_D_STATE), F32)],
        scratch_shapes=[pltpu.VMEM((8, A_CONV_CH), F32), pltpu.VMEM((q, A_CONV_CH), F32)],
        compiler_params=_cparams(("arbitrary", "arbitrary")),
        name=name,
    )(proj, proj, proj, dt_raw, conv_prev, ssm_prev, conv_w, conv_b, dt_bias, a_log, d_x, norm_g, _head_expand())


def _ssd_step_body(z_ref, xs_ref, bc_ref, dt_ref, cprev_ref, sprev_ref, cw_ref, cb_ref, dtb_ref, alog_ref,
                   dx_ref, ng_ref, exp_ref, yn_ref, cnew_ref, snew_ref, da_ref, y_ref):
    bt = z_ref.shape[0]
    cur = jnp.concatenate([xs_ref[...], bc_ref[...]], axis=1).astype(F32)
    conv = cb_ref[...] + cur * cw_ref[A_CONV - 1:A_CONV, :]
    for k in range(A_CONV - 1):
        conv = conv + cprev_ref[:, k, :] * cw_ref[k:k + 1, :]
    for k in range(A_CONV - 2):
        cnew_ref[:, k, :] = cprev_ref[:, k + 1, :]
    cnew_ref[:, A_CONV - 2, :] = cur
    act = _silu(conv)
    xs = act[:, 0:A_D_INNER]
    bm_bf = act[:, A_D_INNER:A_D_INNER + A_GN].astype(BF16)
    cm_bf = act[:, A_D_INNER + A_GN:A_CONV_CH].astype(BF16)
    dt = _softplus(dt_ref[...] + dtb_ref[...])
    da_ref[...] = jnp.exp(dt * (-jnp.exp(alog_ref[...])))
    dt_hi, dt_lo = _split2(dt)
    dt_x = _dot(dt_hi, exp_ref[...]) + _dot(dt_lo, exp_ref[...])
    xdt = xs * dt_x
    rows = lax.broadcasted_iota(I32, (bt, 1), 0)
    y_ref[...] = jnp.zeros_like(y_ref)

    def per_seq(j, carry):
        mine = rows == j
        xdt_j = jnp.where(mine, xdt, 0.0)
        da_j = da_ref[pl.ds(j, 1), :]
        for g in range(A_N_GROUPS):
            gsl = slice(g * A_GW, (g + 1) * A_GW)
            nsl = slice(g * A_D_STATE, (g + 1) * A_D_STATE)
            outer = _dot_tn(xdt_j[:, gsl].astype(BF16), bm_bf[:, nsl])
            for r in range(A_HPG):
                h = g * A_HPG + r
                rsl = slice(r * A_HEAD_DIM, (r + 1) * A_HEAD_DIM)
                hsl = slice(g * A_GW + r * A_HEAD_DIM, g * A_GW + (r + 1) * A_HEAD_DIM)
                snew_ref[j, hsl, :] = sprev_ref[j, hsl, :] * da_j[:, h:h + 1] + outer[rsl, :]
            yg = _dot_nt(cm_bf[:, nsl], snew_ref[j, gsl, :].astype(BF16))
            y_ref[:, gsl] = y_ref[:, gsl] + jnp.where(mine, yg, 0.0)
        return carry

    lax.fori_loop(0, bt, per_seq, 0)
    y = y_ref[...] + dx_ref[...] * xs
    z = z_ref[...].astype(F32)
    for g in range(A_N_GROUPS):
        gsl = slice(g * A_GW, (g + 1) * A_GW)
        yn_ref[:, gsl] = _gate_norm(y[:, gsl], z[:, gsl], ng_ref[:, gsl]).astype(BF16)


_SSD_STEP_INPUTS = 13


def _ssd_step_aliased_body(*refs):
    _ssd_step_body(*refs[:_SSD_STEP_INPUTS], *refs[_SSD_STEP_INPUTS + 1:])


def _ssd_step_call(proj, dt_raw, conv_prev, ssm_all, slot, ssm_out, conv_w, conv_b, dt_bias, a_log, d_x, norm_g,
                   name):
    nb = proj.shape[0]
    bt = 8
    hp = A_N_HEADS * A_HEAD_DIM
    full = lambda shape: pl.BlockSpec(shape, lambda i: (0,) * len(shape))
    state_spec = pl.BlockSpec((None, bt, hp, A_D_STATE), lambda i: (slot, i, 0, 0))
    in_specs = [pl.BlockSpec((bt, A_D_INNER), lambda i: (i, 0)),
                pl.BlockSpec((bt, A_D_INNER), lambda i: (i, 1)),
                pl.BlockSpec((bt, 2 * A_GN), lambda i: (i, 2)),
                pl.BlockSpec((bt, LANE), lambda i: (i, 0)),
                pl.BlockSpec((bt, A_CONV - 1, A_CONV_CH), lambda i: (i, 0, 0)),
                state_spec,
                full((A_CONV, A_CONV_CH)), full((1, A_CONV_CH)), full((1, LANE)), full((1, LANE)),
                full((1, A_D_INNER)), full((1, A_D_INNER)), full((LANE, A_D_INNER))]
    args = [proj, proj, proj, dt_raw, conv_prev, ssm_all, conv_w, conv_b, dt_bias, a_log, d_x, norm_g, _head_expand()]
    assert len(args) == _SSD_STEP_INPUTS
    aliases = {}
    if ssm_out is not None:
        in_specs.append(pl.BlockSpec(memory_space=pl.ANY))
        args.append(ssm_out)
        aliases = {_SSD_STEP_INPUTS: 2}
    return pl.pallas_call(
        _ssd_step_body if ssm_out is None else _ssd_step_aliased_body,
        grid=(nb // bt,),
        in_specs=in_specs,
        out_specs=[pl.BlockSpec((bt, A_D_INNER), lambda i: (i, 0)),
                   pl.BlockSpec((bt, A_CONV - 1, A_CONV_CH), lambda i: (i, 0, 0)),
                   state_spec],
        out_shape=[jax.ShapeDtypeStruct((nb, A_D_INNER), BF16),
                   jax.ShapeDtypeStruct((nb, A_CONV - 1, A_CONV_CH), F32),
                   jax.ShapeDtypeStruct(ssm_all.shape, F32)],
        scratch_shapes=[pltpu.VMEM((bt, LANE), F32), pltpu.VMEM((bt, A_D_INNER), F32)],
        input_output_aliases=aliases,
        compiler_params=_cparams(("arbitrary",)),
        name=name,
    )(*args)


def _gmlp_in_body(x_ref, g_ref, sc_ref, sh_ref, w_ref, b_ref, lg_ref, lb_ref, o_ref, hn_ref):
    j = pl.program_id(1)

    @pl.when(j == 0)
    def _():
        hn_ref[...] = _prenorm(x_ref[...], g_ref[...], sc_ref[...], sh_ref[...]).astype(BF16)

    uv = _gelu_tanh(_dot(hn_ref[...], w_ref[...]) + b_ref[...])

    @pl.when(j == 0)
    def _():
        o_ref[...] = uv.astype(o_ref.dtype)

    @pl.when(j == 1)
    def _():
        vc = uv - jnp.mean(uv, axis=-1, keepdims=True)
        var = jnp.mean(vc * vc, axis=-1, keepdims=True)
        o_ref[...] = (vc * lax.rsqrt(var + EPS) * lg_ref[...] + lb_ref[...]).astype(o_ref.dtype)


def _gmlp_in_call(x, g, mod, w, b, ln_g, ln_b, tm, out_dtype, name):
    t, d = x.shape
    return pl.pallas_call(
        _gmlp_in_body,
        grid=(t // tm, 2),
        in_specs=[pl.BlockSpec((tm, d), lambda i, j: (i, 0)),
                  pl.BlockSpec((1, d), lambda i, j: (0, 0)),
                  mod.spec(1, tm, 2), mod.spec(0, tm, 2),
                  pl.BlockSpec((d, B_D), lambda i, j: (0, j)),
                  pl.BlockSpec((1, B_D), lambda i, j: (0, j)),
                  pl.BlockSpec((1, B_D), lambda i, j: (0, 0)),
                  pl.BlockSpec((1, B_D), lambda i, j: (0, 0))],
        out_specs=pl.BlockSpec((tm, B_D), lambda i, j: (i, j)),
        out_shape=jax.ShapeDtypeStruct((t, 2 * B_D), out_dtype),
        scratch_shapes=[pltpu.VMEM((tm, d), BF16)],
        compiler_params=_cparams(("arbitrary", "arbitrary")),
        name=name,
    )(x, g.reshape(1, d), mod.arr, mod.arr, w, b.reshape(1, -1), ln_g.reshape(1, -1), ln_b.reshape(1, -1))


def _gmlp_out_body(u_ref, v_ref, ws_ref, bs_ref, w_ref, x_ref, gate_ref, o_ref, wbf_ref, wsbf_ref, m_ref):
    q = B_CHUNK

    @pl.when(pl.program_id(0) == 0)
    def _():
        wbf_ref[...] = w_ref[...].astype(BF16)
        causal = lax.broadcasted_iota(I32, (q, q), 0) >= lax.broadcasted_iota(I32, (q, q), 1)
        for g in range(B_N_GROUPS):
            wsbf_ref[g] = jnp.where(causal, ws_ref[g], 0.0).astype(BF16)

    for ci in range(u_ref.shape[0] // q):
        rsl = slice(ci * q, (ci + 1) * q)
        for g in range(B_N_GROUPS):
            gsl = slice(g * B_GROUP_DIM, (g + 1) * B_GROUP_DIM)
            mixed = _dot(wsbf_ref[g], v_ref[rsl, gsl].astype(BF16)) + bs_ref[:, g:g + 1]
            m_ref[rsl, gsl] = (u_ref[rsl, gsl].astype(F32) * mixed).astype(BF16)
    o_ref[...] = x_ref[...] + gate_ref[...] * _dot(m_ref[...], wbf_ref[...])


def _gmlp_out_call(uv, w_s, b_s, w_out, x, mod, tm, name):
    t, d = x.shape
    q = B_CHUNK
    return pl.pallas_call(
        _gmlp_out_body,
        grid=(t // tm,),
        in_specs=[pl.BlockSpec((tm, B_D), lambda i: (i, 0)),
                  pl.BlockSpec((tm, B_D), lambda i: (i, 1)),
                  pl.BlockSpec((B_N_GROUPS, q, q), lambda i: (0, 0, 0)),
                  pl.BlockSpec((q, B_N_GROUPS), lambda i: (0, 0)),
                  pl.BlockSpec((B_D, d), lambda i: (0, 0)),
                  pl.BlockSpec((tm, d), lambda i: (i, 0)),
                  mod.spec(2, tm, 1)],
        out_specs=pl.BlockSpec((tm, d), lambda i: (i, 0)),
        out_shape=jax.ShapeDtypeStruct((t, d), F32),
        scratch_shapes=[pltpu.VMEM((B_D, d), BF16), pltpu.VMEM((B_N_GROUPS, q, q), BF16),
                        pltpu.VMEM((tm, B_D), BF16)],
        compiler_params=_cparams(("arbitrary",)),
        name=name,
    )(uv, uv, w_s, b_s.T, w_out, x, mod.arr)


def _gmlp_out_step_body(u_ref, v_ref, wd_ref, bd_ref, w_ref, x_ref, gate_ref, o_ref):
    mixed = v_ref[...] * wd_ref[...] + bd_ref[...]
    m = (u_ref[...] * mixed).astype(BF16)
    o_ref[...] = x_ref[...] + gate_ref[...] * _dot(m, w_ref[...].astype(BF16))


def _gmlp_out_step_call(uv, w_s, b_s, w_out, x, mod, name):
    t, d = x.shape
    wd = jnp.repeat(w_s[:, 0, 0], B_GROUP_DIM).reshape(1, B_D)
    bd = jnp.repeat(b_s[:, 0], B_GROUP_DIM).reshape(1, B_D)
    return pl.pallas_call(
        _gmlp_out_step_body,
        grid=(1,),
        in_specs=[pl.BlockSpec((t, B_D), lambda i: (0, 0)),
                  pl.BlockSpec((t, B_D), lambda i: (0, 1)),
                  pl.BlockSpec((1, B_D), lambda i: (0, 0)),
                  pl.BlockSpec((1, B_D), lambda i: (0, 0)),
                  pl.BlockSpec((B_D, d), lambda i: (0, 0)),
                  pl.BlockSpec((t, d), lambda i: (0, 0)),
                  mod.spec(2, t, 1)],
        out_specs=pl.BlockSpec((t, d), lambda i: (0, 0)),
        out_shape=jax.ShapeDtypeStruct((t, d), F32),
        compiler_params=_cparams(("arbitrary",)),
        name=name,
    )(uv, uv, wd, bd, w_out, x, mod.arr)


def _pool_matmul(pooled, wg_ref):
    outs = []
    for gi in range(len(C_WINDOWS)):
        gsl = slice(gi * C_GROUP_DIM, (gi + 1) * C_GROUP_DIM)
        outs.append(_dot(pooled[:, gsl].astype(BF16), wg_ref[gi].astype(BF16)))
    return jnp.concatenate(outs, axis=-1)


def _pool_body(x_ref, g_ref, sc_ref, sh_ref, gate_ref, prev_ref, wg_ref, scale_ref, o_ref, pnew_ref, hp_ref,
               *, tiles_per_seq, start):
    i = pl.program_id(0)
    tm = x_ref.shape[0]
    top = 16
    ti = i % tiles_per_seq

    @pl.when(ti == 0)
    def _():
        hp_ref[top - C_STATE:top, :] = prev_ref[...]

    hn = _prenorm(x_ref[...], g_ref[...], sc_ref[...], sh_ref[...])
    hp_ref[top:top + tm, :] = hn
    pos = start + ti * tm + lax.broadcasted_iota(I32, (tm, 1), 0)
    outs = []
    for gi, w in enumerate(C_WINDOWS):
        gsl = slice(gi * C_GROUP_DIM, (gi + 1) * C_GROUP_DIM)
        acc = hn[:, gsl]
        for k in range(1, w):
            acc = acc + hp_ref[top - k:top - k + tm, gsl]
        cnt = jnp.minimum(pos + 1, w).astype(F32)
        outs.append(acc / cnt - hn[:, gsl])
    y = _pool_matmul(jnp.concatenate(outs, axis=-1), wg_ref) * scale_ref[...]
    o_ref[...] = x_ref[...] + gate_ref[...] * y
    hist = hp_ref[top + tm - C_STATE:top + tm, :]
    hp_ref[top - C_STATE:top, :] = hist

    @pl.when(ti == tiles_per_seq - 1)
    def _():
        pnew_ref[...] = hist


def _pool_call(x, g, mod, prev, w_g, scale, nb, tm, start, name):
    t, d = x.shape
    tiles = t // nb // tm
    ng = len(C_WINDOWS)
    return pl.pallas_call(
        functools.partial(_pool_body, tiles_per_seq=tiles, start=start),
        grid=(t // tm,),
        in_specs=[pl.BlockSpec((tm, d), lambda i: (i, 0)),
                  pl.BlockSpec((1, d), lambda i: (0, 0)),
                  mod.spec(1, tm, 1), mod.spec(0, tm, 1), mod.spec(2, tm, 1),
                  pl.BlockSpec((None, C_STATE, d), lambda i: (i // tiles, 0, 0)),
                  pl.BlockSpec((ng, C_GROUP_DIM, C_GROUP_DIM), lambda i: (0, 0, 0)),
                  pl.BlockSpec((1, d), lambda i: (0, 0))],
        out_specs=[pl.BlockSpec((tm, d), lambda i: (i, 0)),
                   pl.BlockSpec((None, C_STATE, d), lambda i: (i // tiles, 0, 0))],
        out_shape=[jax.ShapeDtypeStruct((t, d), F32), jax.ShapeDtypeStruct((nb, C_STATE, d), F32)],
        scratch_shapes=[pltpu.VMEM((16 + tm, d), F32)],
        compiler_params=_cparams(("arbitrary",)),
        name=name,
    )(x, g.reshape(1, d), mod.arr, mod.arr, mod.arr, prev, w_g, scale.reshape(1, d))


def _pool_step_body(x_ref, g_ref, sc_ref, sh_ref, gate_ref, prev_ref, wg_ref, scale_ref, o_ref, pnew_ref, *, start):
    hn = _prenorm(x_ref[...], g_ref[...], sc_ref[...], sh_ref[...])
    outs = []
    for gi, w in enumerate(C_WINDOWS):
        gsl = slice(gi * C_GROUP_DIM, (gi + 1) * C_GROUP_DIM)
        acc = hn[:, gsl]
        for k in range(1, w):
            acc = acc + prev_ref[:, C_STATE - k, gsl]
        outs.append(acc / float(min(start + 1, w)) - hn[:, gsl])
    y = _pool_matmul(jnp.concatenate(outs, axis=-1), wg_ref) * scale_ref[...]
    o_ref[...] = x_ref[...] + gate_ref[...] * y
    for k in range(C_STATE - 1):
        pnew_ref[:, k, :] = prev_ref[:, k + 1, :]
    pnew_ref[:, C_STATE - 1, :] = hn


def _pool_step_call(x, g, mod, prev, w_g, scale, start, name):
    t, d = x.shape
    bt = 32
    ng = len(C_WINDOWS)
    return pl.pallas_call(
        functools.partial(_pool_step_body, start=start),
        grid=(t // bt,),
        in_specs=[pl.BlockSpec((bt, d), lambda i: (i, 0)),
                  pl.BlockSpec((1, d), lambda i: (0, 0)),
                  mod.spec(1, bt, 1), mod.spec(0, bt, 1), mod.spec(2, bt, 1),
                  pl.BlockSpec((bt, C_STATE, d), lambda i: (i, 0, 0)),
                  pl.BlockSpec((ng, C_GROUP_DIM, C_GROUP_DIM), lambda i: (0, 0, 0)),
                  pl.BlockSpec((1, d), lambda i: (0, 0))],
        out_specs=[pl.BlockSpec((bt, d), lambda i: (i, 0)),
                   pl.BlockSpec((bt, C_STATE, d), lambda i: (i, 0, 0))],
        out_shape=[jax.ShapeDtypeStruct((t, d), F32), jax.ShapeDtypeStruct((t, C_STATE, d), F32)],
        compiler_params=_cparams(("arbitrary",)),
        name=name,
    )(x, g.reshape(1, d), mod.arr, mod.arr, mod.arr, prev, w_g, scale.reshape(1, d))


def _route_rows(s, b):
    npg = EXPERTS_PER_GROUP
    gscore = []
    for q in range(N_EXPERT_GROUPS):
        v = b[q * npg:(q + 1) * npg]
        best = None
        for i in range(npg):
            for j in range(i + 1, npg):
                best = v[i] + v[j] if best is None else jnp.maximum(best, v[i] + v[j])
        gscore.append(best)
    gsel = jnp.zeros_like(gscore[0], dtype=I32)
    gbest = gscore[0]
    for q in range(1, N_EXPERT_GROUPS):
        better = gscore[q] > gbest
        gsel = jnp.where(better, q, gsel)
        gbest = jnp.where(better, gscore[q], gbest)
    vb, vs = [], []
    for k in range(npg):
        bk, sk = b[k], s[k]
        for q in range(1, N_EXPERT_GROUPS):
            bk = jnp.where(gsel == q, b[q * npg + k], bk)
            sk = jnp.where(gsel == q, s[q * npg + k], sk)
        vb.append(bk)
        vs.append(sk)
    i1 = jnp.zeros_like(gsel)
    m1 = vb[0]
    for k in range(1, npg):
        better = vb[k] > m1
        i1 = jnp.where(better, k, i1)
        m1 = jnp.where(better, vb[k], m1)
    i2 = jnp.full_like(gsel, -1)
    m2 = jnp.zeros_like(m1)
    for k in range(npg):
        better = (i1 != k) & ((i2 < 0) | (vb[k] > m2))
        i2 = jnp.where(better, k, i2)
        m2 = jnp.where(better, vb[k], m2)
    s1 = vs[0]
    s2 = vs[0]
    for k in range(1, npg):
        s1 = jnp.where(i1 == k, vs[k], s1)
        s2 = jnp.where(i2 == k, vs[k], s2)
    w1 = s1 / (s1 + s2)
    w2 = s2 / (s1 + s2)
    lo = jnp.minimum(i1, i2)
    hi = jnp.maximum(i1, i2)
    pair = jnp.zeros_like(gsel)
    for p, (a, c) in enumerate(PAIRS):
        pair = jnp.where((lo == a) & (hi == c), p, pair)
    first_is_lo = i1 < i2
    return (gsel * len(PAIRS) + pair, jnp.where(first_is_lo, w1, w2), jnp.where(first_is_lo, w2, w1))


def _route_body(x_ref, g_ref, sc_ref, sh_ref, rw_ref, rb_ref, cnt_in_ref, *rest, aliased):
    if aliased:
        rest = rest[4:]
    rows_ref, gates_ref, bucket_ref, rank_ref, cnt_ref = rest
    tm = x_ref.shape[0]

    @pl.when(pl.program_id(0) == 0)
    def _():
        cnt_ref[...] = cnt_in_ref[...]

    hn = _prenorm(x_ref[...], g_ref[...], sc_ref[...], sh_ref[...])
    logits_t = _dot3(hn, rw_ref[...]).T
    scores = _sigmoid(logits_t[0:N_EXPERTS, :])
    biased = scores + rb_ref[0:N_EXPERTS, :]
    bucket, w_lo, w_hi = _route_rows([scores[e:e + 1, :] for e in range(N_EXPERTS)],
                                     [biased[e:e + 1, :] for e in range(N_EXPERTS)])
    bucket_ref[...] = bucket
    onehot = (lax.broadcasted_iota(I32, (BUCKET_ROWS, tm), 0) == bucket).astype(F32)
    before = (lax.broadcasted_iota(I32, (tm, tm), 0) < lax.broadcasted_iota(I32, (tm, tm), 1)).astype(BF16)
    earlier = _dot(onehot.astype(BF16), before) + cnt_ref[:, 0:1]
    rank_ref[...] = jnp.sum(onehot * earlier, axis=0, keepdims=True).astype(I32)
    cnt_ref[...] = cnt_ref[...] + jnp.sum(onehot, axis=1, keepdims=True)

    sub = lax.broadcasted_iota(I32, (LANE, tm), 0)
    gate_t = jnp.where(sub == 0, w_lo, jnp.where(sub == 1, w_hi, 0.0))
    gates_ref[...] = gate_t.T
    for j in range(ROW_PLANES):
        rows_ref[pl.ds(j, tm, stride=ROW_PLANES), :] = hn[:, j * LANE:(j + 1) * LANE]


def _route_call(x, g, mod, router_w, router_b, counts, tm, total_rows, row_offset, prior, name):
    t, d = x.shape
    blk0 = row_offset // tm
    rw = jnp.pad(router_w, ((0, 0), (0, LANE - N_EXPERTS)))
    rb = jnp.pad(router_b.reshape(-1, 1), ((0, LANE - N_EXPERTS), (0, 0)))
    in_specs = [pl.BlockSpec((tm, d), lambda i: (i, 0)),
                pl.BlockSpec((1, d), lambda i: (0, 0)),
                mod.spec(4, tm, 1), mod.spec(3, tm, 1),
                pl.BlockSpec((d, LANE), lambda i: (0, 0)),
                pl.BlockSpec((LANE, 1), lambda i: (0, 0)),
                pl.BlockSpec((BUCKET_ROWS, LANE), lambda i: (0, 0))]
    args = [x, g.reshape(1, d), mod.arr, mod.arr, rw, rb, counts]
    aliases = {}
    if prior is not None:
        in_specs += [pl.BlockSpec(memory_space=pl.ANY)] * len(prior)
        aliases = {len(args) + k: k for k in range(len(prior))}
        args += list(prior)
    return pl.pallas_call(
        functools.partial(_route_body, aliased=prior is not None),
        grid=(t // tm,),
        in_specs=in_specs,
        out_specs=[pl.BlockSpec((tm * ROW_PLANES, LANE), lambda i: (blk0 + i, 0)),
                   pl.BlockSpec((tm, LANE), lambda i: (blk0 + i, 0)),
                   pl.BlockSpec((1, tm), lambda i: (0, blk0 + i)),
                   pl.BlockSpec((1, tm), lambda i: (0, blk0 + i)),
                   pl.BlockSpec((BUCKET_ROWS, LANE), lambda i: (0, 0))],
        out_shape=[jax.ShapeDtypeStruct((total_rows * ROW_PLANES, LANE), F32),
                   jax.ShapeDtypeStruct((total_rows, LANE), F32),
                   jax.ShapeDtypeStruct((1, total_rows), I32),
                   jax.ShapeDtypeStruct((1, total_rows), I32),
                   jax.ShapeDtypeStruct((BUCKET_ROWS, LANE), F32)],
        input_output_aliases=aliases,
        compiler_params=_cparams(("arbitrary",)),
        name=name,
    )(*args)


def _invert_body(dest_ref, init_ref, gather_ref):
    pltpu.sync_copy(init_ref, gather_ref)

    def put(t, carry):
        gather_ref[dest_ref[t]] = t
        return carry

    lax.fori_loop(0, dest_ref.shape[0], put, 0, unroll=8)


def _invert_call(dest, n_slots, name):
    smem = pl.BlockSpec(memory_space=pltpu.SMEM)
    return pl.pallas_call(
        _invert_body,
        in_specs=[smem, pl.BlockSpec(memory_space=pl.ANY)],
        out_specs=smem,
        out_shape=jax.ShapeDtypeStruct((n_slots,), I32),
        name=name,
    )(dest, jnp.zeros((n_slots,), I32))


def _unrolled(lo, hi, body, carry):
    for b in range(lo, hi):
        carry = body(b, carry)
    return carry


def _experts_body(grp_ref, lo_ref, hi_ref, fresh_ref, valid_ref, out_ref, gather_ref, rows_ref, wg_in_ref,
                  wu_in_ref, wd_in_ref, y_ref, wg_ref, wu_ref, wd_ref, xb_ref, gsem):
    i = pl.program_id(0)
    n = pl.num_programs(0)
    tile = EXPERT_TILE
    words = tile * ROW_PLANES
    group = 8
    del grp_ref, out_ref

    def start_gather(step, slot, loop):
        def body(b, carry):
            for k in range(group):
                r = b * group + k
                tok = gather_ref[step * tile + r]
                pltpu.make_async_copy(rows_ref.at[pl.ds(pl.multiple_of(tok * ROW_PLANES, ROW_PLANES), ROW_PLANES)],
                                      xb_ref.at[slot, pl.ds(pl.multiple_of(r * ROW_PLANES, ROW_PLANES), ROW_PLANES)],
                                      gsem.at[slot]).start(priority=k % 2)
            return carry

        loop(0, tile // group, body, 0)

    def wait_gather(slot):
        pltpu.make_async_copy(rows_ref.at[pl.ds(0, words)], xb_ref.at[slot], gsem.at[slot]).wait()

    @pl.when(fresh_ref[i] == 1)
    def _():
        for e in range(EXPERTS_PER_GROUP):
            wg_ref[e] = wg_in_ref[e].astype(BF16)
            wu_ref[e] = wu_in_ref[e].astype(BF16)
            wd_ref[e] = wd_in_ref[e].astype(BF16)

    nxt = jnp.minimum(i + 1, n - 1)
    more = jnp.logical_and(i + 1 < n, valid_ref[nxt] == 1)

    @pl.when(valid_ref[i] == 1)
    def _():
        slot = i % 2

        @pl.when(i == 0)
        def _():
            start_gather(0, 0, lax.fori_loop)

        wait_gather(slot)
        start_gather(nxt, 1 - slot, _unrolled)
        x = jnp.concatenate([xb_ref[slot, pl.ds(j, tile, stride=ROW_PLANES), :].astype(BF16)
                             for j in range(ROW_PLANES)], axis=-1)

        def expert(e):
            act = (_silu(_dot(x, wg_ref[e])) * _dot(x, wu_ref[e])).astype(BF16)
            return _dot(act, wd_ref[e])

        y_lo = expert(lo_ref[i])
        y_hi = expert(hi_ref[i])
        for j in range(ROW_PLANES):
            y_ref[pl.ds(j, tile, stride=OUT_PLANES), :] = y_lo[:, j * LANE:(j + 1) * LANE]
            y_ref[pl.ds(ROW_PLANES + j, tile, stride=OUT_PLANES), :] = y_hi[:, j * LANE:(j + 1) * LANE]

        @pl.when(jnp.logical_not(more))
        def _():
            wait_gather(1 - slot)


def _experts_call(grp, lo, hi, fresh, valid, out_tile, gather_idx, rows, w_gate, w_up, w_down, layer, name):
    n_tiles = grp.shape[0]
    tile = EXPERT_TILE
    d, f = D_MODEL, D_EXPERT
    npg = EXPERTS_PER_GROUP
    group_block = lambda i, grp, *_: (layer, grp[i], 0, 0)
    once = pl.Buffered(1)
    return pl.pallas_call(
        _experts_body,
        grid_spec=pltpu.PrefetchScalarGridSpec(
            num_scalar_prefetch=7,
            grid=(n_tiles,),
            in_specs=[pl.BlockSpec(memory_space=pl.ANY),
                      pl.BlockSpec((None, npg, d, f), group_block, pipeline_mode=once),
                      pl.BlockSpec((None, npg, d, f), group_block, pipeline_mode=once),
                      pl.BlockSpec((None, npg, f, d), group_block, pipeline_mode=once)],
            out_specs=pl.BlockSpec((tile * OUT_PLANES, LANE), lambda i, g, l, h, fr, va, out, *_: (out[i], 0)),
            scratch_shapes=[pltpu.VMEM((npg, d, f), BF16), pltpu.VMEM((npg, d, f), BF16),
                            pltpu.VMEM((npg, f, d), BF16),
                            pltpu.VMEM((2, tile * ROW_PLANES, LANE), F32),
                            pltpu.SemaphoreType.DMA((2,))]),
        out_shape=jax.ShapeDtypeStruct((n_tiles * tile * OUT_PLANES, LANE), F32),
        compiler_params=_cparams(("arbitrary",)),
        name=name,
    )(grp, lo, hi, fresh, valid, out_tile, gather_idx, rows, w_gate, w_up, w_down)


def _moe_res_body(dest_ref, y_ref, w_ref, x_ref, gate_ref, fg_ref, o_ref, *rest, row_offset, final):
    if final:
        on_ref, buf_ref, sem = rest
    else:
        buf_ref, sem = rest
    i = pl.program_id(0)
    tm = x_ref.shape[0]
    group = 8

    def start_gather(step, slot):
        base = row_offset + step * tm

        def body(b, carry):
            for k in range(group):
                r = b * group + k
                src = pl.multiple_of(dest_ref[base + r] * OUT_PLANES, OUT_PLANES)
                pltpu.make_async_copy(y_ref.at[pl.ds(src, OUT_PLANES)],
                                      buf_ref.at[slot, pl.ds(pl.multiple_of(r * OUT_PLANES, OUT_PLANES), OUT_PLANES)],
                                      sem.at[slot]).start(priority=k % 2)
            return carry

        lax.fori_loop(0, tm // group, body, 0)

    @pl.when(i == 0)
    def _():
        start_gather(0, 0)

    @pl.when(i + 1 < pl.num_programs(0))
    def _():
        start_gather(i + 1, (i + 1) % 2)

    slot = i % 2
    pltpu.make_async_copy(y_ref.at[pl.ds(0, tm * OUT_PLANES)], buf_ref.at[slot], sem.at[slot]).wait()
    y_lo = jnp.concatenate([buf_ref[slot, pl.ds(j, tm, stride=OUT_PLANES), :] for j in range(ROW_PLANES)], axis=-1)
    y_hi = jnp.concatenate([buf_ref[slot, pl.ds(ROW_PLANES + j, tm, stride=OUT_PLANES), :]
                            for j in range(ROW_PLANES)], axis=-1)
    xn = x_ref[...] + gate_ref[...] * (w_ref[:, 0:1] * y_lo + w_ref[:, 1:2] * y_hi)
    o_ref[...] = xn
    if final:
        ms = jnp.mean(xn * xn, axis=-1, keepdims=True)
        on_ref[...] = xn * lax.rsqrt(ms + EPS) * fg_ref[...]


def _moe_res_call(dest, y_sorted, weights, x, mod, final_g, tm, row_offset, final, name):
    t, d = x.shape
    blk0 = row_offset // tm
    n_out = 2 if final else 1
    gate_spec = mod.spec(5, tm, 1)
    gate_map = gate_spec.index_map
    return pl.pallas_call(
        functools.partial(_moe_res_body, row_offset=row_offset, final=final),
        grid_spec=pltpu.PrefetchScalarGridSpec(
            num_scalar_prefetch=1,
            grid=(t // tm,),
            in_specs=[pl.BlockSpec(memory_space=pl.ANY),
                      pl.BlockSpec((tm, LANE), lambda i, dest: (blk0 + i, 0)),
                      pl.BlockSpec((tm, d), lambda i, dest: (i, 0)),
                      pl.BlockSpec(gate_spec.block_shape, lambda i, dest: gate_map(i)),
                      pl.BlockSpec((1, d), lambda i, dest: (0, 0))],
            out_specs=[pl.BlockSpec((tm, d), lambda i, dest: (i, 0))] * n_out,
            scratch_shapes=[pltpu.VMEM((2, tm * OUT_PLANES, LANE), F32), pltpu.SemaphoreType.DMA((2,))]),
        out_shape=[jax.ShapeDtypeStruct((t, d), F32)] * n_out,
        compiler_params=_cparams(("arbitrary",)),
        name=name,
    )(dest, y_sorted, weights, x, mod.arr, final_g.reshape(1, d))


def _tile_tables(counts, n_tiles):
    tile = EXPERT_TILE
    tiles_per_bucket = (counts + tile - 1) // tile
    ends = jnp.cumsum(tiles_per_bucket)
    starts = ends - tiles_per_bucket
    used = ends[-1]
    ti = jnp.arange(n_tiles, dtype=I32)
    valid = (ti < used).astype(I32)
    tile_bucket = jnp.sum((jnp.minimum(ti, used - 1)[:, None] >= ends[None, :]).astype(I32), axis=1)
    pair_lo = jnp.asarray([p[0] for p in PAIRS], I32)
    pair_hi = jnp.asarray([p[1] for p in PAIRS], I32)
    grp = tile_bucket // len(PAIRS)
    lo = pair_lo[tile_bucket % len(PAIRS)]
    hi = pair_hi[tile_bucket % len(PAIRS)]
    fresh = jnp.concatenate([jnp.ones((1,), I32), (grp[1:] != grp[:-1]).astype(I32)])
    return grp, lo, hi, fresh, valid, jnp.minimum(ti, used - 1).astype(I32), (starts * tile).astype(I32)


def _moe_layer(xp, xs, norm_g2, mod_p, mod_s, router_w, router_b, w_gate, w_up, w_down, layer, final_g, final):
    tp, ts = xp.shape[0], xs.shape[0]
    total = tp + ts
    n_tiles = -(-total // EXPERT_TILE) + N_BUCKETS
    zero_counts = jnp.zeros((BUCKET_ROWS, LANE), F32)
    rows, weights, bucket, rank, counts = _route_call(xp, norm_g2, mod_p, router_w, router_b, zero_counts, 512, total,
                                                      0, None, "route_prompt")
    rows, weights, bucket, rank, counts = _route_call(xs, norm_g2, mod_s, router_w, router_b, counts, ts, total, tp,
                                                      (rows, weights, bucket, rank), "route_sample")
    grp, lo, hi, fresh, valid, out_tile, bucket_start = _tile_tables(counts[:N_BUCKETS, 0].astype(I32), n_tiles)
    dest = bucket_start[bucket[0]] + rank[0]
    gather_idx = _invert_call(dest, n_tiles * EXPERT_TILE, "invert")
    y_sorted = _experts_call(grp, lo, hi, fresh, valid, out_tile, gather_idx, rows, w_gate, w_up, w_down, layer,
                             "experts")
    outp = _moe_res_call(dest, y_sorted, weights, xp, mod_p, final_g, 512, 0, final, "moe_res_prompt")
    outs = _moe_res_call(dest, y_sorted, weights, xs, mod_s, final_g, ts, tp, final, "moe_res_sample")
    return outp, outs


def _mamba_layer(x, g, mod, conv_prev, ssm, w_zx, w_dt, conv_w, conv_b, dt_bias, a_log, d_skip, norm_g, w_out,
                 tm, tag):
    pad_h = lambda v: jnp.pad(v.reshape(1, -1), ((0, 0), (0, LANE - A_N_HEADS)))
    tm_in = min(2 * tm, x.shape[0])
    proj = _norm_mm_call(x, g, mod, 1, 0, w_zx, tm_in, 512, BF16, "a_in_" + tag)
    dt_raw = _norm_mm3_call(x, g, mod, 1, 0, w_dt, tm, "a_dt_" + tag)
    d_x = jnp.repeat(d_skip, A_HEAD_DIM).reshape(1, A_D_INNER)
    weights = (conv_w, conv_b.reshape(1, -1), pad_h(dt_bias), pad_h(a_log), d_x, norm_g.reshape(1, -1))
    if ssm[0] == "step":
        yn, conv_new, ssm_new = _ssd_step_call(proj, dt_raw, conv_prev, ssm[1], ssm[2], ssm[3], *weights,
                                               "ssd_step_" + tag)
    else:
        yn, conv_new, ssm_new = _ssd_call(proj, dt_raw, conv_prev, ssm[2], *weights, ssm[1], "ssd_" + tag)
    x = _out_res_call(yn, w_out, x, mod, 2, tm, "a_out_" + tag)
    return x, conv_new, ssm_new


def kernel(x_prompt, x_sample, c_prompt, c_sample, state_a_conv, state_a_ssm, state_c_pool, w_mod, b_mod, norm_g, final_g, a_w_in, a_conv_w, a_conv_b, a_dt_bias, a_log, a_d, a_norm_g, a_w_out, b_w_in, b_b_in, b_ln_g, b_ln_b, b_w_s, b_b_s, b_w_out, c_w_g, c_scale, router_w, router_b, e_w_gate, e_w_up, e_w_down):
    bp, seq, d = x_prompt.shape
    bs = x_sample.shape[0]
    n_a, n_c = state_a_conv.shape[0], state_c_pool.shape[0]
    mod_all = _mod_call(jnp.concatenate([c_prompt, c_sample], axis=0), w_mod, b_mod)
    mod_p_arr = mod_all[:, :bp].reshape(DEPTH, bp, 6, 1, d)
    mod_s_arr = mod_all[:, bp:]
    xp = x_prompt.reshape(bp * seq, d)
    xs = x_sample.reshape(bs, d)
    conv_p, ssm_p, pool_p, conv_s, pool_s, v_s = [], [], [], [], [], []
    hp = A_N_HEADS * A_HEAD_DIM
    ssm_s_in = state_a_ssm.reshape(n_a, bs, hp, A_D_STATE)
    ssm_s_out = None
    yp = ys = None
    for i in range(DEPTH):
        kind, s = LAYER_KIND[i], LAYER_SLOT[i]
        mod_p = Mod(mod_p_arr, i, False, seq)
        mod_s = Mod(mod_s_arr, i, True)
        g1 = norm_g[i, 0]
        if kind == 0:
            w_zx = a_w_in[s, :, :A_ZX].astype(BF16)
            w_dt = jnp.pad(a_w_in[s, :, A_ZX:], ((0, 0), (0, LANE - A_N_HEADS)))
            weights = (w_zx, w_dt, a_conv_w[s], a_conv_b[s], a_dt_bias[s], a_log[s], a_d[s], a_norm_g[s], a_w_out[s])
            conv0 = jnp.zeros((bp, A_CONV - 1, A_CONV_CH), F32)
            ssm0 = jnp.zeros((bp, hp, A_D_STATE), F32)
            xp, cv, ss = _mamba_layer(xp, g1, mod_p, conv0, ("prompt", bp, ssm0), *weights, 1024, "p%d" % i)
            conv_p.append(cv)
            ssm_p.append(ss.reshape(bp, A_N_HEADS, A_HEAD_DIM, A_D_STATE))
            xs, cv, ssm_s_out = _mamba_layer(xs, g1, mod_s, state_a_conv[s], ("step", ssm_s_in, s, ssm_s_out),
                                             *weights, bs, "s%d" % i)
            conv_s.append(cv)
        elif kind == 1:
            w_uv = b_w_in[s].astype(BF16)
            uv = _gmlp_in_call(xp, g1, mod_p, w_uv, b_b_in[s], b_ln_g[s], b_ln_b[s], 512, BF16, "b_in_p%d" % i)
            xp = _gmlp_out_call(uv, b_w_s[s], b_b_s[s], b_w_out[s], xp, mod_p, 512, "b_out_p%d" % i)
            uv = _gmlp_in_call(xs, g1, mod_s, w_uv, b_b_in[s], b_ln_g[s], b_ln_b[s], bs, F32, "b_in_s%d" % i)
            xs = _gmlp_out_step_call(uv, b_w_s[s], b_b_s[s], b_w_out[s], xs, mod_s, "b_out_s%d" % i)
            v_s.append(uv[:, B_D:].reshape(bs, 1, B_D))
        else:
            pool0 = jnp.zeros((bp, C_STATE, d), F32)
            xp, pr = _pool_call(xp, g1, mod_p, pool0, c_w_g[s], c_scale[s], bp, 512, 0, "pool_p%d" % i)
            pool_p.append(pr)
            xs, pr = _pool_step_call(xs, g1, mod_s, state_c_pool[s], c_w_g[s], c_scale[s], PAST_LEN, "pool_s%d" % i)
            pool_s.append(pr)
        final = i == DEPTH - 1
        outp, outs = _moe_layer(xp, xs, norm_g[i, 1], mod_p, mod_s, router_w, router_b, e_w_gate, e_w_up, e_w_down, i,
                                final_g, final)
        xp, xs = outp[0], outs[0]
        if final:
            yp, ys = outp[1], outs[1]
    return (yp.reshape(bp, seq, d), ys.reshape(bs, 1, d), jnp.stack(conv_p), jnp.stack(ssm_p), jnp.stack(pool_p),
            jnp.stack(conv_s), ssm_s_out.reshape(state_a_ssm.shape), jnp.stack(pool_s), jnp.stack(v_s))
```

```python
import functools
import math

import numpy as np
import jax
import jax.numpy as jnp
from jax import lax
from jax.experimental import pallas as pl
from jax.experimental.pallas import tpu as pltpu

F32 = jnp.float32
BF16 = jnp.bfloat16
I32 = jnp.int32
EPS = 1e-6

LANE = 128
D_MODEL = 1024
DEPTH = 4
PAST_LEN = 16384
LAYER_KIND = (0, 1, 2, 0)
LAYER_SLOT = (0, 0, 0, 1)
A_D_INNER = 2 * D_MODEL
A_HEAD_DIM = 64
A_N_HEADS = A_D_INNER // A_HEAD_DIM
A_N_GROUPS = 8
A_HPG = A_N_HEADS // A_N_GROUPS
A_D_STATE = 128
A_GN = A_N_GROUPS * A_D_STATE
A_CONV = 4
A_CONV_CH = A_D_INNER + 2 * A_GN
A_ZX = A_D_INNER + A_CONV_CH
A_CHUNK = 128
A_GW = A_HPG * A_HEAD_DIM
B_D = 2 * D_MODEL
B_N_GROUPS = 8
B_GROUP_DIM = B_D // B_N_GROUPS
B_CHUNK = 128
C_WINDOWS = (2, 4, 8, 16)
C_GROUP_DIM = D_MODEL // len(C_WINDOWS)
C_STATE = max(C_WINDOWS) - 1
N_EXPERTS = 16
N_EXPERT_GROUPS = 4
EXPERTS_PER_GROUP = 4
D_EXPERT = D_MODEL // 2
PAIRS = ((0, 1), (0, 2), (0, 3), (1, 2), (1, 3), (2, 3))
N_BUCKETS = N_EXPERT_GROUPS * len(PAIRS)
BUCKET_ROWS = 32
ROW_PLANES = D_MODEL // LANE
OUT_PLANES = 2 * ROW_PLANES
EXPERT_TILE = 256
GATHER_SLOTS = 3
VMEM_LIMIT = 56 * 1024 * 1024


def _cparams(sem, vmem=VMEM_LIMIT):
    return pltpu.CompilerParams(dimension_semantics=sem, vmem_limit_bytes=vmem)


def _sigmoid(x):
    return 1.0 / (1.0 + jnp.exp(-x))


def _silu(x):
    return x * _sigmoid(x)


def _gelu_tanh(x):
    c = 2.0 * math.sqrt(2.0 / math.pi)
    return x / (1.0 + jnp.exp(x * (-c - (c * 0.044715) * (x * x))))


def _softplus(x):
    return jnp.maximum(x, 0.0) + jnp.log1p(jnp.exp(-jnp.abs(x)))


def _split2(a):
    hi = a.astype(BF16)
    lo = (a - hi.astype(F32)).astype(BF16)
    return hi, lo


def _split3(a):
    hi = a.astype(BF16)
    r = a - hi.astype(F32)
    mid = r.astype(BF16)
    lo = (r - mid.astype(F32)).astype(BF16)
    return hi, mid, lo


def _dot(a, b):
    return jnp.dot(a, b, preferred_element_type=F32)


def _dot_nt(a, b):
    return lax.dot_general(a, b, (((1,), (1,)), ((), ())), preferred_element_type=F32)


def _dot_tn(a, b):
    return lax.dot_general(a, b, (((0,), (0,)), ((), ())), preferred_element_type=F32)


def _dot3(a, b):
    a_hi, a_lo = _split2(a)
    b_hi, b_lo = _split2(b)
    return _dot(a_hi, b_hi) + (_dot(a_lo, b_hi) + _dot(a_hi, b_lo))


def _prenorm(x, g, sc, sh):
    ms = jnp.mean(x * x, axis=-1, keepdims=True)
    return (x * lax.rsqrt(ms + EPS) * g) * (1.0 + sc) + sh


def _mod_body(c_ref, w_ref, b_ref, o_ref):
    o_ref[...] = _dot3(_silu(c_ref[...]), w_ref[...]) + b_ref[...]


def _mod_call(c_all, w_mod, b_mod):
    nb, d = c_all.shape
    depth, _, n = w_mod.shape
    tn = 1536
    return pl.pallas_call(
        _mod_body,
        grid=(depth, n // tn),
        in_specs=[pl.BlockSpec((nb, d), lambda i, j: (0, 0)),
                  pl.BlockSpec((None, d, tn), lambda i, j: (i, 0, j)),
                  pl.BlockSpec((None, 1, tn), lambda i, j: (i, 0, j))],
        out_specs=pl.BlockSpec((None, nb, tn), lambda i, j: (i, 0, j)),
        out_shape=jax.ShapeDtypeStruct((depth, nb, n), F32),
        compiler_params=_cparams(("arbitrary", "arbitrary")),
        name="mod",
    )(c_all, w_mod, b_mod.reshape(depth, 1, n))


class Mod:
    def __init__(self, arr, layer, per_row, rows_per_seq=None):
        self.arr, self.layer, self.per_row, self.rows_per_seq = arr, layer, per_row, rows_per_seq

    def spec(self, which, tm, ngrid):
        layer = self.layer
        if self.per_row:
            if ngrid == 1:
                return pl.BlockSpec((None, tm, D_MODEL), lambda i: (layer, i, which))
            return pl.BlockSpec((None, tm, D_MODEL), lambda i, j: (layer, i, which))
        tiles = self.rows_per_seq // tm
        if ngrid == 1:
            return pl.BlockSpec((None, None, None, 1, D_MODEL), lambda i: (layer, i // tiles, which, 0, 0))
        return pl.BlockSpec((None, None, None, 1, D_MODEL), lambda i, j: (layer, i // tiles, which, 0, 0))


def _norm_mm_body(x_ref, g_ref, sc_ref, sh_ref, w_ref, o_ref, hn_ref):
    @pl.when(pl.program_id(1) == 0)
    def _():
        hn_ref[...] = _prenorm(x_ref[...], g_ref[...], sc_ref[...], sh_ref[...]).astype(BF16)

    o_ref[...] = _dot(hn_ref[...], w_ref[...]).astype(o_ref.dtype)


def _norm_mm_call(x, g, mod, which_sc, which_sh, w, tm, tn, out_dtype, name):
    t, d = x.shape
    n_cols = w.shape[1]
    return pl.pallas_call(
        _norm_mm_body,
        grid=(t // tm, n_cols // tn),
        in_specs=[pl.BlockSpec((tm, d), lambda i, j: (i, 0)),
                  pl.BlockSpec((1, d), lambda i, j: (0, 0)),
                  mod.spec(which_sc, tm, 2), mod.spec(which_sh, tm, 2),
                  pl.BlockSpec((d, tn), lambda i, j: (0, j))],
        out_specs=pl.BlockSpec((tm, tn), lambda i, j: (i, j)),
        out_shape=jax.ShapeDtypeStruct((t, n_cols), out_dtype),
        scratch_shapes=[pltpu.VMEM((tm, d), BF16)],
        compiler_params=_cparams(("arbitrary", "arbitrary")),
        name=name,
    )(x, g.reshape(1, d), mod.arr, mod.arr, w)


def _norm_mm3_body(x_ref, g_ref, sc_ref, sh_ref, w_ref, o_ref):
    hn = _prenorm(x_ref[...], g_ref[...], sc_ref[...], sh_ref[...])
    o_ref[...] = _dot3(hn, w_ref[...])


def _norm_mm3_call(x, g, mod, which_sc, which_sh, w, tm, name):
    t, d = x.shape
    n = w.shape[1]
    return pl.pallas_call(
        _norm_mm3_body,
        grid=(t // tm,),
        in_specs=[pl.BlockSpec((tm, d), lambda i: (i, 0)),
                  pl.BlockSpec((1, d), lambda i: (0, 0)),
                  mod.spec(which_sc, tm, 1), mod.spec(which_sh, tm, 1),
                  pl.BlockSpec((d, n), lambda i: (0, 0))],
        out_specs=pl.BlockSpec((tm, n), lambda i: (i, 0)),
        out_shape=jax.ShapeDtypeStruct((t, n), F32),
        compiler_params=_cparams(("arbitrary",)),
        name=name,
    )(x, g.reshape(1, d), mod.arr, mod.arr, w)


def _out_res_body(y_ref, w_ref, x_ref, gate_ref, o_ref, wbf_ref):
    @pl.when(pl.program_id(0) == 0)
    def _():
        wbf_ref[...] = w_ref[...].astype(BF16)

    o_ref[...] = x_ref[...] + gate_ref[...] * _dot(y_ref[...], wbf_ref[...])


def _out_res_call(y, w, x, mod, which_gate, tm, name):
    t, k = y.shape
    d = x.shape[1]
    return pl.pallas_call(
        _out_res_body,
        grid=(t // tm,),
        in_specs=[pl.BlockSpec((tm, k), lambda i: (i, 0)),
                  pl.BlockSpec((k, d), lambda i: (0, 0)),
                  pl.BlockSpec((tm, d), lambda i: (i, 0)),
                  mod.spec(which_gate, tm, 1)],
        out_specs=pl.BlockSpec((tm, d), lambda i: (i, 0)),
        out_shape=jax.ShapeDtypeStruct((t, d), F32),
        scratch_shapes=[pltpu.VMEM((k, d), BF16)],
        compiler_params=_cparams(("arbitrary",)),
        name=name,
    )(y, w, x, mod.arr)


def _head_expand():
    h = np.arange(LANE)[:, None]
    c = np.arange(A_D_INNER)[None, :]
    return jnp.asarray((c // A_HEAD_DIM == h).astype(np.float32), dtype=BF16)


def _gate_norm(y, z, ng):
    gated = y * _silu(z)
    ms = jnp.mean(gated * gated, axis=-1, keepdims=True)
    return gated * lax.rsqrt(ms + EPS) * ng


def _ssd_body(z_ref, xs_ref, bc_ref, dt_ref, cprev_ref, sprev_ref, cw_ref, cb_ref, dtb_ref, alog_ref,
              dx_ref, ng_ref, exp_ref, yn_ref, cnew_ref, snew_ref, tail_ref, act_ref):
    c = pl.program_id(1)
    q = A_CHUNK
    nt = A_CONV - 1

    @pl.when(c == 0)
    def _():
        tail_ref[0:nt, :] = cprev_ref[...]
        snew_ref[...] = sprev_ref[...]

    row = lax.broadcasted_iota(I32, (q, q), 0)
    col = lax.broadcasted_iota(I32, (q, q), 1)
    shifts = [jnp.where(row - col == nt - k, 1.0, 0.0).astype(BF16) for k in range(nt)]
    sub = lax.broadcasted_iota(I32, (8, 1), 0)
    cw = 512
    for j in range(A_CONV_CH // cw):
        sl = slice(j * cw, (j + 1) * cw)
        src = xs_ref if (j + 1) * cw <= A_D_INNER else bc_ref
        off = j * cw if src is xs_ref else j * cw - A_D_INNER
        x_bf = src[:, off:off + cw]
        conv = cb_ref[:, sl] + x_bf.astype(F32) * cw_ref[nt:nt + 1, sl]
        for k in range(nt):
            conv = conv + _dot(shifts[k], x_bf) * cw_ref[k:k + 1, sl]
        act_ref[:, sl] = _silu(conv)
        corr = jnp.zeros((8, cw), F32)
        for l in range(nt):
            c_l = sum(tail_ref[l + k:l + k + 1, sl] * cw_ref[k:k + 1, sl] for k in range(nt - l))
            corr = jnp.where(sub == l, c_l, corr)
        act_ref[0:8, sl] = _silu(conv[0:8, :] + corr)
        tail_ref[0:nt, sl] = x_bf[q - 8:q, :].astype(F32)[8 - nt:8, :]

    @pl.when(c == pl.num_programs(1) - 1)
    def _():
        cnew_ref[...] = tail_ref[0:nt, :]

    dt = _softplus(dt_ref[...] + dtb_ref[...])
    a = dt * (-jnp.exp(alog_ref[...]))
    row = lax.broadcasted_iota(I32, (q, q), 0)
    col = lax.broadcasted_iota(I32, (q, q), 1)
    causal = row >= col
    tril = jnp.where(causal, 1.0, 0.0).astype(BF16)
    a_hi, a_mid, a_lo = _split3(a)
    cs = _dot(tril, a_hi) + (_dot(tril, a_mid) + _dot(tril, a_lo))
    cs_t = cs.T
    dt_t = dt.T
    cs_last = cs[q - 1:q, :]
    ecs = jnp.exp(cs)
    wend = jnp.exp(cs_last - cs) * dt
    st_hi, st_lo = _split2(jnp.concatenate([ecs, wend], axis=0))
    st_x = _dot(st_hi, exp_ref[...]) + _dot(st_lo, exp_ref[...])
    lane_head = lax.broadcasted_iota(I32, (q, A_GW), 1) // A_HEAD_DIM

    for g in range(A_N_GROUPS):
        gsl = slice(g * A_GW, (g + 1) * A_GW)
        b_g = act_ref[:, A_D_INNER + g * A_D_STATE:A_D_INNER + (g + 1) * A_D_STATE].astype(BF16)
        c_g = act_ref[:, A_D_INNER + A_GN + g * A_D_STATE:A_D_INNER + A_GN + (g + 1) * A_D_STATE].astype(BF16)
        x_g = act_ref[:, gsl]
        x_bf = x_g.astype(BF16)
        h_g = snew_ref[gsl, :]
        cb = _dot_nt(c_g, b_g)
        y = jnp.zeros((q, A_GW), F32)
        for r in range(A_HPG):
            h = g * A_HPG + r
            seg = cs[:, h:h + 1] - cs_t[h:h + 1, :]
            decay = jnp.exp(jnp.where(causal, seg, -1e30))
            wm = (cb * decay * dt_t[h:h + 1, :]).astype(BF16)
            y = y + _dot(wm, jnp.where(lane_head == r, x_bf, jnp.zeros_like(x_bf)))
        y = y + st_x[0:q, gsl] * _dot_nt(c_g, h_g.astype(BF16)) + dx_ref[:, gsl] * x_g
        yn_ref[:, gsl] = _gate_norm(y, z_ref[:, gsl].astype(F32), ng_ref[:, gsl]).astype(BF16)
        s_new = _dot_tn((x_g * st_x[q:2 * q, gsl]).astype(BF16), b_g)
        for r in range(A_HPG):
            h = g * A_HPG + r
            rsl = slice(g * A_GW + r * A_HEAD_DIM, g * A_GW + (r + 1) * A_HEAD_DIM)
            keep = jnp.exp(cs[q - 1:q, h:h + 1])
            snew_ref[rsl, :] = snew_ref[rsl, :] * keep + s_new[r * A_HEAD_DIM:(r + 1) * A_HEAD_DIM, :]


def _ssd_call(proj, dt_raw, conv_prev, ssm_prev, conv_w, conv_b, dt_bias, a_log, d_x, norm_g, nb, name):
    t = proj.shape[0]
    q = A_CHUNK
    nc = t // nb // q
    hp = A_N_HEADS * A_HEAD_DIM
    row = lambda b, c: (b * nc + c, 0)
    full = lambda shape: pl.BlockSpec(shape, lambda b, c: (0,) * len(shape))
    return pl.pallas_call(
        _ssd_body,
        grid=(nb, nc),
        in_specs=[pl.BlockSpec((q, A_D_INNER), lambda b, c: (b * nc + c, 0)),
                  pl.BlockSpec((q, A_D_INNER), lambda b, c: (b * nc + c, 1)),
                  pl.BlockSpec((q, 2 * A_GN), lambda b, c: (b * nc + c, 2)),
                  pl.BlockSpec((q, LANE), row),
                  pl.BlockSpec((None, A_CONV - 1, A_CONV_CH), lambda b, c: (b, 0, 0)),
                  pl.BlockSpec((None, hp, A_D_STATE), lambda b, c: (b, 0, 0)),
                  full((A_CONV, A_CONV_CH)), full((1, A_CONV_CH)), full((1, LANE)), full((1, LANE)),
                  full((1, A_D_INNER)), full((1, A_D_INNER)), full((LANE, A_D_INNER))],
        out_specs=[pl.BlockSpec((q, A_D_INNER), row),
                   pl.BlockSpec((None, A_CONV - 1, A_CONV_CH), lambda b, c: (b, 0, 0)),
                   pl.BlockSpec((None, hp, A_D_STATE), lambda b, c: (b, 0, 0))],
        out_shape=[jax.ShapeDtypeStruct((t, A_D_INNER), BF16),
                   jax.ShapeDtypeStruct((nb, A_CONV - 1, A_CONV_CH), F32),
                   jax.ShapeDtypeStruct((nb, hp, A_D_STATE), F32)],
        scratch_shapes=[pltpu.VMEM((8, A_CONV_CH), F32), pltpu.VMEM((q, A_CONV_CH), F32)],
        compiler_params=_cparams(("arbitrary", "arbitrary")),
        name=name,
    )(proj, proj, proj, dt_raw, conv_prev, ssm_prev, conv_w, conv_b, dt_bias, a_log, d_x, norm_g, _head_expand())


def _ssd_step_body(z_ref, xs_ref, bc_ref, dt_ref, cprev_ref, sprev_ref, cw_ref, cb_ref, dtb_ref, alog_ref,
                   dx_ref, ng_ref, exp_ref, yn_ref, cnew_ref, snew_ref, da_ref, y_ref):
    bt = z_ref.shape[0]
    cur = jnp.concatenate([xs_ref[...], bc_ref[...]], axis=1).astype(F32)
    conv = cb_ref[...] + cur * cw_ref[A_CONV - 1:A_CONV, :]
    for k in range(A_CONV - 1):
        conv = conv + cprev_ref[:, k, :] * cw_ref[k:k + 1, :]
    for k in range(A_CONV - 2):
        cnew_ref[:, k, :] = cprev_ref[:, k + 1, :]
    cnew_ref[:, A_CONV - 2, :] = cur
    act = _silu(conv)
    xs = act[:, 0:A_D_INNER]
    bm_bf = act[:, A_D_INNER:A_D_INNER + A_GN].astype(BF16)
    cm_bf = act[:, A_D_INNER + A_GN:A_CONV_CH].astype(BF16)
    dt = _softplus(dt_ref[...] + dtb_ref[...])
    da_ref[...] = jnp.exp(dt * (-jnp.exp(alog_ref[...])))
    dt_hi, dt_lo = _split2(dt)
    dt_x = _dot(dt_hi, exp_ref[...]) + _dot(dt_lo, exp_ref[...])
    xdt = xs * dt_x
    rows = lax.broadcasted_iota(I32, (bt, 1), 0)
    y_ref[...] = jnp.zeros_like(y_ref)

    def per_seq(j, carry):
        mine = rows == j
        xdt_j = jnp.where(mine, xdt, 0.0)
        da_j = da_ref[pl.ds(j, 1), :]
        for g in range(A_N_GROUPS):
            gsl = slice(g * A_GW, (g + 1) * A_GW)
            nsl = slice(g * A_D_STATE, (g + 1) * A_D_STATE)
            outer = _dot_tn(xdt_j[:, gsl].astype(BF16), bm_bf[:, nsl])
            for r in range(A_HPG):
                h = g * A_HPG + r
                rsl = slice(r * A_HEAD_DIM, (r + 1) * A_HEAD_DIM)
                hsl = slice(g * A_GW + r * A_HEAD_DIM, g * A_GW + (r + 1) * A_HEAD_DIM)
                snew_ref[j, hsl, :] = sprev_ref[j, hsl, :] * da_j[:, h:h + 1] + outer[rsl, :]
            yg = _dot_nt(cm_bf[:, nsl], snew_ref[j, gsl, :].astype(BF16))
            y_ref[:, gsl] = y_ref[:, gsl] + jnp.where(mine, yg, 0.0)
        return carry

    lax.fori_loop(0, bt, per_seq, 0)
    y = y_ref[...] + dx_ref[...] * xs
    z = z_ref[...].astype(F32)
    for g in range(A_N_GROUPS):
        gsl = slice(g * A_GW, (g + 1) * A_GW)
        yn_ref[:, gsl] = _gate_norm(y[:, gsl], z[:, gsl], ng_ref[:, gsl]).astype(BF16)


_SSD_STEP_INPUTS = 13


def _ssd_step_aliased_body(*refs):
    _ssd_step_body(*refs[:_SSD_STEP_INPUTS], *refs[_SSD_STEP_INPUTS + 1:])


def _ssd_step_call(proj, dt_raw, conv_prev, ssm_all, slot, ssm_out, conv_w, conv_b, dt_bias, a_log, d_x, norm_g,
                   name):
    nb = proj.shape[0]
    bt = 8
    hp = A_N_HEADS * A_HEAD_DIM
    full = lambda shape: pl.BlockSpec(shape, lambda i: (0,) * len(shape))
    state_spec = pl.BlockSpec((None, bt, hp, A_D_STATE), lambda i: (slot, i, 0, 0))
    in_specs = [pl.BlockSpec((bt, A_D_INNER), lambda i: (i, 0)),
                pl.BlockSpec((bt, A_D_INNER), lambda i: (i, 1)),
                pl.BlockSpec((bt, 2 * A_GN), lambda i: (i, 2)),
                pl.BlockSpec((bt, LANE), lambda i: (i, 0)),
                pl.BlockSpec((bt, A_CONV - 1, A_CONV_CH), lambda i: (i, 0, 0)),
                state_spec,
                full((A_CONV, A_CONV_CH)), full((1, A_CONV_CH)), full((1, LANE)), full((1, LANE)),
                full((1, A_D_INNER)), full((1, A_D_INNER)), full((LANE, A_D_INNER))]
    args = [proj, proj, proj, dt_raw, conv_prev, ssm_all, conv_w, conv_b, dt_bias, a_log, d_x, norm_g, _head_expand()]
    assert len(args) == _SSD_STEP_INPUTS
    aliases = {}
    if ssm_out is not None:
        in_specs.append(pl.BlockSpec(memory_space=pl.ANY))
        args.append(ssm_out)
        aliases = {_SSD_STEP_INPUTS: 2}
    return pl.pallas_call(
        _ssd_step_body if ssm_out is None else _ssd_step_aliased_body,
        grid=(nb // bt,),
        in_specs=in_specs,
        out_specs=[pl.BlockSpec((bt, A_D_INNER), lambda i: (i, 0)),
                   pl.BlockSpec((bt, A_CONV - 1, A_CONV_CH), lambda i: (i, 0, 0)),
                   state_spec],
        out_shape=[jax.ShapeDtypeStruct((nb, A_D_INNER), BF16),
                   jax.ShapeDtypeStruct((nb, A_CONV - 1, A_CONV_CH), F32),
                   jax.ShapeDtypeStruct(ssm_all.shape, F32)],
        scratch_shapes=[pltpu.VMEM((bt, LANE), F32), pltpu.VMEM((bt, A_D_INNER), F32)],
        input_output_aliases=aliases,
        compiler_params=_cparams(("arbitrary",)),
        name=name,
    )(*args)


def _gmlp_in_body(x_ref, g_ref, sc_ref, sh_ref, w_ref, b_ref, lg_ref, lb_ref, o_ref, hn_ref):
    j = pl.program_id(1)

    @pl.when(j == 0)
    def _():
        hn_ref[...] = _prenorm(x_ref[...], g_ref[...], sc_ref[...], sh_ref[...]).astype(BF16)

    uv = _gelu_tanh(_dot(hn_ref[...], w_ref[...]) + b_ref[...])

    @pl.when(j == 0)
    def _():
        o_ref[...] = uv.astype(o_ref.dtype)

    @pl.when(j == 1)
    def _():
        vc = uv - jnp.mean(uv, axis=-1, keepdims=True)
        var = jnp.mean(vc * vc, axis=-1, keepdims=True)
        o_ref[...] = (vc * lax.rsqrt(var + EPS) * lg_ref[...] + lb_ref[...]).astype(o_ref.dtype)


def _gmlp_in_call(x, g, mod, w, b, ln_g, ln_b, tm, out_dtype, name):
    t, d = x.shape
    return pl.pallas_call(
        _gmlp_in_body,
        grid=(t // tm, 2),
        in_specs=[pl.BlockSpec((tm, d), lambda i, j: (i, 0)),
                  pl.BlockSpec((1, d), lambda i, j: (0, 0)),
                  mod.spec(1, tm, 2), mod.spec(0, tm, 2),
                  pl.BlockSpec((d, B_D), lambda i, j: (0, j)),
                  pl.BlockSpec((1, B_D), lambda i, j: (0, j)),
                  pl.BlockSpec((1, B_D), lambda i, j: (0, 0)),
                  pl.BlockSpec((1, B_D), lambda i, j: (0, 0))],
        out_specs=pl.BlockSpec((tm, B_D), lambda i, j: (i, j)),
        out_shape=jax.ShapeDtypeStruct((t, 2 * B_D), out_dtype),
        scratch_shapes=[pltpu.VMEM((tm, d), BF16)],
        compiler_params=_cparams(("arbitrary", "arbitrary")),
        name=name,
    )(x, g.reshape(1, d), mod.arr, mod.arr, w, b.reshape(1, -1), ln_g.reshape(1, -1), ln_b.reshape(1, -1))


def _gmlp_out_body(u_ref, v_ref, ws_ref, bs_ref, w_ref, x_ref, gate_ref, o_ref, wbf_ref, wsbf_ref, m_ref):
    q = B_CHUNK

    @pl.when(pl.program_id(0) == 0)
    def _():
        wbf_ref[...] = w_ref[...].astype(BF16)
        causal = lax.broadcasted_iota(I32, (q, q), 0) >= lax.broadcasted_iota(I32, (q, q), 1)
        for g in range(B_N_GROUPS):
            wsbf_ref[g] = jnp.where(causal, ws_ref[g], 0.0).astype(BF16)

    for ci in range(u_ref.shape[0] // q):
        rsl = slice(ci * q, (ci + 1) * q)
        for g in range(B_N_GROUPS):
            gsl = slice(g * B_GROUP_DIM, (g + 1) * B_GROUP_DIM)
            mixed = _dot(wsbf_ref[g], v_ref[rsl, gsl].astype(BF16)) + bs_ref[:, g:g + 1]
            m_ref[rsl, gsl] = (u_ref[rsl, gsl].astype(F32) * mixed).astype(BF16)
    o_ref[...] = x_ref[...] + gate_ref[...] * _dot(m_ref[...], wbf_ref[...])


def _gmlp_out_call(uv, w_s, b_s, w_out, x, mod, tm, name):
    t, d = x.shape
    q = B_CHUNK
    return pl.pallas_call(
        _gmlp_out_body,
        grid=(t // tm,),
        in_specs=[pl.BlockSpec((tm, B_D), lambda i: (i, 0)),
                  pl.BlockSpec((tm, B_D), lambda i: (i, 1)),
                  pl.BlockSpec((B_N_GROUPS, q, q), lambda i: (0, 0, 0)),
                  pl.BlockSpec((q, B_N_GROUPS), lambda i: (0, 0)),
                  pl.BlockSpec((B_D, d), lambda i: (0, 0)),
                  pl.BlockSpec((tm, d), lambda i: (i, 0)),
                  mod.spec(2, tm, 1)],
        out_specs=pl.BlockSpec((tm, d), lambda i: (i, 0)),
        out_shape=jax.ShapeDtypeStruct((t, d), F32),
        scratch_shapes=[pltpu.VMEM((B_D, d), BF16), pltpu.VMEM((B_N_GROUPS, q, q), BF16),
                        pltpu.VMEM((tm, B_D), BF16)],
        compiler_params=_cparams(("arbitrary",)),
        name=name,
    )(uv, uv, w_s, b_s.T, w_out, x, mod.arr)


def _gmlp_out_step_body(u_ref, v_ref, wd_ref, bd_ref, w_ref, x_ref, gate_ref, o_ref):
    mixed = v_ref[...] * wd_ref[...] + bd_ref[...]
    m = (u_ref[...] * mixed).astype(BF16)
    o_ref[...] = x_ref[...] + gate_ref[...] * _dot(m, w_ref[...].astype(BF16))


def _gmlp_out_step_call(uv, w_s, b_s, w_out, x, mod, name):
    t, d = x.shape
    wd = jnp.repeat(w_s[:, 0, 0], B_GROUP_DIM).reshape(1, B_D)
    bd = jnp.repeat(b_s[:, 0], B_GROUP_DIM).reshape(1, B_D)
    return pl.pallas_call(
        _gmlp_out_step_body,
        grid=(1,),
        in_specs=[pl.BlockSpec((t, B_D), lambda i: (0, 0)),
                  pl.BlockSpec((t, B_D), lambda i: (0, 1)),
                  pl.BlockSpec((1, B_D), lambda i: (0, 0)),
                  pl.BlockSpec((1, B_D), lambda i: (0, 0)),
                  pl.BlockSpec((B_D, d), lambda i: (0, 0)),
                  pl.BlockSpec((t, d), lambda i: (0, 0)),
                  mod.spec(2, t, 1)],
        out_specs=pl.BlockSpec((t, d), lambda i: (0, 0)),
        out_shape=jax.ShapeDtypeStruct((t, d), F32),
        compiler_params=_cparams(("arbitrary",)),
        name=name,
    )(uv, uv, wd, bd, w_out, x, mod.arr)


def _pool_matmul(pooled, wg_ref):
    outs = []
    for gi in range(len(C_WINDOWS)):
        gsl = slice(gi * C_GROUP_DIM, (gi + 1) * C_GROUP_DIM)
        outs.append(_dot(pooled[:, gsl].astype(BF16), wg_ref[gi].astype(BF16)))
    return jnp.concatenate(outs, axis=-1)


def _pool_body(x_ref, g_ref, sc_ref, sh_ref, gate_ref, prev_ref, wg_ref, scale_ref, o_ref, pnew_ref, hp_ref,
               *, tiles_per_seq, start):
    i = pl.program_id(0)
    tm = x_ref.shape[0]
    top = 16
    ti = i % tiles_per_seq

    @pl.when(ti == 0)
    def _():
        hp_ref[top - C_STATE:top, :] = prev_ref[...]

    hn = _prenorm(x_ref[...], g_ref[...], sc_ref[...], sh_ref[...])
    hp_ref[top:top + tm, :] = hn
    pos = start + ti * tm + lax.broadcasted_iota(I32, (tm, 1), 0)
    outs = []
    for gi, w in enumerate(C_WINDOWS):
        gsl = slice(gi * C_GROUP_DIM, (gi + 1) * C_GROUP_DIM)
        acc = hn[:, gsl]
        for k in range(1, w):
            acc = acc + hp_ref[top - k:top - k + tm, gsl]
        cnt = jnp.minimum(pos + 1, w).astype(F32)
        outs.append(acc / cnt - hn[:, gsl])
    y = _pool_matmul(jnp.concatenate(outs, axis=-1), wg_ref) * scale_ref[...]
    o_ref[...] = x_ref[...] + gate_ref[...] * y
    hist = hp_ref[top + tm - C_STATE:top + tm, :]
    hp_ref[top - C_STATE:top, :] = hist

    @pl.when(ti == tiles_per_seq - 1)
    def _():
        pnew_ref[...] = hist


def _pool_call(x, g, mod, prev, w_g, scale, nb, tm, start, name):
    t, d = x.shape
    tiles = t // nb // tm
    ng = len(C_WINDOWS)
    return pl.pallas_call(
        functools.partial(_pool_body, tiles_per_seq=tiles, start=start),
        grid=(t // tm,),
        in_specs=[pl.BlockSpec((tm, d), lambda i: (i, 0)),
                  pl.BlockSpec((1, d), lambda i: (0, 0)),
                  mod.spec(1, tm, 1), mod.spec(0, tm, 1), mod.spec(2, tm, 1),
                  pl.BlockSpec((None, C_STATE, d), lambda i: (i // tiles, 0, 0)),
                  pl.BlockSpec((ng, C_GROUP_DIM, C_GROUP_DIM), lambda i: (0, 0, 0)),
                  pl.BlockSpec((1, d), lambda i: (0, 0))],
        out_specs=[pl.BlockSpec((tm, d), lambda i: (i, 0)),
                   pl.BlockSpec((None, C_STATE, d), lambda i: (i // tiles, 0, 0))],
        out_shape=[jax.ShapeDtypeStruct((t, d), F32), jax.ShapeDtypeStruct((nb, C_STATE, d), F32)],
        scratch_shapes=[pltpu.VMEM((16 + tm, d), F32)],
        compiler_params=_cparams(("arbitrary",)),
        name=name,
    )(x, g.reshape(1, d), mod.arr, mod.arr, mod.arr, prev, w_g, scale.reshape(1, d))


def _pool_step_body(x_ref, g_ref, sc_ref, sh_ref, gate_ref, prev_ref, wg_ref, scale_ref, o_ref, pnew_ref, *, start):
    hn = _prenorm(x_ref[...], g_ref[...], sc_ref[...], sh_ref[...])
    outs = []
    for gi, w in enumerate(C_WINDOWS):
        gsl = slice(gi * C_GROUP_DIM, (gi + 1) * C_GROUP_DIM)
        acc = hn[:, gsl]
        for k in range(1, w):
            acc = acc + prev_ref[:, C_STATE - k, gsl]
        outs.append(acc / float(min(start + 1, w)) - hn[:, gsl])
    y = _pool_matmul(jnp.concatenate(outs, axis=-1), wg_ref) * scale_ref[...]
    o_ref[...] = x_ref[...] + gate_ref[...] * y
    for k in range(C_STATE - 1):
        pnew_ref[:, k, :] = prev_ref[:, k + 1, :]
    pnew_ref[:, C_STATE - 1, :] = hn


def _pool_step_call(x, g, mod, prev, w_g, scale, start, name):
    t, d = x.shape
    bt = 32
    ng = len(C_WINDOWS)
    return pl.pallas_call(
        functools.partial(_pool_step_body, start=start),
        grid=(t // bt,),
        in_specs=[pl.BlockSpec((bt, d), lambda i: (i, 0)),
                  pl.BlockSpec((1, d), lambda i: (0, 0)),
                  mod.spec(1, bt, 1), mod.spec(0, bt, 1), mod.spec(2, bt, 1),
                  pl.BlockSpec((bt, C_STATE, d), lambda i: (i, 0, 0)),
                  pl.BlockSpec((ng, C_GROUP_DIM, C_GROUP_DIM), lambda i: (0, 0, 0)),
                  pl.BlockSpec((1, d), lambda i: (0, 0))],
        out_specs=[pl.BlockSpec((bt, d), lambda i: (i, 0)),
                   pl.BlockSpec((bt, C_STATE, d), lambda i: (i, 0, 0))],
        out_shape=[jax.ShapeDtypeStruct((t, d), F32), jax.ShapeDtypeStruct((t, C_STATE, d), F32)],
        compiler_params=_cparams(("arbitrary",)),
        name=name,
    )(x, g.reshape(1, d), mod.arr, mod.arr, mod.arr, prev, w_g, scale.reshape(1, d))


def _route_rows(s, b):
    npg = EXPERTS_PER_GROUP
    gscore = []
    for q in range(N_EXPERT_GROUPS):
        v = b[q * npg:(q + 1) * npg]
        best = None
        for i in range(npg):
            for j in range(i + 1, npg):
                best = v[i] + v[j] if best is None else jnp.maximum(best, v[i] + v[j])
        gscore.append(best)
    gsel = jnp.zeros_like(gscore[0], dtype=I32)
    gbest = gscore[0]
    for q in range(1, N_EXPERT_GROUPS):
        better = gscore[q] > gbest
        gsel = jnp.where(better, q, gsel)
        gbest = jnp.where(better, gscore[q], gbest)
    vb, vs = [], []
    for k in range(npg):
        bk, sk = b[k], s[k]
        for q in range(1, N_EXPERT_GROUPS):
            bk = jnp.where(gsel == q, b[q * npg + k], bk)
            sk = jnp.where(gsel == q, s[q * npg + k], sk)
        vb.append(bk)
        vs.append(sk)
    i1 = jnp.zeros_like(gsel)
    m1 = vb[0]
    for k in range(1, npg):
        better = vb[k] > m1
        i1 = jnp.where(better, k, i1)
        m1 = jnp.where(better, vb[k], m1)
    i2 = jnp.full_like(gsel, -1)
    m2 = jnp.zeros_like(m1)
    for k in range(npg):
        better = (i1 != k) & ((i2 < 0) | (vb[k] > m2))
        i2 = jnp.where(better, k, i2)
        m2 = jnp.where(better, vb[k], m2)
    s1 = vs[0]
    s2 = vs[0]
    for k in range(1, npg):
        s1 = jnp.where(i1 == k, vs[k], s1)
        s2 = jnp.where(i2 == k, vs[k], s2)
    w1 = s1 / (s1 + s2)
    w2 = s2 / (s1 + s2)
    lo = jnp.minimum(i1, i2)
    hi = jnp.maximum(i1, i2)
    pair = jnp.zeros_like(gsel)
    for p, (a, c) in enumerate(PAIRS):
        pair = jnp.where((lo == a) & (hi == c), p, pair)
    first_is_lo = i1 < i2
    return (gsel * len(PAIRS) + pair, jnp.where(first_is_lo, w1, w2), jnp.where(first_is_lo, w2, w1))


def _route_body(x_ref, g_ref, sc_ref, sh_ref, rw_ref, rb_ref, cnt_in_ref, *rest, aliased):
    if aliased:
        rest = rest[4:]
    rows_ref, gates_ref, bucket_ref, rank_ref, cnt_ref = rest
    tm = x_ref.shape[0]

    @pl.when(pl.program_id(0) == 0)
    def _():
        cnt_ref[...] = cnt_in_ref[...]

    hn = _prenorm(x_ref[...], g_ref[...], sc_ref[...], sh_ref[...])
    logits_t = _dot3(hn, rw_ref[...]).T
    scores = _sigmoid(logits_t[0:N_EXPERTS, :])
    biased = scores + rb_ref[0:N_EXPERTS, :]
    bucket, w_lo, w_hi = _route_rows([scores[e:e + 1, :] for e in range(N_EXPERTS)],
                                     [biased[e:e + 1, :] for e in range(N_EXPERTS)])
    bucket_ref[...] = bucket
    onehot = (lax.broadcasted_iota(I32, (BUCKET_ROWS, tm), 0) == bucket).astype(F32)
    before = (lax.broadcasted_iota(I32, (tm, tm), 0) < lax.broadcasted_iota(I32, (tm, tm), 1)).astype(BF16)
    earlier = _dot(onehot.astype(BF16), before) + cnt_ref[:, 0:1]
    rank_ref[...] = jnp.sum(onehot * earlier, axis=0, keepdims=True).astype(I32)
    cnt_ref[...] = cnt_ref[...] + jnp.sum(onehot, axis=1, keepdims=True)

    sub = lax.broadcasted_iota(I32, (LANE, tm), 0)
    gate_t = jnp.where(sub == 0, w_lo, jnp.where(sub == 1, w_hi, 0.0))
    gates_ref[...] = gate_t.T
    for j in range(ROW_PLANES):
        rows_ref[pl.ds(j, tm, stride=ROW_PLANES), :] = hn[:, j * LANE:(j + 1) * LANE]


def _route_call(x, g, mod, router_w, router_b, counts, tm, total_rows, row_offset, prior, name):
    t, d = x.shape
    blk0 = row_offset // tm
    rw = jnp.pad(router_w, ((0, 0), (0, LANE - N_EXPERTS)))
    rb = jnp.pad(router_b.reshape(-1, 1), ((0, LANE - N_EXPERTS), (0, 0)))
    in_specs = [pl.BlockSpec((tm, d), lambda i: (i, 0)),
                pl.BlockSpec((1, d), lambda i: (0, 0)),
                mod.spec(4, tm, 1), mod.spec(3, tm, 1),
                pl.BlockSpec((d, LANE), lambda i: (0, 0)),
                pl.BlockSpec((LANE, 1), lambda i: (0, 0)),
                pl.BlockSpec((BUCKET_ROWS, LANE), lambda i: (0, 0))]
    args = [x, g.reshape(1, d), mod.arr, mod.arr, rw, rb, counts]
    aliases = {}
    if prior is not None:
        in_specs += [pl.BlockSpec(memory_space=pl.ANY)] * len(prior)
        aliases = {len(args) + k: k for k in range(len(prior))}
        args += list(prior)
    return pl.pallas_call(
        functools.partial(_route_body, aliased=prior is not None),
        grid=(t // tm,),
        in_specs=in_specs,
        out_specs=[pl.BlockSpec((tm * ROW_PLANES, LANE), lambda i: (blk0 + i, 0)),
                   pl.BlockSpec((tm, LANE), lambda i: (blk0 + i, 0)),
                   pl.BlockSpec((1, tm), lambda i: (0, blk0 + i)),
                   pl.BlockSpec((1, tm), lambda i: (0, blk0 + i)),
                   pl.BlockSpec((BUCKET_ROWS, LANE), lambda i: (0, 0))],
        out_shape=[jax.ShapeDtypeStruct((total_rows * ROW_PLANES, LANE), F32),
                   jax.ShapeDtypeStruct((total_rows, LANE), F32),
                   jax.ShapeDtypeStruct((1, total_rows), I32),
                   jax.ShapeDtypeStruct((1, total_rows), I32),
                   jax.ShapeDtypeStruct((BUCKET_ROWS, LANE), F32)],
        input_output_aliases=aliases,
        compiler_params=_cparams(("arbitrary",)),
        name=name,
    )(*args)


def _invert_body(dest_ref, init_ref, gather_ref):
    pltpu.sync_copy(init_ref, gather_ref)

    def put(t, carry):
        gather_ref[dest_ref[t]] = t
        return carry

    lax.fori_loop(0, dest_ref.shape[0], put, 0, unroll=8)


def _invert_call(dest, n_slots, name):
    smem = pl.BlockSpec(memory_space=pltpu.SMEM)
    return pl.pallas_call(
        _invert_body,
        in_specs=[smem, pl.BlockSpec(memory_space=pl.ANY)],
        out_specs=smem,
        out_shape=jax.ShapeDtypeStruct((n_slots,), I32),
        name=name,
    )(dest, jnp.zeros((n_slots,), I32))


def _unrolled(lo, hi, body, carry):
    for b in range(lo, hi):
        carry = body(b, carry)
    return carry


def _experts_body(grp_ref, lo_ref, hi_ref, fresh_ref, valid_ref, out_ref, gather_ref, rows_ref, wg_in_ref,
                  wu_in_ref, wd_in_ref, y_ref, wg_ref, wu_ref, wd_ref, xb_ref, gsem):
    i = pl.program_id(0)
    n = pl.num_programs(0)
    tile = EXPERT_TILE
    words = tile * ROW_PLANES
    group = 8
    del grp_ref, out_ref

    def start_gather(step, slot, loop):
        def body(b, carry):
            for k in range(group):
                r = b * group + k
                tok = gather_ref[step * tile + r]
                pltpu.make_async_copy(rows_ref.at[pl.ds(pl.multiple_of(tok * ROW_PLANES, ROW_PLANES), ROW_PLANES)],
                                      xb_ref.at[slot, pl.ds(pl.multiple_of(r * ROW_PLANES, ROW_PLANES), ROW_PLANES)],
                                      gsem.at[slot]).start(priority=k % 2)
            return carry

        loop(0, tile // group, body, 0)

    def wait_gather(slot):
        pltpu.make_async_copy(rows_ref.at[pl.ds(0, words)], xb_ref.at[slot], gsem.at[slot]).wait()

    @pl.when(fresh_ref[i] == 1)
    def _():
        for e in range(EXPERTS_PER_GROUP):
            wg_ref[e] = wg_in_ref[e].astype(BF16)
            wu_ref[e] = wu_in_ref[e].astype(BF16)
            wd_ref[e] = wd_in_ref[e].astype(BF16)

    nxt = jnp.minimum(i + 1, n - 1)
    more = jnp.logical_and(i + 1 < n, valid_ref[nxt] == 1)

    @pl.when(valid_ref[i] == 1)
    def _():
        slot = i % GATHER_SLOTS

        @pl.when(i == 0)
        def _():
            for ahead in range(GATHER_SLOTS - 1):
                start_gather(ahead, ahead, lax.fori_loop)

        wait_gather(slot)
        ahead = i + GATHER_SLOTS - 1
        start_gather(jnp.minimum(ahead, n - 1), ahead % GATHER_SLOTS, _unrolled)
        x = jnp.concatenate([xb_ref[slot, pl.ds(j, tile, stride=ROW_PLANES), :].astype(BF16)
                             for j in range(ROW_PLANES)], axis=-1)

        def expert(e):
            act = (_silu(_dot(x, wg_ref[e])) * _dot(x, wu_ref[e])).astype(BF16)
            return _dot(act, wd_ref[e])

        y_lo = expert(lo_ref[i])
        y_hi = expert(hi_ref[i])
        for j in range(ROW_PLANES):
            y_ref[pl.ds(j, tile, stride=OUT_PLANES), :] = y_lo[:, j * LANE:(j + 1) * LANE]
            y_ref[pl.ds(ROW_PLANES + j, tile, stride=OUT_PLANES), :] = y_hi[:, j * LANE:(j + 1) * LANE]

        @pl.when(jnp.logical_not(more))
        def _():
            for k in range(1, GATHER_SLOTS):
                wait_gather((i + k) % GATHER_SLOTS)


def _experts_call(grp, lo, hi, fresh, valid, out_tile, gather_idx, rows, w_gate, w_up, w_down, layer, name):
    n_tiles = grp.shape[0]
    tile = EXPERT_TILE
    d, f = D_MODEL, D_EXPERT
    npg = EXPERTS_PER_GROUP
    group_block = lambda i, grp, *_: (layer, grp[i], 0, 0)
    once = pl.Buffered(1)
    return pl.pallas_call(
        _experts_body,
        grid_spec=pltpu.PrefetchScalarGridSpec(
            num_scalar_prefetch=7,
            grid=(n_tiles,),
            in_specs=[pl.BlockSpec(memory_space=pl.ANY),
                      pl.BlockSpec((None, npg, d, f), group_block, pipeline_mode=once),
                      pl.BlockSpec((None, npg, d, f), group_block, pipeline_mode=once),
                      pl.BlockSpec((None, npg, f, d), group_block, pipeline_mode=once)],
            out_specs=pl.BlockSpec((tile * OUT_PLANES, LANE), lambda i, g, l, h, fr, va, out, *_: (out[i], 0)),
            scratch_shapes=[pltpu.VMEM((npg, d, f), BF16), pltpu.VMEM((npg, d, f), BF16),
                            pltpu.VMEM((npg, f, d), BF16),
                            pltpu.VMEM((GATHER_SLOTS, tile * ROW_PLANES, LANE), F32),
                            pltpu.SemaphoreType.DMA((GATHER_SLOTS,))]),
        out_shape=jax.ShapeDtypeStruct((n_tiles * tile * OUT_PLANES, LANE), F32),
        compiler_params=_cparams(("arbitrary",)),
        name=name,
    )(grp, lo, hi, fresh, valid, out_tile, gather_idx, rows, w_gate, w_up, w_down)


def _moe_res_body(dest_ref, y_ref, w_ref, x_ref, gate_ref, fg_ref, o_ref, *rest, row_offset, final):
    if final:
        on_ref, buf_ref, sem = rest
    else:
        buf_ref, sem = rest
    i = pl.program_id(0)
    tm = x_ref.shape[0]
    group = 8

    def start_gather(step, slot, loop):
        base = row_offset + step * tm

        def body(b, carry):
            for k in range(group):
                r = b * group + k
                src = pl.multiple_of(dest_ref[base + r] * OUT_PLANES, OUT_PLANES)
                pltpu.make_async_copy(y_ref.at[pl.ds(src, OUT_PLANES)],
                                      buf_ref.at[slot, pl.ds(pl.multiple_of(r * OUT_PLANES, OUT_PLANES), OUT_PLANES)],
                                      sem.at[slot]).start(priority=k % 2)
            return carry

        loop(0, tm // group, body, 0)

    def wait_gather(slot):
        pltpu.make_async_copy(y_ref.at[pl.ds(0, tm * OUT_PLANES)], buf_ref.at[slot], sem.at[slot]).wait()

    @pl.when(i == 0)
    def _():
        start_gather(0, 0, lax.fori_loop)

    slot = i % 2
    last = pl.num_programs(0) - 1
    wait_gather(slot)
    start_gather(jnp.minimum(i + 1, last), 1 - slot, lax.fori_loop)
    y_lo = jnp.concatenate([buf_ref[slot, pl.ds(j, tm, stride=OUT_PLANES), :] for j in range(ROW_PLANES)], axis=-1)
    y_hi = jnp.concatenate([buf_ref[slot, pl.ds(ROW_PLANES + j, tm, stride=OUT_PLANES), :]
                            for j in range(ROW_PLANES)], axis=-1)
    xn = x_ref[...] + gate_ref[...] * (w_ref[:, 0:1] * y_lo + w_ref[:, 1:2] * y_hi)
    o_ref[...] = xn
    if final:
        ms = jnp.mean(xn * xn, axis=-1, keepdims=True)
        on_ref[...] = xn * lax.rsqrt(ms + EPS) * fg_ref[...]

    @pl.when(i == last)
    def _():
        wait_gather(1 - slot)


def _moe_res_call(dest, y_sorted, weights, x, mod, final_g, tm, row_offset, final, name):
    t, d = x.shape
    blk0 = row_offset // tm
    n_out = 2 if final else 1
    gate_spec = mod.spec(5, tm, 1)
    gate_map = gate_spec.index_map
    return pl.pallas_call(
        functools.partial(_moe_res_body, row_offset=row_offset, final=final),
        grid_spec=pltpu.PrefetchScalarGridSpec(
            num_scalar_prefetch=1,
            grid=(t // tm,),
            in_specs=[pl.BlockSpec(memory_space=pl.ANY),
                      pl.BlockSpec((tm, LANE), lambda i, dest: (blk0 + i, 0)),
                      pl.BlockSpec((tm, d), lambda i, dest: (i, 0)),
                      pl.BlockSpec(gate_spec.block_shape, lambda i, dest: gate_map(i)),
                      pl.BlockSpec((1, d), lambda i, dest: (0, 0))],
            out_specs=[pl.BlockSpec((tm, d), lambda i, dest: (i, 0))] * n_out,
            scratch_shapes=[pltpu.VMEM((2, tm * OUT_PLANES, LANE), F32), pltpu.SemaphoreType.DMA((2,))]),
        out_shape=[jax.ShapeDtypeStruct((t, d), F32)] * n_out,
        compiler_params=_cparams(("arbitrary",)),
        name=name,
    )(dest, y_sorted, weights, x, mod.arr, final_g.reshape(1, d))


def _tile_tables(counts, n_tiles):
    tile = EXPERT_TILE
    tiles_per_bucket = (counts + tile - 1) // tile
    ends = jnp.cumsum(tiles_per_bucket)
    starts = ends - tiles_per_bucket
    used = ends[-1]
    ti = jnp.arange(n_tiles, dtype=I32)
    valid = (ti < used).astype(I32)
    tile_bucket = jnp.sum((jnp.minimum(ti, used - 1)[:, None] >= ends[None, :]).astype(I32), axis=1)
    pair_lo = jnp.asarray([p[0] for p in PAIRS], I32)
    pair_hi = jnp.asarray([p[1] for p in PAIRS], I32)
    grp = tile_bucket // len(PAIRS)
    lo = pair_lo[tile_bucket % len(PAIRS)]
    hi = pair_hi[tile_bucket % len(PAIRS)]
    fresh = jnp.concatenate([jnp.ones((1,), I32), (grp[1:] != grp[:-1]).astype(I32)])
    return grp, lo, hi, fresh, valid, jnp.minimum(ti, used - 1).astype(I32), (starts * tile).astype(I32)


def _moe_layer(xp, xs, norm_g2, mod_p, mod_s, router_w, router_b, w_gate, w_up, w_down, layer, final_g, final):
    tp, ts = xp.shape[0], xs.shape[0]
    total = tp + ts
    n_tiles = -(-total // EXPERT_TILE) + N_BUCKETS
    zero_counts = jnp.zeros((BUCKET_ROWS, LANE), F32)
    rows, weights, bucket, rank, counts = _route_call(xp, norm_g2, mod_p, router_w, router_b, zero_counts, 512, total,
                                                      0, None, "route_prompt")
    rows, weights, bucket, rank, counts = _route_call(xs, norm_g2, mod_s, router_w, router_b, counts, ts, total, tp,
                                                      (rows, weights, bucket, rank), "route_sample")
    grp, lo, hi, fresh, valid, out_tile, bucket_start = _tile_tables(counts[:N_BUCKETS, 0].astype(I32), n_tiles)
    dest = bucket_start[bucket[0]] + rank[0]
    gather_idx = _invert_call(dest, n_tiles * EXPERT_TILE, "invert")
    y_sorted = _experts_call(grp, lo, hi, fresh, valid, out_tile, gather_idx, rows, w_gate, w_up, w_down, layer,
                             "experts")
    outp = _moe_res_call(dest, y_sorted, weights, xp, mod_p, final_g, 512, 0, final, "moe_res_prompt")
    outs = _moe_res_call(dest, y_sorted, weights, xs, mod_s, final_g, ts, tp, final, "moe_res_sample")
    return outp, outs


def _mamba_layer(x, g, mod, conv_prev, ssm, w_zx, w_dt, conv_w, conv_b, dt_bias, a_log, d_skip, norm_g, w_out,
                 tm, tag):
    pad_h = lambda v: jnp.pad(v.reshape(1, -1), ((0, 0), (0, LANE - A_N_HEADS)))
    tm_in = min(2 * tm, x.shape[0])
    proj = _norm_mm_call(x, g, mod, 1, 0, w_zx, tm_in, 512, BF16, "a_in_" + tag)
    dt_raw = _norm_mm3_call(x, g, mod, 1, 0, w_dt, tm, "a_dt_" + tag)
    d_x = jnp.repeat(d_skip, A_HEAD_DIM).reshape(1, A_D_INNER)
    weights = (conv_w, conv_b.reshape(1, -1), pad_h(dt_bias), pad_h(a_log), d_x, norm_g.reshape(1, -1))
    if ssm[0] == "step":
        yn, conv_new, ssm_new = _ssd_step_call(proj, dt_raw, conv_prev, ssm[1], ssm[2], ssm[3], *weights,
                                               "ssd_step_" + tag)
    else:
        yn, conv_new, ssm_new = _ssd_call(proj, dt_raw, conv_prev, ssm[2], *weights, ssm[1], "ssd_" + tag)
    x = _out_res_call(yn, w_out, x, mod, 2, tm, "a_out_" + tag)
    return x, conv_new, ssm_new


def kernel(x_prompt, x_sample, c_prompt, c_sample, state_a_conv, state_a_ssm, state_c_pool, w_mod, b_mod, norm_g, final_g, a_w_in, a_conv_w, a_conv_b, a_dt_bias, a_log, a_d, a_norm_g, a_w_out, b_w_in, b_b_in, b_ln_g, b_ln_b, b_w_s, b_b_s, b_w_out, c_w_g, c_scale, router_w, router_b, e_w_gate, e_w_up, e_w_down):
    bp, seq, d = x_prompt.shape
    bs = x_sample.shape[0]
    n_a, n_c = state_a_conv.shape[0], state_c_pool.shape[0]
    mod_all = _mod_call(jnp.concatenate([c_prompt, c_sample], axis=0), w_mod, b_mod)
    mod_p_arr = mod_all[:, :bp].reshape(DEPTH, bp, 6, 1, d)
    mod_s_arr = mod_all[:, bp:]
    xp = x_prompt.reshape(bp * seq, d)
    xs = x_sample.reshape(bs, d)
    conv_p, ssm_p, pool_p, conv_s, pool_s, v_s = [], [], [], [], [], []
    hp = A_N_HEADS * A_HEAD_DIM
    ssm_s_in = state_a_ssm.reshape(n_a, bs, hp, A_D_STATE)
    ssm_s_out = None
    yp = ys = None
    for i in range(DEPTH):
        kind, s = LAYER_KIND[i], LAYER_SLOT[i]
        mod_p = Mod(mod_p_arr, i, False, seq)
        mod_s = Mod(mod_s_arr, i, True)
        g1 = norm_g[i, 0]
        if kind == 0:
            w_zx = a_w_in[s, :, :A_ZX].astype(BF16)
            w_dt = jnp.pad(a_w_in[s, :, A_ZX:], ((0, 0), (0, LANE - A_N_HEADS)))
            weights = (w_zx, w_dt, a_conv_w[s], a_conv_b[s], a_dt_bias[s], a_log[s], a_d[s], a_norm_g[s], a_w_out[s])
            conv0 = jnp.zeros((bp, A_CONV - 1, A_CONV_CH), F32)
            ssm0 = jnp.zeros((bp, hp, A_D_STATE), F32)
            xp, cv, ss = _mamba_layer(xp, g1, mod_p, conv0, ("prompt", bp, ssm0), *weights, 1024, "p%d" % i)
            conv_p.append(cv)
            ssm_p.append(ss.reshape(bp, A_N_HEADS, A_HEAD_DIM, A_D_STATE))
            xs, cv, ssm_s_out = _mamba_layer(xs, g1, mod_s, state_a_conv[s], ("step", ssm_s_in, s, ssm_s_out),
                                             *weights, bs, "s%d" % i)
            conv_s.append(cv)
        elif kind == 1:
            w_uv = b_w_in[s].astype(BF16)
            uv = _gmlp_in_call(xp, g1, mod_p, w_uv, b_b_in[s], b_ln_g[s], b_ln_b[s], 512, BF16, "b_in_p%d" % i)
            xp = _gmlp_out_call(uv, b_w_s[s], b_b_s[s], b_w_out[s], xp, mod_p, 512, "b_out_p%d" % i)
            uv = _gmlp_in_call(xs, g1, mod_s, w_uv, b_b_in[s], b_ln_g[s], b_ln_b[s], bs, F32, "b_in_s%d" % i)
            xs = _gmlp_out_step_call(uv, b_w_s[s], b_b_s[s], b_w_out[s], xs, mod_s, "b_out_s%d" % i)
            v_s.append(uv[:, B_D:].reshape(bs, 1, B_D))
        else:
            pool0 = jnp.zeros((bp, C_STATE, d), F32)
            xp, pr = _pool_call(xp, g1, mod_p, pool0, c_w_g[s], c_scale[s], bp, 512, 0, "pool_p%d" % i)
            pool_p.append(pr)
            xs, pr = _pool_step_call(xs, g1, mod_s, state_c_pool[s], c_w_g[s], c_scale[s], PAST_LEN, "pool_s%d" % i)
            pool_s.append(pr)
        final = i == DEPTH - 1
        outp, outs = _moe_layer(xp, xs, norm_g[i, 1], mod_p, mod_s, router_w, router_b, e_w_gate, e_w_up, e_w_down, i,
                                final_g, final)
        xp, xs = outp[0], outs[0]
        if final:
            yp, ys = outp[1], outs[1]
    return (yp.reshape(bp, seq, d), ys.reshape(bs, 1, d), jnp.stack(conv_p), jnp.stack(ssm_p), jnp.stack(pool_p),
            jnp.stack(conv_s), ssm_s_out.reshape(state_a_ssm.shape), jnp.stack(pool_s), jnp.stack(v_s))
```

```python
import functools
import math

import numpy as np
import jax
import jax.numpy as jnp
from jax import lax
from jax.experimental import pallas as pl
from jax.experimental.pallas import tpu as pltpu

F32 = jnp.float32
BF16 = jnp.bfloat16
I32 = jnp.int32
EPS = 1e-6

LANE = 128
D_MODEL = 1024
DEPTH = 4
PAST_LEN = 16384
LAYER_KIND = (0, 1, 2, 0)
LAYER_SLOT = (0, 0, 0, 1)
A_D_INNER = 2 * D_MODEL
A_HEAD_DIM = 64
A_N_HEADS = A_D_INNER // A_HEAD_DIM
A_N_GROUPS = 8
A_HPG = A_N_HEADS // A_N_GROUPS
A_D_STATE = 128
A_GN = A_N_GROUPS * A_D_STATE
A_CONV = 4
A_CONV_CH = A_D_INNER + 2 * A_GN
A_ZX = A_D_INNER + A_CONV_CH
A_CHUNK = 128
A_GW = A_HPG * A_HEAD_DIM
B_D = 2 * D_MODEL
B_N_GROUPS = 8
B_GROUP_DIM = B_D // B_N_GROUPS
B_CHUNK = 128
C_WINDOWS = (2, 4, 8, 16)
C_GROUP_DIM = D_MODEL // len(C_WINDOWS)
C_STATE = max(C_WINDOWS) - 1
N_EXPERTS = 16
N_EXPERT_GROUPS = 4
EXPERTS_PER_GROUP = 4
D_EXPERT = D_MODEL // 2
PAIRS = ((0, 1), (0, 2), (0, 3), (1, 2), (1, 3), (2, 3))
N_BUCKETS = N_EXPERT_GROUPS * len(PAIRS)
BUCKET_ROWS = 32
ROW_PLANES = D_MODEL // LANE
ROW_PITCH = ROW_PLANES + 4
OUT_PLANES = 2 * ROW_PLANES
OUT_PITCH = OUT_PLANES + 4
EXPERT_TILE = 256
GATHER_SLOTS = 3
VMEM_LIMIT = 56 * 1024 * 1024


def _cparams(sem, vmem=VMEM_LIMIT):
    return pltpu.CompilerParams(dimension_semantics=sem, vmem_limit_bytes=vmem)


def _sigmoid(x):
    return 1.0 / (1.0 + jnp.exp(-x))


def _silu(x):
    return x * _sigmoid(x)


def _gelu_tanh(x):
    c = 2.0 * math.sqrt(2.0 / math.pi)
    return x / (1.0 + jnp.exp(x * (-c - (c * 0.044715) * (x * x))))


def _softplus(x):
    return jnp.maximum(x, 0.0) + jnp.log1p(jnp.exp(-jnp.abs(x)))


def _split2(a):
    hi = a.astype(BF16)
    lo = (a - hi.astype(F32)).astype(BF16)
    return hi, lo


def _split3(a):
    hi = a.astype(BF16)
    r = a - hi.astype(F32)
    mid = r.astype(BF16)
    lo = (r - mid.astype(F32)).astype(BF16)
    return hi, mid, lo


def _dot(a, b):
    return jnp.dot(a, b, preferred_element_type=F32)


def _dot_nt(a, b):
    return lax.dot_general(a, b, (((1,), (1,)), ((), ())), preferred_element_type=F32)


def _dot_tn(a, b):
    return lax.dot_general(a, b, (((0,), (0,)), ((), ())), preferred_element_type=F32)


def _dot3(a, b):
    a_hi, a_lo = _split2(a)
    b_hi, b_lo = _split2(b)
    return _dot(a_hi, b_hi) + (_dot(a_lo, b_hi) + _dot(a_hi, b_lo))


def _prenorm(x, g, sc, sh):
    ms = jnp.mean(x * x, axis=-1, keepdims=True)
    return (x * lax.rsqrt(ms + EPS) * g) * (1.0 + sc) + sh


def _mod_body(c_ref, w_ref, b_ref, o_ref):
    o_ref[...] = _dot3(_silu(c_ref[...]), w_ref[...]) + b_ref[...]


def _mod_call(c_all, w_mod, b_mod):
    nb, d = c_all.shape
    depth, _, n = w_mod.shape
    tn = 1536
    return pl.pallas_call(
        _mod_body,
        grid=(depth, n // tn),
        in_specs=[pl.BlockSpec((nb, d), lambda i, j: (0, 0)),
                  pl.BlockSpec((None, d, tn), lambda i, j: (i, 0, j)),
                  pl.BlockSpec((None, 1, tn), lambda i, j: (i, 0, j))],
        out_specs=pl.BlockSpec((None, nb, tn), lambda i, j: (i, 0, j)),
        out_shape=jax.ShapeDtypeStruct((depth, nb, n), F32),
        compiler_params=_cparams(("arbitrary", "arbitrary")),
        name="mod",
    )(c_all, w_mod, b_mod.reshape(depth, 1, n))


class Mod:
    def __init__(self, arr, layer, per_row, rows_per_seq=None):
        self.arr, self.layer, self.per_row, self.rows_per_seq = arr, layer, per_row, rows_per_seq

    def spec(self, which, tm, ngrid):
        layer = self.layer
        if self.per_row:
            if ngrid == 1:
                return pl.BlockSpec((None, tm, D_MODEL), lambda i: (layer, i, which))
            return pl.BlockSpec((None, tm, D_MODEL), lambda i, j: (layer, i, which))
        tiles = self.rows_per_seq // tm
        if ngrid == 1:
            return pl.BlockSpec((None, None, None, 1, D_MODEL), lambda i: (layer, i // tiles, which, 0, 0))
        return pl.BlockSpec((None, None, None, 1, D_MODEL), lambda i, j: (layer, i // tiles, which, 0, 0))


def _norm_mm_body(x_ref, g_ref, sc_ref, sh_ref, w_ref, o_ref, hn_ref):
    @pl.when(pl.program_id(1) == 0)
    def _():
        hn_ref[...] = _prenorm(x_ref[...], g_ref[...], sc_ref[...], sh_ref[...]).astype(BF16)

    o_ref[...] = _dot(hn_ref[...], w_ref[...]).astype(o_ref.dtype)


def _norm_mm_call(x, g, mod, which_sc, which_sh, w, tm, tn, out_dtype, name):
    t, d = x.shape
    n_cols = w.shape[1]
    return pl.pallas_call(
        _norm_mm_body,
        grid=(t // tm, n_cols // tn),
        in_specs=[pl.BlockSpec((tm, d), lambda i, j: (i, 0)),
                  pl.BlockSpec((1, d), lambda i, j: (0, 0)),
                  mod.spec(which_sc, tm, 2), mod.spec(which_sh, tm, 2),
                  pl.BlockSpec((d, tn), lambda i, j: (0, j))],
        out_specs=pl.BlockSpec((tm, tn), lambda i, j: (i, j)),
        out_shape=jax.ShapeDtypeStruct((t, n_cols), out_dtype),
        scratch_shapes=[pltpu.VMEM((tm, d), BF16)],
        compiler_params=_cparams(("arbitrary", "arbitrary")),
        name=name,
    )(x, g.reshape(1, d), mod.arr, mod.arr, w)


def _norm_mm3_body(x_ref, g_ref, sc_ref, sh_ref, w_ref, o_ref):
    hn = _prenorm(x_ref[...], g_ref[...], sc_ref[...], sh_ref[...])
    o_ref[...] = _dot3(hn, w_ref[...])


def _norm_mm3_call(x, g, mod, which_sc, which_sh, w, tm, name):
    t, d = x.shape
    n = w.shape[1]
    return pl.pallas_call(
        _norm_mm3_body,
        grid=(t // tm,),
        in_specs=[pl.BlockSpec((tm, d), lambda i: (i, 0)),
                  pl.BlockSpec((1, d), lambda i: (0, 0)),
                  mod.spec(which_sc, tm, 1), mod.spec(which_sh, tm, 1),
                  pl.BlockSpec((d, n), lambda i: (0, 0))],
        out_specs=pl.BlockSpec((tm, n), lambda i: (i, 0)),
        out_shape=jax.ShapeDtypeStruct((t, n), F32),
        compiler_params=_cparams(("arbitrary",)),
        name=name,
    )(x, g.reshape(1, d), mod.arr, mod.arr, w)


def _out_res_body(y_ref, w_ref, x_ref, gate_ref, o_ref, wbf_ref):
    @pl.when(pl.program_id(0) == 0)
    def _():
        wbf_ref[...] = w_ref[...].astype(BF16)

    o_ref[...] = x_ref[...] + gate_ref[...] * _dot(y_ref[...], wbf_ref[...])


def _out_res_call(y, w, x, mod, which_gate, tm, name):
    t, k = y.shape
    d = x.shape[1]
    return pl.pallas_call(
        _out_res_body,
        grid=(t // tm,),
        in_specs=[pl.BlockSpec((tm, k), lambda i: (i, 0)),
                  pl.BlockSpec((k, d), lambda i: (0, 0)),
                  pl.BlockSpec((tm, d), lambda i: (i, 0)),
                  mod.spec(which_gate, tm, 1)],
        out_specs=pl.BlockSpec((tm, d), lambda i: (i, 0)),
        out_shape=jax.ShapeDtypeStruct((t, d), F32),
        scratch_shapes=[pltpu.VMEM((k, d), BF16)],
        compiler_params=_cparams(("arbitrary",)),
        name=name,
    )(y, w, x, mod.arr)


def _head_expand():
    h = np.arange(LANE)[:, None]
    c = np.arange(A_D_INNER)[None, :]
    return jnp.asarray((c // A_HEAD_DIM == h).astype(np.float32), dtype=BF16)


def _gate_norm(y, z, ng):
    gated = y * _silu(z)
    ms = jnp.mean(gated * gated, axis=-1, keepdims=True)
    return gated * lax.rsqrt(ms + EPS) * ng


def _ssd_body(z_ref, xs_ref, bc_ref, dt_ref, cprev_ref, sprev_ref, cw_ref, cb_ref, dtb_ref, alog_ref,
              dx_ref, ng_ref, exp_ref, yn_ref, cnew_ref, snew_ref, tail_ref, act_ref):
    c = pl.program_id(1)
    q = A_CHUNK
    nt = A_CONV - 1

    @pl.when(c == 0)
    def _():
        tail_ref[0:nt, :] = cprev_ref[...]
        snew_ref[...] = sprev_ref[...]

    row = lax.broadcasted_iota(I32, (q, q), 0)
    col = lax.broadcasted_iota(I32, (q, q), 1)
    shifts = [jnp.where(row - col == nt - k, 1.0, 0.0).astype(BF16) for k in range(nt)]
    sub = lax.broadcasted_iota(I32, (8, 1), 0)
    cw = 512
    for j in range(A_CONV_CH // cw):
        sl = slice(j * cw, (j + 1) * cw)
        src = xs_ref if (j + 1) * cw <= A_D_INNER else bc_ref
        off = j * cw if src is xs_ref else j * cw - A_D_INNER
        x_bf = src[:, off:off + cw]
        conv = cb_ref[:, sl] + x_bf.astype(F32) * cw_ref[nt:nt + 1, sl]
        for k in range(nt):
            conv = conv + _dot(shifts[k], x_bf) * cw_ref[k:k + 1, sl]
        act_ref[:, sl] = _silu(conv)
        corr = jnp.zeros((8, cw), F32)
        for l in range(nt):
            c_l = sum(tail_ref[l + k:l + k + 1, sl] * cw_ref[k:k + 1, sl] for k in range(nt - l))
            corr = jnp.where(sub == l, c_l, corr)
        act_ref[0:8, sl] = _silu(conv[0:8, :] + corr)
        tail_ref[0:nt, sl] = x_bf[q - 8:q, :].astype(F32)[8 - nt:8, :]

    @pl.when(c == pl.num_programs(1) - 1)
    def _():
        cnew_ref[...] = tail_ref[0:nt, :]

    dt = _softplus(dt_ref[...] + dtb_ref[...])
    a = dt * (-jnp.exp(alog_ref[...]))
    row = lax.broadcasted_iota(I32, (q, q), 0)
    col = lax.broadcasted_iota(I32, (q, q), 1)
    causal = row >= col
    tril = jnp.where(causal, 1.0, 0.0).astype(BF16)
    a_hi, a_mid, a_lo = _split3(a)
    cs = _dot(tril, a_hi) + (_dot(tril, a_mid) + _dot(tril, a_lo))
    cs_t = cs.T
    dt_t = dt.T
    cs_last = cs[q - 1:q, :]
    ecs = jnp.exp(cs)
    wend = jnp.exp(cs_last - cs) * dt
    st_hi, st_lo = _split2(jnp.concatenate([ecs, wend], axis=0))
    st_x = _dot(st_hi, exp_ref[...]) + _dot(st_lo, exp_ref[...])
    lane_head = lax.broadcasted_iota(I32, (q, A_GW), 1) // A_HEAD_DIM

    for g in range(A_N_GROUPS):
        gsl = slice(g * A_GW, (g + 1) * A_GW)
        b_g = act_ref[:, A_D_INNER + g * A_D_STATE:A_D_INNER + (g + 1) * A_D_STATE].astype(BF16)
        c_g = act_ref[:, A_D_INNER + A_GN + g * A_D_STATE:A_D_INNER + A_GN + (g + 1) * A_D_STATE].astype(BF16)
        x_g = act_ref[:, gsl]
        x_bf = x_g.astype(BF16)
        h_g = snew_ref[gsl, :]
        cb = _dot_nt(c_g, b_g)
        y = jnp.zeros((q, A_GW), F32)
        for r in range(A_HPG):
            h = g * A_HPG + r
            seg = cs[:, h:h + 1] - cs_t[h:h + 1, :]
            decay = jnp.exp(jnp.where(causal, seg, -1e30))
            wm = (cb * decay * dt_t[h:h + 1, :]).astype(BF16)
            y = y + _dot(wm, jnp.where(lane_head == r, x_bf, jnp.zeros_like(x_bf)))
        y = y + st_x[0:q, gsl] * _dot_nt(c_g, h_g.astype(BF16)) + dx_ref[:, gsl] * x_g
        yn_ref[:, gsl] = _gate_norm(y, z_ref[:, gsl].astype(F32), ng_ref[:, gsl]).astype(BF16)
        s_new = _dot_tn((x_g * st_x[q:2 * q, gsl]).astype(BF16), b_g)
        for r in range(A_HPG):
            h = g * A_HPG + r
            rsl = slice(g * A_GW + r * A_HEAD_DIM, g * A_GW + (r + 1) * A_HEAD_DIM)
            keep = jnp.exp(cs[q - 1:q, h:h + 1])
            snew_ref[rsl, :] = snew_ref[rsl, :] * keep + s_new[r * A_HEAD_DIM:(r + 1) * A_HEAD_DIM, :]


def _ssd_call(proj, dt_raw, conv_prev, ssm_prev, conv_w, conv_b, dt_bias, a_log, d_x, norm_g, nb, name):
    t = proj.shape[0]
    q = A_CHUNK
    nc = t // nb // q
    hp = A_N_HEADS * A_HEAD_DIM
    row = lambda b, c: (b * nc + c, 0)
    full = lambda shape: pl.BlockSpec(shape, lambda b, c: (0,) * len(shape))
    return pl.pallas_call(
        _ssd_body,
        grid=(nb, nc),
        in_specs=[pl.BlockSpec((q, A_D_INNER), lambda b, c: (b * nc + c, 0)),
                  pl.BlockSpec((q, A_D_INNER), lambda b, c: (b * nc + c, 1)),
                  pl.BlockSpec((q, 2 * A_GN), lambda b, c: (b * nc + c, 2)),
                  pl.BlockSpec((q, LANE), row),
                  pl.BlockSpec((None, A_CONV - 1, A_CONV_CH), lambda b, c: (b, 0, 0)),
                  pl.BlockSpec((None, hp, A_D_STATE), lambda b, c: (b, 0, 0)),
                  full((A_CONV, A_CONV_CH)), full((1, A_CONV_CH)), full((1, LANE)), full((1, LANE)),
                  full((1, A_D_INNER)), full((1, A_D_INNER)), full((LANE, A_D_INNER))],
        out_specs=[pl.BlockSpec((q, A_D_INNER), row),
                   pl.BlockSpec((None, A_CONV - 1, A_CONV_CH), lambda b, c: (b, 0, 0)),
                   pl.BlockSpec((None, hp, A_D_STATE), lambda b, c: (b, 0, 0))],
        out_shape=[jax.ShapeDtypeStruct((t, A_D_INNER), BF16),
                   jax.ShapeDtypeStruct((nb, A_CONV - 1, A_CONV_CH), F32),
                   jax.ShapeDtypeStruct((nb, hp, A_D_STATE), F32)],
        scratch_shapes=[pltpu.VMEM((8, A_CONV_CH), F32), pltpu.VMEM((q, A_CONV_CH), F32)],
        compiler_params=_cparams(("arbitrary", "arbitrary")),
        name=name,
    )(proj, proj, proj, dt_raw, conv_prev, ssm_prev, conv_w, conv_b, dt_bias, a_log, d_x, norm_g, _head_expand())


def _ssd_step_body(z_ref, xs_ref, bc_ref, dt_ref, cprev_ref, sprev_ref, cw_ref, cb_ref, dtb_ref, alog_ref,
                   dx_ref, ng_ref, exp_ref, yn_ref, cnew_ref, snew_ref, da_ref, y_ref):
    bt = z_ref.shape[0]
    cur = jnp.concatenate([xs_ref[...], bc_ref[...]], axis=1).astype(F32)
    conv = cb_ref[...] + cur * cw_ref[A_CONV - 1:A_CONV, :]
    for k in range(A_CONV - 1):
        conv = conv + cprev_ref[:, k, :] * cw_ref[k:k + 1, :]
    for k in range(A_CONV - 2):
        cnew_ref[:, k, :] = cprev_ref[:, k + 1, :]
    cnew_ref[:, A_CONV - 2, :] = cur
    act = _silu(conv)
    xs = act[:, 0:A_D_INNER]
    bm_bf = act[:, A_D_INNER:A_D_INNER + A_GN].astype(BF16)
    cm_bf = act[:, A_D_INNER + A_GN:A_CONV_CH].astype(BF16)
    dt = _softplus(dt_ref[...] + dtb_ref[...])
    da_ref[...] = jnp.exp(dt * (-jnp.exp(alog_ref[...])))
    dt_hi, dt_lo = _split2(dt)
    dt_x = _dot(dt_hi, exp_ref[...]) + _dot(dt_lo, exp_ref[...])
    xdt = xs * dt_x
    rows = lax.broadcasted_iota(I32, (bt, 1), 0)
    y_ref[...] = jnp.zeros_like(y_ref)

    def per_seq(j, carry):
        mine = rows == j
        xdt_j = jnp.where(mine, xdt, 0.0)
        da_j = da_ref[pl.ds(j, 1), :]
        for g in range(A_N_GROUPS):
            gsl = slice(g * A_GW, (g + 1) * A_GW)
            nsl = slice(g * A_D_STATE, (g + 1) * A_D_STATE)
            outer = _dot_tn(xdt_j[:, gsl].astype(BF16), bm_bf[:, nsl])
            for r in range(A_HPG):
                h = g * A_HPG + r
                rsl = slice(r * A_HEAD_DIM, (r + 1) * A_HEAD_DIM)
                hsl = slice(g * A_GW + r * A_HEAD_DIM, g * A_GW + (r + 1) * A_HEAD_DIM)
                snew_ref[j, hsl, :] = sprev_ref[j, hsl, :] * da_j[:, h:h + 1] + outer[rsl, :]
            yg = _dot_nt(cm_bf[:, nsl], snew_ref[j, gsl, :].astype(BF16))
            y_ref[:, gsl] = y_ref[:, gsl] + jnp.where(mine, yg, 0.0)
        return carry

    lax.fori_loop(0, bt, per_seq, 0)
    y = y_ref[...] + dx_ref[...] * xs
    z = z_ref[...].astype(F32)
    for g in range(A_N_GROUPS):
        gsl = slice(g * A_GW, (g + 1) * A_GW)
        yn_ref[:, gsl] = _gate_norm(y[:, gsl], z[:, gsl], ng_ref[:, gsl]).astype(BF16)


_SSD_STEP_INPUTS = 13


def _ssd_step_aliased_body(*refs):
    _ssd_step_body(*refs[:_SSD_STEP_INPUTS], *refs[_SSD_STEP_INPUTS + 1:])


def _ssd_step_call(proj, dt_raw, conv_prev, ssm_all, slot, ssm_out, conv_w, conv_b, dt_bias, a_log, d_x, norm_g,
                   name):
    nb = proj.shape[0]
    bt = 8
    hp = A_N_HEADS * A_HEAD_DIM
    full = lambda shape: pl.BlockSpec(shape, lambda i: (0,) * len(shape))
    state_spec = pl.BlockSpec((None, bt, hp, A_D_STATE), lambda i: (slot, i, 0, 0))
    in_specs = [pl.BlockSpec((bt, A_D_INNER), lambda i: (i, 0)),
                pl.BlockSpec((bt, A_D_INNER), lambda i: (i, 1)),
                pl.BlockSpec((bt, 2 * A_GN), lambda i: (i, 2)),
                pl.BlockSpec((bt, LANE), lambda i: (i, 0)),
                pl.BlockSpec((bt, A_CONV - 1, A_CONV_CH), lambda i: (i, 0, 0)),
                state_spec,
                full((A_CONV, A_CONV_CH)), full((1, A_CONV_CH)), full((1, LANE)), full((1, LANE)),
                full((1, A_D_INNER)), full((1, A_D_INNER)), full((LANE, A_D_INNER))]
    args = [proj, proj, proj, dt_raw, conv_prev, ssm_all, conv_w, conv_b, dt_bias, a_log, d_x, norm_g, _head_expand()]
    assert len(args) == _SSD_STEP_INPUTS
    aliases = {}
    if ssm_out is not None:
        in_specs.append(pl.BlockSpec(memory_space=pl.ANY))
        args.append(ssm_out)
        aliases = {_SSD_STEP_INPUTS: 2}
    return pl.pallas_call(
        _ssd_step_body if ssm_out is None else _ssd_step_aliased_body,
        grid=(nb // bt,),
        in_specs=in_specs,
        out_specs=[pl.BlockSpec((bt, A_D_INNER), lambda i: (i, 0)),
                   pl.BlockSpec((bt, A_CONV - 1, A_CONV_CH), lambda i: (i, 0, 0)),
                   state_spec],
        out_shape=[jax.ShapeDtypeStruct((nb, A_D_INNER), BF16),
                   jax.ShapeDtypeStruct((nb, A_CONV - 1, A_CONV_CH), F32),
                   jax.ShapeDtypeStruct(ssm_all.shape, F32)],
        scratch_shapes=[pltpu.VMEM((bt, LANE), F32), pltpu.VMEM((bt, A_D_INNER), F32)],
        input_output_aliases=aliases,
        compiler_params=_cparams(("arbitrary",)),
        name=name,
    )(*args)


def _gmlp_in_body(x_ref, g_ref, sc_ref, sh_ref, w_ref, b_ref, lg_ref, lb_ref, o_ref, hn_ref):
    j = pl.program_id(1)

    @pl.when(j == 0)
    def _():
        hn_ref[...] = _prenorm(x_ref[...], g_ref[...], sc_ref[...], sh_ref[...]).astype(BF16)

    uv = _gelu_tanh(_dot(hn_ref[...], w_ref[...]) + b_ref[...])

    @pl.when(j == 0)
    def _():
        o_ref[...] = uv.astype(o_ref.dtype)

    @pl.when(j == 1)
    def _():
        vc = uv - jnp.mean(uv, axis=-1, keepdims=True)
        var = jnp.mean(vc * vc, axis=-1, keepdims=True)
        o_ref[...] = (vc * lax.rsqrt(var + EPS) * lg_ref[...] + lb_ref[...]).astype(o_ref.dtype)


def _gmlp_in_call(x, g, mod, w, b, ln_g, ln_b, tm, out_dtype, name):
    t, d = x.shape
    return pl.pallas_call(
        _gmlp_in_body,
        grid=(t // tm, 2),
        in_specs=[pl.BlockSpec((tm, d), lambda i, j: (i, 0)),
                  pl.BlockSpec((1, d), lambda i, j: (0, 0)),
                  mod.spec(1, tm, 2), mod.spec(0, tm, 2),
                  pl.BlockSpec((d, B_D), lambda i, j: (0, j)),
                  pl.BlockSpec((1, B_D), lambda i, j: (0, j)),
                  pl.BlockSpec((1, B_D), lambda i, j: (0, 0)),
                  pl.BlockSpec((1, B_D), lambda i, j: (0, 0))],
        out_specs=pl.BlockSpec((tm, B_D), lambda i, j: (i, j)),
        out_shape=jax.ShapeDtypeStruct((t, 2 * B_D), out_dtype),
        scratch_shapes=[pltpu.VMEM((tm, d), BF16)],
        compiler_params=_cparams(("arbitrary", "arbitrary")),
        name=name,
    )(x, g.reshape(1, d), mod.arr, mod.arr, w, b.reshape(1, -1), ln_g.reshape(1, -1), ln_b.reshape(1, -1))


def _gmlp_out_body(u_ref, v_ref, ws_ref, bs_ref, w_ref, x_ref, gate_ref, o_ref, wbf_ref, wsbf_ref, m_ref):
    q = B_CHUNK

    @pl.when(pl.program_id(0) == 0)
    def _():
        wbf_ref[...] = w_ref[...].astype(BF16)
        causal = lax.broadcasted_iota(I32, (q, q), 0) >= lax.broadcasted_iota(I32, (q, q), 1)
        for g in range(B_N_GROUPS):
            wsbf_ref[g] = jnp.where(causal, ws_ref[g], 0.0).astype(BF16)

    for ci in range(u_ref.shape[0] // q):
        rsl = slice(ci * q, (ci + 1) * q)
        for g in range(B_N_GROUPS):
            gsl = slice(g * B_GROUP_DIM, (g + 1) * B_GROUP_DIM)
            mixed = _dot(wsbf_ref[g], v_ref[rsl, gsl].astype(BF16)) + bs_ref[:, g:g + 1]
            m_ref[rsl, gsl] = (u_ref[rsl, gsl].astype(F32) * mixed).astype(BF16)
    o_ref[...] = x_ref[...] + gate_ref[...] * _dot(m_ref[...], wbf_ref[...])


def _gmlp_out_call(uv, w_s, b_s, w_out, x, mod, tm, name):
    t, d = x.shape
    q = B_CHUNK
    return pl.pallas_call(
        _gmlp_out_body,
        grid=(t // tm,),
        in_specs=[pl.BlockSpec((tm, B_D), lambda i: (i, 0)),
                  pl.BlockSpec((tm, B_D), lambda i: (i, 1)),
                  pl.BlockSpec((B_N_GROUPS, q, q), lambda i: (0, 0, 0)),
                  pl.BlockSpec((q, B_N_GROUPS), lambda i: (0, 0)),
                  pl.BlockSpec((B_D, d), lambda i: (0, 0)),
                  pl.BlockSpec((tm, d), lambda i: (i, 0)),
                  mod.spec(2, tm, 1)],
        out_specs=pl.BlockSpec((tm, d), lambda i: (i, 0)),
        out_shape=jax.ShapeDtypeStruct((t, d), F32),
        scratch_shapes=[pltpu.VMEM((B_D, d), BF16), pltpu.VMEM((B_N_GROUPS, q, q), BF16),
                        pltpu.VMEM((tm, B_D), BF16)],
        compiler_params=_cparams(("arbitrary",)),
        name=name,
    )(uv, uv, w_s, b_s.T, w_out, x, mod.arr)


def _gmlp_out_step_body(u_ref, v_ref, wd_ref, bd_ref, w_ref, x_ref, gate_ref, o_ref):
    mixed = v_ref[...] * wd_ref[...] + bd_ref[...]
    m = (u_ref[...] * mixed).astype(BF16)
    o_ref[...] = x_ref[...] + gate_ref[...] * _dot(m, w_ref[...].astype(BF16))


def _gmlp_out_step_call(uv, w_s, b_s, w_out, x, mod, name):
    t, d = x.shape
    wd = jnp.repeat(w_s[:, 0, 0], B_GROUP_DIM).reshape(1, B_D)
    bd = jnp.repeat(b_s[:, 0], B_GROUP_DIM).reshape(1, B_D)
    return pl.pallas_call(
        _gmlp_out_step_body,
        grid=(1,),
        in_specs=[pl.BlockSpec((t, B_D), lambda i: (0, 0)),
                  pl.BlockSpec((t, B_D), lambda i: (0, 1)),
                  pl.BlockSpec((1, B_D), lambda i: (0, 0)),
                  pl.BlockSpec((1, B_D), lambda i: (0, 0)),
                  pl.BlockSpec((B_D, d), lambda i: (0, 0)),
                  pl.BlockSpec((t, d), lambda i: (0, 0)),
                  mod.spec(2, t, 1)],
        out_specs=pl.BlockSpec((t, d), lambda i: (0, 0)),
        out_shape=jax.ShapeDtypeStruct((t, d), F32),
        compiler_params=_cparams(("arbitrary",)),
        name=name,
    )(uv, uv, wd, bd, w_out, x, mod.arr)


def _pool_matmul(pooled, wg_ref):
    outs = []
    for gi in range(len(C_WINDOWS)):
        gsl = slice(gi * C_GROUP_DIM, (gi + 1) * C_GROUP_DIM)
        outs.append(_dot(pooled[:, gsl].astype(BF16), wg_ref[gi].astype(BF16)))
    return jnp.concatenate(outs, axis=-1)


def _pool_body(x_ref, g_ref, sc_ref, sh_ref, gate_ref, prev_ref, wg_ref, scale_ref, o_ref, pnew_ref, hp_ref,
               *, tiles_per_seq, start):
    i = pl.program_id(0)
    tm = x_ref.shape[0]
    top = 16
    ti = i % tiles_per_seq

    @pl.when(ti == 0)
    def _():
        hp_ref[top - C_STATE:top, :] = prev_ref[...]

    hn = _prenorm(x_ref[...], g_ref[...], sc_ref[...], sh_ref[...])
    hp_ref[top:top + tm, :] = hn
    pos = start + ti * tm + lax.broadcasted_iota(I32, (tm, 1), 0)
    outs = []
    for gi, w in enumerate(C_WINDOWS):
        gsl = slice(gi * C_GROUP_DIM, (gi + 1) * C_GROUP_DIM)
        acc = hn[:, gsl]
        for k in range(1, w):
            acc = acc + hp_ref[top - k:top - k + tm, gsl]
        cnt = jnp.minimum(pos + 1, w).astype(F32)
        outs.append(acc / cnt - hn[:, gsl])
    y = _pool_matmul(jnp.concatenate(outs, axis=-1), wg_ref) * scale_ref[...]
    o_ref[...] = x_ref[...] + gate_ref[...] * y
    hist = hp_ref[top + tm - C_STATE:top + tm, :]
    hp_ref[top - C_STATE:top, :] = hist

    @pl.when(ti == tiles_per_seq - 1)
    def _():
        pnew_ref[...] = hist


def _pool_call(x, g, mod, prev, w_g, scale, nb, tm, start, name):
    t, d = x.shape
    tiles = t // nb // tm
    ng = len(C_WINDOWS)
    return pl.pallas_call(
        functools.partial(_pool_body, tiles_per_seq=tiles, start=start),
        grid=(t // tm,),
        in_specs=[pl.BlockSpec((tm, d), lambda i: (i, 0)),
                  pl.BlockSpec((1, d), lambda i: (0, 0)),
                  mod.spec(1, tm, 1), mod.spec(0, tm, 1), mod.spec(2, tm, 1),
                  pl.BlockSpec((None, C_STATE, d), lambda i: (i // tiles, 0, 0)),
                  pl.BlockSpec((ng, C_GROUP_DIM, C_GROUP_DIM), lambda i: (0, 0, 0)),
                  pl.BlockSpec((1, d), lambda i: (0, 0))],
        out_specs=[pl.BlockSpec((tm, d), lambda i: (i, 0)),
                   pl.BlockSpec((None, C_STATE, d), lambda i: (i // tiles, 0, 0))],
        out_shape=[jax.ShapeDtypeStruct((t, d), F32), jax.ShapeDtypeStruct((nb, C_STATE, d), F32)],
        scratch_shapes=[pltpu.VMEM((16 + tm, d), F32)],
        compiler_params=_cparams(("arbitrary",)),
        name=name,
    )(x, g.reshape(1, d), mod.arr, mod.arr, mod.arr, prev, w_g, scale.reshape(1, d))


def _pool_step_body(x_ref, g_ref, sc_ref, sh_ref, gate_ref, prev_ref, wg_ref, scale_ref, o_ref, pnew_ref, *, start):
    hn = _prenorm(x_ref[...], g_ref[...], sc_ref[...], sh_ref[...])
    outs = []
    for gi, w in enumerate(C_WINDOWS):
        gsl = slice(gi * C_GROUP_DIM, (gi + 1) * C_GROUP_DIM)
        acc = hn[:, gsl]
        for k in range(1, w):
            acc = acc + prev_ref[:, C_STATE - k, gsl]
        outs.append(acc / float(min(start + 1, w)) - hn[:, gsl])
    y = _pool_matmul(jnp.concatenate(outs, axis=-1), wg_ref) * scale_ref[...]
    o_ref[...] = x_ref[...] + gate_ref[...] * y
    for k in range(C_STATE - 1):
        pnew_ref[:, k, :] = prev_ref[:, k + 1, :]
    pnew_ref[:, C_STATE - 1, :] = hn


def _pool_step_call(x, g, mod, prev, w_g, scale, start, name):
    t, d = x.shape
    bt = 32
    ng = len(C_WINDOWS)
    return pl.pallas_call(
        functools.partial(_pool_step_body, start=start),
        grid=(t // bt,),
        in_specs=[pl.BlockSpec((bt, d), lambda i: (i, 0)),
                  pl.BlockSpec((1, d), lambda i: (0, 0)),
                  mod.spec(1, bt, 1), mod.spec(0, bt, 1), mod.spec(2, bt, 1),
                  pl.BlockSpec((bt, C_STATE, d), lambda i: (i, 0, 0)),
                  pl.BlockSpec((ng, C_GROUP_DIM, C_GROUP_DIM), lambda i: (0, 0, 0)),
                  pl.BlockSpec((1, d), lambda i: (0, 0))],
        out_specs=[pl.BlockSpec((bt, d), lambda i: (i, 0)),
                   pl.BlockSpec((bt, C_STATE, d), lambda i: (i, 0, 0))],
        out_shape=[jax.ShapeDtypeStruct((t, d), F32), jax.ShapeDtypeStruct((t, C_STATE, d), F32)],
        compiler_params=_cparams(("arbitrary",)),
        name=name,
    )(x, g.reshape(1, d), mod.arr, mod.arr, mod.arr, prev, w_g, scale.reshape(1, d))


def _route_rows(s, b):
    npg = EXPERTS_PER_GROUP
    gscore = []
    for q in range(N_EXPERT_GROUPS):
        v = b[q * npg:(q + 1) * npg]
        best = None
        for i in range(npg):
            for j in range(i + 1, npg):
                best = v[i] + v[j] if best is None else jnp.maximum(best, v[i] + v[j])
        gscore.append(best)
    gsel = jnp.zeros_like(gscore[0], dtype=I32)
    gbest = gscore[0]
    for q in range(1, N_EXPERT_GROUPS):
        better = gscore[q] > gbest
        gsel = jnp.where(better, q, gsel)
        gbest = jnp.where(better, gscore[q], gbest)
    vb, vs = [], []
    for k in range(npg):
        bk, sk = b[k], s[k]
        for q in range(1, N_EXPERT_GROUPS):
            bk = jnp.where(gsel == q, b[q * npg + k], bk)
            sk = jnp.where(gsel == q, s[q * npg + k], sk)
        vb.append(bk)
        vs.append(sk)
    i1 = jnp.zeros_like(gsel)
    m1 = vb[0]
    for k in range(1, npg):
        better = vb[k] > m1
        i1 = jnp.where(better, k, i1)
        m1 = jnp.where(better, vb[k], m1)
    i2 = jnp.full_like(gsel, -1)
    m2 = jnp.zeros_like(m1)
    for k in range(npg):
        better = (i1 != k) & ((i2 < 0) | (vb[k] > m2))
        i2 = jnp.where(better, k, i2)
        m2 = jnp.where(better, vb[k], m2)
    s1 = vs[0]
    s2 = vs[0]
    for k in range(1, npg):
        s1 = jnp.where(i1 == k, vs[k], s1)
        s2 = jnp.where(i2 == k, vs[k], s2)
    w1 = s1 / (s1 + s2)
    w2 = s2 / (s1 + s2)
    lo = jnp.minimum(i1, i2)
    hi = jnp.maximum(i1, i2)
    pair = jnp.zeros_like(gsel)
    for p, (a, c) in enumerate(PAIRS):
        pair = jnp.where((lo == a) & (hi == c), p, pair)
    first_is_lo = i1 < i2
    return (gsel * len(PAIRS) + pair, jnp.where(first_is_lo, w1, w2), jnp.where(first_is_lo, w2, w1))


def _route_body(x_ref, g_ref, sc_ref, sh_ref, rw_ref, rb_ref, cnt_in_ref, *rest, aliased):
    if aliased:
        rest = rest[4:]
    rows_ref, gates_ref, bucket_ref, rank_ref, cnt_ref = rest
    tm = x_ref.shape[0]

    @pl.when(pl.program_id(0) == 0)
    def _():
        cnt_ref[...] = cnt_in_ref[...]

    hn = _prenorm(x_ref[...], g_ref[...], sc_ref[...], sh_ref[...])
    logits_t = _dot3(hn, rw_ref[...]).T
    scores = _sigmoid(logits_t[0:N_EXPERTS, :])
    biased = scores + rb_ref[0:N_EXPERTS, :]
    bucket, w_lo, w_hi = _route_rows([scores[e:e + 1, :] for e in range(N_EXPERTS)],
                                     [biased[e:e + 1, :] for e in range(N_EXPERTS)])
    bucket_ref[...] = bucket
    onehot = (lax.broadcasted_iota(I32, (BUCKET_ROWS, tm), 0) == bucket).astype(F32)
    before = (lax.broadcasted_iota(I32, (tm, tm), 0) < lax.broadcasted_iota(I32, (tm, tm), 1)).astype(BF16)
    earlier = _dot(onehot.astype(BF16), before) + cnt_ref[:, 0:1]
    rank_ref[...] = jnp.sum(onehot * earlier, axis=0, keepdims=True).astype(I32)
    cnt_ref[...] = cnt_ref[...] + jnp.sum(onehot, axis=1, keepdims=True)

    sub = lax.broadcasted_iota(I32, (LANE, tm), 0)
    gate_t = jnp.where(sub == 0, w_lo, jnp.where(sub == 1, w_hi, 0.0))
    gates_ref[...] = gate_t.T
    for j in range(ROW_PLANES):
        rows_ref[pl.ds(j, tm, stride=ROW_PITCH), :] = hn[:, j * LANE:(j + 1) * LANE]


def _route_call(x, g, mod, router_w, router_b, counts, tm, total_rows, row_offset, prior, name):
    t, d = x.shape
    blk0 = row_offset // tm
    rw = jnp.pad(router_w, ((0, 0), (0, LANE - N_EXPERTS)))
    rb = jnp.pad(router_b.reshape(-1, 1), ((0, LANE - N_EXPERTS), (0, 0)))
    in_specs = [pl.BlockSpec((tm, d), lambda i: (i, 0)),
                pl.BlockSpec((1, d), lambda i: (0, 0)),
                mod.spec(4, tm, 1), mod.spec(3, tm, 1),
                pl.BlockSpec((d, LANE), lambda i: (0, 0)),
                pl.BlockSpec((LANE, 1), lambda i: (0, 0)),
                pl.BlockSpec((BUCKET_ROWS, LANE), lambda i: (0, 0))]
    args = [x, g.reshape(1, d), mod.arr, mod.arr, rw, rb, counts]
    aliases = {}
    if prior is not None:
        in_specs += [pl.BlockSpec(memory_space=pl.ANY)] * len(prior)
        aliases = {len(args) + k: k for k in range(len(prior))}
        args += list(prior)
    return pl.pallas_call(
        functools.partial(_route_body, aliased=prior is not None),
        grid=(t // tm,),
        in_specs=in_specs,
        out_specs=[pl.BlockSpec((tm * ROW_PITCH, LANE), lambda i: (blk0 + i, 0)),
                   pl.BlockSpec((tm, LANE), lambda i: (blk0 + i, 0)),
                   pl.BlockSpec((1, tm), lambda i: (0, blk0 + i)),
                   pl.BlockSpec((1, tm), lambda i: (0, blk0 + i)),
                   pl.BlockSpec((BUCKET_ROWS, LANE), lambda i: (0, 0))],
        out_shape=[jax.ShapeDtypeStruct((total_rows * ROW_PITCH, LANE), F32),
                   jax.ShapeDtypeStruct((total_rows, LANE), F32),
                   jax.ShapeDtypeStruct((1, total_rows), I32),
                   jax.ShapeDtypeStruct((1, total_rows), I32),
                   jax.ShapeDtypeStruct((BUCKET_ROWS, LANE), F32)],
        input_output_aliases=aliases,
        compiler_params=_cparams(("arbitrary",)),
        name=name,
    )(*args)


def _invert_body(dest_ref, init_ref, gather_ref):
    pltpu.sync_copy(init_ref, gather_ref)

    def put(t, carry):
        gather_ref[dest_ref[t]] = t
        return carry

    lax.fori_loop(0, dest_ref.shape[0], put, 0, unroll=8)


def _invert_call(dest, n_slots, name):
    smem = pl.BlockSpec(memory_space=pltpu.SMEM)
    return pl.pallas_call(
        _invert_body,
        in_specs=[smem, pl.BlockSpec(memory_space=pl.ANY)],
        out_specs=smem,
        out_shape=jax.ShapeDtypeStruct((n_slots,), I32),
        name=name,
    )(dest, jnp.zeros((n_slots,), I32))


def _unrolled(lo, hi, body, carry):
    for b in range(lo, hi):
        carry = body(b, carry)
    return carry


def _experts_body(grp_ref, lo_ref, hi_ref, fresh_ref, valid_ref, out_ref, gather_ref, rows_ref, wg_in_ref,
                  wu_in_ref, wd_in_ref, y_ref, wg_ref, wu_ref, wd_ref, xb_ref, gsem):
    i = pl.program_id(0)
    n = pl.num_programs(0)
    tile = EXPERT_TILE
    group = 8
    del grp_ref, out_ref

    def start_gather(step, slot, loop):
        def body(b, carry):
            for k in range(group):
                r = b * group + k
                tok = gather_ref[step * tile + r]
                pltpu.make_async_copy(rows_ref.at[pl.ds(tok * ROW_PITCH, ROW_PLANES)],
                                      xb_ref.at[slot, pl.ds(r * ROW_PITCH, ROW_PLANES)],
                                      gsem.at[slot]).start(priority=k % 2)
            return carry

        loop(0, tile // group, body, 0)

    def wait_gather(slot):
        rows = tile * ROW_PLANES
        pltpu.make_async_copy(rows_ref.at[pl.ds(0, rows)], xb_ref.at[slot, pl.ds(0, rows)], gsem.at[slot]).wait()

    @pl.when(fresh_ref[i] == 1)
    def _():
        for e in range(EXPERTS_PER_GROUP):
            wg_ref[e] = wg_in_ref[e].astype(BF16)
            wu_ref[e] = wu_in_ref[e].astype(BF16)
            wd_ref[e] = wd_in_ref[e].astype(BF16)

    nxt = jnp.minimum(i + 1, n - 1)
    more = jnp.logical_and(i + 1 < n, valid_ref[nxt] == 1)

    @pl.when(valid_ref[i] == 1)
    def _():
        slot = i % GATHER_SLOTS

        @pl.when(i == 0)
        def _():
            for ahead in range(GATHER_SLOTS - 1):
                start_gather(ahead, ahead, lax.fori_loop)

        wait_gather(slot)
        ahead = i + GATHER_SLOTS - 1
        start_gather(jnp.minimum(ahead, n - 1), ahead % GATHER_SLOTS, _unrolled)
        x = jnp.concatenate([xb_ref[slot, pl.ds(j, tile, stride=ROW_PITCH), :].astype(BF16)
                             for j in range(ROW_PLANES)], axis=-1)

        def expert(e):
            act = (_silu(_dot(x, wg_ref[e])) * _dot(x, wu_ref[e])).astype(BF16)
            return _dot(act, wd_ref[e])

        y_lo = expert(lo_ref[i])
        y_hi = expert(hi_ref[i])
        for j in range(ROW_PLANES):
            y_ref[pl.ds(j, tile, stride=OUT_PITCH), :] = y_lo[:, j * LANE:(j + 1) * LANE]
            y_ref[pl.ds(ROW_PLANES + j, tile, stride=OUT_PITCH), :] = y_hi[:, j * LANE:(j + 1) * LANE]

        @pl.when(jnp.logical_not(more))
        def _():
            for k in range(1, GATHER_SLOTS):
                wait_gather((i + k) % GATHER_SLOTS)


def _experts_call(grp, lo, hi, fresh, valid, out_tile, gather_idx, rows, w_gate, w_up, w_down, layer, name):
    n_tiles = grp.shape[0]
    tile = EXPERT_TILE
    d, f = D_MODEL, D_EXPERT
    npg = EXPERTS_PER_GROUP
    group_block = lambda i, grp, *_: (layer, grp[i], 0, 0)
    once = pl.Buffered(1)
    return pl.pallas_call(
        _experts_body,
        grid_spec=pltpu.PrefetchScalarGridSpec(
            num_scalar_prefetch=7,
            grid=(n_tiles,),
            in_specs=[pl.BlockSpec(memory_space=pl.ANY),
                      pl.BlockSpec((None, npg, d, f), group_block, pipeline_mode=once),
                      pl.BlockSpec((None, npg, d, f), group_block, pipeline_mode=once),
                      pl.BlockSpec((None, npg, f, d), group_block, pipeline_mode=once)],
            out_specs=pl.BlockSpec((tile * OUT_PITCH, LANE), lambda i, g, l, h, fr, va, out, *_: (out[i], 0)),
            scratch_shapes=[pltpu.VMEM((npg, d, f), BF16), pltpu.VMEM((npg, d, f), BF16),
                            pltpu.VMEM((npg, f, d), BF16),
                            pltpu.VMEM((GATHER_SLOTS, tile * ROW_PITCH, LANE), F32),
                            pltpu.SemaphoreType.DMA((GATHER_SLOTS,))]),
        out_shape=jax.ShapeDtypeStruct((n_tiles * tile * OUT_PITCH, LANE), F32),
        compiler_params=_cparams(("arbitrary",)),
        name=name,
    )(grp, lo, hi, fresh, valid, out_tile, gather_idx, rows, w_gate, w_up, w_down)


def _moe_res_body(dest_ref, y_ref, w_ref, x_ref, gate_ref, fg_ref, o_ref, *rest, row_offset, final):
    if final:
        on_ref, buf_ref, sem = rest
    else:
        buf_ref, sem = rest
    i = pl.program_id(0)
    tm = x_ref.shape[0]
    group = 8

    def start_gather(step, slot, loop):
        base = row_offset + step * tm

        def body(b, carry):
            for k in range(group):
                r = b * group + k
                pltpu.make_async_copy(y_ref.at[pl.ds(dest_ref[base + r] * OUT_PITCH, OUT_PLANES)],
                                      buf_ref.at[slot, pl.ds(r * OUT_PITCH, OUT_PLANES)],
                                      sem.at[slot]).start(priority=k % 2)
            return carry

        loop(0, tm // group, body, 0)

    def wait_gather(slot):
        rows = tm * OUT_PLANES
        pltpu.make_async_copy(y_ref.at[pl.ds(0, rows)], buf_ref.at[slot, pl.ds(0, rows)], sem.at[slot]).wait()

    @pl.when(i == 0)
    def _():
        start_gather(0, 0, lax.fori_loop)

    slot = i % 2
    last = pl.num_programs(0) - 1
    wait_gather(slot)
    start_gather(jnp.minimum(i + 1, last), 1 - slot, lax.fori_loop)
    y_lo = jnp.concatenate([buf_ref[slot, pl.ds(j, tm, stride=OUT_PITCH), :] for j in range(ROW_PLANES)], axis=-1)
    y_hi = jnp.concatenate([buf_ref[slot, pl.ds(ROW_PLANES + j, tm, stride=OUT_PITCH), :]
                            for j in range(ROW_PLANES)], axis=-1)
    xn = x_ref[...] + gate_ref[...] * (w_ref[:, 0:1] * y_lo + w_ref[:, 1:2] * y_hi)
    o_ref[...] = xn
    if final:
        ms = jnp.mean(xn * xn, axis=-1, keepdims=True)
        on_ref[...] = xn * lax.rsqrt(ms + EPS) * fg_ref[...]

    @pl.when(i == last)
    def _():
        wait_gather(1 - slot)


def _moe_res_call(dest, y_sorted, weights, x, mod, final_g, tm, row_offset, final, name):
    t, d = x.shape
    blk0 = row_offset // tm
    n_out = 2 if final else 1
    gate_spec = mod.spec(5, tm, 1)
    gate_map = gate_spec.index_map
    return pl.pallas_call(
        functools.partial(_moe_res_body, row_offset=row_offset, final=final),
        grid_spec=pltpu.PrefetchScalarGridSpec(
            num_scalar_prefetch=1,
            grid=(t // tm,),
            in_specs=[pl.BlockSpec(memory_space=pl.ANY),
                      pl.BlockSpec((tm, LANE), lambda i, dest: (blk0 + i, 0)),
                      pl.BlockSpec((tm, d), lambda i, dest: (i, 0)),
                      pl.BlockSpec(gate_spec.block_shape, lambda i, dest: gate_map(i)),
                      pl.BlockSpec((1, d), lambda i, dest: (0, 0))],
            out_specs=[pl.BlockSpec((tm, d), lambda i, dest: (i, 0))] * n_out,
            scratch_shapes=[pltpu.VMEM((2, tm * OUT_PITCH, LANE), F32), pltpu.SemaphoreType.DMA((2,))]),
        out_shape=[jax.ShapeDtypeStruct((t, d), F32)] * n_out,
        compiler_params=_cparams(("arbitrary",)),
        name=name,
    )(dest, y_sorted, weights, x, mod.arr, final_g.reshape(1, d))


def _tile_tables(counts, n_tiles):
    tile = EXPERT_TILE
    tiles_per_bucket = (counts + tile - 1) // tile
    ends = jnp.cumsum(tiles_per_bucket)
    starts = ends - tiles_per_bucket
    used = ends[-1]
    ti = jnp.arange(n_tiles, dtype=I32)
    valid = (ti < used).astype(I32)
    tile_bucket = jnp.sum((jnp.minimum(ti, used - 1)[:, None] >= ends[None, :]).astype(I32), axis=1)
    pair_lo = jnp.asarray([p[0] for p in PAIRS], I32)
    pair_hi = jnp.asarray([p[1] for p in PAIRS], I32)
    grp = tile_bucket // len(PAIRS)
    lo = pair_lo[tile_bucket % len(PAIRS)]
    hi = pair_hi[tile_bucket % len(PAIRS)]
    fresh = jnp.concatenate([jnp.ones((1,), I32), (grp[1:] != grp[:-1]).astype(I32)])
    return grp, lo, hi, fresh, valid, jnp.minimum(ti, used - 1).astype(I32), (starts * tile).astype(I32)


def _moe_layer(xp, xs, norm_g2, mod_p, mod_s, router_w, router_b, w_gate, w_up, w_down, layer, final_g, final):
    tp, ts = xp.shape[0], xs.shape[0]
    total = tp + ts
    n_tiles = -(-total // EXPERT_TILE) + N_BUCKETS
    zero_counts = jnp.zeros((BUCKET_ROWS, LANE), F32)
    rows, weights, bucket, rank, counts = _route_call(xp, norm_g2, mod_p, router_w, router_b, zero_counts, 512, total,
                                                      0, None, "route_prompt")
    rows, weights, bucket, rank, counts = _route_call(xs, norm_g2, mod_s, router_w, router_b, counts, ts, total, tp,
                                                      (rows, weights, bucket, rank), "route_sample")
    grp, lo, hi, fresh, valid, out_tile, bucket_start = _tile_tables(counts[:N_BUCKETS, 0].astype(I32), n_tiles)
    dest = bucket_start[bucket[0]] + rank[0]
    gather_idx = _invert_call(dest, n_tiles * EXPERT_TILE, "invert")
    y_sorted = _experts_call(grp, lo, hi, fresh, valid, out_tile, gather_idx, rows, w_gate, w_up, w_down, layer,
                             "experts")
    outp = _moe_res_call(dest, y_sorted, weights, xp, mod_p, final_g, 512, 0, final, "moe_res_prompt")
    outs = _moe_res_call(dest, y_sorted, weights, xs, mod_s, final_g, ts, tp, final, "moe_res_sample")
    return outp, outs


def _mamba_layer(x, g, mod, conv_prev, ssm, w_zx, w_dt, conv_w, conv_b, dt_bias, a_log, d_skip, norm_g, w_out,
                 tm, tag):
    pad_h = lambda v: jnp.pad(v.reshape(1, -1), ((0, 0), (0, LANE - A_N_HEADS)))
    tm_in = min(2 * tm, x.shape[0])
    proj = _norm_mm_call(x, g, mod, 1, 0, w_zx, tm_in, 512, BF16, "a_in_" + tag)
    dt_raw = _norm_mm3_call(x, g, mod, 1, 0, w_dt, tm, "a_dt_" + tag)
    d_x = jnp.repeat(d_skip, A_HEAD_DIM).reshape(1, A_D_INNER)
    weights = (conv_w, conv_b.reshape(1, -1), pad_h(dt_bias), pad_h(a_log), d_x, norm_g.reshape(1, -1))
    if ssm[0] == "step":
        yn, conv_new, ssm_new = _ssd_step_call(proj, dt_raw, conv_prev, ssm[1], ssm[2], ssm[3], *weights,
                                               "ssd_step_" + tag)
    else:
        yn, conv_new, ssm_new = _ssd_call(proj, dt_raw, conv_prev, ssm[2], *weights, ssm[1], "ssd_" + tag)
    x = _out_res_call(yn, w_out, x, mod, 2, tm, "a_out_" + tag)
    return x, conv_new, ssm_new


def kernel(x_prompt, x_sample, c_prompt, c_sample, state_a_conv, state_a_ssm, state_c_pool, w_mod, b_mod, norm_g, final_g, a_w_in, a_conv_w, a_conv_b, a_dt_bias, a_log, a_d, a_norm_g, a_w_out, b_w_in, b_b_in, b_ln_g, b_ln_b, b_w_s, b_b_s, b_w_out, c_w_g, c_scale, router_w, router_b, e_w_gate, e_w_up, e_w_down):
    bp, seq, d = x_prompt.shape
    bs = x_sample.shape[0]
    n_a, n_c = state_a_conv.shape[0], state_c_pool.shape[0]
    mod_all = _mod_call(jnp.concatenate([c_prompt, c_sample], axis=0), w_mod, b_mod)
    mod_p_arr = mod_all[:, :bp].reshape(DEPTH, bp, 6, 1, d)
    mod_s_arr = mod_all[:, bp:]
    xp = x_prompt.reshape(bp * seq, d)
    xs = x_sample.reshape(bs, d)
    conv_p, ssm_p, pool_p, conv_s, pool_s, v_s = [], [], [], [], [], []
    hp = A_N_HEADS * A_HEAD_DIM
    ssm_s_in = state_a_ssm.reshape(n_a, bs, hp, A_D_STATE)
    ssm_s_out = None
    yp = ys = None
    for i in range(DEPTH):
        kind, s = LAYER_KIND[i], LAYER_SLOT[i]
        mod_p = Mod(mod_p_arr, i, False, seq)
        mod_s = Mod(mod_s_arr, i, True)
        g1 = norm_g[i, 0]
        if kind == 0:
            w_zx = a_w_in[s, :, :A_ZX].astype(BF16)
            w_dt = jnp.pad(a_w_in[s, :, A_ZX:], ((0, 0), (0, LANE - A_N_HEADS)))
            weights = (w_zx, w_dt, a_conv_w[s], a_conv_b[s], a_dt_bias[s], a_log[s], a_d[s], a_norm_g[s], a_w_out[s])
            conv0 = jnp.zeros((bp, A_CONV - 1, A_CONV_CH), F32)
            ssm0 = jnp.zeros((bp, hp, A_D_STATE), F32)
            xp, cv, ss = _mamba_layer(xp, g1, mod_p, conv0, ("prompt", bp, ssm0), *weights, 1024, "p%d" % i)
            conv_p.append(cv)
            ssm_p.append(ss.reshape(bp, A_N_HEADS, A_HEAD_DIM, A_D_STATE))
            xs, cv, ssm_s_out = _mamba_layer(xs, g1, mod_s, state_a_conv[s], ("step", ssm_s_in, s, ssm_s_out),
                                             *weights, bs, "s%d" % i)
            conv_s.append(cv)
        elif kind == 1:
            w_uv = b_w_in[s].astype(BF16)
            uv = _gmlp_in_call(xp, g1, mod_p, w_uv, b_b_in[s], b_ln_g[s], b_ln_b[s], 512, BF16, "b_in_p%d" % i)
            xp = _gmlp_out_call(uv, b_w_s[s], b_b_s[s], b_w_out[s], xp, mod_p, 512, "b_out_p%d" % i)
            uv = _gmlp_in_call(xs, g1, mod_s, w_uv, b_b_in[s], b_ln_g[s], b_ln_b[s], bs, F32, "b_in_s%d" % i)
            xs = _gmlp_out_step_call(uv, b_w_s[s], b_b_s[s], b_w_out[s], xs, mod_s, "b_out_s%d" % i)
            v_s.append(uv[:, B_D:].reshape(bs, 1, B_D))
        else:
            pool0 = jnp.zeros((bp, C_STATE, d), F32)
            xp, pr = _pool_call(xp, g1, mod_p, pool0, c_w_g[s], c_scale[s], bp, 512, 0, "pool_p%d" % i)
            pool_p.append(pr)
            xs, pr = _pool_step_call(xs, g1, mod_s, state_c_pool[s], c_w_g[s], c_scale[s], PAST_LEN, "pool_s%d" % i)
            pool_s.append(pr)
        final = i == DEPTH - 1
        outp, outs = _moe_layer(xp, xs, norm_g[i, 1], mod_p, mod_s, router_w, router_b, e_w_gate, e_w_up, e_w_down, i,
                                final_g, final)
        xp, xs = outp[0], outs[0]
        if final:
            yp, ys = outp[1], outs[1]
    return (yp.reshape(bp, seq, d), ys.reshape(bs, 1, d), jnp.stack(conv_p), jnp.stack(ssm_p), jnp.stack(pool_p),
            jnp.stack(conv_s), ssm_s_out.reshape(state_a_ssm.shape), jnp.stack(pool_s), jnp.stack(v_s))
```

```python
import functools
import math

import numpy as np
import jax
import jax.numpy as jnp
from jax import lax
from jax.experimental import pallas as pl
from jax.experimental.pallas import tpu as pltpu

F32 = jnp.float32
BF16 = jnp.bfloat16
I32 = jnp.int32
EPS = 1e-6

LANE = 128
D_MODEL = 1024
DEPTH = 4
PAST_LEN = 16384
LAYER_KIND = (0, 1, 2, 0)
LAYER_SLOT = (0, 0, 0, 1)
A_D_INNER = 2 * D_MODEL
A_HEAD_DIM = 64
A_N_HEADS = A_D_INNER // A_HEAD_DIM
A_N_GROUPS = 8
A_HPG = A_N_HEADS // A_N_GROUPS
A_D_STATE = 128
A_GN = A_N_GROUPS * A_D_STATE
A_CONV = 4
A_CONV_CH = A_D_INNER + 2 * A_GN
A_ZX = A_D_INNER + A_CONV_CH
A_CHUNK = 128
A_GW = A_HPG * A_HEAD_DIM
B_D = 2 * D_MODEL
B_N_GROUPS = 8
B_GROUP_DIM = B_D // B_N_GROUPS
B_CHUNK = 128
C_WINDOWS = (2, 4, 8, 16)
C_GROUP_DIM = D_MODEL // len(C_WINDOWS)
C_STATE = max(C_WINDOWS) - 1
N_EXPERTS = 16
N_EXPERT_GROUPS = 4
EXPERTS_PER_GROUP = 4
D_EXPERT = D_MODEL // 2
PAIRS = ((0, 1), (0, 2), (0, 3), (1, 2), (1, 3), (2, 3))
N_BUCKETS = N_EXPERT_GROUPS * len(PAIRS)
BUCKET_ROWS = 32
ROW_PLANES = D_MODEL // LANE
ROW_PITCH = ROW_PLANES + 4
GATE_ROW = ROW_PLANES
EXPERT_TILE = 256
GATHER_SLOTS = 3
VMEM_LIMIT = 56 * 1024 * 1024


def _cparams(sem, vmem=VMEM_LIMIT):
    return pltpu.CompilerParams(dimension_semantics=sem, vmem_limit_bytes=vmem)


def _sigmoid(x):
    return 1.0 / (1.0 + jnp.exp(-x))


def _silu(x):
    return x * _sigmoid(x)


def _gelu_tanh(x):
    c = 2.0 * math.sqrt(2.0 / math.pi)
    return x / (1.0 + jnp.exp(x * (-c - (c * 0.044715) * (x * x))))


def _softplus(x):
    return jnp.maximum(x, 0.0) + jnp.log1p(jnp.exp(-jnp.abs(x)))


def _split2(a):
    hi = a.astype(BF16)
    lo = (a - hi.astype(F32)).astype(BF16)
    return hi, lo


def _split3(a):
    hi = a.astype(BF16)
    r = a - hi.astype(F32)
    mid = r.astype(BF16)
    lo = (r - mid.astype(F32)).astype(BF16)
    return hi, mid, lo


def _dot(a, b):
    return jnp.dot(a, b, preferred_element_type=F32)


def _dot_nt(a, b):
    return lax.dot_general(a, b, (((1,), (1,)), ((), ())), preferred_element_type=F32)


def _dot_tn(a, b):
    return lax.dot_general(a, b, (((0,), (0,)), ((), ())), preferred_element_type=F32)


def _dot3(a, b):
    a_hi, a_lo = _split2(a)
    b_hi, b_lo = _split2(b)
    return _dot(a_hi, b_hi) + (_dot(a_lo, b_hi) + _dot(a_hi, b_lo))


def _prenorm(x, g, sc, sh):
    ms = jnp.mean(x * x, axis=-1, keepdims=True)
    return (x * lax.rsqrt(ms + EPS) * g) * (1.0 + sc) + sh


def _mod_body(c_ref, w_ref, b_ref, o_ref):
    o_ref[...] = _dot3(_silu(c_ref[...]), w_ref[...]) + b_ref[...]


def _mod_call(c_all, w_mod, b_mod):
    nb, d = c_all.shape
    depth, _, n = w_mod.shape
    tn = 1536
    return pl.pallas_call(
        _mod_body,
        grid=(depth, n // tn),
        in_specs=[pl.BlockSpec((nb, d), lambda i, j: (0, 0)),
                  pl.BlockSpec((None, d, tn), lambda i, j: (i, 0, j)),
                  pl.BlockSpec((None, 1, tn), lambda i, j: (i, 0, j))],
        out_specs=pl.BlockSpec((None, nb, tn), lambda i, j: (i, 0, j)),
        out_shape=jax.ShapeDtypeStruct((depth, nb, n), F32),
        compiler_params=_cparams(("arbitrary", "arbitrary")),
        name="mod",
    )(c_all, w_mod, b_mod.reshape(depth, 1, n))


class Mod:
    def __init__(self, arr, layer, per_row, rows_per_seq=None):
        self.arr, self.layer, self.per_row, self.rows_per_seq = arr, layer, per_row, rows_per_seq

    def spec(self, which, tm, ngrid):
        layer = self.layer
        if self.per_row:
            if ngrid == 1:
                return pl.BlockSpec((None, tm, D_MODEL), lambda i: (layer, i, which))
            return pl.BlockSpec((None, tm, D_MODEL), lambda i, j: (layer, i, which))
        tiles = self.rows_per_seq // tm
        if ngrid == 1:
            return pl.BlockSpec((None, None, None, 1, D_MODEL), lambda i: (layer, i // tiles, which, 0, 0))
        return pl.BlockSpec((None, None, None, 1, D_MODEL), lambda i, j: (layer, i // tiles, which, 0, 0))


def _norm_mm_body(x_ref, g_ref, sc_ref, sh_ref, w_ref, o_ref, hn_ref):
    @pl.when(pl.program_id(1) == 0)
    def _():
        hn_ref[...] = _prenorm(x_ref[...], g_ref[...], sc_ref[...], sh_ref[...]).astype(BF16)

    o_ref[...] = _dot(hn_ref[...], w_ref[...]).astype(o_ref.dtype)


def _norm_mm_call(x, g, mod, which_sc, which_sh, w, tm, tn, out_dtype, name):
    t, d = x.shape
    n_cols = w.shape[1]
    return pl.pallas_call(
        _norm_mm_body,
        grid=(t // tm, n_cols // tn),
        in_specs=[pl.BlockSpec((tm, d), lambda i, j: (i, 0)),
                  pl.BlockSpec((1, d), lambda i, j: (0, 0)),
                  mod.spec(which_sc, tm, 2), mod.spec(which_sh, tm, 2),
                  pl.BlockSpec((d, tn), lambda i, j: (0, j))],
        out_specs=pl.BlockSpec((tm, tn), lambda i, j: (i, j)),
        out_shape=jax.ShapeDtypeStruct((t, n_cols), out_dtype),
        scratch_shapes=[pltpu.VMEM((tm, d), BF16)],
        compiler_params=_cparams(("arbitrary", "arbitrary")),
        name=name,
    )(x, g.reshape(1, d), mod.arr, mod.arr, w)


def _norm_mm3_body(x_ref, g_ref, sc_ref, sh_ref, w_ref, o_ref):
    hn = _prenorm(x_ref[...], g_ref[...], sc_ref[...], sh_ref[...])
    o_ref[...] = _dot3(hn, w_ref[...])


def _norm_mm3_call(x, g, mod, which_sc, which_sh, w, tm, name):
    t, d = x.shape
    n = w.shape[1]
    return pl.pallas_call(
        _norm_mm3_body,
        grid=(t // tm,),
        in_specs=[pl.BlockSpec((tm, d), lambda i: (i, 0)),
                  pl.BlockSpec((1, d), lambda i: (0, 0)),
                  mod.spec(which_sc, tm, 1), mod.spec(which_sh, tm, 1),
                  pl.BlockSpec((d, n), lambda i: (0, 0))],
        out_specs=pl.BlockSpec((tm, n), lambda i: (i, 0)),
        out_shape=jax.ShapeDtypeStruct((t, n), F32),
        compiler_params=_cparams(("arbitrary",)),
        name=name,
    )(x, g.reshape(1, d), mod.arr, mod.arr, w)


def _out_res_body(y_ref, w_ref, x_ref, gate_ref, o_ref, wbf_ref):
    @pl.when(pl.program_id(0) == 0)
    def _():
        wbf_ref[...] = w_ref[...].astype(BF16)

    o_ref[...] = x_ref[...] + gate_ref[...] * _dot(y_ref[...], wbf_ref[...])


def _out_res_call(y, w, x, mod, which_gate, tm, name):
    t, k = y.shape
    d = x.shape[1]
    return pl.pallas_call(
        _out_res_body,
        grid=(t // tm,),
        in_specs=[pl.BlockSpec((tm, k), lambda i: (i, 0)),
                  pl.BlockSpec((k, d), lambda i: (0, 0)),
                  pl.BlockSpec((tm, d), lambda i: (i, 0)),
                  mod.spec(which_gate, tm, 1)],
        out_specs=pl.BlockSpec((tm, d), lambda i: (i, 0)),
        out_shape=jax.ShapeDtypeStruct((t, d), F32),
        scratch_shapes=[pltpu.VMEM((k, d), BF16)],
        compiler_params=_cparams(("arbitrary",)),
        name=name,
    )(y, w, x, mod.arr)


def _head_expand():
    h = np.arange(LANE)[:, None]
    c = np.arange(A_D_INNER)[None, :]
    return jnp.asarray((c // A_HEAD_DIM == h).astype(np.float32), dtype=BF16)


def _gate_norm(y, z, ng):
    gated = y * _silu(z)
    ms = jnp.mean(gated * gated, axis=-1, keepdims=True)
    return gated * lax.rsqrt(ms + EPS) * ng


def _ssd_body(z_ref, xs_ref, bc_ref, dt_ref, cprev_ref, sprev_ref, cw_ref, cb_ref, dtb_ref, alog_ref,
              dx_ref, ng_ref, exp_ref, yn_ref, cnew_ref, snew_ref, tail_ref, act_ref):
    c = pl.program_id(1)
    q = A_CHUNK
    nt = A_CONV - 1

    @pl.when(c == 0)
    def _():
        tail_ref[0:nt, :] = cprev_ref[...]
        snew_ref[...] = sprev_ref[...]

    row = lax.broadcasted_iota(I32, (q, q), 0)
    col = lax.broadcasted_iota(I32, (q, q), 1)
    shifts = [jnp.where(row - col == nt - k, 1.0, 0.0).astype(BF16) for k in range(nt)]
    sub = lax.broadcasted_iota(I32, (8, 1), 0)
    cw = 512
    for j in range(A_CONV_CH // cw):
        sl = slice(j * cw, (j + 1) * cw)
        src = xs_ref if (j + 1) * cw <= A_D_INNER else bc_ref
        off = j * cw if src is xs_ref else j * cw - A_D_INNER
        x_bf = src[:, off:off + cw]
        conv = cb_ref[:, sl] + x_bf.astype(F32) * cw_ref[nt:nt + 1, sl]
        for k in range(nt):
            conv = conv + _dot(shifts[k], x_bf) * cw_ref[k:k + 1, sl]
        act_ref[:, sl] = _silu(conv)
        corr = jnp.zeros((8, cw), F32)
        for l in range(nt):
            c_l = sum(tail_ref[l + k:l + k + 1, sl] * cw_ref[k:k + 1, sl] for k in range(nt - l))
            corr = jnp.where(sub == l, c_l, corr)
        act_ref[0:8, sl] = _silu(conv[0:8, :] + corr)
        tail_ref[0:nt, sl] = x_bf[q - 8:q, :].astype(F32)[8 - nt:8, :]

    @pl.when(c == pl.num_programs(1) - 1)
    def _():
        cnew_ref[...] = tail_ref[0:nt, :]

    dt = _softplus(dt_ref[...] + dtb_ref[...])
    a = dt * (-jnp.exp(alog_ref[...]))
    row = lax.broadcasted_iota(I32, (q, q), 0)
    col = lax.broadcasted_iota(I32, (q, q), 1)
    causal = row >= col
    tril = jnp.where(causal, 1.0, 0.0).astype(BF16)
    a_hi, a_mid, a_lo = _split3(a)
    cs = _dot(tril, a_hi) + (_dot(tril, a_mid) + _dot(tril, a_lo))
    cs_t = cs.T
    dt_t = dt.T
    cs_last = cs[q - 1:q, :]
    ecs = jnp.exp(cs)
    wend = jnp.exp(cs_last - cs) * dt
    st_hi, st_lo = _split2(jnp.concatenate([ecs, wend], axis=0))
    st_x = _dot(st_hi, exp_ref[...]) + _dot(st_lo, exp_ref[...])
    lane_head = lax.broadcasted_iota(I32, (q, A_GW), 1) // A_HEAD_DIM

    for g in range(A_N_GROUPS):
        gsl = slice(g * A_GW, (g + 1) * A_GW)
        b_g = act_ref[:, A_D_INNER + g * A_D_STATE:A_D_INNER + (g + 1) * A_D_STATE].astype(BF16)
        c_g = act_ref[:, A_D_INNER + A_GN + g * A_D_STATE:A_D_INNER + A_GN + (g + 1) * A_D_STATE].astype(BF16)
        x_g = act_ref[:, gsl]
        x_bf = x_g.astype(BF16)
        h_g = snew_ref[gsl, :]
        cb = _dot_nt(c_g, b_g)
        y = jnp.zeros((q, A_GW), F32)
        for r in range(A_HPG):
            h = g * A_HPG + r
            seg = cs[:, h:h + 1] - cs_t[h:h + 1, :]
            decay = jnp.exp(jnp.where(causal, seg, -1e30))
            wm = (cb * decay * dt_t[h:h + 1, :]).astype(BF16)
            y = y + _dot(wm, jnp.where(lane_head == r, x_bf, jnp.zeros_like(x_bf)))
        y = y + st_x[0:q, gsl] * _dot_nt(c_g, h_g.astype(BF16)) + dx_ref[:, gsl] * x_g
        yn_ref[:, gsl] = _gate_norm(y, z_ref[:, gsl].astype(F32), ng_ref[:, gsl]).astype(BF16)
        s_new = _dot_tn((x_g * st_x[q:2 * q, gsl]).astype(BF16), b_g)
        for r in range(A_HPG):
            h = g * A_HPG + r
            rsl = slice(g * A_GW + r * A_HEAD_DIM, g * A_GW + (r + 1) * A_HEAD_DIM)
            keep = jnp.exp(cs[q - 1:q, h:h + 1])
            snew_ref[rsl, :] = snew_ref[rsl, :] * keep + s_new[r * A_HEAD_DIM:(r + 1) * A_HEAD_DIM, :]


def _ssd_call(proj, dt_raw, conv_prev, ssm_prev, conv_w, conv_b, dt_bias, a_log, d_x, norm_g, nb, name):
    t = proj.shape[0]
    q = A_CHUNK
    nc = t // nb // q
    hp = A_N_HEADS * A_HEAD_DIM
    row = lambda b, c: (b * nc + c, 0)
    full = lambda shape: pl.BlockSpec(shape, lambda b, c: (0,) * len(shape))
    return pl.pallas_call(
        _ssd_body,
        grid=(nb, nc),
        in_specs=[pl.BlockSpec((q, A_D_INNER), lambda b, c: (b * nc + c, 0)),
                  pl.BlockSpec((q, A_D_INNER), lambda b, c: (b * nc + c, 1)),
                  pl.BlockSpec((q, 2 * A_GN), lambda b, c: (b * nc + c, 2)),
                  pl.BlockSpec((q, LANE), row),
                  pl.BlockSpec((None, A_CONV - 1, A_CONV_CH), lambda b, c: (b, 0, 0)),
                  pl.BlockSpec((None, hp, A_D_STATE), lambda b, c: (b, 0, 0)),
                  full((A_CONV, A_CONV_CH)), full((1, A_CONV_CH)), full((1, LANE)), full((1, LANE)),
                  full((1, A_D_INNER)), full((1, A_D_INNER)), full((LANE, A_D_INNER))],
        out_specs=[pl.BlockSpec((q, A_D_INNER), row),
                   pl.BlockSpec((None, A_CONV - 1, A_CONV_CH), lambda b, c: (b, 0, 0)),
                   pl.BlockSpec((None, hp, A_D_STATE), lambda b, c: (b, 0, 0))],
        out_shape=[jax.ShapeDtypeStruct((t, A_D_INNER), BF16),
                   jax.ShapeDtypeStruct((nb, A_CONV - 1, A_CONV_CH), F32),
                   jax.ShapeDtypeStruct((nb, hp, A_D_STATE), F32)],
        scratch_shapes=[pltpu.VMEM((8, A_CONV_CH), F32), pltpu.VMEM((q, A_CONV_CH), F32)],
        compiler_params=_cparams(("arbitrary", "arbitrary")),
        name=name,
    )(proj, proj, proj, dt_raw, conv_prev, ssm_prev, conv_w, conv_b, dt_bias, a_log, d_x, norm_g, _head_expand())


def _ssd_step_body(z_ref, xs_ref, bc_ref, dt_ref, cprev_ref, sprev_ref, cw_ref, cb_ref, dtb_ref, alog_ref,
                   dx_ref, ng_ref, exp_ref, yn_ref, cnew_ref, snew_ref, da_ref, y_ref):
    bt = z_ref.shape[0]
    cur = jnp.concatenate([xs_ref[...], bc_ref[...]], axis=1).astype(F32)
    conv = cb_ref[...] + cur * cw_ref[A_CONV - 1:A_CONV, :]
    for k in range(A_CONV - 1):
        conv = conv + cprev_ref[:, k, :] * cw_ref[k:k + 1, :]
    for k in range(A_CONV - 2):
        cnew_ref[:, k, :] = cprev_ref[:, k + 1, :]
    cnew_ref[:, A_CONV - 2, :] = cur
    act = _silu(conv)
    xs = act[:, 0:A_D_INNER]
    bm_bf = act[:, A_D_INNER:A_D_INNER + A_GN].astype(BF16)
    cm_bf = act[:, A_D_INNER + A_GN:A_CONV_CH].astype(BF16)
    dt = _softplus(dt_ref[...] + dtb_ref[...])
    da_ref[...] = jnp.exp(dt * (-jnp.exp(alog_ref[...])))
    dt_hi, dt_lo = _split2(dt)
    dt_x = _dot(dt_hi, exp_ref[...]) + _dot(dt_lo, exp_ref[...])
    xdt = xs * dt_x
    rows = lax.broadcasted_iota(I32, (bt, 1), 0)
    y_ref[...] = jnp.zeros_like(y_ref)

    def per_seq(j, carry):
        mine = rows == j
        xdt_j = jnp.where(mine, xdt, 0.0)
        da_j = da_ref[pl.ds(j, 1), :]
        for g in range(A_N_GROUPS):
            gsl = slice(g * A_GW, (g + 1) * A_GW)
            nsl = slice(g * A_D_STATE, (g + 1) * A_D_STATE)
            outer = _dot_tn(xdt_j[:, gsl].astype(BF16), bm_bf[:, nsl])
            for r in range(A_HPG):
                h = g * A_HPG + r
                rsl = slice(r * A_HEAD_DIM, (r + 1) * A_HEAD_DIM)
                hsl = slice(g * A_GW + r * A_HEAD_DIM, g * A_GW + (r + 1) * A_HEAD_DIM)
                snew_ref[j, hsl, :] = sprev_ref[j, hsl, :] * da_j[:, h:h + 1] + outer[rsl, :]
            yg = _dot_nt(cm_bf[:, nsl], snew_ref[j, gsl, :].astype(BF16))
            y_ref[:, gsl] = y_ref[:, gsl] + jnp.where(mine, yg, 0.0)
        return carry

    lax.fori_loop(0, bt, per_seq, 0)
    y = y_ref[...] + dx_ref[...] * xs
    z = z_ref[...].astype(F32)
    for g in range(A_N_GROUPS):
        gsl = slice(g * A_GW, (g + 1) * A_GW)
        yn_ref[:, gsl] = _gate_norm(y[:, gsl], z[:, gsl], ng_ref[:, gsl]).astype(BF16)


_SSD_STEP_INPUTS = 13


def _ssd_step_aliased_body(*refs):
    _ssd_step_body(*refs[:_SSD_STEP_INPUTS], *refs[_SSD_STEP_INPUTS + 1:])


def _ssd_step_call(proj, dt_raw, conv_prev, ssm_all, slot, ssm_out, conv_w, conv_b, dt_bias, a_log, d_x, norm_g,
                   name):
    nb = proj.shape[0]
    bt = 8
    hp = A_N_HEADS * A_HEAD_DIM
    full = lambda shape: pl.BlockSpec(shape, lambda i: (0,) * len(shape))
    state_spec = pl.BlockSpec((None, bt, hp, A_D_STATE), lambda i: (slot, i, 0, 0))
    in_specs = [pl.BlockSpec((bt, A_D_INNER), lambda i: (i, 0)),
                pl.BlockSpec((bt, A_D_INNER), lambda i: (i, 1)),
                pl.BlockSpec((bt, 2 * A_GN), lambda i: (i, 2)),
                pl.BlockSpec((bt, LANE), lambda i: (i, 0)),
                pl.BlockSpec((bt, A_CONV - 1, A_CONV_CH), lambda i: (i, 0, 0)),
                state_spec,
                full((A_CONV, A_CONV_CH)), full((1, A_CONV_CH)), full((1, LANE)), full((1, LANE)),
                full((1, A_D_INNER)), full((1, A_D_INNER)), full((LANE, A_D_INNER))]
    args = [proj, proj, proj, dt_raw, conv_prev, ssm_all, conv_w, conv_b, dt_bias, a_log, d_x, norm_g, _head_expand()]
    assert len(args) == _SSD_STEP_INPUTS
    aliases = {}
    if ssm_out is not None:
        in_specs.append(pl.BlockSpec(memory_space=pl.ANY))
        args.append(ssm_out)
        aliases = {_SSD_STEP_INPUTS: 2}
    return pl.pallas_call(
        _ssd_step_body if ssm_out is None else _ssd_step_aliased_body,
        grid=(nb // bt,),
        in_specs=in_specs,
        out_specs=[pl.BlockSpec((bt, A_D_INNER), lambda i: (i, 0)),
                   pl.BlockSpec((bt, A_CONV - 1, A_CONV_CH), lambda i: (i, 0, 0)),
                   state_spec],
        out_shape=[jax.ShapeDtypeStruct((nb, A_D_INNER), BF16),
                   jax.ShapeDtypeStruct((nb, A_CONV - 1, A_CONV_CH), F32),
                   jax.ShapeDtypeStruct(ssm_all.shape, F32)],
        scratch_shapes=[pltpu.VMEM((bt, LANE), F32), pltpu.VMEM((bt, A_D_INNER), F32)],
        input_output_aliases=aliases,
        compiler_params=_cparams(("arbitrary",)),
        name=name,
    )(*args)


def _gmlp_in_body(x_ref, g_ref, sc_ref, sh_ref, w_ref, b_ref, lg_ref, lb_ref, o_ref, hn_ref):
    j = pl.program_id(1)

    @pl.when(j == 0)
    def _():
        hn_ref[...] = _prenorm(x_ref[...], g_ref[...], sc_ref[...], sh_ref[...]).astype(BF16)

    uv = _gelu_tanh(_dot(hn_ref[...], w_ref[...]) + b_ref[...])

    @pl.when(j == 0)
    def _():
        o_ref[...] = uv.astype(o_ref.dtype)

    @pl.when(j == 1)
    def _():
        vc = uv - jnp.mean(uv, axis=-1, keepdims=True)
        var = jnp.mean(vc * vc, axis=-1, keepdims=True)
        o_ref[...] = (vc * lax.rsqrt(var + EPS) * lg_ref[...] + lb_ref[...]).astype(o_ref.dtype)


def _gmlp_in_call(x, g, mod, w, b, ln_g, ln_b, tm, out_dtype, name):
    t, d = x.shape
    return pl.pallas_call(
        _gmlp_in_body,
        grid=(t // tm, 2),
        in_specs=[pl.BlockSpec((tm, d), lambda i, j: (i, 0)),
                  pl.BlockSpec((1, d), lambda i, j: (0, 0)),
                  mod.spec(1, tm, 2), mod.spec(0, tm, 2),
                  pl.BlockSpec((d, B_D), lambda i, j: (0, j)),
                  pl.BlockSpec((1, B_D), lambda i, j: (0, j)),
                  pl.BlockSpec((1, B_D), lambda i, j: (0, 0)),
                  pl.BlockSpec((1, B_D), lambda i, j: (0, 0))],
        out_specs=pl.BlockSpec((tm, B_D), lambda i, j: (i, j)),
        out_shape=jax.ShapeDtypeStruct((t, 2 * B_D), out_dtype),
        scratch_shapes=[pltpu.VMEM((tm, d), BF16)],
        compiler_params=_cparams(("arbitrary", "arbitrary")),
        name=name,
    )(x, g.reshape(1, d), mod.arr, mod.arr, w, b.reshape(1, -1), ln_g.reshape(1, -1), ln_b.reshape(1, -1))


def _gmlp_out_body(u_ref, v_ref, ws_ref, bs_ref, w_ref, x_ref, gate_ref, o_ref, wbf_ref, wsbf_ref, m_ref):
    q = B_CHUNK

    @pl.when(pl.program_id(0) == 0)
    def _():
        wbf_ref[...] = w_ref[...].astype(BF16)
        causal = lax.broadcasted_iota(I32, (q, q), 0) >= lax.broadcasted_iota(I32, (q, q), 1)
        for g in range(B_N_GROUPS):
            wsbf_ref[g] = jnp.where(causal, ws_ref[g], 0.0).astype(BF16)

    for ci in range(u_ref.shape[0] // q):
        rsl = slice(ci * q, (ci + 1) * q)
        for g in range(B_N_GROUPS):
            gsl = slice(g * B_GROUP_DIM, (g + 1) * B_GROUP_DIM)
            mixed = _dot(wsbf_ref[g], v_ref[rsl, gsl].astype(BF16)) + bs_ref[:, g:g + 1]
            m_ref[rsl, gsl] = (u_ref[rsl, gsl].astype(F32) * mixed).astype(BF16)
    o_ref[...] = x_ref[...] + gate_ref[...] * _dot(m_ref[...], wbf_ref[...])


def _gmlp_out_call(uv, w_s, b_s, w_out, x, mod, tm, name):
    t, d = x.shape
    q = B_CHUNK
    return pl.pallas_call(
        _gmlp_out_body,
        grid=(t // tm,),
        in_specs=[pl.BlockSpec((tm, B_D), lambda i: (i, 0)),
                  pl.BlockSpec((tm, B_D), lambda i: (i, 1)),
                  pl.BlockSpec((B_N_GROUPS, q, q), lambda i: (0, 0, 0)),
                  pl.BlockSpec((q, B_N_GROUPS), lambda i: (0, 0)),
                  pl.BlockSpec((B_D, d), lambda i: (0, 0)),
                  pl.BlockSpec((tm, d), lambda i: (i, 0)),
                  mod.spec(2, tm, 1)],
        out_specs=pl.BlockSpec((tm, d), lambda i: (i, 0)),
        out_shape=jax.ShapeDtypeStruct((t, d), F32),
        scratch_shapes=[pltpu.VMEM((B_D, d), BF16), pltpu.VMEM((B_N_GROUPS, q, q), BF16),
                        pltpu.VMEM((tm, B_D), BF16)],
        compiler_params=_cparams(("arbitrary",)),
        name=name,
    )(uv, uv, w_s, b_s.T, w_out, x, mod.arr)


def _gmlp_out_step_body(u_ref, v_ref, wd_ref, bd_ref, w_ref, x_ref, gate_ref, o_ref):
    mixed = v_ref[...] * wd_ref[...] + bd_ref[...]
    m = (u_ref[...] * mixed).astype(BF16)
    o_ref[...] = x_ref[...] + gate_ref[...] * _dot(m, w_ref[...].astype(BF16))


def _gmlp_out_step_call(uv, w_s, b_s, w_out, x, mod, name):
    t, d = x.shape
    wd = jnp.repeat(w_s[:, 0, 0], B_GROUP_DIM).reshape(1, B_D)
    bd = jnp.repeat(b_s[:, 0], B_GROUP_DIM).reshape(1, B_D)
    return pl.pallas_call(
        _gmlp_out_step_body,
        grid=(1,),
        in_specs=[pl.BlockSpec((t, B_D), lambda i: (0, 0)),
                  pl.BlockSpec((t, B_D), lambda i: (0, 1)),
                  pl.BlockSpec((1, B_D), lambda i: (0, 0)),
                  pl.BlockSpec((1, B_D), lambda i: (0, 0)),
                  pl.BlockSpec((B_D, d), lambda i: (0, 0)),
                  pl.BlockSpec((t, d), lambda i: (0, 0)),
                  mod.spec(2, t, 1)],
        out_specs=pl.BlockSpec((t, d), lambda i: (0, 0)),
        out_shape=jax.ShapeDtypeStruct((t, d), F32),
        compiler_params=_cparams(("arbitrary",)),
        name=name,
    )(uv, uv, wd, bd, w_out, x, mod.arr)


def _pool_matmul(pooled, wg_ref):
    outs = []
    for gi in range(len(C_WINDOWS)):
        gsl = slice(gi * C_GROUP_DIM, (gi + 1) * C_GROUP_DIM)
        outs.append(_dot(pooled[:, gsl].astype(BF16), wg_ref[gi].astype(BF16)))
    return jnp.concatenate(outs, axis=-1)


def _pool_body(x_ref, g_ref, sc_ref, sh_ref, gate_ref, prev_ref, wg_ref, scale_ref, o_ref, pnew_ref, hp_ref,
               *, tiles_per_seq, start):
    i = pl.program_id(0)
    tm = x_ref.shape[0]
    top = 16
    ti = i % tiles_per_seq

    @pl.when(ti == 0)
    def _():
        hp_ref[top - C_STATE:top, :] = prev_ref[...]

    hn = _prenorm(x_ref[...], g_ref[...], sc_ref[...], sh_ref[...])
    hp_ref[top:top + tm, :] = hn
    pos = start + ti * tm + lax.broadcasted_iota(I32, (tm, 1), 0)
    outs = []
    for gi, w in enumerate(C_WINDOWS):
        gsl = slice(gi * C_GROUP_DIM, (gi + 1) * C_GROUP_DIM)
        acc = hn[:, gsl]
        for k in range(1, w):
            acc = acc + hp_ref[top - k:top - k + tm, gsl]
        cnt = jnp.minimum(pos + 1, w).astype(F32)
        outs.append(acc / cnt - hn[:, gsl])
    y = _pool_matmul(jnp.concatenate(outs, axis=-1), wg_ref) * scale_ref[...]
    o_ref[...] = x_ref[...] + gate_ref[...] * y
    hist = hp_ref[top + tm - C_STATE:top + tm, :]
    hp_ref[top - C_STATE:top, :] = hist

    @pl.when(ti == tiles_per_seq - 1)
    def _():
        pnew_ref[...] = hist


def _pool_call(x, g, mod, prev, w_g, scale, nb, tm, start, name):
    t, d = x.shape
    tiles = t // nb // tm
    ng = len(C_WINDOWS)
    return pl.pallas_call(
        functools.partial(_pool_body, tiles_per_seq=tiles, start=start),
        grid=(t // tm,),
        in_specs=[pl.BlockSpec((tm, d), lambda i: (i, 0)),
                  pl.BlockSpec((1, d), lambda i: (0, 0)),
                  mod.spec(1, tm, 1), mod.spec(0, tm, 1), mod.spec(2, tm, 1),
                  pl.BlockSpec((None, C_STATE, d), lambda i: (i // tiles, 0, 0)),
                  pl.BlockSpec((ng, C_GROUP_DIM, C_GROUP_DIM), lambda i: (0, 0, 0)),
                  pl.BlockSpec((1, d), lambda i: (0, 0))],
        out_specs=[pl.BlockSpec((tm, d), lambda i: (i, 0)),
                   pl.BlockSpec((None, C_STATE, d), lambda i: (i // tiles, 0, 0))],
        out_shape=[jax.ShapeDtypeStruct((t, d), F32), jax.ShapeDtypeStruct((nb, C_STATE, d), F32)],
        scratch_shapes=[pltpu.VMEM((16 + tm, d), F32)],
        compiler_params=_cparams(("arbitrary",)),
        name=name,
    )(x, g.reshape(1, d), mod.arr, mod.arr, mod.arr, prev, w_g, scale.reshape(1, d))


def _pool_step_body(x_ref, g_ref, sc_ref, sh_ref, gate_ref, prev_ref, wg_ref, scale_ref, o_ref, pnew_ref, *, start):
    hn = _prenorm(x_ref[...], g_ref[...], sc_ref[...], sh_ref[...])
    outs = []
    for gi, w in enumerate(C_WINDOWS):
        gsl = slice(gi * C_GROUP_DIM, (gi + 1) * C_GROUP_DIM)
        acc = hn[:, gsl]
        for k in range(1, w):
            acc = acc + prev_ref[:, C_STATE - k, gsl]
        outs.append(acc / float(min(start + 1, w)) - hn[:, gsl])
    y = _pool_matmul(jnp.concatenate(outs, axis=-1), wg_ref) * scale_ref[...]
    o_ref[...] = x_ref[...] + gate_ref[...] * y
    for k in range(C_STATE - 1):
        pnew_ref[:, k, :] = prev_ref[:, k + 1, :]
    pnew_ref[:, C_STATE - 1, :] = hn


def _pool_step_call(x, g, mod, prev, w_g, scale, start, name):
    t, d = x.shape
    bt = 32
    ng = len(C_WINDOWS)
    return pl.pallas_call(
        functools.partial(_pool_step_body, start=start),
        grid=(t // bt,),
        in_specs=[pl.BlockSpec((bt, d), lambda i: (i, 0)),
                  pl.BlockSpec((1, d), lambda i: (0, 0)),
                  mod.spec(1, bt, 1), mod.spec(0, bt, 1), mod.spec(2, bt, 1),
                  pl.BlockSpec((bt, C_STATE, d), lambda i: (i, 0, 0)),
                  pl.BlockSpec((ng, C_GROUP_DIM, C_GROUP_DIM), lambda i: (0, 0, 0)),
                  pl.BlockSpec((1, d), lambda i: (0, 0))],
        out_specs=[pl.BlockSpec((bt, d), lambda i: (i, 0)),
                   pl.BlockSpec((bt, C_STATE, d), lambda i: (i, 0, 0))],
        out_shape=[jax.ShapeDtypeStruct((t, d), F32), jax.ShapeDtypeStruct((t, C_STATE, d), F32)],
        compiler_params=_cparams(("arbitrary",)),
        name=name,
    )(x, g.reshape(1, d), mod.arr, mod.arr, mod.arr, prev, w_g, scale.reshape(1, d))


def _route_rows(s, b):
    npg = EXPERTS_PER_GROUP
    gscore = []
    for q in range(N_EXPERT_GROUPS):
        v = b[q * npg:(q + 1) * npg]
        best = None
        for i in range(npg):
            for j in range(i + 1, npg):
                best = v[i] + v[j] if best is None else jnp.maximum(best, v[i] + v[j])
        gscore.append(best)
    gsel = jnp.zeros_like(gscore[0], dtype=I32)
    gbest = gscore[0]
    for q in range(1, N_EXPERT_GROUPS):
        better = gscore[q] > gbest
        gsel = jnp.where(better, q, gsel)
        gbest = jnp.where(better, gscore[q], gbest)
    vb, vs = [], []
    for k in range(npg):
        bk, sk = b[k], s[k]
        for q in range(1, N_EXPERT_GROUPS):
            bk = jnp.where(gsel == q, b[q * npg + k], bk)
            sk = jnp.where(gsel == q, s[q * npg + k], sk)
        vb.append(bk)
        vs.append(sk)
    i1 = jnp.zeros_like(gsel)
    m1 = vb[0]
    for k in range(1, npg):
        better = vb[k] > m1
        i1 = jnp.where(better, k, i1)
        m1 = jnp.where(better, vb[k], m1)
    i2 = jnp.full_like(gsel, -1)
    m2 = jnp.zeros_like(m1)
    for k in range(npg):
        better = (i1 != k) & ((i2 < 0) | (vb[k] > m2))
        i2 = jnp.where(better, k, i2)
        m2 = jnp.where(better, vb[k], m2)
    s1 = vs[0]
    s2 = vs[0]
    for k in range(1, npg):
        s1 = jnp.where(i1 == k, vs[k], s1)
        s2 = jnp.where(i2 == k, vs[k], s2)
    w1 = s1 / (s1 + s2)
    w2 = s2 / (s1 + s2)
    lo = jnp.minimum(i1, i2)
    hi = jnp.maximum(i1, i2)
    pair = jnp.zeros_like(gsel)
    for p, (a, c) in enumerate(PAIRS):
        pair = jnp.where((lo == a) & (hi == c), p, pair)
    first_is_lo = i1 < i2
    return (gsel * len(PAIRS) + pair, jnp.where(first_is_lo, w1, w2), jnp.where(first_is_lo, w2, w1))


def _route_body(x_ref, g_ref, sc_ref, sh_ref, rw_ref, rb_ref, cnt_in_ref, *rest, aliased):
    if aliased:
        rest = rest[3:]
    rows_ref, bucket_ref, rank_ref, cnt_ref = rest
    tm = x_ref.shape[0]

    @pl.when(pl.program_id(0) == 0)
    def _():
        cnt_ref[...] = cnt_in_ref[...]

    hn = _prenorm(x_ref[...], g_ref[...], sc_ref[...], sh_ref[...])
    logits_t = _dot3(hn, rw_ref[...]).T
    scores = _sigmoid(logits_t[0:N_EXPERTS, :])
    biased = scores + rb_ref[0:N_EXPERTS, :]
    bucket, w_lo, w_hi = _route_rows([scores[e:e + 1, :] for e in range(N_EXPERTS)],
                                     [biased[e:e + 1, :] for e in range(N_EXPERTS)])
    bucket_ref[...] = bucket
    onehot = (lax.broadcasted_iota(I32, (BUCKET_ROWS, tm), 0) == bucket).astype(F32)
    before = (lax.broadcasted_iota(I32, (tm, tm), 0) < lax.broadcasted_iota(I32, (tm, tm), 1)).astype(BF16)
    earlier = _dot(onehot.astype(BF16), before) + cnt_ref[:, 0:1]
    rank_ref[...] = jnp.sum(onehot * earlier, axis=0, keepdims=True).astype(I32)
    cnt_ref[...] = cnt_ref[...] + jnp.sum(onehot, axis=1, keepdims=True)

    sub = lax.broadcasted_iota(I32, (LANE, tm), 0)
    gate_t = jnp.where(sub == 0, w_lo, jnp.where(sub == 1, w_hi, 0.0))
    rows_ref[pl.ds(GATE_ROW, tm, stride=ROW_PITCH), :] = gate_t.T
    for j in range(ROW_PLANES):
        rows_ref[pl.ds(j, tm, stride=ROW_PITCH), :] = hn[:, j * LANE:(j + 1) * LANE]


def _route_call(x, g, mod, router_w, router_b, counts, tm, total_rows, row_offset, prior, name):
    t, d = x.shape
    blk0 = row_offset // tm
    rw = jnp.pad(router_w, ((0, 0), (0, LANE - N_EXPERTS)))
    rb = jnp.pad(router_b.reshape(-1, 1), ((0, LANE - N_EXPERTS), (0, 0)))
    in_specs = [pl.BlockSpec((tm, d), lambda i: (i, 0)),
                pl.BlockSpec((1, d), lambda i: (0, 0)),
                mod.spec(4, tm, 1), mod.spec(3, tm, 1),
                pl.BlockSpec((d, LANE), lambda i: (0, 0)),
                pl.BlockSpec((LANE, 1), lambda i: (0, 0)),
                pl.BlockSpec((BUCKET_ROWS, LANE), lambda i: (0, 0))]
    args = [x, g.reshape(1, d), mod.arr, mod.arr, rw, rb, counts]
    aliases = {}
    if prior is not None:
        in_specs += [pl.BlockSpec(memory_space=pl.ANY)] * len(prior)
        aliases = {len(args) + k: k for k in range(len(prior))}
        args += list(prior)
    return pl.pallas_call(
        functools.partial(_route_body, aliased=prior is not None),
        grid=(t // tm,),
        in_specs=in_specs,
        out_specs=[pl.BlockSpec((tm * ROW_PITCH, LANE), lambda i: (blk0 + i, 0)),
                   pl.BlockSpec((1, tm), lambda i: (0, blk0 + i)),
                   pl.BlockSpec((1, tm), lambda i: (0, blk0 + i)),
                   pl.BlockSpec((BUCKET_ROWS, LANE), lambda i: (0, 0))],
        out_shape=[jax.ShapeDtypeStruct((total_rows * ROW_PITCH, LANE), F32),
                   jax.ShapeDtypeStruct((1, total_rows), I32),
                   jax.ShapeDtypeStruct((1, total_rows), I32),
                   jax.ShapeDtypeStruct((BUCKET_ROWS, LANE), F32)],
        input_output_aliases=aliases,
        compiler_params=_cparams(("arbitrary",)),
        name=name,
    )(*args)


def _invert_body(dest_ref, init_ref, gather_ref):
    pltpu.sync_copy(init_ref, gather_ref)

    def put(t, carry):
        gather_ref[dest_ref[t]] = t
        return carry

    lax.fori_loop(0, dest_ref.shape[0], put, 0, unroll=8)


def _invert_call(dest, n_slots, name):
    smem = pl.BlockSpec(memory_space=pltpu.SMEM)
    return pl.pallas_call(
        _invert_body,
        in_specs=[smem, pl.BlockSpec(memory_space=pl.ANY)],
        out_specs=smem,
        out_shape=jax.ShapeDtypeStruct((n_slots,), I32),
        name=name,
    )(dest, jnp.zeros((n_slots,), I32))


def _unrolled(lo, hi, body, carry):
    for b in range(lo, hi):
        carry = body(b, carry)
    return carry


def _experts_body(grp_ref, lo_ref, hi_ref, fresh_ref, valid_ref, out_ref, gather_ref, rows_ref, wg_in_ref,
                  wu_in_ref, wd_in_ref, y_ref, wg_ref, wu_ref, wd_ref, xb_ref, gsem):
    i = pl.program_id(0)
    n = pl.num_programs(0)
    tile = EXPERT_TILE
    copied = GATE_ROW + 1
    group = 8
    del grp_ref, out_ref

    def start_gather(step, slot, loop):
        def body(b, carry):
            for k in range(group):
                r = b * group + k
                tok = gather_ref[step * tile + r]
                pltpu.make_async_copy(rows_ref.at[pl.ds(tok * ROW_PITCH, copied)],
                                      xb_ref.at[slot, pl.ds(r * ROW_PITCH, copied)],
                                      gsem.at[slot]).start(priority=k % 2)
            return carry

        loop(0, tile // group, body, 0)

    def wait_gather(slot):
        rows = tile * copied
        pltpu.make_async_copy(rows_ref.at[pl.ds(0, rows)], xb_ref.at[slot, pl.ds(0, rows)], gsem.at[slot]).wait()

    @pl.when(fresh_ref[i] == 1)
    def _():
        for e in range(EXPERTS_PER_GROUP):
            wg_ref[e] = wg_in_ref[e].astype(BF16)
            wu_ref[e] = wu_in_ref[e].astype(BF16)
            wd_ref[e] = wd_in_ref[e].astype(BF16)

    nxt = jnp.minimum(i + 1, n - 1)
    more = jnp.logical_and(i + 1 < n, valid_ref[nxt] == 1)

    @pl.when(valid_ref[i] == 1)
    def _():
        slot = i % GATHER_SLOTS

        @pl.when(i == 0)
        def _():
            for ahead in range(GATHER_SLOTS - 1):
                start_gather(ahead, ahead, lax.fori_loop)

        wait_gather(slot)
        ahead = i + GATHER_SLOTS - 1
        start_gather(jnp.minimum(ahead, n - 1), ahead % GATHER_SLOTS, _unrolled)
        x = jnp.concatenate([xb_ref[slot, pl.ds(j, tile, stride=ROW_PITCH), :].astype(BF16)
                             for j in range(ROW_PLANES)], axis=-1)

        def expert(e):
            act = (_silu(_dot(x, wg_ref[e])) * _dot(x, wu_ref[e])).astype(BF16)
            return _dot(act, wd_ref[e])

        w = xb_ref[slot, pl.ds(GATE_ROW, tile, stride=ROW_PITCH), :]
        y = w[:, 0:1] * expert(lo_ref[i]) + w[:, 1:2] * expert(hi_ref[i])
        for j in range(ROW_PLANES):
            y_ref[pl.ds(j, tile, stride=ROW_PITCH), :] = y[:, j * LANE:(j + 1) * LANE]

        @pl.when(jnp.logical_not(more))
        def _():
            for k in range(1, GATHER_SLOTS):
                wait_gather((i + k) % GATHER_SLOTS)


def _experts_call(grp, lo, hi, fresh, valid, out_tile, gather_idx, rows, w_gate, w_up, w_down, layer, name):
    n_tiles = grp.shape[0]
    tile = EXPERT_TILE
    d, f = D_MODEL, D_EXPERT
    npg = EXPERTS_PER_GROUP
    group_block = lambda i, grp, *_: (layer, grp[i], 0, 0)
    once = pl.Buffered(1)
    return pl.pallas_call(
        _experts_body,
        grid_spec=pltpu.PrefetchScalarGridSpec(
            num_scalar_prefetch=7,
            grid=(n_tiles,),
            in_specs=[pl.BlockSpec(memory_space=pl.ANY),
                      pl.BlockSpec((None, npg, d, f), group_block, pipeline_mode=once),
                      pl.BlockSpec((None, npg, d, f), group_block, pipeline_mode=once),
                      pl.BlockSpec((None, npg, f, d), group_block, pipeline_mode=once)],
            out_specs=pl.BlockSpec((tile * ROW_PITCH, LANE), lambda i, g, l, h, fr, va, out, *_: (out[i], 0)),
            scratch_shapes=[pltpu.VMEM((npg, d, f), BF16), pltpu.VMEM((npg, d, f), BF16),
                            pltpu.VMEM((npg, f, d), BF16),
                            pltpu.VMEM((GATHER_SLOTS, tile * ROW_PITCH, LANE), F32),
                            pltpu.SemaphoreType.DMA((GATHER_SLOTS,))]),
        out_shape=jax.ShapeDtypeStruct((n_tiles * tile * ROW_PITCH, LANE), F32),
        compiler_params=_cparams(("arbitrary",)),
        name=name,
    )(grp, lo, hi, fresh, valid, out_tile, gather_idx, rows, w_gate, w_up, w_down)


def _moe_res_body(dest_ref, y_ref, x_ref, gate_ref, fg_ref, o_ref, *rest, row_offset, final):
    if final:
        on_ref, buf_ref, sem = rest
    else:
        buf_ref, sem = rest
    i = pl.program_id(0)
    tm = x_ref.shape[0]
    group = 8

    def start_gather(step, slot, loop):
        base = row_offset + step * tm

        def body(b, carry):
            for k in range(group):
                r = b * group + k
                pltpu.make_async_copy(y_ref.at[pl.ds(dest_ref[base + r] * ROW_PITCH, ROW_PLANES)],
                                      buf_ref.at[slot, pl.ds(r * ROW_PITCH, ROW_PLANES)],
                                      sem.at[slot]).start(priority=k % 2)
            return carry

        loop(0, tm // group, body, 0)

    def wait_gather(slot):
        rows = tm * ROW_PLANES
        pltpu.make_async_copy(y_ref.at[pl.ds(0, rows)], buf_ref.at[slot, pl.ds(0, rows)], sem.at[slot]).wait()

    @pl.when(i == 0)
    def _():
        start_gather(0, 0, lax.fori_loop)

    slot = i % 2
    last = pl.num_programs(0) - 1
    wait_gather(slot)
    start_gather(jnp.minimum(i + 1, last), 1 - slot, lax.fori_loop)
    y = jnp.concatenate([buf_ref[slot, pl.ds(j, tm, stride=ROW_PITCH), :] for j in range(ROW_PLANES)], axis=-1)
    xn = x_ref[...] + gate_ref[...] * y
    o_ref[...] = xn
    if final:
        ms = jnp.mean(xn * xn, axis=-1, keepdims=True)
        on_ref[...] = xn * lax.rsqrt(ms + EPS) * fg_ref[...]

    @pl.when(i == last)
    def _():
        wait_gather(1 - slot)


def _moe_res_call(dest, y_sorted, x, mod, final_g, tm, row_offset, final, name):
    t, d = x.shape
    n_out = 2 if final else 1
    gate_spec = mod.spec(5, tm, 1)
    gate_map = gate_spec.index_map
    return pl.pallas_call(
        functools.partial(_moe_res_body, row_offset=row_offset, final=final),
        grid_spec=pltpu.PrefetchScalarGridSpec(
            num_scalar_prefetch=1,
            grid=(t // tm,),
            in_specs=[pl.BlockSpec(memory_space=pl.ANY),
                      pl.BlockSpec((tm, d), lambda i, dest: (i, 0)),
                      pl.BlockSpec(gate_spec.block_shape, lambda i, dest: gate_map(i)),
                      pl.BlockSpec((1, d), lambda i, dest: (0, 0))],
            out_specs=[pl.BlockSpec((tm, d), lambda i, dest: (i, 0))] * n_out,
            scratch_shapes=[pltpu.VMEM((2, tm * ROW_PITCH, LANE), F32), pltpu.SemaphoreType.DMA((2,))]),
        out_shape=[jax.ShapeDtypeStruct((t, d), F32)] * n_out,
        compiler_params=_cparams(("arbitrary",)),
        name=name,
    )(dest, y_sorted, x, mod.arr, final_g.reshape(1, d))


def _tile_tables(counts, n_tiles):
    tile = EXPERT_TILE
    tiles_per_bucket = (counts + tile - 1) // tile
    ends = jnp.cumsum(tiles_per_bucket)
    starts = ends - tiles_per_bucket
    used = ends[-1]
    ti = jnp.arange(n_tiles, dtype=I32)
    valid = (ti < used).astype(I32)
    tile_bucket = jnp.sum((jnp.minimum(ti, used - 1)[:, None] >= ends[None, :]).astype(I32), axis=1)
    pair_lo = jnp.asarray([p[0] for p in PAIRS], I32)
    pair_hi = jnp.asarray([p[1] for p in PAIRS], I32)
    grp = tile_bucket // len(PAIRS)
    lo = pair_lo[tile_bucket % len(PAIRS)]
    hi = pair_hi[tile_bucket % len(PAIRS)]
    fresh = jnp.concatenate([jnp.ones((1,), I32), (grp[1:] != grp[:-1]).astype(I32)])
    return grp, lo, hi, fresh, valid, jnp.minimum(ti, used - 1).astype(I32), (starts * tile).astype(I32)


def _moe_layer(xp, xs, norm_g2, mod_p, mod_s, router_w, router_b, w_gate, w_up, w_down, layer, final_g, final):
    tp, ts = xp.shape[0], xs.shape[0]
    total = tp + ts
    n_tiles = -(-total // EXPERT_TILE) + N_BUCKETS
    zero_counts = jnp.zeros((BUCKET_ROWS, LANE), F32)
    rows, bucket, rank, counts = _route_call(xp, norm_g2, mod_p, router_w, router_b, zero_counts, 512, total, 0,
                                             None, "route_prompt")
    rows, bucket, rank, counts = _route_call(xs, norm_g2, mod_s, router_w, router_b, counts, ts, total, tp,
                                             (rows, bucket, rank), "route_sample")
    grp, lo, hi, fresh, valid, out_tile, bucket_start = _tile_tables(counts[:N_BUCKETS, 0].astype(I32), n_tiles)
    dest = bucket_start[bucket[0]] + rank[0]
    gather_idx = _invert_call(dest, n_tiles * EXPERT_TILE, "invert")
    y_sorted = _experts_call(grp, lo, hi, fresh, valid, out_tile, gather_idx, rows, w_gate, w_up, w_down, layer,
                             "experts")
    outp = _moe_res_call(dest, y_sorted, xp, mod_p, final_g, 512, 0, final, "moe_res_prompt")
    outs = _moe_res_call(dest, y_sorted, xs, mod_s, final_g, ts, tp, final, "moe_res_sample")
    return outp, outs


def _mamba_layer(x, g, mod, conv_prev, ssm, w_zx, w_dt, conv_w, conv_b, dt_bias, a_log, d_skip, norm_g, w_out,
                 tm, tag):
    pad_h = lambda v: jnp.pad(v.reshape(1, -1), ((0, 0), (0, LANE - A_N_HEADS)))
    tm_in = min(2 * tm, x.shape[0])
    proj = _norm_mm_call(x, g, mod, 1, 0, w_zx, tm_in, 512, BF16, "a_in_" + tag)
    dt_raw = _norm_mm3_call(x, g, mod, 1, 0, w_dt, tm, "a_dt_" + tag)
    d_x = jnp.repeat(d_skip, A_HEAD_DIM).reshape(1, A_D_INNER)
    weights = (conv_w, conv_b.reshape(1, -1), pad_h(dt_bias), pad_h(a_log), d_x, norm_g.reshape(1, -1))
    if ssm[0] == "step":
        yn, conv_new, ssm_new = _ssd_step_call(proj, dt_raw, conv_prev, ssm[1], ssm[2], ssm[3], *weights,
                                               "ssd_step_" + tag)
    else:
        yn, conv_new, ssm_new = _ssd_call(proj, dt_raw, conv_prev, ssm[2], *weights, ssm[1], "ssd_" + tag)
    x = _out_res_call(yn, w_out, x, mod, 2, tm, "a_out_" + tag)
    return x, conv_new, ssm_new


def kernel(x_prompt, x_sample, c_prompt, c_sample, state_a_conv, state_a_ssm, state_c_pool, w_mod, b_mod, norm_g, final_g, a_w_in, a_conv_w, a_conv_b, a_dt_bias, a_log, a_d, a_norm_g, a_w_out, b_w_in, b_b_in, b_ln_g, b_ln_b, b_w_s, b_b_s, b_w_out, c_w_g, c_scale, router_w, router_b, e_w_gate, e_w_up, e_w_down):
    bp, seq, d = x_prompt.shape
    bs = x_sample.shape[0]
    n_a, n_c = state_a_conv.shape[0], state_c_pool.shape[0]
    mod_all = _mod_call(jnp.concatenate([c_prompt, c_sample], axis=0), w_mod, b_mod)
    mod_p_arr = mod_all[:, :bp].reshape(DEPTH, bp, 6, 1, d)
    mod_s_arr = mod_all[:, bp:]
    xp = x_prompt.reshape(bp * seq, d)
    xs = x_sample.reshape(bs, d)
    conv_p, ssm_p, pool_p, conv_s, pool_s, v_s = [], [], [], [], [], []
    hp = A_N_HEADS * A_HEAD_DIM
    ssm_s_in = state_a_ssm.reshape(n_a, bs, hp, A_D_STATE)
    ssm_s_out = None
    yp = ys = None
    for i in range(DEPTH):
        kind, s = LAYER_KIND[i], LAYER_SLOT[i]
        mod_p = Mod(mod_p_arr, i, False, seq)
        mod_s = Mod(mod_s_arr, i, True)
        g1 = norm_g[i, 0]
        if kind == 0:
            w_zx = a_w_in[s, :, :A_ZX].astype(BF16)
            w_dt = jnp.pad(a_w_in[s, :, A_ZX:], ((0, 0), (0, LANE - A_N_HEADS)))
            weights = (w_zx, w_dt, a_conv_w[s], a_conv_b[s], a_dt_bias[s], a_log[s], a_d[s], a_norm_g[s], a_w_out[s])
            conv0 = jnp.zeros((bp, A_CONV - 1, A_CONV_CH), F32)
            ssm0 = jnp.zeros((bp, hp, A_D_STATE), F32)
            xp, cv, ss = _mamba_layer(xp, g1, mod_p, conv0, ("prompt", bp, ssm0), *weights, 1024, "p%d" % i)
            conv_p.append(cv)
            ssm_p.append(ss.reshape(bp, A_N_HEADS, A_HEAD_DIM, A_D_STATE))
            xs, cv, ssm_s_out = _mamba_layer(xs, g1, mod_s, state_a_conv[s], ("step", ssm_s_in, s, ssm_s_out),
                                             *weights, bs, "s%d" % i)
            conv_s.append(cv)
        elif kind == 1:
            w_uv = b_w_in[s].astype(BF16)
            uv = _gmlp_in_call(xp, g1, mod_p, w_uv, b_b_in[s], b_ln_g[s], b_ln_b[s], 512, BF16, "b_in_p%d" % i)
            xp = _gmlp_out_call(uv, b_w_s[s], b_b_s[s], b_w_out[s], xp, mod_p, 512, "b_out_p%d" % i)
            uv = _gmlp_in_call(xs, g1, mod_s, w_uv, b_b_in[s], b_ln_g[s], b_ln_b[s], bs, F32, "b_in_s%d" % i)
            xs = _gmlp_out_step_call(uv, b_w_s[s], b_b_s[s], b_w_out[s], xs, mod_s, "b_out_s%d" % i)
            v_s.append(uv[:, B_D:].reshape(bs, 1, B_D))
        else:
            pool0 = jnp.zeros((bp, C_STATE, d), F32)
            xp, pr = _pool_call(xp, g1, mod_p, pool0, c_w_g[s], c_scale[s], bp, 512, 0, "pool_p%d" % i)
            pool_p.append(pr)
            xs, pr = _pool_step_call(xs, g1, mod_s, state_c_pool[s], c_w_g[s], c_scale[s], PAST_LEN, "pool_s%d" % i)
            pool_s.append(pr)
        final = i == DEPTH - 1
        outp, outs = _moe_layer(xp, xs, norm_g[i, 1], mod_p, mod_s, router_w, router_b, e_w_gate, e_w_up, e_w_down, i,
                                final_g, final)
        xp, xs = outp[0], outs[0]
        if final:
            yp, ys = outp[1], outs[1]
    return (yp.reshape(bp, seq, d), ys.reshape(bs, 1, d), jnp.stack(conv_p), jnp.stack(ssm_p), jnp.stack(pool_p),
            jnp.stack(conv_s), ssm_s_out.reshape(state_a_ssm.shape), jnp.stack(pool_s), jnp.stack(v_s))
```

```python
import functools
import math

import numpy as np
import jax
import jax.numpy as jnp
from jax import lax
from jax.experimental import pallas as pl
from jax.experimental.pallas import tpu as pltpu

F32 = jnp.float32
BF16 = jnp.bfloat16
I32 = jnp.int32
EPS = 1e-6

LANE = 128
D_MODEL = 1024
DEPTH = 4
PAST_LEN = 16384
LAYER_KIND = (0, 1, 2, 0)
LAYER_SLOT = (0, 0, 0, 1)
A_D_INNER = 2 * D_MODEL
A_HEAD_DIM = 64
A_N_HEADS = A_D_INNER // A_HEAD_DIM
A_N_GROUPS = 8
A_HPG = A_N_HEADS // A_N_GROUPS
A_D_STATE = 128
A_GN = A_N_GROUPS * A_D_STATE
A_CONV = 4
A_CONV_CH = A_D_INNER + 2 * A_GN
A_ZX = A_D_INNER + A_CONV_CH
A_CHUNK = 128
A_GW = A_HPG * A_HEAD_DIM
B_D = 2 * D_MODEL
B_N_GROUPS = 8
B_GROUP_DIM = B_D // B_N_GROUPS
B_CHUNK = 128
C_WINDOWS = (2, 4, 8, 16)
C_GROUP_DIM = D_MODEL // len(C_WINDOWS)
C_STATE = max(C_WINDOWS) - 1
N_EXPERTS = 16
N_EXPERT_GROUPS = 4
EXPERTS_PER_GROUP = 4
D_EXPERT = D_MODEL // 2
PAIRS = ((0, 1), (0, 2), (0, 3), (1, 2), (1, 3), (2, 3))
N_BUCKETS = N_EXPERT_GROUPS * len(PAIRS)
BUCKET_ROWS = 32
ROW_PLANES = D_MODEL // LANE
ROW_PITCH = ROW_PLANES + 4
GATE_ROW = ROW_PLANES
EXPERT_TILE = 512
GATHER_SLOTS = 3
VMEM_LIMIT = 56 * 1024 * 1024


def _cparams(sem, vmem=VMEM_LIMIT):
    return pltpu.CompilerParams(dimension_semantics=sem, vmem_limit_bytes=vmem)


def _sigmoid(x):
    return 1.0 / (1.0 + jnp.exp(-x))


def _silu(x):
    return x * _sigmoid(x)


def _gelu_tanh(x):
    c = 2.0 * math.sqrt(2.0 / math.pi)
    return x / (1.0 + jnp.exp(x * (-c - (c * 0.044715) * (x * x))))


def _softplus(x):
    return jnp.maximum(x, 0.0) + jnp.log1p(jnp.exp(-jnp.abs(x)))


def _split2(a):
    hi = a.astype(BF16)
    lo = (a - hi.astype(F32)).astype(BF16)
    return hi, lo


def _split3(a):
    hi = a.astype(BF16)
    r = a - hi.astype(F32)
    mid = r.astype(BF16)
    lo = (r - mid.astype(F32)).astype(BF16)
    return hi, mid, lo


def _dot(a, b):
    return jnp.dot(a, b, preferred_element_type=F32)


def _dot_nt(a, b):
    return lax.dot_general(a, b, (((1,), (1,)), ((), ())), preferred_element_type=F32)


def _dot_tn(a, b):
    return lax.dot_general(a, b, (((0,), (0,)), ((), ())), preferred_element_type=F32)


def _dot3(a, b):
    a_hi, a_lo = _split2(a)
    b_hi, b_lo = _split2(b)
    return _dot(a_hi, b_hi) + (_dot(a_lo, b_hi) + _dot(a_hi, b_lo))


def _prenorm(x, g, sc, sh):
    ms = jnp.mean(x * x, axis=-1, keepdims=True)
    return (x * lax.rsqrt(ms + EPS) * g) * (1.0 + sc) + sh


def _mod_body(c_ref, w_ref, b_ref, o_ref):
    o_ref[...] = _dot3(_silu(c_ref[...]), w_ref[...]) + b_ref[...]


def _mod_call(c_all, w_mod, b_mod):
    nb, d = c_all.shape
    depth, _, n = w_mod.shape
    tn = 1536
    return pl.pallas_call(
        _mod_body,
        grid=(depth, n // tn),
        in_specs=[pl.BlockSpec((nb, d), lambda i, j: (0, 0)),
                  pl.BlockSpec((None, d, tn), lambda i, j: (i, 0, j)),
                  pl.BlockSpec((None, 1, tn), lambda i, j: (i, 0, j))],
        out_specs=pl.BlockSpec((None, nb, tn), lambda i, j: (i, 0, j)),
        out_shape=jax.ShapeDtypeStruct((depth, nb, n), F32),
        compiler_params=_cparams(("arbitrary", "arbitrary")),
        name="mod",
    )(c_all, w_mod, b_mod.reshape(depth, 1, n))


class Mod:
    def __init__(self, arr, layer, per_row, rows_per_seq=None):
        self.arr, self.layer, self.per_row, self.rows_per_seq = arr, layer, per_row, rows_per_seq

    def spec(self, which, tm, ngrid):
        layer = self.layer
        if self.per_row:
            if ngrid == 1:
                return pl.BlockSpec((None, tm, D_MODEL), lambda i: (layer, i, which))
            return pl.BlockSpec((None, tm, D_MODEL), lambda i, j: (layer, i, which))
        tiles = self.rows_per_seq // tm
        if ngrid == 1:
            return pl.BlockSpec((None, None, None, 1, D_MODEL), lambda i: (layer, i // tiles, which, 0, 0))
        return pl.BlockSpec((None, None, None, 1, D_MODEL), lambda i, j: (layer, i // tiles, which, 0, 0))


def _norm_mm_body(x_ref, g_ref, sc_ref, sh_ref, w_ref, o_ref, hn_ref):
    @pl.when(pl.program_id(1) == 0)
    def _():
        hn_ref[...] = _prenorm(x_ref[...], g_ref[...], sc_ref[...], sh_ref[...]).astype(BF16)

    o_ref[...] = _dot(hn_ref[...], w_ref[...]).astype(o_ref.dtype)


def _norm_mm_call(x, g, mod, which_sc, which_sh, w, tm, tn, out_dtype, name):
    t, d = x.shape
    n_cols = w.shape[1]
    return pl.pallas_call(
        _norm_mm_body,
        grid=(t // tm, n_cols // tn),
        in_specs=[pl.BlockSpec((tm, d), lambda i, j: (i, 0)),
                  pl.BlockSpec((1, d), lambda i, j: (0, 0)),
                  mod.spec(which_sc, tm, 2), mod.spec(which_sh, tm, 2),
                  pl.BlockSpec((d, tn), lambda i, j: (0, j))],
        out_specs=pl.BlockSpec((tm, tn), lambda i, j: (i, j)),
        out_shape=jax.ShapeDtypeStruct((t, n_cols), out_dtype),
        scratch_shapes=[pltpu.VMEM((tm, d), BF16)],
        compiler_params=_cparams(("arbitrary", "arbitrary")),
        name=name,
    )(x, g.reshape(1, d), mod.arr, mod.arr, w)


def _norm_mm3_body(x_ref, g_ref, sc_ref, sh_ref, w_ref, o_ref):
    hn = _prenorm(x_ref[...], g_ref[...], sc_ref[...], sh_ref[...])
    o_ref[...] = _dot3(hn, w_ref[...])


def _norm_mm3_call(x, g, mod, which_sc, which_sh, w, tm, name):
    t, d = x.shape
    n = w.shape[1]
    return pl.pallas_call(
        _norm_mm3_body,
        grid=(t // tm,),
        in_specs=[pl.BlockSpec((tm, d), lambda i: (i, 0)),
                  pl.BlockSpec((1, d), lambda i: (0, 0)),
                  mod.spec(which_sc, tm, 1), mod.spec(which_sh, tm, 1),
                  pl.BlockSpec((d, n), lambda i: (0, 0))],
        out_specs=pl.BlockSpec((tm, n), lambda i: (i, 0)),
        out_shape=jax.ShapeDtypeStruct((t, n), F32),
        compiler_params=_cparams(("arbitrary",)),
        name=name,
    )(x, g.reshape(1, d), mod.arr, mod.arr, w)


def _out_res_body(y_ref, w_ref, x_ref, gate_ref, o_ref, wbf_ref):
    @pl.when(pl.program_id(0) == 0)
    def _():
        wbf_ref[...] = w_ref[...].astype(BF16)

    o_ref[...] = x_ref[...] + gate_ref[...] * _dot(y_ref[...], wbf_ref[...])


def _out_res_call(y, w, x, mod, which_gate, tm, name):
    t, k = y.shape
    d = x.shape[1]
    return pl.pallas_call(
        _out_res_body,
        grid=(t // tm,),
        in_specs=[pl.BlockSpec((tm, k), lambda i: (i, 0)),
                  pl.BlockSpec((k, d), lambda i: (0, 0)),
                  pl.BlockSpec((tm, d), lambda i: (i, 0)),
                  mod.spec(which_gate, tm, 1)],
        out_specs=pl.BlockSpec((tm, d), lambda i: (i, 0)),
        out_shape=jax.ShapeDtypeStruct((t, d), F32),
        scratch_shapes=[pltpu.VMEM((k, d), BF16)],
        compiler_params=_cparams(("arbitrary",)),
        name=name,
    )(y, w, x, mod.arr)


def _head_expand():
    h = np.arange(LANE)[:, None]
    c = np.arange(A_D_INNER)[None, :]
    return jnp.asarray((c // A_HEAD_DIM == h).astype(np.float32), dtype=BF16)


def _gate_norm(y, z, ng):
    gated = y * _silu(z)
    ms = jnp.mean(gated * gated, axis=-1, keepdims=True)
    return gated * lax.rsqrt(ms + EPS) * ng


def _ssd_body(z_ref, xs_ref, bc_ref, dt_ref, cprev_ref, sprev_ref, cw_ref, cb_ref, dtb_ref, alog_ref,
              dx_ref, ng_ref, exp_ref, yn_ref, cnew_ref, snew_ref, tail_ref, act_ref):
    c = pl.program_id(1)
    q = A_CHUNK
    nt = A_CONV - 1

    @pl.when(c == 0)
    def _():
        tail_ref[0:nt, :] = cprev_ref[...]
        snew_ref[...] = sprev_ref[...]

    row = lax.broadcasted_iota(I32, (q, q), 0)
    col = lax.broadcasted_iota(I32, (q, q), 1)
    shifts = [jnp.where(row - col == nt - k, 1.0, 0.0).astype(BF16) for k in range(nt)]
    sub = lax.broadcasted_iota(I32, (8, 1), 0)
    cw = 512
    for j in range(A_CONV_CH // cw):
        sl = slice(j * cw, (j + 1) * cw)
        src = xs_ref if (j + 1) * cw <= A_D_INNER else bc_ref
        off = j * cw if src is xs_ref else j * cw - A_D_INNER
        x_bf = src[:, off:off + cw]
        conv = cb_ref[:, sl] + x_bf.astype(F32) * cw_ref[nt:nt + 1, sl]
        for k in range(nt):
            conv = conv + _dot(shifts[k], x_bf) * cw_ref[k:k + 1, sl]
        act_ref[:, sl] = _silu(conv)
        corr = jnp.zeros((8, cw), F32)
        for l in range(nt):
            c_l = sum(tail_ref[l + k:l + k + 1, sl] * cw_ref[k:k + 1, sl] for k in range(nt - l))
            corr = jnp.where(sub == l, c_l, corr)
        act_ref[0:8, sl] = _silu(conv[0:8, :] + corr)
        tail_ref[0:nt, sl] = x_bf[q - 8:q, :].astype(F32)[8 - nt:8, :]

    @pl.when(c == pl.num_programs(1) - 1)
    def _():
        cnew_ref[...] = tail_ref[0:nt, :]

    dt = _softplus(dt_ref[...] + dtb_ref[...])
    a = dt * (-jnp.exp(alog_ref[...]))
    row = lax.broadcasted_iota(I32, (q, q), 0)
    col = lax.broadcasted_iota(I32, (q, q), 1)
    causal = row >= col
    tril = jnp.where(causal, 1.0, 0.0).astype(BF16)
    a_hi, a_mid, a_lo = _split3(a)
    cs = _dot(tril, a_hi) + (_dot(tril, a_mid) + _dot(tril, a_lo))
    cs_t = cs.T
    dt_t = dt.T
    cs_last = cs[q - 1:q, :]
    ecs = jnp.exp(cs)
    wend = jnp.exp(cs_last - cs) * dt
    st_hi, st_lo = _split2(jnp.concatenate([ecs, wend], axis=0))
    st_x = _dot(st_hi, exp_ref[...]) + _dot(st_lo, exp_ref[...])
    lane_head = lax.broadcasted_iota(I32, (q, A_GW), 1) // A_HEAD_DIM

    for g in range(A_N_GROUPS):
        gsl = slice(g * A_GW, (g + 1) * A_GW)
        b_g = act_ref[:, A_D_INNER + g * A_D_STATE:A_D_INNER + (g + 1) * A_D_STATE].astype(BF16)
        c_g = act_ref[:, A_D_INNER + A_GN + g * A_D_STATE:A_D_INNER + A_GN + (g + 1) * A_D_STATE].astype(BF16)
        x_g = act_ref[:, gsl]
        x_bf = x_g.astype(BF16)
        h_g = snew_ref[gsl, :]
        cb = _dot_nt(c_g, b_g)
        y = jnp.zeros((q, A_GW), F32)
        for r in range(A_HPG):
            h = g * A_HPG + r
            seg = cs[:, h:h + 1] - cs_t[h:h + 1, :]
            decay = jnp.exp(jnp.where(causal, seg, -1e30))
            wm = (cb * decay * dt_t[h:h + 1, :]).astype(BF16)
            y = y + _dot(wm, jnp.where(lane_head == r, x_bf, jnp.zeros_like(x_bf)))
        y = y + st_x[0:q, gsl] * _dot_nt(c_g, h_g.astype(BF16)) + dx_ref[:, gsl] * x_g
        yn_ref[:, gsl] = _gate_norm(y, z_ref[:, gsl].astype(F32), ng_ref[:, gsl]).astype(BF16)
        s_new = _dot_tn((x_g * st_x[q:2 * q, gsl]).astype(BF16), b_g)
        for r in range(A_HPG):
            h = g * A_HPG + r
            rsl = slice(g * A_GW + r * A_HEAD_DIM, g * A_GW + (r + 1) * A_HEAD_DIM)
            keep = jnp.exp(cs[q - 1:q, h:h + 1])
            snew_ref[rsl, :] = snew_ref[rsl, :] * keep + s_new[r * A_HEAD_DIM:(r + 1) * A_HEAD_DIM, :]


def _ssd_call(proj, dt_raw, conv_prev, ssm_prev, conv_w, conv_b, dt_bias, a_log, d_x, norm_g, nb, name):
    t = proj.shape[0]
    q = A_CHUNK
    nc = t // nb // q
    hp = A_N_HEADS * A_HEAD_DIM
    row = lambda b, c: (b * nc + c, 0)
    full = lambda shape: pl.BlockSpec(shape, lambda b, c: (0,) * len(shape))
    return pl.pallas_call(
        _ssd_body,
        grid=(nb, nc),
        in_specs=[pl.BlockSpec((q, A_D_INNER), lambda b, c: (b * nc + c, 0)),
                  pl.BlockSpec((q, A_D_INNER), lambda b, c: (b * nc + c, 1)),
                  pl.BlockSpec((q, 2 * A_GN), lambda b, c: (b * nc + c, 2)),
                  pl.BlockSpec((q, LANE), row),
                  pl.BlockSpec((None, A_CONV - 1, A_CONV_CH), lambda b, c: (b, 0, 0)),
                  pl.BlockSpec((None, hp, A_D_STATE), lambda b, c: (b, 0, 0)),
                  full((A_CONV, A_CONV_CH)), full((1, A_CONV_CH)), full((1, LANE)), full((1, LANE)),
                  full((1, A_D_INNER)), full((1, A_D_INNER)), full((LANE, A_D_INNER))],
        out_specs=[pl.BlockSpec((q, A_D_INNER), row),
                   pl.BlockSpec((None, A_CONV - 1, A_CONV_CH), lambda b, c: (b, 0, 0)),
                   pl.BlockSpec((None, hp, A_D_STATE), lambda b, c: (b, 0, 0))],
        out_shape=[jax.ShapeDtypeStruct((t, A_D_INNER), BF16),
                   jax.ShapeDtypeStruct((nb, A_CONV - 1, A_CONV_CH), F32),
                   jax.ShapeDtypeStruct((nb, hp, A_D_STATE), F32)],
        scratch_shapes=[pltpu.VMEM((8, A_CONV_CH), F32), pltpu.VMEM((q, A_CONV_CH), F32)],
        compiler_params=_cparams(("arbitrary", "arbitrary")),
        name=name,
    )(proj, proj, proj, dt_raw, conv_prev, ssm_prev, conv_w, conv_b, dt_bias, a_log, d_x, norm_g, _head_expand())


def _ssd_step_body(z_ref, xs_ref, bc_ref, dt_ref, cprev_ref, sprev_ref, cw_ref, cb_ref, dtb_ref, alog_ref,
                   dx_ref, ng_ref, exp_ref, yn_ref, cnew_ref, snew_ref, da_ref, y_ref):
    bt = z_ref.shape[0]
    cur = jnp.concatenate([xs_ref[...], bc_ref[...]], axis=1).astype(F32)
    conv = cb_ref[...] + cur * cw_ref[A_CONV - 1:A_CONV, :]
    for k in range(A_CONV - 1):
        conv = conv + cprev_ref[:, k, :] * cw_ref[k:k + 1, :]
    for k in range(A_CONV - 2):
        cnew_ref[:, k, :] = cprev_ref[:, k + 1, :]
    cnew_ref[:, A_CONV - 2, :] = cur
    act = _silu(conv)
    xs = act[:, 0:A_D_INNER]
    bm_bf = act[:, A_D_INNER:A_D_INNER + A_GN].astype(BF16)
    cm_bf = act[:, A_D_INNER + A_GN:A_CONV_CH].astype(BF16)
    dt = _softplus(dt_ref[...] + dtb_ref[...])
    da_ref[...] = jnp.exp(dt * (-jnp.exp(alog_ref[...])))
    dt_hi, dt_lo = _split2(dt)
    dt_x = _dot(dt_hi, exp_ref[...]) + _dot(dt_lo, exp_ref[...])
    xdt = xs * dt_x
    rows = lax.broadcasted_iota(I32, (bt, 1), 0)
    y_ref[...] = jnp.zeros_like(y_ref)

    def per_seq(j, carry):
        mine = rows == j
        xdt_j = jnp.where(mine, xdt, 0.0)
        da_j = da_ref[pl.ds(j, 1), :]
        for g in range(A_N_GROUPS):
            gsl = slice(g * A_GW, (g + 1) * A_GW)
            nsl = slice(g * A_D_STATE, (g + 1) * A_D_STATE)
            outer = _dot_tn(xdt_j[:, gsl].astype(BF16), bm_bf[:, nsl])
            for r in range(A_HPG):
                h = g * A_HPG + r
                rsl = slice(r * A_HEAD_DIM, (r + 1) * A_HEAD_DIM)
                hsl = slice(g * A_GW + r * A_HEAD_DIM, g * A_GW + (r + 1) * A_HEAD_DIM)
                snew_ref[j, hsl, :] = sprev_ref[j, hsl, :] * da_j[:, h:h + 1] + outer[rsl, :]
            yg = _dot_nt(cm_bf[:, nsl], snew_ref[j, gsl, :].astype(BF16))
            y_ref[:, gsl] = y_ref[:, gsl] + jnp.where(mine, yg, 0.0)
        return carry

    lax.fori_loop(0, bt, per_seq, 0)
    y = y_ref[...] + dx_ref[...] * xs
    z = z_ref[...].astype(F32)
    for g in range(A_N_GROUPS):
        gsl = slice(g * A_GW, (g + 1) * A_GW)
        yn_ref[:, gsl] = _gate_norm(y[:, gsl], z[:, gsl], ng_ref[:, gsl]).astype(BF16)


_SSD_STEP_INPUTS = 13


def _ssd_step_aliased_body(*refs):
    _ssd_step_body(*refs[:_SSD_STEP_INPUTS], *refs[_SSD_STEP_INPUTS + 1:])


def _ssd_step_call(proj, dt_raw, conv_prev, ssm_all, slot, ssm_out, conv_w, conv_b, dt_bias, a_log, d_x, norm_g,
                   name):
    nb = proj.shape[0]
    bt = 8
    hp = A_N_HEADS * A_HEAD_DIM
    full = lambda shape: pl.BlockSpec(shape, lambda i: (0,) * len(shape))
    state_spec = pl.BlockSpec((None, bt, hp, A_D_STATE), lambda i: (slot, i, 0, 0))
    in_specs = [pl.BlockSpec((bt, A_D_INNER), lambda i: (i, 0)),
                pl.BlockSpec((bt, A_D_INNER), lambda i: (i, 1)),
                pl.BlockSpec((bt, 2 * A_GN), lambda i: (i, 2)),
                pl.BlockSpec((bt, LANE), lambda i: (i, 0)),
                pl.BlockSpec((bt, A_CONV - 1, A_CONV_CH), lambda i: (i, 0, 0)),
                state_spec,
                full((A_CONV, A_CONV_CH)), full((1, A_CONV_CH)), full((1, LANE)), full((1, LANE)),
                full((1, A_D_INNER)), full((1, A_D_INNER)), full((LANE, A_D_INNER))]
    args = [proj, proj, proj, dt_raw, conv_prev, ssm_all, conv_w, conv_b, dt_bias, a_log, d_x, norm_g, _head_expand()]
    assert len(args) == _SSD_STEP_INPUTS
    aliases = {}
    if ssm_out is not None:
        in_specs.append(pl.BlockSpec(memory_space=pl.ANY))
        args.append(ssm_out)
        aliases = {_SSD_STEP_INPUTS: 2}
    return pl.pallas_call(
        _ssd_step_body if ssm_out is None else _ssd_step_aliased_body,
        grid=(nb // bt,),
        in_specs=in_specs,
        out_specs=[pl.BlockSpec((bt, A_D_INNER), lambda i: (i, 0)),
                   pl.BlockSpec((bt, A_CONV - 1, A_CONV_CH), lambda i: (i, 0, 0)),
                   state_spec],
        out_shape=[jax.ShapeDtypeStruct((nb, A_D_INNER), BF16),
                   jax.ShapeDtypeStruct((nb, A_CONV - 1, A_CONV_CH), F32),
                   jax.ShapeDtypeStruct(ssm_all.shape, F32)],
        scratch_shapes=[pltpu.VMEM((bt, LANE), F32), pltpu.VMEM((bt, A_D_INNER), F32)],
        input_output_aliases=aliases,
        compiler_params=_cparams(("arbitrary",)),
        name=name,
    )(*args)


def _gmlp_in_body(x_ref, g_ref, sc_ref, sh_ref, w_ref, b_ref, lg_ref, lb_ref, o_ref, hn_ref):
    j = pl.program_id(1)

    @pl.when(j == 0)
    def _():
        hn_ref[...] = _prenorm(x_ref[...], g_ref[...], sc_ref[...], sh_ref[...]).astype(BF16)

    uv = _gelu_tanh(_dot(hn_ref[...], w_ref[...]) + b_ref[...])

    @pl.when(j == 0)
    def _():
        o_ref[...] = uv.astype(o_ref.dtype)

    @pl.when(j == 1)
    def _():
        vc = uv - jnp.mean(uv, axis=-1, keepdims=True)
        var = jnp.mean(vc * vc, axis=-1, keepdims=True)
        o_ref[...] = (vc * lax.rsqrt(var + EPS) * lg_ref[...] + lb_ref[...]).astype(o_ref.dtype)


def _gmlp_in_call(x, g, mod, w, b, ln_g, ln_b, tm, out_dtype, name):
    t, d = x.shape
    return pl.pallas_call(
        _gmlp_in_body,
        grid=(t // tm, 2),
        in_specs=[pl.BlockSpec((tm, d), lambda i, j: (i, 0)),
                  pl.BlockSpec((1, d), lambda i, j: (0, 0)),
                  mod.spec(1, tm, 2), mod.spec(0, tm, 2),
                  pl.BlockSpec((d, B_D), lambda i, j: (0, j)),
                  pl.BlockSpec((1, B_D), lambda i, j: (0, j)),
                  pl.BlockSpec((1, B_D), lambda i, j: (0, 0)),
                  pl.BlockSpec((1, B_D), lambda i, j: (0, 0))],
        out_specs=pl.BlockSpec((tm, B_D), lambda i, j: (i, j)),
        out_shape=jax.ShapeDtypeStruct((t, 2 * B_D), out_dtype),
        scratch_shapes=[pltpu.VMEM((tm, d), BF16)],
        compiler_params=_cparams(("arbitrary", "arbitrary")),
        name=name,
    )(x, g.reshape(1, d), mod.arr, mod.arr, w, b.reshape(1, -1), ln_g.reshape(1, -1), ln_b.reshape(1, -1))


def _gmlp_out_body(u_ref, v_ref, ws_ref, bs_ref, w_ref, x_ref, gate_ref, o_ref, wbf_ref, wsbf_ref, m_ref):
    q = B_CHUNK

    @pl.when(pl.program_id(0) == 0)
    def _():
        wbf_ref[...] = w_ref[...].astype(BF16)
        causal = lax.broadcasted_iota(I32, (q, q), 0) >= lax.broadcasted_iota(I32, (q, q), 1)
        for g in range(B_N_GROUPS):
            wsbf_ref[g] = jnp.where(causal, ws_ref[g], 0.0).astype(BF16)

    for ci in range(u_ref.shape[0] // q):
        rsl = slice(ci * q, (ci + 1) * q)
        for g in range(B_N_GROUPS):
            gsl = slice(g * B_GROUP_DIM, (g + 1) * B_GROUP_DIM)
            mixed = _dot(wsbf_ref[g], v_ref[rsl, gsl].astype(BF16)) + bs_ref[:, g:g + 1]
            m_ref[rsl, gsl] = (u_ref[rsl, gsl].astype(F32) * mixed).astype(BF16)
    o_ref[...] = x_ref[...] + gate_ref[...] * _dot(m_ref[...], wbf_ref[...])


def _gmlp_out_call(uv, w_s, b_s, w_out, x, mod, tm, name):
    t, d = x.shape
    q = B_CHUNK
    return pl.pallas_call(
        _gmlp_out_body,
        grid=(t // tm,),
        in_specs=[pl.BlockSpec((tm, B_D), lambda i: (i, 0)),
                  pl.BlockSpec((tm, B_D), lambda i: (i, 1)),
                  pl.BlockSpec((B_N_GROUPS, q, q), lambda i: (0, 0, 0)),
                  pl.BlockSpec((q, B_N_GROUPS), lambda i: (0, 0)),
                  pl.BlockSpec((B_D, d), lambda i: (0, 0)),
                  pl.BlockSpec((tm, d), lambda i: (i, 0)),
                  mod.spec(2, tm, 1)],
        out_specs=pl.BlockSpec((tm, d), lambda i: (i, 0)),
        out_shape=jax.ShapeDtypeStruct((t, d), F32),
        scratch_shapes=[pltpu.VMEM((B_D, d), BF16), pltpu.VMEM((B_N_GROUPS, q, q), BF16),
                        pltpu.VMEM((tm, B_D), BF16)],
        compiler_params=_cparams(("arbitrary",)),
        name=name,
    )(uv, uv, w_s, b_s.T, w_out, x, mod.arr)


def _gmlp_out_step_body(u_ref, v_ref, wd_ref, bd_ref, w_ref, x_ref, gate_ref, o_ref):
    mixed = v_ref[...] * wd_ref[...] + bd_ref[...]
    m = (u_ref[...] * mixed).astype(BF16)
    o_ref[...] = x_ref[...] + gate_ref[...] * _dot(m, w_ref[...].astype(BF16))


def _gmlp_out_step_call(uv, w_s, b_s, w_out, x, mod, name):
    t, d = x.shape
    wd = jnp.repeat(w_s[:, 0, 0], B_GROUP_DIM).reshape(1, B_D)
    bd = jnp.repeat(b_s[:, 0], B_GROUP_DIM).reshape(1, B_D)
    return pl.pallas_call(
        _gmlp_out_step_body,
        grid=(1,),
        in_specs=[pl.BlockSpec((t, B_D), lambda i: (0, 0)),
                  pl.BlockSpec((t, B_D), lambda i: (0, 1)),
                  pl.BlockSpec((1, B_D), lambda i: (0, 0)),
                  pl.BlockSpec((1, B_D), lambda i: (0, 0)),
                  pl.BlockSpec((B_D, d), lambda i: (0, 0)),
                  pl.BlockSpec((t, d), lambda i: (0, 0)),
                  mod.spec(2, t, 1)],
        out_specs=pl.BlockSpec((t, d), lambda i: (0, 0)),
        out_shape=jax.ShapeDtypeStruct((t, d), F32),
        compiler_params=_cparams(("arbitrary",)),
        name=name,
    )(uv, uv, wd, bd, w_out, x, mod.arr)


def _pool_matmul(pooled, wg_ref):
    outs = []
    for gi in range(len(C_WINDOWS)):
        gsl = slice(gi * C_GROUP_DIM, (gi + 1) * C_GROUP_DIM)
        outs.append(_dot(pooled[:, gsl].astype(BF16), wg_ref[gi].astype(BF16)))
    return jnp.concatenate(outs, axis=-1)


def _pool_body(x_ref, g_ref, sc_ref, sh_ref, gate_ref, prev_ref, wg_ref, scale_ref, o_ref, pnew_ref, hp_ref,
               *, tiles_per_seq, start):
    i = pl.program_id(0)
    tm = x_ref.shape[0]
    top = 16
    ti = i % tiles_per_seq

    @pl.when(ti == 0)
    def _():
        hp_ref[top - C_STATE:top, :] = prev_ref[...]

    hn = _prenorm(x_ref[...], g_ref[...], sc_ref[...], sh_ref[...])
    hp_ref[top:top + tm, :] = hn
    pos = start + ti * tm + lax.broadcasted_iota(I32, (tm, 1), 0)
    outs = []
    for gi, w in enumerate(C_WINDOWS):
        gsl = slice(gi * C_GROUP_DIM, (gi + 1) * C_GROUP_DIM)
        acc = hn[:, gsl]
        for k in range(1, w):
            acc = acc + hp_ref[top - k:top - k + tm, gsl]
        cnt = jnp.minimum(pos + 1, w).astype(F32)
        outs.append(acc / cnt - hn[:, gsl])
    y = _pool_matmul(jnp.concatenate(outs, axis=-1), wg_ref) * scale_ref[...]
    o_ref[...] = x_ref[...] + gate_ref[...] * y
    hist = hp_ref[top + tm - C_STATE:top + tm, :]
    hp_ref[top - C_STATE:top, :] = hist

    @pl.when(ti == tiles_per_seq - 1)
    def _():
        pnew_ref[...] = hist


def _pool_call(x, g, mod, prev, w_g, scale, nb, tm, start, name):
    t, d = x.shape
    tiles = t // nb // tm
    ng = len(C_WINDOWS)
    return pl.pallas_call(
        functools.partial(_pool_body, tiles_per_seq=tiles, start=start),
        grid=(t // tm,),
        in_specs=[pl.BlockSpec((tm, d), lambda i: (i, 0)),
                  pl.BlockSpec((1, d), lambda i: (0, 0)),
                  mod.spec(1, tm, 1), mod.spec(0, tm, 1), mod.spec(2, tm, 1),
                  pl.BlockSpec((None, C_STATE, d), lambda i: (i // tiles, 0, 0)),
                  pl.BlockSpec((ng, C_GROUP_DIM, C_GROUP_DIM), lambda i: (0, 0, 0)),
                  pl.BlockSpec((1, d), lambda i: (0, 0))],
        out_specs=[pl.BlockSpec((tm, d), lambda i: (i, 0)),
                   pl.BlockSpec((None, C_STATE, d), lambda i: (i // tiles, 0, 0))],
        out_shape=[jax.ShapeDtypeStruct((t, d), F32), jax.ShapeDtypeStruct((nb, C_STATE, d), F32)],
        scratch_shapes=[pltpu.VMEM((16 + tm, d), F32)],
        compiler_params=_cparams(("arbitrary",)),
        name=name,
    )(x, g.reshape(1, d), mod.arr, mod.arr, mod.arr, prev, w_g, scale.reshape(1, d))


def _pool_step_body(x_ref, g_ref, sc_ref, sh_ref, gate_ref, prev_ref, wg_ref, scale_ref, o_ref, pnew_ref, *, start):
    hn = _prenorm(x_ref[...], g_ref[...], sc_ref[...], sh_ref[...])
    outs = []
    for gi, w in enumerate(C_WINDOWS):
        gsl = slice(gi * C_GROUP_DIM, (gi + 1) * C_GROUP_DIM)
        acc = hn[:, gsl]
        for k in range(1, w):
            acc = acc + prev_ref[:, C_STATE - k, gsl]
        outs.append(acc / float(min(start + 1, w)) - hn[:, gsl])
    y = _pool_matmul(jnp.concatenate(outs, axis=-1), wg_ref) * scale_ref[...]
    o_ref[...] = x_ref[...] + gate_ref[...] * y
    for k in range(C_STATE - 1):
        pnew_ref[:, k, :] = prev_ref[:, k + 1, :]
    pnew_ref[:, C_STATE - 1, :] = hn


def _pool_step_call(x, g, mod, prev, w_g, scale, start, name):
    t, d = x.shape
    bt = 32
    ng = len(C_WINDOWS)
    return pl.pallas_call(
        functools.partial(_pool_step_body, start=start),
        grid=(t // bt,),
        in_specs=[pl.BlockSpec((bt, d), lambda i: (i, 0)),
                  pl.BlockSpec((1, d), lambda i: (0, 0)),
                  mod.spec(1, bt, 1), mod.spec(0, bt, 1), mod.spec(2, bt, 1),
                  pl.BlockSpec((bt, C_STATE, d), lambda i: (i, 0, 0)),
                  pl.BlockSpec((ng, C_GROUP_DIM, C_GROUP_DIM), lambda i: (0, 0, 0)),
                  pl.BlockSpec((1, d), lambda i: (0, 0))],
        out_specs=[pl.BlockSpec((bt, d), lambda i: (i, 0)),
                   pl.BlockSpec((bt, C_STATE, d), lambda i: (i, 0, 0))],
        out_shape=[jax.ShapeDtypeStruct((t, d), F32), jax.ShapeDtypeStruct((t, C_STATE, d), F32)],
        compiler_params=_cparams(("arbitrary",)),
        name=name,
    )(x, g.reshape(1, d), mod.arr, mod.arr, mod.arr, prev, w_g, scale.reshape(1, d))


def _route_rows(s, b):
    npg = EXPERTS_PER_GROUP
    gscore = []
    for q in range(N_EXPERT_GROUPS):
        v = b[q * npg:(q + 1) * npg]
        best = None
        for i in range(npg):
            for j in range(i + 1, npg):
                best = v[i] + v[j] if best is None else jnp.maximum(best, v[i] + v[j])
        gscore.append(best)
    gsel = jnp.zeros_like(gscore[0], dtype=I32)
    gbest = gscore[0]
    for q in range(1, N_EXPERT_GROUPS):
        better = gscore[q] > gbest
        gsel = jnp.where(better, q, gsel)
        gbest = jnp.where(better, gscore[q], gbest)
    vb, vs = [], []
    for k in range(npg):
        bk, sk = b[k], s[k]
        for q in range(1, N_EXPERT_GROUPS):
            bk = jnp.where(gsel == q, b[q * npg + k], bk)
            sk = jnp.where(gsel == q, s[q * npg + k], sk)
        vb.append(bk)
        vs.append(sk)
    i1 = jnp.zeros_like(gsel)
    m1 = vb[0]
    for k in range(1, npg):
        better = vb[k] > m1
        i1 = jnp.where(better, k, i1)
        m1 = jnp.where(better, vb[k], m1)
    i2 = jnp.full_like(gsel, -1)
    m2 = jnp.zeros_like(m1)
    for k in range(npg):
        better = (i1 != k) & ((i2 < 0) | (vb[k] > m2))
        i2 = jnp.where(better, k, i2)
        m2 = jnp.where(better, vb[k], m2)
    s1 = vs[0]
    s2 = vs[0]
    for k in range(1, npg):
        s1 = jnp.where(i1 == k, vs[k], s1)
        s2 = jnp.where(i2 == k, vs[k], s2)
    w1 = s1 / (s1 + s2)
    w2 = s2 / (s1 + s2)
    lo = jnp.minimum(i1, i2)
    hi = jnp.maximum(i1, i2)
    pair = jnp.zeros_like(gsel)
    for p, (a, c) in enumerate(PAIRS):
        pair = jnp.where((lo == a) & (hi == c), p, pair)
    first_is_lo = i1 < i2
    return (gsel * len(PAIRS) + pair, jnp.where(first_is_lo, w1, w2), jnp.where(first_is_lo, w2, w1))


def _route_body(x_ref, g_ref, sc_ref, sh_ref, rw_ref, rb_ref, cnt_in_ref, *rest, aliased):
    if aliased:
        rest = rest[3:]
    rows_ref, bucket_ref, rank_ref, cnt_ref = rest
    tm = x_ref.shape[0]

    @pl.when(pl.program_id(0) == 0)
    def _():
        cnt_ref[...] = cnt_in_ref[...]

    hn = _prenorm(x_ref[...], g_ref[...], sc_ref[...], sh_ref[...])
    logits_t = _dot3(hn, rw_ref[...]).T
    scores = _sigmoid(logits_t[0:N_EXPERTS, :])
    biased = scores + rb_ref[0:N_EXPERTS, :]
    bucket, w_lo, w_hi = _route_rows([scores[e:e + 1, :] for e in range(N_EXPERTS)],
                                     [biased[e:e + 1, :] for e in range(N_EXPERTS)])
    bucket_ref[...] = bucket
    onehot = (lax.broadcasted_iota(I32, (BUCKET_ROWS, tm), 0) == bucket).astype(F32)
    before = (lax.broadcasted_iota(I32, (tm, tm), 0) < lax.broadcasted_iota(I32, (tm, tm), 1)).astype(BF16)
    earlier = _dot(onehot.astype(BF16), before) + cnt_ref[:, 0:1]
    rank_ref[...] = jnp.sum(onehot * earlier, axis=0, keepdims=True).astype(I32)
    cnt_ref[...] = cnt_ref[...] + jnp.sum(onehot, axis=1, keepdims=True)

    sub = lax.broadcasted_iota(I32, (LANE, tm), 0)
    gate_t = jnp.where(sub == 0, w_lo, jnp.where(sub == 1, w_hi, 0.0))
    rows_ref[pl.ds(GATE_ROW, tm, stride=ROW_PITCH), :] = gate_t.T
    for j in range(ROW_PLANES):
        rows_ref[pl.ds(j, tm, stride=ROW_PITCH), :] = hn[:, j * LANE:(j + 1) * LANE]


def _route_call(x, g, mod, router_w, router_b, counts, tm, total_rows, row_offset, prior, name):
    t, d = x.shape
    blk0 = row_offset // tm
    rw = jnp.pad(router_w, ((0, 0), (0, LANE - N_EXPERTS)))
    rb = jnp.pad(router_b.reshape(-1, 1), ((0, LANE - N_EXPERTS), (0, 0)))
    in_specs = [pl.BlockSpec((tm, d), lambda i: (i, 0)),
                pl.BlockSpec((1, d), lambda i: (0, 0)),
                mod.spec(4, tm, 1), mod.spec(3, tm, 1),
                pl.BlockSpec((d, LANE), lambda i: (0, 0)),
                pl.BlockSpec((LANE, 1), lambda i: (0, 0)),
                pl.BlockSpec((BUCKET_ROWS, LANE), lambda i: (0, 0))]
    args = [x, g.reshape(1, d), mod.arr, mod.arr, rw, rb, counts]
    aliases = {}
    if prior is not None:
        in_specs += [pl.BlockSpec(memory_space=pl.ANY)] * len(prior)
        aliases = {len(args) + k: k for k in range(len(prior))}
        args += list(prior)
    return pl.pallas_call(
        functools.partial(_route_body, aliased=prior is not None),
        grid=(t // tm,),
        in_specs=in_specs,
        out_specs=[pl.BlockSpec((tm * ROW_PITCH, LANE), lambda i: (blk0 + i, 0)),
                   pl.BlockSpec((1, tm), lambda i: (0, blk0 + i)),
                   pl.BlockSpec((1, tm), lambda i: (0, blk0 + i)),
                   pl.BlockSpec((BUCKET_ROWS, LANE), lambda i: (0, 0))],
        out_shape=[jax.ShapeDtypeStruct((total_rows * ROW_PITCH, LANE), F32),
                   jax.ShapeDtypeStruct((1, total_rows), I32),
                   jax.ShapeDtypeStruct((1, total_rows), I32),
                   jax.ShapeDtypeStruct((BUCKET_ROWS, LANE), F32)],
        input_output_aliases=aliases,
        compiler_params=_cparams(("arbitrary",)),
        name=name,
    )(*args)


def _invert_body(dest_ref, init_ref, gather_ref):
    pltpu.sync_copy(init_ref, gather_ref)

    def put(t, carry):
        gather_ref[dest_ref[t]] = t
        return carry

    lax.fori_loop(0, dest_ref.shape[0], put, 0, unroll=8)


def _invert_call(dest, n_slots, name):
    smem = pl.BlockSpec(memory_space=pltpu.SMEM)
    return pl.pallas_call(
        _invert_body,
        in_specs=[smem, pl.BlockSpec(memory_space=pl.ANY)],
        out_specs=smem,
        out_shape=jax.ShapeDtypeStruct((n_slots,), I32),
        name=name,
    )(dest, jnp.zeros((n_slots,), I32))


def _unrolled(lo, hi, body, carry):
    for b in range(lo, hi):
        carry = body(b, carry)
    return carry


def _experts_body(grp_ref, lo_ref, hi_ref, fresh_ref, valid_ref, out_ref, gather_ref, rows_ref, wg_in_ref,
                  wu_in_ref, wd_in_ref, y_ref, wg_ref, wu_ref, wd_ref, xb_ref, gsem):
    i = pl.program_id(0)
    n = pl.num_programs(0)
    tile = EXPERT_TILE
    copied = GATE_ROW + 1
    group = 8
    del grp_ref, out_ref

    def start_gather(step, slot, loop):
        def body(b, carry):
            for k in range(group):
                r = b * group + k
                tok = gather_ref[step * tile + r]
                pltpu.make_async_copy(rows_ref.at[pl.ds(tok * ROW_PITCH, copied)],
                                      xb_ref.at[slot, pl.ds(r * ROW_PITCH, copied)],
                                      gsem.at[slot]).start(priority=k % 2)
            return carry

        loop(0, tile // group, body, 0)

    def wait_gather(slot):
        rows = tile * copied
        pltpu.make_async_copy(rows_ref.at[pl.ds(0, rows)], xb_ref.at[slot, pl.ds(0, rows)], gsem.at[slot]).wait()

    @pl.when(fresh_ref[i] == 1)
    def _():
        for e in range(EXPERTS_PER_GROUP):
            wg_ref[e] = wg_in_ref[e].astype(BF16)
            wu_ref[e] = wu_in_ref[e].astype(BF16)
            wd_ref[e] = wd_in_ref[e].astype(BF16)

    nxt = jnp.minimum(i + 1, n - 1)
    more = jnp.logical_and(i + 1 < n, valid_ref[nxt] == 1)

    @pl.when(valid_ref[i] == 1)
    def _():
        slot = i % GATHER_SLOTS

        @pl.when(i == 0)
        def _():
            for ahead in range(GATHER_SLOTS - 1):
                start_gather(ahead, ahead, lax.fori_loop)

        wait_gather(slot)
        ahead = i + GATHER_SLOTS - 1
        start_gather(jnp.minimum(ahead, n - 1), ahead % GATHER_SLOTS, _unrolled)
        x = jnp.concatenate([xb_ref[slot, pl.ds(j, tile, stride=ROW_PITCH), :].astype(BF16)
                             for j in range(ROW_PLANES)], axis=-1)

        def expert(e):
            act = (_silu(_dot(x, wg_ref[e])) * _dot(x, wu_ref[e])).astype(BF16)
            return _dot(act, wd_ref[e])

        w = xb_ref[slot, pl.ds(GATE_ROW, tile, stride=ROW_PITCH), :]
        y = w[:, 0:1] * expert(lo_ref[i]) + w[:, 1:2] * expert(hi_ref[i])
        for j in range(ROW_PLANES):
            y_ref[pl.ds(j, tile, stride=ROW_PITCH), :] = y[:, j * LANE:(j + 1) * LANE]

        @pl.when(jnp.logical_not(more))
        def _():
            for k in range(1, GATHER_SLOTS):
                wait_gather((i + k) % GATHER_SLOTS)


def _experts_call(grp, lo, hi, fresh, valid, out_tile, gather_idx, rows, w_gate, w_up, w_down, layer, name):
    n_tiles = grp.shape[0]
    tile = EXPERT_TILE
    d, f = D_MODEL, D_EXPERT
    npg = EXPERTS_PER_GROUP
    group_block = lambda i, grp, *_: (layer, grp[i], 0, 0)
    once = pl.Buffered(1)
    return pl.pallas_call(
        _experts_body,
        grid_spec=pltpu.PrefetchScalarGridSpec(
            num_scalar_prefetch=7,
            grid=(n_tiles,),
            in_specs=[pl.BlockSpec(memory_space=pl.ANY),
                      pl.BlockSpec((None, npg, d, f), group_block, pipeline_mode=once),
                      pl.BlockSpec((None, npg, d, f), group_block, pipeline_mode=once),
                      pl.BlockSpec((None, npg, f, d), group_block, pipeline_mode=once)],
            out_specs=pl.BlockSpec((tile * ROW_PITCH, LANE), lambda i, g, l, h, fr, va, out, *_: (out[i], 0)),
            scratch_shapes=[pltpu.VMEM((npg, d, f), BF16), pltpu.VMEM((npg, d, f), BF16),
                            pltpu.VMEM((npg, f, d), BF16),
                            pltpu.VMEM((GATHER_SLOTS, tile * ROW_PITCH, LANE), F32),
                            pltpu.SemaphoreType.DMA((GATHER_SLOTS,))]),
        out_shape=jax.ShapeDtypeStruct((n_tiles * tile * ROW_PITCH, LANE), F32),
        compiler_params=_cparams(("arbitrary",)),
        name=name,
    )(grp, lo, hi, fresh, valid, out_tile, gather_idx, rows, w_gate, w_up, w_down)


def _moe_res_body(dest_ref, y_ref, x_ref, gate_ref, fg_ref, o_ref, *rest, row_offset, final):
    if final:
        on_ref, buf_ref, sem = rest
    else:
        buf_ref, sem = rest
    i = pl.program_id(0)
    tm = x_ref.shape[0]
    group = 8

    def start_gather(step, slot, loop):
        base = row_offset + step * tm

        def body(b, carry):
            for k in range(group):
                r = b * group + k
                pltpu.make_async_copy(y_ref.at[pl.ds(dest_ref[base + r] * ROW_PITCH, ROW_PLANES)],
                                      buf_ref.at[slot, pl.ds(r * ROW_PITCH, ROW_PLANES)],
                                      sem.at[slot]).start(priority=k % 2)
            return carry

        loop(0, tm // group, body, 0)

    def wait_gather(slot):
        rows = tm * ROW_PLANES
        pltpu.make_async_copy(y_ref.at[pl.ds(0, rows)], buf_ref.at[slot, pl.ds(0, rows)], sem.at[slot]).wait()

    @pl.when(i == 0)
    def _():
        start_gather(0, 0, lax.fori_loop)

    slot = i % 2
    last = pl.num_programs(0) - 1
    wait_gather(slot)
    start_gather(jnp.minimum(i + 1, last), 1 - slot, lax.fori_loop)
    y = jnp.concatenate([buf_ref[slot, pl.ds(j, tm, stride=ROW_PITCH), :] for j in range(ROW_PLANES)], axis=-1)
    xn = x_ref[...] + gate_ref[...] * y
    o_ref[...] = xn
    if final:
        ms = jnp.mean(xn * xn, axis=-1, keepdims=True)
        on_ref[...] = xn * lax.rsqrt(ms + EPS) * fg_ref[...]

    @pl.when(i == last)
    def _():
        wait_gather(1 - slot)


def _moe_res_call(dest, y_sorted, x, mod, final_g, tm, row_offset, final, name):
    t, d = x.shape
    n_out = 2 if final else 1
    gate_spec = mod.spec(5, tm, 1)
    gate_map = gate_spec.index_map
    return pl.pallas_call(
        functools.partial(_moe_res_body, row_offset=row_offset, final=final),
        grid_spec=pltpu.PrefetchScalarGridSpec(
            num_scalar_prefetch=1,
            grid=(t // tm,),
            in_specs=[pl.BlockSpec(memory_space=pl.ANY),
                      pl.BlockSpec((tm, d), lambda i, dest: (i, 0)),
                      pl.BlockSpec(gate_spec.block_shape, lambda i, dest: gate_map(i)),
                      pl.BlockSpec((1, d), lambda i, dest: (0, 0))],
            out_specs=[pl.BlockSpec((tm, d), lambda i, dest: (i, 0))] * n_out,
            scratch_shapes=[pltpu.VMEM((2, tm * ROW_PITCH, LANE), F32), pltpu.SemaphoreType.DMA((2,))]),
        out_shape=[jax.ShapeDtypeStruct((t, d), F32)] * n_out,
        compiler_params=_cparams(("arbitrary",)),
        name=name,
    )(dest, y_sorted, x, mod.arr, final_g.reshape(1, d))


def _tile_tables(counts, n_tiles):
    tile = EXPERT_TILE
    tiles_per_bucket = (counts + tile - 1) // tile
    ends = jnp.cumsum(tiles_per_bucket)
    starts = ends - tiles_per_bucket
    used = ends[-1]
    ti = jnp.arange(n_tiles, dtype=I32)
    valid = (ti < used).astype(I32)
    tile_bucket = jnp.sum((jnp.minimum(ti, used - 1)[:, None] >= ends[None, :]).astype(I32), axis=1)
    pair_lo = jnp.asarray([p[0] for p in PAIRS], I32)
    pair_hi = jnp.asarray([p[1] for p in PAIRS], I32)
    grp = tile_bucket // len(PAIRS)
    lo = pair_lo[tile_bucket % len(PAIRS)]
    hi = pair_hi[tile_bucket % len(PAIRS)]
    fresh = jnp.concatenate([jnp.ones((1,), I32), (grp[1:] != grp[:-1]).astype(I32)])
    return grp, lo, hi, fresh, valid, jnp.minimum(ti, used - 1).astype(I32), (starts * tile).astype(I32)


def _moe_layer(xp, xs, norm_g2, mod_p, mod_s, router_w, router_b, w_gate, w_up, w_down, layer, final_g, final):
    tp, ts = xp.shape[0], xs.shape[0]
    total = tp + ts
    n_tiles = -(-total // EXPERT_TILE) + N_BUCKETS
    zero_counts = jnp.zeros((BUCKET_ROWS, LANE), F32)
    rows, bucket, rank, counts = _route_call(xp, norm_g2, mod_p, router_w, router_b, zero_counts, 512, total, 0,
                                             None, "route_prompt")
    rows, bucket, rank, counts = _route_call(xs, norm_g2, mod_s, router_w, router_b, counts, ts, total, tp,
                                             (rows, bucket, rank), "route_sample")
    grp, lo, hi, fresh, valid, out_tile, bucket_start = _tile_tables(counts[:N_BUCKETS, 0].astype(I32), n_tiles)
    dest = bucket_start[bucket[0]] + rank[0]
    gather_idx = _invert_call(dest, n_tiles * EXPERT_TILE, "invert")
    y_sorted = _experts_call(grp, lo, hi, fresh, valid, out_tile, gather_idx, rows, w_gate, w_up, w_down, layer,
                             "experts")
    outp = _moe_res_call(dest, y_sorted, xp, mod_p, final_g, 512, 0, final, "moe_res_prompt")
    outs = _moe_res_call(dest, y_sorted, xs, mod_s, final_g, ts, tp, final, "moe_res_sample")
    return outp, outs


def _mamba_layer(x, g, mod, conv_prev, ssm, w_zx, w_dt, conv_w, conv_b, dt_bias, a_log, d_skip, norm_g, w_out,
                 tm, tag):
    pad_h = lambda v: jnp.pad(v.reshape(1, -1), ((0, 0), (0, LANE - A_N_HEADS)))
    tm_in = min(2 * tm, x.shape[0])
    proj = _norm_mm_call(x, g, mod, 1, 0, w_zx, tm_in, 512, BF16, "a_in_" + tag)
    dt_raw = _norm_mm3_call(x, g, mod, 1, 0, w_dt, tm, "a_dt_" + tag)
    d_x = jnp.repeat(d_skip, A_HEAD_DIM).reshape(1, A_D_INNER)
    weights = (conv_w, conv_b.reshape(1, -1), pad_h(dt_bias), pad_h(a_log), d_x, norm_g.reshape(1, -1))
    if ssm[0] == "step":
        yn, conv_new, ssm_new = _ssd_step_call(proj, dt_raw, conv_prev, ssm[1], ssm[2], ssm[3], *weights,
                                               "ssd_step_" + tag)
    else:
        yn, conv_new, ssm_new = _ssd_call(proj, dt_raw, conv_prev, ssm[2], *weights, ssm[1], "ssd_" + tag)
    x = _out_res_call(yn, w_out, x, mod, 2, tm, "a_out_" + tag)
    return x, conv_new, ssm_new


def kernel(x_prompt, x_sample, c_prompt, c_sample, state_a_conv, state_a_ssm, state_c_pool, w_mod, b_mod, norm_g, final_g, a_w_in, a_conv_w, a_conv_b, a_dt_bias, a_log, a_d, a_norm_g, a_w_out, b_w_in, b_b_in, b_ln_g, b_ln_b, b_w_s, b_b_s, b_w_out, c_w_g, c_scale, router_w, router_b, e_w_gate, e_w_up, e_w_down):
    bp, seq, d = x_prompt.shape
    bs = x_sample.shape[0]
    n_a, n_c = state_a_conv.shape[0], state_c_pool.shape[0]
    mod_all = _mod_call(jnp.concatenate([c_prompt, c_sample], axis=0), w_mod, b_mod)
    mod_p_arr = mod_all[:, :bp].reshape(DEPTH, bp, 6, 1, d)
    mod_s_arr = mod_all[:, bp:]
    xp = x_prompt.reshape(bp * seq, d)
    xs = x_sample.reshape(bs, d)
    conv_p, ssm_p, pool_p, conv_s, pool_s, v_s = [], [], [], [], [], []
    hp = A_N_HEADS * A_HEAD_DIM
    ssm_s_in = state_a_ssm.reshape(n_a, bs, hp, A_D_STATE)
    ssm_s_out = None
    yp = ys = None
    for i in range(DEPTH):
        kind, s = LAYER_KIND[i], LAYER_SLOT[i]
        mod_p = Mod(mod_p_arr, i, False, seq)
        mod_s = Mod(mod_s_arr, i, True)
        g1 = norm_g[i, 0]
        if kind == 0:
            w_zx = a_w_in[s, :, :A_ZX].astype(BF16)
            w_dt = jnp.pad(a_w_in[s, :, A_ZX:], ((0, 0), (0, LANE - A_N_HEADS)))
            weights = (w_zx, w_dt, a_conv_w[s], a_conv_b[s], a_dt_bias[s], a_log[s], a_d[s], a_norm_g[s], a_w_out[s])
            conv0 = jnp.zeros((bp, A_CONV - 1, A_CONV_CH), F32)
            ssm0 = jnp.zeros((bp, hp, A_D_STATE), F32)
            xp, cv, ss = _mamba_layer(xp, g1, mod_p, conv0, ("prompt", bp, ssm0), *weights, 1024, "p%d" % i)
            conv_p.append(cv)
            ssm_p.append(ss.reshape(bp, A_N_HEADS, A_HEAD_DIM, A_D_STATE))
            xs, cv, ssm_s_out = _mamba_layer(xs, g1, mod_s, state_a_conv[s], ("step", ssm_s_in, s, ssm_s_out),
                                             *weights, bs, "s%d" % i)
            conv_s.append(cv)
        elif kind == 1:
            w_uv = b_w_in[s].astype(BF16)
            uv = _gmlp_in_call(xp, g1, mod_p, w_uv, b_b_in[s], b_ln_g[s], b_ln_b[s], 512, BF16, "b_in_p%d" % i)
            xp = _gmlp_out_call(uv, b_w_s[s], b_b_s[s], b_w_out[s], xp, mod_p, 512, "b_out_p%d" % i)
            uv = _gmlp_in_call(xs, g1, mod_s, w_uv, b_b_in[s], b_ln_g[s], b_ln_b[s], bs, F32, "b_in_s%d" % i)
            xs = _gmlp_out_step_call(uv, b_w_s[s], b_b_s[s], b_w_out[s], xs, mod_s, "b_out_s%d" % i)
            v_s.append(uv[:, B_D:].reshape(bs, 1, B_D))
        else:
            pool0 = jnp.zeros((bp, C_STATE, d), F32)
            xp, pr = _pool_call(xp, g1, mod_p, pool0, c_w_g[s], c_scale[s], bp, 512, 0, "pool_p%d" % i)
            pool_p.append(pr)
            xs, pr = _pool_step_call(xs, g1, mod_s, state_c_pool[s], c_w_g[s], c_scale[s], PAST_LEN, "pool_s%d" % i)
            pool_s.append(pr)
        final = i == DEPTH - 1
        outp, outs = _moe_layer(xp, xs, norm_g[i, 1], mod_p, mod_s, router_w, router_b, e_w_gate, e_w_up, e_w_down, i,
                                final_g, final)
        xp, xs = outp[0], outs[0]
        if final:
            yp, ys = outp[1], outs[1]
    return (yp.reshape(bp, seq, d), ys.reshape(bs, 1, d), jnp.stack(conv_p), jnp.stack(ssm_p), jnp.stack(pool_p),
            jnp.stack(conv_s), ssm_s_out.reshape(state_a_ssm.shape), jnp.stack(pool_s), jnp.stack(v_s))
```

```python
import functools
import math

import numpy as np
import jax
import jax.numpy as jnp
from jax import lax
from jax.experimental import pallas as pl
from jax.experimental.pallas import tpu as pltpu

F32 = jnp.float32
BF16 = jnp.bfloat16
I32 = jnp.int32
EPS = 1e-6

LANE = 128
D_MODEL = 1024
DEPTH = 4
PAST_LEN = 16384
LAYER_KIND = (0, 1, 2, 0)
LAYER_SLOT = (0, 0, 0, 1)
A_D_INNER = 2 * D_MODEL
A_HEAD_DIM = 64
A_N_HEADS = A_D_INNER // A_HEAD_DIM
A_N_GROUPS = 8
A_HPG = A_N_HEADS // A_N_GROUPS
A_D_STATE = 128
A_GN = A_N_GROUPS * A_D_STATE
A_CONV = 4
A_CONV_CH = A_D_INNER + 2 * A_GN
A_ZX = A_D_INNER + A_CONV_CH
A_CHUNK = 128
A_GW = A_HPG * A_HEAD_DIM
B_D = 2 * D_MODEL
B_N_GROUPS = 8
B_GROUP_DIM = B_D // B_N_GROUPS
B_CHUNK = 128
C_WINDOWS = (2, 4, 8, 16)
C_GROUP_DIM = D_MODEL // len(C_WINDOWS)
C_STATE = max(C_WINDOWS) - 1
N_EXPERTS = 16
N_EXPERT_GROUPS = 4
EXPERTS_PER_GROUP = 4
D_EXPERT = D_MODEL // 2
PAIRS = ((0, 1), (0, 2), (0, 3), (1, 2), (1, 3), (2, 3))
N_BUCKETS = N_EXPERT_GROUPS * len(PAIRS)
BUCKET_ROWS = 32
ROW_PLANES = D_MODEL // LANE
ROW_PITCH = ROW_PLANES + 4
GATE_ROW = ROW_PLANES
EXPERT_TILE = 256
GATHER_SLOTS = 4
VMEM_LIMIT = 56 * 1024 * 1024


def _cparams(sem, vmem=VMEM_LIMIT):
    return pltpu.CompilerParams(dimension_semantics=sem, vmem_limit_bytes=vmem)


def _sigmoid(x):
    return 1.0 / (1.0 + jnp.exp(-x))


def _silu(x):
    return x * _sigmoid(x)


def _gelu_tanh(x):
    c = 2.0 * math.sqrt(2.0 / math.pi)
    return x / (1.0 + jnp.exp(x * (-c - (c * 0.044715) * (x * x))))


def _softplus(x):
    return jnp.maximum(x, 0.0) + jnp.log1p(jnp.exp(-jnp.abs(x)))


def _split2(a):
    hi = a.astype(BF16)
    lo = (a - hi.astype(F32)).astype(BF16)
    return hi, lo


def _split3(a):
    hi = a.astype(BF16)
    r = a - hi.astype(F32)
    mid = r.astype(BF16)
    lo = (r - mid.astype(F32)).astype(BF16)
    return hi, mid, lo


def _dot(a, b):
    return jnp.dot(a, b, preferred_element_type=F32)


def _dot_nt(a, b):
    return lax.dot_general(a, b, (((1,), (1,)), ((), ())), preferred_element_type=F32)


def _dot_tn(a, b):
    return lax.dot_general(a, b, (((0,), (0,)), ((), ())), preferred_element_type=F32)


def _dot3(a, b):
    a_hi, a_lo = _split2(a)
    b_hi, b_lo = _split2(b)
    return _dot(a_hi, b_hi) + (_dot(a_lo, b_hi) + _dot(a_hi, b_lo))


def _prenorm(x, g, sc, sh):
    ms = jnp.mean(x * x, axis=-1, keepdims=True)
    return (x * lax.rsqrt(ms + EPS) * g) * (1.0 + sc) + sh


def _mod_body(c_ref, w_ref, b_ref, o_ref):
    o_ref[...] = _dot3(_silu(c_ref[...]), w_ref[...]) + b_ref[...]


def _mod_call(c_all, w_mod, b_mod):
    nb, d = c_all.shape
    depth, _, n = w_mod.shape
    tn = 1536
    return pl.pallas_call(
        _mod_body,
        grid=(depth, n // tn),
        in_specs=[pl.BlockSpec((nb, d), lambda i, j: (0, 0)),
                  pl.BlockSpec((None, d, tn), lambda i, j: (i, 0, j)),
                  pl.BlockSpec((None, 1, tn), lambda i, j: (i, 0, j))],
        out_specs=pl.BlockSpec((None, nb, tn), lambda i, j: (i, 0, j)),
        out_shape=jax.ShapeDtypeStruct((depth, nb, n), F32),
        compiler_params=_cparams(("arbitrary", "arbitrary")),
        name="mod",
    )(c_all, w_mod, b_mod.reshape(depth, 1, n))


class Mod:
    def __init__(self, arr, layer, per_row, rows_per_seq=None):
        self.arr, self.layer, self.per_row, self.rows_per_seq = arr, layer, per_row, rows_per_seq

    def spec(self, which, tm, ngrid):
        layer = self.layer
        if self.per_row:
            if ngrid == 1:
                return pl.BlockSpec((None, tm, D_MODEL), lambda i: (layer, i, which))
            return pl.BlockSpec((None, tm, D_MODEL), lambda i, j: (layer, i, which))
        tiles = self.rows_per_seq // tm
        if ngrid == 1:
            return pl.BlockSpec((None, None, None, 1, D_MODEL), lambda i: (layer, i // tiles, which, 0, 0))
        return pl.BlockSpec((None, None, None, 1, D_MODEL), lambda i, j: (layer, i // tiles, which, 0, 0))


def _norm_mm_body(x_ref, g_ref, sc_ref, sh_ref, w_ref, o_ref, hn_ref):
    @pl.when(pl.program_id(1) == 0)
    def _():
        hn_ref[...] = _prenorm(x_ref[...], g_ref[...], sc_ref[...], sh_ref[...]).astype(BF16)

    o_ref[...] = _dot(hn_ref[...], w_ref[...]).astype(o_ref.dtype)


def _norm_mm_call(x, g, mod, which_sc, which_sh, w, tm, tn, out_dtype, name):
    t, d = x.shape
    n_cols = w.shape[1]
    return pl.pallas_call(
        _norm_mm_body,
        grid=(t // tm, n_cols // tn),
        in_specs=[pl.BlockSpec((tm, d), lambda i, j: (i, 0)),
                  pl.BlockSpec((1, d), lambda i, j: (0, 0)),
                  mod.spec(which_sc, tm, 2), mod.spec(which_sh, tm, 2),
                  pl.BlockSpec((d, tn), lambda i, j: (0, j))],
        out_specs=pl.BlockSpec((tm, tn), lambda i, j: (i, j)),
        out_shape=jax.ShapeDtypeStruct((t, n_cols), out_dtype),
        scratch_shapes=[pltpu.VMEM((tm, d), BF16)],
        compiler_params=_cparams(("arbitrary", "arbitrary")),
        name=name,
    )(x, g.reshape(1, d), mod.arr, mod.arr, w)


def _norm_mm3_body(x_ref, g_ref, sc_ref, sh_ref, w_ref, o_ref):
    hn = _prenorm(x_ref[...], g_ref[...], sc_ref[...], sh_ref[...])
    o_ref[...] = _dot3(hn, w_ref[...])


def _norm_mm3_call(x, g, mod, which_sc, which_sh, w, tm, name):
    t, d = x.shape
    n = w.shape[1]
    return pl.pallas_call(
        _norm_mm3_body,
        grid=(t // tm,),
        in_specs=[pl.BlockSpec((tm, d), lambda i: (i, 0)),
                  pl.BlockSpec((1, d), lambda i: (0, 0)),
                  mod.spec(which_sc, tm, 1), mod.spec(which_sh, tm, 1),
                  pl.BlockSpec((d, n), lambda i: (0, 0))],
        out_specs=pl.BlockSpec((tm, n), lambda i: (i, 0)),
        out_shape=jax.ShapeDtypeStruct((t, n), F32),
        compiler_params=_cparams(("arbitrary",)),
        name=name,
    )(x, g.reshape(1, d), mod.arr, mod.arr, w)


def _out_res_body(y_ref, w_ref, x_ref, gate_ref, o_ref, wbf_ref):
    @pl.when(pl.program_id(0) == 0)
    def _():
        wbf_ref[...] = w_ref[...].astype(BF16)

    o_ref[...] = x_ref[...] + gate_ref[...] * _dot(y_ref[...], wbf_ref[...])


def _out_res_call(y, w, x, mod, which_gate, tm, name):
    t, k = y.shape
    d = x.shape[1]
    return pl.pallas_call(
        _out_res_body,
        grid=(t // tm,),
        in_specs=[pl.BlockSpec((tm, k), lambda i: (i, 0)),
                  pl.BlockSpec((k, d), lambda i: (0, 0)),
                  pl.BlockSpec((tm, d), lambda i: (i, 0)),
                  mod.spec(which_gate, tm, 1)],
        out_specs=pl.BlockSpec((tm, d), lambda i: (i, 0)),
        out_shape=jax.ShapeDtypeStruct((t, d), F32),
        scratch_shapes=[pltpu.VMEM((k, d), BF16)],
        compiler_params=_cparams(("arbitrary",)),
        name=name,
    )(y, w, x, mod.arr)


def _head_expand():
    h = np.arange(LANE)[:, None]
    c = np.arange(A_D_INNER)[None, :]
    return jnp.asarray((c // A_HEAD_DIM == h).astype(np.float32), dtype=BF16)


def _gate_norm(y, z, ng):
    gated = y * _silu(z)
    ms = jnp.mean(gated * gated, axis=-1, keepdims=True)
    return gated * lax.rsqrt(ms + EPS) * ng


def _ssd_body(z_ref, xs_ref, bc_ref, dt_ref, cprev_ref, sprev_ref, cw_ref, cb_ref, dtb_ref, alog_ref,
              dx_ref, ng_ref, exp_ref, yn_ref, cnew_ref, snew_ref, tail_ref, act_ref):
    c = pl.program_id(1)
    q = A_CHUNK
    nt = A_CONV - 1

    @pl.when(c == 0)
    def _():
        tail_ref[0:nt, :] = cprev_ref[...]
        snew_ref[...] = sprev_ref[...]

    row = lax.broadcasted_iota(I32, (q, q), 0)
    col = lax.broadcasted_iota(I32, (q, q), 1)
    shifts = [jnp.where(row - col == nt - k, 1.0, 0.0).astype(BF16) for k in range(nt)]
    sub = lax.broadcasted_iota(I32, (8, 1), 0)
    cw = 512
    for j in range(A_CONV_CH // cw):
        sl = slice(j * cw, (j + 1) * cw)
        src = xs_ref if (j + 1) * cw <= A_D_INNER else bc_ref
        off = j * cw if src is xs_ref else j * cw - A_D_INNER
        x_bf = src[:, off:off + cw]
        conv = cb_ref[:, sl] + x_bf.astype(F32) * cw_ref[nt:nt + 1, sl]
        for k in range(nt):
            conv = conv + _dot(shifts[k], x_bf) * cw_ref[k:k + 1, sl]
        act_ref[:, sl] = _silu(conv)
        corr = jnp.zeros((8, cw), F32)
        for l in range(nt):
            c_l = sum(tail_ref[l + k:l + k + 1, sl] * cw_ref[k:k + 1, sl] for k in range(nt - l))
            corr = jnp.where(sub == l, c_l, corr)
        act_ref[0:8, sl] = _silu(conv[0:8, :] + corr)
        tail_ref[0:nt, sl] = x_bf[q - 8:q, :].astype(F32)[8 - nt:8, :]

    @pl.when(c == pl.num_programs(1) - 1)
    def _():
        cnew_ref[...] = tail_ref[0:nt, :]

    dt = _softplus(dt_ref[...] + dtb_ref[...])
    a = dt * (-jnp.exp(alog_ref[...]))
    row = lax.broadcasted_iota(I32, (q, q), 0)
    col = lax.broadcasted_iota(I32, (q, q), 1)
    causal = row >= col
    tril = jnp.where(causal, 1.0, 0.0).astype(BF16)
    a_hi, a_mid, a_lo = _split3(a)
    cs = _dot(tril, a_hi) + (_dot(tril, a_mid) + _dot(tril, a_lo))
    cs_t = cs.T
    dt_t = dt.T
    cs_last = cs[q - 1:q, :]
    ecs = jnp.exp(cs)
    wend = jnp.exp(cs_last - cs) * dt
    st_hi, st_lo = _split2(jnp.concatenate([ecs, wend], axis=0))
    st_x = _dot(st_hi, exp_ref[...]) + _dot(st_lo, exp_ref[...])
    lane_head = lax.broadcasted_iota(I32, (q, A_GW), 1) // A_HEAD_DIM

    for g in range(A_N_GROUPS):
        gsl = slice(g * A_GW, (g + 1) * A_GW)
        b_g = act_ref[:, A_D_INNER + g * A_D_STATE:A_D_INNER + (g + 1) * A_D_STATE].astype(BF16)
        c_g = act_ref[:, A_D_INNER + A_GN + g * A_D_STATE:A_D_INNER + A_GN + (g + 1) * A_D_STATE].astype(BF16)
        x_g = act_ref[:, gsl]
        x_bf = x_g.astype(BF16)
        h_g = snew_ref[gsl, :]
        cb = _dot_nt(c_g, b_g)
        y = jnp.zeros((q, A_GW), F32)
        for r in range(A_HPG):
            h = g * A_HPG + r
            seg = cs[:, h:h + 1] - cs_t[h:h + 1, :]
            decay = jnp.exp(jnp.where(causal, seg, -1e30))
            wm = (cb * decay * dt_t[h:h + 1, :]).astype(BF16)
            y = y + _dot(wm, jnp.where(lane_head == r, x_bf, jnp.zeros_like(x_bf)))
        y = y + st_x[0:q, gsl] * _dot_nt(c_g, h_g.astype(BF16)) + dx_ref[:, gsl] * x_g
        yn_ref[:, gsl] = _gate_norm(y, z_ref[:, gsl].astype(F32), ng_ref[:, gsl]).astype(BF16)
        s_new = _dot_tn((x_g * st_x[q:2 * q, gsl]).astype(BF16), b_g)
        for r in range(A_HPG):
            h = g * A_HPG + r
            rsl = slice(g * A_GW + r * A_HEAD_DIM, g * A_GW + (r + 1) * A_HEAD_DIM)
            keep = jnp.exp(cs[q - 1:q, h:h + 1])
            snew_ref[rsl, :] = snew_ref[rsl, :] * keep + s_new[r * A_HEAD_DIM:(r + 1) * A_HEAD_DIM, :]


def _ssd_call(proj, dt_raw, conv_prev, ssm_prev, conv_w, conv_b, dt_bias, a_log, d_x, norm_g, nb, name):
    t = proj.shape[0]
    q = A_CHUNK
    nc = t // nb // q
    hp = A_N_HEADS * A_HEAD_DIM
    row = lambda b, c: (b * nc + c, 0)
    full = lambda shape: pl.BlockSpec(shape, lambda b, c: (0,) * len(shape))
    return pl.pallas_call(
        _ssd_body,
        grid=(nb, nc),
        in_specs=[pl.BlockSpec((q, A_D_INNER), lambda b, c: (b * nc + c, 0)),
                  pl.BlockSpec((q, A_D_INNER), lambda b, c: (b * nc + c, 1)),
                  pl.BlockSpec((q, 2 * A_GN), lambda b, c: (b * nc + c, 2)),
                  pl.BlockSpec((q, LANE), row),
                  pl.BlockSpec((None, A_CONV - 1, A_CONV_CH), lambda b, c: (b, 0, 0)),
                  pl.BlockSpec((None, hp, A_D_STATE), lambda b, c: (b, 0, 0)),
                  full((A_CONV, A_CONV_CH)), full((1, A_CONV_CH)), full((1, LANE)), full((1, LANE)),
                  full((1, A_D_INNER)), full((1, A_D_INNER)), full((LANE, A_D_INNER))],
        out_specs=[pl.BlockSpec((q, A_D_INNER), row),
                   pl.BlockSpec((None, A_CONV - 1, A_CONV_CH), lambda b, c: (b, 0, 0)),
                   pl.BlockSpec((None, hp, A_D_STATE), lambda b, c: (b, 0, 0))],
        out_shape=[jax.ShapeDtypeStruct((t, A_D_INNER), BF16),
                   jax.ShapeDtypeStruct((nb, A_CONV - 1, A_CONV_CH), F32),
                   jax.ShapeDtypeStruct((nb, hp, A_D_STATE), F32)],
        scratch_shapes=[pltpu.VMEM((8, A_CONV_CH), F32), pltpu.VMEM((q, A_CONV_CH), F32)],
        compiler_params=_cparams(("arbitrary", "arbitrary")),
        name=name,
    )(proj, proj, proj, dt_raw, conv_prev, ssm_prev, conv_w, conv_b, dt_bias, a_log, d_x, norm_g, _head_expand())


def _ssd_step_body(z_ref, xs_ref, bc_ref, dt_ref, cprev_ref, sprev_ref, cw_ref, cb_ref, dtb_ref, alog_ref,
                   dx_ref, ng_ref, exp_ref, yn_ref, cnew_ref, snew_ref, da_ref, y_ref):
    bt = z_ref.shape[0]
    cur = jnp.concatenate([xs_ref[...], bc_ref[...]], axis=1).astype(F32)
    conv = cb_ref[...] + cur * cw_ref[A_CONV - 1:A_CONV, :]
    for k in range(A_CONV - 1):
        conv = conv + cprev_ref[:, k, :] * cw_ref[k:k + 1, :]
    for k in range(A_CONV - 2):
        cnew_ref[:, k, :] = cprev_ref[:, k + 1, :]
    cnew_ref[:, A_CONV - 2, :] = cur
    act = _silu(conv)
    xs = act[:, 0:A_D_INNER]
    bm_bf = act[:, A_D_INNER:A_D_INNER + A_GN].astype(BF16)
    cm_bf = act[:, A_D_INNER + A_GN:A_CONV_CH].astype(BF16)
    dt = _softplus(dt_ref[...] + dtb_ref[...])
    da_ref[...] = jnp.exp(dt * (-jnp.exp(alog_ref[...])))
    dt_hi, dt_lo = _split2(dt)
    dt_x = _dot(dt_hi, exp_ref[...]) + _dot(dt_lo, exp_ref[...])
    xdt = xs * dt_x
    rows = lax.broadcasted_iota(I32, (bt, 1), 0)
    y_ref[...] = jnp.zeros_like(y_ref)

    def per_seq(j, carry):
        mine = rows == j
        xdt_j = jnp.where(mine, xdt, 0.0)
        da_j = da_ref[pl.ds(j, 1), :]
        for g in range(A_N_GROUPS):
            gsl = slice(g * A_GW, (g + 1) * A_GW)
            nsl = slice(g * A_D_STATE, (g + 1) * A_D_STATE)
            outer = _dot_tn(xdt_j[:, gsl].astype(BF16), bm_bf[:, nsl])
            for r in range(A_HPG):
                h = g * A_HPG + r
                rsl = slice(r * A_HEAD_DIM, (r + 1) * A_HEAD_DIM)
                hsl = slice(g * A_GW + r * A_HEAD_DIM, g * A_GW + (r + 1) * A_HEAD_DIM)
                snew_ref[j, hsl, :] = sprev_ref[j, hsl, :] * da_j[:, h:h + 1] + outer[rsl, :]
            yg = _dot_nt(cm_bf[:, nsl], snew_ref[j, gsl, :].astype(BF16))
            y_ref[:, gsl] = y_ref[:, gsl] + jnp.where(mine, yg, 0.0)
        return carry

    lax.fori_loop(0, bt, per_seq, 0)
    y = y_ref[...] + dx_ref[...] * xs
    z = z_ref[...].astype(F32)
    for g in range(A_N_GROUPS):
        gsl = slice(g * A_GW, (g + 1) * A_GW)
        yn_ref[:, gsl] = _gate_norm(y[:, gsl], z[:, gsl], ng_ref[:, gsl]).astype(BF16)


_SSD_STEP_INPUTS = 13


def _ssd_step_aliased_body(*refs):
    _ssd_step_body(*refs[:_SSD_STEP_INPUTS], *refs[_SSD_STEP_INPUTS + 1:])


def _ssd_step_call(proj, dt_raw, conv_prev, ssm_all, slot, ssm_out, conv_w, conv_b, dt_bias, a_log, d_x, norm_g,
                   name):
    nb = proj.shape[0]
    bt = 8
    hp = A_N_HEADS * A_HEAD_DIM
    full = lambda shape: pl.BlockSpec(shape, lambda i: (0,) * len(shape))
    state_spec = pl.BlockSpec((None, bt, hp, A_D_STATE), lambda i: (slot, i, 0, 0))
    in_specs = [pl.BlockSpec((bt, A_D_INNER), lambda i: (i, 0)),
                pl.BlockSpec((bt, A_D_INNER), lambda i: (i, 1)),
                pl.BlockSpec((bt, 2 * A_GN), lambda i: (i, 2)),
                pl.BlockSpec((bt, LANE), lambda i: (i, 0)),
                pl.BlockSpec((bt, A_CONV - 1, A_CONV_CH), lambda i: (i, 0, 0)),
                state_spec,
                full((A_CONV, A_CONV_CH)), full((1, A_CONV_CH)), full((1, LANE)), full((1, LANE)),
                full((1, A_D_INNER)), full((1, A_D_INNER)), full((LANE, A_D_INNER))]
    args = [proj, proj, proj, dt_raw, conv_prev, ssm_all, conv_w, conv_b, dt_bias, a_log, d_x, norm_g, _head_expand()]
    assert len(args) == _SSD_STEP_INPUTS
    aliases = {}
    if ssm_out is not None:
        in_specs.append(pl.BlockSpec(memory_space=pl.ANY))
        args.append(ssm_out)
        aliases = {_SSD_STEP_INPUTS: 2}
    return pl.pallas_call(
        _ssd_step_body if ssm_out is None else _ssd_step_aliased_body,
        grid=(nb // bt,),
        in_specs=in_specs,
        out_specs=[pl.BlockSpec((bt, A_D_INNER), lambda i: (i, 0)),
                   pl.BlockSpec((bt, A_CONV - 1, A_CONV_CH), lambda i: (i, 0, 0)),
                   state_spec],
        out_shape=[jax.ShapeDtypeStruct((nb, A_D_INNER), BF16),
                   jax.ShapeDtypeStruct((nb, A_CONV - 1, A_CONV_CH), F32),
                   jax.ShapeDtypeStruct(ssm_all.shape, F32)],
        scratch_shapes=[pltpu.VMEM((bt, LANE), F32), pltpu.VMEM((bt, A_D_INNER), F32)],
        input_output_aliases=aliases,
        compiler_params=_cparams(("arbitrary",)),
        name=name,
    )(*args)


def _gmlp_in_body(x_ref, g_ref, sc_ref, sh_ref, w_ref, b_ref, lg_ref, lb_ref, o_ref, hn_ref):
    j = pl.program_id(1)

    @pl.when(j == 0)
    def _():
        hn_ref[...] = _prenorm(x_ref[...], g_ref[...], sc_ref[...], sh_ref[...]).astype(BF16)

    uv = _gelu_tanh(_dot(hn_ref[...], w_ref[...]) + b_ref[...])

    @pl.when(j == 0)
    def _():
        o_ref[...] = uv.astype(o_ref.dtype)

    @pl.when(j == 1)
    def _():
        vc = uv - jnp.mean(uv, axis=-1, keepdims=True)
        var = jnp.mean(vc * vc, axis=-1, keepdims=True)
        o_ref[...] = (vc * lax.rsqrt(var + EPS) * lg_ref[...] + lb_ref[...]).astype(o_ref.dtype)


def _gmlp_in_call(x, g, mod, w, b, ln_g, ln_b, tm, out_dtype, name):
    t, d = x.shape
    return pl.pallas_call(
        _gmlp_in_body,
        grid=(t // tm, 2),
        in_specs=[pl.BlockSpec((tm, d), lambda i, j: (i, 0)),
                  pl.BlockSpec((1, d), lambda i, j: (0, 0)),
                  mod.spec(1, tm, 2), mod.spec(0, tm, 2),
                  pl.BlockSpec((d, B_D), lambda i, j: (0, j)),
                  pl.BlockSpec((1, B_D), lambda i, j: (0, j)),
                  pl.BlockSpec((1, B_D), lambda i, j: (0, 0)),
                  pl.BlockSpec((1, B_D), lambda i, j: (0, 0))],
        out_specs=pl.BlockSpec((tm, B_D), lambda i, j: (i, j)),
        out_shape=jax.ShapeDtypeStruct((t, 2 * B_D), out_dtype),
        scratch_shapes=[pltpu.VMEM((tm, d), BF16)],
        compiler_params=_cparams(("arbitrary", "arbitrary")),
        name=name,
    )(x, g.reshape(1, d), mod.arr, mod.arr, w, b.reshape(1, -1), ln_g.reshape(1, -1), ln_b.reshape(1, -1))


def _gmlp_out_body(u_ref, v_ref, ws_ref, bs_ref, w_ref, x_ref, gate_ref, o_ref, wbf_ref, wsbf_ref, m_ref):
    q = B_CHUNK

    @pl.when(pl.program_id(0) == 0)
    def _():
        wbf_ref[...] = w_ref[...].astype(BF16)
        causal = lax.broadcasted_iota(I32, (q, q), 0) >= lax.broadcasted_iota(I32, (q, q), 1)
        for g in range(B_N_GROUPS):
            wsbf_ref[g] = jnp.where(causal, ws_ref[g], 0.0).astype(BF16)

    for ci in range(u_ref.shape[0] // q):
        rsl = slice(ci * q, (ci + 1) * q)
        for g in range(B_N_GROUPS):
            gsl = slice(g * B_GROUP_DIM, (g + 1) * B_GROUP_DIM)
            mixed = _dot(wsbf_ref[g], v_ref[rsl, gsl].astype(BF16)) + bs_ref[:, g:g + 1]
            m_ref[rsl, gsl] = (u_ref[rsl, gsl].astype(F32) * mixed).astype(BF16)
    o_ref[...] = x_ref[...] + gate_ref[...] * _dot(m_ref[...], wbf_ref[...])


def _gmlp_out_call(uv, w_s, b_s, w_out, x, mod, tm, name):
    t, d = x.shape
    q = B_CHUNK
    return pl.pallas_call(
        _gmlp_out_body,
        grid=(t // tm,),
        in_specs=[pl.BlockSpec((tm, B_D), lambda i: (i, 0)),
                  pl.BlockSpec((tm, B_D), lambda i: (i, 1)),
                  pl.BlockSpec((B_N_GROUPS, q, q), lambda i: (0, 0, 0)),
                  pl.BlockSpec((q, B_N_GROUPS), lambda i: (0, 0)),
                  pl.BlockSpec((B_D, d), lambda i: (0, 0)),
                  pl.BlockSpec((tm, d), lambda i: (i, 0)),
                  mod.spec(2, tm, 1)],
        out_specs=pl.BlockSpec((tm, d), lambda i: (i, 0)),
        out_shape=jax.ShapeDtypeStruct((t, d), F32),
        scratch_shapes=[pltpu.VMEM((B_D, d), BF16), pltpu.VMEM((B_N_GROUPS, q, q), BF16),
                        pltpu.VMEM((tm, B_D), BF16)],
        compiler_params=_cparams(("arbitrary",)),
        name=name,
    )(uv, uv, w_s, b_s.T, w_out, x, mod.arr)


def _gmlp_out_step_body(u_ref, v_ref, wd_ref, bd_ref, w_ref, x_ref, gate_ref, o_ref):
    mixed = v_ref[...] * wd_ref[...] + bd_ref[...]
    m = (u_ref[...] * mixed).astype(BF16)
    o_ref[...] = x_ref[...] + gate_ref[...] * _dot(m, w_ref[...].astype(BF16))


def _gmlp_out_step_call(uv, w_s, b_s, w_out, x, mod, name):
    t, d = x.shape
    wd = jnp.repeat(w_s[:, 0, 0], B_GROUP_DIM).reshape(1, B_D)
    bd = jnp.repeat(b_s[:, 0], B_GROUP_DIM).reshape(1, B_D)
    return pl.pallas_call(
        _gmlp_out_step_body,
        grid=(1,),
        in_specs=[pl.BlockSpec((t, B_D), lambda i: (0, 0)),
                  pl.BlockSpec((t, B_D), lambda i: (0, 1)),
                  pl.BlockSpec((1, B_D), lambda i: (0, 0)),
                  pl.BlockSpec((1, B_D), lambda i: (0, 0)),
                  pl.BlockSpec((B_D, d), lambda i: (0, 0)),
                  pl.BlockSpec((t, d), lambda i: (0, 0)),
                  mod.spec(2, t, 1)],
        out_specs=pl.BlockSpec((t, d), lambda i: (0, 0)),
        out_shape=jax.ShapeDtypeStruct((t, d), F32),
        compiler_params=_cparams(("arbitrary",)),
        name=name,
    )(uv, uv, wd, bd, w_out, x, mod.arr)


def _pool_matmul(pooled, wg_ref):
    outs = []
    for gi in range(len(C_WINDOWS)):
        gsl = slice(gi * C_GROUP_DIM, (gi + 1) * C_GROUP_DIM)
        outs.append(_dot(pooled[:, gsl].astype(BF16), wg_ref[gi].astype(BF16)))
    return jnp.concatenate(outs, axis=-1)


def _pool_body(x_ref, g_ref, sc_ref, sh_ref, gate_ref, prev_ref, wg_ref, scale_ref, o_ref, pnew_ref, hp_ref,
               *, tiles_per_seq, start):
    i = pl.program_id(0)
    tm = x_ref.shape[0]
    top = 16
    ti = i % tiles_per_seq

    @pl.when(ti == 0)
    def _():
        hp_ref[top - C_STATE:top, :] = prev_ref[...]

    hn = _prenorm(x_ref[...], g_ref[...], sc_ref[...], sh_ref[...])
    hp_ref[top:top + tm, :] = hn
    pos = start + ti * tm + lax.broadcasted_iota(I32, (tm, 1), 0)
    outs = []
    for gi, w in enumerate(C_WINDOWS):
        gsl = slice(gi * C_GROUP_DIM, (gi + 1) * C_GROUP_DIM)
        acc = hn[:, gsl]
        for k in range(1, w):
            acc = acc + hp_ref[top - k:top - k + tm, gsl]
        cnt = jnp.minimum(pos + 1, w).astype(F32)
        outs.append(acc / cnt - hn[:, gsl])
    y = _pool_matmul(jnp.concatenate(outs, axis=-1), wg_ref) * scale_ref[...]
    o_ref[...] = x_ref[...] + gate_ref[...] * y
    hist = hp_ref[top + tm - C_STATE:top + tm, :]
    hp_ref[top - C_STATE:top, :] = hist

    @pl.when(ti == tiles_per_seq - 1)
    def _():
        pnew_ref[...] = hist


def _pool_call(x, g, mod, prev, w_g, scale, nb, tm, start, name):
    t, d = x.shape
    tiles = t // nb // tm
    ng = len(C_WINDOWS)
    return pl.pallas_call(
        functools.partial(_pool_body, tiles_per_seq=tiles, start=start),
        grid=(t // tm,),
        in_specs=[pl.BlockSpec((tm, d), lambda i: (i, 0)),
                  pl.BlockSpec((1, d), lambda i: (0, 0)),
                  mod.spec(1, tm, 1), mod.spec(0, tm, 1), mod.spec(2, tm, 1),
                  pl.BlockSpec((None, C_STATE, d), lambda i: (i // tiles, 0, 0)),
                  pl.BlockSpec((ng, C_GROUP_DIM, C_GROUP_DIM), lambda i: (0, 0, 0)),
                  pl.BlockSpec((1, d), lambda i: (0, 0))],
        out_specs=[pl.BlockSpec((tm, d), lambda i: (i, 0)),
                   pl.BlockSpec((None, C_STATE, d), lambda i: (i // tiles, 0, 0))],
        out_shape=[jax.ShapeDtypeStruct((t, d), F32), jax.ShapeDtypeStruct((nb, C_STATE, d), F32)],
        scratch_shapes=[pltpu.VMEM((16 + tm, d), F32)],
        compiler_params=_cparams(("arbitrary",)),
        name=name,
    )(x, g.reshape(1, d), mod.arr, mod.arr, mod.arr, prev, w_g, scale.reshape(1, d))


def _pool_step_body(x_ref, g_ref, sc_ref, sh_ref, gate_ref, prev_ref, wg_ref, scale_ref, o_ref, pnew_ref, *, start):
    hn = _prenorm(x_ref[...], g_ref[...], sc_ref[...], sh_ref[...])
    outs = []
    for gi, w in enumerate(C_WINDOWS):
        gsl = slice(gi * C_GROUP_DIM, (gi + 1) * C_GROUP_DIM)
        acc = hn[:, gsl]
        for k in range(1, w):
            acc = acc + prev_ref[:, C_STATE - k, gsl]
        outs.append(acc / float(min(start + 1, w)) - hn[:, gsl])
    y = _pool_matmul(jnp.concatenate(outs, axis=-1), wg_ref) * scale_ref[...]
    o_ref[...] = x_ref[...] + gate_ref[...] * y
    for k in range(C_STATE - 1):
        pnew_ref[:, k, :] = prev_ref[:, k + 1, :]
    pnew_ref[:, C_STATE - 1, :] = hn


def _pool_step_call(x, g, mod, prev, w_g, scale, start, name):
    t, d = x.shape
    bt = 32
    ng = len(C_WINDOWS)
    return pl.pallas_call(
        functools.partial(_pool_step_body, start=start),
        grid=(t // bt,),
        in_specs=[pl.BlockSpec((bt, d), lambda i: (i, 0)),
                  pl.BlockSpec((1, d), lambda i: (0, 0)),
                  mod.spec(1, bt, 1), mod.spec(0, bt, 1), mod.spec(2, bt, 1),
                  pl.BlockSpec((bt, C_STATE, d), lambda i: (i, 0, 0)),
                  pl.BlockSpec((ng, C_GROUP_DIM, C_GROUP_DIM), lambda i: (0, 0, 0)),
                  pl.BlockSpec((1, d), lambda i: (0, 0))],
        out_specs=[pl.BlockSpec((bt, d), lambda i: (i, 0)),
                   pl.BlockSpec((bt, C_STATE, d), lambda i: (i, 0, 0))],
        out_shape=[jax.ShapeDtypeStruct((t, d), F32), jax.ShapeDtypeStruct((t, C_STATE, d), F32)],
        compiler_params=_cparams(("arbitrary",)),
        name=name,
    )(x, g.reshape(1, d), mod.arr, mod.arr, mod.arr, prev, w_g, scale.reshape(1, d))


def _route_rows(s, b):
    npg = EXPERTS_PER_GROUP
    gscore = []
    for q in range(N_EXPERT_GROUPS):
        v = b[q * npg:(q + 1) * npg]
        best = None
        for i in range(npg):
            for j in range(i + 1, npg):
                best = v[i] + v[j] if best is None else jnp.maximum(best, v[i] + v[j])
        gscore.append(best)
    gsel = jnp.zeros_like(gscore[0], dtype=I32)
    gbest = gscore[0]
    for q in range(1, N_EXPERT_GROUPS):
        better = gscore[q] > gbest
        gsel = jnp.where(better, q, gsel)
        gbest = jnp.where(better, gscore[q], gbest)
    vb, vs = [], []
    for k in range(npg):
        bk, sk = b[k], s[k]
        for q in range(1, N_EXPERT_GROUPS):
            bk = jnp.where(gsel == q, b[q * npg + k], bk)
            sk = jnp.where(gsel == q, s[q * npg + k], sk)
        vb.append(bk)
        vs.append(sk)
    i1 = jnp.zeros_like(gsel)
    m1 = vb[0]
    for k in range(1, npg):
        better = vb[k] > m1
        i1 = jnp.where(better, k, i1)
        m1 = jnp.where(better, vb[k], m1)
    i2 = jnp.full_like(gsel, -1)
    m2 = jnp.zeros_like(m1)
    for k in range(npg):
        better = (i1 != k) & ((i2 < 0) | (vb[k] > m2))
        i2 = jnp.where(better, k, i2)
        m2 = jnp.where(better, vb[k], m2)
    s1 = vs[0]
    s2 = vs[0]
    for k in range(1, npg):
        s1 = jnp.where(i1 == k, vs[k], s1)
        s2 = jnp.where(i2 == k, vs[k], s2)
    w1 = s1 / (s1 + s2)
    w2 = s2 / (s1 + s2)
    lo = jnp.minimum(i1, i2)
    hi = jnp.maximum(i1, i2)
    pair = jnp.zeros_like(gsel)
    for p, (a, c) in enumerate(PAIRS):
        pair = jnp.where((lo == a) & (hi == c), p, pair)
    first_is_lo = i1 < i2
    return (gsel * len(PAIRS) + pair, jnp.where(first_is_lo, w1, w2), jnp.where(first_is_lo, w2, w1))


def _route_body(x_ref, g_ref, sc_ref, sh_ref, rw_ref, rb_ref, cnt_in_ref, *rest, aliased):
    if aliased:
        rest = rest[3:]
    rows_ref, bucket_ref, rank_ref, cnt_ref = rest
    tm = x_ref.shape[0]

    @pl.when(pl.program_id(0) == 0)
    def _():
        cnt_ref[...] = cnt_in_ref[...]

    hn = _prenorm(x_ref[...], g_ref[...], sc_ref[...], sh_ref[...])
    logits_t = _dot3(hn, rw_ref[...]).T
    scores = _sigmoid(logits_t[0:N_EXPERTS, :])
    biased = scores + rb_ref[0:N_EXPERTS, :]
    bucket, w_lo, w_hi = _route_rows([scores[e:e + 1, :] for e in range(N_EXPERTS)],
                                     [biased[e:e + 1, :] for e in range(N_EXPERTS)])
    bucket_ref[...] = bucket
    onehot = (lax.broadcasted_iota(I32, (BUCKET_ROWS, tm), 0) == bucket).astype(F32)
    before = (lax.broadcasted_iota(I32, (tm, tm), 0) < lax.broadcasted_iota(I32, (tm, tm), 1)).astype(BF16)
    earlier = _dot(onehot.astype(BF16), before) + cnt_ref[:, 0:1]
    rank_ref[...] = jnp.sum(onehot * earlier, axis=0, keepdims=True).astype(I32)
    cnt_ref[...] = cnt_ref[...] + jnp.sum(onehot, axis=1, keepdims=True)

    sub = lax.broadcasted_iota(I32, (LANE, tm), 0)
    gate_t = jnp.where(sub == 0, w_lo, jnp.where(sub == 1, w_hi, 0.0))
    rows_ref[pl.ds(GATE_ROW, tm, stride=ROW_PITCH), :] = gate_t.T
    for j in range(ROW_PLANES):
        rows_ref[pl.ds(j, tm, stride=ROW_PITCH), :] = hn[:, j * LANE:(j + 1) * LANE]


def _route_call(x, g, mod, router_w, router_b, counts, tm, total_rows, row_offset, prior, name):
    t, d = x.shape
    blk0 = row_offset // tm
    rw = jnp.pad(router_w, ((0, 0), (0, LANE - N_EXPERTS)))
    rb = jnp.pad(router_b.reshape(-1, 1), ((0, LANE - N_EXPERTS), (0, 0)))
    in_specs = [pl.BlockSpec((tm, d), lambda i: (i, 0)),
                pl.BlockSpec((1, d), lambda i: (0, 0)),
                mod.spec(4, tm, 1), mod.spec(3, tm, 1),
                pl.BlockSpec((d, LANE), lambda i: (0, 0)),
                pl.BlockSpec((LANE, 1), lambda i: (0, 0)),
                pl.BlockSpec((BUCKET_ROWS, LANE), lambda i: (0, 0))]
    args = [x, g.reshape(1, d), mod.arr, mod.arr, rw, rb, counts]
    aliases = {}
    if prior is not None:
        in_specs += [pl.BlockSpec(memory_space=pl.ANY)] * len(prior)
        aliases = {len(args) + k: k for k in range(len(prior))}
        args += list(prior)
    return pl.pallas_call(
        functools.partial(_route_body, aliased=prior is not None),
        grid=(t // tm,),
        in_specs=in_specs,
        out_specs=[pl.BlockSpec((tm * ROW_PITCH, LANE), lambda i: (blk0 + i, 0)),
                   pl.BlockSpec((1, tm), lambda i: (0, blk0 + i)),
                   pl.BlockSpec((1, tm), lambda i: (0, blk0 + i)),
                   pl.BlockSpec((BUCKET_ROWS, LANE), lambda i: (0, 0))],
        out_shape=[jax.ShapeDtypeStruct((total_rows * ROW_PITCH, LANE), F32),
                   jax.ShapeDtypeStruct((1, total_rows), I32),
                   jax.ShapeDtypeStruct((1, total_rows), I32),
                   jax.ShapeDtypeStruct((BUCKET_ROWS, LANE), F32)],
        input_output_aliases=aliases,
        compiler_params=_cparams(("arbitrary",)),
        name=name,
    )(*args)


def _invert_body(dest_ref, init_ref, gather_ref):
    pltpu.sync_copy(init_ref, gather_ref)

    def put(t, carry):
        gather_ref[dest_ref[t]] = t
        return carry

    lax.fori_loop(0, dest_ref.shape[0], put, 0, unroll=8)


def _invert_call(dest, n_slots, name):
    smem = pl.BlockSpec(memory_space=pltpu.SMEM)
    return pl.pallas_call(
        _invert_body,
        in_specs=[smem, pl.BlockSpec(memory_space=pl.ANY)],
        out_specs=smem,
        out_shape=jax.ShapeDtypeStruct((n_slots,), I32),
        name=name,
    )(dest, jnp.zeros((n_slots,), I32))


def _unrolled(lo, hi, body, carry):
    for b in range(lo, hi):
        carry = body(b, carry)
    return carry


def _experts_body(grp_ref, lo_ref, hi_ref, fresh_ref, valid_ref, out_ref, gather_ref, rows_ref, wg_in_ref,
                  wu_in_ref, wd_in_ref, y_ref, wg_ref, wu_ref, wd_ref, xb_ref, gsem):
    i = pl.program_id(0)
    n = pl.num_programs(0)
    tile = EXPERT_TILE
    copied = GATE_ROW + 1
    group = 8
    del grp_ref, out_ref

    def start_gather(step, slot, loop):
        def body(b, carry):
            for k in range(group):
                r = b * group + k
                tok = gather_ref[step * tile + r]
                pltpu.make_async_copy(rows_ref.at[pl.ds(tok * ROW_PITCH, copied)],
                                      xb_ref.at[slot, pl.ds(r * ROW_PITCH, copied)],
                                      gsem.at[slot]).start(priority=k % 2)
            return carry

        loop(0, tile // group, body, 0)

    def wait_gather(slot):
        rows = tile * copied
        pltpu.make_async_copy(rows_ref.at[pl.ds(0, rows)], xb_ref.at[slot, pl.ds(0, rows)], gsem.at[slot]).wait()

    @pl.when(fresh_ref[i] == 1)
    def _():
        for e in range(EXPERTS_PER_GROUP):
            wg_ref[e] = wg_in_ref[e].astype(BF16)
            wu_ref[e] = wu_in_ref[e].astype(BF16)
            wd_ref[e] = wd_in_ref[e].astype(BF16)

    nxt = jnp.minimum(i + 1, n - 1)
    more = jnp.logical_and(i + 1 < n, valid_ref[nxt] == 1)

    @pl.when(valid_ref[i] == 1)
    def _():
        slot = i % GATHER_SLOTS

        @pl.when(i == 0)
        def _():
            for ahead in range(GATHER_SLOTS - 1):
                start_gather(ahead, ahead, lax.fori_loop)

        wait_gather(slot)
        ahead = i + GATHER_SLOTS - 1
        start_gather(jnp.minimum(ahead, n - 1), ahead % GATHER_SLOTS, _unrolled)
        x = jnp.concatenate([xb_ref[slot, pl.ds(j, tile, stride=ROW_PITCH), :].astype(BF16)
                             for j in range(ROW_PLANES)], axis=-1)

        def expert(e):
            act = (_silu(_dot(x, wg_ref[e])) * _dot(x, wu_ref[e])).astype(BF16)
            return _dot(act, wd_ref[e])

        w = xb_ref[slot, pl.ds(GATE_ROW, tile, stride=ROW_PITCH), :]
        y = w[:, 0:1] * expert(lo_ref[i]) + w[:, 1:2] * expert(hi_ref[i])
        for j in range(ROW_PLANES):
            y_ref[pl.ds(j, tile, stride=ROW_PITCH), :] = y[:, j * LANE:(j + 1) * LANE]

        @pl.when(jnp.logical_not(more))
        def _():
            for k in range(1, GATHER_SLOTS):
                wait_gather((i + k) % GATHER_SLOTS)


def _experts_call(grp, lo, hi, fresh, valid, out_tile, gather_idx, rows, w_gate, w_up, w_down, layer, name):
    n_tiles = grp.shape[0]
    tile = EXPERT_TILE
    d, f = D_MODEL, D_EXPERT
    npg = EXPERTS_PER_GROUP
    group_block = lambda i, grp, *_: (layer, grp[i], 0, 0)
    once = pl.Buffered(1)
    return pl.pallas_call(
        _experts_body,
        grid_spec=pltpu.PrefetchScalarGridSpec(
            num_scalar_prefetch=7,
            grid=(n_tiles,),
            in_specs=[pl.BlockSpec(memory_space=pl.ANY),
                      pl.BlockSpec((None, npg, d, f), group_block, pipeline_mode=once),
                      pl.BlockSpec((None, npg, d, f), group_block, pipeline_mode=once),
                      pl.BlockSpec((None, npg, f, d), group_block, pipeline_mode=once)],
            out_specs=pl.BlockSpec((tile * ROW_PITCH, LANE), lambda i, g, l, h, fr, va, out, *_: (out[i], 0)),
            scratch_shapes=[pltpu.VMEM((npg, d, f), BF16), pltpu.VMEM((npg, d, f), BF16),
                            pltpu.VMEM((npg, f, d), BF16),
                            pltpu.VMEM((GATHER_SLOTS, tile * ROW_PITCH, LANE), F32),
                            pltpu.SemaphoreType.DMA((GATHER_SLOTS,))]),
        out_shape=jax.ShapeDtypeStruct((n_tiles * tile * ROW_PITCH, LANE), F32),
        compiler_params=_cparams(("arbitrary",)),
        name=name,
    )(grp, lo, hi, fresh, valid, out_tile, gather_idx, rows, w_gate, w_up, w_down)


def _moe_res_body(dest_ref, y_ref, x_ref, gate_ref, fg_ref, o_ref, *rest, row_offset, final):
    if final:
        on_ref, buf_ref, sem = rest
    else:
        buf_ref, sem = rest
    i = pl.program_id(0)
    tm = x_ref.shape[0]
    group = 8

    def start_gather(step, slot, loop):
        base = row_offset + step * tm

        def body(b, carry):
            for k in range(group):
                r = b * group + k
                pltpu.make_async_copy(y_ref.at[pl.ds(dest_ref[base + r] * ROW_PITCH, ROW_PLANES)],
                                      buf_ref.at[slot, pl.ds(r * ROW_PITCH, ROW_PLANES)],
                                      sem.at[slot]).start(priority=k % 2)
            return carry

        loop(0, tm // group, body, 0)

    def wait_gather(slot):
        rows = tm * ROW_PLANES
        pltpu.make_async_copy(y_ref.at[pl.ds(0, rows)], buf_ref.at[slot, pl.ds(0, rows)], sem.at[slot]).wait()

    @pl.when(i == 0)
    def _():
        start_gather(0, 0, lax.fori_loop)

    slot = i % 2
    wait_gather(slot)

    @pl.when(i + 1 < pl.num_programs(0))
    def _():
        start_gather(i + 1, 1 - slot, lax.fori_loop)
    y = jnp.concatenate([buf_ref[slot, pl.ds(j, tm, stride=ROW_PITCH), :] for j in range(ROW_PLANES)], axis=-1)
    xn = x_ref[...] + gate_ref[...] * y
    o_ref[...] = xn
    if final:
        ms = jnp.mean(xn * xn, axis=-1, keepdims=True)
        on_ref[...] = xn * lax.rsqrt(ms + EPS) * fg_ref[...]


def _moe_res_call(dest, y_sorted, x, mod, final_g, tm, row_offset, final, name):
    t, d = x.shape
    n_out = 2 if final else 1
    gate_spec = mod.spec(5, tm, 1)
    gate_map = gate_spec.index_map
    return pl.pallas_call(
        functools.partial(_moe_res_body, row_offset=row_offset, final=final),
        grid_spec=pltpu.PrefetchScalarGridSpec(
            num_scalar_prefetch=1,
            grid=(t // tm,),
            in_specs=[pl.BlockSpec(memory_space=pl.ANY),
                      pl.BlockSpec((tm, d), lambda i, dest: (i, 0)),
                      pl.BlockSpec(gate_spec.block_shape, lambda i, dest: gate_map(i)),
                      pl.BlockSpec((1, d), lambda i, dest: (0, 0))],
            out_specs=[pl.BlockSpec((tm, d), lambda i, dest: (i, 0))] * n_out,
            scratch_shapes=[pltpu.VMEM((2, tm * ROW_PITCH, LANE), F32), pltpu.SemaphoreType.DMA((2,))]),
        out_shape=[jax.ShapeDtypeStruct((t, d), F32)] * n_out,
        compiler_params=_cparams(("arbitrary",)),
        name=name,
    )(dest, y_sorted, x, mod.arr, final_g.reshape(1, d))


def _tile_tables(counts, n_tiles):
    tile = EXPERT_TILE
    tiles_per_bucket = (counts + tile - 1) // tile
    ends = jnp.cumsum(tiles_per_bucket)
    starts = ends - tiles_per_bucket
    used = ends[-1]
    ti = jnp.arange(n_tiles, dtype=I32)
    valid = (ti < used).astype(I32)
    tile_bucket = jnp.sum((jnp.minimum(ti, used - 1)[:, None] >= ends[None, :]).astype(I32), axis=1)
    pair_lo = jnp.asarray([p[0] for p in PAIRS], I32)
    pair_hi = jnp.asarray([p[1] for p in PAIRS], I32)
    grp = tile_bucket // len(PAIRS)
    lo = pair_lo[tile_bucket % len(PAIRS)]
    hi = pair_hi[tile_bucket % len(PAIRS)]
    fresh = jnp.concatenate([jnp.ones((1,), I32), (grp[1:] != grp[:-1]).astype(I32)])
    return grp, lo, hi, fresh, valid, jnp.minimum(ti, used - 1).astype(I32), (starts * tile).astype(I32)


def _moe_layer(xp, xs, norm_g2, mod_p, mod_s, router_w, router_b, w_gate, w_up, w_down, layer, final_g, final):
    tp, ts = xp.shape[0], xs.shape[0]
    total = tp + ts
    n_tiles = -(-total // EXPERT_TILE) + N_BUCKETS
    zero_counts = jnp.zeros((BUCKET_ROWS, LANE), F32)
    rows, bucket, rank, counts = _route_call(xp, norm_g2, mod_p, router_w, router_b, zero_counts, 512, total, 0,
                                             None, "route_prompt")
    rows, bucket, rank, counts = _route_call(xs, norm_g2, mod_s, router_w, router_b, counts, ts, total, tp,
                                             (rows, bucket, rank), "route_sample")
    grp, lo, hi, fresh, valid, out_tile, bucket_start = _tile_tables(counts[:N_BUCKETS, 0].astype(I32), n_tiles)
    dest = bucket_start[bucket[0]] + rank[0]
    gather_idx = _invert_call(dest, n_tiles * EXPERT_TILE, "invert")
    y_sorted = _experts_call(grp, lo, hi, fresh, valid, out_tile, gather_idx, rows, w_gate, w_up, w_down, layer,
                             "experts")
    outp = _moe_res_call(dest, y_sorted, xp, mod_p, final_g, 512, 0, final, "moe_res_prompt")
    outs = _moe_res_call(dest, y_sorted, xs, mod_s, final_g, ts, tp, final, "moe_res_sample")
    return outp, outs


def _mamba_layer(x, g, mod, conv_prev, ssm, w_zx, w_dt, conv_w, conv_b, dt_bias, a_log, d_skip, norm_g, w_out,
                 tm, tag):
    pad_h = lambda v: jnp.pad(v.reshape(1, -1), ((0, 0), (0, LANE - A_N_HEADS)))
    tm_in = min(2 * tm, x.shape[0])
    proj = _norm_mm_call(x, g, mod, 1, 0, w_zx, tm_in, 1024, BF16, "a_in_" + tag)
    dt_raw = _norm_mm3_call(x, g, mod, 1, 0, w_dt, tm, "a_dt_" + tag)
    d_x = jnp.repeat(d_skip, A_HEAD_DIM).reshape(1, A_D_INNER)
    weights = (conv_w, conv_b.reshape(1, -1), pad_h(dt_bias), pad_h(a_log), d_x, norm_g.reshape(1, -1))
    if ssm[0] == "step":
        yn, conv_new, ssm_new = _ssd_step_call(proj, dt_raw, conv_prev, ssm[1], ssm[2], ssm[3], *weights,
                                               "ssd_step_" + tag)
    else:
        yn, conv_new, ssm_new = _ssd_call(proj, dt_raw, conv_prev, ssm[2], *weights, ssm[1], "ssd_" + tag)
    x = _out_res_call(yn, w_out, x, mod, 2, tm, "a_out_" + tag)
    return x, conv_new, ssm_new


def kernel(x_prompt, x_sample, c_prompt, c_sample, state_a_conv, state_a_ssm, state_c_pool, w_mod, b_mod, norm_g, final_g, a_w_in, a_conv_w, a_conv_b, a_dt_bias, a_log, a_d, a_norm_g, a_w_out, b_w_in, b_b_in, b_ln_g, b_ln_b, b_w_s, b_b_s, b_w_out, c_w_g, c_scale, router_w, router_b, e_w_gate, e_w_up, e_w_down):
    bp, seq, d = x_prompt.shape
    bs = x_sample.shape[0]
    n_a, n_c = state_a_conv.shape[0], state_c_pool.shape[0]
    mod_all = _mod_call(jnp.concatenate([c_prompt, c_sample], axis=0), w_mod, b_mod)
    mod_p_arr = mod_all[:, :bp].reshape(DEPTH, bp, 6, 1, d)
    mod_s_arr = mod_all[:, bp:]
    xp = x_prompt.reshape(bp * seq, d)
    xs = x_sample.reshape(bs, d)
    conv_p, ssm_p, pool_p, conv_s, pool_s, v_s = [], [], [], [], [], []
    hp = A_N_HEADS * A_HEAD_DIM
    ssm_s_in = state_a_ssm.reshape(n_a, bs, hp, A_D_STATE)
    ssm_s_out = None
    yp = ys = None
    for i in range(DEPTH):
        kind, s = LAYER_KIND[i], LAYER_SLOT[i]
        mod_p = Mod(mod_p_arr, i, False, seq)
        mod_s = Mod(mod_s_arr, i, True)
        g1 = norm_g[i, 0]
        if kind == 0:
            w_zx = a_w_in[s, :, :A_ZX].astype(BF16)
            w_dt = jnp.pad(a_w_in[s, :, A_ZX:], ((0, 0), (0, LANE - A_N_HEADS)))
            weights = (w_zx, w_dt, a_conv_w[s], a_conv_b[s], a_dt_bias[s], a_log[s], a_d[s], a_norm_g[s], a_w_out[s])
            conv0 = jnp.zeros((bp, A_CONV - 1, A_CONV_CH), F32)
            ssm0 = jnp.zeros((bp, hp, A_D_STATE), F32)
            xp, cv, ss = _mamba_layer(xp, g1, mod_p, conv0, ("prompt", bp, ssm0), *weights, 1024, "p%d" % i)
            conv_p.append(cv)
            ssm_p.append(ss.reshape(bp, A_N_HEADS, A_HEAD_DIM, A_D_STATE))
            xs, cv, ssm_s_out = _mamba_layer(xs, g1, mod_s, state_a_conv[s], ("step", ssm_s_in, s, ssm_s_out),
                                             *weights, bs, "s%d" % i)
            conv_s.append(cv)
        elif kind == 1:
            w_uv = b_w_in[s].astype(BF16)
            uv = _gmlp_in_call(xp, g1, mod_p, w_uv, b_b_in[s], b_ln_g[s], b_ln_b[s], 512, BF16, "b_in_p%d" % i)
            xp = _gmlp_out_call(uv, b_w_s[s], b_b_s[s], b_w_out[s], xp, mod_p, 512, "b_out_p%d" % i)
            uv = _gmlp_in_call(xs, g1, mod_s, w_uv, b_b_in[s], b_ln_g[s], b_ln_b[s], bs, F32, "b_in_s%d" % i)
            xs = _gmlp_out_step_call(uv, b_w_s[s], b_b_s[s], b_w_out[s], xs, mod_s, "b_out_s%d" % i)
            v_s.append(uv[:, B_D:].reshape(bs, 1, B_D))
        else:
            pool0 = jnp.zeros((bp, C_STATE, d), F32)
            xp, pr = _pool_call(xp, g1, mod_p, pool0, c_w_g[s], c_scale[s], bp, 512, 0, "pool_p%d" % i)
            pool_p.append(pr)
            xs, pr = _pool_step_call(xs, g1, mod_s, state_c_pool[s], c_w_g[s], c_scale[s], PAST_LEN, "pool_s%d" % i)
            pool_s.append(pr)
        final = i == DEPTH - 1
        outp, outs = _moe_layer(xp, xs, norm_g[i, 1], mod_p, mod_s, router_w, router_b, e_w_gate, e_w_up, e_w_down, i,
                                final_g, final)
        xp, xs = outp[0], outs[0]
        if final:
            yp, ys = outp[1], outs[1]
    return (yp.reshape(bp, seq, d), ys.reshape(bs, 1, d), jnp.stack(conv_p), jnp.stack(ssm_p), jnp.stack(pool_p),
            jnp.stack(conv_s), ssm_s_out.reshape(state_a_ssm.shape), jnp.stack(pool_s), jnp.stack(v_s))
```

```python
import functools
import math

import numpy as np
import jax
import jax.numpy as jnp
from jax import lax
from jax.experimental import pallas as pl
from jax.experimental.pallas import tpu as pltpu

F32 = jnp.float32
BF16 = jnp.bfloat16
I32 = jnp.int32
EPS = 1e-6

LANE = 128
D_MODEL = 1024
DEPTH = 4
PAST_LEN = 16384
LAYER_KIND = (0, 1, 2, 0)
LAYER_SLOT = (0, 0, 0, 1)
A_D_INNER = 2 * D_MODEL
A_HEAD_DIM = 64
A_N_HEADS = A_D_INNER // A_HEAD_DIM
A_N_GROUPS = 8
A_HPG = A_N_HEADS // A_N_GROUPS
A_D_STATE = 128
A_GN = A_N_GROUPS * A_D_STATE
A_CONV = 4
A_CONV_CH = A_D_INNER + 2 * A_GN
A_ZX = A_D_INNER + A_CONV_CH
A_CHUNK = 128
A_GW = A_HPG * A_HEAD_DIM
B_D = 2 * D_MODEL
B_N_GROUPS = 8
B_GROUP_DIM = B_D // B_N_GROUPS
B_CHUNK = 128
C_WINDOWS = (2, 4, 8, 16)
C_GROUP_DIM = D_MODEL // len(C_WINDOWS)
C_STATE = max(C_WINDOWS) - 1
N_EXPERTS = 16
N_EXPERT_GROUPS = 4
EXPERTS_PER_GROUP = 4
D_EXPERT = D_MODEL // 2
PAIRS = ((0, 1), (0, 2), (0, 3), (1, 2), (1, 3), (2, 3))
N_BUCKETS = N_EXPERT_GROUPS * len(PAIRS)
BUCKET_ROWS = 32
ROW_PLANES = D_MODEL // LANE
ROW_PITCH = ROW_PLANES + 4
GATE_ROW = ROW_PLANES
EXPERT_TILE = 256
GATHER_SLOTS = 6
VMEM_LIMIT = 56 * 1024 * 1024


def _cparams(sem, vmem=VMEM_LIMIT):
    return pltpu.CompilerParams(dimension_semantics=sem, vmem_limit_bytes=vmem)


def _sigmoid(x):
    return 1.0 / (1.0 + jnp.exp(-x))


def _silu(x):
    return x * _sigmoid(x)


def _gelu_tanh(x):
    c = 2.0 * math.sqrt(2.0 / math.pi)
    return x / (1.0 + jnp.exp(x * (-c - (c * 0.044715) * (x * x))))


def _softplus(x):
    return jnp.maximum(x, 0.0) + jnp.log1p(jnp.exp(-jnp.abs(x)))


def _split2(a):
    hi = a.astype(BF16)
    lo = (a - hi.astype(F32)).astype(BF16)
    return hi, lo


def _split3(a):
    hi = a.astype(BF16)
    r = a - hi.astype(F32)
    mid = r.astype(BF16)
    lo = (r - mid.astype(F32)).astype(BF16)
    return hi, mid, lo


def _dot(a, b):
    return jnp.dot(a, b, preferred_element_type=F32)


def _dot_nt(a, b):
    return lax.dot_general(a, b, (((1,), (1,)), ((), ())), preferred_element_type=F32)


def _dot_tn(a, b):
    return lax.dot_general(a, b, (((0,), (0,)), ((), ())), preferred_element_type=F32)


def _dot3(a, b):
    a_hi, a_lo = _split2(a)
    b_hi, b_lo = _split2(b)
    return _dot(a_hi, b_hi) + (_dot(a_lo, b_hi) + _dot(a_hi, b_lo))


def _prenorm(x, g, sc, sh):
    ms = jnp.mean(x * x, axis=-1, keepdims=True)
    return (x * lax.rsqrt(ms + EPS) * g) * (1.0 + sc) + sh


def _mod_body(c_ref, w_ref, b_ref, o_ref):
    o_ref[...] = _dot3(_silu(c_ref[...]), w_ref[...]) + b_ref[...]


def _mod_call(c_all, w_mod, b_mod):
    nb, d = c_all.shape
    depth, _, n = w_mod.shape
    tn = 1536
    return pl.pallas_call(
        _mod_body,
        grid=(depth, n // tn),
        in_specs=[pl.BlockSpec((nb, d), lambda i, j: (0, 0)),
                  pl.BlockSpec((None, d, tn), lambda i, j: (i, 0, j)),
                  pl.BlockSpec((None, 1, tn), lambda i, j: (i, 0, j))],
        out_specs=pl.BlockSpec((None, nb, tn), lambda i, j: (i, 0, j)),
        out_shape=jax.ShapeDtypeStruct((depth, nb, n), F32),
        compiler_params=_cparams(("arbitrary", "arbitrary")),
        name="mod",
    )(c_all, w_mod, b_mod.reshape(depth, 1, n))


class Mod:
    def __init__(self, arr, layer, per_row, rows_per_seq=None):
        self.arr, self.layer, self.per_row, self.rows_per_seq = arr, layer, per_row, rows_per_seq

    def spec(self, which, tm, ngrid):
        layer = self.layer
        if self.per_row:
            if ngrid == 1:
                return pl.BlockSpec((None, tm, D_MODEL), lambda i: (layer, i, which))
            return pl.BlockSpec((None, tm, D_MODEL), lambda i, j: (layer, i, which))
        tiles = self.rows_per_seq // tm
        if ngrid == 1:
            return pl.BlockSpec((None, None, None, 1, D_MODEL), lambda i: (layer, i // tiles, which, 0, 0))
        return pl.BlockSpec((None, None, None, 1, D_MODEL), lambda i, j: (layer, i // tiles, which, 0, 0))


def _norm_mm_body(x_ref, g_ref, sc_ref, sh_ref, w_ref, o_ref, hn_ref):
    @pl.when(pl.program_id(1) == 0)
    def _():
        hn_ref[...] = _prenorm(x_ref[...], g_ref[...], sc_ref[...], sh_ref[...]).astype(BF16)

    o_ref[...] = _dot(hn_ref[...], w_ref[...]).astype(o_ref.dtype)


def _norm_mm_call(x, g, mod, which_sc, which_sh, w, tm, tn, out_dtype, name):
    t, d = x.shape
    n_cols = w.shape[1]
    return pl.pallas_call(
        _norm_mm_body,
        grid=(t // tm, n_cols // tn),
        in_specs=[pl.BlockSpec((tm, d), lambda i, j: (i, 0)),
                  pl.BlockSpec((1, d), lambda i, j: (0, 0)),
                  mod.spec(which_sc, tm, 2), mod.spec(which_sh, tm, 2),
                  pl.BlockSpec((d, tn), lambda i, j: (0, j))],
        out_specs=pl.BlockSpec((tm, tn), lambda i, j: (i, j)),
        out_shape=jax.ShapeDtypeStruct((t, n_cols), out_dtype),
        scratch_shapes=[pltpu.VMEM((tm, d), BF16)],
        compiler_params=_cparams(("arbitrary", "arbitrary")),
        name=name,
    )(x, g.reshape(1, d), mod.arr, mod.arr, w)


def _norm_mm3_body(x_ref, g_ref, sc_ref, sh_ref, w_ref, o_ref):
    hn = _prenorm(x_ref[...], g_ref[...], sc_ref[...], sh_ref[...])
    o_ref[...] = _dot3(hn, w_ref[...])


def _norm_mm3_call(x, g, mod, which_sc, which_sh, w, tm, name):
    t, d = x.shape
    n = w.shape[1]
    return pl.pallas_call(
        _norm_mm3_body,
        grid=(t // tm,),
        in_specs=[pl.BlockSpec((tm, d), lambda i: (i, 0)),
                  pl.BlockSpec((1, d), lambda i: (0, 0)),
                  mod.spec(which_sc, tm, 1), mod.spec(which_sh, tm, 1),
                  pl.BlockSpec((d, n), lambda i: (0, 0))],
        out_specs=pl.BlockSpec((tm, n), lambda i: (i, 0)),
        out_shape=jax.ShapeDtypeStruct((t, n), F32),
        compiler_params=_cparams(("arbitrary",)),
        name=name,
    )(x, g.reshape(1, d), mod.arr, mod.arr, w)


def _out_res_body(y_ref, w_ref, x_ref, gate_ref, o_ref, wbf_ref):
    @pl.when(pl.program_id(0) == 0)
    def _():
        wbf_ref[...] = w_ref[...].astype(BF16)

    o_ref[...] = x_ref[...] + gate_ref[...] * _dot(y_ref[...], wbf_ref[...])


def _out_res_call(y, w, x, mod, which_gate, tm, name):
    t, k = y.shape
    d = x.shape[1]
    return pl.pallas_call(
        _out_res_body,
        grid=(t // tm,),
        in_specs=[pl.BlockSpec((tm, k), lambda i: (i, 0)),
                  pl.BlockSpec((k, d), lambda i: (0, 0)),
                  pl.BlockSpec((tm, d), lambda i: (i, 0)),
                  mod.spec(which_gate, tm, 1)],
        out_specs=pl.BlockSpec((tm, d), lambda i: (i, 0)),
        out_shape=jax.ShapeDtypeStruct((t, d), F32),
        scratch_shapes=[pltpu.VMEM((k, d), BF16)],
        compiler_params=_cparams(("arbitrary",)),
        name=name,
    )(y, w, x, mod.arr)


def _head_expand():
    h = np.arange(LANE)[:, None]
    c = np.arange(A_D_INNER)[None, :]
    return jnp.asarray((c // A_HEAD_DIM == h).astype(np.float32), dtype=BF16)


def _gate_norm(y, z, ng):
    gated = y * _silu(z)
    ms = jnp.mean(gated * gated, axis=-1, keepdims=True)
    return gated * lax.rsqrt(ms + EPS) * ng


def _ssd_body(z_ref, xs_ref, bc_ref, dt_ref, cprev_ref, sprev_ref, cw_ref, cb_ref, dtb_ref, alog_ref,
              dx_ref, ng_ref, exp_ref, yn_ref, cnew_ref, snew_ref, tail_ref, act_ref):
    c = pl.program_id(1)
    q = A_CHUNK
    nt = A_CONV - 1

    @pl.when(c == 0)
    def _():
        tail_ref[0:nt, :] = cprev_ref[...]
        snew_ref[...] = sprev_ref[...]

    row = lax.broadcasted_iota(I32, (q, q), 0)
    col = lax.broadcasted_iota(I32, (q, q), 1)
    shifts = [jnp.where(row - col == nt - k, 1.0, 0.0).astype(BF16) for k in range(nt)]
    sub = lax.broadcasted_iota(I32, (8, 1), 0)
    cw = 512
    for j in range(A_CONV_CH // cw):
        sl = slice(j * cw, (j + 1) * cw)
        src = xs_ref if (j + 1) * cw <= A_D_INNER else bc_ref
        off = j * cw if src is xs_ref else j * cw - A_D_INNER
        x_bf = src[:, off:off + cw]
        conv = cb_ref[:, sl] + x_bf.astype(F32) * cw_ref[nt:nt + 1, sl]
        for k in range(nt):
            conv = conv + _dot(shifts[k], x_bf) * cw_ref[k:k + 1, sl]
        act_ref[:, sl] = _silu(conv)
        corr = jnp.zeros((8, cw), F32)
        for l in range(nt):
            c_l = sum(tail_ref[l + k:l + k + 1, sl] * cw_ref[k:k + 1, sl] for k in range(nt - l))
            corr = jnp.where(sub == l, c_l, corr)
        act_ref[0:8, sl] = _silu(conv[0:8, :] + corr)
        tail_ref[0:nt, sl] = x_bf[q - 8:q, :].astype(F32)[8 - nt:8, :]

    @pl.when(c == pl.num_programs(1) - 1)
    def _():
        cnew_ref[...] = tail_ref[0:nt, :]

    dt = _softplus(dt_ref[...] + dtb_ref[...])
    a = dt * (-jnp.exp(alog_ref[...]))
    row = lax.broadcasted_iota(I32, (q, q), 0)
    col = lax.broadcasted_iota(I32, (q, q), 1)
    causal = row >= col
    tril = jnp.where(causal, 1.0, 0.0).astype(BF16)
    a_hi, a_mid, a_lo = _split3(a)
    cs = _dot(tril, a_hi) + (_dot(tril, a_mid) + _dot(tril, a_lo))
    cs_t = cs.T
    dt_t = dt.T
    cs_last = cs[q - 1:q, :]
    ecs = jnp.exp(cs)
    wend = jnp.exp(cs_last - cs) * dt
    st_hi, st_lo = _split2(jnp.concatenate([ecs, wend], axis=0))
    st_x = _dot(st_hi, exp_ref[...]) + _dot(st_lo, exp_ref[...])
    lane_head = lax.broadcasted_iota(I32, (q, A_GW), 1) // A_HEAD_DIM

    for g in range(A_N_GROUPS):
        gsl = slice(g * A_GW, (g + 1) * A_GW)
        b_g = act_ref[:, A_D_INNER + g * A_D_STATE:A_D_INNER + (g + 1) * A_D_STATE].astype(BF16)
        c_g = act_ref[:, A_D_INNER + A_GN + g * A_D_STATE:A_D_INNER + A_GN + (g + 1) * A_D_STATE].astype(BF16)
        x_g = act_ref[:, gsl]
        x_bf = x_g.astype(BF16)
        h_g = snew_ref[gsl, :]
        cb = _dot_nt(c_g, b_g)
        y = jnp.zeros((q, A_GW), F32)
        for r in range(A_HPG):
            h = g * A_HPG + r
            seg = cs[:, h:h + 1] - cs_t[h:h + 1, :]
            decay = jnp.exp(jnp.where(causal, seg, -1e30))
            wm = (cb * decay * dt_t[h:h + 1, :]).astype(BF16)
            y = y + _dot(wm, jnp.where(lane_head == r, x_bf, jnp.zeros_like(x_bf)))
        y = y + st_x[0:q, gsl] * _dot_nt(c_g, h_g.astype(BF16)) + dx_ref[:, gsl] * x_g
        yn_ref[:, gsl] = _gate_norm(y, z_ref[:, gsl].astype(F32), ng_ref[:, gsl]).astype(BF16)
        s_new = _dot_tn((x_g * st_x[q:2 * q, gsl]).astype(BF16), b_g)
        for r in range(A_HPG):
            h = g * A_HPG + r
            rsl = slice(g * A_GW + r * A_HEAD_DIM, g * A_GW + (r + 1) * A_HEAD_DIM)
            keep = jnp.exp(cs[q - 1:q, h:h + 1])
            snew_ref[rsl, :] = snew_ref[rsl, :] * keep + s_new[r * A_HEAD_DIM:(r + 1) * A_HEAD_DIM, :]


def _ssd_call(proj, dt_raw, conv_prev, ssm_prev, conv_w, conv_b, dt_bias, a_log, d_x, norm_g, nb, name):
    t = proj.shape[0]
    q = A_CHUNK
    nc = t // nb // q
    hp = A_N_HEADS * A_HEAD_DIM
    row = lambda b, c: (b * nc + c, 0)
    full = lambda shape: pl.BlockSpec(shape, lambda b, c: (0,) * len(shape))
    return pl.pallas_call(
        _ssd_body,
        grid=(nb, nc),
        in_specs=[pl.BlockSpec((q, A_D_INNER), lambda b, c: (b * nc + c, 0)),
                  pl.BlockSpec((q, A_D_INNER), lambda b, c: (b * nc + c, 1)),
                  pl.BlockSpec((q, 2 * A_GN), lambda b, c: (b * nc + c, 2)),
                  pl.BlockSpec((q, LANE), row),
                  pl.BlockSpec((None, A_CONV - 1, A_CONV_CH), lambda b, c: (b, 0, 0)),
                  pl.BlockSpec((None, hp, A_D_STATE), lambda b, c: (b, 0, 0)),
                  full((A_CONV, A_CONV_CH)), full((1, A_CONV_CH)), full((1, LANE)), full((1, LANE)),
                  full((1, A_D_INNER)), full((1, A_D_INNER)), full((LANE, A_D_INNER))],
        out_specs=[pl.BlockSpec((q, A_D_INNER), row),
                   pl.BlockSpec((None, A_CONV - 1, A_CONV_CH), lambda b, c: (b, 0, 0)),
                   pl.BlockSpec((None, hp, A_D_STATE), lambda b, c: (b, 0, 0))],
        out_shape=[jax.ShapeDtypeStruct((t, A_D_INNER), BF16),
                   jax.ShapeDtypeStruct((nb, A_CONV - 1, A_CONV_CH), F32),
                   jax.ShapeDtypeStruct((nb, hp, A_D_STATE), F32)],
        scratch_shapes=[pltpu.VMEM((8, A_CONV_CH), F32), pltpu.VMEM((q, A_CONV_CH), F32)],
        compiler_params=_cparams(("arbitrary", "arbitrary")),
        name=name,
    )(proj, proj, proj, dt_raw, conv_prev, ssm_prev, conv_w, conv_b, dt_bias, a_log, d_x, norm_g, _head_expand())


def _ssd_step_body(z_ref, xs_ref, bc_ref, dt_ref, cprev_ref, sprev_ref, cw_ref, cb_ref, dtb_ref, alog_ref,
                   dx_ref, ng_ref, exp_ref, yn_ref, cnew_ref, snew_ref, da_ref, y_ref):
    bt = z_ref.shape[0]
    cur = jnp.concatenate([xs_ref[...], bc_ref[...]], axis=1).astype(F32)
    conv = cb_ref[...] + cur * cw_ref[A_CONV - 1:A_CONV, :]
    for k in range(A_CONV - 1):
        conv = conv + cprev_ref[:, k, :] * cw_ref[k:k + 1, :]
    for k in range(A_CONV - 2):
        cnew_ref[:, k, :] = cprev_ref[:, k + 1, :]
    cnew_ref[:, A_CONV - 2, :] = cur
    act = _silu(conv)
    xs = act[:, 0:A_D_INNER]
    bm_bf = act[:, A_D_INNER:A_D_INNER + A_GN].astype(BF16)
    cm_bf = act[:, A_D_INNER + A_GN:A_CONV_CH].astype(BF16)
    dt = _softplus(dt_ref[...] + dtb_ref[...])
    da_ref[...] = jnp.exp(dt * (-jnp.exp(alog_ref[...])))
    dt_hi, dt_lo = _split2(dt)
    dt_x = _dot(dt_hi, exp_ref[...]) + _dot(dt_lo, exp_ref[...])
    xdt = xs * dt_x
    rows = lax.broadcasted_iota(I32, (bt, 1), 0)
    y_ref[...] = jnp.zeros_like(y_ref)

    def per_seq(j, carry):
        mine = rows == j
        xdt_j = jnp.where(mine, xdt, 0.0)
        da_j = da_ref[pl.ds(j, 1), :]
        for g in range(A_N_GROUPS):
            gsl = slice(g * A_GW, (g + 1) * A_GW)
            nsl = slice(g * A_D_STATE, (g + 1) * A_D_STATE)
            outer = _dot_tn(xdt_j[:, gsl].astype(BF16), bm_bf[:, nsl])
            for r in range(A_HPG):
                h = g * A_HPG + r
                rsl = slice(r * A_HEAD_DIM, (r + 1) * A_HEAD_DIM)
                hsl = slice(g * A_GW + r * A_HEAD_DIM, g * A_GW + (r + 1) * A_HEAD_DIM)
                snew_ref[j, hsl, :] = sprev_ref[j, hsl, :] * da_j[:, h:h + 1] + outer[rsl, :]
            yg = _dot_nt(cm_bf[:, nsl], snew_ref[j, gsl, :].astype(BF16))
            y_ref[:, gsl] = y_ref[:, gsl] + jnp.where(mine, yg, 0.0)
        return carry

    lax.fori_loop(0, bt, per_seq, 0)
    y = y_ref[...] + dx_ref[...] * xs
    z = z_ref[...].astype(F32)
    for g in range(A_N_GROUPS):
        gsl = slice(g * A_GW, (g + 1) * A_GW)
        yn_ref[:, gsl] = _gate_norm(y[:, gsl], z[:, gsl], ng_ref[:, gsl]).astype(BF16)


_SSD_STEP_INPUTS = 13


def _ssd_step_aliased_body(*refs):
    _ssd_step_body(*refs[:_SSD_STEP_INPUTS], *refs[_SSD_STEP_INPUTS + 1:])


def _ssd_step_call(proj, dt_raw, conv_prev, ssm_all, slot, ssm_out, conv_w, conv_b, dt_bias, a_log, d_x, norm_g,
                   name):
    nb = proj.shape[0]
    bt = 8
    hp = A_N_HEADS * A_HEAD_DIM
    full = lambda shape: pl.BlockSpec(shape, lambda i: (0,) * len(shape))
    state_spec = pl.BlockSpec((None, bt, hp, A_D_STATE), lambda i: (slot, i, 0, 0))
    in_specs = [pl.BlockSpec((bt, A_D_INNER), lambda i: (i, 0)),
                pl.BlockSpec((bt, A_D_INNER), lambda i: (i, 1)),
                pl.BlockSpec((bt, 2 * A_GN), lambda i: (i, 2)),
                pl.BlockSpec((bt, LANE), lambda i: (i, 0)),
                pl.BlockSpec((bt, A_CONV - 1, A_CONV_CH), lambda i: (i, 0, 0)),
                state_spec,
                full((A_CONV, A_CONV_CH)), full((1, A_CONV_CH)), full((1, LANE)), full((1, LANE)),
                full((1, A_D_INNER)), full((1, A_D_INNER)), full((LANE, A_D_INNER))]
    args = [proj, proj, proj, dt_raw, conv_prev, ssm_all, conv_w, conv_b, dt_bias, a_log, d_x, norm_g, _head_expand()]
    assert len(args) == _SSD_STEP_INPUTS
    aliases = {}
    if ssm_out is not None:
        in_specs.append(pl.BlockSpec(memory_space=pl.ANY))
        args.append(ssm_out)
        aliases = {_SSD_STEP_INPUTS: 2}
    return pl.pallas_call(
        _ssd_step_body if ssm_out is None else _ssd_step_aliased_body,
        grid=(nb // bt,),
        in_specs=in_specs,
        out_specs=[pl.BlockSpec((bt, A_D_INNER), lambda i: (i, 0)),
                   pl.BlockSpec((bt, A_CONV - 1, A_CONV_CH), lambda i: (i, 0, 0)),
                   state_spec],
        out_shape=[jax.ShapeDtypeStruct((nb, A_D_INNER), BF16),
                   jax.ShapeDtypeStruct((nb, A_CONV - 1, A_CONV_CH), F32),
                   jax.ShapeDtypeStruct(ssm_all.shape, F32)],
        scratch_shapes=[pltpu.VMEM((bt, LANE), F32), pltpu.VMEM((bt, A_D_INNER), F32)],
        input_output_aliases=aliases,
        compiler_params=_cparams(("arbitrary",)),
        name=name,
    )(*args)


def _gmlp_in_body(x_ref, g_ref, sc_ref, sh_ref, w_ref, b_ref, lg_ref, lb_ref, o_ref, hn_ref):
    j = pl.program_id(1)

    @pl.when(j == 0)
    def _():
        hn_ref[...] = _prenorm(x_ref[...], g_ref[...], sc_ref[...], sh_ref[...]).astype(BF16)

    uv = _gelu_tanh(_dot(hn_ref[...], w_ref[...]) + b_ref[...])

    @pl.when(j == 0)
    def _():
        o_ref[...] = uv.astype(o_ref.dtype)

    @pl.when(j == 1)
    def _():
        vc = uv - jnp.mean(uv, axis=-1, keepdims=True)
        var = jnp.mean(vc * vc, axis=-1, keepdims=True)
        o_ref[...] = (vc * lax.rsqrt(var + EPS) * lg_ref[...] + lb_ref[...]).astype(o_ref.dtype)


def _gmlp_in_call(x, g, mod, w, b, ln_g, ln_b, tm, out_dtype, name):
    t, d = x.shape
    return pl.pallas_call(
        _gmlp_in_body,
        grid=(t // tm, 2),
        in_specs=[pl.BlockSpec((tm, d), lambda i, j: (i, 0)),
                  pl.BlockSpec((1, d), lambda i, j: (0, 0)),
                  mod.spec(1, tm, 2), mod.spec(0, tm, 2),
                  pl.BlockSpec((d, B_D), lambda i, j: (0, j)),
                  pl.BlockSpec((1, B_D), lambda i, j: (0, j)),
                  pl.BlockSpec((1, B_D), lambda i, j: (0, 0)),
                  pl.BlockSpec((1, B_D), lambda i, j: (0, 0))],
        out_specs=pl.BlockSpec((tm, B_D), lambda i, j: (i, j)),
        out_shape=jax.ShapeDtypeStruct((t, 2 * B_D), out_dtype),
        scratch_shapes=[pltpu.VMEM((tm, d), BF16)],
        compiler_params=_cparams(("arbitrary", "arbitrary")),
        name=name,
    )(x, g.reshape(1, d), mod.arr, mod.arr, w, b.reshape(1, -1), ln_g.reshape(1, -1), ln_b.reshape(1, -1))


def _gmlp_out_body(u_ref, v_ref, ws_ref, bs_ref, w_ref, x_ref, gate_ref, o_ref, wbf_ref, wsbf_ref, m_ref):
    q = B_CHUNK

    @pl.when(pl.program_id(0) == 0)
    def _():
        wbf_ref[...] = w_ref[...].astype(BF16)
        causal = lax.broadcasted_iota(I32, (q, q), 0) >= lax.broadcasted_iota(I32, (q, q), 1)
        for g in range(B_N_GROUPS):
            wsbf_ref[g] = jnp.where(causal, ws_ref[g], 0.0).astype(BF16)

    for ci in range(u_ref.shape[0] // q):
        rsl = slice(ci * q, (ci + 1) * q)
        for g in range(B_N_GROUPS):
            gsl = slice(g * B_GROUP_DIM, (g + 1) * B_GROUP_DIM)
            mixed = _dot(wsbf_ref[g], v_ref[rsl, gsl].astype(BF16)) + bs_ref[:, g:g + 1]
            m_ref[rsl, gsl] = (u_ref[rsl, gsl].astype(F32) * mixed).astype(BF16)
    o_ref[...] = x_ref[...] + gate_ref[...] * _dot(m_ref[...], wbf_ref[...])


def _gmlp_out_call(uv, w_s, b_s, w_out, x, mod, tm, name):
    t, d = x.shape
    q = B_CHUNK
    return pl.pallas_call(
        _gmlp_out_body,
        grid=(t // tm,),
        in_specs=[pl.BlockSpec((tm, B_D), lambda i: (i, 0)),
                  pl.BlockSpec((tm, B_D), lambda i: (i, 1)),
                  pl.BlockSpec((B_N_GROUPS, q, q), lambda i: (0, 0, 0)),
                  pl.BlockSpec((q, B_N_GROUPS), lambda i: (0, 0)),
                  pl.BlockSpec((B_D, d), lambda i: (0, 0)),
                  pl.BlockSpec((tm, d), lambda i: (i, 0)),
                  mod.spec(2, tm, 1)],
        out_specs=pl.BlockSpec((tm, d), lambda i: (i, 0)),
        out_shape=jax.ShapeDtypeStruct((t, d), F32),
        scratch_shapes=[pltpu.VMEM((B_D, d), BF16), pltpu.VMEM((B_N_GROUPS, q, q), BF16),
                        pltpu.VMEM((tm, B_D), BF16)],
        compiler_params=_cparams(("arbitrary",)),
        name=name,
    )(uv, uv, w_s, b_s.T, w_out, x, mod.arr)


def _gmlp_out_step_body(u_ref, v_ref, wd_ref, bd_ref, w_ref, x_ref, gate_ref, o_ref):
    mixed = v_ref[...] * wd_ref[...] + bd_ref[...]
    m = (u_ref[...] * mixed).astype(BF16)
    o_ref[...] = x_ref[...] + gate_ref[...] * _dot(m, w_ref[...].astype(BF16))


def _gmlp_out_step_call(uv, w_s, b_s, w_out, x, mod, name):
    t, d = x.shape
    wd = jnp.repeat(w_s[:, 0, 0], B_GROUP_DIM).reshape(1, B_D)
    bd = jnp.repeat(b_s[:, 0], B_GROUP_DIM).reshape(1, B_D)
    return pl.pallas_call(
        _gmlp_out_step_body,
        grid=(1,),
        in_specs=[pl.BlockSpec((t, B_D), lambda i: (0, 0)),
                  pl.BlockSpec((t, B_D), lambda i: (0, 1)),
                  pl.BlockSpec((1, B_D), lambda i: (0, 0)),
                  pl.BlockSpec((1, B_D), lambda i: (0, 0)),
                  pl.BlockSpec((B_D, d), lambda i: (0, 0)),
                  pl.BlockSpec((t, d), lambda i: (0, 0)),
                  mod.spec(2, t, 1)],
        out_specs=pl.BlockSpec((t, d), lambda i: (0, 0)),
        out_shape=jax.ShapeDtypeStruct((t, d), F32),
        compiler_params=_cparams(("arbitrary",)),
        name=name,
    )(uv, uv, wd, bd, w_out, x, mod.arr)


def _pool_matmul(pooled, wg_ref):
    outs = []
    for gi in range(len(C_WINDOWS)):
        gsl = slice(gi * C_GROUP_DIM, (gi + 1) * C_GROUP_DIM)
        outs.append(_dot(pooled[:, gsl].astype(BF16), wg_ref[gi].astype(BF16)))
    return jnp.concatenate(outs, axis=-1)


def _pool_body(x_ref, g_ref, sc_ref, sh_ref, gate_ref, prev_ref, wg_ref, scale_ref, o_ref, pnew_ref, hp_ref,
               *, tiles_per_seq, start):
    i = pl.program_id(0)
    tm = x_ref.shape[0]
    top = 16
    ti = i % tiles_per_seq

    @pl.when(ti == 0)
    def _():
        hp_ref[top - C_STATE:top, :] = prev_ref[...]

    hn = _prenorm(x_ref[...], g_ref[...], sc_ref[...], sh_ref[...])
    hp_ref[top:top + tm, :] = hn
    pos = start + ti * tm + lax.broadcasted_iota(I32, (tm, 1), 0)
    outs = []
    for gi, w in enumerate(C_WINDOWS):
        gsl = slice(gi * C_GROUP_DIM, (gi + 1) * C_GROUP_DIM)
        acc = hn[:, gsl]
        for k in range(1, w):
            acc = acc + hp_ref[top - k:top - k + tm, gsl]
        cnt = jnp.minimum(pos + 1, w).astype(F32)
        outs.append(acc / cnt - hn[:, gsl])
    y = _pool_matmul(jnp.concatenate(outs, axis=-1), wg_ref) * scale_ref[...]
    o_ref[...] = x_ref[...] + gate_ref[...] * y
    hist = hp_ref[top + tm - C_STATE:top + tm, :]
    hp_ref[top - C_STATE:top, :] = hist

    @pl.when(ti == tiles_per_seq - 1)
    def _():
        pnew_ref[...] = hist


def _pool_call(x, g, mod, prev, w_g, scale, nb, tm, start, name):
    t, d = x.shape
    tiles = t // nb // tm
    ng = len(C_WINDOWS)
    return pl.pallas_call(
        functools.partial(_pool_body, tiles_per_seq=tiles, start=start),
        grid=(t // tm,),
        in_specs=[pl.BlockSpec((tm, d), lambda i: (i, 0)),
                  pl.BlockSpec((1, d), lambda i: (0, 0)),
                  mod.spec(1, tm, 1), mod.spec(0, tm, 1), mod.spec(2, tm, 1),
                  pl.BlockSpec((None, C_STATE, d), lambda i: (i // tiles, 0, 0)),
                  pl.BlockSpec((ng, C_GROUP_DIM, C_GROUP_DIM), lambda i: (0, 0, 0)),
                  pl.BlockSpec((1, d), lambda i: (0, 0))],
        out_specs=[pl.BlockSpec((tm, d), lambda i: (i, 0)),
                   pl.BlockSpec((None, C_STATE, d), lambda i: (i // tiles, 0, 0))],
        out_shape=[jax.ShapeDtypeStruct((t, d), F32), jax.ShapeDtypeStruct((nb, C_STATE, d), F32)],
        scratch_shapes=[pltpu.VMEM((16 + tm, d), F32)],
        compiler_params=_cparams(("arbitrary",)),
        name=name,
    )(x, g.reshape(1, d), mod.arr, mod.arr, mod.arr, prev, w_g, scale.reshape(1, d))


def _pool_step_body(x_ref, g_ref, sc_ref, sh_ref, gate_ref, prev_ref, wg_ref, scale_ref, o_ref, pnew_ref, *, start):
    hn = _prenorm(x_ref[...], g_ref[...], sc_ref[...], sh_ref[...])
    outs = []
    for gi, w in enumerate(C_WINDOWS):
        gsl = slice(gi * C_GROUP_DIM, (gi + 1) * C_GROUP_DIM)
        acc = hn[:, gsl]
        for k in range(1, w):
            acc = acc + prev_ref[:, C_STATE - k, gsl]
        outs.append(acc / float(min(start + 1, w)) - hn[:, gsl])
    y = _pool_matmul(jnp.concatenate(outs, axis=-1), wg_ref) * scale_ref[...]
    o_ref[...] = x_ref[...] + gate_ref[...] * y
    for k in range(C_STATE - 1):
        pnew_ref[:, k, :] = prev_ref[:, k + 1, :]
    pnew_ref[:, C_STATE - 1, :] = hn


def _pool_step_call(x, g, mod, prev, w_g, scale, start, name):
    t, d = x.shape
    bt = 32
    ng = len(C_WINDOWS)
    return pl.pallas_call(
        functools.partial(_pool_step_body, start=start),
        grid=(t // bt,),
        in_specs=[pl.BlockSpec((bt, d), lambda i: (i, 0)),
                  pl.BlockSpec((1, d), lambda i: (0, 0)),
                  mod.spec(1, bt, 1), mod.spec(0, bt, 1), mod.spec(2, bt, 1),
                  pl.BlockSpec((bt, C_STATE, d), lambda i: (i, 0, 0)),
                  pl.BlockSpec((ng, C_GROUP_DIM, C_GROUP_DIM), lambda i: (0, 0, 0)),
                  pl.BlockSpec((1, d), lambda i: (0, 0))],
        out_specs=[pl.BlockSpec((bt, d), lambda i: (i, 0)),
                   pl.BlockSpec((bt, C_STATE, d), lambda i: (i, 0, 0))],
        out_shape=[jax.ShapeDtypeStruct((t, d), F32), jax.ShapeDtypeStruct((t, C_STATE, d), F32)],
        compiler_params=_cparams(("arbitrary",)),
        name=name,
    )(x, g.reshape(1, d), mod.arr, mod.arr, mod.arr, prev, w_g, scale.reshape(1, d))


def _route_rows(s, b):
    npg = EXPERTS_PER_GROUP
    gscore = []
    for q in range(N_EXPERT_GROUPS):
        v = b[q * npg:(q + 1) * npg]
        best = None
        for i in range(npg):
            for j in range(i + 1, npg):
                best = v[i] + v[j] if best is None else jnp.maximum(best, v[i] + v[j])
        gscore.append(best)
    gsel = jnp.zeros_like(gscore[0], dtype=I32)
    gbest = gscore[0]
    for q in range(1, N_EXPERT_GROUPS):
        better = gscore[q] > gbest
        gsel = jnp.where(better, q, gsel)
        gbest = jnp.where(better, gscore[q], gbest)
    vb, vs = [], []
    for k in range(npg):
        bk, sk = b[k], s[k]
        for q in range(1, N_EXPERT_GROUPS):
            bk = jnp.where(gsel == q, b[q * npg + k], bk)
            sk = jnp.where(gsel == q, s[q * npg + k], sk)
        vb.append(bk)
        vs.append(sk)
    i1 = jnp.zeros_like(gsel)
    m1 = vb[0]
    for k in range(1, npg):
        better = vb[k] > m1
        i1 = jnp.where(better, k, i1)
        m1 = jnp.where(better, vb[k], m1)
    i2 = jnp.full_like(gsel, -1)
    m2 = jnp.zeros_like(m1)
    for k in range(npg):
        better = (i1 != k) & ((i2 < 0) | (vb[k] > m2))
        i2 = jnp.where(better, k, i2)
        m2 = jnp.where(better, vb[k], m2)
    s1 = vs[0]
    s2 = vs[0]
    for k in range(1, npg):
        s1 = jnp.where(i1 == k, vs[k], s1)
        s2 = jnp.where(i2 == k, vs[k], s2)
    w1 = s1 / (s1 + s2)
    w2 = s2 / (s1 + s2)
    lo = jnp.minimum(i1, i2)
    hi = jnp.maximum(i1, i2)
    pair = jnp.zeros_like(gsel)
    for p, (a, c) in enumerate(PAIRS):
        pair = jnp.where((lo == a) & (hi == c), p, pair)
    first_is_lo = i1 < i2
    return (gsel * len(PAIRS) + pair, jnp.where(first_is_lo, w1, w2), jnp.where(first_is_lo, w2, w1))


def _route_body(x_ref, g_ref, sc_ref, sh_ref, rw_ref, rb_ref, cnt_in_ref, *rest, aliased):
    if aliased:
        rest = rest[3:]
    rows_ref, bucket_ref, rank_ref, cnt_ref = rest
    tm = x_ref.shape[0]

    @pl.when(pl.program_id(0) == 0)
    def _():
        cnt_ref[...] = cnt_in_ref[...]

    hn = _prenorm(x_ref[...], g_ref[...], sc_ref[...], sh_ref[...])
    logits_t = _dot3(hn, rw_ref[...]).T
    scores = _sigmoid(logits_t[0:N_EXPERTS, :])
    biased = scores + rb_ref[0:N_EXPERTS, :]
    bucket, w_lo, w_hi = _route_rows([scores[e:e + 1, :] for e in range(N_EXPERTS)],
                                     [biased[e:e + 1, :] for e in range(N_EXPERTS)])
    bucket_ref[...] = bucket
    onehot = (lax.broadcasted_iota(I32, (BUCKET_ROWS, tm), 0) == bucket).astype(F32)
    before = (lax.broadcasted_iota(I32, (tm, tm), 0) < lax.broadcasted_iota(I32, (tm, tm), 1)).astype(BF16)
    earlier = _dot(onehot.astype(BF16), before) + cnt_ref[:, 0:1]
    rank_ref[...] = jnp.sum(onehot * earlier, axis=0, keepdims=True).astype(I32)
    cnt_ref[...] = cnt_ref[...] + jnp.sum(onehot, axis=1, keepdims=True)

    sub = lax.broadcasted_iota(I32, (LANE, tm), 0)
    gate_t = jnp.where(sub == 0, w_lo, jnp.where(sub == 1, w_hi, 0.0))
    rows_ref[pl.ds(GATE_ROW, tm, stride=ROW_PITCH), :] = gate_t.T
    for j in range(ROW_PLANES):
        rows_ref[pl.ds(j, tm, stride=ROW_PITCH), :] = hn[:, j * LANE:(j + 1) * LANE]


def _route_call(x, g, mod, router_w, router_b, counts, tm, total_rows, row_offset, prior, name):
    t, d = x.shape
    blk0 = row_offset // tm
    rw = jnp.pad(router_w, ((0, 0), (0, LANE - N_EXPERTS)))
    rb = jnp.pad(router_b.reshape(-1, 1), ((0, LANE - N_EXPERTS), (0, 0)))
    in_specs = [pl.BlockSpec((tm, d), lambda i: (i, 0)),
                pl.BlockSpec((1, d), lambda i: (0, 0)),
                mod.spec(4, tm, 1), mod.spec(3, tm, 1),
                pl.BlockSpec((d, LANE), lambda i: (0, 0)),
                pl.BlockSpec((LANE, 1), lambda i: (0, 0)),
                pl.BlockSpec((BUCKET_ROWS, LANE), lambda i: (0, 0))]
    args = [x, g.reshape(1, d), mod.arr, mod.arr, rw, rb, counts]
    aliases = {}
    if prior is not None:
        in_specs += [pl.BlockSpec(memory_space=pl.ANY)] * len(prior)
        aliases = {len(args) + k: k for k in range(len(prior))}
        args += list(prior)
    return pl.pallas_call(
        functools.partial(_route_body, aliased=prior is not None),
        grid=(t // tm,),
        in_specs=in_specs,
        out_specs=[pl.BlockSpec((tm * ROW_PITCH, LANE), lambda i: (blk0 + i, 0)),
                   pl.BlockSpec((1, tm), lambda i: (0, blk0 + i)),
                   pl.BlockSpec((1, tm), lambda i: (0, blk0 + i)),
                   pl.BlockSpec((BUCKET_ROWS, LANE), lambda i: (0, 0))],
        out_shape=[jax.ShapeDtypeStruct((total_rows * ROW_PITCH, LANE), F32),
                   jax.ShapeDtypeStruct((1, total_rows), I32),
                   jax.ShapeDtypeStruct((1, total_rows), I32),
                   jax.ShapeDtypeStruct((BUCKET_ROWS, LANE), F32)],
        input_output_aliases=aliases,
        compiler_params=_cparams(("arbitrary",)),
        name=name,
    )(*args)


def _invert_body(dest_ref, init_ref, gather_ref):
    pltpu.sync_copy(init_ref, gather_ref)

    def put(t, carry):
        gather_ref[dest_ref[t]] = t
        return carry

    lax.fori_loop(0, dest_ref.shape[0], put, 0, unroll=8)


def _invert_call(dest, n_slots, name):
    smem = pl.BlockSpec(memory_space=pltpu.SMEM)
    return pl.pallas_call(
        _invert_body,
        in_specs=[smem, pl.BlockSpec(memory_space=pl.ANY)],
        out_specs=smem,
        out_shape=jax.ShapeDtypeStruct((n_slots,), I32),
        name=name,
    )(dest, jnp.zeros((n_slots,), I32))


def _unrolled(lo, hi, body, carry):
    for b in range(lo, hi):
        carry = body(b, carry)
    return carry


def _experts_body(grp_ref, lo_ref, hi_ref, fresh_ref, valid_ref, out_ref, gather_ref, rows_ref, wg_in_ref,
                  wu_in_ref, wd_in_ref, y_ref, wg_ref, wu_ref, wd_ref, xb_ref, gsem):
    i = pl.program_id(0)
    n = pl.num_programs(0)
    tile = EXPERT_TILE
    copied = GATE_ROW + 1
    group = 8
    del grp_ref, out_ref

    def start_gather(step, slot, loop):
        def body(b, carry):
            for k in range(group):
                r = b * group + k
                tok = gather_ref[step * tile + r]
                pltpu.make_async_copy(rows_ref.at[pl.ds(tok * ROW_PITCH, copied)],
                                      xb_ref.at[slot, pl.ds(r * ROW_PITCH, copied)],
                                      gsem.at[slot]).start(priority=k % 2)
            return carry

        loop(0, tile // group, body, 0)

    def wait_gather(slot):
        rows = tile * copied
        pltpu.make_async_copy(rows_ref.at[pl.ds(0, rows)], xb_ref.at[slot, pl.ds(0, rows)], gsem.at[slot]).wait()

    @pl.when(fresh_ref[i] == 1)
    def _():
        for e in range(EXPERTS_PER_GROUP):
            wg_ref[e] = wg_in_ref[e].astype(BF16)
            wu_ref[e] = wu_in_ref[e].astype(BF16)
            wd_ref[e] = wd_in_ref[e].astype(BF16)

    nxt = jnp.minimum(i + 1, n - 1)
    more = jnp.logical_and(i + 1 < n, valid_ref[nxt] == 1)

    @pl.when(valid_ref[i] == 1)
    def _():
        slot = i % GATHER_SLOTS

        @pl.when(i == 0)
        def _():
            for ahead in range(GATHER_SLOTS - 1):
                start_gather(ahead, ahead, lax.fori_loop)

        wait_gather(slot)
        ahead = i + GATHER_SLOTS - 1
        start_gather(jnp.minimum(ahead, n - 1), ahead % GATHER_SLOTS, _unrolled)
        x = jnp.concatenate([xb_ref[slot, pl.ds(j, tile, stride=ROW_PITCH), :].astype(BF16)
                             for j in range(ROW_PLANES)], axis=-1)

        def expert(e):
            act = (_silu(_dot(x, wg_ref[e])) * _dot(x, wu_ref[e])).astype(BF16)
            return _dot(act, wd_ref[e])

        w = xb_ref[slot, pl.ds(GATE_ROW, tile, stride=ROW_PITCH), :]
        y = w[:, 0:1] * expert(lo_ref[i]) + w[:, 1:2] * expert(hi_ref[i])
        for j in range(ROW_PLANES):
            y_ref[pl.ds(j, tile, stride=ROW_PITCH), :] = y[:, j * LANE:(j + 1) * LANE]

        @pl.when(jnp.logical_not(more))
        def _():
            for k in range(1, GATHER_SLOTS):
                wait_gather((i + k) % GATHER_SLOTS)


def _experts_call(grp, lo, hi, fresh, valid, out_tile, gather_idx, rows, w_gate, w_up, w_down, layer, name):
    n_tiles = grp.shape[0]
    tile = EXPERT_TILE
    d, f = D_MODEL, D_EXPERT
    npg = EXPERTS_PER_GROUP
    group_block = lambda i, grp, *_: (layer, grp[i], 0, 0)
    once = pl.Buffered(1)
    return pl.pallas_call(
        _experts_body,
        grid_spec=pltpu.PrefetchScalarGridSpec(
            num_scalar_prefetch=7,
            grid=(n_tiles,),
            in_specs=[pl.BlockSpec(memory_space=pl.ANY),
                      pl.BlockSpec((None, npg, d, f), group_block, pipeline_mode=once),
                      pl.BlockSpec((None, npg, d, f), group_block, pipeline_mode=once),
                      pl.BlockSpec((None, npg, f, d), group_block, pipeline_mode=once)],
            out_specs=pl.BlockSpec((tile * ROW_PITCH, LANE), lambda i, g, l, h, fr, va, out, *_: (out[i], 0)),
            scratch_shapes=[pltpu.VMEM((npg, d, f), BF16), pltpu.VMEM((npg, d, f), BF16),
                            pltpu.VMEM((npg, f, d), BF16),
                            pltpu.VMEM((GATHER_SLOTS, tile * ROW_PITCH, LANE), F32),
                            pltpu.SemaphoreType.DMA((GATHER_SLOTS,))]),
        out_shape=jax.ShapeDtypeStruct((n_tiles * tile * ROW_PITCH, LANE), F32),
        compiler_params=_cparams(("arbitrary",)),
        name=name,
    )(grp, lo, hi, fresh, valid, out_tile, gather_idx, rows, w_gate, w_up, w_down)


def _moe_res_body(dest_ref, y_ref, x_ref, gate_ref, fg_ref, o_ref, *rest, row_offset, final):
    if final:
        on_ref, buf_ref, sem = rest
    else:
        buf_ref, sem = rest
    i = pl.program_id(0)
    tm = x_ref.shape[0]
    group = 8

    def start_gather(step, slot, loop):
        base = row_offset + step * tm

        def body(b, carry):
            for k in range(group):
                r = b * group + k
                pltpu.make_async_copy(y_ref.at[pl.ds(dest_ref[base + r] * ROW_PITCH, ROW_PLANES)],
                                      buf_ref.at[slot, pl.ds(r * ROW_PITCH, ROW_PLANES)],
                                      sem.at[slot]).start(priority=k % 2)
            return carry

        loop(0, tm // group, body, 0)

    def wait_gather(slot):
        rows = tm * ROW_PLANES
        pltpu.make_async_copy(y_ref.at[pl.ds(0, rows)], buf_ref.at[slot, pl.ds(0, rows)], sem.at[slot]).wait()

    @pl.when(i == 0)
    def _():
        start_gather(0, 0, lax.fori_loop)

    slot = i % 2
    wait_gather(slot)

    @pl.when(i + 1 < pl.num_programs(0))
    def _():
        start_gather(i + 1, 1 - slot, lax.fori_loop)
    y = jnp.concatenate([buf_ref[slot, pl.ds(j, tm, stride=ROW_PITCH), :] for j in range(ROW_PLANES)], axis=-1)
    xn = x_ref[...] + gate_ref[...] * y
    o_ref[...] = xn
    if final:
        ms = jnp.mean(xn * xn, axis=-1, keepdims=True)
        on_ref[...] = xn * lax.rsqrt(ms + EPS) * fg_ref[...]


def _moe_res_call(dest, y_sorted, x, mod, final_g, tm, row_offset, final, name):
    t, d = x.shape
    n_out = 2 if final else 1
    gate_spec = mod.spec(5, tm, 1)
    gate_map = gate_spec.index_map
    return pl.pallas_call(
        functools.partial(_moe_res_body, row_offset=row_offset, final=final),
        grid_spec=pltpu.PrefetchScalarGridSpec(
            num_scalar_prefetch=1,
            grid=(t // tm,),
            in_specs=[pl.BlockSpec(memory_space=pl.ANY),
                      pl.BlockSpec((tm, d), lambda i, dest: (i, 0)),
                      pl.BlockSpec(gate_spec.block_shape, lambda i, dest: gate_map(i)),
                      pl.BlockSpec((1, d), lambda i, dest: (0, 0))],
            out_specs=[pl.BlockSpec((tm, d), lambda i, dest: (i, 0))] * n_out,
            scratch_shapes=[pltpu.VMEM((2, tm * ROW_PITCH, LANE), F32), pltpu.SemaphoreType.DMA((2,))]),
        out_shape=[jax.ShapeDtypeStruct((t, d), F32)] * n_out,
        compiler_params=_cparams(("arbitrary",)),
        name=name,
    )(dest, y_sorted, x, mod.arr, final_g.reshape(1, d))


def _tile_tables(counts, n_tiles):
    tile = EXPERT_TILE
    tiles_per_bucket = (counts + tile - 1) // tile
    ends = jnp.cumsum(tiles_per_bucket)
    starts = ends - tiles_per_bucket
    used = ends[-1]
    ti = jnp.arange(n_tiles, dtype=I32)
    valid = (ti < used).astype(I32)
    tile_bucket = jnp.sum((jnp.minimum(ti, used - 1)[:, None] >= ends[None, :]).astype(I32), axis=1)
    pair_lo = jnp.asarray([p[0] for p in PAIRS], I32)
    pair_hi = jnp.asarray([p[1] for p in PAIRS], I32)
    grp = tile_bucket // len(PAIRS)
    lo = pair_lo[tile_bucket % len(PAIRS)]
    hi = pair_hi[tile_bucket % len(PAIRS)]
    fresh = jnp.concatenate([jnp.ones((1,), I32), (grp[1:] != grp[:-1]).astype(I32)])
    return grp, lo, hi, fresh, valid, jnp.minimum(ti, used - 1).astype(I32), (starts * tile).astype(I32)


def _moe_layer(xp, xs, norm_g2, mod_p, mod_s, router_w, router_b, w_gate, w_up, w_down, layer, final_g, final):
    tp, ts = xp.shape[0], xs.shape[0]
    total = tp + ts
    n_tiles = -(-total // EXPERT_TILE) + N_BUCKETS
    zero_counts = jnp.zeros((BUCKET_ROWS, LANE), F32)
    rows, bucket, rank, counts = _route_call(xp, norm_g2, mod_p, router_w, router_b, zero_counts, 512, total, 0,
                                             None, "route_prompt")
    rows, bucket, rank, counts = _route_call(xs, norm_g2, mod_s, router_w, router_b, counts, ts, total, tp,
                                             (rows, bucket, rank), "route_sample")
    grp, lo, hi, fresh, valid, out_tile, bucket_start = _tile_tables(counts[:N_BUCKETS, 0].astype(I32), n_tiles)
    dest = bucket_start[bucket[0]] + rank[0]
    gather_idx = _invert_call(dest, n_tiles * EXPERT_TILE, "invert")
    y_sorted = _experts_call(grp, lo, hi, fresh, valid, out_tile, gather_idx, rows, w_gate, w_up, w_down, layer,
                             "experts")
    outp = _moe_res_call(dest, y_sorted, xp, mod_p, final_g, 512, 0, final, "moe_res_prompt")
    outs = _moe_res_call(dest, y_sorted, xs, mod_s, final_g, ts, tp, final, "moe_res_sample")
    return outp, outs


def _mamba_layer(x, g, mod, conv_prev, ssm, w_zx, w_dt, conv_w, conv_b, dt_bias, a_log, d_skip, norm_g, w_out,
                 tm, tag):
    pad_h = lambda v: jnp.pad(v.reshape(1, -1), ((0, 0), (0, LANE - A_N_HEADS)))
    tm_in = min(2 * tm, x.shape[0])
    proj = _norm_mm_call(x, g, mod, 1, 0, w_zx, tm_in, 1024, BF16, "a_in_" + tag)
    dt_raw = _norm_mm3_call(x, g, mod, 1, 0, w_dt, tm, "a_dt_" + tag)
    d_x = jnp.repeat(d_skip, A_HEAD_DIM).reshape(1, A_D_INNER)
    weights = (conv_w, conv_b.reshape(1, -1), pad_h(dt_bias), pad_h(a_log), d_x, norm_g.reshape(1, -1))
    if ssm[0] == "step":
        yn, conv_new, ssm_new = _ssd_step_call(proj, dt_raw, conv_prev, ssm[1], ssm[2], ssm[3], *weights,
                                               "ssd_step_" + tag)
    else:
        yn, conv_new, ssm_new = _ssd_call(proj, dt_raw, conv_prev, ssm[2], *weights, ssm[1], "ssd_" + tag)
    x = _out_res_call(yn, w_out, x, mod, 2, tm, "a_out_" + tag)
    return x, conv_new, ssm_new


def kernel(x_prompt, x_sample, c_prompt, c_sample, state_a_conv, state_a_ssm, state_c_pool, w_mod, b_mod, norm_g, final_g, a_w_in, a_conv_w, a_conv_b, a_dt_bias, a_log, a_d, a_norm_g, a_w_out, b_w_in, b_b_in, b_ln_g, b_ln_b, b_w_s, b_b_s, b_w_out, c_w_g, c_scale, router_w, router_b, e_w_gate, e_w_up, e_w_down):
    bp, seq, d = x_prompt.shape
    bs = x_sample.shape[0]
    n_a, n_c = state_a_conv.shape[0], state_c_pool.shape[0]
    mod_all = _mod_call(jnp.concatenate([c_prompt, c_sample], axis=0), w_mod, b_mod)
    mod_p_arr = mod_all[:, :bp].reshape(DEPTH, bp, 6, 1, d)
    mod_s_arr = mod_all[:, bp:]
    xp = x_prompt.reshape(bp * seq, d)
    xs = x_sample.reshape(bs, d)
    conv_p, ssm_p, pool_p, conv_s, pool_s, v_s = [], [], [], [], [], []
    hp = A_N_HEADS * A_HEAD_DIM
    ssm_s_in = state_a_ssm.reshape(n_a, bs, hp, A_D_STATE)
    ssm_s_out = None
    yp = ys = None
    for i in range(DEPTH):
        kind, s = LAYER_KIND[i], LAYER_SLOT[i]
        mod_p = Mod(mod_p_arr, i, False, seq)
        mod_s = Mod(mod_s_arr, i, True)
        g1 = norm_g[i, 0]
        if kind == 0:
            w_zx = a_w_in[s, :, :A_ZX].astype(BF16)
            w_dt = jnp.pad(a_w_in[s, :, A_ZX:], ((0, 0), (0, LANE - A_N_HEADS)))
            weights = (w_zx, w_dt, a_conv_w[s], a_conv_b[s], a_dt_bias[s], a_log[s], a_d[s], a_norm_g[s], a_w_out[s])
            conv0 = jnp.zeros((bp, A_CONV - 1, A_CONV_CH), F32)
            ssm0 = jnp.zeros((bp, hp, A_D_STATE), F32)
            xp, cv, ss = _mamba_layer(xp, g1, mod_p, conv0, ("prompt", bp, ssm0), *weights, 1024, "p%d" % i)
            conv_p.append(cv)
            ssm_p.append(ss.reshape(bp, A_N_HEADS, A_HEAD_DIM, A_D_STATE))
            xs, cv, ssm_s_out = _mamba_layer(xs, g1, mod_s, state_a_conv[s], ("step", ssm_s_in, s, ssm_s_out),
                                             *weights, bs, "s%d" % i)
            conv_s.append(cv)
        elif kind == 1:
            w_uv = b_w_in[s].astype(BF16)
            uv = _gmlp_in_call(xp, g1, mod_p, w_uv, b_b_in[s], b_ln_g[s], b_ln_b[s], 1024, BF16, "b_in_p%d" % i)
            xp = _gmlp_out_call(uv, b_w_s[s], b_b_s[s], b_w_out[s], xp, mod_p, 512, "b_out_p%d" % i)
            uv = _gmlp_in_call(xs, g1, mod_s, w_uv, b_b_in[s], b_ln_g[s], b_ln_b[s], bs, F32, "b_in_s%d" % i)
            xs = _gmlp_out_step_call(uv, b_w_s[s], b_b_s[s], b_w_out[s], xs, mod_s, "b_out_s%d" % i)
            v_s.append(uv[:, B_D:].reshape(bs, 1, B_D))
        else:
            pool0 = jnp.zeros((bp, C_STATE, d), F32)
            xp, pr = _pool_call(xp, g1, mod_p, pool0, c_w_g[s], c_scale[s], bp, 512, 0, "pool_p%d" % i)
            pool_p.append(pr)
            xs, pr = _pool_step_call(xs, g1, mod_s, state_c_pool[s], c_w_g[s], c_scale[s], PAST_LEN, "pool_s%d" % i)
            pool_s.append(pr)
        final = i == DEPTH - 1
        outp, outs = _moe_layer(xp, xs, norm_g[i, 1], mod_p, mod_s, router_w, router_b, e_w_gate, e_w_up, e_w_down, i,
                                final_g, final)
        xp, xs = outp[0], outs[0]
        if final:
            yp, ys = outp[1], outs[1]
    return (yp.reshape(bp, seq, d), ys.reshape(bs, 1, d), jnp.stack(conv_p), jnp.stack(ssm_p), jnp.stack(pool_p),
            jnp.stack(conv_s), ssm_s_out.reshape(state_a_ssm.shape), jnp.stack(pool_s), jnp.stack(v_s))
```

```python
import functools
import math

import numpy as np
import jax
import jax.numpy as jnp
from jax import lax
from jax.experimental import pallas as pl
from jax.experimental.pallas import tpu as pltpu

F32 = jnp.float32
BF16 = jnp.bfloat16
I32 = jnp.int32
EPS = 1e-6

LANE = 128
D_MODEL = 1024
DEPTH = 4
PAST_LEN = 16384
LAYER_KIND = (0, 1, 2, 0)
LAYER_SLOT = (0, 0, 0, 1)
A_D_INNER = 2 * D_MODEL
A_HEAD_DIM = 64
A_N_HEADS = A_D_INNER // A_HEAD_DIM
A_N_GROUPS = 8
A_HPG = A_N_HEADS // A_N_GROUPS
A_D_STATE = 128
A_GN = A_N_GROUPS * A_D_STATE
A_CONV = 4
A_CONV_CH = A_D_INNER + 2 * A_GN
A_ZX = A_D_INNER + A_CONV_CH
A_CHUNK = 128
A_GW = A_HPG * A_HEAD_DIM
B_D = 2 * D_MODEL
B_N_GROUPS = 8
B_GROUP_DIM = B_D // B_N_GROUPS
B_CHUNK = 128
C_WINDOWS = (2, 4, 8, 16)
C_GROUP_DIM = D_MODEL // len(C_WINDOWS)
C_STATE = max(C_WINDOWS) - 1
N_EXPERTS = 16
N_EXPERT_GROUPS = 4
EXPERTS_PER_GROUP = 4
D_EXPERT = D_MODEL // 2
PAIRS = ((0, 1), (0, 2), (0, 3), (1, 2), (1, 3), (2, 3))
N_BUCKETS = N_EXPERT_GROUPS * len(PAIRS)
BUCKET_ROWS = 32
ROW_PLANES = D_MODEL // LANE
ROW_PITCH = ROW_PLANES + 4
GATE_ROW = ROW_PLANES
EXPERT_TILE = 256
GATHER_SLOTS = 6
VMEM_LIMIT = 56 * 1024 * 1024
TM_MAMBA = 1024
A_IN_TN = 1024
TM_GMLP_IN = 1024
TM_GMLP_OUT = 512
TM_POOL = 512
TM_ROUTE = 512
TM_MOE_RES = 512


def _cparams(sem, vmem=VMEM_LIMIT):
    return pltpu.CompilerParams(dimension_semantics=sem, vmem_limit_bytes=vmem)


def _sigmoid(x):
    return 1.0 / (1.0 + jnp.exp(-x))


def _silu(x):
    return x * _sigmoid(x)


def _gelu_tanh(x):
    c = 2.0 * math.sqrt(2.0 / math.pi)
    return x / (1.0 + jnp.exp(x * (-c - (c * 0.044715) * (x * x))))


def _softplus(x):
    return jnp.maximum(x, 0.0) + jnp.log1p(jnp.exp(-jnp.abs(x)))


def _split2(a):
    hi = a.astype(BF16)
    lo = (a - hi.astype(F32)).astype(BF16)
    return hi, lo


def _split3(a):
    hi = a.astype(BF16)
    r = a - hi.astype(F32)
    mid = r.astype(BF16)
    lo = (r - mid.astype(F32)).astype(BF16)
    return hi, mid, lo


def _dot(a, b):
    return jnp.dot(a, b, preferred_element_type=F32)


def _dot_nt(a, b):
    return lax.dot_general(a, b, (((1,), (1,)), ((), ())), preferred_element_type=F32)


def _dot_tn(a, b):
    return lax.dot_general(a, b, (((0,), (0,)), ((), ())), preferred_element_type=F32)


def _dot3(a, b):
    a_hi, a_lo = _split2(a)
    b_hi, b_lo = _split2(b)
    return _dot(a_hi, b_hi) + (_dot(a_lo, b_hi) + _dot(a_hi, b_lo))


def _prenorm(x, g, sc, sh):
    ms = jnp.mean(x * x, axis=-1, keepdims=True)
    return (x * lax.rsqrt(ms + EPS) * g) * (1.0 + sc) + sh


def _mod_body(c_ref, w_ref, b_ref, o_ref):
    o_ref[...] = _dot3(_silu(c_ref[...]), w_ref[...]) + b_ref[...]


def _mod_call(c_all, w_mod, b_mod):
    nb, d = c_all.shape
    depth, _, n = w_mod.shape
    tn = 1536
    return pl.pallas_call(
        _mod_body,
        grid=(depth, n // tn),
        in_specs=[pl.BlockSpec((nb, d), lambda i, j: (0, 0)),
                  pl.BlockSpec((None, d, tn), lambda i, j: (i, 0, j)),
                  pl.BlockSpec((None, 1, tn), lambda i, j: (i, 0, j))],
        out_specs=pl.BlockSpec((None, nb, tn), lambda i, j: (i, 0, j)),
        out_shape=jax.ShapeDtypeStruct((depth, nb, n), F32),
        compiler_params=_cparams(("arbitrary", "arbitrary")),
        name="mod",
    )(c_all, w_mod, b_mod.reshape(depth, 1, n))


class Mod:
    def __init__(self, arr, layer, per_row, rows_per_seq=None):
        self.arr, self.layer, self.per_row, self.rows_per_seq = arr, layer, per_row, rows_per_seq

    def spec(self, which, tm, ngrid):
        layer = self.layer
        if self.per_row:
            if ngrid == 1:
                return pl.BlockSpec((None, tm, D_MODEL), lambda i: (layer, i, which))
            return pl.BlockSpec((None, tm, D_MODEL), lambda i, j: (layer, i, which))
        tiles = self.rows_per_seq // tm
        if ngrid == 1:
            return pl.BlockSpec((None, None, None, 1, D_MODEL), lambda i: (layer, i // tiles, which, 0, 0))
        return pl.BlockSpec((None, None, None, 1, D_MODEL), lambda i, j: (layer, i // tiles, which, 0, 0))


def _norm_mm_body(x_ref, g_ref, sc_ref, sh_ref, w_ref, o_ref, hn_ref):
    @pl.when(pl.program_id(1) == 0)
    def _():
        hn_ref[...] = _prenorm(x_ref[...], g_ref[...], sc_ref[...], sh_ref[...]).astype(BF16)

    o_ref[...] = _dot(hn_ref[...], w_ref[...]).astype(o_ref.dtype)


def _norm_mm_call(x, g, mod, which_sc, which_sh, w, tm, tn, out_dtype, name):
    t, d = x.shape
    n_cols = w.shape[1]
    return pl.pallas_call(
        _norm_mm_body,
        grid=(t // tm, n_cols // tn),
        in_specs=[pl.BlockSpec((tm, d), lambda i, j: (i, 0)),
                  pl.BlockSpec((1, d), lambda i, j: (0, 0)),
                  mod.spec(which_sc, tm, 2), mod.spec(which_sh, tm, 2),
                  pl.BlockSpec((d, tn), lambda i, j: (0, j))],
        out_specs=pl.BlockSpec((tm, tn), lambda i, j: (i, j)),
        out_shape=jax.ShapeDtypeStruct((t, n_cols), out_dtype),
        scratch_shapes=[pltpu.VMEM((tm, d), BF16)],
        compiler_params=_cparams(("arbitrary", "arbitrary")),
        name=name,
    )(x, g.reshape(1, d), mod.arr, mod.arr, w)


def _norm_mm3_body(x_ref, g_ref, sc_ref, sh_ref, w_ref, o_ref):
    hn = _prenorm(x_ref[...], g_ref[...], sc_ref[...], sh_ref[...])
    o_ref[...] = _dot3(hn, w_ref[...])


def _norm_mm3_call(x, g, mod, which_sc, which_sh, w, tm, name):
    t, d = x.shape
    n = w.shape[1]
    return pl.pallas_call(
        _norm_mm3_body,
        grid=(t // tm,),
        in_specs=[pl.BlockSpec((tm, d), lambda i: (i, 0)),
                  pl.BlockSpec((1, d), lambda i: (0, 0)),
                  mod.spec(which_sc, tm, 1), mod.spec(which_sh, tm, 1),
                  pl.BlockSpec((d, n), lambda i: (0, 0))],
        out_specs=pl.BlockSpec((tm, n), lambda i: (i, 0)),
        out_shape=jax.ShapeDtypeStruct((t, n), F32),
        compiler_params=_cparams(("arbitrary",)),
        name=name,
    )(x, g.reshape(1, d), mod.arr, mod.arr, w)


def _out_res_body(y_ref, w_ref, x_ref, gate_ref, o_ref, wbf_ref):
    @pl.when(pl.program_id(0) == 0)
    def _():
        wbf_ref[...] = w_ref[...].astype(BF16)

    o_ref[...] = x_ref[...] + gate_ref[...] * _dot(y_ref[...], wbf_ref[...])


def _out_res_call(y, w, x, mod, which_gate, tm, name):
    t, k = y.shape
    d = x.shape[1]
    return pl.pallas_call(
        _out_res_body,
        grid=(t // tm,),
        in_specs=[pl.BlockSpec((tm, k), lambda i: (i, 0)),
                  pl.BlockSpec((k, d), lambda i: (0, 0)),
                  pl.BlockSpec((tm, d), lambda i: (i, 0)),
                  mod.spec(which_gate, tm, 1)],
        out_specs=pl.BlockSpec((tm, d), lambda i: (i, 0)),
        out_shape=jax.ShapeDtypeStruct((t, d), F32),
        scratch_shapes=[pltpu.VMEM((k, d), BF16)],
        compiler_params=_cparams(("arbitrary",)),
        name=name,
    )(y, w, x, mod.arr)


def _head_expand():
    h = np.arange(LANE)[:, None]
    c = np.arange(A_D_INNER)[None, :]
    return jnp.asarray((c // A_HEAD_DIM == h).astype(np.float32), dtype=BF16)


def _gate_norm(y, z, ng):
    gated = y * _silu(z)
    ms = jnp.mean(gated * gated, axis=-1, keepdims=True)
    return gated * lax.rsqrt(ms + EPS) * ng


def _ssd_body(z_ref, xs_ref, bc_ref, dt_ref, cprev_ref, sprev_ref, cw_ref, cb_ref, dtb_ref, alog_ref,
              dx_ref, ng_ref, exp_ref, yn_ref, cnew_ref, snew_ref, tail_ref, act_ref):
    c = pl.program_id(1)
    q = A_CHUNK
    nt = A_CONV - 1

    @pl.when(c == 0)
    def _():
        tail_ref[0:nt, :] = cprev_ref[...]
        snew_ref[...] = sprev_ref[...]

    row = lax.broadcasted_iota(I32, (q, q), 0)
    col = lax.broadcasted_iota(I32, (q, q), 1)
    shifts = [jnp.where(row - col == nt - k, 1.0, 0.0).astype(BF16) for k in range(nt)]
    sub = lax.broadcasted_iota(I32, (8, 1), 0)
    cw = 512
    for j in range(A_CONV_CH // cw):
        sl = slice(j * cw, (j + 1) * cw)
        src = xs_ref if (j + 1) * cw <= A_D_INNER else bc_ref
        off = j * cw if src is xs_ref else j * cw - A_D_INNER
        x_bf = src[:, off:off + cw]
        conv = cb_ref[:, sl] + x_bf.astype(F32) * cw_ref[nt:nt + 1, sl]
        for k in range(nt):
            conv = conv + _dot(shifts[k], x_bf) * cw_ref[k:k + 1, sl]
        act_ref[:, sl] = _silu(conv)
        corr = jnp.zeros((8, cw), F32)
        for l in range(nt):
            c_l = sum(tail_ref[l + k:l + k + 1, sl] * cw_ref[k:k + 1, sl] for k in range(nt - l))
            corr = jnp.where(sub == l, c_l, corr)
        act_ref[0:8, sl] = _silu(conv[0:8, :] + corr)
        tail_ref[0:nt, sl] = x_bf[q - 8:q, :].astype(F32)[8 - nt:8, :]

    @pl.when(c == pl.num_programs(1) - 1)
    def _():
        cnew_ref[...] = tail_ref[0:nt, :]

    dt = _softplus(dt_ref[...] + dtb_ref[...])
    a = dt * (-jnp.exp(alog_ref[...]))
    row = lax.broadcasted_iota(I32, (q, q), 0)
    col = lax.broadcasted_iota(I32, (q, q), 1)
    causal = row >= col
    tril = jnp.where(causal, 1.0, 0.0).astype(BF16)
    a_hi, a_mid, a_lo = _split3(a)
    cs = _dot(tril, a_hi) + (_dot(tril, a_mid) + _dot(tril, a_lo))
    cs_t = cs.T
    dt_t = dt.T
    cs_last = cs[q - 1:q, :]
    ecs = jnp.exp(cs)
    wend = jnp.exp(cs_last - cs) * dt
    st_hi, st_lo = _split2(jnp.concatenate([ecs, wend], axis=0))
    st_x = _dot(st_hi, exp_ref[...]) + _dot(st_lo, exp_ref[...])
    lane_head = lax.broadcasted_iota(I32, (q, A_GW), 1) // A_HEAD_DIM

    for g in range(A_N_GROUPS):
        gsl = slice(g * A_GW, (g + 1) * A_GW)
        b_g = act_ref[:, A_D_INNER + g * A_D_STATE:A_D_INNER + (g + 1) * A_D_STATE].astype(BF16)
        c_g = act_ref[:, A_D_INNER + A_GN + g * A_D_STATE:A_D_INNER + A_GN + (g + 1) * A_D_STATE].astype(BF16)
        x_g = act_ref[:, gsl]
        x_bf = x_g.astype(BF16)
        h_g = snew_ref[gsl, :]
        cb = _dot_nt(c_g, b_g)
        y = jnp.zeros((q, A_GW), F32)
        for r in range(A_HPG):
            h = g * A_HPG + r
            seg = cs[:, h:h + 1] - cs_t[h:h + 1, :]
            decay = jnp.exp(jnp.where(causal, seg, -1e30))
            wm = (cb * decay * dt_t[h:h + 1, :]).astype(BF16)
            y = y + _dot(wm, jnp.where(lane_head == r, x_bf, jnp.zeros_like(x_bf)))
        y = y + st_x[0:q, gsl] * _dot_nt(c_g, h_g.astype(BF16)) + dx_ref[:, gsl] * x_g
        yn_ref[:, gsl] = _gate_norm(y, z_ref[:, gsl].astype(F32), ng_ref[:, gsl]).astype(BF16)
        s_new = _dot_tn((x_g * st_x[q:2 * q, gsl]).astype(BF16), b_g)
        for r in range(A_HPG):
            h = g * A_HPG + r
            rsl = slice(g * A_GW + r * A_HEAD_DIM, g * A_GW + (r + 1) * A_HEAD_DIM)
            keep = jnp.exp(cs[q - 1:q, h:h + 1])
            snew_ref[rsl, :] = snew_ref[rsl, :] * keep + s_new[r * A_HEAD_DIM:(r + 1) * A_HEAD_DIM, :]


def _ssd_call(proj, dt_raw, conv_prev, ssm_prev, conv_w, conv_b, dt_bias, a_log, d_x, norm_g, nb, name):
    t = proj.shape[0]
    q = A_CHUNK
    nc = t // nb // q
    hp = A_N_HEADS * A_HEAD_DIM
    row = lambda b, c: (b * nc + c, 0)
    full = lambda shape: pl.BlockSpec(shape, lambda b, c: (0,) * len(shape))
    return pl.pallas_call(
        _ssd_body,
        grid=(nb, nc),
        in_specs=[pl.BlockSpec((q, A_D_INNER), lambda b, c: (b * nc + c, 0)),
                  pl.BlockSpec((q, A_D_INNER), lambda b, c: (b * nc + c, 1)),
                  pl.BlockSpec((q, 2 * A_GN), lambda b, c: (b * nc + c, 2)),
                  pl.BlockSpec((q, LANE), row),
                  pl.BlockSpec((None, A_CONV - 1, A_CONV_CH), lambda b, c: (b, 0, 0)),
                  pl.BlockSpec((None, hp, A_D_STATE), lambda b, c: (b, 0, 0)),
                  full((A_CONV, A_CONV_CH)), full((1, A_CONV_CH)), full((1, LANE)), full((1, LANE)),
                  full((1, A_D_INNER)), full((1, A_D_INNER)), full((LANE, A_D_INNER))],
        out_specs=[pl.BlockSpec((q, A_D_INNER), row),
                   pl.BlockSpec((None, A_CONV - 1, A_CONV_CH), lambda b, c: (b, 0, 0)),
                   pl.BlockSpec((None, hp, A_D_STATE), lambda b, c: (b, 0, 0))],
        out_shape=[jax.ShapeDtypeStruct((t, A_D_INNER), BF16),
                   jax.ShapeDtypeStruct((nb, A_CONV - 1, A_CONV_CH), F32),
                   jax.ShapeDtypeStruct((nb, hp, A_D_STATE), F32)],
        scratch_shapes=[pltpu.VMEM((8, A_CONV_CH), F32), pltpu.VMEM((q, A_CONV_CH), F32)],
        compiler_params=_cparams(("arbitrary", "arbitrary")),
        name=name,
    )(proj, proj, proj, dt_raw, conv_prev, ssm_prev, conv_w, conv_b, dt_bias, a_log, d_x, norm_g, _head_expand())


def _ssd_step_body(z_ref, xs_ref, bc_ref, dt_ref, cprev_ref, sprev_ref, cw_ref, cb_ref, dtb_ref, alog_ref,
                   dx_ref, ng_ref, exp_ref, yn_ref, cnew_ref, snew_ref, da_ref, y_ref):
    bt = z_ref.shape[0]
    cur = jnp.concatenate([xs_ref[...], bc_ref[...]], axis=1).astype(F32)
    conv = cb_ref[...] + cur * cw_ref[A_CONV - 1:A_CONV, :]
    for k in range(A_CONV - 1):
        conv = conv + cprev_ref[:, k, :] * cw_ref[k:k + 1, :]
    for k in range(A_CONV - 2):
        cnew_ref[:, k, :] = cprev_ref[:, k + 1, :]
    cnew_ref[:, A_CONV - 2, :] = cur
    act = _silu(conv)
    xs = act[:, 0:A_D_INNER]
    bm_bf = act[:, A_D_INNER:A_D_INNER + A_GN].astype(BF16)
    cm_bf = act[:, A_D_INNER + A_GN:A_CONV_CH].astype(BF16)
    dt = _softplus(dt_ref[...] + dtb_ref[...])
    da_ref[...] = jnp.exp(dt * (-jnp.exp(alog_ref[...])))
    dt_hi, dt_lo = _split2(dt)
    dt_x = _dot(dt_hi, exp_ref[...]) + _dot(dt_lo, exp_ref[...])
    xdt = xs * dt_x
    rows = lax.broadcasted_iota(I32, (bt, 1), 0)
    y_ref[...] = jnp.zeros_like(y_ref)

    def per_seq(j, carry):
        mine = rows == j
        xdt_j = jnp.where(mine, xdt, 0.0)
        da_j = da_ref[pl.ds(j, 1), :]
        for g in range(A_N_GROUPS):
            gsl = slice(g * A_GW, (g + 1) * A_GW)
            nsl = slice(g * A_D_STATE, (g + 1) * A_D_STATE)
            outer = _dot_tn(xdt_j[:, gsl].astype(BF16), bm_bf[:, nsl])
            for r in range(A_HPG):
                h = g * A_HPG + r
                rsl = slice(r * A_HEAD_DIM, (r + 1) * A_HEAD_DIM)
                hsl = slice(g * A_GW + r * A_HEAD_DIM, g * A_GW + (r + 1) * A_HEAD_DIM)
                snew_ref[j, hsl, :] = sprev_ref[j, hsl, :] * da_j[:, h:h + 1] + outer[rsl, :]
            yg = _dot_nt(cm_bf[:, nsl], snew_ref[j, gsl, :].astype(BF16))
            y_ref[:, gsl] = y_ref[:, gsl] + jnp.where(mine, yg, 0.0)
        return carry

    lax.fori_loop(0, bt, per_seq, 0)
    y = y_ref[...] + dx_ref[...] * xs
    z = z_ref[...].astype(F32)
    for g in range(A_N_GROUPS):
        gsl = slice(g * A_GW, (g + 1) * A_GW)
        yn_ref[:, gsl] = _gate_norm(y[:, gsl], z[:, gsl], ng_ref[:, gsl]).astype(BF16)


_SSD_STEP_INPUTS = 13


def _ssd_step_aliased_body(*refs):
    _ssd_step_body(*refs[:_SSD_STEP_INPUTS], *refs[_SSD_STEP_INPUTS + 1:])


def _ssd_step_call(proj, dt_raw, conv_prev, ssm_all, slot, ssm_out, conv_w, conv_b, dt_bias, a_log, d_x, norm_g,
                   name):
    nb = proj.shape[0]
    bt = 8
    hp = A_N_HEADS * A_HEAD_DIM
    full = lambda shape: pl.BlockSpec(shape, lambda i: (0,) * len(shape))
    state_spec = pl.BlockSpec((None, bt, hp, A_D_STATE), lambda i: (slot, i, 0, 0))
    in_specs = [pl.BlockSpec((bt, A_D_INNER), lambda i: (i, 0)),
                pl.BlockSpec((bt, A_D_INNER), lambda i: (i, 1)),
                pl.BlockSpec((bt, 2 * A_GN), lambda i: (i, 2)),
                pl.BlockSpec((bt, LANE), lambda i: (i, 0)),
                pl.BlockSpec((bt, A_CONV - 1, A_CONV_CH), lambda i: (i, 0, 0)),
                state_spec,
                full((A_CONV, A_CONV_CH)), full((1, A_CONV_CH)), full((1, LANE)), full((1, LANE)),
                full((1, A_D_INNER)), full((1, A_D_INNER)), full((LANE, A_D_INNER))]
    args = [proj, proj, proj, dt_raw, conv_prev, ssm_all, conv_w, conv_b, dt_bias, a_log, d_x, norm_g, _head_expand()]
    assert len(args) == _SSD_STEP_INPUTS
    aliases = {}
    if ssm_out is not None:
        in_specs.append(pl.BlockSpec(memory_space=pl.ANY))
        args.append(ssm_out)
        aliases = {_SSD_STEP_INPUTS: 2}
    return pl.pallas_call(
        _ssd_step_body if ssm_out is None else _ssd_step_aliased_body,
        grid=(nb // bt,),
        in_specs=in_specs,
        out_specs=[pl.BlockSpec((bt, A_D_INNER), lambda i: (i, 0)),
                   pl.BlockSpec((bt, A_CONV - 1, A_CONV_CH), lambda i: (i, 0, 0)),
                   state_spec],
        out_shape=[jax.ShapeDtypeStruct((nb, A_D_INNER), BF16),
                   jax.ShapeDtypeStruct((nb, A_CONV - 1, A_CONV_CH), F32),
                   jax.ShapeDtypeStruct(ssm_all.shape, F32)],
        scratch_shapes=[pltpu.VMEM((bt, LANE), F32), pltpu.VMEM((bt, A_D_INNER), F32)],
        input_output_aliases=aliases,
        compiler_params=_cparams(("arbitrary",)),
        name=name,
    )(*args)


def _gmlp_in_body(x_ref, g_ref, sc_ref, sh_ref, w_ref, b_ref, lg_ref, lb_ref, o_ref, hn_ref):
    j = pl.program_id(1)

    @pl.when(j == 0)
    def _():
        hn_ref[...] = _prenorm(x_ref[...], g_ref[...], sc_ref[...], sh_ref[...]).astype(BF16)

    uv = _gelu_tanh(_dot(hn_ref[...], w_ref[...]) + b_ref[...])

    @pl.when(j == 0)
    def _():
        o_ref[...] = uv.astype(o_ref.dtype)

    @pl.when(j == 1)
    def _():
        vc = uv - jnp.mean(uv, axis=-1, keepdims=True)
        var = jnp.mean(vc * vc, axis=-1, keepdims=True)
        o_ref[...] = (vc * lax.rsqrt(var + EPS) * lg_ref[...] + lb_ref[...]).astype(o_ref.dtype)


def _gmlp_in_call(x, g, mod, w, b, ln_g, ln_b, tm, out_dtype, name):
    t, d = x.shape
    return pl.pallas_call(
        _gmlp_in_body,
        grid=(t // tm, 2),
        in_specs=[pl.BlockSpec((tm, d), lambda i, j: (i, 0)),
                  pl.BlockSpec((1, d), lambda i, j: (0, 0)),
                  mod.spec(1, tm, 2), mod.spec(0, tm, 2),
                  pl.BlockSpec((d, B_D), lambda i, j: (0, j)),
                  pl.BlockSpec((1, B_D), lambda i, j: (0, j)),
                  pl.BlockSpec((1, B_D), lambda i, j: (0, 0)),
                  pl.BlockSpec((1, B_D), lambda i, j: (0, 0))],
        out_specs=pl.BlockSpec((tm, B_D), lambda i, j: (i, j)),
        out_shape=jax.ShapeDtypeStruct((t, 2 * B_D), out_dtype),
        scratch_shapes=[pltpu.VMEM((tm, d), BF16)],
        compiler_params=_cparams(("arbitrary", "arbitrary")),
        name=name,
    )(x, g.reshape(1, d), mod.arr, mod.arr, w, b.reshape(1, -1), ln_g.reshape(1, -1), ln_b.reshape(1, -1))


def _gmlp_out_body(u_ref, v_ref, ws_ref, bs_ref, w_ref, x_ref, gate_ref, o_ref, wbf_ref, wsbf_ref, m_ref):
    q = B_CHUNK

    @pl.when(pl.program_id(0) == 0)
    def _():
        wbf_ref[...] = w_ref[...].astype(BF16)
        causal = lax.broadcasted_iota(I32, (q, q), 0) >= lax.broadcasted_iota(I32, (q, q), 1)
        for g in range(B_N_GROUPS):
            wsbf_ref[g] = jnp.where(causal, ws_ref[g], 0.0).astype(BF16)

    for ci in range(u_ref.shape[0] // q):
        rsl = slice(ci * q, (ci + 1) * q)
        for g in range(B_N_GROUPS):
            gsl = slice(g * B_GROUP_DIM, (g + 1) * B_GROUP_DIM)
            mixed = _dot(wsbf_ref[g], v_ref[rsl, gsl].astype(BF16)) + bs_ref[:, g:g + 1]
            m_ref[rsl, gsl] = (u_ref[rsl, gsl].astype(F32) * mixed).astype(BF16)
    o_ref[...] = x_ref[...] + gate_ref[...] * _dot(m_ref[...], wbf_ref[...])


def _gmlp_out_call(uv, w_s, b_s, w_out, x, mod, tm, name):
    t, d = x.shape
    q = B_CHUNK
    return pl.pallas_call(
        _gmlp_out_body,
        grid=(t // tm,),
        in_specs=[pl.BlockSpec((tm, B_D), lambda i: (i, 0)),
                  pl.BlockSpec((tm, B_D), lambda i: (i, 1)),
                  pl.BlockSpec((B_N_GROUPS, q, q), lambda i: (0, 0, 0)),
                  pl.BlockSpec((q, B_N_GROUPS), lambda i: (0, 0)),
                  pl.BlockSpec((B_D, d), lambda i: (0, 0)),
                  pl.BlockSpec((tm, d), lambda i: (i, 0)),
                  mod.spec(2, tm, 1)],
        out_specs=pl.BlockSpec((tm, d), lambda i: (i, 0)),
        out_shape=jax.ShapeDtypeStruct((t, d), F32),
        scratch_shapes=[pltpu.VMEM((B_D, d), BF16), pltpu.VMEM((B_N_GROUPS, q, q), BF16),
                        pltpu.VMEM((tm, B_D), BF16)],
        compiler_params=_cparams(("arbitrary",)),
        name=name,
    )(uv, uv, w_s, b_s.T, w_out, x, mod.arr)


def _gmlp_out_step_body(u_ref, v_ref, wd_ref, bd_ref, w_ref, x_ref, gate_ref, o_ref):
    mixed = v_ref[...] * wd_ref[...] + bd_ref[...]
    m = (u_ref[...] * mixed).astype(BF16)
    o_ref[...] = x_ref[...] + gate_ref[...] * _dot(m, w_ref[...].astype(BF16))


def _gmlp_out_step_call(uv, w_s, b_s, w_out, x, mod, name):
    t, d = x.shape
    wd = jnp.repeat(w_s[:, 0, 0], B_GROUP_DIM).reshape(1, B_D)
    bd = jnp.repeat(b_s[:, 0], B_GROUP_DIM).reshape(1, B_D)
    return pl.pallas_call(
        _gmlp_out_step_body,
        grid=(1,),
        in_specs=[pl.BlockSpec((t, B_D), lambda i: (0, 0)),
                  pl.BlockSpec((t, B_D), lambda i: (0, 1)),
                  pl.BlockSpec((1, B_D), lambda i: (0, 0)),
                  pl.BlockSpec((1, B_D), lambda i: (0, 0)),
                  pl.BlockSpec((B_D, d), lambda i: (0, 0)),
                  pl.BlockSpec((t, d), lambda i: (0, 0)),
                  mod.spec(2, t, 1)],
        out_specs=pl.BlockSpec((t, d), lambda i: (0, 0)),
        out_shape=jax.ShapeDtypeStruct((t, d), F32),
        compiler_params=_cparams(("arbitrary",)),
        name=name,
    )(uv, uv, wd, bd, w_out, x, mod.arr)


def _pool_matmul(pooled, wg_ref):
    outs = []
    for gi in range(len(C_WINDOWS)):
        gsl = slice(gi * C_GROUP_DIM, (gi + 1) * C_GROUP_DIM)
        outs.append(_dot(pooled[:, gsl].astype(BF16), wg_ref[gi].astype(BF16)))
    return jnp.concatenate(outs, axis=-1)


def _pool_body(x_ref, g_ref, sc_ref, sh_ref, gate_ref, prev_ref, wg_ref, scale_ref, o_ref, pnew_ref, hp_ref,
               *, tiles_per_seq, start):
    i = pl.program_id(0)
    tm = x_ref.shape[0]
    top = 16
    ti = i % tiles_per_seq

    @pl.when(ti == 0)
    def _():
        hp_ref[top - C_STATE:top, :] = prev_ref[...]

    hn = _prenorm(x_ref[...], g_ref[...], sc_ref[...], sh_ref[...])
    hp_ref[top:top + tm, :] = hn
    pos = start + ti * tm + lax.broadcasted_iota(I32, (tm, 1), 0)
    outs = []
    for gi, w in enumerate(C_WINDOWS):
        gsl = slice(gi * C_GROUP_DIM, (gi + 1) * C_GROUP_DIM)
        acc = hn[:, gsl]
        for k in range(1, w):
            acc = acc + hp_ref[top - k:top - k + tm, gsl]
        cnt = jnp.minimum(pos + 1, w).astype(F32)
        outs.append(acc / cnt - hn[:, gsl])
    y = _pool_matmul(jnp.concatenate(outs, axis=-1), wg_ref) * scale_ref[...]
    o_ref[...] = x_ref[...] + gate_ref[...] * y
    hist = hp_ref[top + tm - C_STATE:top + tm, :]
    hp_ref[top - C_STATE:top, :] = hist

    @pl.when(ti == tiles_per_seq - 1)
    def _():
        pnew_ref[...] = hist


def _pool_call(x, g, mod, prev, w_g, scale, nb, tm, start, name):
    t, d = x.shape
    tiles = t // nb // tm
    ng = len(C_WINDOWS)
    return pl.pallas_call(
        functools.partial(_pool_body, tiles_per_seq=tiles, start=start),
        grid=(t // tm,),
        in_specs=[pl.BlockSpec((tm, d), lambda i: (i, 0)),
                  pl.BlockSpec((1, d), lambda i: (0, 0)),
                  mod.spec(1, tm, 1), mod.spec(0, tm, 1), mod.spec(2, tm, 1),
                  pl.BlockSpec((None, C_STATE, d), lambda i: (i // tiles, 0, 0)),
                  pl.BlockSpec((ng, C_GROUP_DIM, C_GROUP_DIM), lambda i: (0, 0, 0)),
                  pl.BlockSpec((1, d), lambda i: (0, 0))],
        out_specs=[pl.BlockSpec((tm, d), lambda i: (i, 0)),
                   pl.BlockSpec((None, C_STATE, d), lambda i: (i // tiles, 0, 0))],
        out_shape=[jax.ShapeDtypeStruct((t, d), F32), jax.ShapeDtypeStruct((nb, C_STATE, d), F32)],
        scratch_shapes=[pltpu.VMEM((16 + tm, d), F32)],
        compiler_params=_cparams(("arbitrary",)),
        name=name,
    )(x, g.reshape(1, d), mod.arr, mod.arr, mod.arr, prev, w_g, scale.reshape(1, d))


def _pool_step_body(x_ref, g_ref, sc_ref, sh_ref, gate_ref, prev_ref, wg_ref, scale_ref, o_ref, pnew_ref, *, start):
    hn = _prenorm(x_ref[...], g_ref[...], sc_ref[...], sh_ref[...])
    outs = []
    for gi, w in enumerate(C_WINDOWS):
        gsl = slice(gi * C_GROUP_DIM, (gi + 1) * C_GROUP_DIM)
        acc = hn[:, gsl]
        for k in range(1, w):
            acc = acc + prev_ref[:, C_STATE - k, gsl]
        outs.append(acc / float(min(start + 1, w)) - hn[:, gsl])
    y = _pool_matmul(jnp.concatenate(outs, axis=-1), wg_ref) * scale_ref[...]
    o_ref[...] = x_ref[...] + gate_ref[...] * y
    for k in range(C_STATE - 1):
        pnew_ref[:, k, :] = prev_ref[:, k + 1, :]
    pnew_ref[:, C_STATE - 1, :] = hn


def _pool_step_call(x, g, mod, prev, w_g, scale, start, name):
    t, d = x.shape
    bt = 32
    ng = len(C_WINDOWS)
    return pl.pallas_call(
        functools.partial(_pool_step_body, start=start),
        grid=(t // bt,),
        in_specs=[pl.BlockSpec((bt, d), lambda i: (i, 0)),
                  pl.BlockSpec((1, d), lambda i: (0, 0)),
                  mod.spec(1, bt, 1), mod.spec(0, bt, 1), mod.spec(2, bt, 1),
                  pl.BlockSpec((bt, C_STATE, d), lambda i: (i, 0, 0)),
                  pl.BlockSpec((ng, C_GROUP_DIM, C_GROUP_DIM), lambda i: (0, 0, 0)),
                  pl.BlockSpec((1, d), lambda i: (0, 0))],
        out_specs=[pl.BlockSpec((bt, d), lambda i: (i, 0)),
                   pl.BlockSpec((bt, C_STATE, d), lambda i: (i, 0, 0))],
        out_shape=[jax.ShapeDtypeStruct((t, d), F32), jax.ShapeDtypeStruct((t, C_STATE, d), F32)],
        compiler_params=_cparams(("arbitrary",)),
        name=name,
    )(x, g.reshape(1, d), mod.arr, mod.arr, mod.arr, prev, w_g, scale.reshape(1, d))


def _route_rows(s, b):
    npg = EXPERTS_PER_GROUP
    gscore = []
    for q in range(N_EXPERT_GROUPS):
        v = b[q * npg:(q + 1) * npg]
        best = None
        for i in range(npg):
            for j in range(i + 1, npg):
                best = v[i] + v[j] if best is None else jnp.maximum(best, v[i] + v[j])
        gscore.append(best)
    gsel = jnp.zeros_like(gscore[0], dtype=I32)
    gbest = gscore[0]
    for q in range(1, N_EXPERT_GROUPS):
        better = gscore[q] > gbest
        gsel = jnp.where(better, q, gsel)
        gbest = jnp.where(better, gscore[q], gbest)
    vb, vs = [], []
    for k in range(npg):
        bk, sk = b[k], s[k]
        for q in range(1, N_EXPERT_GROUPS):
            bk = jnp.where(gsel == q, b[q * npg + k], bk)
            sk = jnp.where(gsel == q, s[q * npg + k], sk)
        vb.append(bk)
        vs.append(sk)
    i1 = jnp.zeros_like(gsel)
    m1 = vb[0]
    for k in range(1, npg):
        better = vb[k] > m1
        i1 = jnp.where(better, k, i1)
        m1 = jnp.where(better, vb[k], m1)
    i2 = jnp.full_like(gsel, -1)
    m2 = jnp.zeros_like(m1)
    for k in range(npg):
        better = (i1 != k) & ((i2 < 0) | (vb[k] > m2))
        i2 = jnp.where(better, k, i2)
        m2 = jnp.where(better, vb[k], m2)
    s1 = vs[0]
    s2 = vs[0]
    for k in range(1, npg):
        s1 = jnp.where(i1 == k, vs[k], s1)
        s2 = jnp.where(i2 == k, vs[k], s2)
    w1 = s1 / (s1 + s2)
    w2 = s2 / (s1 + s2)
    lo = jnp.minimum(i1, i2)
    hi = jnp.maximum(i1, i2)
    pair = jnp.zeros_like(gsel)
    for p, (a, c) in enumerate(PAIRS):
        pair = jnp.where((lo == a) & (hi == c), p, pair)
    first_is_lo = i1 < i2
    return (gsel * len(PAIRS) + pair, jnp.where(first_is_lo, w1, w2), jnp.where(first_is_lo, w2, w1))


def _route_body(x_ref, g_ref, sc_ref, sh_ref, rw_ref, rb_ref, cnt_in_ref, *rest, aliased):
    if aliased:
        rest = rest[3:]
    rows_ref, bucket_ref, rank_ref, cnt_ref = rest
    tm = x_ref.shape[0]

    @pl.when(pl.program_id(0) == 0)
    def _():
        cnt_ref[...] = cnt_in_ref[...]

    hn = _prenorm(x_ref[...], g_ref[...], sc_ref[...], sh_ref[...])
    logits_t = _dot3(hn, rw_ref[...]).T
    scores = _sigmoid(logits_t[0:N_EXPERTS, :])
    biased = scores + rb_ref[0:N_EXPERTS, :]
    bucket, w_lo, w_hi = _route_rows([scores[e:e + 1, :] for e in range(N_EXPERTS)],
                                     [biased[e:e + 1, :] for e in range(N_EXPERTS)])
    bucket_ref[...] = bucket
    onehot = (lax.broadcasted_iota(I32, (BUCKET_ROWS, tm), 0) == bucket).astype(F32)
    before = (lax.broadcasted_iota(I32, (tm, tm), 0) < lax.broadcasted_iota(I32, (tm, tm), 1)).astype(BF16)
    earlier = _dot(onehot.astype(BF16), before) + cnt_ref[:, 0:1]
    rank_ref[...] = jnp.sum(onehot * earlier, axis=0, keepdims=True).astype(I32)
    cnt_ref[...] = cnt_ref[...] + jnp.sum(onehot, axis=1, keepdims=True)

    sub = lax.broadcasted_iota(I32, (LANE, tm), 0)
    gate_t = jnp.where(sub == 0, w_lo, jnp.where(sub == 1, w_hi, 0.0))
    rows_ref[pl.ds(GATE_ROW, tm, stride=ROW_PITCH), :] = gate_t.T
    for j in range(ROW_PLANES):
        rows_ref[pl.ds(j, tm, stride=ROW_PITCH), :] = hn[:, j * LANE:(j + 1) * LANE]


def _route_call(x, g, mod, router_w, router_b, counts, tm, total_rows, row_offset, prior, name):
    t, d = x.shape
    blk0 = row_offset // tm
    rw = jnp.pad(router_w, ((0, 0), (0, LANE - N_EXPERTS)))
    rb = jnp.pad(router_b.reshape(-1, 1), ((0, LANE - N_EXPERTS), (0, 0)))
    in_specs = [pl.BlockSpec((tm, d), lambda i: (i, 0)),
                pl.BlockSpec((1, d), lambda i: (0, 0)),
                mod.spec(4, tm, 1), mod.spec(3, tm, 1),
                pl.BlockSpec((d, LANE), lambda i: (0, 0)),
                pl.BlockSpec((LANE, 1), lambda i: (0, 0)),
                pl.BlockSpec((BUCKET_ROWS, LANE), lambda i: (0, 0))]
    args = [x, g.reshape(1, d), mod.arr, mod.arr, rw, rb, counts]
    aliases = {}
    if prior is not None:
        in_specs += [pl.BlockSpec(memory_space=pl.ANY)] * len(prior)
        aliases = {len(args) + k: k for k in range(len(prior))}
        args += list(prior)
    return pl.pallas_call(
        functools.partial(_route_body, aliased=prior is not None),
        grid=(t // tm,),
        in_specs=in_specs,
        out_specs=[pl.BlockSpec((tm * ROW_PITCH, LANE), lambda i: (blk0 + i, 0)),
                   pl.BlockSpec((1, tm), lambda i: (0, blk0 + i)),
                   pl.BlockSpec((1, tm), lambda i: (0, blk0 + i)),
                   pl.BlockSpec((BUCKET_ROWS, LANE), lambda i: (0, 0))],
        out_shape=[jax.ShapeDtypeStruct((total_rows * ROW_PITCH, LANE), F32),
                   jax.ShapeDtypeStruct((1, total_rows), I32),
                   jax.ShapeDtypeStruct((1, total_rows), I32),
                   jax.ShapeDtypeStruct((BUCKET_ROWS, LANE), F32)],
        input_output_aliases=aliases,
        compiler_params=_cparams(("arbitrary",)),
        name=name,
    )(*args)


def _invert_body(dest_ref, init_ref, gather_ref):
    pltpu.sync_copy(init_ref, gather_ref)

    def put(t, carry):
        gather_ref[dest_ref[t]] = t
        return carry

    lax.fori_loop(0, dest_ref.shape[0], put, 0, unroll=8)


def _invert_call(dest, n_slots, name):
    smem = pl.BlockSpec(memory_space=pltpu.SMEM)
    return pl.pallas_call(
        _invert_body,
        in_specs=[smem, pl.BlockSpec(memory_space=pl.ANY)],
        out_specs=smem,
        out_shape=jax.ShapeDtypeStruct((n_slots,), I32),
        name=name,
    )(dest, jnp.zeros((n_slots,), I32))


def _unrolled(lo, hi, body, carry):
    for b in range(lo, hi):
        carry = body(b, carry)
    return carry


def _experts_body(grp_ref, lo_ref, hi_ref, fresh_ref, valid_ref, out_ref, gather_ref, rows_ref, wg_in_ref,
                  wu_in_ref, wd_in_ref, y_ref, wg_ref, wu_ref, wd_ref, xb_ref, gsem):
    i = pl.program_id(0)
    n = pl.num_programs(0)
    tile = EXPERT_TILE
    copied = GATE_ROW + 1
    group = 8
    del grp_ref, out_ref

    def start_gather(step, slot, loop):
        def body(b, carry):
            for k in range(group):
                r = b * group + k
                tok = gather_ref[step * tile + r]
                pltpu.make_async_copy(rows_ref.at[pl.ds(tok * ROW_PITCH, copied)],
                                      xb_ref.at[slot, pl.ds(r * ROW_PITCH, copied)],
                                      gsem.at[slot]).start(priority=k % 2)
            return carry

        loop(0, tile // group, body, 0)

    def wait_gather(slot):
        rows = tile * copied
        pltpu.make_async_copy(rows_ref.at[pl.ds(0, rows)], xb_ref.at[slot, pl.ds(0, rows)], gsem.at[slot]).wait()

    @pl.when(fresh_ref[i] == 1)
    def _():
        for e in range(EXPERTS_PER_GROUP):
            wg_ref[e] = wg_in_ref[e].astype(BF16)
            wu_ref[e] = wu_in_ref[e].astype(BF16)
            wd_ref[e] = wd_in_ref[e].astype(BF16)

    nxt = jnp.minimum(i + 1, n - 1)
    more = jnp.logical_and(i + 1 < n, valid_ref[nxt] == 1)

    @pl.when(valid_ref[i] == 1)
    def _():
        slot = i % GATHER_SLOTS

        @pl.when(i == 0)
        def _():
            for ahead in range(GATHER_SLOTS - 1):
                start_gather(ahead, ahead, lax.fori_loop)

        wait_gather(slot)
        ahead = i + GATHER_SLOTS - 1
        start_gather(jnp.minimum(ahead, n - 1), ahead % GATHER_SLOTS, _unrolled)
        x = jnp.concatenate([xb_ref[slot, pl.ds(j, tile, stride=ROW_PITCH), :].astype(BF16)
                             for j in range(ROW_PLANES)], axis=-1)

        def expert(e):
            act = (_silu(_dot(x, wg_ref[e])) * _dot(x, wu_ref[e])).astype(BF16)
            return _dot(act, wd_ref[e])

        w = xb_ref[slot, pl.ds(GATE_ROW, tile, stride=ROW_PITCH), :]
        y = w[:, 0:1] * expert(lo_ref[i]) + w[:, 1:2] * expert(hi_ref[i])
        for j in range(ROW_PLANES):
            y_ref[pl.ds(j, tile, stride=ROW_PITCH), :] = y[:, j * LANE:(j + 1) * LANE]

        @pl.when(jnp.logical_not(more))
        def _():
            for k in range(1, GATHER_SLOTS):
                wait_gather((i + k) % GATHER_SLOTS)


def _experts_call(grp, lo, hi, fresh, valid, out_tile, gather_idx, rows, w_gate, w_up, w_down, layer, name):
    n_tiles = grp.shape[0]
    tile = EXPERT_TILE
    d, f = D_MODEL, D_EXPERT
    npg = EXPERTS_PER_GROUP
    group_block = lambda i, grp, *_: (layer, grp[i], 0, 0)
    once = pl.Buffered(1)
    return pl.pallas_call(
        _experts_body,
        grid_spec=pltpu.PrefetchScalarGridSpec(
            num_scalar_prefetch=7,
            grid=(n_tiles,),
            in_specs=[pl.BlockSpec(memory_space=pl.ANY),
                      pl.BlockSpec((None, npg, d, f), group_block, pipeline_mode=once),
                      pl.BlockSpec((None, npg, d, f), group_block, pipeline_mode=once),
                      pl.BlockSpec((None, npg, f, d), group_block, pipeline_mode=once)],
            out_specs=pl.BlockSpec((tile * ROW_PITCH, LANE), lambda i, g, l, h, fr, va, out, *_: (out[i], 0)),
            scratch_shapes=[pltpu.VMEM((npg, d, f), BF16), pltpu.VMEM((npg, d, f), BF16),
                            pltpu.VMEM((npg, f, d), BF16),
                            pltpu.VMEM((GATHER_SLOTS, tile * ROW_PITCH, LANE), F32),
                            pltpu.SemaphoreType.DMA((GATHER_SLOTS,))]),
        out_shape=jax.ShapeDtypeStruct((n_tiles * tile * ROW_PITCH, LANE), F32),
        compiler_params=_cparams(("arbitrary",)),
        name=name,
    )(grp, lo, hi, fresh, valid, out_tile, gather_idx, rows, w_gate, w_up, w_down)


def _moe_res_body(dest_ref, y_ref, x_ref, gate_ref, fg_ref, o_ref, *rest, row_offset, final):
    if final:
        on_ref, buf_ref, sem = rest
    else:
        buf_ref, sem = rest
    i = pl.program_id(0)
    tm = x_ref.shape[0]
    group = 8

    def start_gather(step, slot, loop):
        base = row_offset + step * tm

        def body(b, carry):
            for k in range(group):
                r = b * group + k
                pltpu.make_async_copy(y_ref.at[pl.ds(dest_ref[base + r] * ROW_PITCH, ROW_PLANES)],
                                      buf_ref.at[slot, pl.ds(r * ROW_PITCH, ROW_PLANES)],
                                      sem.at[slot]).start(priority=k % 2)
            return carry

        loop(0, tm // group, body, 0)

    def wait_gather(slot):
        rows = tm * ROW_PLANES
        pltpu.make_async_copy(y_ref.at[pl.ds(0, rows)], buf_ref.at[slot, pl.ds(0, rows)], sem.at[slot]).wait()

    @pl.when(i == 0)
    def _():
        start_gather(0, 0, lax.fori_loop)

    slot = i % 2
    wait_gather(slot)

    @pl.when(i + 1 < pl.num_programs(0))
    def _():
        start_gather(i + 1, 1 - slot, lax.fori_loop)
    y = jnp.concatenate([buf_ref[slot, pl.ds(j, tm, stride=ROW_PITCH), :] for j in range(ROW_PLANES)], axis=-1)
    xn = x_ref[...] + gate_ref[...] * y
    o_ref[...] = xn
    if final:
        ms = jnp.mean(xn * xn, axis=-1, keepdims=True)
        on_ref[...] = xn * lax.rsqrt(ms + EPS) * fg_ref[...]


def _moe_res_call(dest, y_sorted, x, mod, final_g, tm, row_offset, final, name):
    t, d = x.shape
    n_out = 2 if final else 1
    gate_spec = mod.spec(5, tm, 1)
    gate_map = gate_spec.index_map
    return pl.pallas_call(
        functools.partial(_moe_res_body, row_offset=row_offset, final=final),
        grid_spec=pltpu.PrefetchScalarGridSpec(
            num_scalar_prefetch=1,
            grid=(t // tm,),
            in_specs=[pl.BlockSpec(memory_space=pl.ANY),
                      pl.BlockSpec((tm, d), lambda i, dest: (i, 0)),
                      pl.BlockSpec(gate_spec.block_shape, lambda i, dest: gate_map(i)),
                      pl.BlockSpec((1, d), lambda i, dest: (0, 0))],
            out_specs=[pl.BlockSpec((tm, d), lambda i, dest: (i, 0))] * n_out,
            scratch_shapes=[pltpu.VMEM((2, tm * ROW_PITCH, LANE), F32), pltpu.SemaphoreType.DMA((2,))]),
        out_shape=[jax.ShapeDtypeStruct((t, d), F32)] * n_out,
        compiler_params=_cparams(("arbitrary",)),
        name=name,
    )(dest, y_sorted, x, mod.arr, final_g.reshape(1, d))


def _tile_tables(counts, n_tiles):
    tile = EXPERT_TILE
    tiles_per_bucket = (counts + tile - 1) // tile
    ends = jnp.cumsum(tiles_per_bucket)
    starts = ends - tiles_per_bucket
    used = ends[-1]
    ti = jnp.arange(n_tiles, dtype=I32)
    valid = (ti < used).astype(I32)
    tile_bucket = jnp.sum((jnp.minimum(ti, used - 1)[:, None] >= ends[None, :]).astype(I32), axis=1)
    pair_lo = jnp.asarray([p[0] for p in PAIRS], I32)
    pair_hi = jnp.asarray([p[1] for p in PAIRS], I32)
    grp = tile_bucket // len(PAIRS)
    lo = pair_lo[tile_bucket % len(PAIRS)]
    hi = pair_hi[tile_bucket % len(PAIRS)]
    fresh = jnp.concatenate([jnp.ones((1,), I32), (grp[1:] != grp[:-1]).astype(I32)])
    return grp, lo, hi, fresh, valid, jnp.minimum(ti, used - 1).astype(I32), (starts * tile).astype(I32)


def _moe_layer(xp, xs, norm_g2, mod_p, mod_s, router_w, router_b, w_gate, w_up, w_down, layer, final_g, final):
    tp, ts = xp.shape[0], xs.shape[0]
    total = tp + ts
    n_tiles = -(-total // EXPERT_TILE) + N_BUCKETS
    zero_counts = jnp.zeros((BUCKET_ROWS, LANE), F32)
    rows, bucket, rank, counts = _route_call(xp, norm_g2, mod_p, router_w, router_b, zero_counts, TM_ROUTE, total, 0,
                                             None, "route_prompt")
    rows, bucket, rank, counts = _route_call(xs, norm_g2, mod_s, router_w, router_b, counts, ts, total, tp,
                                             (rows, bucket, rank), "route_sample")
    grp, lo, hi, fresh, valid, out_tile, bucket_start = _tile_tables(counts[:N_BUCKETS, 0].astype(I32), n_tiles)
    dest = bucket_start[bucket[0]] + rank[0]
    gather_idx = _invert_call(dest, n_tiles * EXPERT_TILE, "invert")
    y_sorted = _experts_call(grp, lo, hi, fresh, valid, out_tile, gather_idx, rows, w_gate, w_up, w_down, layer,
                             "experts")
    outp = _moe_res_call(dest, y_sorted, xp, mod_p, final_g, TM_MOE_RES, 0, final, "moe_res_prompt")
    outs = _moe_res_call(dest, y_sorted, xs, mod_s, final_g, ts, tp, final, "moe_res_sample")
    return outp, outs


def _mamba_layer(x, g, mod, conv_prev, ssm, w_zx, w_dt, conv_w, conv_b, dt_bias, a_log, d_skip, norm_g, w_out,
                 tm, tag):
    pad_h = lambda v: jnp.pad(v.reshape(1, -1), ((0, 0), (0, LANE - A_N_HEADS)))
    tm_in = min(2 * tm, x.shape[0])
    proj = _norm_mm_call(x, g, mod, 1, 0, w_zx, tm_in, A_IN_TN, BF16, "a_in_" + tag)
    dt_raw = _norm_mm3_call(x, g, mod, 1, 0, w_dt, tm, "a_dt_" + tag)
    d_x = jnp.repeat(d_skip, A_HEAD_DIM).reshape(1, A_D_INNER)
    weights = (conv_w, conv_b.reshape(1, -1), pad_h(dt_bias), pad_h(a_log), d_x, norm_g.reshape(1, -1))
    if ssm[0] == "step":
        yn, conv_new, ssm_new = _ssd_step_call(proj, dt_raw, conv_prev, ssm[1], ssm[2], ssm[3], *weights,
                                               "ssd_step_" + tag)
    else:
        yn, conv_new, ssm_new = _ssd_call(proj, dt_raw, conv_prev, ssm[2], *weights, ssm[1], "ssd_" + tag)
    x = _out_res_call(yn, w_out, x, mod, 2, tm, "a_out_" + tag)
    return x, conv_new, ssm_new


def kernel(x_prompt, x_sample, c_prompt, c_sample, state_a_conv, state_a_ssm, state_c_pool, w_mod, b_mod, norm_g, final_g, a_w_in, a_conv_w, a_conv_b, a_dt_bias, a_log, a_d, a_norm_g, a_w_out, b_w_in, b_b_in, b_ln_g, b_ln_b, b_w_s, b_b_s, b_w_out, c_w_g, c_scale, router_w, router_b, e_w_gate, e_w_up, e_w_down):
    bp, seq, d = x_prompt.shape
    bs = x_sample.shape[0]
    n_a, n_c = state_a_conv.shape[0], state_c_pool.shape[0]
    mod_all = _mod_call(jnp.concatenate([c_prompt, c_sample], axis=0), w_mod, b_mod)
    mod_p_arr = mod_all[:, :bp].reshape(DEPTH, bp, 6, 1, d)
    mod_s_arr = mod_all[:, bp:]
    xp = x_prompt.reshape(bp * seq, d)
    xs = x_sample.reshape(bs, d)
    conv_p, ssm_p, pool_p, conv_s, pool_s, v_s = [], [], [], [], [], []
    hp = A_N_HEADS * A_HEAD_DIM
    ssm_s_in = state_a_ssm.reshape(n_a, bs, hp, A_D_STATE)
    ssm_s_out = None
    yp = ys = None
    for i in range(DEPTH):
        kind, s = LAYER_KIND[i], LAYER_SLOT[i]
        mod_p = Mod(mod_p_arr, i, False, seq)
        mod_s = Mod(mod_s_arr, i, True)
        g1 = norm_g[i, 0]
        if kind == 0:
            w_zx = a_w_in[s, :, :A_ZX].astype(BF16)
            w_dt = jnp.pad(a_w_in[s, :, A_ZX:], ((0, 0), (0, LANE - A_N_HEADS)))
            weights = (w_zx, w_dt, a_conv_w[s], a_conv_b[s], a_dt_bias[s], a_log[s], a_d[s], a_norm_g[s], a_w_out[s])
            conv0 = jnp.zeros((bp, A_CONV - 1, A_CONV_CH), F32)
            ssm0 = jnp.zeros((bp, hp, A_D_STATE), F32)
            xp, cv, ss = _mamba_layer(xp, g1, mod_p, conv0, ("prompt", bp, ssm0), *weights, TM_MAMBA, "p%d" % i)
            conv_p.append(cv)
            ssm_p.append(ss.reshape(bp, A_N_HEADS, A_HEAD_DIM, A_D_STATE))
            xs, cv, ssm_s_out = _mamba_layer(xs, g1, mod_s, state_a_conv[s], ("step", ssm_s_in, s, ssm_s_out),
                                             *weights, bs, "s%d" % i)
            conv_s.append(cv)
        elif kind == 1:
            w_uv = b_w_in[s].astype(BF16)
            uv = _gmlp_in_call(xp, g1, mod_p, w_uv, b_b_in[s], b_ln_g[s], b_ln_b[s], TM_GMLP_IN, BF16,
                               "b_in_p%d" % i)
            xp = _gmlp_out_call(uv, b_w_s[s], b_b_s[s], b_w_out[s], xp, mod_p, TM_GMLP_OUT, "b_out_p%d" % i)
            uv = _gmlp_in_call(xs, g1, mod_s, w_uv, b_b_in[s], b_ln_g[s], b_ln_b[s], bs, F32, "b_in_s%d" % i)
            xs = _gmlp_out_step_call(uv, b_w_s[s], b_b_s[s], b_w_out[s], xs, mod_s, "b_out_s%d" % i)
            v_s.append(uv[:, B_D:].reshape(bs, 1, B_D))
        else:
            pool0 = jnp.zeros((bp, C_STATE, d), F32)
            xp, pr = _pool_call(xp, g1, mod_p, pool0, c_w_g[s], c_scale[s], bp, TM_POOL, 0, "pool_p%d" % i)
            pool_p.append(pr)
            xs, pr = _pool_step_call(xs, g1, mod_s, state_c_pool[s], c_w_g[s], c_scale[s], PAST_LEN, "pool_s%d" % i)
            pool_s.append(pr)
        final = i == DEPTH - 1
        outp, outs = _moe_layer(xp, xs, norm_g[i, 1], mod_p, mod_s, router_w, router_b, e_w_gate, e_w_up, e_w_down, i,
                                final_g, final)
        xp, xs = outp[0], outs[0]
        if final:
            yp, ys = outp[1], outs[1]
    return (yp.reshape(bp, seq, d), ys.reshape(bs, 1, d), jnp.stack(conv_p), jnp.stack(ssm_p), jnp.stack(pool_p),
            jnp.stack(conv_s), ssm_s_out.reshape(state_a_ssm.shape), jnp.stack(pool_s), jnp.stack(v_s))
```

```python
import functools
import math

import numpy as np
import jax
import jax.numpy as jnp
from jax import lax
from jax.experimental import pallas as pl
from jax.experimental.pallas import tpu as pltpu

F32 = jnp.float32
BF16 = jnp.bfloat16
I32 = jnp.int32
EPS = 1e-6

LANE = 128
D_MODEL = 1024
DEPTH = 4
PAST_LEN = 16384
LAYER_KIND = (0, 1, 2, 0)
LAYER_SLOT = (0, 0, 0, 1)
A_D_INNER = 2 * D_MODEL
A_HEAD_DIM = 64
A_N_HEADS = A_D_INNER // A_HEAD_DIM
A_N_GROUPS = 8
A_HPG = A_N_HEADS // A_N_GROUPS
A_D_STATE = 128
A_GN = A_N_GROUPS * A_D_STATE
A_CONV = 4
A_CONV_CH = A_D_INNER + 2 * A_GN
A_ZX = A_D_INNER + A_CONV_CH
A_CHUNK = 128
A_GW = A_HPG * A_HEAD_DIM
B_D = 2 * D_MODEL
B_N_GROUPS = 8
B_GROUP_DIM = B_D // B_N_GROUPS
B_CHUNK = 128
C_WINDOWS = (2, 4, 8, 16)
C_GROUP_DIM = D_MODEL // len(C_WINDOWS)
C_STATE = max(C_WINDOWS) - 1
N_EXPERTS = 16
N_EXPERT_GROUPS = 4
EXPERTS_PER_GROUP = 4
D_EXPERT = D_MODEL // 2
PAIRS = ((0, 1), (0, 2), (0, 3), (1, 2), (1, 3), (2, 3))
N_BUCKETS = N_EXPERT_GROUPS * len(PAIRS)
BUCKET_ROWS = 32
ROW_PLANES = D_MODEL // LANE
ROW_PITCH = ROW_PLANES + 4
GATE_ROW = ROW_PLANES
EXPERT_TILE = 256
GATHER_SLOTS = 6
VMEM_LIMIT = 56 * 1024 * 1024
TM_MAMBA = 1024
A_IN_TN = 1024
TM_GMLP_IN = 1024
TM_GMLP_OUT = 512
TM_POOL = 1024
TM_ROUTE = 512
TM_MOE_RES = 1024


def _cparams(sem, vmem=VMEM_LIMIT):
    return pltpu.CompilerParams(dimension_semantics=sem, vmem_limit_bytes=vmem)


def _sigmoid(x):
    return 1.0 / (1.0 + jnp.exp(-x))


def _silu(x):
    return x * _sigmoid(x)


def _gelu_tanh(x):
    c = 2.0 * math.sqrt(2.0 / math.pi)
    return x / (1.0 + jnp.exp(x * (-c - (c * 0.044715) * (x * x))))


def _softplus(x):
    return jnp.maximum(x, 0.0) + jnp.log1p(jnp.exp(-jnp.abs(x)))


def _split2(a):
    hi = a.astype(BF16)
    lo = (a - hi.astype(F32)).astype(BF16)
    return hi, lo


def _split3(a):
    hi = a.astype(BF16)
    r = a - hi.astype(F32)
    mid = r.astype(BF16)
    lo = (r - mid.astype(F32)).astype(BF16)
    return hi, mid, lo


def _dot(a, b):
    return jnp.dot(a, b, preferred_element_type=F32)


def _dot_nt(a, b):
    return lax.dot_general(a, b, (((1,), (1,)), ((), ())), preferred_element_type=F32)


def _dot_tn(a, b):
    return lax.dot_general(a, b, (((0,), (0,)), ((), ())), preferred_element_type=F32)


def _dot3(a, b):
    a_hi, a_lo = _split2(a)
    b_hi, b_lo = _split2(b)
    return _dot(a_hi, b_hi) + (_dot(a_lo, b_hi) + _dot(a_hi, b_lo))


def _prenorm(x, g, sc, sh):
    ms = jnp.mean(x * x, axis=-1, keepdims=True)
    return (x * lax.rsqrt(ms + EPS) * g) * (1.0 + sc) + sh


def _mod_body(c_ref, w_ref, b_ref, o_ref):
    o_ref[...] = _dot3(_silu(c_ref[...]), w_ref[...]) + b_ref[...]


def _mod_call(c_all, w_mod, b_mod):
    nb, d = c_all.shape
    depth, _, n = w_mod.shape
    tn = 1536
    return pl.pallas_call(
        _mod_body,
        grid=(depth, n // tn),
        in_specs=[pl.BlockSpec((nb, d), lambda i, j: (0, 0)),
                  pl.BlockSpec((None, d, tn), lambda i, j: (i, 0, j)),
                  pl.BlockSpec((None, 1, tn), lambda i, j: (i, 0, j))],
        out_specs=pl.BlockSpec((None, nb, tn), lambda i, j: (i, 0, j)),
        out_shape=jax.ShapeDtypeStruct((depth, nb, n), F32),
        compiler_params=_cparams(("arbitrary", "arbitrary")),
        name="mod",
    )(c_all, w_mod, b_mod.reshape(depth, 1, n))


class Mod:
    def __init__(self, arr, layer, per_row, rows_per_seq=None):
        self.arr, self.layer, self.per_row, self.rows_per_seq = arr, layer, per_row, rows_per_seq

    def spec(self, which, tm, ngrid):
        layer = self.layer
        if self.per_row:
            if ngrid == 1:
                return pl.BlockSpec((None, tm, D_MODEL), lambda i: (layer, i, which))
            return pl.BlockSpec((None, tm, D_MODEL), lambda i, j: (layer, i, which))
        tiles = self.rows_per_seq // tm
        if ngrid == 1:
            return pl.BlockSpec((None, None, None, 1, D_MODEL), lambda i: (layer, i // tiles, which, 0, 0))
        return pl.BlockSpec((None, None, None, 1, D_MODEL), lambda i, j: (layer, i // tiles, which, 0, 0))


def _norm_mm_body(x_ref, g_ref, sc_ref, sh_ref, w_ref, o_ref, hn_ref):
    @pl.when(pl.program_id(1) == 0)
    def _():
        hn_ref[...] = _prenorm(x_ref[...], g_ref[...], sc_ref[...], sh_ref[...]).astype(BF16)

    o_ref[...] = _dot(hn_ref[...], w_ref[...]).astype(o_ref.dtype)


def _norm_mm_call(x, g, mod, which_sc, which_sh, w, tm, tn, out_dtype, name):
    t, d = x.shape
    n_cols = w.shape[1]
    return pl.pallas_call(
        _norm_mm_body,
        grid=(t // tm, n_cols // tn),
        in_specs=[pl.BlockSpec((tm, d), lambda i, j: (i, 0)),
                  pl.BlockSpec((1, d), lambda i, j: (0, 0)),
                  mod.spec(which_sc, tm, 2), mod.spec(which_sh, tm, 2),
                  pl.BlockSpec((d, tn), lambda i, j: (0, j))],
        out_specs=pl.BlockSpec((tm, tn), lambda i, j: (i, j)),
        out_shape=jax.ShapeDtypeStruct((t, n_cols), out_dtype),
        scratch_shapes=[pltpu.VMEM((tm, d), BF16)],
        compiler_params=_cparams(("arbitrary", "arbitrary")),
        name=name,
    )(x, g.reshape(1, d), mod.arr, mod.arr, w)


def _norm_mm3_body(x_ref, g_ref, sc_ref, sh_ref, w_ref, o_ref):
    hn = _prenorm(x_ref[...], g_ref[...], sc_ref[...], sh_ref[...])
    o_ref[...] = _dot3(hn, w_ref[...])


def _norm_mm3_call(x, g, mod, which_sc, which_sh, w, tm, name):
    t, d = x.shape
    n = w.shape[1]
    return pl.pallas_call(
        _norm_mm3_body,
        grid=(t // tm,),
        in_specs=[pl.BlockSpec((tm, d), lambda i: (i, 0)),
                  pl.BlockSpec((1, d), lambda i: (0, 0)),
                  mod.spec(which_sc, tm, 1), mod.spec(which_sh, tm, 1),
                  pl.BlockSpec((d, n), lambda i: (0, 0))],
        out_specs=pl.BlockSpec((tm, n), lambda i: (i, 0)),
        out_shape=jax.ShapeDtypeStruct((t, n), F32),
        compiler_params=_cparams(("arbitrary",)),
        name=name,
    )(x, g.reshape(1, d), mod.arr, mod.arr, w)


def _out_res_body(y_ref, w_ref, x_ref, gate_ref, o_ref, wbf_ref):
    @pl.when(pl.program_id(0) == 0)
    def _():
        wbf_ref[...] = w_ref[...].astype(BF16)

    o_ref[...] = x_ref[...] + gate_ref[...] * _dot(y_ref[...], wbf_ref[...])


def _out_res_call(y, w, x, mod, which_gate, tm, name):
    t, k = y.shape
    d = x.shape[1]
    return pl.pallas_call(
        _out_res_body,
        grid=(t // tm,),
        in_specs=[pl.BlockSpec((tm, k), lambda i: (i, 0)),
                  pl.BlockSpec((k, d), lambda i: (0, 0)),
                  pl.BlockSpec((tm, d), lambda i: (i, 0)),
                  mod.spec(which_gate, tm, 1)],
        out_specs=pl.BlockSpec((tm, d), lambda i: (i, 0)),
        out_shape=jax.ShapeDtypeStruct((t, d), F32),
        scratch_shapes=[pltpu.VMEM((k, d), BF16)],
        compiler_params=_cparams(("arbitrary",)),
        name=name,
    )(y, w, x, mod.arr)


def _head_expand():
    h = np.arange(LANE)[:, None]
    c = np.arange(A_D_INNER)[None, :]
    return jnp.asarray((c // A_HEAD_DIM == h).astype(np.float32), dtype=BF16)


def _gate_norm(y, z, ng):
    gated = y * _silu(z)
    ms = jnp.mean(gated * gated, axis=-1, keepdims=True)
    return gated * lax.rsqrt(ms + EPS) * ng


def _ssd_body(z_ref, xs_ref, bc_ref, dt_ref, cprev_ref, sprev_ref, cw_ref, cb_ref, dtb_ref, alog_ref,
              dx_ref, ng_ref, exp_ref, yn_ref, cnew_ref, snew_ref, tail_ref, act_ref):
    c = pl.program_id(1)
    q = A_CHUNK
    nt = A_CONV - 1

    @pl.when(c == 0)
    def _():
        tail_ref[0:nt, :] = cprev_ref[...]
        snew_ref[...] = sprev_ref[...]

    row = lax.broadcasted_iota(I32, (q, q), 0)
    col = lax.broadcasted_iota(I32, (q, q), 1)
    shifts = [jnp.where(row - col == nt - k, 1.0, 0.0).astype(BF16) for k in range(nt)]
    sub = lax.broadcasted_iota(I32, (8, 1), 0)
    cw = 512
    for j in range(A_CONV_CH // cw):
        sl = slice(j * cw, (j + 1) * cw)
        src = xs_ref if (j + 1) * cw <= A_D_INNER else bc_ref
        off = j * cw if src is xs_ref else j * cw - A_D_INNER
        x_bf = src[:, off:off + cw]
        conv = cb_ref[:, sl] + x_bf.astype(F32) * cw_ref[nt:nt + 1, sl]
        for k in range(nt):
            conv = conv + _dot(shifts[k], x_bf) * cw_ref[k:k + 1, sl]
        act_ref[:, sl] = _silu(conv)
        corr = jnp.zeros((8, cw), F32)
        for l in range(nt):
            c_l = sum(tail_ref[l + k:l + k + 1, sl] * cw_ref[k:k + 1, sl] for k in range(nt - l))
            corr = jnp.where(sub == l, c_l, corr)
        act_ref[0:8, sl] = _silu(conv[0:8, :] + corr)
        tail_ref[0:nt, sl] = x_bf[q - 8:q, :].astype(F32)[8 - nt:8, :]

    @pl.when(c == pl.num_programs(1) - 1)
    def _():
        cnew_ref[...] = tail_ref[0:nt, :]

    dt = _softplus(dt_ref[...] + dtb_ref[...])
    a = dt * (-jnp.exp(alog_ref[...]))
    row = lax.broadcasted_iota(I32, (q, q), 0)
    col = lax.broadcasted_iota(I32, (q, q), 1)
    causal = row >= col
    tril = jnp.where(causal, 1.0, 0.0).astype(BF16)
    a_hi, a_mid, a_lo = _split3(a)
    cs = _dot(tril, a_hi) + (_dot(tril, a_mid) + _dot(tril, a_lo))
    cs_t = cs.T
    dt_t = dt.T
    cs_last = cs[q - 1:q, :]
    ecs = jnp.exp(cs)
    wend = jnp.exp(cs_last - cs) * dt
    st_hi, st_lo = _split2(jnp.concatenate([ecs, wend], axis=0))
    st_x = _dot(st_hi, exp_ref[...]) + _dot(st_lo, exp_ref[...])
    lane_head = lax.broadcasted_iota(I32, (q, A_GW), 1) // A_HEAD_DIM

    for g in range(A_N_GROUPS):
        gsl = slice(g * A_GW, (g + 1) * A_GW)
        b_g = act_ref[:, A_D_INNER + g * A_D_STATE:A_D_INNER + (g + 1) * A_D_STATE].astype(BF16)
        c_g = act_ref[:, A_D_INNER + A_GN + g * A_D_STATE:A_D_INNER + A_GN + (g + 1) * A_D_STATE].astype(BF16)
        x_g = act_ref[:, gsl]
        x_bf = x_g.astype(BF16)
        h_g = snew_ref[gsl, :]
        cb = _dot_nt(c_g, b_g)
        y = jnp.zeros((q, A_GW), F32)
        for r in range(A_HPG):
            h = g * A_HPG + r
            seg = cs[:, h:h + 1] - cs_t[h:h + 1, :]
            decay = jnp.exp(jnp.where(causal, seg, -1e30))
            wm = (cb * decay * dt_t[h:h + 1, :]).astype(BF16)
            y = y + _dot(wm, jnp.where(lane_head == r, x_bf, jnp.zeros_like(x_bf)))
        y = y + st_x[0:q, gsl] * _dot_nt(c_g, h_g.astype(BF16)) + dx_ref[:, gsl] * x_g
        yn_ref[:, gsl] = _gate_norm(y, z_ref[:, gsl].astype(F32), ng_ref[:, gsl]).astype(BF16)
        s_new = _dot_tn((x_g * st_x[q:2 * q, gsl]).astype(BF16), b_g)
        for r in range(A_HPG):
            h = g * A_HPG + r
            rsl = slice(g * A_GW + r * A_HEAD_DIM, g * A_GW + (r + 1) * A_HEAD_DIM)
            keep = jnp.exp(cs[q - 1:q, h:h + 1])
            snew_ref[rsl, :] = snew_ref[rsl, :] * keep + s_new[r * A_HEAD_DIM:(r + 1) * A_HEAD_DIM, :]


def _ssd_call(proj, dt_raw, conv_prev, ssm_prev, conv_w, conv_b, dt_bias, a_log, d_x, norm_g, nb, name):
    t = proj.shape[0]
    q = A_CHUNK
    nc = t // nb // q
    hp = A_N_HEADS * A_HEAD_DIM
    row = lambda b, c: (b * nc + c, 0)
    full = lambda shape: pl.BlockSpec(shape, lambda b, c: (0,) * len(shape))
    return pl.pallas_call(
        _ssd_body,
        grid=(nb, nc),
        in_specs=[pl.BlockSpec((q, A_D_INNER), lambda b, c: (b * nc + c, 0)),
                  pl.BlockSpec((q, A_D_INNER), lambda b, c: (b * nc + c, 1)),
                  pl.BlockSpec((q, 2 * A_GN), lambda b, c: (b * nc + c, 2)),
                  pl.BlockSpec((q, LANE), row),
                  pl.BlockSpec((None, A_CONV - 1, A_CONV_CH), lambda b, c: (b, 0, 0)),
                  pl.BlockSpec((None, hp, A_D_STATE), lambda b, c: (b, 0, 0)),
                  full((A_CONV, A_CONV_CH)), full((1, A_CONV_CH)), full((1, LANE)), full((1, LANE)),
                  full((1, A_D_INNER)), full((1, A_D_INNER)), full((LANE, A_D_INNER))],
        out_specs=[pl.BlockSpec((q, A_D_INNER), row),
                   pl.BlockSpec((None, A_CONV - 1, A_CONV_CH), lambda b, c: (b, 0, 0)),
                   pl.BlockSpec((None, hp, A_D_STATE), lambda b, c: (b, 0, 0))],
        out_shape=[jax.ShapeDtypeStruct((t, A_D_INNER), BF16),
                   jax.ShapeDtypeStruct((nb, A_CONV - 1, A_CONV_CH), F32),
                   jax.ShapeDtypeStruct((nb, hp, A_D_STATE), F32)],
        scratch_shapes=[pltpu.VMEM((8, A_CONV_CH), F32), pltpu.VMEM((q, A_CONV_CH), F32)],
        compiler_params=_cparams(("arbitrary", "arbitrary")),
        name=name,
    )(proj, proj, proj, dt_raw, conv_prev, ssm_prev, conv_w, conv_b, dt_bias, a_log, d_x, norm_g, _head_expand())


def _ssd_step_body(z_ref, xs_ref, bc_ref, dt_ref, cprev_ref, sprev_ref, cw_ref, cb_ref, dtb_ref, alog_ref,
                   dx_ref, ng_ref, exp_ref, yn_ref, cnew_ref, snew_ref, da_ref, y_ref):
    bt = z_ref.shape[0]
    cur = jnp.concatenate([xs_ref[...], bc_ref[...]], axis=1).astype(F32)
    conv = cb_ref[...] + cur * cw_ref[A_CONV - 1:A_CONV, :]
    for k in range(A_CONV - 1):
        conv = conv + cprev_ref[:, k, :] * cw_ref[k:k + 1, :]
    for k in range(A_CONV - 2):
        cnew_ref[:, k, :] = cprev_ref[:, k + 1, :]
    cnew_ref[:, A_CONV - 2, :] = cur
    act = _silu(conv)
    xs = act[:, 0:A_D_INNER]
    bm_bf = act[:, A_D_INNER:A_D_INNER + A_GN].astype(BF16)
    cm_bf = act[:, A_D_INNER + A_GN:A_CONV_CH].astype(BF16)
    dt = _softplus(dt_ref[...] + dtb_ref[...])
    da_ref[...] = jnp.exp(dt * (-jnp.exp(alog_ref[...])))
    dt_hi, dt_lo = _split2(dt)
    dt_x = _dot(dt_hi, exp_ref[...]) + _dot(dt_lo, exp_ref[...])
    xdt = xs * dt_x
    rows = lax.broadcasted_iota(I32, (bt, 1), 0)
    y_ref[...] = jnp.zeros_like(y_ref)

    def per_seq(j, carry):
        mine = rows == j
        xdt_j = jnp.where(mine, xdt, 0.0)
        da_j = da_ref[pl.ds(j, 1), :]
        for g in range(A_N_GROUPS):
            gsl = slice(g * A_GW, (g + 1) * A_GW)
            nsl = slice(g * A_D_STATE, (g + 1) * A_D_STATE)
            outer = _dot_tn(xdt_j[:, gsl].astype(BF16), bm_bf[:, nsl])
            for r in range(A_HPG):
                h = g * A_HPG + r
                rsl = slice(r * A_HEAD_DIM, (r + 1) * A_HEAD_DIM)
                hsl = slice(g * A_GW + r * A_HEAD_DIM, g * A_GW + (r + 1) * A_HEAD_DIM)
                snew_ref[j, hsl, :] = sprev_ref[j, hsl, :] * da_j[:, h:h + 1] + outer[rsl, :]
            yg = _dot_nt(cm_bf[:, nsl], snew_ref[j, gsl, :].astype(BF16))
            y_ref[:, gsl] = y_ref[:, gsl] + jnp.where(mine, yg, 0.0)
        return carry

    lax.fori_loop(0, bt, per_seq, 0)
    y = y_ref[...] + dx_ref[...] * xs
    z = z_ref[...].astype(F32)
    for g in range(A_N_GROUPS):
        gsl = slice(g * A_GW, (g + 1) * A_GW)
        yn_ref[:, gsl] = _gate_norm(y[:, gsl], z[:, gsl], ng_ref[:, gsl]).astype(BF16)


_SSD_STEP_INPUTS = 13


def _ssd_step_aliased_body(*refs):
    _ssd_step_body(*refs[:_SSD_STEP_INPUTS], *refs[_SSD_STEP_INPUTS + 1:])


def _ssd_step_call(proj, dt_raw, conv_prev, ssm_all, slot, ssm_out, conv_w, conv_b, dt_bias, a_log, d_x, norm_g,
                   name):
    nb = proj.shape[0]
    bt = 8
    hp = A_N_HEADS * A_HEAD_DIM
    full = lambda shape: pl.BlockSpec(shape, lambda i: (0,) * len(shape))
    state_spec = pl.BlockSpec((None, bt, hp, A_D_STATE), lambda i: (slot, i, 0, 0))
    in_specs = [pl.BlockSpec((bt, A_D_INNER), lambda i: (i, 0)),
                pl.BlockSpec((bt, A_D_INNER), lambda i: (i, 1)),
                pl.BlockSpec((bt, 2 * A_GN), lambda i: (i, 2)),
                pl.BlockSpec((bt, LANE), lambda i: (i, 0)),
                pl.BlockSpec((bt, A_CONV - 1, A_CONV_CH), lambda i: (i, 0, 0)),
                state_spec,
                full((A_CONV, A_CONV_CH)), full((1, A_CONV_CH)), full((1, LANE)), full((1, LANE)),
                full((1, A_D_INNER)), full((1, A_D_INNER)), full((LANE, A_D_INNER))]
    args = [proj, proj, proj, dt_raw, conv_prev, ssm_all, conv_w, conv_b, dt_bias, a_log, d_x, norm_g, _head_expand()]
    assert len(args) == _SSD_STEP_INPUTS
    aliases = {}
    if ssm_out is not None:
        in_specs.append(pl.BlockSpec(memory_space=pl.ANY))
        args.append(ssm_out)
        aliases = {_SSD_STEP_INPUTS: 2}
    return pl.pallas_call(
        _ssd_step_body if ssm_out is None else _ssd_step_aliased_body,
        grid=(nb // bt,),
        in_specs=in_specs,
        out_specs=[pl.BlockSpec((bt, A_D_INNER), lambda i: (i, 0)),
                   pl.BlockSpec((bt, A_CONV - 1, A_CONV_CH), lambda i: (i, 0, 0)),
                   state_spec],
        out_shape=[jax.ShapeDtypeStruct((nb, A_D_INNER), BF16),
                   jax.ShapeDtypeStruct((nb, A_CONV - 1, A_CONV_CH), F32),
                   jax.ShapeDtypeStruct(ssm_all.shape, F32)],
        scratch_shapes=[pltpu.VMEM((bt, LANE), F32), pltpu.VMEM((bt, A_D_INNER), F32)],
        input_output_aliases=aliases,
        compiler_params=_cparams(("arbitrary",)),
        name=name,
    )(*args)


def _gmlp_in_body(x_ref, g_ref, sc_ref, sh_ref, w_ref, b_ref, lg_ref, lb_ref, o_ref, hn_ref):
    j = pl.program_id(1)

    @pl.when(j == 0)
    def _():
        hn_ref[...] = _prenorm(x_ref[...], g_ref[...], sc_ref[...], sh_ref[...]).astype(BF16)

    uv = _gelu_tanh(_dot(hn_ref[...], w_ref[...]) + b_ref[...])

    @pl.when(j == 0)
    def _():
        o_ref[...] = uv.astype(o_ref.dtype)

    @pl.when(j == 1)
    def _():
        vc = uv - jnp.mean(uv, axis=-1, keepdims=True)
        var = jnp.mean(vc * vc, axis=-1, keepdims=True)
        o_ref[...] = (vc * lax.rsqrt(var + EPS) * lg_ref[...] + lb_ref[...]).astype(o_ref.dtype)


def _gmlp_in_call(x, g, mod, w, b, ln_g, ln_b, tm, out_dtype, name):
    t, d = x.shape
    return pl.pallas_call(
        _gmlp_in_body,
        grid=(t // tm, 2),
        in_specs=[pl.BlockSpec((tm, d), lambda i, j: (i, 0)),
                  pl.BlockSpec((1, d), lambda i, j: (0, 0)),
                  mod.spec(1, tm, 2), mod.spec(0, tm, 2),
                  pl.BlockSpec((d, B_D), lambda i, j: (0, j)),
                  pl.BlockSpec((1, B_D), lambda i, j: (0, j)),
                  pl.BlockSpec((1, B_D), lambda i, j: (0, 0)),
                  pl.BlockSpec((1, B_D), lambda i, j: (0, 0))],
        out_specs=pl.BlockSpec((tm, B_D), lambda i, j: (i, j)),
        out_shape=jax.ShapeDtypeStruct((t, 2 * B_D), out_dtype),
        scratch_shapes=[pltpu.VMEM((tm, d), BF16)],
        compiler_params=_cparams(("arbitrary", "arbitrary")),
        name=name,
    )(x, g.reshape(1, d), mod.arr, mod.arr, w, b.reshape(1, -1), ln_g.reshape(1, -1), ln_b.reshape(1, -1))


def _gmlp_out_body(u_ref, v_ref, ws_ref, bs_ref, w_ref, x_ref, gate_ref, o_ref, wbf_ref, wsbf_ref, m_ref):
    q = B_CHUNK

    @pl.when(pl.program_id(0) == 0)
    def _():
        wbf_ref[...] = w_ref[...].astype(BF16)
        causal = lax.broadcasted_iota(I32, (q, q), 0) >= lax.broadcasted_iota(I32, (q, q), 1)
        for g in range(B_N_GROUPS):
            wsbf_ref[g] = jnp.where(causal, ws_ref[g], 0.0).astype(BF16)

    for ci in range(u_ref.shape[0] // q):
        rsl = slice(ci * q, (ci + 1) * q)
        for g in range(B_N_GROUPS):
            gsl = slice(g * B_GROUP_DIM, (g + 1) * B_GROUP_DIM)
            mixed = _dot(wsbf_ref[g], v_ref[rsl, gsl].astype(BF16)) + bs_ref[:, g:g + 1]
            m_ref[rsl, gsl] = (u_ref[rsl, gsl].astype(F32) * mixed).astype(BF16)
    o_ref[...] = x_ref[...] + gate_ref[...] * _dot(m_ref[...], wbf_ref[...])


def _gmlp_out_call(uv, w_s, b_s, w_out, x, mod, tm, name):
    t, d = x.shape
    q = B_CHUNK
    return pl.pallas_call(
        _gmlp_out_body,
        grid=(t // tm,),
        in_specs=[pl.BlockSpec((tm, B_D), lambda i: (i, 0)),
                  pl.BlockSpec((tm, B_D), lambda i: (i, 1)),
                  pl.BlockSpec((B_N_GROUPS, q, q), lambda i: (0, 0, 0)),
                  pl.BlockSpec((q, B_N_GROUPS), lambda i: (0, 0)),
                  pl.BlockSpec((B_D, d), lambda i: (0, 0)),
                  pl.BlockSpec((tm, d), lambda i: (i, 0)),
                  mod.spec(2, tm, 1)],
        out_specs=pl.BlockSpec((tm, d), lambda i: (i, 0)),
        out_shape=jax.ShapeDtypeStruct((t, d), F32),
        scratch_shapes=[pltpu.VMEM((B_D, d), BF16), pltpu.VMEM((B_N_GROUPS, q, q), BF16),
                        pltpu.VMEM((tm, B_D), BF16)],
        compiler_params=_cparams(("arbitrary",)),
        name=name,
    )(uv, uv, w_s, b_s.T, w_out, x, mod.arr)


def _gmlp_out_step_body(u_ref, v_ref, wd_ref, bd_ref, w_ref, x_ref, gate_ref, o_ref):
    mixed = v_ref[...] * wd_ref[...] + bd_ref[...]
    m = (u_ref[...] * mixed).astype(BF16)
    o_ref[...] = x_ref[...] + gate_ref[...] * _dot(m, w_ref[...].astype(BF16))


def _gmlp_out_step_call(uv, w_s, b_s, w_out, x, mod, name):
    t, d = x.shape
    wd = jnp.repeat(w_s[:, 0, 0], B_GROUP_DIM).reshape(1, B_D)
    bd = jnp.repeat(b_s[:, 0], B_GROUP_DIM).reshape(1, B_D)
    return pl.pallas_call(
        _gmlp_out_step_body,
        grid=(1,),
        in_specs=[pl.BlockSpec((t, B_D), lambda i: (0, 0)),
                  pl.BlockSpec((t, B_D), lambda i: (0, 1)),
                  pl.BlockSpec((1, B_D), lambda i: (0, 0)),
                  pl.BlockSpec((1, B_D), lambda i: (0, 0)),
                  pl.BlockSpec((B_D, d), lambda i: (0, 0)),
                  pl.BlockSpec((t, d), lambda i: (0, 0)),
                  mod.spec(2, t, 1)],
        out_specs=pl.BlockSpec((t, d), lambda i: (0, 0)),
        out_shape=jax.ShapeDtypeStruct((t, d), F32),
        compiler_params=_cparams(("arbitrary",)),
        name=name,
    )(uv, uv, wd, bd, w_out, x, mod.arr)


def _pool_matmul(pooled, wg_ref):
    outs = []
    for gi in range(len(C_WINDOWS)):
        gsl = slice(gi * C_GROUP_DIM, (gi + 1) * C_GROUP_DIM)
        outs.append(_dot(pooled[:, gsl].astype(BF16), wg_ref[gi].astype(BF16)))
    return jnp.concatenate(outs, axis=-1)


def _pool_body(x_ref, g_ref, sc_ref, sh_ref, gate_ref, prev_ref, wg_ref, scale_ref, o_ref, pnew_ref, hp_ref,
               *, tiles_per_seq, start):
    i = pl.program_id(0)
    tm = x_ref.shape[0]
    top = 16
    ti = i % tiles_per_seq

    @pl.when(ti == 0)
    def _():
        hp_ref[top - C_STATE:top, :] = prev_ref[...]

    hn = _prenorm(x_ref[...], g_ref[...], sc_ref[...], sh_ref[...])
    hp_ref[top:top + tm, :] = hn
    pos = start + ti * tm + lax.broadcasted_iota(I32, (tm, 1), 0)
    outs = []
    for gi, w in enumerate(C_WINDOWS):
        gsl = slice(gi * C_GROUP_DIM, (gi + 1) * C_GROUP_DIM)
        acc = hn[:, gsl]
        for k in range(1, w):
            acc = acc + hp_ref[top - k:top - k + tm, gsl]
        cnt = jnp.minimum(pos + 1, w).astype(F32)
        outs.append(acc / cnt - hn[:, gsl])
    y = _pool_matmul(jnp.concatenate(outs, axis=-1), wg_ref) * scale_ref[...]
    o_ref[...] = x_ref[...] + gate_ref[...] * y
    hist = hp_ref[top + tm - C_STATE:top + tm, :]
    hp_ref[top - C_STATE:top, :] = hist

    @pl.when(ti == tiles_per_seq - 1)
    def _():
        pnew_ref[...] = hist


def _pool_call(x, g, mod, prev, w_g, scale, nb, tm, start, name):
    t, d = x.shape
    tiles = t // nb // tm
    ng = len(C_WINDOWS)
    return pl.pallas_call(
        functools.partial(_pool_body, tiles_per_seq=tiles, start=start),
        grid=(t // tm,),
        in_specs=[pl.BlockSpec((tm, d), lambda i: (i, 0)),
                  pl.BlockSpec((1, d), lambda i: (0, 0)),
                  mod.spec(1, tm, 1), mod.spec(0, tm, 1), mod.spec(2, tm, 1),
                  pl.BlockSpec((None, C_STATE, d), lambda i: (i // tiles, 0, 0)),
                  pl.BlockSpec((ng, C_GROUP_DIM, C_GROUP_DIM), lambda i: (0, 0, 0)),
                  pl.BlockSpec((1, d), lambda i: (0, 0))],
        out_specs=[pl.BlockSpec((tm, d), lambda i: (i, 0)),
                   pl.BlockSpec((None, C_STATE, d), lambda i: (i // tiles, 0, 0))],
        out_shape=[jax.ShapeDtypeStruct((t, d), F32), jax.ShapeDtypeStruct((nb, C_STATE, d), F32)],
        scratch_shapes=[pltpu.VMEM((16 + tm, d), F32)],
        compiler_params=_cparams(("arbitrary",)),
        name=name,
    )(x, g.reshape(1, d), mod.arr, mod.arr, mod.arr, prev, w_g, scale.reshape(1, d))


def _pool_step_body(x_ref, g_ref, sc_ref, sh_ref, gate_ref, prev_ref, wg_ref, scale_ref, o_ref, pnew_ref, *, start):
    hn = _prenorm(x_ref[...], g_ref[...], sc_ref[...], sh_ref[...])
    outs = []
    for gi, w in enumerate(C_WINDOWS):
        gsl = slice(gi * C_GROUP_DIM, (gi + 1) * C_GROUP_DIM)
        acc = hn[:, gsl]
        for k in range(1, w):
            acc = acc + prev_ref[:, C_STATE - k, gsl]
        outs.append(acc / float(min(start + 1, w)) - hn[:, gsl])
    y = _pool_matmul(jnp.concatenate(outs, axis=-1), wg_ref) * scale_ref[...]
    o_ref[...] = x_ref[...] + gate_ref[...] * y
    for k in range(C_STATE - 1):
        pnew_ref[:, k, :] = prev_ref[:, k + 1, :]
    pnew_ref[:, C_STATE - 1, :] = hn


def _pool_step_call(x, g, mod, prev, w_g, scale, start, name):
    t, d = x.shape
    bt = 32
    ng = len(C_WINDOWS)
    return pl.pallas_call(
        functools.partial(_pool_step_body, start=start),
        grid=(t // bt,),
        in_specs=[pl.BlockSpec((bt, d), lambda i: (i, 0)),
                  pl.BlockSpec((1, d), lambda i: (0, 0)),
                  mod.spec(1, bt, 1), mod.spec(0, bt, 1), mod.spec(2, bt, 1),
                  pl.BlockSpec((bt, C_STATE, d), lambda i: (i, 0, 0)),
                  pl.BlockSpec((ng, C_GROUP_DIM, C_GROUP_DIM), lambda i: (0, 0, 0)),
                  pl.BlockSpec((1, d), lambda i: (0, 0))],
        out_specs=[pl.BlockSpec((bt, d), lambda i: (i, 0)),
                   pl.BlockSpec((bt, C_STATE, d), lambda i: (i, 0, 0))],
        out_shape=[jax.ShapeDtypeStruct((t, d), F32), jax.ShapeDtypeStruct((t, C_STATE, d), F32)],
        compiler_params=_cparams(("arbitrary",)),
        name=name,
    )(x, g.reshape(1, d), mod.arr, mod.arr, mod.arr, prev, w_g, scale.reshape(1, d))


def _route_rows(s, b):
    npg = EXPERTS_PER_GROUP
    gscore = []
    for q in range(N_EXPERT_GROUPS):
        v = b[q * npg:(q + 1) * npg]
        best = None
        for i in range(npg):
            for j in range(i + 1, npg):
                best = v[i] + v[j] if best is None else jnp.maximum(best, v[i] + v[j])
        gscore.append(best)
    gsel = jnp.zeros_like(gscore[0], dtype=I32)
    gbest = gscore[0]
    for q in range(1, N_EXPERT_GROUPS):
        better = gscore[q] > gbest
        gsel = jnp.where(better, q, gsel)
        gbest = jnp.where(better, gscore[q], gbest)
    vb, vs = [], []
    for k in range(npg):
        bk, sk = b[k], s[k]
        for q in range(1, N_EXPERT_GROUPS):
            bk = jnp.where(gsel == q, b[q * npg + k], bk)
            sk = jnp.where(gsel == q, s[q * npg + k], sk)
        vb.append(bk)
        vs.append(sk)
    i1 = jnp.zeros_like(gsel)
    m1 = vb[0]
    for k in range(1, npg):
        better = vb[k] > m1
        i1 = jnp.where(better, k, i1)
        m1 = jnp.where(better, vb[k], m1)
    i2 = jnp.full_like(gsel, -1)
    m2 = jnp.zeros_like(m1)
    for k in range(npg):
        better = (i1 != k) & ((i2 < 0) | (vb[k] > m2))
        i2 = jnp.where(better, k, i2)
        m2 = jnp.where(better, vb[k], m2)
    s1 = vs[0]
    s2 = vs[0]
    for k in range(1, npg):
        s1 = jnp.where(i1 == k, vs[k], s1)
        s2 = jnp.where(i2 == k, vs[k], s2)
    w1 = s1 / (s1 + s2)
    w2 = s2 / (s1 + s2)
    lo = jnp.minimum(i1, i2)
    hi = jnp.maximum(i1, i2)
    pair = jnp.zeros_like(gsel)
    for p, (a, c) in enumerate(PAIRS):
        pair = jnp.where((lo == a) & (hi == c), p, pair)
    first_is_lo = i1 < i2
    return (gsel * len(PAIRS) + pair, jnp.where(first_is_lo, w1, w2), jnp.where(first_is_lo, w2, w1))


def _route_body(x_ref, g_ref, sc_ref, sh_ref, rw_ref, rb_ref, cnt_in_ref, *rest, aliased):
    if aliased:
        rest = rest[3:]
    rows_ref, bucket_ref, rank_ref, cnt_ref = rest
    tm = x_ref.shape[0]

    @pl.when(pl.program_id(0) == 0)
    def _():
        cnt_ref[...] = cnt_in_ref[...]

    hn = _prenorm(x_ref[...], g_ref[...], sc_ref[...], sh_ref[...])
    logits_t = _dot3(hn, rw_ref[...]).T
    scores = _sigmoid(logits_t[0:N_EXPERTS, :])
    biased = scores + rb_ref[0:N_EXPERTS, :]
    bucket, w_lo, w_hi = _route_rows([scores[e:e + 1, :] for e in range(N_EXPERTS)],
                                     [biased[e:e + 1, :] for e in range(N_EXPERTS)])
    bucket_ref[...] = bucket
    onehot = (lax.broadcasted_iota(I32, (BUCKET_ROWS, tm), 0) == bucket).astype(F32)
    before = (lax.broadcasted_iota(I32, (tm, tm), 0) < lax.broadcasted_iota(I32, (tm, tm), 1)).astype(BF16)
    earlier = _dot(onehot.astype(BF16), before) + cnt_ref[:, 0:1]
    rank_ref[...] = jnp.sum(onehot * earlier, axis=0, keepdims=True).astype(I32)
    cnt_ref[...] = cnt_ref[...] + jnp.sum(onehot, axis=1, keepdims=True)

    sub = lax.broadcasted_iota(I32, (LANE, tm), 0)
    gate_t = jnp.where(sub == 0, w_lo, jnp.where(sub == 1, w_hi, 0.0))
    rows_ref[pl.ds(GATE_ROW, tm, stride=ROW_PITCH), :] = gate_t.T
    for j in range(ROW_PLANES):
        rows_ref[pl.ds(j, tm, stride=ROW_PITCH), :] = hn[:, j * LANE:(j + 1) * LANE]


def _route_call(x, g, mod, router_w, router_b, counts, tm, total_rows, row_offset, prior, name):
    t, d = x.shape
    blk0 = row_offset // tm
    rw = jnp.pad(router_w, ((0, 0), (0, LANE - N_EXPERTS)))
    rb = jnp.pad(router_b.reshape(-1, 1), ((0, LANE - N_EXPERTS), (0, 0)))
    in_specs = [pl.BlockSpec((tm, d), lambda i: (i, 0)),
                pl.BlockSpec((1, d), lambda i: (0, 0)),
                mod.spec(4, tm, 1), mod.spec(3, tm, 1),
                pl.BlockSpec((d, LANE), lambda i: (0, 0)),
                pl.BlockSpec((LANE, 1), lambda i: (0, 0)),
                pl.BlockSpec((BUCKET_ROWS, LANE), lambda i: (0, 0))]
    args = [x, g.reshape(1, d), mod.arr, mod.arr, rw, rb, counts]
    aliases = {}
    if prior is not None:
        in_specs += [pl.BlockSpec(memory_space=pl.ANY)] * len(prior)
        aliases = {len(args) + k: k for k in range(len(prior))}
        args += list(prior)
    return pl.pallas_call(
        functools.partial(_route_body, aliased=prior is not None),
        grid=(t // tm,),
        in_specs=in_specs,
        out_specs=[pl.BlockSpec((tm * ROW_PITCH, LANE), lambda i: (blk0 + i, 0)),
                   pl.BlockSpec((1, tm), lambda i: (0, blk0 + i)),
                   pl.BlockSpec((1, tm), lambda i: (0, blk0 + i)),
                   pl.BlockSpec((BUCKET_ROWS, LANE), lambda i: (0, 0))],
        out_shape=[jax.ShapeDtypeStruct((total_rows * ROW_PITCH, LANE), F32),
                   jax.ShapeDtypeStruct((1, total_rows), I32),
                   jax.ShapeDtypeStruct((1, total_rows), I32),
                   jax.ShapeDtypeStruct((BUCKET_ROWS, LANE), F32)],
        input_output_aliases=aliases,
        compiler_params=_cparams(("arbitrary",)),
        name=name,
    )(*args)


def _invert_body(dest_ref, init_ref, gather_ref):
    pltpu.sync_copy(init_ref, gather_ref)

    def put(t, carry):
        gather_ref[dest_ref[t]] = t
        return carry

    lax.fori_loop(0, dest_ref.shape[0], put, 0, unroll=8)


def _invert_call(dest, n_slots, name):
    smem = pl.BlockSpec(memory_space=pltpu.SMEM)
    return pl.pallas_call(
        _invert_body,
        in_specs=[smem, pl.BlockSpec(memory_space=pl.ANY)],
        out_specs=smem,
        out_shape=jax.ShapeDtypeStruct((n_slots,), I32),
        name=name,
    )(dest, jnp.zeros((n_slots,), I32))


def _unrolled(lo, hi, body, carry):
    for b in range(lo, hi):
        carry = body(b, carry)
    return carry


def _experts_body(grp_ref, lo_ref, hi_ref, fresh_ref, valid_ref, out_ref, gather_ref, rows_ref, wg_in_ref,
                  wu_in_ref, wd_in_ref, y_ref, wg_ref, wu_ref, wd_ref, xb_ref, gsem):
    i = pl.program_id(0)
    n = pl.num_programs(0)
    tile = EXPERT_TILE
    copied = GATE_ROW + 1
    group = 8
    del grp_ref, out_ref

    def start_gather(step, slot, loop):
        def body(b, carry):
            for k in range(group):
                r = b * group + k
                tok = gather_ref[step * tile + r]
                pltpu.make_async_copy(rows_ref.at[pl.ds(tok * ROW_PITCH, copied)],
                                      xb_ref.at[slot, pl.ds(r * ROW_PITCH, copied)],
                                      gsem.at[slot]).start(priority=k % 2)
            return carry

        loop(0, tile // group, body, 0)

    def wait_gather(slot):
        rows = tile * copied
        pltpu.make_async_copy(rows_ref.at[pl.ds(0, rows)], xb_ref.at[slot, pl.ds(0, rows)], gsem.at[slot]).wait()

    @pl.when(fresh_ref[i] == 1)
    def _():
        for e in range(EXPERTS_PER_GROUP):
            wg_ref[e] = wg_in_ref[e].astype(BF16)
            wu_ref[e] = wu_in_ref[e].astype(BF16)
            wd_ref[e] = wd_in_ref[e].astype(BF16)

    nxt = jnp.minimum(i + 1, n - 1)
    more = jnp.logical_and(i + 1 < n, valid_ref[nxt] == 1)

    @pl.when(valid_ref[i] == 1)
    def _():
        slot = i % GATHER_SLOTS

        @pl.when(i == 0)
        def _():
            for ahead in range(GATHER_SLOTS - 1):
                start_gather(ahead, ahead, lax.fori_loop)

        wait_gather(slot)
        ahead = i + GATHER_SLOTS - 1
        start_gather(jnp.minimum(ahead, n - 1), ahead % GATHER_SLOTS, _unrolled)
        x = jnp.concatenate([xb_ref[slot, pl.ds(j, tile, stride=ROW_PITCH), :].astype(BF16)
                             for j in range(ROW_PLANES)], axis=-1)

        def expert(e):
            act = (_silu(_dot(x, wg_ref[e])) * _dot(x, wu_ref[e])).astype(BF16)
            return _dot(act, wd_ref[e])

        w = xb_ref[slot, pl.ds(GATE_ROW, tile, stride=ROW_PITCH), :]
        y = w[:, 0:1] * expert(lo_ref[i]) + w[:, 1:2] * expert(hi_ref[i])
        for j in range(ROW_PLANES):
            y_ref[pl.ds(j, tile, stride=ROW_PITCH), :] = y[:, j * LANE:(j + 1) * LANE]

        @pl.when(jnp.logical_not(more))
        def _():
            for k in range(1, GATHER_SLOTS):
                wait_gather((i + k) % GATHER_SLOTS)


def _experts_call(grp, lo, hi, fresh, valid, out_tile, gather_idx, rows, w_gate, w_up, w_down, layer, name):
    n_tiles = grp.shape[0]
    tile = EXPERT_TILE
    d, f = D_MODEL, D_EXPERT
    npg = EXPERTS_PER_GROUP
    group_block = lambda i, grp, *_: (layer, grp[i], 0, 0)
    once = pl.Buffered(1)
    return pl.pallas_call(
        _experts_body,
        grid_spec=pltpu.PrefetchScalarGridSpec(
            num_scalar_prefetch=7,
            grid=(n_tiles,),
            in_specs=[pl.BlockSpec(memory_space=pl.ANY),
                      pl.BlockSpec((None, npg, d, f), group_block, pipeline_mode=once),
                      pl.BlockSpec((None, npg, d, f), group_block, pipeline_mode=once),
                      pl.BlockSpec((None, npg, f, d), group_block, pipeline_mode=once)],
            out_specs=pl.BlockSpec((tile * ROW_PITCH, LANE), lambda i, g, l, h, fr, va, out, *_: (out[i], 0)),
            scratch_shapes=[pltpu.VMEM((npg, d, f), BF16), pltpu.VMEM((npg, d, f), BF16),
                            pltpu.VMEM((npg, f, d), BF16),
                            pltpu.VMEM((GATHER_SLOTS, tile * ROW_PITCH, LANE), F32),
                            pltpu.SemaphoreType.DMA((GATHER_SLOTS,))]),
        out_shape=jax.ShapeDtypeStruct((n_tiles * tile * ROW_PITCH, LANE), F32),
        compiler_params=_cparams(("arbitrary",)),
        name=name,
    )(grp, lo, hi, fresh, valid, out_tile, gather_idx, rows, w_gate, w_up, w_down)


def _moe_res_body(dest_ref, y_ref, x_ref, gate_ref, fg_ref, o_ref, *rest, row_offset, final):
    if final:
        on_ref, buf_ref, sem = rest
    else:
        buf_ref, sem = rest
    i = pl.program_id(0)
    tm = x_ref.shape[0]
    group = 8

    def start_gather(step, slot, loop):
        base = row_offset + step * tm

        def body(b, carry):
            for k in range(group):
                r = b * group + k
                pltpu.make_async_copy(y_ref.at[pl.ds(dest_ref[base + r] * ROW_PITCH, ROW_PLANES)],
                                      buf_ref.at[slot, pl.ds(r * ROW_PITCH, ROW_PLANES)],
                                      sem.at[slot]).start(priority=k % 2)
            return carry

        loop(0, tm // group, body, 0)

    def wait_gather(slot):
        rows = tm * ROW_PLANES
        pltpu.make_async_copy(y_ref.at[pl.ds(0, rows)], buf_ref.at[slot, pl.ds(0, rows)], sem.at[slot]).wait()

    @pl.when(i == 0)
    def _():
        start_gather(0, 0, lax.fori_loop)

    slot = i % 2
    wait_gather(slot)

    @pl.when(i + 1 < pl.num_programs(0))
    def _():
        start_gather(i + 1, 1 - slot, lax.fori_loop)
    y = jnp.concatenate([buf_ref[slot, pl.ds(j, tm, stride=ROW_PITCH), :] for j in range(ROW_PLANES)], axis=-1)
    xn = x_ref[...] + gate_ref[...] * y
    o_ref[...] = xn
    if final:
        ms = jnp.mean(xn * xn, axis=-1, keepdims=True)
        on_ref[...] = xn * lax.rsqrt(ms + EPS) * fg_ref[...]


def _moe_res_call(dest, y_sorted, x, mod, final_g, tm, row_offset, final, name):
    t, d = x.shape
    n_out = 2 if final else 1
    gate_spec = mod.spec(5, tm, 1)
    gate_map = gate_spec.index_map
    return pl.pallas_call(
        functools.partial(_moe_res_body, row_offset=row_offset, final=final),
        grid_spec=pltpu.PrefetchScalarGridSpec(
            num_scalar_prefetch=1,
            grid=(t // tm,),
            in_specs=[pl.BlockSpec(memory_space=pl.ANY),
                      pl.BlockSpec((tm, d), lambda i, dest: (i, 0)),
                      pl.BlockSpec(gate_spec.block_shape, lambda i, dest: gate_map(i)),
                      pl.BlockSpec((1, d), lambda i, dest: (0, 0))],
            out_specs=[pl.BlockSpec((tm, d), lambda i, dest: (i, 0))] * n_out,
            scratch_shapes=[pltpu.VMEM((2, tm * ROW_PITCH, LANE), F32), pltpu.SemaphoreType.DMA((2,))]),
        out_shape=[jax.ShapeDtypeStruct((t, d), F32)] * n_out,
        compiler_params=_cparams(("arbitrary",)),
        name=name,
    )(dest, y_sorted, x, mod.arr, final_g.reshape(1, d))


def _tile_tables(counts, n_tiles):
    tile = EXPERT_TILE
    tiles_per_bucket = (counts + tile - 1) // tile
    ends = jnp.cumsum(tiles_per_bucket)
    starts = ends - tiles_per_bucket
    used = ends[-1]
    ti = jnp.arange(n_tiles, dtype=I32)
    valid = (ti < used).astype(I32)
    tile_bucket = jnp.sum((jnp.minimum(ti, used - 1)[:, None] >= ends[None, :]).astype(I32), axis=1)
    pair_lo = jnp.asarray([p[0] for p in PAIRS], I32)
    pair_hi = jnp.asarray([p[1] for p in PAIRS], I32)
    grp = tile_bucket // len(PAIRS)
    lo = pair_lo[tile_bucket % len(PAIRS)]
    hi = pair_hi[tile_bucket % len(PAIRS)]
    fresh = jnp.concatenate([jnp.ones((1,), I32), (grp[1:] != grp[:-1]).astype(I32)])
    return grp, lo, hi, fresh, valid, jnp.minimum(ti, used - 1).astype(I32), (starts * tile).astype(I32)


def _moe_layer(xp, xs, norm_g2, mod_p, mod_s, router_w, router_b, w_gate, w_up, w_down, layer, final_g, final):
    tp, ts = xp.shape[0], xs.shape[0]
    total = tp + ts
    n_tiles = -(-total // EXPERT_TILE) + N_BUCKETS
    zero_counts = jnp.zeros((BUCKET_ROWS, LANE), F32)
    rows, bucket, rank, counts = _route_call(xp, norm_g2, mod_p, router_w, router_b, zero_counts, TM_ROUTE, total, 0,
                                             None, "route_prompt")
    rows, bucket, rank, counts = _route_call(xs, norm_g2, mod_s, router_w, router_b, counts, ts, total, tp,
                                             (rows, bucket, rank), "route_sample")
    grp, lo, hi, fresh, valid, out_tile, bucket_start = _tile_tables(counts[:N_BUCKETS, 0].astype(I32), n_tiles)
    dest = bucket_start[bucket[0]] + rank[0]
    gather_idx = _invert_call(dest, n_tiles * EXPERT_TILE, "invert")
    y_sorted = _experts_call(grp, lo, hi, fresh, valid, out_tile, gather_idx, rows, w_gate, w_up, w_down, layer,
                             "experts")
    outp = _moe_res_call(dest, y_sorted, xp, mod_p, final_g, TM_MOE_RES, 0, final, "moe_res_prompt")
    outs = _moe_res_call(dest, y_sorted, xs, mod_s, final_g, ts, tp, final, "moe_res_sample")
    return outp, outs


def _mamba_layer(x, g, mod, conv_prev, ssm, w_zx, w_dt, conv_w, conv_b, dt_bias, a_log, d_skip, norm_g, w_out,
                 tm, tag):
    pad_h = lambda v: jnp.pad(v.reshape(1, -1), ((0, 0), (0, LANE - A_N_HEADS)))
    tm_in = min(2 * tm, x.shape[0])
    proj = _norm_mm_call(x, g, mod, 1, 0, w_zx, tm_in, A_IN_TN, BF16, "a_in_" + tag)
    dt_raw = _norm_mm3_call(x, g, mod, 1, 0, w_dt, tm, "a_dt_" + tag)
    d_x = jnp.repeat(d_skip, A_HEAD_DIM).reshape(1, A_D_INNER)
    weights = (conv_w, conv_b.reshape(1, -1), pad_h(dt_bias), pad_h(a_log), d_x, norm_g.reshape(1, -1))
    if ssm[0] == "step":
        yn, conv_new, ssm_new = _ssd_step_call(proj, dt_raw, conv_prev, ssm[1], ssm[2], ssm[3], *weights,
                                               "ssd_step_" + tag)
    else:
        yn, conv_new, ssm_new = _ssd_call(proj, dt_raw, conv_prev, ssm[2], *weights, ssm[1], "ssd_" + tag)
    x = _out_res_call(yn, w_out, x, mod, 2, tm, "a_out_" + tag)
    return x, conv_new, ssm_new


def kernel(x_prompt, x_sample, c_prompt, c_sample, state_a_conv, state_a_ssm, state_c_pool, w_mod, b_mod, norm_g, final_g, a_w_in, a_conv_w, a_conv_b, a_dt_bias, a_log, a_d, a_norm_g, a_w_out, b_w_in, b_b_in, b_ln_g, b_ln_b, b_w_s, b_b_s, b_w_out, c_w_g, c_scale, router_w, router_b, e_w_gate, e_w_up, e_w_down):
    bp, seq, d = x_prompt.shape
    bs = x_sample.shape[0]
    n_a, n_c = state_a_conv.shape[0], state_c_pool.shape[0]
    mod_all = _mod_call(jnp.concatenate([c_prompt, c_sample], axis=0), w_mod, b_mod)
    mod_p_arr = mod_all[:, :bp].reshape(DEPTH, bp, 6, 1, d)
    mod_s_arr = mod_all[:, bp:]
    xp = x_prompt.reshape(bp * seq, d)
    xs = x_sample.reshape(bs, d)
    conv_p, ssm_p, pool_p, conv_s, pool_s, v_s = [], [], [], [], [], []
    hp = A_N_HEADS * A_HEAD_DIM
    ssm_s_in = state_a_ssm.reshape(n_a, bs, hp, A_D_STATE)
    ssm_s_out = None
    yp = ys = None
    for i in range(DEPTH):
        kind, s = LAYER_KIND[i], LAYER_SLOT[i]
        mod_p = Mod(mod_p_arr, i, False, seq)
        mod_s = Mod(mod_s_arr, i, True)
        g1 = norm_g[i, 0]
        if kind == 0:
            w_zx = a_w_in[s, :, :A_ZX].astype(BF16)
            w_dt = jnp.pad(a_w_in[s, :, A_ZX:], ((0, 0), (0, LANE - A_N_HEADS)))
            weights = (w_zx, w_dt, a_conv_w[s], a_conv_b[s], a_dt_bias[s], a_log[s], a_d[s], a_norm_g[s], a_w_out[s])
            conv0 = jnp.zeros((bp, A_CONV - 1, A_CONV_CH), F32)
            ssm0 = jnp.zeros((bp, hp, A_D_STATE), F32)
            xp, cv, ss = _mamba_layer(xp, g1, mod_p, conv0, ("prompt", bp, ssm0), *weights, TM_MAMBA, "p%d" % i)
            conv_p.append(cv)
            ssm_p.append(ss.reshape(bp, A_N_HEADS, A_HEAD_DIM, A_D_STATE))
            xs, cv, ssm_s_out = _mamba_layer(xs, g1, mod_s, state_a_conv[s], ("step", ssm_s_in, s, ssm_s_out),
                                             *weights, bs, "s%d" % i)
            conv_s.append(cv)
        elif kind == 1:
            w_uv = b_w_in[s].astype(BF16)
            uv = _gmlp_in_call(xp, g1, mod_p, w_uv, b_b_in[s], b_ln_g[s], b_ln_b[s], TM_GMLP_IN, BF16,
                               "b_in_p%d" % i)
            xp = _gmlp_out_call(uv, b_w_s[s], b_b_s[s], b_w_out[s], xp, mod_p, TM_GMLP_OUT, "b_out_p%d" % i)
            uv = _gmlp_in_call(xs, g1, mod_s, w_uv, b_b_in[s], b_ln_g[s], b_ln_b[s], bs, F32, "b_in_s%d" % i)
            xs = _gmlp_out_step_call(uv, b_w_s[s], b_b_s[s], b_w_out[s], xs, mod_s, "b_out_s%d" % i)
            v_s.append(uv[:, B_D:].reshape(bs, 1, B_D))
        else:
            pool0 = jnp.zeros((bp, C_STATE, d), F32)
            xp, pr = _pool_call(xp, g1, mod_p, pool0, c_w_g[s], c_scale[s], bp, TM_POOL, 0, "pool_p%d" % i)
            pool_p.append(pr)
            xs, pr = _pool_step_call(xs, g1, mod_s, state_c_pool[s], c_w_g[s], c_scale[s], PAST_LEN, "pool_s%d" % i)
            pool_s.append(pr)
        final = i == DEPTH - 1
        outp, outs = _moe_layer(xp, xs, norm_g[i, 1], mod_p, mod_s, router_w, router_b, e_w_gate, e_w_up, e_w_down, i,
                                final_g, final)
        xp, xs = outp[0], outs[0]
        if final:
            yp, ys = outp[1], outs[1]
    return (yp.reshape(bp, seq, d), ys.reshape(bs, 1, d), jnp.stack(conv_p), jnp.stack(ssm_p), jnp.stack(pool_p),
            jnp.stack(conv_s), ssm_s_out.reshape(state_a_ssm.shape), jnp.stack(pool_s), jnp.stack(v_s))
```

```python
import functools
import math

import numpy as np
import jax
import jax.numpy as jnp
from jax import lax
from jax.experimental import pallas as pl
from jax.experimental.pallas import tpu as pltpu

F32 = jnp.float32
BF16 = jnp.bfloat16
I32 = jnp.int32
EPS = 1e-6

LANE = 128
D_MODEL = 1024
DEPTH = 4
PAST_LEN = 16384
LAYER_KIND = (0, 1, 2, 0)
LAYER_SLOT = (0, 0, 0, 1)
A_D_INNER = 2 * D_MODEL
A_HEAD_DIM = 64
A_N_HEADS = A_D_INNER // A_HEAD_DIM
A_N_GROUPS = 8
A_HPG = A_N_HEADS // A_N_GROUPS
A_D_STATE = 128
A_GN = A_N_GROUPS * A_D_STATE
A_CONV = 4
A_CONV_CH = A_D_INNER + 2 * A_GN
A_ZX = A_D_INNER + A_CONV_CH
A_CHUNK = 128
A_GW = A_HPG * A_HEAD_DIM
B_D = 2 * D_MODEL
B_N_GROUPS = 8
B_GROUP_DIM = B_D // B_N_GROUPS
B_CHUNK = 128
C_WINDOWS = (2, 4, 8, 16)
C_GROUP_DIM = D_MODEL // len(C_WINDOWS)
C_STATE = max(C_WINDOWS) - 1
N_EXPERTS = 16
N_EXPERT_GROUPS = 4
EXPERTS_PER_GROUP = 4
D_EXPERT = D_MODEL // 2
PAIRS = ((0, 1), (0, 2), (0, 3), (1, 2), (1, 3), (2, 3))
N_BUCKETS = N_EXPERT_GROUPS * len(PAIRS)
BUCKET_ROWS = 32
ROW_PLANES = D_MODEL // LANE
ROW_PITCH = ROW_PLANES + 4
GATE_ROW = ROW_PLANES
EXPERT_TILE = 256
GATHER_SLOTS = 6
VMEM_LIMIT = 56 * 1024 * 1024
TM_MAMBA = 1024
A_IN_TN = 1024
TM_GMLP_IN = 1024
TM_GMLP_OUT = 512
TM_POOL = 1024
TM_ROUTE = 512
TM_MOE_RES = 1024


def _cparams(sem, vmem=VMEM_LIMIT):
    return pltpu.CompilerParams(dimension_semantics=sem, vmem_limit_bytes=vmem)


def _sigmoid(x):
    return 1.0 / (1.0 + jnp.exp(-x))


def _silu(x):
    return x * _sigmoid(x)


def _gelu_tanh(x):
    c = 2.0 * math.sqrt(2.0 / math.pi)
    return x / (1.0 + jnp.exp(x * (-c - (c * 0.044715) * (x * x))))


def _softplus(x):
    return jnp.maximum(x, 0.0) + jnp.log1p(jnp.exp(-jnp.abs(x)))


def _split2(a):
    hi = a.astype(BF16)
    lo = (a - hi.astype(F32)).astype(BF16)
    return hi, lo


def _split3(a):
    hi = a.astype(BF16)
    r = a - hi.astype(F32)
    mid = r.astype(BF16)
    lo = (r - mid.astype(F32)).astype(BF16)
    return hi, mid, lo


def _dot(a, b):
    return jnp.dot(a, b, preferred_element_type=F32)


def _dot_nt(a, b):
    return lax.dot_general(a, b, (((1,), (1,)), ((), ())), preferred_element_type=F32)


def _dot_tn(a, b):
    return lax.dot_general(a, b, (((0,), (0,)), ((), ())), preferred_element_type=F32)


def _dot3(a, b):
    a_hi, a_lo = _split2(a)
    b_hi, b_lo = _split2(b)
    return _dot(a_hi, b_hi) + (_dot(a_lo, b_hi) + _dot(a_hi, b_lo))


def _prenorm(x, g, sc, sh):
    ms = jnp.mean(x * x, axis=-1, keepdims=True)
    return (x * lax.rsqrt(ms + EPS) * g) * (1.0 + sc) + sh


def _mod_body(c_ref, w_ref, b_ref, o_ref):
    o_ref[...] = _dot3(_silu(c_ref[...]), w_ref[...]) + b_ref[...]


def _mod_call(c_all, w_mod, b_mod):
    nb, d = c_all.shape
    depth, _, n = w_mod.shape
    tn = 1536
    return pl.pallas_call(
        _mod_body,
        grid=(depth, n // tn),
        in_specs=[pl.BlockSpec((nb, d), lambda i, j: (0, 0)),
                  pl.BlockSpec((None, d, tn), lambda i, j: (i, 0, j)),
                  pl.BlockSpec((None, 1, tn), lambda i, j: (i, 0, j))],
        out_specs=pl.BlockSpec((None, nb, tn), lambda i, j: (i, 0, j)),
        out_shape=jax.ShapeDtypeStruct((depth, nb, n), F32),
        compiler_params=_cparams(("arbitrary", "arbitrary")),
        name="mod",
    )(c_all, w_mod, b_mod.reshape(depth, 1, n))


class Mod:
    def __init__(self, arr, layer, per_row, rows_per_seq=None):
        self.arr, self.layer, self.per_row, self.rows_per_seq = arr, layer, per_row, rows_per_seq

    def spec(self, which, tm, ngrid):
        layer = self.layer
        if self.per_row:
            if ngrid == 1:
                return pl.BlockSpec((None, tm, D_MODEL), lambda i: (layer, i, which))
            return pl.BlockSpec((None, tm, D_MODEL), lambda i, j: (layer, i, which))
        tiles = self.rows_per_seq // tm
        if ngrid == 1:
            return pl.BlockSpec((None, None, None, 1, D_MODEL), lambda i: (layer, i // tiles, which, 0, 0))
        return pl.BlockSpec((None, None, None, 1, D_MODEL), lambda i, j: (layer, i // tiles, which, 0, 0))


def _norm_mm_body(x_ref, g_ref, sc_ref, sh_ref, w_ref, o_ref, hn_ref):
    @pl.when(pl.program_id(1) == 0)
    def _():
        hn_ref[...] = _prenorm(x_ref[...], g_ref[...], sc_ref[...], sh_ref[...]).astype(BF16)

    o_ref[...] = _dot(hn_ref[...], w_ref[...]).astype(o_ref.dtype)


def _norm_mm_call(x, g, mod, which_sc, which_sh, w, tm, tn, out_dtype, name):
    t, d = x.shape
    n_cols = w.shape[1]
    return pl.pallas_call(
        _norm_mm_body,
        grid=(t // tm, n_cols // tn),
        in_specs=[pl.BlockSpec((tm, d), lambda i, j: (i, 0)),
                  pl.BlockSpec((1, d), lambda i, j: (0, 0)),
                  mod.spec(which_sc, tm, 2), mod.spec(which_sh, tm, 2),
                  pl.BlockSpec((d, tn), lambda i, j: (0, j))],
        out_specs=pl.BlockSpec((tm, tn), lambda i, j: (i, j)),
        out_shape=jax.ShapeDtypeStruct((t, n_cols), out_dtype),
        scratch_shapes=[pltpu.VMEM((tm, d), BF16)],
        compiler_params=_cparams(("arbitrary", "arbitrary")),
        name=name,
    )(x, g.reshape(1, d), mod.arr, mod.arr, w)


def _norm_mm3_body(x_ref, g_ref, sc_ref, sh_ref, w_ref, o_ref):
    hn = _prenorm(x_ref[...], g_ref[...], sc_ref[...], sh_ref[...])
    o_ref[...] = _dot3(hn, w_ref[...])


def _norm_mm3_call(x, g, mod, which_sc, which_sh, w, tm, name):
    t, d = x.shape
    n = w.shape[1]
    return pl.pallas_call(
        _norm_mm3_body,
        grid=(t // tm,),
        in_specs=[pl.BlockSpec((tm, d), lambda i: (i, 0)),
                  pl.BlockSpec((1, d), lambda i: (0, 0)),
                  mod.spec(which_sc, tm, 1), mod.spec(which_sh, tm, 1),
                  pl.BlockSpec((d, n), lambda i: (0, 0))],
        out_specs=pl.BlockSpec((tm, n), lambda i: (i, 0)),
        out_shape=jax.ShapeDtypeStruct((t, n), F32),
        compiler_params=_cparams(("arbitrary",)),
        name=name,
    )(x, g.reshape(1, d), mod.arr, mod.arr, w)


def _out_res_body(y_ref, w_ref, x_ref, gate_ref, o_ref, wbf_ref):
    @pl.when(pl.program_id(0) == 0)
    def _():
        wbf_ref[...] = w_ref[...].astype(BF16)

    o_ref[...] = x_ref[...] + gate_ref[...] * _dot(y_ref[...], wbf_ref[...])


def _out_res_call(y, w, x, mod, which_gate, tm, name):
    t, k = y.shape
    d = x.shape[1]
    return pl.pallas_call(
        _out_res_body,
        grid=(t // tm,),
        in_specs=[pl.BlockSpec((tm, k), lambda i: (i, 0)),
                  pl.BlockSpec((k, d), lambda i: (0, 0)),
                  pl.BlockSpec((tm, d), lambda i: (i, 0)),
                  mod.spec(which_gate, tm, 1)],
        out_specs=pl.BlockSpec((tm, d), lambda i: (i, 0)),
        out_shape=jax.ShapeDtypeStruct((t, d), F32),
        scratch_shapes=[pltpu.VMEM((k, d), BF16)],
        compiler_params=_cparams(("arbitrary",)),
        name=name,
    )(y, w, x, mod.arr)


def _head_expand():
    h = np.arange(LANE)[:, None]
    c = np.arange(A_D_INNER)[None, :]
    return jnp.asarray((c // A_HEAD_DIM == h).astype(np.float32), dtype=BF16)


def _gate_norm(y, z, ng):
    gated = y * _silu(z)
    ms = jnp.mean(gated * gated, axis=-1, keepdims=True)
    return gated * lax.rsqrt(ms + EPS) * ng


def _ssd_body(z_ref, xs_ref, bc_ref, dt_ref, cprev_ref, sprev_ref, cw_ref, cb_ref, dtb_ref, alog_ref,
              dx_ref, ng_ref, exp_ref, yn_ref, cnew_ref, snew_ref, tail_ref, act_ref):
    c = pl.program_id(1)
    q = A_CHUNK
    nt = A_CONV - 1

    @pl.when(c == 0)
    def _():
        tail_ref[0:nt, :] = cprev_ref[...]
        snew_ref[...] = sprev_ref[...]

    row = lax.broadcasted_iota(I32, (q, q), 0)
    col = lax.broadcasted_iota(I32, (q, q), 1)
    shifts = [jnp.where(row - col == nt - k, 1.0, 0.0).astype(BF16) for k in range(nt)]
    sub = lax.broadcasted_iota(I32, (8, 1), 0)
    cw = 512
    for j in range(A_CONV_CH // cw):
        sl = slice(j * cw, (j + 1) * cw)
        src = xs_ref if (j + 1) * cw <= A_D_INNER else bc_ref
        off = j * cw if src is xs_ref else j * cw - A_D_INNER
        x_bf = src[:, off:off + cw]
        conv = cb_ref[:, sl] + x_bf.astype(F32) * cw_ref[nt:nt + 1, sl]
        for k in range(nt):
            conv = conv + _dot(shifts[k], x_bf) * cw_ref[k:k + 1, sl]
        act_ref[:, sl] = _silu(conv)
        corr = jnp.zeros((8, cw), F32)
        for l in range(nt):
            c_l = sum(tail_ref[l + k:l + k + 1, sl] * cw_ref[k:k + 1, sl] for k in range(nt - l))
            corr = jnp.where(sub == l, c_l, corr)
        act_ref[0:8, sl] = _silu(conv[0:8, :] + corr)
        tail_ref[0:nt, sl] = x_bf[q - 8:q, :].astype(F32)[8 - nt:8, :]

    @pl.when(c == pl.num_programs(1) - 1)
    def _():
        cnew_ref[...] = tail_ref[0:nt, :]

    dt = _softplus(dt_ref[...] + dtb_ref[...])
    a = dt * (-jnp.exp(alog_ref[...]))
    row = lax.broadcasted_iota(I32, (q, q), 0)
    col = lax.broadcasted_iota(I32, (q, q), 1)
    causal = row >= col
    tril = jnp.where(causal, 1.0, 0.0).astype(BF16)
    a_hi, a_mid, a_lo = _split3(a)
    cs = _dot(tril, a_hi) + (_dot(tril, a_mid) + _dot(tril, a_lo))
    cs_t = cs.T
    dt_t = dt.T
    cs_last = cs[q - 1:q, :]
    ecs = jnp.exp(cs)
    wend = jnp.exp(cs_last - cs) * dt
    st_hi, st_lo = _split2(jnp.concatenate([ecs, wend], axis=0))
    st_x = _dot(st_hi, exp_ref[...]) + _dot(st_lo, exp_ref[...])
    lane_head = lax.broadcasted_iota(I32, (q, A_GW), 1) // A_HEAD_DIM

    for g in range(A_N_GROUPS):
        gsl = slice(g * A_GW, (g + 1) * A_GW)
        b_g = act_ref[:, A_D_INNER + g * A_D_STATE:A_D_INNER + (g + 1) * A_D_STATE].astype(BF16)
        c_g = act_ref[:, A_D_INNER + A_GN + g * A_D_STATE:A_D_INNER + A_GN + (g + 1) * A_D_STATE].astype(BF16)
        x_g = act_ref[:, gsl]
        x_bf = x_g.astype(BF16)
        h_g = snew_ref[gsl, :]
        cb = _dot_nt(c_g, b_g)
        y = jnp.zeros((q, A_GW), F32)
        for r in range(A_HPG):
            h = g * A_HPG + r
            seg = cs[:, h:h + 1] - cs_t[h:h + 1, :]
            decay = jnp.exp(jnp.where(causal, seg, -1e30))
            wm = (cb * decay * dt_t[h:h + 1, :]).astype(BF16)
            y = y + _dot(wm, jnp.where(lane_head == r, x_bf, jnp.zeros_like(x_bf)))
        y = y + st_x[0:q, gsl] * _dot_nt(c_g, h_g.astype(BF16)) + dx_ref[:, gsl] * x_g
        yn_ref[:, gsl] = _gate_norm(y, z_ref[:, gsl].astype(F32), ng_ref[:, gsl]).astype(BF16)
        s_new = _dot_tn((x_g * st_x[q:2 * q, gsl]).astype(BF16), b_g)
        for r in range(A_HPG):
            h = g * A_HPG + r
            rsl = slice(g * A_GW + r * A_HEAD_DIM, g * A_GW + (r + 1) * A_HEAD_DIM)
            keep = jnp.exp(cs[q - 1:q, h:h + 1])
            snew_ref[rsl, :] = snew_ref[rsl, :] * keep + s_new[r * A_HEAD_DIM:(r + 1) * A_HEAD_DIM, :]


def _ssd_call(proj, dt_raw, conv_prev, ssm_prev, conv_w, conv_b, dt_bias, a_log, d_x, norm_g, nb, name):
    t = proj.shape[0]
    q = A_CHUNK
    nc = t // nb // q
    hp = A_N_HEADS * A_HEAD_DIM
    row = lambda b, c: (b * nc + c, 0)
    full = lambda shape: pl.BlockSpec(shape, lambda b, c: (0,) * len(shape))
    return pl.pallas_call(
        _ssd_body,
        grid=(nb, nc),
        in_specs=[pl.BlockSpec((q, A_D_INNER), lambda b, c: (b * nc + c, 0)),
                  pl.BlockSpec((q, A_D_INNER), lambda b, c: (b * nc + c, 1)),
                  pl.BlockSpec((q, 2 * A_GN), lambda b, c: (b * nc + c, 2)),
                  pl.BlockSpec((q, LANE), row),
                  pl.BlockSpec((None, A_CONV - 1, A_CONV_CH), lambda b, c: (b, 0, 0)),
                  pl.BlockSpec((None, hp, A_D_STATE), lambda b, c: (b, 0, 0)),
                  full((A_CONV, A_CONV_CH)), full((1, A_CONV_CH)), full((1, LANE)), full((1, LANE)),
                  full((1, A_D_INNER)), full((1, A_D_INNER)), full((LANE, A_D_INNER))],
        out_specs=[pl.BlockSpec((q, A_D_INNER), row),
                   pl.BlockSpec((None, A_CONV - 1, A_CONV_CH), lambda b, c: (b, 0, 0)),
                   pl.BlockSpec((None, hp, A_D_STATE), lambda b, c: (b, 0, 0))],
        out_shape=[jax.ShapeDtypeStruct((t, A_D_INNER), BF16),
                   jax.ShapeDtypeStruct((nb, A_CONV - 1, A_CONV_CH), F32),
                   jax.ShapeDtypeStruct((nb, hp, A_D_STATE), F32)],
        scratch_shapes=[pltpu.VMEM((8, A_CONV_CH), F32), pltpu.VMEM((q, A_CONV_CH), F32)],
        compiler_params=_cparams(("arbitrary", "arbitrary")),
        name=name,
    )(proj, proj, proj, dt_raw, conv_prev, ssm_prev, conv_w, conv_b, dt_bias, a_log, d_x, norm_g, _head_expand())


def _ssd_step_body(z_ref, xs_ref, bc_ref, dt_ref, cprev_ref, sprev_ref, cw_ref, cb_ref, dtb_ref, alog_ref,
                   dx_ref, ng_ref, exp_ref, yn_ref, cnew_ref, snew_ref, da_ref, y_ref):
    bt = z_ref.shape[0]
    cur = jnp.concatenate([xs_ref[...], bc_ref[...]], axis=1).astype(F32)
    conv = cb_ref[...] + cur * cw_ref[A_CONV - 1:A_CONV, :]
    for k in range(A_CONV - 1):
        conv = conv + cprev_ref[:, k, :] * cw_ref[k:k + 1, :]
    for k in range(A_CONV - 2):
        cnew_ref[:, k, :] = cprev_ref[:, k + 1, :]
    cnew_ref[:, A_CONV - 2, :] = cur
    act = _silu(conv)
    xs = act[:, 0:A_D_INNER]
    bm_bf = act[:, A_D_INNER:A_D_INNER + A_GN].astype(BF16)
    cm_bf = act[:, A_D_INNER + A_GN:A_CONV_CH].astype(BF16)
    dt = _softplus(dt_ref[...] + dtb_ref[...])
    da_ref[...] = jnp.exp(dt * (-jnp.exp(alog_ref[...])))
    dt_hi, dt_lo = _split2(dt)
    dt_x = _dot(dt_hi, exp_ref[...]) + _dot(dt_lo, exp_ref[...])
    xdt = xs * dt_x
    rows = lax.broadcasted_iota(I32, (bt, 1), 0)
    y_ref[...] = jnp.zeros_like(y_ref)

    def per_seq(j, carry):
        mine = rows == j
        xdt_j = jnp.where(mine, xdt, 0.0)
        da_j = da_ref[pl.ds(j, 1), :]
        for g in range(A_N_GROUPS):
            gsl = slice(g * A_GW, (g + 1) * A_GW)
            nsl = slice(g * A_D_STATE, (g + 1) * A_D_STATE)
            outer = _dot_tn(xdt_j[:, gsl].astype(BF16), bm_bf[:, nsl])
            for r in range(A_HPG):
                h = g * A_HPG + r
                rsl = slice(r * A_HEAD_DIM, (r + 1) * A_HEAD_DIM)
                hsl = slice(g * A_GW + r * A_HEAD_DIM, g * A_GW + (r + 1) * A_HEAD_DIM)
                snew_ref[j, hsl, :] = sprev_ref[j, hsl, :] * da_j[:, h:h + 1] + outer[rsl, :]
            yg = _dot_nt(cm_bf[:, nsl], snew_ref[j, gsl, :].astype(BF16))
            y_ref[:, gsl] = y_ref[:, gsl] + jnp.where(mine, yg, 0.0)
        return carry

    lax.fori_loop(0, bt, per_seq, 0)
    y = y_ref[...] + dx_ref[...] * xs
    z = z_ref[...].astype(F32)
    for g in range(A_N_GROUPS):
        gsl = slice(g * A_GW, (g + 1) * A_GW)
        yn_ref[:, gsl] = _gate_norm(y[:, gsl], z[:, gsl], ng_ref[:, gsl]).astype(BF16)


_SSD_STEP_INPUTS = 13


def _ssd_step_aliased_body(*refs):
    _ssd_step_body(*refs[:_SSD_STEP_INPUTS], *refs[_SSD_STEP_INPUTS + 1:])


def _ssd_step_call(proj, dt_raw, conv_prev, ssm_all, slot, ssm_out, conv_w, conv_b, dt_bias, a_log, d_x, norm_g,
                   name):
    nb = proj.shape[0]
    bt = 8
    hp = A_N_HEADS * A_HEAD_DIM
    full = lambda shape: pl.BlockSpec(shape, lambda i: (0,) * len(shape))
    state_spec = pl.BlockSpec((None, bt, hp, A_D_STATE), lambda i: (slot, i, 0, 0))
    in_specs = [pl.BlockSpec((bt, A_D_INNER), lambda i: (i, 0)),
                pl.BlockSpec((bt, A_D_INNER), lambda i: (i, 1)),
                pl.BlockSpec((bt, 2 * A_GN), lambda i: (i, 2)),
                pl.BlockSpec((bt, LANE), lambda i: (i, 0)),
                pl.BlockSpec((bt, A_CONV - 1, A_CONV_CH), lambda i: (i, 0, 0)),
                state_spec,
                full((A_CONV, A_CONV_CH)), full((1, A_CONV_CH)), full((1, LANE)), full((1, LANE)),
                full((1, A_D_INNER)), full((1, A_D_INNER)), full((LANE, A_D_INNER))]
    args = [proj, proj, proj, dt_raw, conv_prev, ssm_all, conv_w, conv_b, dt_bias, a_log, d_x, norm_g, _head_expand()]
    assert len(args) == _SSD_STEP_INPUTS
    aliases = {}
    if ssm_out is not None:
        in_specs.append(pl.BlockSpec(memory_space=pl.ANY))
        args.append(ssm_out)
        aliases = {_SSD_STEP_INPUTS: 2}
    return pl.pallas_call(
        _ssd_step_body if ssm_out is None else _ssd_step_aliased_body,
        grid=(nb // bt,),
        in_specs=in_specs,
        out_specs=[pl.BlockSpec((bt, A_D_INNER), lambda i: (i, 0)),
                   pl.BlockSpec((bt, A_CONV - 1, A_CONV_CH), lambda i: (i, 0, 0)),
                   state_spec],
        out_shape=[jax.ShapeDtypeStruct((nb, A_D_INNER), BF16),
                   jax.ShapeDtypeStruct((nb, A_CONV - 1, A_CONV_CH), F32),
                   jax.ShapeDtypeStruct(ssm_all.shape, F32)],
        scratch_shapes=[pltpu.VMEM((bt, LANE), F32), pltpu.VMEM((bt, A_D_INNER), F32)],
        input_output_aliases=aliases,
        compiler_params=_cparams(("arbitrary",)),
        name=name,
    )(*args)


def _gmlp_in_body(x_ref, g_ref, sc_ref, sh_ref, w_ref, b_ref, lg_ref, lb_ref, o_ref, hn_ref):
    j = pl.program_id(1)

    @pl.when(j == 0)
    def _():
        hn_ref[...] = _prenorm(x_ref[...], g_ref[...], sc_ref[...], sh_ref[...]).astype(BF16)

    uv = _gelu_tanh(_dot(hn_ref[...], w_ref[...]) + b_ref[...])

    @pl.when(j == 0)
    def _():
        o_ref[...] = uv.astype(o_ref.dtype)

    @pl.when(j == 1)
    def _():
        vc = uv - jnp.mean(uv, axis=-1, keepdims=True)
        var = jnp.mean(vc * vc, axis=-1, keepdims=True)
        o_ref[...] = (vc * lax.rsqrt(var + EPS) * lg_ref[...] + lb_ref[...]).astype(o_ref.dtype)


def _gmlp_in_call(x, g, mod, w, b, ln_g, ln_b, tm, out_dtype, name):
    t, d = x.shape
    return pl.pallas_call(
        _gmlp_in_body,
        grid=(t // tm, 2),
        in_specs=[pl.BlockSpec((tm, d), lambda i, j: (i, 0)),
                  pl.BlockSpec((1, d), lambda i, j: (0, 0)),
                  mod.spec(1, tm, 2), mod.spec(0, tm, 2),
                  pl.BlockSpec((d, B_D), lambda i, j: (0, j)),
                  pl.BlockSpec((1, B_D), lambda i, j: (0, j)),
                  pl.BlockSpec((1, B_D), lambda i, j: (0, 0)),
                  pl.BlockSpec((1, B_D), lambda i, j: (0, 0))],
        out_specs=pl.BlockSpec((tm, B_D), lambda i, j: (i, j)),
        out_shape=jax.ShapeDtypeStruct((t, 2 * B_D), out_dtype),
        scratch_shapes=[pltpu.VMEM((tm, d), BF16)],
        compiler_params=_cparams(("arbitrary", "arbitrary")),
        name=name,
    )(x, g.reshape(1, d), mod.arr, mod.arr, w, b.reshape(1, -1), ln_g.reshape(1, -1), ln_b.reshape(1, -1))


def _gmlp_out_body(u_ref, v_ref, ws_ref, bs_ref, w_ref, x_ref, gate_ref, o_ref, wbf_ref, wsbf_ref, m_ref):
    q = B_CHUNK

    @pl.when(pl.program_id(0) == 0)
    def _():
        wbf_ref[...] = w_ref[...].astype(BF16)
        causal = lax.broadcasted_iota(I32, (q, q), 0) >= lax.broadcasted_iota(I32, (q, q), 1)
        for g in range(B_N_GROUPS):
            wsbf_ref[g] = jnp.where(causal, ws_ref[g], 0.0).astype(BF16)

    for ci in range(u_ref.shape[0] // q):
        rsl = slice(ci * q, (ci + 1) * q)
        for g in range(B_N_GROUPS):
            gsl = slice(g * B_GROUP_DIM, (g + 1) * B_GROUP_DIM)
            mixed = _dot(wsbf_ref[g], v_ref[rsl, gsl].astype(BF16)) + bs_ref[:, g:g + 1]
            m_ref[rsl, gsl] = (u_ref[rsl, gsl].astype(F32) * mixed).astype(BF16)
    o_ref[...] = x_ref[...] + gate_ref[...] * _dot(m_ref[...], wbf_ref[...])


def _gmlp_out_call(uv, w_s, b_s, w_out, x, mod, tm, name):
    t, d = x.shape
    q = B_CHUNK
    return pl.pallas_call(
        _gmlp_out_body,
        grid=(t // tm,),
        in_specs=[pl.BlockSpec((tm, B_D), lambda i: (i, 0)),
                  pl.BlockSpec((tm, B_D), lambda i: (i, 1)),
                  pl.BlockSpec((B_N_GROUPS, q, q), lambda i: (0, 0, 0)),
                  pl.BlockSpec((q, B_N_GROUPS), lambda i: (0, 0)),
                  pl.BlockSpec((B_D, d), lambda i: (0, 0)),
                  pl.BlockSpec((tm, d), lambda i: (i, 0)),
                  mod.spec(2, tm, 1)],
        out_specs=pl.BlockSpec((tm, d), lambda i: (i, 0)),
        out_shape=jax.ShapeDtypeStruct((t, d), F32),
        scratch_shapes=[pltpu.VMEM((B_D, d), BF16), pltpu.VMEM((B_N_GROUPS, q, q), BF16),
                        pltpu.VMEM((tm, B_D), BF16)],
        compiler_params=_cparams(("arbitrary",)),
        name=name,
    )(uv, uv, w_s, b_s.T, w_out, x, mod.arr)


def _gmlp_out_step_body(u_ref, v_ref, wd_ref, bd_ref, w_ref, x_ref, gate_ref, o_ref):
    mixed = v_ref[...] * wd_ref[...] + bd_ref[...]
    m = (u_ref[...] * mixed).astype(BF16)
    o_ref[...] = x_ref[...] + gate_ref[...] * _dot(m, w_ref[...].astype(BF16))


def _gmlp_out_step_call(uv, w_s, b_s, w_out, x, mod, name):
    t, d = x.shape
    wd = jnp.repeat(w_s[:, 0, 0], B_GROUP_DIM).reshape(1, B_D)
    bd = jnp.repeat(b_s[:, 0], B_GROUP_DIM).reshape(1, B_D)
    return pl.pallas_call(
        _gmlp_out_step_body,
        grid=(1,),
        in_specs=[pl.BlockSpec((t, B_D), lambda i: (0, 0)),
                  pl.BlockSpec((t, B_D), lambda i: (0, 1)),
                  pl.BlockSpec((1, B_D), lambda i: (0, 0)),
                  pl.BlockSpec((1, B_D), lambda i: (0, 0)),
                  pl.BlockSpec((B_D, d), lambda i: (0, 0)),
                  pl.BlockSpec((t, d), lambda i: (0, 0)),
                  mod.spec(2, t, 1)],
        out_specs=pl.BlockSpec((t, d), lambda i: (0, 0)),
        out_shape=jax.ShapeDtypeStruct((t, d), F32),
        compiler_params=_cparams(("arbitrary",)),
        name=name,
    )(uv, uv, wd, bd, w_out, x, mod.arr)


def _pool_matmul(pooled, wg_ref):
    outs = []
    for gi in range(len(C_WINDOWS)):
        gsl = slice(gi * C_GROUP_DIM, (gi + 1) * C_GROUP_DIM)
        outs.append(_dot(pooled[:, gsl].astype(BF16), wg_ref[gi].astype(BF16)))
    return jnp.concatenate(outs, axis=-1)


def _pool_body(x_ref, g_ref, sc_ref, sh_ref, gate_ref, prev_ref, wg_ref, scale_ref, o_ref, pnew_ref, hp_ref, s_ref,
               *, tiles_per_seq, start):
    i = pl.program_id(0)
    tm = x_ref.shape[0]
    top = 16
    rows = top + tm
    ti = i % tiles_per_seq

    @pl.when(ti == 0)
    def _():
        hp_ref[0:top - C_STATE, :] = jnp.zeros((top - C_STATE, hp_ref.shape[1]), F32)
        hp_ref[top - C_STATE:top, :] = prev_ref[...]

    hn = _prenorm(x_ref[...], g_ref[...], sc_ref[...], sh_ref[...])
    hp_ref[top:top + tm, :] = hn
    pos = start + ti * tm + lax.broadcasted_iota(I32, (tm, 1), 0)
    outs = []
    for gi, w in enumerate(C_WINDOWS):
        gsl = slice(gi * C_GROUP_DIM, (gi + 1) * C_GROUP_DIM)
        read = lambda lo, hi: hp_ref[lo:hi, gsl]
        valid, span = top - C_STATE, 1
        while 2 * span < w:
            lo = valid + span
            s_ref[span.bit_length() % 2, lo:rows, :] = read(lo, rows) + read(lo - span, rows - span)
            read = functools.partial(lambda lo, hi, k: s_ref[k, lo:hi, :], k=span.bit_length() % 2)
            valid, span = lo, 2 * span
        acc = read(top, rows) + read(top - span, rows - span)
        cnt = jnp.minimum(pos + 1, w).astype(F32)
        outs.append(acc / cnt - hn[:, gsl])
    y = _pool_matmul(jnp.concatenate(outs, axis=-1), wg_ref) * scale_ref[...]
    o_ref[...] = x_ref[...] + gate_ref[...] * y
    hist = hp_ref[top + tm - C_STATE:top + tm, :]
    hp_ref[top - C_STATE:top, :] = hist

    @pl.when(ti == tiles_per_seq - 1)
    def _():
        pnew_ref[...] = hist


def _pool_call(x, g, mod, prev, w_g, scale, nb, tm, start, name):
    t, d = x.shape
    tiles = t // nb // tm
    ng = len(C_WINDOWS)
    return pl.pallas_call(
        functools.partial(_pool_body, tiles_per_seq=tiles, start=start),
        grid=(t // tm,),
        in_specs=[pl.BlockSpec((tm, d), lambda i: (i, 0)),
                  pl.BlockSpec((1, d), lambda i: (0, 0)),
                  mod.spec(1, tm, 1), mod.spec(0, tm, 1), mod.spec(2, tm, 1),
                  pl.BlockSpec((None, C_STATE, d), lambda i: (i // tiles, 0, 0)),
                  pl.BlockSpec((ng, C_GROUP_DIM, C_GROUP_DIM), lambda i: (0, 0, 0)),
                  pl.BlockSpec((1, d), lambda i: (0, 0))],
        out_specs=[pl.BlockSpec((tm, d), lambda i: (i, 0)),
                   pl.BlockSpec((None, C_STATE, d), lambda i: (i // tiles, 0, 0))],
        out_shape=[jax.ShapeDtypeStruct((t, d), F32), jax.ShapeDtypeStruct((nb, C_STATE, d), F32)],
        scratch_shapes=[pltpu.VMEM((16 + tm, d), F32), pltpu.VMEM((2, 16 + tm, C_GROUP_DIM), F32)],
        compiler_params=_cparams(("arbitrary",)),
        name=name,
    )(x, g.reshape(1, d), mod.arr, mod.arr, mod.arr, prev, w_g, scale.reshape(1, d))


def _pool_step_body(x_ref, g_ref, sc_ref, sh_ref, gate_ref, prev_ref, wg_ref, scale_ref, o_ref, pnew_ref, *, start):
    hn = _prenorm(x_ref[...], g_ref[...], sc_ref[...], sh_ref[...])
    outs = []
    for gi, w in enumerate(C_WINDOWS):
        gsl = slice(gi * C_GROUP_DIM, (gi + 1) * C_GROUP_DIM)
        acc = hn[:, gsl]
        for k in range(1, w):
            acc = acc + prev_ref[:, C_STATE - k, gsl]
        outs.append(acc / float(min(start + 1, w)) - hn[:, gsl])
    y = _pool_matmul(jnp.concatenate(outs, axis=-1), wg_ref) * scale_ref[...]
    o_ref[...] = x_ref[...] + gate_ref[...] * y
    for k in range(C_STATE - 1):
        pnew_ref[:, k, :] = prev_ref[:, k + 1, :]
    pnew_ref[:, C_STATE - 1, :] = hn


def _pool_step_call(x, g, mod, prev, w_g, scale, start, name):
    t, d = x.shape
    bt = 32
    ng = len(C_WINDOWS)
    return pl.pallas_call(
        functools.partial(_pool_step_body, start=start),
        grid=(t // bt,),
        in_specs=[pl.BlockSpec((bt, d), lambda i: (i, 0)),
                  pl.BlockSpec((1, d), lambda i: (0, 0)),
                  mod.spec(1, bt, 1), mod.spec(0, bt, 1), mod.spec(2, bt, 1),
                  pl.BlockSpec((bt, C_STATE, d), lambda i: (i, 0, 0)),
                  pl.BlockSpec((ng, C_GROUP_DIM, C_GROUP_DIM), lambda i: (0, 0, 0)),
                  pl.BlockSpec((1, d), lambda i: (0, 0))],
        out_specs=[pl.BlockSpec((bt, d), lambda i: (i, 0)),
                   pl.BlockSpec((bt, C_STATE, d), lambda i: (i, 0, 0))],
        out_shape=[jax.ShapeDtypeStruct((t, d), F32), jax.ShapeDtypeStruct((t, C_STATE, d), F32)],
        compiler_params=_cparams(("arbitrary",)),
        name=name,
    )(x, g.reshape(1, d), mod.arr, mod.arr, mod.arr, prev, w_g, scale.reshape(1, d))


def _route_rows(s, b):
    npg = EXPERTS_PER_GROUP
    gscore = []
    for q in range(N_EXPERT_GROUPS):
        v = b[q * npg:(q + 1) * npg]
        best = None
        for i in range(npg):
            for j in range(i + 1, npg):
                best = v[i] + v[j] if best is None else jnp.maximum(best, v[i] + v[j])
        gscore.append(best)
    gsel = jnp.zeros_like(gscore[0], dtype=I32)
    gbest = gscore[0]
    for q in range(1, N_EXPERT_GROUPS):
        better = gscore[q] > gbest
        gsel = jnp.where(better, q, gsel)
        gbest = jnp.where(better, gscore[q], gbest)
    vb, vs = [], []
    for k in range(npg):
        bk, sk = b[k], s[k]
        for q in range(1, N_EXPERT_GROUPS):
            bk = jnp.where(gsel == q, b[q * npg + k], bk)
            sk = jnp.where(gsel == q, s[q * npg + k], sk)
        vb.append(bk)
        vs.append(sk)
    i1 = jnp.zeros_like(gsel)
    m1 = vb[0]
    for k in range(1, npg):
        better = vb[k] > m1
        i1 = jnp.where(better, k, i1)
        m1 = jnp.where(better, vb[k], m1)
    i2 = jnp.full_like(gsel, -1)
    m2 = jnp.zeros_like(m1)
    for k in range(npg):
        better = (i1 != k) & ((i2 < 0) | (vb[k] > m2))
        i2 = jnp.where(better, k, i2)
        m2 = jnp.where(better, vb[k], m2)
    s1 = vs[0]
    s2 = vs[0]
    for k in range(1, npg):
        s1 = jnp.where(i1 == k, vs[k], s1)
        s2 = jnp.where(i2 == k, vs[k], s2)
    w1 = s1 / (s1 + s2)
    w2 = s2 / (s1 + s2)
    lo = jnp.minimum(i1, i2)
    hi = jnp.maximum(i1, i2)
    pair = jnp.zeros_like(gsel)
    for p, (a, c) in enumerate(PAIRS):
        pair = jnp.where((lo == a) & (hi == c), p, pair)
    first_is_lo = i1 < i2
    return (gsel * len(PAIRS) + pair, jnp.where(first_is_lo, w1, w2), jnp.where(first_is_lo, w2, w1))


def _route_body(x_ref, g_ref, sc_ref, sh_ref, rw_ref, rb_ref, cnt_in_ref, *rest, aliased):
    if aliased:
        rest = rest[3:]
    rows_ref, bucket_ref, rank_ref, cnt_ref = rest
    tm = x_ref.shape[0]

    @pl.when(pl.program_id(0) == 0)
    def _():
        cnt_ref[...] = cnt_in_ref[...]

    hn = _prenorm(x_ref[...], g_ref[...], sc_ref[...], sh_ref[...])
    logits_t = _dot3(hn, rw_ref[...]).T
    scores = _sigmoid(logits_t[0:N_EXPERTS, :])
    biased = scores + rb_ref[0:N_EXPERTS, :]
    bucket, w_lo, w_hi = _route_rows([scores[e:e + 1, :] for e in range(N_EXPERTS)],
                                     [biased[e:e + 1, :] for e in range(N_EXPERTS)])
    bucket_ref[...] = bucket
    onehot = (lax.broadcasted_iota(I32, (BUCKET_ROWS, tm), 0) == bucket).astype(F32)
    before = (lax.broadcasted_iota(I32, (tm, tm), 0) < lax.broadcasted_iota(I32, (tm, tm), 1)).astype(BF16)
    earlier = _dot(onehot.astype(BF16), before) + cnt_ref[:, 0:1]
    rank_ref[...] = jnp.sum(onehot * earlier, axis=0, keepdims=True).astype(I32)
    cnt_ref[...] = cnt_ref[...] + jnp.sum(onehot, axis=1, keepdims=True)

    sub = lax.broadcasted_iota(I32, (LANE, tm), 0)
    gate_t = jnp.where(sub == 0, w_lo, jnp.where(sub == 1, w_hi, 0.0))
    rows_ref[pl.ds(GATE_ROW, tm, stride=ROW_PITCH), :] = gate_t.T
    for j in range(ROW_PLANES):
        rows_ref[pl.ds(j, tm, stride=ROW_PITCH), :] = hn[:, j * LANE:(j + 1) * LANE]


def _route_call(x, g, mod, router_w, router_b, counts, tm, total_rows, row_offset, prior, name):
    t, d = x.shape
    blk0 = row_offset // tm
    rw = jnp.pad(router_w, ((0, 0), (0, LANE - N_EXPERTS)))
    rb = jnp.pad(router_b.reshape(-1, 1), ((0, LANE - N_EXPERTS), (0, 0)))
    in_specs = [pl.BlockSpec((tm, d), lambda i: (i, 0)),
                pl.BlockSpec((1, d), lambda i: (0, 0)),
                mod.spec(4, tm, 1), mod.spec(3, tm, 1),
                pl.BlockSpec((d, LANE), lambda i: (0, 0)),
                pl.BlockSpec((LANE, 1), lambda i: (0, 0)),
                pl.BlockSpec((BUCKET_ROWS, LANE), lambda i: (0, 0))]
    args = [x, g.reshape(1, d), mod.arr, mod.arr, rw, rb, counts]
    aliases = {}
    if prior is not None:
        in_specs += [pl.BlockSpec(memory_space=pl.ANY)] * len(prior)
        aliases = {len(args) + k: k for k in range(len(prior))}
        args += list(prior)
    return pl.pallas_call(
        functools.partial(_route_body, aliased=prior is not None),
        grid=(t // tm,),
        in_specs=in_specs,
        out_specs=[pl.BlockSpec((tm * ROW_PITCH, LANE), lambda i: (blk0 + i, 0)),
                   pl.BlockSpec((1, tm), lambda i: (0, blk0 + i)),
                   pl.BlockSpec((1, tm), lambda i: (0, blk0 + i)),
                   pl.BlockSpec((BUCKET_ROWS, LANE), lambda i: (0, 0))],
        out_shape=[jax.ShapeDtypeStruct((total_rows * ROW_PITCH, LANE), F32),
                   jax.ShapeDtypeStruct((1, total_rows), I32),
                   jax.ShapeDtypeStruct((1, total_rows), I32),
                   jax.ShapeDtypeStruct((BUCKET_ROWS, LANE), F32)],
        input_output_aliases=aliases,
        compiler_params=_cparams(("arbitrary",)),
        name=name,
    )(*args)


def _invert_body(dest_ref, init_ref, gather_ref):
    pltpu.sync_copy(init_ref, gather_ref)

    def put(t, carry):
        gather_ref[dest_ref[t]] = t
        return carry

    lax.fori_loop(0, dest_ref.shape[0], put, 0, unroll=8)


def _invert_call(dest, n_slots, name):
    smem = pl.BlockSpec(memory_space=pltpu.SMEM)
    return pl.pallas_call(
        _invert_body,
        in_specs=[smem, pl.BlockSpec(memory_space=pl.ANY)],
        out_specs=smem,
        out_shape=jax.ShapeDtypeStruct((n_slots,), I32),
        name=name,
    )(dest, jnp.zeros((n_slots,), I32))


def _unrolled(lo, hi, body, carry):
    for b in range(lo, hi):
        carry = body(b, carry)
    return carry


def _experts_body(grp_ref, lo_ref, hi_ref, fresh_ref, valid_ref, out_ref, gather_ref, rows_ref, wg_in_ref,
                  wu_in_ref, wd_in_ref, y_ref, wg_ref, wu_ref, wd_ref, xb_ref, gsem):
    i = pl.program_id(0)
    n = pl.num_programs(0)
    tile = EXPERT_TILE
    copied = GATE_ROW + 1
    group = 8
    del grp_ref, out_ref

    def start_gather(step, slot, loop):
        def body(b, carry):
            for k in range(group):
                r = b * group + k
                tok = gather_ref[step * tile + r]
                pltpu.make_async_copy(rows_ref.at[pl.ds(tok * ROW_PITCH, copied)],
                                      xb_ref.at[slot, pl.ds(r * ROW_PITCH, copied)],
                                      gsem.at[slot]).start(priority=k % 2)
            return carry

        loop(0, tile // group, body, 0)

    def wait_gather(slot):
        rows = tile * copied
        pltpu.make_async_copy(rows_ref.at[pl.ds(0, rows)], xb_ref.at[slot, pl.ds(0, rows)], gsem.at[slot]).wait()

    @pl.when(fresh_ref[i] == 1)
    def _():
        for e in range(EXPERTS_PER_GROUP):
            wg_ref[e] = wg_in_ref[e].astype(BF16)
            wu_ref[e] = wu_in_ref[e].astype(BF16)
            wd_ref[e] = wd_in_ref[e].astype(BF16)

    nxt = jnp.minimum(i + 1, n - 1)
    more = jnp.logical_and(i + 1 < n, valid_ref[nxt] == 1)

    @pl.when(valid_ref[i] == 1)
    def _():
        slot = i % GATHER_SLOTS

        @pl.when(i == 0)
        def _():
            for ahead in range(GATHER_SLOTS - 1):
                start_gather(ahead, ahead, lax.fori_loop)

        wait_gather(slot)
        ahead = i + GATHER_SLOTS - 1
        start_gather(jnp.minimum(ahead, n - 1), ahead % GATHER_SLOTS, _unrolled)
        x = jnp.concatenate([xb_ref[slot, pl.ds(j, tile, stride=ROW_PITCH), :].astype(BF16)
                             for j in range(ROW_PLANES)], axis=-1)

        def expert(e):
            act = (_silu(_dot(x, wg_ref[e])) * _dot(x, wu_ref[e])).astype(BF16)
            return _dot(act, wd_ref[e])

        w = xb_ref[slot, pl.ds(GATE_ROW, tile, stride=ROW_PITCH), :]
        y = w[:, 0:1] * expert(lo_ref[i]) + w[:, 1:2] * expert(hi_ref[i])
        for j in range(ROW_PLANES):
            y_ref[pl.ds(j, tile, stride=ROW_PITCH), :] = y[:, j * LANE:(j + 1) * LANE]

        @pl.when(jnp.logical_not(more))
        def _():
            for k in range(1, GATHER_SLOTS):
                wait_gather((i + k) % GATHER_SLOTS)


def _experts_call(grp, lo, hi, fresh, valid, out_tile, gather_idx, rows, w_gate, w_up, w_down, layer, name):
    n_tiles = grp.shape[0]
    tile = EXPERT_TILE
    d, f = D_MODEL, D_EXPERT
    npg = EXPERTS_PER_GROUP
    group_block = lambda i, grp, *_: (layer, grp[i], 0, 0)
    once = pl.Buffered(1)
    return pl.pallas_call(
        _experts_body,
        grid_spec=pltpu.PrefetchScalarGridSpec(
            num_scalar_prefetch=7,
            grid=(n_tiles,),
            in_specs=[pl.BlockSpec(memory_space=pl.ANY),
                      pl.BlockSpec((None, npg, d, f), group_block, pipeline_mode=once),
                      pl.BlockSpec((None, npg, d, f), group_block, pipeline_mode=once),
                      pl.BlockSpec((None, npg, f, d), group_block, pipeline_mode=once)],
            out_specs=pl.BlockSpec((tile * ROW_PITCH, LANE), lambda i, g, l, h, fr, va, out, *_: (out[i], 0)),
            scratch_shapes=[pltpu.VMEM((npg, d, f), BF16), pltpu.VMEM((npg, d, f), BF16),
                            pltpu.VMEM((npg, f, d), BF16),
                            pltpu.VMEM((GATHER_SLOTS, tile * ROW_PITCH, LANE), F32),
                            pltpu.SemaphoreType.DMA((GATHER_SLOTS,))]),
        out_shape=jax.ShapeDtypeStruct((n_tiles * tile * ROW_PITCH, LANE), F32),
        compiler_params=_cparams(("arbitrary",)),
        name=name,
    )(grp, lo, hi, fresh, valid, out_tile, gather_idx, rows, w_gate, w_up, w_down)


def _moe_res_body(dest_ref, y_ref, x_ref, gate_ref, fg_ref, o_ref, *rest, row_offset, final):
    if final:
        on_ref, buf_ref, sem = rest
    else:
        buf_ref, sem = rest
    i = pl.program_id(0)
    tm = x_ref.shape[0]
    group = 8

    def start_gather(step, slot, loop):
        base = row_offset + step * tm

        def body(b, carry):
            for k in range(group):
                r = b * group + k
                pltpu.make_async_copy(y_ref.at[pl.ds(dest_ref[base + r] * ROW_PITCH, ROW_PLANES)],
                                      buf_ref.at[slot, pl.ds(r * ROW_PITCH, ROW_PLANES)],
                                      sem.at[slot]).start(priority=k % 2)
            return carry

        loop(0, tm // group, body, 0)

    def wait_gather(slot):
        rows = tm * ROW_PLANES
        pltpu.make_async_copy(y_ref.at[pl.ds(0, rows)], buf_ref.at[slot, pl.ds(0, rows)], sem.at[slot]).wait()

    @pl.when(i == 0)
    def _():
        start_gather(0, 0, lax.fori_loop)

    slot = i % 2
    wait_gather(slot)

    @pl.when(i + 1 < pl.num_programs(0))
    def _():
        start_gather(i + 1, 1 - slot, lax.fori_loop)
    y = jnp.concatenate([buf_ref[slot, pl.ds(j, tm, stride=ROW_PITCH), :] for j in range(ROW_PLANES)], axis=-1)
    xn = x_ref[...] + gate_ref[...] * y
    o_ref[...] = xn
    if final:
        ms = jnp.mean(xn * xn, axis=-1, keepdims=True)
        on_ref[...] = xn * lax.rsqrt(ms + EPS) * fg_ref[...]


def _moe_res_call(dest, y_sorted, x, mod, final_g, tm, row_offset, final, name):
    t, d = x.shape
    n_out = 2 if final else 1
    gate_spec = mod.spec(5, tm, 1)
    gate_map = gate_spec.index_map
    return pl.pallas_call(
        functools.partial(_moe_res_body, row_offset=row_offset, final=final),
        grid_spec=pltpu.PrefetchScalarGridSpec(
            num_scalar_prefetch=1,
            grid=(t // tm,),
            in_specs=[pl.BlockSpec(memory_space=pl.ANY),
                      pl.BlockSpec((tm, d), lambda i, dest: (i, 0)),
                      pl.BlockSpec(gate_spec.block_shape, lambda i, dest: gate_map(i)),
                      pl.BlockSpec((1, d), lambda i, dest: (0, 0))],
            out_specs=[pl.BlockSpec((tm, d), lambda i, dest: (i, 0))] * n_out,
            scratch_shapes=[pltpu.VMEM((2, tm * ROW_PITCH, LANE), F32), pltpu.SemaphoreType.DMA((2,))]),
        out_shape=[jax.ShapeDtypeStruct((t, d), F32)] * n_out,
        compiler_params=_cparams(("arbitrary",)),
        name=name,
    )(dest, y_sorted, x, mod.arr, final_g.reshape(1, d))


def _tile_tables(counts, n_tiles):
    tile = EXPERT_TILE
    tiles_per_bucket = (counts + tile - 1) // tile
    ends = jnp.cumsum(tiles_per_bucket)
    starts = ends - tiles_per_bucket
    used = ends[-1]
    ti = jnp.arange(n_tiles, dtype=I32)
    valid = (ti < used).astype(I32)
    tile_bucket = jnp.sum((jnp.minimum(ti, used - 1)[:, None] >= ends[None, :]).astype(I32), axis=1)
    pair_lo = jnp.asarray([p[0] for p in PAIRS], I32)
    pair_hi = jnp.asarray([p[1] for p in PAIRS], I32)
    grp = tile_bucket // len(PAIRS)
    lo = pair_lo[tile_bucket % len(PAIRS)]
    hi = pair_hi[tile_bucket % len(PAIRS)]
    fresh = jnp.concatenate([jnp.ones((1,), I32), (grp[1:] != grp[:-1]).astype(I32)])
    return grp, lo, hi, fresh, valid, jnp.minimum(ti, used - 1).astype(I32), (starts * tile).astype(I32)


def _moe_layer(xp, xs, norm_g2, mod_p, mod_s, router_w, router_b, w_gate, w_up, w_down, layer, final_g, final):
    tp, ts = xp.shape[0], xs.shape[0]
    total = tp + ts
    n_tiles = -(-total // EXPERT_TILE) + N_BUCKETS
    zero_counts = jnp.zeros((BUCKET_ROWS, LANE), F32)
    rows, bucket, rank, counts = _route_call(xp, norm_g2, mod_p, router_w, router_b, zero_counts, TM_ROUTE, total, 0,
                                             None, "route_prompt")
    rows, bucket, rank, counts = _route_call(xs, norm_g2, mod_s, router_w, router_b, counts, ts, total, tp,
                                             (rows, bucket, rank), "route_sample")
    grp, lo, hi, fresh, valid, out_tile, bucket_start = _tile_tables(counts[:N_BUCKETS, 0].astype(I32), n_tiles)
    dest = bucket_start[bucket[0]] + rank[0]
    gather_idx = _invert_call(dest, n_tiles * EXPERT_TILE, "invert")
    y_sorted = _experts_call(grp, lo, hi, fresh, valid, out_tile, gather_idx, rows, w_gate, w_up, w_down, layer,
                             "experts")
    outp = _moe_res_call(dest, y_sorted, xp, mod_p, final_g, TM_MOE_RES, 0, final, "moe_res_prompt")
    outs = _moe_res_call(dest, y_sorted, xs, mod_s, final_g, ts, tp, final, "moe_res_sample")
    return outp, outs


def _mamba_layer(x, g, mod, conv_prev, ssm, w_zx, w_dt, conv_w, conv_b, dt_bias, a_log, d_skip, norm_g, w_out,
                 tm, tag):
    pad_h = lambda v: jnp.pad(v.reshape(1, -1), ((0, 0), (0, LANE - A_N_HEADS)))
    tm_in = min(2 * tm, x.shape[0])
    proj = _norm_mm_call(x, g, mod, 1, 0, w_zx, tm_in, A_IN_TN, BF16, "a_in_" + tag)
    dt_raw = _norm_mm3_call(x, g, mod, 1, 0, w_dt, tm, "a_dt_" + tag)
    d_x = jnp.repeat(d_skip, A_HEAD_DIM).reshape(1, A_D_INNER)
    weights = (conv_w, conv_b.reshape(1, -1), pad_h(dt_bias), pad_h(a_log), d_x, norm_g.reshape(1, -1))
    if ssm[0] == "step":
        yn, conv_new, ssm_new = _ssd_step_call(proj, dt_raw, conv_prev, ssm[1], ssm[2], ssm[3], *weights,
                                               "ssd_step_" + tag)
    else:
        yn, conv_new, ssm_new = _ssd_call(proj, dt_raw, conv_prev, ssm[2], *weights, ssm[1], "ssd_" + tag)
    x = _out_res_call(yn, w_out, x, mod, 2, tm, "a_out_" + tag)
    return x, conv_new, ssm_new


def kernel(x_prompt, x_sample, c_prompt, c_sample, state_a_conv, state_a_ssm, state_c_pool, w_mod, b_mod, norm_g, final_g, a_w_in, a_conv_w, a_conv_b, a_dt_bias, a_log, a_d, a_norm_g, a_w_out, b_w_in, b_b_in, b_ln_g, b_ln_b, b_w_s, b_b_s, b_w_out, c_w_g, c_scale, router_w, router_b, e_w_gate, e_w_up, e_w_down):
    bp, seq, d = x_prompt.shape
    bs = x_sample.shape[0]
    n_a, n_c = state_a_conv.shape[0], state_c_pool.shape[0]
    mod_all = _mod_call(jnp.concatenate([c_prompt, c_sample], axis=0), w_mod, b_mod)
    mod_p_arr = mod_all[:, :bp].reshape(DEPTH, bp, 6, 1, d)
    mod_s_arr = mod_all[:, bp:]
    xp = x_prompt.reshape(bp * seq, d)
    xs = x_sample.reshape(bs, d)
    conv_p, ssm_p, pool_p, conv_s, pool_s, v_s = [], [], [], [], [], []
    hp = A_N_HEADS * A_HEAD_DIM
    ssm_s_in = state_a_ssm.reshape(n_a, bs, hp, A_D_STATE)
    ssm_s_out = None
    yp = ys = None
    for i in range(DEPTH):
        kind, s = LAYER_KIND[i], LAYER_SLOT[i]
        mod_p = Mod(mod_p_arr, i, False, seq)
        mod_s = Mod(mod_s_arr, i, True)
        g1 = norm_g[i, 0]
        if kind == 0:
            w_zx = a_w_in[s, :, :A_ZX].astype(BF16)
            w_dt = jnp.pad(a_w_in[s, :, A_ZX:], ((0, 0), (0, LANE - A_N_HEADS)))
            weights = (w_zx, w_dt, a_conv_w[s], a_conv_b[s], a_dt_bias[s], a_log[s], a_d[s], a_norm_g[s], a_w_out[s])
            conv0 = jnp.zeros((bp, A_CONV - 1, A_CONV_CH), F32)
            ssm0 = jnp.zeros((bp, hp, A_D_STATE), F32)
            xp, cv, ss = _mamba_layer(xp, g1, mod_p, conv0, ("prompt", bp, ssm0), *weights, TM_MAMBA, "p%d" % i)
            conv_p.append(cv)
            ssm_p.append(ss.reshape(bp, A_N_HEADS, A_HEAD_DIM, A_D_STATE))
            xs, cv, ssm_s_out = _mamba_layer(xs, g1, mod_s, state_a_conv[s], ("step", ssm_s_in, s, ssm_s_out),
                                             *weights, bs, "s%d" % i)
            conv_s.append(cv)
        elif kind == 1:
            w_uv = b_w_in[s].astype(BF16)
            uv = _gmlp_in_call(xp, g1, mod_p, w_uv, b_b_in[s], b_ln_g[s], b_ln_b[s], TM_GMLP_IN, BF16,
                               "b_in_p%d" % i)
            xp = _gmlp_out_call(uv, b_w_s[s], b_b_s[s], b_w_out[s], xp, mod_p, TM_GMLP_OUT, "b_out_p%d" % i)
            uv = _gmlp_in_call(xs, g1, mod_s, w_uv, b_b_in[s], b_ln_g[s], b_ln_b[s], bs, F32, "b_in_s%d" % i)
            xs = _gmlp_out_step_call(uv, b_w_s[s], b_b_s[s], b_w_out[s], xs, mod_s, "b_out_s%d" % i)
            v_s.append(uv[:, B_D:].reshape(bs, 1, B_D))
        else:
            pool0 = jnp.zeros((bp, C_STATE, d), F32)
            xp, pr = _pool_call(xp, g1, mod_p, pool0, c_w_g[s], c_scale[s], bp, TM_POOL, 0, "pool_p%d" % i)
            pool_p.append(pr)
            xs, pr = _pool_step_call(xs, g1, mod_s, state_c_pool[s], c_w_g[s], c_scale[s], PAST_LEN, "pool_s%d" % i)
            pool_s.append(pr)
        final = i == DEPTH - 1
        outp, outs = _moe_layer(xp, xs, norm_g[i, 1], mod_p, mod_s, router_w, router_b, e_w_gate, e_w_up, e_w_down, i,
                                final_g, final)
        xp, xs = outp[0], outs[0]
        if final:
            yp, ys = outp[1], outs[1]
    return (yp.reshape(bp, seq, d), ys.reshape(bs, 1, d), jnp.stack(conv_p), jnp.stack(ssm_p), jnp.stack(pool_p),
            jnp.stack(conv_s), ssm_s_out.reshape(state_a_ssm.shape), jnp.stack(pool_s), jnp.stack(v_s))
```
